```python
import jax, jax.numpy as jnp
from jax import lax
import numpy as np

D_MODEL = 1024
BATCH = 8
SEQ = 4096
DEPTH = 2

CHUNK = 64
D_RNN = 1280
RNN_HEADS = 20
RNN_HEAD_DIM = D_RNN // RNN_HEADS
CONV_WIDTH = 4
LRU_C = 8.0
D_SGU = 1024
SGU_GROUPS = 8
SGU_GROUP_DIM = D_SGU // SGU_GROUPS
SGU_BLOCK = 128
N_BRANCH = 2
D_FF = 4 * D_MODEL
D_IN = 2 * D_RNN + 2 * D_SGU + N_BRANCH * D_MODEL
EPS = 1e-6

kernel_name = "hybrid_rglru_sgu_gated_encoder"


def rmsnorm(x, g):
    xf = x.astype(jnp.float32)
    y = xf * lax.rsqrt(jnp.mean(xf * xf, axis=-1, keepdims=True) + EPS)
    return (y * g.astype(jnp.float32)).astype(x.dtype)


def layernorm(x, g, b):
    xf = x.astype(jnp.float32)
    mu = jnp.mean(xf, axis=-1, keepdims=True)
    xc = xf - mu
    y = xc * lax.rsqrt(jnp.mean(xc * xc, axis=-1, keepdims=True) + EPS)
    return (y * g.astype(jnp.float32) + b.astype(jnp.float32)).astype(x.dtype)


def causal_depthwise_conv(x, w, b):
    s = x.shape[1]
    xp = jnp.pad(x, ((0, 0), (CONV_WIDTH - 1, 0), (0, 0)))
    y = b
    for k in range(CONV_WIDTH):
        y = y + xp[:, k:k + s, :] * w[k]
    return y


def rg_lru(x, w_a, b_a, w_x, b_x, lam):
    bsz, s, _ = x.shape
    xh = x.reshape(bsz, s, RNN_HEADS, RNN_HEAD_DIM)
    r = jax.nn.sigmoid(jnp.einsum('bshi,hij->bshj', xh, w_a) + b_a).reshape(bsz, s, D_RNN)
    i = jax.nn.sigmoid(jnp.einsum('bshi,hij->bshj', xh, w_x) + b_x).reshape(bsz, s, D_RNN)
    log_a = (-LRU_C * r.astype(jnp.float32)) * jax.nn.softplus(-lam.astype(jnp.float32))
    a = jnp.exp(log_a)
    norm = jnp.sqrt(-jnp.expm1(2.0 * log_a))
    u = norm * (i * x).astype(jnp.float32)

    def combine(left, right):
        a_l, b_l = left
        a_r, b_r = right
        return a_l * a_r, a_r * b_l + b_r

    _, h = lax.associative_scan(combine, (a, u), axis=1)
    return h.astype(x.dtype)


def spatial_gating(u, v, w_s, b_s, ln_g, ln_b):
    bsz, s, _ = u.shape
    nblk = s // SGU_BLOCK
    v = layernorm(v, ln_g, ln_b)
    vb = v.reshape(bsz, nblk, SGU_BLOCK, SGU_GROUPS, SGU_GROUP_DIM)
    chunk_id = jnp.arange(SGU_BLOCK) // CHUNK
    mask = (chunk_id[:, None] >= chunk_id[None, :]).astype(w_s.dtype)
    mixed = jnp.einsum('gts,bnsgc->bntgc', w_s * mask, vb)
    mixed = mixed + jnp.transpose(b_s)[None, None, :, :, None]
    return u * mixed.reshape(bsz, s, D_SGU)


def hybrid_layer(x, norm_mix_g, w_in, conv_w, conv_b, lru_w_a, lru_b_a, lru_w_x, lru_b_x,
                 lru_lambda, sgu_ln_g, sgu_ln_b, sgu_w_s, sgu_b_s, w_branch_a, w_branch_b,
                 w_out, norm_ffn_g, w_up, w_down):
    h = rmsnorm(x, norm_mix_g)
    proj = jnp.einsum('bsd,de->bse', h, w_in)
    cuts = np.cumsum([D_RNN, D_RNN, D_SGU, D_SGU, D_MODEL])
    x_rnn, g_rnn, u, v, gate_a, gate_b = jnp.split(proj, cuts, axis=-1)

    xr = causal_depthwise_conv(x_rnn, conv_w, conv_b)
    ya = rg_lru(xr, lru_w_a, lru_b_a, lru_w_x, lru_b_x, lru_lambda) * jax.nn.gelu(g_rnn)
    ya = jnp.einsum('bsr,rd->bsd', ya, w_branch_a)

    yb = spatial_gating(jax.nn.gelu(u), jax.nn.gelu(v), sgu_w_s, sgu_b_s, sgu_ln_g, sgu_ln_b)
    yb = jnp.einsum('bsc,cd->bsd', yb, w_branch_b)

    merged = jax.nn.sigmoid(gate_a) * ya + jax.nn.sigmoid(gate_b) * yb
    x = x + jnp.einsum('bsd,de->bse', merged, w_out)

    h2 = rmsnorm(x, norm_ffn_g)
    f = jnp.square(jax.nn.relu(jnp.einsum('bsd,df->bsf', h2, w_up)))
    return x + jnp.einsum('bsf,fd->bsd', f, w_down)


def _fwd_setup_inputs(seed: int = 0) -> dict:
    key = jax.random.key(seed)
    ks = jax.random.split(key, 24)
    f32 = jnp.float32

    def nrm(k, shape, scale):
        return jax.random.normal(k, shape, f32) * scale

    a_c = jax.random.uniform(ks[7], (DEPTH, D_RNN), f32, 0.9, 0.999)
    a0 = a_c ** (1.0 / LRU_C)
    lru_lambda = jnp.log(a0) - jnp.log1p(-a0)

    return {
        "x": nrm(ks[0], (BATCH, SEQ, D_MODEL), 1.0),
        "norm_mix_g": 1.0 + nrm(ks[1], (DEPTH, D_MODEL), 0.02),
        "w_in": nrm(ks[2], (DEPTH, D_MODEL, D_IN), D_MODEL ** -0.5),
        "conv_w": nrm(ks[3], (DEPTH, CONV_WIDTH, D_RNN), CONV_WIDTH ** -0.5),
        "conv_b": nrm(ks[4], (DEPTH, D_RNN), 0.02),
        "lru_w_a": nrm(ks[5], (DEPTH, RNN_HEADS, RNN_HEAD_DIM, RNN_HEAD_DIM), RNN_HEAD_DIM ** -0.5),
        "lru_b_a": nrm(ks[6], (DEPTH, RNN_HEADS, RNN_HEAD_DIM), 0.02),
        "lru_w_x": nrm(ks[8], (DEPTH, RNN_HEADS, RNN_HEAD_DIM, RNN_HEAD_DIM), RNN_HEAD_DIM ** -0.5),
        "lru_b_x": nrm(ks[9], (DEPTH, RNN_HEADS, RNN_HEAD_DIM), 0.02),
        "lru_lambda": lru_lambda,
        "sgu_ln_g": 1.0 + nrm(ks[10], (DEPTH, D_SGU), 0.02),
        "sgu_ln_b": nrm(ks[11], (DEPTH, D_SGU), 0.02),
        "sgu_w_s": nrm(ks[12], (DEPTH, SGU_GROUPS, SGU_BLOCK, SGU_BLOCK), SGU_BLOCK ** -0.5),
        "sgu_b_s": 1.0 + nrm(ks[13], (DEPTH, SGU_GROUPS, SGU_BLOCK), 0.02),
        "w_branch_a": nrm(ks[14], (DEPTH, D_RNN, D_MODEL), D_RNN ** -0.5),
        "w_branch_b": nrm(ks[15], (DEPTH, D_SGU, D_MODEL), D_SGU ** -0.5),
        "w_out": nrm(ks[16], (DEPTH, D_MODEL, D_MODEL), D_MODEL ** -0.5),
        "norm_ffn_g": 1.0 + nrm(ks[17], (DEPTH, D_MODEL), 0.02),
        "w_up": nrm(ks[18], (DEPTH, D_MODEL, D_FF), D_MODEL ** -0.5),
        "w_down": nrm(ks[19], (DEPTH, D_FF, D_MODEL), D_FF ** -0.5),
        "final_norm_g": 1.0 + nrm(ks[20], (D_MODEL,), 0.02),
    }


def _fwd_reference(x, norm_mix_g, w_in, conv_w, conv_b, lru_w_a, lru_b_a, lru_w_x, lru_b_x,
              lru_lambda, sgu_ln_g, sgu_ln_b, sgu_w_s, sgu_b_s, w_branch_a, w_branch_b,
              w_out, norm_ffn_g, w_up, w_down, final_norm_g):
    for l in range(DEPTH):
        x = hybrid_layer(x, norm_mix_g[l], w_in[l], conv_w[l], conv_b[l], lru_w_a[l], lru_b_a[l],
                         lru_w_x[l], lru_b_x[l], lru_lambda[l], sgu_ln_g[l], sgu_ln_b[l],
                         sgu_w_s[l], sgu_b_s[l], w_branch_a[l], w_branch_b[l], w_out[l],
                         norm_ffn_g[l], w_up[l], w_down[l])
    return rmsnorm(x, final_norm_g)


import jax as _jax
import jax.numpy as _jnp

TWIN_FORMAT = 'train_step'
FWD_PARAMS = ['x', 'norm_mix_g', 'w_in', 'conv_w', 'conv_b', 'lru_w_a', 'lru_b_a', 'lru_w_x', 'lru_b_x', 'lru_lambda', 'sgu_ln_g', 'sgu_ln_b', 'sgu_w_s', 'sgu_b_s', 'w_branch_a', 'w_branch_b', 'w_out', 'norm_ffn_g', 'w_up', 'w_down', 'final_norm_g']
TWIN_WEIGHTS = ['norm_mix_g', 'w_in', 'conv_w', 'conv_b', 'lru_w_a', 'lru_b_a', 'lru_w_x', 'lru_b_x', 'lru_lambda', 'sgu_ln_g', 'sgu_ln_b', 'sgu_w_s', 'sgu_b_s', 'w_branch_a', 'w_branch_b', 'w_out', 'norm_ffn_g', 'w_up', 'w_down', 'final_norm_g']
TWIN_DIFF_INPUT = 'x'
TWIN_INPUTS = ['x', 'norm_mix_g', 'w_in', 'conv_w', 'conv_b', 'lru_w_a', 'lru_b_a', 'lru_w_x', 'lru_b_x', 'lru_lambda', 'sgu_ln_g', 'sgu_ln_b', 'sgu_w_s', 'sgu_b_s', 'w_branch_a', 'w_branch_b', 'w_out', 'norm_ffn_g', 'w_up', 'w_down', 'final_norm_g', 'loss_target', 'm_norm_mix_g', 'm_w_in', 'm_conv_w', 'm_conv_b', 'm_lru_w_a', 'm_lru_b_a', 'm_lru_w_x', 'm_lru_b_x', 'm_lru_lambda', 'm_sgu_ln_g', 'm_sgu_ln_b', 'm_sgu_w_s', 'm_sgu_b_s', 'm_w_branch_a', 'm_w_branch_b', 'm_w_out', 'm_norm_ffn_g', 'm_w_up', 'm_w_down', 'm_final_norm_g', 'v_norm_mix_g', 'v_w_in', 'v_conv_w', 'v_conv_b', 'v_lru_w_a', 'v_lru_b_a', 'v_lru_w_x', 'v_lru_b_x', 'v_lru_lambda', 'v_sgu_ln_g', 'v_sgu_ln_b', 'v_sgu_w_s', 'v_sgu_b_s', 'v_w_branch_a', 'v_w_branch_b', 'v_w_out', 'v_norm_ffn_g', 'v_w_up', 'v_w_down', 'v_final_norm_g']
TWIN_OUTPUTS = ['loss', 'grad_x', 'grad_norm_mix_g', 'grad_w_in', 'grad_conv_w', 'grad_conv_b', 'grad_lru_w_a', 'grad_lru_b_a', 'grad_lru_w_x', 'grad_lru_b_x', 'grad_lru_lambda', 'grad_sgu_ln_g', 'grad_sgu_ln_b', 'grad_sgu_w_s', 'grad_sgu_b_s', 'grad_w_branch_a', 'grad_w_branch_b', 'grad_w_out', 'grad_norm_ffn_g', 'grad_w_up', 'grad_w_down', 'grad_final_norm_g', 'delta_norm_mix_g', 'delta_w_in', 'delta_conv_w', 'delta_conv_b', 'delta_lru_w_a', 'delta_lru_b_a', 'delta_lru_w_x', 'delta_lru_b_x', 'delta_lru_lambda', 'delta_sgu_ln_g', 'delta_sgu_ln_b', 'delta_sgu_w_s', 'delta_sgu_b_s', 'delta_w_branch_a', 'delta_w_branch_b', 'delta_w_out', 'delta_norm_ffn_g', 'delta_w_up', 'delta_w_down', 'delta_final_norm_g', 'new_m_norm_mix_g', 'new_m_w_in', 'new_m_conv_w', 'new_m_conv_b', 'new_m_lru_w_a', 'new_m_lru_b_a', 'new_m_lru_w_x', 'new_m_lru_b_x', 'new_m_lru_lambda', 'new_m_sgu_ln_g', 'new_m_sgu_ln_b', 'new_m_sgu_w_s', 'new_m_sgu_b_s', 'new_m_w_branch_a', 'new_m_w_branch_b', 'new_m_w_out', 'new_m_norm_ffn_g', 'new_m_w_up', 'new_m_w_down', 'new_m_final_norm_g', 'new_v_norm_mix_g', 'new_v_w_in', 'new_v_conv_w', 'new_v_conv_b', 'new_v_lru_w_a', 'new_v_lru_b_a', 'new_v_lru_w_x', 'new_v_lru_b_x', 'new_v_lru_lambda', 'new_v_sgu_ln_g', 'new_v_sgu_ln_b', 'new_v_sgu_w_s', 'new_v_sgu_b_s', 'new_v_w_branch_a', 'new_v_w_branch_b', 'new_v_w_out', 'new_v_norm_ffn_g', 'new_v_w_up', 'new_v_w_down', 'new_v_final_norm_g']
TWIN_LEAF_KINDS = {'loss': 'loss', 'grad_x': 'grad_x', 'grad_norm_mix_g': 'grad_w', 'grad_w_in': 'grad_w', 'grad_conv_w': 'grad_w', 'grad_conv_b': 'grad_w', 'grad_lru_w_a': 'grad_w', 'grad_lru_b_a': 'grad_w', 'grad_lru_w_x': 'grad_w', 'grad_lru_b_x': 'grad_w', 'grad_lru_lambda': 'grad_w', 'grad_sgu_ln_g': 'grad_w', 'grad_sgu_ln_b': 'grad_w', 'grad_sgu_w_s': 'grad_w', 'grad_sgu_b_s': 'grad_w', 'grad_w_branch_a': 'grad_w', 'grad_w_branch_b': 'grad_w', 'grad_w_out': 'grad_w', 'grad_norm_ffn_g': 'grad_w', 'grad_w_up': 'grad_w', 'grad_w_down': 'grad_w', 'grad_final_norm_g': 'grad_w', 'delta_norm_mix_g': 'delta_w', 'delta_w_in': 'delta_w', 'delta_conv_w': 'delta_w', 'delta_conv_b': 'delta_w', 'delta_lru_w_a': 'delta_w', 'delta_lru_b_a': 'delta_w', 'delta_lru_w_x': 'delta_w', 'delta_lru_b_x': 'delta_w', 'delta_lru_lambda': 'delta_w', 'delta_sgu_ln_g': 'delta_w', 'delta_sgu_ln_b': 'delta_w', 'delta_sgu_w_s': 'delta_w', 'delta_sgu_b_s': 'delta_w', 'delta_w_branch_a': 'delta_w', 'delta_w_branch_b': 'delta_w', 'delta_w_out': 'delta_w', 'delta_norm_ffn_g': 'delta_w', 'delta_w_up': 'delta_w', 'delta_w_down': 'delta_w', 'delta_final_norm_g': 'delta_w', 'new_m_norm_mix_g': 'new_m', 'new_m_w_in': 'new_m', 'new_m_conv_w': 'new_m', 'new_m_conv_b': 'new_m', 'new_m_lru_w_a': 'new_m', 'new_m_lru_b_a': 'new_m', 'new_m_lru_w_x': 'new_m', 'new_m_lru_b_x': 'new_m', 'new_m_lru_lambda': 'new_m', 'new_m_sgu_ln_g': 'new_m', 'new_m_sgu_ln_b': 'new_m', 'new_m_sgu_w_s': 'new_m', 'new_m_sgu_b_s': 'new_m', 'new_m_w_branch_a': 'new_m', 'new_m_w_branch_b': 'new_m', 'new_m_w_out': 'new_m', 'new_m_norm_ffn_g': 'new_m', 'new_m_w_up': 'new_m', 'new_m_w_down': 'new_m', 'new_m_final_norm_g': 'new_m', 'new_v_norm_mix_g': 'new_v', 'new_v_w_in': 'new_v', 'new_v_conv_w': 'new_v', 'new_v_conv_b': 'new_v', 'new_v_lru_w_a': 'new_v', 'new_v_lru_b_a': 'new_v', 'new_v_lru_w_x': 'new_v', 'new_v_lru_b_x': 'new_v', 'new_v_lru_lambda': 'new_v', 'new_v_sgu_ln_g': 'new_v', 'new_v_sgu_ln_b': 'new_v', 'new_v_sgu_w_s': 'new_v', 'new_v_sgu_b_s': 'new_v', 'new_v_w_branch_a': 'new_v', 'new_v_w_branch_b': 'new_v', 'new_v_w_out': 'new_v', 'new_v_norm_ffn_g': 'new_v', 'new_v_w_up': 'new_v', 'new_v_w_down': 'new_v', 'new_v_final_norm_g': 'new_v'}


def _forward(args):
    return _fwd_reference(*[args[k] for k in FWD_PARAMS])


def _output_shape():
    def fwd():
        inp = _fwd_setup_inputs(0)
        return _fwd_reference(*[inp[k] for k in FWD_PARAMS])
    out = _jax.eval_shape(fwd)
    return out.shape, out.dtype

N_MICROBATCH = 1
ADAM_LR = 0.001
ADAM_B1 = 0.9
ADAM_B2 = 0.999
ADAM_EPS = 1e-08
ADAM_WD = 0.01
ADAM_STEP = 10
PER_EXAMPLE_BATCH_AXIS = {'x': 0, 'loss_target': 0}
SHARED_INPUTS = []
_WEIGHT_DTYPES = {'norm_mix_g': _jnp.float32, 'w_in': _jnp.float32, 'conv_w': _jnp.float32, 'conv_b': _jnp.float32, 'lru_w_a': _jnp.float32, 'lru_b_a': _jnp.float32, 'lru_w_x': _jnp.float32, 'lru_b_x': _jnp.float32, 'lru_lambda': _jnp.float32, 'sgu_ln_g': _jnp.float32, 'sgu_ln_b': _jnp.float32, 'sgu_w_s': _jnp.float32, 'sgu_b_s': _jnp.float32, 'w_branch_a': _jnp.float32, 'w_branch_b': _jnp.float32, 'w_out': _jnp.float32, 'norm_ffn_g': _jnp.float32, 'w_up': _jnp.float32, 'w_down': _jnp.float32, 'final_norm_g': _jnp.float32}
MOMENT_SCALE = {'norm_mix_g': 1.402411e-01, 'w_in': 5.561201e-02, 'conv_w': 7.391083e-02, 'conv_b': 4.029327e-01, 'lru_w_a': 1.441700e-02, 'lru_b_a': 1.505660e-02, 'lru_w_x': 2.782154e-02, 'lru_b_x': 2.954580e-02, 'lru_lambda': 3.236469e-02, 'sgu_ln_g': 4.990572e-02, 'sgu_ln_b': 4.644510e-02, 'sgu_w_s': 4.629870e-02, 'sgu_b_s': 5.356691e-02, 'w_branch_a': 7.758717e-02, 'w_branch_b': 7.945838e-02, 'w_out': 1.071202e-01, 'norm_ffn_g': 1.438643e-01, 'w_up': 7.288735e-02, 'w_down': 1.624651e-01, 'final_norm_g': 3.275926e+01}


def _to_microbatches(a, axis):
    t = _jnp.moveaxis(a, axis, 0)
    t = t.reshape((N_MICROBATCH, t.shape[0] // N_MICROBATCH) + t.shape[1:])
    return _jnp.moveaxis(t, 1, axis + 1)


def setup_inputs(seed: int = 0) -> dict:
    inp = _fwd_setup_inputs(seed)
    key = _jax.random.fold_in(_jax.random.key(seed), 7919)
    shape, _ = _output_shape()
    out = dict(inp)
    out["loss_target"] = _jax.random.normal(_jax.random.fold_in(key, 0), shape, _jnp.float32)
    for i, name in enumerate(TWIN_WEIGHTS):
        w = inp[name].astype(_jnp.float32)
        if MOMENT_SCALE is None:
            s = _jnp.sqrt(_jnp.mean(_jnp.square(w)) + 1e-30)
        else:
            s = MOMENT_SCALE[name]
        km, kv = _jax.random.split(_jax.random.fold_in(key, i + 1))
        out[name] = w
        out["m_" + name] = s * _jax.random.normal(km, w.shape, _jnp.float32)
        out["v_" + name] = (s * s) * _jax.random.uniform(kv, w.shape, _jnp.float32, 0.5, 1.5)
    if N_MICROBATCH > 1:
        for name, axis in PER_EXAMPLE_BATCH_AXIS.items():
            out[name] = _to_microbatches(out[name], axis)
    return {'x': out['x'], 'norm_mix_g': out['norm_mix_g'], 'w_in': out['w_in'], 'conv_w': out['conv_w'], 'conv_b': out['conv_b'], 'lru_w_a': out['lru_w_a'], 'lru_b_a': out['lru_b_a'], 'lru_w_x': out['lru_w_x'], 'lru_b_x': out['lru_b_x'], 'lru_lambda': out['lru_lambda'], 'sgu_ln_g': out['sgu_ln_g'], 'sgu_ln_b': out['sgu_ln_b'], 'sgu_w_s': out['sgu_w_s'], 'sgu_b_s': out['sgu_b_s'], 'w_branch_a': out['w_branch_a'], 'w_branch_b': out['w_branch_b'], 'w_out': out['w_out'], 'norm_ffn_g': out['norm_ffn_g'], 'w_up': out['w_up'], 'w_down': out['w_down'], 'final_norm_g': out['final_norm_g'], 'loss_target': out['loss_target'], 'm_norm_mix_g': out['m_norm_mix_g'], 'm_w_in': out['m_w_in'], 'm_conv_w': out['m_conv_w'], 'm_conv_b': out['m_conv_b'], 'm_lru_w_a': out['m_lru_w_a'], 'm_lru_b_a': out['m_lru_b_a'], 'm_lru_w_x': out['m_lru_w_x'], 'm_lru_b_x': out['m_lru_b_x'], 'm_lru_lambda': out['m_lru_lambda'], 'm_sgu_ln_g': out['m_sgu_ln_g'], 'm_sgu_ln_b': out['m_sgu_ln_b'], 'm_sgu_w_s': out['m_sgu_w_s'], 'm_sgu_b_s': out['m_sgu_b_s'], 'm_w_branch_a': out['m_w_branch_a'], 'm_w_branch_b': out['m_w_branch_b'], 'm_w_out': out['m_w_out'], 'm_norm_ffn_g': out['m_norm_ffn_g'], 'm_w_up': out['m_w_up'], 'm_w_down': out['m_w_down'], 'm_final_norm_g': out['m_final_norm_g'], 'v_norm_mix_g': out['v_norm_mix_g'], 'v_w_in': out['v_w_in'], 'v_conv_w': out['v_conv_w'], 'v_conv_b': out['v_conv_b'], 'v_lru_w_a': out['v_lru_w_a'], 'v_lru_b_a': out['v_lru_b_a'], 'v_lru_w_x': out['v_lru_w_x'], 'v_lru_b_x': out['v_lru_b_x'], 'v_lru_lambda': out['v_lru_lambda'], 'v_sgu_ln_g': out['v_sgu_ln_g'], 'v_sgu_ln_b': out['v_sgu_ln_b'], 'v_sgu_w_s': out['v_sgu_w_s'], 'v_sgu_b_s': out['v_sgu_b_s'], 'v_w_branch_a': out['v_w_branch_a'], 'v_w_branch_b': out['v_w_branch_b'], 'v_w_out': out['v_w_out'], 'v_norm_ffn_g': out['v_norm_ffn_g'], 'v_w_up': out['v_w_up'], 'v_w_down': out['v_w_down'], 'v_final_norm_g': out['v_final_norm_g']}


def _loss(weights, diff, rest, loss_target):
    with _jax.named_scope("forward"):
        args = {**rest, TWIN_DIFF_INPUT: diff, **{k: w.astype(_WEIGHT_DTYPES[k]) for k, w in weights.items()}}
        y = _forward(args)
    with _jax.named_scope("loss_head"):
        err = _jnp.square(y.astype(_jnp.float32) - loss_target)
        return 0.5 * _jnp.sum(_jnp.mean(err, axis=-1)) if err.ndim else 0.5 * err


def _adamw(w, g, m, v):
    m = ADAM_B1 * m + (1.0 - ADAM_B1) * g
    v = ADAM_B2 * v + (1.0 - ADAM_B2) * _jnp.square(g)
    m_hat = m / (1.0 - ADAM_B1 ** ADAM_STEP)
    v_hat = v / (1.0 - ADAM_B2 ** ADAM_STEP)
    delta = -ADAM_LR * (m_hat / (_jnp.sqrt(v_hat) + ADAM_EPS) + ADAM_WD * w)
    return delta, m, v


def reference(x, norm_mix_g, w_in, conv_w, conv_b, lru_w_a, lru_b_a, lru_w_x, lru_b_x, lru_lambda, sgu_ln_g, sgu_ln_b, sgu_w_s, sgu_b_s, w_branch_a, w_branch_b, w_out, norm_ffn_g, w_up, w_down, final_norm_g, loss_target, m_norm_mix_g, m_w_in, m_conv_w, m_conv_b, m_lru_w_a, m_lru_b_a, m_lru_w_x, m_lru_b_x, m_lru_lambda, m_sgu_ln_g, m_sgu_ln_b, m_sgu_w_s, m_sgu_b_s, m_w_branch_a, m_w_branch_b, m_w_out, m_norm_ffn_g, m_w_up, m_w_down, m_final_norm_g, v_norm_mix_g, v_w_in, v_conv_w, v_conv_b, v_lru_w_a, v_lru_b_a, v_lru_w_x, v_lru_b_x, v_lru_lambda, v_sgu_ln_g, v_sgu_ln_b, v_sgu_w_s, v_sgu_b_s, v_w_branch_a, v_w_branch_b, v_w_out, v_norm_ffn_g, v_w_up, v_w_down, v_final_norm_g):
    given = dict(x=x, norm_mix_g=norm_mix_g, w_in=w_in, conv_w=conv_w, conv_b=conv_b, lru_w_a=lru_w_a, lru_b_a=lru_b_a, lru_w_x=lru_w_x, lru_b_x=lru_b_x, lru_lambda=lru_lambda, sgu_ln_g=sgu_ln_g, sgu_ln_b=sgu_ln_b, sgu_w_s=sgu_w_s, sgu_b_s=sgu_b_s, w_branch_a=w_branch_a, w_branch_b=w_branch_b, w_out=w_out, norm_ffn_g=norm_ffn_g, w_up=w_up, w_down=w_down, final_norm_g=final_norm_g, loss_target=loss_target, m_norm_mix_g=m_norm_mix_g, m_w_in=m_w_in, m_conv_w=m_conv_w, m_conv_b=m_conv_b, m_lru_w_a=m_lru_w_a, m_lru_b_a=m_lru_b_a, m_lru_w_x=m_lru_w_x, m_lru_b_x=m_lru_b_x, m_lru_lambda=m_lru_lambda, m_sgu_ln_g=m_sgu_ln_g, m_sgu_ln_b=m_sgu_ln_b, m_sgu_w_s=m_sgu_w_s, m_sgu_b_s=m_sgu_b_s, m_w_branch_a=m_w_branch_a, m_w_branch_b=m_w_branch_b, m_w_out=m_w_out, m_norm_ffn_g=m_norm_ffn_g, m_w_up=m_w_up, m_w_down=m_w_down, m_final_norm_g=m_final_norm_g, v_norm_mix_g=v_norm_mix_g, v_w_in=v_w_in, v_conv_w=v_conv_w, v_conv_b=v_conv_b, v_lru_w_a=v_lru_w_a, v_lru_b_a=v_lru_b_a, v_lru_w_x=v_lru_w_x, v_lru_b_x=v_lru_b_x, v_lru_lambda=v_lru_lambda, v_sgu_ln_g=v_sgu_ln_g, v_sgu_ln_b=v_sgu_ln_b, v_sgu_w_s=v_sgu_w_s, v_sgu_b_s=v_sgu_b_s, v_w_branch_a=v_w_branch_a, v_w_branch_b=v_w_branch_b, v_w_out=v_w_out, v_norm_ffn_g=v_norm_ffn_g, v_w_up=v_w_up, v_w_down=v_w_down, v_final_norm_g=v_final_norm_g)
    weights = {n: given[n] for n in TWIN_WEIGHTS}
    shared = {n: given[n] for n in SHARED_INPUTS}
    per_example = {n: given[n] for n in ['x']}
    grad_fn = _jax.value_and_grad(_loss, argnums=(0, 1))

    def one_microbatch(ex, loss_target):
        ex = dict(ex)
        diff = ex.pop(TWIN_DIFF_INPUT)
        return grad_fn(weights, diff, {**shared, **ex}, loss_target)

    if N_MICROBATCH == 1:
        loss, (grad_w, grad_x) = one_microbatch(per_example, given["loss_target"])
    else:
        def body(carry, xs):
            loss_sum, grad_sum = carry
            l_k, (gw_k, gx_k) = one_microbatch(xs[0], xs[1])
            with _jax.named_scope("update"):
                return (loss_sum + l_k, _jax.tree.map(_jnp.add, grad_sum, gw_k)), gx_k

        init = (_jnp.zeros((), _jnp.float32), _jax.tree.map(_jnp.zeros_like, weights))
        (loss, grad_w), grad_x = _jax.lax.scan(body, init, (per_example, given["loss_target"]))
    with _jax.named_scope("update"):
        delta_w, new_m, new_v = {}, {}, {}
        for n in TWIN_WEIGHTS:
            delta_w[n], new_m[n], new_v[n] = _adamw(weights[n], grad_w[n], given["m_" + n], given["v_" + n])
    return (loss, grad_x, *[grad_w[n] for n in TWIN_WEIGHTS], *[delta_w[n] for n in TWIN_WEIGHTS],
            *[new_m[n] for n in TWIN_WEIGHTS], *[new_v[n] for n in TWIN_WEIGHTS])
```

```python
import jax
import jax.numpy as jnp
from jax import lax
from jax.experimental import pallas as pl
from jax.experimental.pallas import tpu as pltpu

F32 = jnp.float32
BF16 = jnp.bfloat16
SDS = jax.ShapeDtypeStruct
MESH = pl.DeviceIdType.MESH

D = 1024
D_RNN = 1280
D_SGU = 1024
D_FF = 4096
D_IN = 2 * D_RNN + 2 * D_SGU + 2 * D
DEPTH = 2
RNN_HEADS = 20
HEAD_DIM = 64
CONV_WIDTH = 4
LRU_C = 8.0
SGU_GROUPS = 8
SGU_BLOCK = 128
CHUNK = 64
EPS = 1e-6
N_DEV = 8

ADAM_LR = 0.001
ADAM_B1 = 0.9
ADAM_B2 = 0.999
ADAM_EPS = 1e-08
ADAM_WD = 0.01
ADAM_STEP = 10

LANES = 128
SUBLANES = 8
VMEM_LIMIT_BYTES = 56 * 1024 * 1024

N_RNN_TILES = D_RNN // LANES
GRNN_BLK128 = D_RNN // LANES
U_BLK512 = (2 * D_RNN) // 512
V_BLK512 = (2 * D_RNN + D_SGU) // 512
GA_BLK512 = (2 * D_RNN + 2 * D_SGU) // 512
GB_BLK512 = (2 * D_RNN + 2 * D_SGU + D) // 512

SMALL_ROWS_PER_DEV = 80
SMALL_ROWS = N_DEV * SMALL_ROWS_PER_DEV


def _params(*sem):
    return pltpu.CompilerParams(dimension_semantics=sem, vmem_limit_bytes=VMEM_LIMIT_BYTES)


def _sigmoid(x):
    return 1.0 / (1.0 + jnp.exp(-x))


_GELU_C = 0.7978845608028654
_GELU_K = 0.044715


def _gelu(x):
    t = jnp.tanh(_GELU_C * (x + _GELU_K * x * x * x))
    return 0.5 * x * (1.0 + t)


def _gelu_and_grad(x):
    t = jnp.tanh(_GELU_C * (x + _GELU_K * x * x * x))
    val = 0.5 * x * (1.0 + t)
    grad = 0.5 * (1.0 + t) + 0.5 * x * (1.0 - t * t) * _GELU_C * (1.0 + 3.0 * _GELU_K * x * x)
    return val, grad


def _one_minus_exp(y):
    series = -(y * (1.0 + y * (0.5 + y * (1.0 / 6.0 + y * (1.0 / 24.0)))))
    return jnp.where(y > -0.03, series, 1.0 - jnp.exp(y))


def _dot(a, b):
    return jnp.dot(a, b, preferred_element_type=F32)


def _dot_nt(a, b):
    return lax.dot_general(a, b, (((1,), (1,)), ((), ())), preferred_element_type=F32)


def _dot_tn(a, b):
    return lax.dot_general(a, b, (((0,), (0,)), ((), ())), preferred_element_type=F32)


def _norm_matmul_nt(x, g, w, *, tm, tn, name):
    s, n = x.shape[0], w.shape[0]
    tm, tn = min(tm, s), min(tn, n)

    def body(x_ref, g_ref, w_ref, o_ref, h_ref):
        @pl.when(pl.program_id(1) == 0)
        def _():
            xv = x_ref[...]
            r = lax.rsqrt(jnp.mean(xv * xv, axis=-1, keepdims=True) + EPS)
            h_ref[...] = (xv * r * g_ref[...]).astype(BF16)

        o_ref[...] = _dot_nt(h_ref[...], w_ref[...]).astype(o_ref.dtype)

    return pl.pallas_call(
        body,
        name=name,
        grid=(s // tm, n // tn),
        in_specs=[
            pl.BlockSpec((tm, D), lambda i, j: (i, 0)),
            pl.BlockSpec((1, D), lambda i, j: (0, 0)),
            pl.BlockSpec((tn, D), lambda i, j: (j, 0)),
        ],
        out_specs=[pl.BlockSpec((tm, tn), lambda i, j: (i, j)), pl.BlockSpec((tm, D), lambda i, j: (i, 0))],
        out_shape=[SDS((s, n), BF16), SDS((s, D), BF16)],
        compiler_params=_params("parallel", "arbitrary"),
    )(x, g, w)


def _matmul_nn_res(a, w, res, *, relu2, tm, tk, name):
    s, k = a.shape
    tm, tk = min(tm, s), min(tk, k)

    def body(a_ref, w_ref, r_ref, o_ref):
        av = a_ref[...]
        if relu2:
            t = jnp.maximum(av.astype(F32), 0.0)
            av = (t * t).astype(BF16)
        p = _dot(av, w_ref[...])

        @pl.when(pl.program_id(1) == 0)
        def _():
            o_ref[...] = r_ref[...] + p

        @pl.when(pl.program_id(1) > 0)
        def _():
            o_ref[...] += p

    return pl.pallas_call(
        body,
        name=name,
        grid=(s // tm, k // tk),
        in_specs=[
            pl.BlockSpec((tm, tk), lambda i, j: (i, j)),
            pl.BlockSpec((tk, D), lambda i, j: (j, 0)),
            pl.BlockSpec((tm, D), lambda i, j: (i, 0)),
        ],
        out_specs=pl.BlockSpec((tm, D), lambda i, j: (i, 0)),
        out_shape=SDS((s, D), F32),
        compiler_params=_params("parallel", "arbitrary"),
    )(a, w, res)


def _matmul_nt_drelu2(a, w, pre, *, tm, tn, name):
    s, n = a.shape[0], w.shape[0]
    tm, tn = min(tm, s), min(tn, n)

    def body(a_ref, w_ref, p_ref, o_ref, abf):
        @pl.when(pl.program_id(1) == 0)
        def _():
            abf[...] = a_ref[...].astype(BF16)

        d = _dot_nt(abf[...], w_ref[...])
        o_ref[...] = (d * (2.0 * jnp.maximum(p_ref[...].astype(F32), 0.0))).astype(o_ref.dtype)

    return pl.pallas_call(
        body,
        name=name,
        grid=(s // tm, n // tn),
        in_specs=[
            pl.BlockSpec((tm, D), lambda i, j: (i, 0)),
            pl.BlockSpec((tn, D), lambda i, j: (j, 0)),
            pl.BlockSpec((tm, tn), lambda i, j: (i, j)),
        ],
        out_specs=pl.BlockSpec((tm, tn), lambda i, j: (i, j)),
        out_shape=SDS((s, n), BF16),
        scratch_shapes=[pltpu.VMEM((tm, D), BF16)],
        compiler_params=_params("parallel", "arbitrary"),
    )(a, w, pre)


def _matmul_tn(a, b, *, relu2, tka, ts, name):
    s, ka = a.shape
    tka, ts = min(tka, ka), min(ts, s)
    ns = s // ts

    def body(a_ref, b_ref, o_ref, acc):
        av = a_ref[...]
        if relu2:
            t = jnp.maximum(av.astype(F32), 0.0)
            av = t * t
        p = _dot_tn(av.astype(BF16), b_ref[...].astype(BF16))

        @pl.when(pl.program_id(1) == 0)
        def _():
            acc[...] = p

        @pl.when(pl.program_id(1) > 0)
        def _():
            acc[...] += p

        @pl.when(pl.program_id(1) == ns - 1)
        def _():
            o_ref[...] = acc[...].astype(o_ref.dtype)

    return pl.pallas_call(
        body,
        name=name,
        grid=(ka // tka, ns),
        in_specs=[pl.BlockSpec((ts, tka), lambda i, j: (j, i)), pl.BlockSpec((ts, D), lambda i, j: (j, 0))],
        out_specs=pl.BlockSpec((tka, D), lambda i, j: (i, 0)),
        out_shape=SDS((ka, D), BF16),
        scratch_shapes=[pltpu.VMEM((tka, D), F32)],
        compiler_params=_params("parallel", "arbitrary"),
    )(a, b)


def _matmul_nn_rmsnorm_bwd(a, w, x, g, res, *, tm, tk, name):
    s, k = a.shape
    tm, tk = min(tm, s), min(tk, k)
    nk = k // tk

    def body(a_ref, w_ref, x_ref, g_ref, r_ref, dx_ref, dg_ref, acc):
        i, j = pl.program_id(0), pl.program_id(1)
        p = _dot(a_ref[...], w_ref[...])

        @pl.when(j == 0)
        def _():
            acc[...] = p

        @pl.when(j > 0)
        def _():
            acc[...] += p

        @pl.when((i == 0) & (j == 0))
        def _():
            dg_ref[...] = jnp.zeros_like(dg_ref)

        @pl.when(j == nk - 1)
        def _():
            dh = acc[...]
            xv = x_ref[...]
            r = lax.rsqrt(jnp.mean(xv * xv, axis=-1, keepdims=True) + EPS)
            xhat = xv * r
            dxh = dh * g_ref[...]
            dx_ref[...] = r_ref[...] + r * (dxh - xhat * jnp.mean(dxh * xhat, axis=-1, keepdims=True))
            dg_ref[...] += jnp.sum(dh * xhat, axis=0, keepdims=True)

    return pl.pallas_call(
        body,
        name=name,
        grid=(s // tm, nk),
        in_specs=[
            pl.BlockSpec((tm, tk), lambda i, j: (i, j)),
            pl.BlockSpec((tk, D), lambda i, j: (j, 0)),
            pl.BlockSpec((tm, D), lambda i, j: (i, 0)),
            pl.BlockSpec((1, D), lambda i, j: (0, 0)),
            pl.BlockSpec((tm, D), lambda i, j: (i, 0)),
        ],
        out_specs=[pl.BlockSpec((tm, D), lambda i, j: (i, 0)), pl.BlockSpec((1, D), lambda i, j: (0, 0))],
        out_shape=[SDS((s, D), F32), SDS((1, D), F32)],
        scratch_shapes=[pltpu.VMEM((tm, D), F32)],
        compiler_params=_params("arbitrary", "arbitrary"),
    )(a, w, x, g, res)


def _rows_before(ext, k):
    if k == 0:
        return ext[SUBLANES:, :]
    return pltpu.roll(ext, k, 0)[SUBLANES:, :]


def _rows_after(ext, k, n):
    if k == 0:
        return ext[:n, :]
    return pltpu.roll(ext, n + SUBLANES - k, 0)[:n, :]


def _scan_forward(a, b, n):
    row = lax.broadcasted_iota(jnp.int32, a.shape, 0)
    d = 1
    while d < n:
        m = row >= d
        a_s = jnp.where(m, pltpu.roll(a, d, 0), 1.0)
        b_s = jnp.where(m, pltpu.roll(b, d, 0), 0.0)
        b = a * b_s + b
        a = a * a_s
        d *= 2
    return a, b


def _scan_backward(a, b, n):
    row = lax.broadcasted_iota(jnp.int32, a.shape, 0)
    d = 1
    while d < n:
        m = row < n - d
        a_s = jnp.where(m, pltpu.roll(a, n - d, 0), 1.0)
        b_s = jnp.where(m, pltpu.roll(b, n - d, 0), 0.0)
        b = a * b_s + b
        a = a * a_s
        d *= 2
    return b


def _softplus_neg(lam):
    z = -lam
    return jnp.maximum(z, 0.0) + jnp.log1p(jnp.exp(-jnp.abs(z)))


def _conv_and_gates(xc, xprev, cw_ref, cb_ref, wa_ref, ba_ref, wx_ref, bx_ref, lam_ref):
    ext = jnp.concatenate([xprev, xc], axis=0)
    x1, x2, x3 = _rows_before(ext, 1), _rows_before(ext, 2), _rows_before(ext, 3)
    xr = cb_ref[...] + x3 * cw_ref[0:1, :] + x2 * cw_ref[1:2, :] + x1 * cw_ref[2:3, :] + xc * cw_ref[3:4, :]
    xrb = xr.astype(BF16)
    r = _sigmoid(_dot(xrb, wa_ref[...]) + ba_ref[...])
    i = _sigmoid(_dot(xrb, wx_ref[...]) + bx_ref[...])
    sp = _softplus_neg(lam_ref[...])
    log_a = (-LRU_C * r) * sp
    a = jnp.exp(log_a)
    one_minus_a2 = _one_minus_exp(2.0 * log_a)
    return xr, (x1, x2, x3), r, i, a, one_minus_a2


def _branch_a_fwd(proj, cw, cb, wa2, ba, wx2, bx, lam, *, tc, name):
    s = proj.shape[0]
    tc = min(tc, s)

    def body(x_ref, g_ref, cw_ref, cb_ref, wa_ref, ba_ref, wx_ref, bx_ref, lam_ref, h_ref, y_ref, xprev, hlast):
        @pl.when(pl.program_id(1) == 0)
        def _():
            xprev[...] = jnp.zeros_like(xprev)
            hlast[...] = jnp.zeros_like(hlast)

        xc = x_ref[...].astype(F32)
        xr, _, r, i, a, om = _conv_and_gates(xc, xprev[...], cw_ref, cb_ref, wa_ref, ba_ref, wx_ref, bx_ref, lam_ref)
        xprev[...] = xc[tc - SUBLANES :, :]
        u = jnp.sqrt(om) * (i * xr)
        acum, b = _scan_forward(a, u, tc)
        h = b + acum * hlast[SUBLANES - 1 : SUBLANES, :]
        hlast[...] = h[tc - SUBLANES :, :]
        h_ref[...] = h
        y_ref[...] = (h * _gelu(g_ref[...].astype(F32))).astype(BF16)

    tile = lambda j, c: (0, j)
    return pl.pallas_call(
        body,
        name=name,
        grid=(N_RNN_TILES, s // tc),
        in_specs=[
            pl.BlockSpec((tc, LANES), lambda j, c: (c, j)),
            pl.BlockSpec((tc, LANES), lambda j, c: (c, GRNN_BLK128 + j)),
            pl.BlockSpec((CONV_WIDTH, LANES), tile),
            pl.BlockSpec((1, LANES), tile),
            pl.BlockSpec((None, LANES, LANES), lambda j, c: (j, 0, 0)),
            pl.BlockSpec((1, LANES), tile),
            pl.BlockSpec((None, LANES, LANES), lambda j, c: (j, 0, 0)),
            pl.BlockSpec((1, LANES), tile),
            pl.BlockSpec((1, LANES), tile),
        ],
        out_specs=[pl.BlockSpec((tc, LANES), lambda j, c: (c, j)), pl.BlockSpec((tc, LANES), lambda j, c: (c, j))],
        out_shape=[SDS((s, D_RNN), F32), SDS((s, D_RNN), BF16)],
        scratch_shapes=[pltpu.VMEM((SUBLANES, LANES), F32), pltpu.VMEM((SUBLANES, LANES), F32)],
        compiler_params=_params("parallel", "arbitrary"),
    )(proj, proj, cw, cb, wa2, ba, wx2, bx, lam)


def _branch_a_bwd(dy, proj, h, cw, cb, wa2, ba, wx2, bx, lam, wa2t, wx2t, *, tc, name):
    s = proj.shape[0]
    tc = min(tc, s)
    nc = s // tc
    halo16 = tc // 16
    halo8 = tc // SUBLANES

    def body(dy_ref, x_ref, xh_ref, g_ref, h_ref, hh_ref, cw_ref, cb_ref, wa_ref, ba_ref, wx_ref, bx_ref, lam_ref,
             wat_ref, wxt_ref, dx_ref, dg_ref, dcw_ref, dcb_ref, dba_ref, dbx_ref, dlam_ref, dwa_ref, dwx_ref,
             carry, dxr_next):
        cc = pl.program_id(1)
        ct = nc - 1 - cc

        @pl.when(cc == 0)
        def _():
            carry[...] = jnp.zeros_like(carry)
            dxr_next[...] = jnp.zeros_like(dxr_next)
            for ref in (dcw_ref, dcb_ref, dba_ref, dbx_ref, dlam_ref, dwa_ref, dwx_ref):
                ref[...] = jnp.zeros_like(ref)

        xc = x_ref[...].astype(F32)
        xprev = jnp.where(ct > 0, xh_ref[SUBLANES:, :].astype(F32), 0.0)
        xr, (x1, x2, x3), r, i, a, om = _conv_and_gates(
            xc, xprev, cw_ref, cb_ref, wa_ref, ba_ref, wx_ref, bx_ref, lam_ref
        )
        norm = jnp.sqrt(om)
        row = lax.broadcasted_iota(jnp.int32, xc.shape, 0)

        hv = h_ref[...]
        ge, ge_grad = _gelu_and_grad(g_ref[...].astype(F32))
        dyv = dy_ref[...].astype(F32)
        dg_ref[...] = (dyv * hv * ge_grad).astype(dg_ref.dtype)
        dh = dyv * ge

        b = dh + jnp.where(row == tc - 1, carry[0:1, :], 0.0)
        a_next = jnp.where(row < tc - 1, pltpu.roll(a, tc - 1, 0), 0.0)
        gadj = _scan_backward(a_next, b, tc)
        carry[...] = (a * gadj)[:SUBLANES, :]

        hprev_first = jnp.where(ct > 0, hh_ref[SUBLANES - 1 : SUBLANES, :], 0.0)
        hprev = jnp.where(row >= 1, pltpu.roll(hv, 1, 0), hprev_first)
        da = gadj * hprev
        ix = i * xr
        dnorm = gadj * ix
        di = gadj * norm * xr
        dlog_a = da * a - dnorm * (1.0 - om) / norm
        sp = _softplus_neg(lam_ref[...])
        dr = dlog_a * (-LRU_C * sp)
        dsp = jnp.sum(dlog_a * (-LRU_C * r), axis=0, keepdims=True)
        dlam_ref[...] += dsp * (-_sigmoid(-lam_ref[...]))
        dza = dr * r * (1.0 - r)
        dzx = di * i * (1.0 - i)
        dzab, dzxb = dza.astype(BF16), dzx.astype(BF16)
        dxr = gadj * norm * i + _dot(dzab, wat_ref[...]) + _dot(dzxb, wxt_ref[...])
        xrb = xr.astype(BF16)
        dwa_ref[...] += _dot_tn(xrb, dzab)
        dwx_ref[...] += _dot_tn(xrb, dzxb)
        dba_ref[...] += jnp.sum(dza, axis=0, keepdims=True)
        dbx_ref[...] += jnp.sum(dzx, axis=0, keepdims=True)

        ext = jnp.concatenate([dxr, dxr_next[...]], axis=0)
        dx = (
            dxr * cw_ref[3:4, :]
            + _rows_after(ext, 1, tc) * cw_ref[2:3, :]
            + _rows_after(ext, 2, tc) * cw_ref[1:2, :]
            + _rows_after(ext, 3, tc) * cw_ref[0:1, :]
        )
        dxr_next[...] = dxr[:SUBLANES, :]
        dx_ref[...] = dx.astype(dx_ref.dtype)
        dcb_ref[...] += jnp.sum(dxr, axis=0, keepdims=True)
        dcw_ref[3:4, :] += jnp.sum(dxr * xc, axis=0, keepdims=True)
        dcw_ref[2:3, :] += jnp.sum(dxr * x1, axis=0, keepdims=True)
        dcw_ref[1:2, :] += jnp.sum(dxr * x2, axis=0, keepdims=True)
        dcw_ref[0:1, :] += jnp.sum(dxr * x3, axis=0, keepdims=True)

    tile = lambda j, c: (0, j)
    mat = lambda j, c: (j, 0, 0)
    cur = lambda j, c: (nc - 1 - c, j)
    vec = pl.BlockSpec((1, LANES), tile)
    matspec = pl.BlockSpec((None, LANES, LANES), mat)
    return pl.pallas_call(
        body,
        name=name,
        grid=(N_RNN_TILES, nc),
        in_specs=[
            pl.BlockSpec((tc, LANES), cur),
            pl.BlockSpec((tc, LANES), cur),
            pl.BlockSpec((16, LANES), lambda j, c: (jnp.maximum((nc - 1 - c) * halo16 - 1, 0), j)),
            pl.BlockSpec((tc, LANES), lambda j, c: (nc - 1 - c, GRNN_BLK128 + j)),
            pl.BlockSpec((tc, LANES), cur),
            pl.BlockSpec((SUBLANES, LANES), lambda j, c: (jnp.maximum((nc - 1 - c) * halo8 - 1, 0), j)),
            pl.BlockSpec((CONV_WIDTH, LANES), tile),
            vec,
            matspec,
            vec,
            matspec,
            vec,
            vec,
            matspec,
            matspec,
        ],
        out_specs=[
            pl.BlockSpec((tc, LANES), cur),
            pl.BlockSpec((tc, LANES), cur),
            pl.BlockSpec((CONV_WIDTH, LANES), tile),
            vec,
            vec,
            vec,
            vec,
            matspec,
            matspec,
        ],
        out_shape=[
            SDS((s, D_RNN), BF16),
            SDS((s, D_RNN), BF16),
            SDS((CONV_WIDTH, D_RNN), F32),
            SDS((1, D_RNN), F32),
            SDS((1, D_RNN), F32),
            SDS((1, D_RNN), F32),
            SDS((1, D_RNN), F32),
            SDS((N_RNN_TILES, LANES, LANES), F32),
            SDS((N_RNN_TILES, LANES, LANES), F32),
        ],
        scratch_shapes=[pltpu.VMEM((SUBLANES, LANES), F32), pltpu.VMEM((SUBLANES, LANES), F32)],
        compiler_params=_params("parallel", "arbitrary"),
    )(dy, proj, proj, proj, h, h, cw, cb, wa2, ba, wx2, bx, lam, wa2t, wx2t)


def _sgu_specs(tb):
    half = lambda blk: pl.BlockSpec((tb, 512), lambda n: (n, blk))
    return [half(U_BLK512), half(U_BLK512 + 1), half(V_BLK512), half(V_BLK512 + 1)]


def _sgu_normed(v, lng_ref, lnb_ref):
    gv, gv_grad = _gelu_and_grad(v)
    mu = jnp.mean(gv, axis=-1, keepdims=True)
    xc = gv - mu
    rs = lax.rsqrt(jnp.mean(xc * xc, axis=-1, keepdims=True) + EPS)
    xhat = xc * rs
    return xhat * lng_ref[...] + lnb_ref[...], xhat, rs, gv_grad


def _sgu_fwd(proj, lng, lnb, wm, bias, *, tb, name):
    s = proj.shape[0]
    tb = min(tb, s)

    def body(u0_ref, u1_ref, v0_ref, v1_ref, lng_ref, lnb_ref, wm_ref, bias_ref, y_ref):
        u = jnp.concatenate([u0_ref[...], u1_ref[...]], axis=1).astype(F32)
        v = jnp.concatenate([v0_ref[...], v1_ref[...]], axis=1).astype(F32)
        gu = _gelu(u)
        vn, _, _, _ = _sgu_normed(v, lng_ref, lnb_ref)
        vnb = vn.astype(BF16)
        for blk in range(tb // SGU_BLOCK):
            rows = slice(blk * SGU_BLOCK, (blk + 1) * SGU_BLOCK)
            for g in range(SGU_GROUPS):
                cols = slice(g * LANES, (g + 1) * LANES)
                mixed = _dot(wm_ref[g], vnb[rows, cols]) + bias_ref[g]
                y_ref[rows, cols] = (gu[rows, cols] * mixed).astype(BF16)

    const2 = lambda n: (0, 0)
    const3 = lambda n: (0, 0, 0)
    return pl.pallas_call(
        body,
        name=name,
        grid=(s // tb,),
        in_specs=_sgu_specs(tb)
        + [
            pl.BlockSpec((1, D_SGU), const2),
            pl.BlockSpec((1, D_SGU), const2),
            pl.BlockSpec((SGU_GROUPS, SGU_BLOCK, SGU_BLOCK), const3),
            pl.BlockSpec((SGU_GROUPS, SGU_BLOCK, LANES), const3),
        ],
        out_specs=pl.BlockSpec((tb, D_SGU), lambda n: (n, 0)),
        out_shape=SDS((s, D_SGU), BF16),
        compiler_params=_params("parallel"),
    )(proj, proj, proj, proj, lng, lnb, wm, bias)


def _sgu_bwd(dy, proj, lng, lnb, wm, wmt, bias, mask, *, tb, name):
    s = proj.shape[0]
    tb = min(tb, s)
    nb = s // tb

    def body(dy_ref, u0_ref, u1_ref, v0_ref, v1_ref, lng_ref, lnb_ref, wm_ref, wmt_ref, bias_ref, mask_ref,
             du_ref, dv_ref, dws_ref, dbs_ref, dlng_ref, dlnb_ref, dvn_scr, dbs_acc):
        n = pl.program_id(0)

        @pl.when(n == 0)
        def _():
            dbs_acc[...] = jnp.zeros_like(dbs_acc)
            for ref in (dws_ref, dlng_ref, dlnb_ref):
                ref[...] = jnp.zeros_like(ref)

        u = jnp.concatenate([u0_ref[...], u1_ref[...]], axis=1).astype(F32)
        v = jnp.concatenate([v0_ref[...], v1_ref[...]], axis=1).astype(F32)
        gu, gu_grad = _gelu_and_grad(u)
        vn, xhat, rs, gv_grad = _sgu_normed(v, lng_ref, lnb_ref)
        vnb = vn.astype(BF16)
        dyv = dy_ref[...].astype(F32)
        for blk in range(tb // SGU_BLOCK):
            rows = slice(blk * SGU_BLOCK, (blk + 1) * SGU_BLOCK)
            for g in range(SGU_GROUPS):
                cols = slice(g * LANES, (g + 1) * LANES)
                vt = vnb[rows, cols]
                mixed = _dot(wm_ref[g], vt) + bias_ref[g]
                dyt = dyv[rows, cols]
                du_ref[rows, cols] = (dyt * mixed * gu_grad[rows, cols]).astype(BF16)
                dmix = dyt * gu[rows, cols]
                dmixb = dmix.astype(BF16)
                dvn_scr[rows, cols] = _dot(wmt_ref[g], dmixb)
                dws_ref[g] += _dot_nt(dmixb, vt) * mask_ref[...]
                dbs_acc[g] += dmix
        dvn = dvn_scr[...]
        dlng_ref[...] += jnp.sum(dvn * xhat, axis=0, keepdims=True)
        dlnb_ref[...] += jnp.sum(dvn, axis=0, keepdims=True)
        dxh = dvn * lng_ref[...]
        dgv = rs * (
            dxh - jnp.mean(dxh, axis=-1, keepdims=True) - xhat * jnp.mean(dxh * xhat, axis=-1, keepdims=True)
        )
        dv_ref[...] = (dgv * gv_grad).astype(BF16)

        @pl.when(n == nb - 1)
        def _():
            for g in range(SGU_GROUPS):
                dbs_ref[g] = jnp.broadcast_to(jnp.sum(dbs_acc[g], axis=-1, keepdims=True), (SGU_BLOCK, LANES))

    const2 = lambda n: (0, 0)
    const3 = lambda n: (0, 0, 0)
    gmat = pl.BlockSpec((SGU_GROUPS, SGU_BLOCK, SGU_BLOCK), const3)
    vec = pl.BlockSpec((1, D_SGU), const2)
    act = pl.BlockSpec((tb, D_SGU), lambda n: (n, 0))
    return pl.pallas_call(
        body,
        name=name,
        grid=(nb,),
        in_specs=[act] + _sgu_specs(tb) + [vec, vec, gmat, gmat, gmat, pl.BlockSpec((SGU_BLOCK, SGU_BLOCK), const2)],
        out_specs=[act, act, gmat, gmat, vec, vec],
        out_shape=[
            SDS((s, D_SGU), BF16),
            SDS((s, D_SGU), BF16),
            SDS((SGU_GROUPS, SGU_BLOCK, SGU_BLOCK), F32),
            SDS((SGU_GROUPS, SGU_BLOCK, LANES), F32),
            SDS((1, D_SGU), F32),
            SDS((1, D_SGU), F32),
        ],
        scratch_shapes=[pltpu.VMEM((tb, D_SGU), F32), pltpu.VMEM((SGU_GROUPS, SGU_BLOCK, LANES), F32)],
        compiler_params=_params("arbitrary"),
    )(dy, proj, proj, proj, proj, lng, lnb, wm, wmt, bias, mask)


def _gate_specs(tm):
    half = lambda blk: pl.BlockSpec((tm, 512), lambda i: (i, blk))
    return [half(GA_BLK512), half(GA_BLK512 + 1), half(GB_BLK512), half(GB_BLK512 + 1)]


def _merge_fwd(ya_pre, yb_pre, proj, x, w_ba, w_bb, w_out, *, tm, name):
    s = x.shape[0]
    tm = min(tm, s)

    def body(ya_ref, yb_ref, a0, a1, b0, b1, x_ref, wa_ref, wb_ref, wo_ref, x1_ref, yao_ref, ybo_ref):
        ya = _dot(ya_ref[...], wa_ref[...])
        yb = _dot(yb_ref[...], wb_ref[...])
        sa = _sigmoid(jnp.concatenate([a0[...], a1[...]], axis=1).astype(F32))
        sb = _sigmoid(jnp.concatenate([b0[...], b1[...]], axis=1).astype(F32))
        merged = sa * ya + sb * yb
        x1_ref[...] = x_ref[...] + _dot(merged.astype(BF16), wo_ref[...])
        yao_ref[...] = ya.astype(BF16)
        ybo_ref[...] = yb.astype(BF16)

    whole = lambda r: pl.BlockSpec((r, D), lambda i: (0, 0))
    act = pl.BlockSpec((tm, D), lambda i: (i, 0))
    return pl.pallas_call(
        body,
        name=name,
        grid=(s // tm,),
        in_specs=[pl.BlockSpec((tm, D_RNN), lambda i: (i, 0)), act] + _gate_specs(tm) + [act, whole(D_RNN), whole(D_SGU), whole(D)],
        out_specs=[act, act, act],
        out_shape=[SDS((s, D), F32), SDS((s, D), BF16), SDS((s, D), BF16)],
        compiler_params=_params("parallel"),
    )(ya_pre, yb_pre, proj, proj, proj, proj, x, w_ba, w_bb, w_out)


def _merge_bwd(dx1, ya, yb, proj, w_ba, w_bb, w_out, *, tm, name):
    s = dx1.shape[0]
    tm = min(tm, s)

    def body(dx_ref, ya_ref, yb_ref, a0, a1, b0, b1, wa_ref, wb_ref, wo_ref,
             mg_ref, dya_ref, dyb_ref, dga_ref, dgb_ref, dyap_ref, dybp_ref):
        dm = _dot_nt(dx_ref[...].astype(BF16), wo_ref[...])
        ya = ya_ref[...].astype(F32)
        yb = yb_ref[...].astype(F32)
        sa = _sigmoid(jnp.concatenate([a0[...], a1[...]], axis=1).astype(F32))
        sb = _sigmoid(jnp.concatenate([b0[...], b1[...]], axis=1).astype(F32))
        mg_ref[...] = (sa * ya + sb * yb).astype(BF16)
        dya = (dm * sa).astype(BF16)
        dyb = (dm * sb).astype(BF16)
        dya_ref[...] = dya
        dyb_ref[...] = dyb
        dga_ref[...] = (dm * ya * sa * (1.0 - sa)).astype(BF16)
        dgb_ref[...] = (dm * yb * sb * (1.0 - sb)).astype(BF16)
        dyap_ref[...] = _dot_nt(dya, wa_ref[...]).astype(BF16)
        dybp_ref[...] = _dot_nt(dyb, wb_ref[...]).astype(BF16)

    whole = lambda r: pl.BlockSpec((r, D), lambda i: (0, 0))
    act = pl.BlockSpec((tm, D), lambda i: (i, 0))
    act_rnn = pl.BlockSpec((tm, D_RNN), lambda i: (i, 0))
    return pl.pallas_call(
        body,
        name=name,
        grid=(s // tm,),
        in_specs=[act, act, act] + _gate_specs(tm) + [whole(D_RNN), whole(D_SGU), whole(D)],
        out_specs=[act, act, act, act, act, act_rnn, act],
        out_shape=[SDS((s, D), BF16)] * 5 + [SDS((s, D_RNN), BF16), SDS((s, D_SGU), BF16)],
        compiler_params=_params("parallel"),
    )(dx1, ya, yb, proj, proj, proj, proj, w_ba, w_bb, w_out)


def _final_loss(x, g, target, *, tm, name):
    s = x.shape[0]
    tm = min(tm, s)

    def body(x_ref, g_ref, t_ref, dx_ref, dg_ref, loss_ref):
        @pl.when(pl.program_id(0) == 0)
        def _():
            dg_ref[...] = jnp.zeros_like(dg_ref)
            loss_ref[...] = jnp.zeros_like(loss_ref)

        xv = x_ref[...]
        r = lax.rsqrt(jnp.mean(xv * xv, axis=-1, keepdims=True) + EPS)
        xhat = xv * r
        e = xhat * g_ref[...] - t_ref[...]
        loss_ref[...] += 0.5 * jnp.sum(jnp.mean(e * e, axis=-1, keepdims=True), axis=0, keepdims=True)
        dy = e * (1.0 / D)
        dxh = dy * g_ref[...]
        dx_ref[...] = r * (dxh - xhat * jnp.mean(dxh * xhat, axis=-1, keepdims=True))
        dg_ref[...] += jnp.sum(dy * xhat, axis=0, keepdims=True)

    act = pl.BlockSpec((tm, D), lambda i: (i, 0))
    vec = pl.BlockSpec((1, D), lambda i: (0, 0))
    return pl.pallas_call(
        body,
        name=name,
        grid=(s // tm,),
        in_specs=[act, vec, act],
        out_specs=[act, vec, pl.BlockSpec((SUBLANES, LANES), lambda i: (0, 0))],
        out_shape=[SDS((s, D), F32), SDS((1, D), F32), SDS((SUBLANES, LANES), F32)],
        compiler_params=_params("arbitrary"),
    )(x, g, target)


def _adamw_math(w, g, m, v):
    m2 = ADAM_B1 * m + (1.0 - ADAM_B1) * g
    v2 = ADAM_B2 * v + (1.0 - ADAM_B2) * (g * g)
    m_hat = m2 / (1.0 - ADAM_B1**ADAM_STEP)
    v_hat = v2 / (1.0 - ADAM_B2**ADAM_STEP)
    delta = -ADAM_LR * (m_hat / (jnp.sqrt(v_hat) + ADAM_EPS) + ADAM_WD * w)
    return delta, m2, v2


def _adamw(w, g, m, v, *, tr, name):
    r, c = w.shape
    tr = max(t for t in range(SUBLANES, min(tr, r) + 1, SUBLANES) if r % t == 0)

    def body(w_ref, g_ref, m_ref, v_ref, d_ref, mo_ref, vo_ref):
        d_ref[...], mo_ref[...], vo_ref[...] = _adamw_math(w_ref[...], g_ref[...], m_ref[...], v_ref[...])

    blk = pl.BlockSpec((tr, c), lambda i: (i, 0))
    return pl.pallas_call(
        body,
        name=name,
        grid=(r // tr,),
        in_specs=[blk] * 4,
        out_specs=[blk] * 3,
        out_shape=[SDS((r, c), F32)] * 3,
        compiler_params=_params("parallel"),
    )(w, g, m, v)


ANY = pl.BlockSpec(memory_space=pl.ANY)


def _position():
    return lax.axis_index("x"), lax.axis_index("y"), lax.axis_index("c")


def _other_chips(x, y):
    return [(1 - x, y), (x, 1 - y), (1 - x, 1 - y)]


def _all_gather(shards, *, name):
    n = len(shards)
    per = 7

    def body(*refs):
        ins, outs = refs[:n], refs[n : 2 * n]
        send, recv, local = refs[2 * n :]
        x, y, c = _position()
        me, sibling = (x, y, c), (x, y, 1 - c)
        chips = _other_chips(x, y)

        def block(t, px, py, pc):
            return outs[t].at[pl.ds(4 * px + 2 * py + pc, 1)]

        def copy(t, k, blk, to, src=None):
            return pltpu.make_async_remote_copy(
                src_ref=block(t, *blk) if src is None else src,
                dst_ref=block(t, *blk),
                send_sem=send.at[t * per + k],
                recv_sem=recv.at[t * per + k],
                device_id=to,
                device_id_type=MESH,
            )

        mine = [pltpu.make_async_copy(ins[t], block(t, *me), local.at[t]) for t in range(n)]
        for cp in mine:
            cp.start()
        first = []
        for t in range(n):
            first += [copy(t, 1 + j, me, (*chip, c), src=ins[t]) for j, chip in enumerate(chips)]
        for t in range(n):
            first.append(copy(t, 0, me, sibling, src=ins[t]))
        for cp in first:
            cp.start()
        passed = []
        for t in range(n):
            for j, chip in enumerate(chips):
                copy(t, 1 + j, (*chip, c), me).wait_recv()
                cp = copy(t, 4 + j, (*chip, c), sibling)
                cp.start()
                passed.append(cp)
        for t in range(n):
            copy(t, 0, sibling, me).wait_recv()
            for j, chip in enumerate(chips):
                copy(t, 4 + j, (*chip, 1 - c), me).wait_recv()
        for cp in first + passed:
            cp.wait_send()
        for cp in mine:
            cp.wait()

    return pl.pallas_call(
        body,
        name=name,
        in_specs=[ANY] * n,
        out_specs=[ANY] * n,
        out_shape=[SDS((N_DEV,) + sh.shape[1:], sh.dtype) for sh in shards],
        scratch_shapes=[
            pltpu.SemaphoreType.DMA((n * per,)),
            pltpu.SemaphoreType.DMA((n * per,)),
            pltpu.SemaphoreType.DMA((n,)),
        ],
    )(*shards)


def _exchange_sibling(grads, *, name):
    n = len(grads)

    def body(*refs):
        ins, outs = refs[:n], refs[n : 2 * n]
        send, recv = refs[2 * n :]
        x, y, c = _position()
        copies = []
        for t in range(n):
            cp = pltpu.make_async_remote_copy(
                src_ref=ins[t].at[:, pl.ds(1 - c, 1)],
                dst_ref=outs[t],
                send_sem=send.at[t],
                recv_sem=recv.at[t],
                device_id=(x, y, 1 - c),
                device_id_type=MESH,
            )
            cp.start()
            copies.append(cp)
        for cp in copies:
            cp.wait()

    return pl.pallas_call(
        body,
        name=name,
        in_specs=[ANY] * n,
        out_specs=[ANY] * n,
        out_shape=[SDS((4, 1) + g.shape[2:], g.dtype) for g in grads],
        scratch_shapes=[pltpu.SemaphoreType.DMA((n,)), pltpu.SemaphoreType.DMA((n,))],
    )(*grads)


def _exchange_chips(parts, *, name):
    n = len(parts)

    def body(*refs):
        ins, outs = refs[:n], refs[n : 2 * n]
        send, recv = refs[2 * n :]
        x, y, c = _position()
        copies = []
        for t in range(n):
            for k, (px, py) in enumerate(_other_chips(x, y)):
                cp = pltpu.make_async_remote_copy(
                    src_ref=ins[t].at[pl.ds(2 * px + py, 1)],
                    dst_ref=outs[t].at[pl.ds(k, 1)],
                    send_sem=send.at[3 * t + k],
                    recv_sem=recv.at[3 * t + k],
                    device_id=(px, py, c),
                    device_id_type=MESH,
                )
                cp.start()
                copies.append(cp)
        for cp in copies:
            cp.wait()

    return pl.pallas_call(
        body,
        name=name,
        in_specs=[ANY] * n,
        out_specs=[ANY] * n,
        out_shape=[SDS((3,) + p.shape[1:], p.dtype) for p in parts],
        scratch_shapes=[pltpu.SemaphoreType.DMA((3 * n,)), pltpu.SemaphoreType.DMA((3 * n,))],
    )(*parts)


def _sum_with_sibling(grad, got, core, *, name):
    rows = grad.shape[2]

    def body(core_ref, a_ref, b_ref, o_ref):
        o_ref[...] = (a_ref[...].astype(F32) + b_ref[...].astype(F32)).astype(o_ref.dtype)

    return pl.pallas_call(
        body,
        name=name,
        grid_spec=pltpu.PrefetchScalarGridSpec(
            num_scalar_prefetch=1,
            grid=(4,),
            in_specs=[
                pl.BlockSpec((None, None, rows, D), lambda q, core_ref: (q, core_ref[0], 0, 0)),
                pl.BlockSpec((None, None, rows, D), lambda q, core_ref: (q, 0, 0, 0)),
            ],
            out_specs=pl.BlockSpec((None, rows, D), lambda q, core_ref: (q, 0, 0)),
        ),
        out_shape=SDS((4, rows, D), grad.dtype),
        compiler_params=_params("parallel"),
    )(core, grad, got)


def _sum_chips(part, got, chip, *, name):
    rows = part.shape[1]

    def body(chip_ref, a_ref, b_ref, o_ref):
        o_ref[...] = ((a_ref[...].astype(F32) + b_ref[0].astype(F32)) + b_ref[1].astype(F32)) + b_ref[2].astype(F32)

    return pl.pallas_call(
        body,
        name=name,
        grid_spec=pltpu.PrefetchScalarGridSpec(
            num_scalar_prefetch=1,
            grid=(1,),
            in_specs=[
                pl.BlockSpec((None, rows, D), lambda i, chip_ref: (chip_ref[0], 0, 0)),
                pl.BlockSpec((3, rows, D), lambda i, chip_ref: (0, 0, 0)),
            ],
            out_specs=pl.BlockSpec((rows, D), lambda i, chip_ref: (0, 0)),
        ),
        out_shape=SDS((rows, D), F32),
        compiler_params=_params("arbitrary"),
    )(chip, part, got)


def _all_reduce_small(pack, *, name):
    rows = pack.shape[1]
    relations = [(kx, ky, kc) for kx in (0, 1) for ky in (0, 1) for kc in (0, 1)][1:]

    def body(in_ref, out_ref, landed, send1, recv1, send2, recv2):
        x, y, c = _position()
        mine = 4 * x + 2 * y + c

        def peer(rel):
            kx, ky, kc = rel
            return (1 - x if kx else x, 1 - y if ky else y, 1 - c if kc else c)

        first = []
        for k, rel in enumerate(relations):
            px, py, pc = peer(rel)
            cp = pltpu.make_async_remote_copy(
                src_ref=in_ref.at[4 * px + 2 * py + pc],
                dst_ref=landed.at[k],
                send_sem=send1.at[k],
                recv_sem=recv1.at[k],
                device_id=(px, py, pc),
                device_id_type=MESH,
            )
            cp.start()
            first.append(cp)
        total = in_ref[mine]
        for k, cp in enumerate(first):
            cp.wait_recv()
            total = total + landed[k]
        out_ref[mine] = total
        second = []
        for k, rel in enumerate(relations):
            cp = pltpu.make_async_remote_copy(
                src_ref=out_ref.at[mine],
                dst_ref=out_ref.at[mine],
                send_sem=send2.at[k],
                recv_sem=recv2.at[k],
                device_id=peer(rel),
                device_id_type=MESH,
            )
            cp.start()
            second.append(cp)
        for k, rel in enumerate(relations):
            px, py, pc = peer(rel)
            got = out_ref.at[4 * px + 2 * py + pc]
            pltpu.make_async_remote_copy(
                src_ref=got, dst_ref=got, send_sem=send2.at[k], recv_sem=recv2.at[k], device_id=peer(rel), device_id_type=MESH
            ).wait_recv()
        for cp in first + second:
            cp.wait_send()

    vmem = pl.BlockSpec(memory_space=pltpu.VMEM)
    return pl.pallas_call(
        body,
        name=name,
        in_specs=[vmem],
        out_specs=vmem,
        out_shape=SDS(pack.shape, F32),
        scratch_shapes=[
            pltpu.VMEM((7, rows, D), F32),
            pltpu.SemaphoreType.DMA((7,)),
            pltpu.SemaphoreType.DMA((7,)),
            pltpu.SemaphoreType.DMA((7,)),
            pltpu.SemaphoreType.DMA((7,)),
        ],
        compiler_params=pltpu.CompilerParams(vmem_limit_bytes=VMEM_LIMIT_BYTES),
    )(pack)


def _pack(arrays, rows):
    flat = jnp.concatenate([a.reshape(-1).astype(F32) for a in arrays])
    return jnp.pad(flat, (0, rows * D - flat.shape[0])).reshape(rows, D)


def _unpack(pack, shapes):
    flat = pack.reshape(-1)
    out, off = [], 0
    for sh in shapes:
        size = 1
        for dim in sh:
            size *= dim
        out.append(flat[off : off + size].reshape(sh))
        off += size
    return out


def _block_diag_pairs(w):
    w = w.reshape(N_RNN_TILES, 2, HEAD_DIM, HEAD_DIM)
    z = jnp.zeros_like(w[:, 0])
    top = jnp.concatenate([w[:, 0], z], axis=2)
    bot = jnp.concatenate([z, w[:, 1]], axis=2)
    return jnp.concatenate([top, bot], axis=1)


def _diag_blocks(w2):
    a = w2[:, :HEAD_DIM, :HEAD_DIM]
    b = w2[:, HEAD_DIM:, HEAD_DIM:]
    return jnp.stack([a, b], axis=1).reshape(RNN_HEADS, HEAD_DIM, HEAD_DIM)


BIG = ("w_in", "w_branch_a", "w_branch_b", "w_out", "w_up", "w_down")
TRANSPOSED = ("w_in", "w_up")
SMALL = (
    "norm_mix_g", "conv_w", "conv_b", "lru_w_a", "lru_b_a", "lru_w_x", "lru_b_x", "lru_lambda",
    "sgu_ln_g", "sgu_ln_b", "sgu_w_s", "sgu_b_s", "norm_ffn_g", "final_norm_g",
)
WEIGHTS = (
    "norm_mix_g", "w_in", "conv_w", "conv_b", "lru_w_a", "lru_b_a", "lru_w_x", "lru_b_x", "lru_lambda", "sgu_ln_g",
    "sgu_ln_b", "sgu_w_s", "sgu_b_s", "w_branch_a", "w_branch_b", "w_out", "norm_ffn_g", "w_up", "w_down", "final_norm_g",
)

TM = 512
TN = 512
TK = 512
TC = 512
TB = 256
TR = 256


def _layer_forward(l, x, p, w):
    proj, h = _norm_matmul_nt(x, p["norm_mix_g"], w["w_in"], tm=TM, tn=TN, name=f"in_proj_{l}")
    hseq, ya_pre = _branch_a_fwd(
        proj, p["conv_w"], p["conv_b"], p["wa2"], p["lru_b_a"], p["wx2"], p["lru_b_x"], p["lru_lambda"],
        tc=TC, name=f"branch_a_fwd_{l}",
    )
    yb_pre = _sgu_fwd(proj, p["sgu_ln_g"], p["sgu_ln_b"], p["wm"], p["sgu_bias"], tb=TB, name=f"sgu_fwd_{l}")
    x1, ya, yb = _merge_fwd(ya_pre, yb_pre, proj, x, w["w_branch_a"], w["w_branch_b"], w["w_out"], tm=TM, name=f"merge_fwd_{l}")
    f_pre, h2 = _norm_matmul_nt(x1, p["norm_ffn_g"], w["w_up"], tm=TM, tn=TN, name=f"ffn_up_{l}")
    x2 = _matmul_nn_res(f_pre, w["w_down"], x1, relu2=True, tm=TM, tk=TK, name=f"ffn_down_{l}")
    saved = dict(x=x, h=h, proj=proj, hseq=hseq, ya_pre=ya_pre, yb_pre=yb_pre, ya=ya, yb=yb, x1=x1, h2=h2, f_pre=f_pre)
    return x2, saved


def _layer_backward(l, dx2, sv, p, w):
    df_pre = _matmul_nt_drelu2(dx2, w["w_down"], sv["f_pre"], tm=TM, tn=TN, name=f"ffn_down_bwd_{l}")
    g_down = _matmul_tn(sv["f_pre"], dx2, relu2=True, tka=TK, ts=TM, name=f"grad_w_down_{l}")
    g_up = _matmul_tn(df_pre, sv["h2"], relu2=False, tka=TK, ts=TM, name=f"grad_w_up_{l}")
    dx1, g_norm_ffn = _matmul_nn_rmsnorm_bwd(df_pre, w["w_up"], sv["x1"], p["norm_ffn_g"], dx2, tm=TM, tk=TK, name=f"ffn_up_bwd_{l}")
    merged, dya, dyb, dga, dgb, dya_pre, dyb_pre = _merge_bwd(
        dx1, sv["ya"], sv["yb"], sv["proj"], w["w_branch_a"], w["w_branch_b"], w["w_out"], tm=TM, name=f"merge_bwd_{l}"
    )
    g_out = _matmul_tn(merged, dx1, relu2=False, tka=TK, ts=TM, name=f"grad_w_out_{l}")
    g_ba = _matmul_tn(sv["ya_pre"], dya, relu2=False, tka=256, ts=TM, name=f"grad_w_branch_a_{l}")
    g_bb = _matmul_tn(sv["yb_pre"], dyb, relu2=False, tka=TK, ts=TM, name=f"grad_w_branch_b_{l}")
    du, dv, g_ws, g_bs, g_lng, g_lnb = _sgu_bwd(
        dyb_pre, sv["proj"], p["sgu_ln_g"], p["sgu_ln_b"], p["wm"], p["wmt"], p["sgu_bias"], p["mask"], tb=TB, name=f"sgu_bwd_{l}"
    )
    dxr, dgr, g_cw, g_cb, g_ba_, g_bx, g_lam, g_wa2, g_wx2 = _branch_a_bwd(
        dya_pre, sv["proj"], sv["hseq"], p["conv_w"], p["conv_b"], p["wa2"], p["lru_b_a"], p["wx2"], p["lru_b_x"],
        p["lru_lambda"], p["wa2t"], p["wx2t"], tc=TC, name=f"branch_a_bwd_{l}",
    )
    dproj = jnp.concatenate([dxr, dgr, du, dv, dga, dgb], axis=1)
    g_in = _matmul_tn(dproj, sv["h"], relu2=False, tka=TK, ts=TM, name=f"grad_w_in_{l}")
    dx, g_norm_mix = _matmul_nn_rmsnorm_bwd(dproj, w["w_in"], sv["x"], p["norm_mix_g"], dx1, tm=TM, tk=TK, name=f"in_proj_bwd_{l}")
    big = dict(w_in=g_in, w_branch_a=g_ba, w_branch_b=g_bb, w_out=g_out, w_up=g_up, w_down=g_down)
    small = dict(
        norm_mix_g=g_norm_mix[0], conv_w=g_cw, conv_b=g_cb[0], lru_w_a=_diag_blocks(g_wa2), lru_b_a=g_ba_.reshape(RNN_HEADS, HEAD_DIM),
        lru_w_x=_diag_blocks(g_wx2), lru_b_x=g_bx.reshape(RNN_HEADS, HEAD_DIM), lru_lambda=g_lam[0], sgu_ln_g=g_lng[0],
        sgu_ln_b=g_lnb[0], sgu_w_s=g_ws, sgu_b_s=g_bs[:, :, 0], norm_ffn_g=g_norm_ffn[0],
    )
    return dx, big, small


def _prepare_small(l, given):
    chunk_id = jnp.arange(SGU_BLOCK) // CHUNK
    mask = (chunk_id[:, None] >= chunk_id[None, :]).astype(F32)
    wm = given["sgu_w_s"][l] * mask
    wa2 = _block_diag_pairs(given["lru_w_a"][l])
    wx2 = _block_diag_pairs(given["lru_w_x"][l])
    row = lambda a: a.reshape(1, -1)
    return dict(
        norm_mix_g=row(given["norm_mix_g"][l]),
        norm_ffn_g=row(given["norm_ffn_g"][l]),
        conv_w=given["conv_w_full"][l],
        conv_b=row(given["conv_b"][l]),
        wa2=wa2.astype(BF16),
        wx2=wx2.astype(BF16),
        wa2t=jnp.swapaxes(wa2, 1, 2).astype(BF16),
        wx2t=jnp.swapaxes(wx2, 1, 2).astype(BF16),
        lru_b_a=row(given["lru_b_a"][l]),
        lru_b_x=row(given["lru_b_x"][l]),
        lru_lambda=row(given["lru_lambda"][l]),
        sgu_ln_g=row(given["sgu_ln_g"][l]),
        sgu_ln_b=row(given["sgu_ln_b"][l]),
        wm=wm.astype(BF16),
        wmt=jnp.swapaxes(wm, 1, 2).astype(BF16),
        sgu_bias=jnp.broadcast_to(given["sgu_b_s"][l][:, :, None], (SGU_GROUPS, SGU_BLOCK, LANES)),
        mask=mask,
    )


def _step(given):
    x_idx, y_idx, c_idx = _position()
    dev = 4 * x_idx + 2 * y_idx + c_idx
    core = c_idx.astype(jnp.int32).reshape(1)
    chip = (2 * x_idx + y_idx).astype(jnp.int32).reshape(1)

    shards = []
    for l in range(DEPTH):
        for name in BIG:
            wl = given[name][l]
            if name in TRANSPOSED:
                wl = wl.T
            shards.append(wl.astype(BF16)[None])
    gathered = _all_gather(shards, name="all_gather_weights")
    weights = []
    for l in range(DEPTH):
        weights.append({name: gathered[l * len(BIG) + t].reshape(-1, D) for t, name in enumerate(BIG)})

    conv_slot = jnp.zeros((N_DEV, DEPTH * CONV_WIDTH * (D_RNN // N_DEV)), F32)
    conv_mine = given["conv_w"].reshape(1, -1)
    conv_pack = lax.dynamic_update_slice(conv_slot, conv_mine, (dev, 0))
    conv_pack = jnp.pad(conv_pack.reshape(-1), (0, N_DEV * SUBLANES * D - conv_pack.size)).reshape(N_DEV, SUBLANES, D)
    conv_all = _all_reduce_small(conv_pack, name="gather_conv_w").reshape(-1)[: conv_slot.size]
    conv_all = conv_all.reshape(N_DEV, DEPTH, CONV_WIDTH, D_RNN // N_DEV)
    given = dict(given, conv_w_full=jnp.moveaxis(conv_all, 0, 2).reshape(DEPTH, CONV_WIDTH, D_RNN))

    small_params = [_prepare_small(l, given) for l in range(DEPTH)]
    x = given["x"][0]
    saved = []
    for l in range(DEPTH):
        x, sv = _layer_forward(l, x, small_params[l], weights[l])
        saved.append(sv)
    dx, g_final, loss = _final_loss(x, given["final_norm_g"].reshape(1, D), given["loss_target"][0], tm=TM, name="final_loss")
    big_grads, small_grads = [None] * DEPTH, [None] * DEPTH
    for l in reversed(range(DEPTH)):
        dx, big_grads[l], small_grads[l] = _layer_backward(l, dx, saved[l], small_params[l], weights[l])

    full = [big_grads[l][name].reshape(4, 2, -1, D) for l in range(DEPTH) for name in BIG]
    from_sibling = _exchange_sibling(full, name="grads_to_sibling")
    parts = [_sum_with_sibling(g, r, core, name=f"sum_sibling_{t}") for t, (g, r) in enumerate(zip(full, from_sibling))]
    from_chips = _exchange_chips(parts, name="grads_to_chips")
    reduced = [_sum_chips(p, r, chip, name=f"sum_chips_{t}") for t, (p, r) in enumerate(zip(parts, from_chips))]

    small_list = []
    for name in SMALL[:-1]:
        small_list.append(jnp.stack([small_grads[l][name] for l in range(DEPTH)]))
    small_list += [g_final[0], loss[0, :1]]
    small_shapes = [a.shape for a in small_list]
    pack = _pack(small_list, SMALL_ROWS).reshape(N_DEV, SMALL_ROWS_PER_DEV, D)
    summed = _unpack(_all_reduce_small(pack, name="all_reduce_small"), small_shapes)
    loss_total = summed[-1][0]
    grads = dict(zip(SMALL, summed[:-1]))
    cw = grads["conv_w"].reshape(DEPTH, CONV_WIDTH, N_DEV, D_RNN // N_DEV)
    grads["conv_w"] = lax.dynamic_index_in_dim(cw, dev, axis=2, keepdims=False)
    for t, name in enumerate(BIG):
        per_layer = []
        for l in range(DEPTH):
            g = reduced[l * len(BIG) + t]
            per_layer.append(g.T if name in TRANSPOSED else g)
        grads[name] = jnp.stack(per_layer)

    delta, new_m, new_v = {}, {}, {}
    for name in BIG:
        shape = given[name].shape
        two_d = lambda a: a.reshape(-1, shape[-1])
        d, m2, v2 = _adamw(
            two_d(given[name]), two_d(grads[name]), two_d(given["m_" + name]), two_d(given["v_" + name]), tr=TR, name=f"adamw_{name}"
        )
        delta[name], new_m[name], new_v[name] = d.reshape(shape), m2.reshape(shape), v2.reshape(shape)
    shapes = [given[name].shape for name in SMALL]
    rows = SMALL_ROWS
    packs = [_pack([src[name] for name in SMALL], rows) for src in (
        {n: given[n] for n in SMALL}, grads, {n: given["m_" + n] for n in SMALL}, {n: given["v_" + n] for n in SMALL}
    )]
    d, m2, v2 = _adamw(*packs, tr=TR, name="adamw_small")
    for res, out in ((d, delta), (m2, new_m), (v2, new_v)):
        out.update(zip(SMALL, _unpack(res, shapes)))

    return (
        loss_total, dx[None],
        *[grads[n] for n in WEIGHTS], *[delta[n] for n in WEIGHTS], *[new_m[n] for n in WEIGHTS], *[new_v[n] for n in WEIGHTS],
    )


def kernel(x, norm_mix_g, w_in, conv_w, conv_b, lru_w_a, lru_b_a, lru_w_x, lru_b_x, lru_lambda, sgu_ln_g, sgu_ln_b, sgu_w_s, sgu_b_s, w_branch_a, w_branch_b, w_out, norm_ffn_g, w_up, w_down, final_norm_g, loss_target, m_norm_mix_g, m_w_in, m_conv_w, m_conv_b, m_lru_w_a, m_lru_b_a, m_lru_w_x, m_lru_b_x, m_lru_lambda, m_sgu_ln_g, m_sgu_ln_b, m_sgu_w_s, m_sgu_b_s, m_w_branch_a, m_w_branch_b, m_w_out, m_norm_ffn_g, m_w_up, m_w_down, m_final_norm_g, v_norm_mix_g, v_w_in, v_conv_w, v_conv_b, v_lru_w_a, v_lru_b_a, v_lru_w_x, v_lru_b_x, v_lru_lambda, v_sgu_ln_g, v_sgu_ln_b, v_sgu_w_s, v_sgu_b_s, v_w_branch_a, v_w_branch_b, v_w_out, v_norm_ffn_g, v_w_up, v_w_down, v_final_norm_g):
    return _step(dict(locals()))
```

```python
import jax
import jax.numpy as jnp
from jax import lax
from jax.experimental import pallas as pl
from jax.experimental.pallas import tpu as pltpu

F32 = jnp.float32
BF16 = jnp.bfloat16
SDS = jax.ShapeDtypeStruct
MESH = pl.DeviceIdType.MESH

D = 1024
D_RNN = 1280
D_SGU = 1024
D_FF = 4096
D_IN = 2 * D_RNN + 2 * D_SGU + 2 * D
DEPTH = 2
RNN_HEADS = 20
HEAD_DIM = 64
CONV_WIDTH = 4
LRU_C = 8.0
SGU_GROUPS = 8
SGU_BLOCK = 128
CHUNK = 64
EPS = 1e-6
N_DEV = 8

ADAM_LR = 0.001
ADAM_B1 = 0.9
ADAM_B2 = 0.999
ADAM_EPS = 1e-08
ADAM_WD = 0.01
ADAM_STEP = 10

LANES = 128
SUBLANES = 8
VMEM_LIMIT_BYTES = 56 * 1024 * 1024

N_RNN_TILES = D_RNN // LANES
GRNN_BLK128 = D_RNN // LANES
U_BLK512 = (2 * D_RNN) // 512
V_BLK512 = (2 * D_RNN + D_SGU) // 512
GA_BLK512 = (2 * D_RNN + 2 * D_SGU) // 512
GB_BLK512 = (2 * D_RNN + 2 * D_SGU + D) // 512

SMALL_ROWS_PER_DEV = 80
SMALL_ROWS = N_DEV * SMALL_ROWS_PER_DEV


def _params(*sem):
    return pltpu.CompilerParams(dimension_semantics=sem, vmem_limit_bytes=VMEM_LIMIT_BYTES)


def _sigmoid(x):
    return 1.0 / (1.0 + jnp.exp(-x))


_GELU_C = 0.7978845608028654
_GELU_K = 0.044715


def _gelu(x):
    t = jnp.tanh(_GELU_C * (x + _GELU_K * x * x * x))
    return 0.5 * x * (1.0 + t)


def _gelu_and_grad(x):
    t = jnp.tanh(_GELU_C * (x + _GELU_K * x * x * x))
    val = 0.5 * x * (1.0 + t)
    grad = 0.5 * (1.0 + t) + 0.5 * x * (1.0 - t * t) * _GELU_C * (1.0 + 3.0 * _GELU_K * x * x)
    return val, grad


def _one_minus_exp(y):
    series = -(y * (1.0 + y * (0.5 + y * (1.0 / 6.0 + y * (1.0 / 24.0)))))
    return jnp.where(y > -0.03, series, 1.0 - jnp.exp(y))


def _dot(a, b):
    return jnp.dot(a, b, preferred_element_type=F32)


def _dot_nt(a, b):
    return lax.dot_general(a, b, (((1,), (1,)), ((), ())), preferred_element_type=F32)


def _dot_tn(a, b):
    return lax.dot_general(a, b, (((0,), (0,)), ((), ())), preferred_element_type=F32)


def _norm_matmul_nt(x, g, w, *, tm, tn, name):
    s, n = x.shape[0], w.shape[0]
    tm, tn = min(tm, s), min(tn, n)

    def body(x_ref, g_ref, w_ref, o_ref, h_ref):
        @pl.when(pl.program_id(1) == 0)
        def _():
            xv = x_ref[...]
            r = lax.rsqrt(jnp.mean(xv * xv, axis=-1, keepdims=True) + EPS)
            h_ref[...] = (xv * r * g_ref[...]).astype(BF16)

        o_ref[...] = _dot_nt(h_ref[...], w_ref[...]).astype(o_ref.dtype)

    return pl.pallas_call(
        body,
        name=name,
        grid=(s // tm, n // tn),
        in_specs=[
            pl.BlockSpec((tm, D), lambda i, j: (i, 0)),
            pl.BlockSpec((1, D), lambda i, j: (0, 0)),
            pl.BlockSpec((tn, D), lambda i, j: (j, 0)),
        ],
        out_specs=[pl.BlockSpec((tm, tn), lambda i, j: (i, j)), pl.BlockSpec((tm, D), lambda i, j: (i, 0))],
        out_shape=[SDS((s, n), BF16), SDS((s, D), BF16)],
        compiler_params=_params("parallel", "arbitrary"),
    )(x, g, w)


def _matmul_nn_res(a, w, res, *, relu2, tm, name):
    s, k = a.shape
    tm = min(tm, s)

    def body(a_ref, w_ref, r_ref, o_ref):
        av = a_ref[...]
        if relu2:
            t = jnp.maximum(av.astype(F32), 0.0)
            av = (t * t).astype(BF16)
        o_ref[...] = r_ref[...] + _dot(av, w_ref[...])

    return pl.pallas_call(
        body,
        name=name,
        grid=(s // tm,),
        in_specs=[
            pl.BlockSpec((tm, k), lambda i: (i, 0)),
            pl.BlockSpec((k, D), lambda i: (0, 0)),
            pl.BlockSpec((tm, D), lambda i: (i, 0)),
        ],
        out_specs=pl.BlockSpec((tm, D), lambda i: (i, 0)),
        out_shape=SDS((s, D), F32),
        compiler_params=_params("parallel"),
    )(a, w, res)


def _matmul_nt_drelu2(a, w, pre, *, tm, tn, name):
    s, n = a.shape[0], w.shape[0]
    tm, tn = min(tm, s), min(tn, n)

    def body(a_ref, w_ref, p_ref, o_ref):
        d = _dot_nt(a_ref[...], w_ref[...])
        o_ref[...] = (d * (2.0 * jnp.maximum(p_ref[...].astype(F32), 0.0))).astype(o_ref.dtype)

    return pl.pallas_call(
        body,
        name=name,
        grid=(s // tm, n // tn),
        in_specs=[
            pl.BlockSpec((tm, D), lambda i, j: (i, 0)),
            pl.BlockSpec((tn, D), lambda i, j: (j, 0)),
            pl.BlockSpec((tm, tn), lambda i, j: (i, j)),
        ],
        out_specs=pl.BlockSpec((tm, tn), lambda i, j: (i, j)),
        out_shape=SDS((s, n), BF16),
        compiler_params=_params("parallel", "arbitrary"),
    )(a, w, pre)


def _matmul_tn(a_list, b, *, relu2, tka, name):
    s = b.shape[0]
    n = len(a_list)
    nblk = [a.shape[1] // tka for a in a_list]
    starts = [sum(nblk[:p]) for p in range(n)]

    def body(*refs):
        a_refs, b_ref, o_ref = refs[:n], refs[n], refs[n + 1]
        i = pl.program_id(0)
        for p in range(n):

            @pl.when((i >= starts[p]) & (i < starts[p] + nblk[p]))
            def _(p=p):
                av = a_refs[p][...]
                if relu2:
                    t = jnp.maximum(av.astype(F32), 0.0)
                    av = (t * t).astype(BF16)
                o_ref[...] = _dot_tn(av, b_ref[...]).astype(o_ref.dtype)

    def piece_spec(p):
        return pl.BlockSpec((s, tka), lambda i: (0, jnp.clip(i - starts[p], 0, nblk[p] - 1)))

    return pl.pallas_call(
        body,
        name=name,
        grid=(sum(nblk),),
        in_specs=[piece_spec(p) for p in range(n)] + [pl.BlockSpec((s, D), lambda i: (0, 0))],
        out_specs=pl.BlockSpec((tka, D), lambda i: (i, 0)),
        out_shape=SDS((sum(nblk) * tka, D), BF16),
        compiler_params=_params("parallel"),
    )(*a_list, b)


def _matmul_nn_rmsnorm_bwd(a_list, w, x, g, res, *, tm, name):
    s = x.shape[0]
    tm = min(tm, s)
    n = len(a_list)
    widths = [a.shape[1] for a in a_list]
    offs = [sum(widths[:p]) for p in range(n)]
    k = sum(widths)

    def body(*refs):
        a_refs = refs[:n]
        w_ref, x_ref, g_ref, r_ref, dx_ref, dxb_ref, dg_ref = refs[n:]

        @pl.when(pl.program_id(0) == 0)
        def _():
            dg_ref[...] = jnp.zeros_like(dg_ref)

        dh = _dot(a_refs[0][...], w_ref[0 : widths[0], :])
        for p in range(1, n):
            dh += _dot(a_refs[p][...], w_ref[offs[p] : offs[p] + widths[p], :])
        xv = x_ref[...]
        r = lax.rsqrt(jnp.mean(xv * xv, axis=-1, keepdims=True) + EPS)
        xhat = xv * r
        dxh = dh * g_ref[...]
        dx = r_ref[...] + r * (dxh - xhat * jnp.mean(dxh * xhat, axis=-1, keepdims=True))
        dx_ref[...] = dx
        dxb_ref[...] = dx.astype(BF16)
        dg_ref[...] += jnp.sum(dh * xhat, axis=0, keepdims=True)

    act = pl.BlockSpec((tm, D), lambda i: (i, 0))
    vec = pl.BlockSpec((1, D), lambda i: (0, 0))
    return pl.pallas_call(
        body,
        name=name,
        grid=(s // tm,),
        in_specs=[pl.BlockSpec((tm, wd), lambda i: (i, 0)) for wd in widths]
        + [pl.BlockSpec((k, D), lambda i: (0, 0), pipeline_mode=pl.Buffered(1)), act, vec, act],
        out_specs=[act, act, vec],
        out_shape=[SDS((s, D), F32), SDS((s, D), BF16), SDS((1, D), F32)],
        compiler_params=_params("arbitrary"),
    )(*a_list, w, x, g, res)


def _rows_before(ext, k):
    if k == 0:
        return ext[SUBLANES:, :]
    return pltpu.roll(ext, k, 0)[SUBLANES:, :]


def _rows_after(ext, k, n):
    if k == 0:
        return ext[:n, :]
    return pltpu.roll(ext, n + SUBLANES - k, 0)[:n, :]


def _scan_forward(a, b, n):
    row = lax.broadcasted_iota(jnp.int32, a.shape, 0)
    d = 1
    while d < n:
        m = row >= d
        a_s = jnp.where(m, pltpu.roll(a, d, 0), 1.0)
        b_s = jnp.where(m, pltpu.roll(b, d, 0), 0.0)
        b = a * b_s + b
        a = a * a_s
        d *= 2
    return a, b


def _scan_backward(a, b, n):
    row = lax.broadcasted_iota(jnp.int32, a.shape, 0)
    d = 1
    while d < n:
        m = row < n - d
        a_s = jnp.where(m, pltpu.roll(a, n - d, 0), 1.0)
        b_s = jnp.where(m, pltpu.roll(b, n - d, 0), 0.0)
        b = a * b_s + b
        a = a * a_s
        d *= 2
    return b


def _softplus_neg(lam):
    z = -lam
    return jnp.maximum(z, 0.0) + jnp.log1p(jnp.exp(-jnp.abs(z)))


def _conv_and_gates(xc, xprev, cw_ref, cb_ref, wa_ref, ba_ref, wx_ref, bx_ref, lam_ref):
    ext = jnp.concatenate([xprev, xc], axis=0)
    x1, x2, x3 = _rows_before(ext, 1), _rows_before(ext, 2), _rows_before(ext, 3)
    xr = cb_ref[...] + x3 * cw_ref[0:1, :] + x2 * cw_ref[1:2, :] + x1 * cw_ref[2:3, :] + xc * cw_ref[3:4, :]
    xrb = xr.astype(BF16)
    r = _sigmoid(_dot(xrb, wa_ref[...]) + ba_ref[...])
    i = _sigmoid(_dot(xrb, wx_ref[...]) + bx_ref[...])
    sp = _softplus_neg(lam_ref[...])
    log_a = (-LRU_C * r) * sp
    a = jnp.exp(log_a)
    one_minus_a2 = _one_minus_exp(2.0 * log_a)
    return xr, (x1, x2, x3), r, i, a, one_minus_a2


def _branch_a_fwd(proj, cw, cb, wa2, ba, wx2, bx, lam, *, tc, name):
    s = proj.shape[0]
    tc = min(tc, s)

    def body(x_ref, g_ref, cw_ref, cb_ref, wa_ref, ba_ref, wx_ref, bx_ref, lam_ref, h_ref, y_ref, xprev, hlast):
        @pl.when(pl.program_id(1) == 0)
        def _():
            xprev[...] = jnp.zeros_like(xprev)
            hlast[...] = jnp.zeros_like(hlast)

        xc = x_ref[...].astype(F32)
        xr, _, r, i, a, om = _conv_and_gates(xc, xprev[...], cw_ref, cb_ref, wa_ref, ba_ref, wx_ref, bx_ref, lam_ref)
        xprev[...] = xc[tc - SUBLANES :, :]
        u = jnp.sqrt(om) * (i * xr)
        acum, b = _scan_forward(a, u, tc)
        h = b + acum * hlast[SUBLANES - 1 : SUBLANES, :]
        hlast[...] = h[tc - SUBLANES :, :]
        h_ref[...] = h
        y_ref[...] = (h * _gelu(g_ref[...].astype(F32))).astype(BF16)

    tile = lambda j, c: (0, j)
    return pl.pallas_call(
        body,
        name=name,
        grid=(N_RNN_TILES, s // tc),
        in_specs=[
            pl.BlockSpec((tc, LANES), lambda j, c: (c, j)),
            pl.BlockSpec((tc, LANES), lambda j, c: (c, GRNN_BLK128 + j)),
            pl.BlockSpec((CONV_WIDTH, LANES), tile),
            pl.BlockSpec((1, LANES), tile),
            pl.BlockSpec((None, LANES, LANES), lambda j, c: (j, 0, 0)),
            pl.BlockSpec((1, LANES), tile),
            pl.BlockSpec((None, LANES, LANES), lambda j, c: (j, 0, 0)),
            pl.BlockSpec((1, LANES), tile),
            pl.BlockSpec((1, LANES), tile),
        ],
        out_specs=[pl.BlockSpec((tc, LANES), lambda j, c: (c, j)), pl.BlockSpec((tc, LANES), lambda j, c: (c, j))],
        out_shape=[SDS((s, D_RNN), F32), SDS((s, D_RNN), BF16)],
        scratch_shapes=[pltpu.VMEM((SUBLANES, LANES), F32), pltpu.VMEM((SUBLANES, LANES), F32)],
        compiler_params=_params("parallel", "arbitrary"),
    )(proj, proj, cw, cb, wa2, ba, wx2, bx, lam)


def _branch_a_bwd(dy, proj, h, cw, cb, wa2, ba, wx2, bx, lam, wa2t, wx2t, *, tc, name):
    s = proj.shape[0]
    tc = min(tc, s)
    nc = s // tc
    halo16 = tc // 16
    halo8 = tc // SUBLANES

    def body(dy_ref, x_ref, xh_ref, g_ref, h_ref, hh_ref, cw_ref, cb_ref, wa_ref, ba_ref, wx_ref, bx_ref, lam_ref,
             wat_ref, wxt_ref, dx_ref, dg_ref, dcw_ref, dcb_ref, dba_ref, dbx_ref, dlam_ref, dwa_ref, dwx_ref,
             carry, dxr_next):
        cc = pl.program_id(1)
        ct = nc - 1 - cc

        @pl.when(cc == 0)
        def _():
            carry[...] = jnp.zeros_like(carry)
            dxr_next[...] = jnp.zeros_like(dxr_next)
            for ref in (dcw_ref, dcb_ref, dba_ref, dbx_ref, dlam_ref, dwa_ref, dwx_ref):
                ref[...] = jnp.zeros_like(ref)

        xc = x_ref[...].astype(F32)
        xprev = jnp.where(ct > 0, xh_ref[SUBLANES:, :].astype(F32), 0.0)
        xr, (x1, x2, x3), r, i, a, om = _conv_and_gates(
            xc, xprev, cw_ref, cb_ref, wa_ref, ba_ref, wx_ref, bx_ref, lam_ref
        )
        norm = jnp.sqrt(om)
        row = lax.broadcasted_iota(jnp.int32, xc.shape, 0)

        hv = h_ref[...]
        ge, ge_grad = _gelu_and_grad(g_ref[...].astype(F32))
        dyv = dy_ref[...].astype(F32)
        dg_ref[...] = (dyv * hv * ge_grad).astype(dg_ref.dtype)
        dh = dyv * ge

        b = dh + jnp.where(row == tc - 1, carry[0:1, :], 0.0)
        a_next = jnp.where(row < tc - 1, pltpu.roll(a, tc - 1, 0), 0.0)
        gadj = _scan_backward(a_next, b, tc)
        carry[...] = (a * gadj)[:SUBLANES, :]

        hprev_first = jnp.where(ct > 0, hh_ref[SUBLANES - 1 : SUBLANES, :], 0.0)
        hprev = jnp.where(row >= 1, pltpu.roll(hv, 1, 0), hprev_first)
        da = gadj * hprev
        ix = i * xr
        dnorm = gadj * ix
        di = gadj * norm * xr
        dlog_a = da * a - dnorm * (1.0 - om) / norm
        sp = _softplus_neg(lam_ref[...])
        dr = dlog_a * (-LRU_C * sp)
        dsp = jnp.sum(dlog_a * (-LRU_C * r), axis=0, keepdims=True)
        dlam_ref[...] += dsp * (-_sigmoid(-lam_ref[...]))
        dza = dr * r * (1.0 - r)
        dzx = di * i * (1.0 - i)
        dzab, dzxb = dza.astype(BF16), dzx.astype(BF16)
        dxr = gadj * norm * i + _dot(dzab, wat_ref[...]) + _dot(dzxb, wxt_ref[...])
        xrb = xr.astype(BF16)
        dwa_ref[...] += _dot_tn(xrb, dzab)
        dwx_ref[...] += _dot_tn(xrb, dzxb)
        dba_ref[...] += jnp.sum(dza, axis=0, keepdims=True)
        dbx_ref[...] += jnp.sum(dzx, axis=0, keepdims=True)

        ext = jnp.concatenate([dxr, dxr_next[...]], axis=0)
        dx = (
            dxr * cw_ref[3:4, :]
            + _rows_after(ext, 1, tc) * cw_ref[2:3, :]
            + _rows_after(ext, 2, tc) * cw_ref[1:2, :]
            + _rows_after(ext, 3, tc) * cw_ref[0:1, :]
        )
        dxr_next[...] = dxr[:SUBLANES, :]
        dx_ref[...] = dx.astype(dx_ref.dtype)
        dcb_ref[...] += jnp.sum(dxr, axis=0, keepdims=True)
        dcw_ref[3:4, :] += jnp.sum(dxr * xc, axis=0, keepdims=True)
        dcw_ref[2:3, :] += jnp.sum(dxr * x1, axis=0, keepdims=True)
        dcw_ref[1:2, :] += jnp.sum(dxr * x2, axis=0, keepdims=True)
        dcw_ref[0:1, :] += jnp.sum(dxr * x3, axis=0, keepdims=True)

    tile = lambda j, c: (0, j)
    mat = lambda j, c: (j, 0, 0)
    cur = lambda j, c: (nc - 1 - c, j)
    vec = pl.BlockSpec((1, LANES), tile)
    matspec = pl.BlockSpec((None, LANES, LANES), mat)
    return pl.pallas_call(
        body,
        name=name,
        grid=(N_RNN_TILES, nc),
        in_specs=[
            pl.BlockSpec((tc, LANES), cur),
            pl.BlockSpec((tc, LANES), cur),
            pl.BlockSpec((16, LANES), lambda j, c: (jnp.maximum((nc - 1 - c) * halo16 - 1, 0), j)),
            pl.BlockSpec((tc, LANES), lambda j, c: (nc - 1 - c, GRNN_BLK128 + j)),
            pl.BlockSpec((tc, LANES), cur),
            pl.BlockSpec((SUBLANES, LANES), lambda j, c: (jnp.maximum((nc - 1 - c) * halo8 - 1, 0), j)),
            pl.BlockSpec((CONV_WIDTH, LANES), tile),
            vec,
            matspec,
            vec,
            matspec,
            vec,
            vec,
            matspec,
            matspec,
        ],
        out_specs=[
            pl.BlockSpec((tc, LANES), cur),
            pl.BlockSpec((tc, LANES), cur),
            pl.BlockSpec((CONV_WIDTH, LANES), tile),
            vec,
            vec,
            vec,
            vec,
            matspec,
            matspec,
        ],
        out_shape=[
            SDS((s, D_RNN), BF16),
            SDS((s, D_RNN), BF16),
            SDS((CONV_WIDTH, D_RNN), F32),
            SDS((1, D_RNN), F32),
            SDS((1, D_RNN), F32),
            SDS((1, D_RNN), F32),
            SDS((1, D_RNN), F32),
            SDS((N_RNN_TILES, LANES, LANES), F32),
            SDS((N_RNN_TILES, LANES, LANES), F32),
        ],
        scratch_shapes=[pltpu.VMEM((SUBLANES, LANES), F32), pltpu.VMEM((SUBLANES, LANES), F32)],
        compiler_params=_params("parallel", "arbitrary"),
    )(dy, proj, proj, proj, h, h, cw, cb, wa2, ba, wx2, bx, lam, wa2t, wx2t)


def _sgu_specs(tb):
    half = lambda blk: pl.BlockSpec((tb, 512), lambda n: (n, blk))
    return [half(U_BLK512), half(U_BLK512 + 1), half(V_BLK512), half(V_BLK512 + 1)]


def _sgu_normed(v, lng_ref, lnb_ref):
    gv, gv_grad = _gelu_and_grad(v)
    mu = jnp.mean(gv, axis=-1, keepdims=True)
    xc = gv - mu
    rs = lax.rsqrt(jnp.mean(xc * xc, axis=-1, keepdims=True) + EPS)
    xhat = xc * rs
    return xhat * lng_ref[...] + lnb_ref[...], xhat, rs, gv_grad


def _sgu_fwd(proj, lng, lnb, wm, bias, *, tb, name):
    s = proj.shape[0]
    tb = min(tb, s)

    def body(u0_ref, u1_ref, v0_ref, v1_ref, lng_ref, lnb_ref, wm_ref, bias_ref, y_ref):
        u = jnp.concatenate([u0_ref[...], u1_ref[...]], axis=1).astype(F32)
        v = jnp.concatenate([v0_ref[...], v1_ref[...]], axis=1).astype(F32)
        gu = _gelu(u)
        vn, _, _, _ = _sgu_normed(v, lng_ref, lnb_ref)
        vnb = vn.astype(BF16)
        for blk in range(tb // SGU_BLOCK):
            rows = slice(blk * SGU_BLOCK, (blk + 1) * SGU_BLOCK)
            for g in range(SGU_GROUPS):
                cols = slice(g * LANES, (g + 1) * LANES)
                mixed = _dot(wm_ref[g], vnb[rows, cols]) + bias_ref[g]
                y_ref[rows, cols] = (gu[rows, cols] * mixed).astype(BF16)

    const2 = lambda n: (0, 0)
    const3 = lambda n: (0, 0, 0)
    return pl.pallas_call(
        body,
        name=name,
        grid=(s // tb,),
        in_specs=_sgu_specs(tb)
        + [
            pl.BlockSpec((1, D_SGU), const2),
            pl.BlockSpec((1, D_SGU), const2),
            pl.BlockSpec((SGU_GROUPS, SGU_BLOCK, SGU_BLOCK), const3),
            pl.BlockSpec((SGU_GROUPS, SGU_BLOCK, LANES), const3),
        ],
        out_specs=pl.BlockSpec((tb, D_SGU), lambda n: (n, 0)),
        out_shape=SDS((s, D_SGU), BF16),
        compiler_params=_params("parallel"),
    )(proj, proj, proj, proj, lng, lnb, wm, bias)


def _sgu_bwd(dy, proj, lng, lnb, wm, wmt, bias, mask, *, tb, name):
    s = proj.shape[0]
    tb = min(tb, s)
    nb = s // tb

    def body(dy_ref, u0_ref, u1_ref, v0_ref, v1_ref, lng_ref, lnb_ref, wm_ref, wmt_ref, bias_ref, mask_ref,
             du_ref, dv_ref, dws_ref, dbs_ref, dlng_ref, dlnb_ref, dvn_scr, dbs_acc):
        n = pl.program_id(0)

        @pl.when(n == 0)
        def _():
            dbs_acc[...] = jnp.zeros_like(dbs_acc)
            for ref in (dws_ref, dlng_ref, dlnb_ref):
                ref[...] = jnp.zeros_like(ref)

        u = jnp.concatenate([u0_ref[...], u1_ref[...]], axis=1).astype(F32)
        v = jnp.concatenate([v0_ref[...], v1_ref[...]], axis=1).astype(F32)
        gu, gu_grad = _gelu_and_grad(u)
        vn, xhat, rs, gv_grad = _sgu_normed(v, lng_ref, lnb_ref)
        vnb = vn.astype(BF16)
        dyv = dy_ref[...].astype(F32)
        for blk in range(tb // SGU_BLOCK):
            rows = slice(blk * SGU_BLOCK, (blk + 1) * SGU_BLOCK)
            for g in range(SGU_GROUPS):
                cols = slice(g * LANES, (g + 1) * LANES)
                vt = vnb[rows, cols]
                mixed = _dot(wm_ref[g], vt) + bias_ref[g]
                dyt = dyv[rows, cols]
                du_ref[rows, cols] = (dyt * mixed * gu_grad[rows, cols]).astype(BF16)
                dmix = dyt * gu[rows, cols]
                dmixb = dmix.astype(BF16)
                dvn_scr[rows, cols] = _dot(wmt_ref[g], dmixb)
                dws_ref[g] += _dot_nt(dmixb, vt) * mask_ref[...]
                dbs_acc[g] += dmix
        dvn = dvn_scr[...]
        dlng_ref[...] += jnp.sum(dvn * xhat, axis=0, keepdims=True)
        dlnb_ref[...] += jnp.sum(dvn, axis=0, keepdims=True)
        dxh = dvn * lng_ref[...]
        dgv = rs * (
            dxh - jnp.mean(dxh, axis=-1, keepdims=True) - xhat * jnp.mean(dxh * xhat, axis=-1, keepdims=True)
        )
        dv_ref[...] = (dgv * gv_grad).astype(BF16)

        @pl.when(n == nb - 1)
        def _():
            for g in range(SGU_GROUPS):
                dbs_ref[g] = jnp.broadcast_to(jnp.sum(dbs_acc[g], axis=-1, keepdims=True), (SGU_BLOCK, LANES))

    const2 = lambda n: (0, 0)
    const3 = lambda n: (0, 0, 0)
    gmat = pl.BlockSpec((SGU_GROUPS, SGU_BLOCK, SGU_BLOCK), const3)
    vec = pl.BlockSpec((1, D_SGU), const2)
    act = pl.BlockSpec((tb, D_SGU), lambda n: (n, 0))
    return pl.pallas_call(
        body,
        name=name,
        grid=(nb,),
        in_specs=[act] + _sgu_specs(tb) + [vec, vec, gmat, gmat, gmat, pl.BlockSpec((SGU_BLOCK, SGU_BLOCK), const2)],
        out_specs=[act, act, gmat, gmat, vec, vec],
        out_shape=[
            SDS((s, D_SGU), BF16),
            SDS((s, D_SGU), BF16),
            SDS((SGU_GROUPS, SGU_BLOCK, SGU_BLOCK), F32),
            SDS((SGU_GROUPS, SGU_BLOCK, LANES), F32),
            SDS((1, D_SGU), F32),
            SDS((1, D_SGU), F32),
        ],
        scratch_shapes=[pltpu.VMEM((tb, D_SGU), F32), pltpu.VMEM((SGU_GROUPS, SGU_BLOCK, LANES), F32)],
        compiler_params=_params("arbitrary"),
    )(dy, proj, proj, proj, proj, lng, lnb, wm, wmt, bias, mask)


def _gate_specs(tm):
    half = lambda blk: pl.BlockSpec((tm, 512), lambda i: (i, blk))
    return [half(GA_BLK512), half(GA_BLK512 + 1), half(GB_BLK512), half(GB_BLK512 + 1)]


def _merge_fwd(ya_pre, yb_pre, proj, x, w_ba, w_bb, w_out, *, tm, name):
    s = x.shape[0]
    tm = min(tm, s)

    def body(ya_ref, yb_ref, a0, a1, b0, b1, x_ref, wa_ref, wb_ref, wo_ref, x1_ref, yao_ref, ybo_ref):
        ya = _dot(ya_ref[...], wa_ref[...])
        yb = _dot(yb_ref[...], wb_ref[...])
        sa = _sigmoid(jnp.concatenate([a0[...], a1[...]], axis=1).astype(F32))
        sb = _sigmoid(jnp.concatenate([b0[...], b1[...]], axis=1).astype(F32))
        merged = sa * ya + sb * yb
        x1_ref[...] = x_ref[...] + _dot(merged.astype(BF16), wo_ref[...])
        yao_ref[...] = ya.astype(BF16)
        ybo_ref[...] = yb.astype(BF16)

    whole = lambda r: pl.BlockSpec((r, D), lambda i: (0, 0))
    act = pl.BlockSpec((tm, D), lambda i: (i, 0))
    return pl.pallas_call(
        body,
        name=name,
        grid=(s // tm,),
        in_specs=[pl.BlockSpec((tm, D_RNN), lambda i: (i, 0)), act] + _gate_specs(tm) + [act, whole(D_RNN), whole(D_SGU), whole(D)],
        out_specs=[act, act, act],
        out_shape=[SDS((s, D), F32), SDS((s, D), BF16), SDS((s, D), BF16)],
        compiler_params=_params("parallel"),
    )(ya_pre, yb_pre, proj, proj, proj, proj, x, w_ba, w_bb, w_out)


def _merge_bwd(dx1, ya, yb, proj, w_ba, w_bb, w_out, *, tm, name):
    s = dx1.shape[0]
    tm = min(tm, s)

    def body(dx_ref, ya_ref, yb_ref, a0, a1, b0, b1, wa_ref, wb_ref, wo_ref,
             mg_ref, dya_ref, dyb_ref, dga_ref, dgb_ref, dyap_ref, dybp_ref):
        dm = _dot_nt(dx_ref[...], wo_ref[...])
        ya = ya_ref[...].astype(F32)
        yb = yb_ref[...].astype(F32)
        sa = _sigmoid(jnp.concatenate([a0[...], a1[...]], axis=1).astype(F32))
        sb = _sigmoid(jnp.concatenate([b0[...], b1[...]], axis=1).astype(F32))
        mg_ref[...] = (sa * ya + sb * yb).astype(BF16)
        dya = (dm * sa).astype(BF16)
        dyb = (dm * sb).astype(BF16)
        dya_ref[...] = dya
        dyb_ref[...] = dyb
        dga_ref[...] = (dm * ya * sa * (1.0 - sa)).astype(BF16)
        dgb_ref[...] = (dm * yb * sb * (1.0 - sb)).astype(BF16)
        dyap_ref[...] = _dot_nt(dya, wa_ref[...]).astype(BF16)
        dybp_ref[...] = _dot_nt(dyb, wb_ref[...]).astype(BF16)

    whole = lambda r: pl.BlockSpec((r, D), lambda i: (0, 0))
    act = pl.BlockSpec((tm, D), lambda i: (i, 0))
    act_rnn = pl.BlockSpec((tm, D_RNN), lambda i: (i, 0))
    return pl.pallas_call(
        body,
        name=name,
        grid=(s // tm,),
        in_specs=[act, act, act] + _gate_specs(tm) + [whole(D_RNN), whole(D_SGU), whole(D)],
        out_specs=[act, act, act, act, act, act_rnn, act],
        out_shape=[SDS((s, D), BF16)] * 5 + [SDS((s, D_RNN), BF16), SDS((s, D_SGU), BF16)],
        compiler_params=_params("parallel"),
    )(dx1, ya, yb, proj, proj, proj, proj, w_ba, w_bb, w_out)


def _final_loss(x, g, target, *, tm, name):
    s = x.shape[0]
    tm = min(tm, s)

    def body(x_ref, g_ref, t_ref, dx_ref, dxb_ref, dg_ref, loss_ref):
        @pl.when(pl.program_id(0) == 0)
        def _():
            dg_ref[...] = jnp.zeros_like(dg_ref)
            loss_ref[...] = jnp.zeros_like(loss_ref)

        xv = x_ref[...]
        r = lax.rsqrt(jnp.mean(xv * xv, axis=-1, keepdims=True) + EPS)
        xhat = xv * r
        e = xhat * g_ref[...] - t_ref[...]
        loss_ref[...] += 0.5 * jnp.sum(jnp.mean(e * e, axis=-1, keepdims=True), axis=0, keepdims=True)
        dy = e * (1.0 / D)
        dxh = dy * g_ref[...]
        dx = r * (dxh - xhat * jnp.mean(dxh * xhat, axis=-1, keepdims=True))
        dx_ref[...] = dx
        dxb_ref[...] = dx.astype(BF16)
        dg_ref[...] += jnp.sum(dy * xhat, axis=0, keepdims=True)

    act = pl.BlockSpec((tm, D), lambda i: (i, 0))
    vec = pl.BlockSpec((1, D), lambda i: (0, 0))
    return pl.pallas_call(
        body,
        name=name,
        grid=(s // tm,),
        in_specs=[act, vec, act],
        out_specs=[act, act, vec, pl.BlockSpec((SUBLANES, LANES), lambda i: (0, 0))],
        out_shape=[SDS((s, D), F32), SDS((s, D), BF16), SDS((1, D), F32), SDS((SUBLANES, LANES), F32)],
        compiler_params=_params("arbitrary"),
    )(x, g, target)


def _adamw_math(w, g, m, v):
    m2 = ADAM_B1 * m + (1.0 - ADAM_B1) * g
    v2 = ADAM_B2 * v + (1.0 - ADAM_B2) * (g * g)
    m_hat = m2 / (1.0 - ADAM_B1**ADAM_STEP)
    v_hat = v2 / (1.0 - ADAM_B2**ADAM_STEP)
    delta = -ADAM_LR * (m_hat / (jnp.sqrt(v_hat) + ADAM_EPS) + ADAM_WD * w)
    return delta, m2, v2


def _adamw(w, g, m, v, *, tr, name):
    r, c = w.shape
    tr = max(t for t in range(SUBLANES, min(tr, r) + 1, SUBLANES) if r % t == 0)

    def body(w_ref, g_ref, m_ref, v_ref, d_ref, mo_ref, vo_ref):
        d_ref[...], mo_ref[...], vo_ref[...] = _adamw_math(w_ref[...], g_ref[...], m_ref[...], v_ref[...])

    blk = pl.BlockSpec((tr, c), lambda i: (i, 0))
    return pl.pallas_call(
        body,
        name=name,
        grid=(r // tr,),
        in_specs=[blk] * 4,
        out_specs=[blk] * 3,
        out_shape=[SDS((r, c), F32)] * 3,
        compiler_params=_params("parallel"),
    )(w, g, m, v)


ANY = pl.BlockSpec(memory_space=pl.ANY)


def _position():
    return lax.axis_index("x"), lax.axis_index("y"), lax.axis_index("c")


def _other_chips(x, y):
    return [(1 - x, y), (x, 1 - y), (1 - x, 1 - y)]


def _all_gather(shards, *, name):
    n = len(shards)
    per = 7

    def body(*refs):
        ins, outs = refs[:n], refs[n : 2 * n]
        send, recv, local = refs[2 * n :]
        x, y, c = _position()
        me, sibling = (x, y, c), (x, y, 1 - c)
        chips = _other_chips(x, y)

        def block(t, px, py, pc):
            return outs[t].at[pl.ds(4 * px + 2 * py + pc, 1)]

        def copy(t, k, blk, to, src=None):
            return pltpu.make_async_remote_copy(
                src_ref=block(t, *blk) if src is None else src,
                dst_ref=block(t, *blk),
                send_sem=send.at[t * per + k],
                recv_sem=recv.at[t * per + k],
                device_id=to,
                device_id_type=MESH,
            )

        mine = [pltpu.make_async_copy(ins[t], block(t, *me), local.at[t]) for t in range(n)]
        for cp in mine:
            cp.start()
        first = []
        for t in range(n):
            first += [copy(t, 1 + j, me, (*chip, c), src=ins[t]) for j, chip in enumerate(chips)]
        for t in range(n):
            first.append(copy(t, 0, me, sibling, src=ins[t]))
        for cp in first:
            cp.start()
        passed = []
        for t in range(n):
            for j, chip in enumerate(chips):
                copy(t, 1 + j, (*chip, c), me).wait_recv()
                cp = copy(t, 4 + j, (*chip, c), sibling)
                cp.start()
                passed.append(cp)
        for t in range(n):
            copy(t, 0, sibling, me).wait_recv()
            for j, chip in enumerate(chips):
                copy(t, 4 + j, (*chip, 1 - c), me).wait_recv()
        for cp in first + passed:
            cp.wait_send()
        for cp in mine:
            cp.wait()

    return pl.pallas_call(
        body,
        name=name,
        in_specs=[ANY] * n,
        out_specs=[ANY] * n,
        out_shape=[SDS((N_DEV,) + sh.shape[1:], sh.dtype) for sh in shards],
        scratch_shapes=[
            pltpu.SemaphoreType.DMA((n * per,)),
            pltpu.SemaphoreType.DMA((n * per,)),
            pltpu.SemaphoreType.DMA((n,)),
        ],
    )(*shards)


def _exchange_sibling(grads, *, name):
    n = len(grads)

    def body(*refs):
        ins, outs = refs[:n], refs[n : 2 * n]
        send, recv = refs[2 * n :]
        x, y, c = _position()
        copies = []
        for t in range(n):
            cp = pltpu.make_async_remote_copy(
                src_ref=ins[t].at[:, pl.ds(1 - c, 1)],
                dst_ref=outs[t],
                send_sem=send.at[t],
                recv_sem=recv.at[t],
                device_id=(x, y, 1 - c),
                device_id_type=MESH,
            )
            cp.start()
            copies.append(cp)
        for cp in copies:
            cp.wait()

    return pl.pallas_call(
        body,
        name=name,
        in_specs=[ANY] * n,
        out_specs=[ANY] * n,
        out_shape=[SDS((4, 1) + g.shape[2:], g.dtype) for g in grads],
        scratch_shapes=[pltpu.SemaphoreType.DMA((n,)), pltpu.SemaphoreType.DMA((n,))],
    )(*grads)


def _exchange_chips(parts, *, name):
    n = len(parts)

    def body(*refs):
        ins, outs = refs[:n], refs[n : 2 * n]
        send, recv = refs[2 * n :]
        x, y, c = _position()
        copies = []
        for t in range(n):
            for k, (px, py) in enumerate(_other_chips(x, y)):
                cp = pltpu.make_async_remote_copy(
                    src_ref=ins[t].at[pl.ds(2 * px + py, 1)],
                    dst_ref=outs[t].at[pl.ds(k, 1)],
                    send_sem=send.at[3 * t + k],
                    recv_sem=recv.at[3 * t + k],
                    device_id=(px, py, c),
                    device_id_type=MESH,
                )
                cp.start()
                copies.append(cp)
        for cp in copies:
            cp.wait()

    return pl.pallas_call(
        body,
        name=name,
        in_specs=[ANY] * n,
        out_specs=[ANY] * n,
        out_shape=[SDS((3,) + p.shape[1:], p.dtype) for p in parts],
        scratch_shapes=[pltpu.SemaphoreType.DMA((3 * n,)), pltpu.SemaphoreType.DMA((3 * n,))],
    )(*parts)


def _sum_with_sibling(grad, got, core, *, name):
    rows = grad.shape[2]

    def body(core_ref, a_ref, b_ref, o_ref):
        o_ref[...] = (a_ref[...].astype(F32) + b_ref[...].astype(F32)).astype(o_ref.dtype)

    return pl.pallas_call(
        body,
        name=name,
        grid_spec=pltpu.PrefetchScalarGridSpec(
            num_scalar_prefetch=1,
            grid=(4,),
            in_specs=[
                pl.BlockSpec((None, None, rows, D), lambda q, core_ref: (q, core_ref[0], 0, 0)),
                pl.BlockSpec((None, None, rows, D), lambda q, core_ref: (q, 0, 0, 0)),
            ],
            out_specs=pl.BlockSpec((None, rows, D), lambda q, core_ref: (q, 0, 0)),
        ),
        out_shape=SDS((4, rows, D), grad.dtype),
        compiler_params=_params("parallel"),
    )(core, grad, got)


def _sum_chips(part, got, chip, *, name):
    rows = part.shape[1]

    def body(chip_ref, a_ref, b_ref, o_ref):
        o_ref[...] = ((a_ref[...].astype(F32) + b_ref[0].astype(F32)) + b_ref[1].astype(F32)) + b_ref[2].astype(F32)

    return pl.pallas_call(
        body,
        name=name,
        grid_spec=pltpu.PrefetchScalarGridSpec(
            num_scalar_prefetch=1,
            grid=(1,),
            in_specs=[
                pl.BlockSpec((None, rows, D), lambda i, chip_ref: (chip_ref[0], 0, 0)),
                pl.BlockSpec((3, rows, D), lambda i, chip_ref: (0, 0, 0)),
            ],
            out_specs=pl.BlockSpec((rows, D), lambda i, chip_ref: (0, 0)),
        ),
        out_shape=SDS((rows, D), F32),
        compiler_params=_params("arbitrary"),
    )(chip, part, got)


def _all_reduce_small(pack, *, name):
    rows = pack.shape[1]
    relations = [(kx, ky, kc) for kx in (0, 1) for ky in (0, 1) for kc in (0, 1)][1:]

    def body(in_ref, out_ref, landed, send1, recv1, send2, recv2):
        x, y, c = _position()
        mine = 4 * x + 2 * y + c

        def peer(rel):
            kx, ky, kc = rel
            return (1 - x if kx else x, 1 - y if ky else y, 1 - c if kc else c)

        first = []
        for k, rel in enumerate(relations):
            px, py, pc = peer(rel)
            cp = pltpu.make_async_remote_copy(
                src_ref=in_ref.at[4 * px + 2 * py + pc],
                dst_ref=landed.at[k],
                send_sem=send1.at[k],
                recv_sem=recv1.at[k],
                device_id=(px, py, pc),
                device_id_type=MESH,
            )
            cp.start()
            first.append(cp)
        total = in_ref[mine]
        for k, cp in enumerate(first):
            cp.wait_recv()
            total = total + landed[k]
        out_ref[mine] = total
        second = []
        for k, rel in enumerate(relations):
            cp = pltpu.make_async_remote_copy(
                src_ref=out_ref.at[mine],
                dst_ref=out_ref.at[mine],
                send_sem=send2.at[k],
                recv_sem=recv2.at[k],
                device_id=peer(rel),
                device_id_type=MESH,
            )
            cp.start()
            second.append(cp)
        for k, rel in enumerate(relations):
            px, py, pc = peer(rel)
            got = out_ref.at[4 * px + 2 * py + pc]
            pltpu.make_async_remote_copy(
                src_ref=got, dst_ref=got, send_sem=send2.at[k], recv_sem=recv2.at[k], device_id=peer(rel), device_id_type=MESH
            ).wait_recv()
        for cp in first + second:
            cp.wait_send()

    vmem = pl.BlockSpec(memory_space=pltpu.VMEM)
    return pl.pallas_call(
        body,
        name=name,
        in_specs=[vmem],
        out_specs=vmem,
        out_shape=SDS(pack.shape, F32),
        scratch_shapes=[
            pltpu.VMEM((7, rows, D), F32),
            pltpu.SemaphoreType.DMA((7,)),
            pltpu.SemaphoreType.DMA((7,)),
            pltpu.SemaphoreType.DMA((7,)),
            pltpu.SemaphoreType.DMA((7,)),
        ],
        compiler_params=pltpu.CompilerParams(vmem_limit_bytes=VMEM_LIMIT_BYTES),
    )(pack)


def _pack(arrays, rows):
    flat = jnp.concatenate([a.reshape(-1).astype(F32) for a in arrays])
    return jnp.pad(flat, (0, rows * D - flat.shape[0])).reshape(rows, D)


def _unpack(pack, shapes):
    flat = pack.reshape(-1)
    out, off = [], 0
    for sh in shapes:
        size = 1
        for dim in sh:
            size *= dim
        out.append(flat[off : off + size].reshape(sh))
        off += size
    return out


def _block_diag_pairs(w):
    w = w.reshape(N_RNN_TILES, 2, HEAD_DIM, HEAD_DIM)
    z = jnp.zeros_like(w[:, 0])
    top = jnp.concatenate([w[:, 0], z], axis=2)
    bot = jnp.concatenate([z, w[:, 1]], axis=2)
    return jnp.concatenate([top, bot], axis=1)


def _diag_blocks(w2):
    a = w2[:, :HEAD_DIM, :HEAD_DIM]
    b = w2[:, HEAD_DIM:, HEAD_DIM:]
    return jnp.stack([a, b], axis=1).reshape(RNN_HEADS, HEAD_DIM, HEAD_DIM)


BIG = ("w_in", "w_branch_a", "w_branch_b", "w_out", "w_up", "w_down")
TRANSPOSED = ("w_in", "w_up")
SMALL = (
    "norm_mix_g", "conv_w", "conv_b", "lru_w_a", "lru_b_a", "lru_w_x", "lru_b_x", "lru_lambda",
    "sgu_ln_g", "sgu_ln_b", "sgu_w_s", "sgu_b_s", "norm_ffn_g", "final_norm_g",
)
WEIGHTS = (
    "norm_mix_g", "w_in", "conv_w", "conv_b", "lru_w_a", "lru_b_a", "lru_w_x", "lru_b_x", "lru_lambda", "sgu_ln_g",
    "sgu_ln_b", "sgu_w_s", "sgu_b_s", "w_branch_a", "w_branch_b", "w_out", "norm_ffn_g", "w_up", "w_down", "final_norm_g",
)

TM = 512
TM_NT = 1024
TN_IN = 1664
TN_UP = 2048
TKA = 512
TKA_PIECES = 256
TC = 512
TB = 256
TR = 256


def _layer_forward(l, x, p, w):
    proj, h = _norm_matmul_nt(x, p["norm_mix_g"], w["w_in"], tm=TM_NT, tn=TN_IN, name=f"in_proj_{l}")
    hseq, ya_pre = _branch_a_fwd(
        proj, p["conv_w"], p["conv_b"], p["wa2"], p["lru_b_a"], p["wx2"], p["lru_b_x"], p["lru_lambda"],
        tc=TC, name=f"branch_a_fwd_{l}",
    )
    yb_pre = _sgu_fwd(proj, p["sgu_ln_g"], p["sgu_ln_b"], p["wm"], p["sgu_bias"], tb=TB, name=f"sgu_fwd_{l}")
    x1, ya, yb = _merge_fwd(ya_pre, yb_pre, proj, x, w["w_branch_a"], w["w_branch_b"], w["w_out"], tm=TM, name=f"merge_fwd_{l}")
    f_pre, h2 = _norm_matmul_nt(x1, p["norm_ffn_g"], w["w_up"], tm=TM_NT, tn=TN_UP, name=f"ffn_up_{l}")
    x2 = _matmul_nn_res(f_pre, w["w_down"], x1, relu2=True, tm=TM, name=f"ffn_down_{l}")
    saved = dict(x=x, h=h, proj=proj, hseq=hseq, ya_pre=ya_pre, yb_pre=yb_pre, ya=ya, yb=yb, x1=x1, h2=h2, f_pre=f_pre)
    return x2, saved


def _layer_backward(l, dx2, dx2b, sv, p, w):
    df_pre = _matmul_nt_drelu2(dx2b, w["w_down"], sv["f_pre"], tm=TM_NT, tn=TN_UP, name=f"ffn_down_bwd_{l}")
    g_down = _matmul_tn([sv["f_pre"]], dx2b, relu2=True, tka=TKA, name=f"grad_w_down_{l}")
    g_up = _matmul_tn([df_pre], sv["h2"], relu2=False, tka=TKA, name=f"grad_w_up_{l}")
    dx1, dx1b, g_norm_ffn = _matmul_nn_rmsnorm_bwd([df_pre], w["w_up"], sv["x1"], p["norm_ffn_g"], dx2, tm=TM, name=f"ffn_up_bwd_{l}")
    merged, dya, dyb, dga, dgb, dya_pre, dyb_pre = _merge_bwd(
        dx1b, sv["ya"], sv["yb"], sv["proj"], w["w_branch_a"], w["w_branch_b"], w["w_out"], tm=TM, name=f"merge_bwd_{l}"
    )
    g_out = _matmul_tn([merged], dx1b, relu2=False, tka=TKA, name=f"grad_w_out_{l}")
    g_ba = _matmul_tn([sv["ya_pre"]], dya, relu2=False, tka=TKA_PIECES, name=f"grad_w_branch_a_{l}")
    g_bb = _matmul_tn([sv["yb_pre"]], dyb, relu2=False, tka=TKA, name=f"grad_w_branch_b_{l}")
    du, dv, g_ws, g_bs, g_lng, g_lnb = _sgu_bwd(
        dyb_pre, sv["proj"], p["sgu_ln_g"], p["sgu_ln_b"], p["wm"], p["wmt"], p["sgu_bias"], p["mask"], tb=TB, name=f"sgu_bwd_{l}"
    )
    dxr, dgr, g_cw, g_cb, g_ba_, g_bx, g_lam, g_wa2, g_wx2 = _branch_a_bwd(
        dya_pre, sv["proj"], sv["hseq"], p["conv_w"], p["conv_b"], p["wa2"], p["lru_b_a"], p["wx2"], p["lru_b_x"],
        p["lru_lambda"], p["wa2t"], p["wx2t"], tc=TC, name=f"branch_a_bwd_{l}",
    )
    dproj = [dxr, dgr, du, dv, dga, dgb]
    g_in = _matmul_tn(dproj, sv["h"], relu2=False, tka=TKA_PIECES, name=f"grad_w_in_{l}")
    dx, dxb, g_norm_mix = _matmul_nn_rmsnorm_bwd(dproj, w["w_in"], sv["x"], p["norm_mix_g"], dx1, tm=TM, name=f"in_proj_bwd_{l}")
    big = dict(w_in=g_in, w_branch_a=g_ba, w_branch_b=g_bb, w_out=g_out, w_up=g_up, w_down=g_down)
    small = dict(
        norm_mix_g=g_norm_mix[0], conv_w=g_cw, conv_b=g_cb[0], lru_w_a=_diag_blocks(g_wa2), lru_b_a=g_ba_.reshape(RNN_HEADS, HEAD_DIM),
        lru_w_x=_diag_blocks(g_wx2), lru_b_x=g_bx.reshape(RNN_HEADS, HEAD_DIM), lru_lambda=g_lam[0], sgu_ln_g=g_lng[0],
        sgu_ln_b=g_lnb[0], sgu_w_s=g_ws, sgu_b_s=g_bs[:, :, 0], norm_ffn_g=g_norm_ffn[0],
    )
    return dx, dxb, big, small


def _prepare_small(l, given):
    chunk_id = jnp.arange(SGU_BLOCK) // CHUNK
    mask = (chunk_id[:, None] >= chunk_id[None, :]).astype(F32)
    wm = given["sgu_w_s"][l] * mask
    wa2 = _block_diag_pairs(given["lru_w_a"][l])
    wx2 = _block_diag_pairs(given["lru_w_x"][l])
    row = lambda a: a.reshape(1, -1)
    return dict(
        norm_mix_g=row(given["norm_mix_g"][l]),
        norm_ffn_g=row(given["norm_ffn_g"][l]),
        conv_w=given["conv_w_full"][l],
        conv_b=row(given["conv_b"][l]),
        wa2=wa2.astype(BF16),
        wx2=wx2.astype(BF16),
        wa2t=jnp.swapaxes(wa2, 1, 2).astype(BF16),
        wx2t=jnp.swapaxes(wx2, 1, 2).astype(BF16),
        lru_b_a=row(given["lru_b_a"][l]),
        lru_b_x=row(given["lru_b_x"][l]),
        lru_lambda=row(given["lru_lambda"][l]),
        sgu_ln_g=row(given["sgu_ln_g"][l]),
        sgu_ln_b=row(given["sgu_ln_b"][l]),
        wm=wm.astype(BF16),
        wmt=jnp.swapaxes(wm, 1, 2).astype(BF16),
        sgu_bias=jnp.broadcast_to(given["sgu_b_s"][l][:, :, None], (SGU_GROUPS, SGU_BLOCK, LANES)),
        mask=mask,
    )


def _step(given):
    x_idx, y_idx, c_idx = _position()
    dev = 4 * x_idx + 2 * y_idx + c_idx
    core = c_idx.astype(jnp.int32).reshape(1)
    chip = (2 * x_idx + y_idx).astype(jnp.int32).reshape(1)

    shards = []
    for l in range(DEPTH):
        for name in BIG:
            wl = given[name][l]
            if name in TRANSPOSED:
                wl = wl.T
            shards.append(wl.astype(BF16)[None])
    gathered = _all_gather(shards, name="all_gather_weights")
    weights = []
    for l in range(DEPTH):
        weights.append({name: gathered[l * len(BIG) + t].reshape(-1, D) for t, name in enumerate(BIG)})

    conv_slot = jnp.zeros((N_DEV, DEPTH * CONV_WIDTH * (D_RNN // N_DEV)), F32)
    conv_mine = given["conv_w"].reshape(1, -1)
    conv_pack = lax.dynamic_update_slice(conv_slot, conv_mine, (dev, 0))
    conv_pack = jnp.pad(conv_pack.reshape(-1), (0, N_DEV * SUBLANES * D - conv_pack.size)).reshape(N_DEV, SUBLANES, D)
    conv_all = _all_reduce_small(conv_pack, name="gather_conv_w").reshape(-1)[: conv_slot.size]
    conv_all = conv_all.reshape(N_DEV, DEPTH, CONV_WIDTH, D_RNN // N_DEV)
    given = dict(given, conv_w_full=jnp.moveaxis(conv_all, 0, 2).reshape(DEPTH, CONV_WIDTH, D_RNN))

    small_params = [_prepare_small(l, given) for l in range(DEPTH)]
    x = given["x"][0]
    saved = []
    for l in range(DEPTH):
        x, sv = _layer_forward(l, x, small_params[l], weights[l])
        saved.append(sv)
    dx, dxb, g_final, loss = _final_loss(x, given["final_norm_g"].reshape(1, D), given["loss_target"][0], tm=TM, name="final_loss")
    big_grads, small_grads = [None] * DEPTH, [None] * DEPTH
    for l in reversed(range(DEPTH)):
        dx, dxb, big_grads[l], small_grads[l] = _layer_backward(l, dx, dxb, saved[l], small_params[l], weights[l])

    full = [big_grads[l][name].reshape(4, 2, -1, D) for l in range(DEPTH) for name in BIG]
    from_sibling = _exchange_sibling(full, name="grads_to_sibling")
    parts = [_sum_with_sibling(g, r, core, name=f"sum_sibling_{t}") for t, (g, r) in enumerate(zip(full, from_sibling))]
    from_chips = _exchange_chips(parts, name="grads_to_chips")
    reduced = [_sum_chips(p, r, chip, name=f"sum_chips_{t}") for t, (p, r) in enumerate(zip(parts, from_chips))]

    small_list = []
    for name in SMALL[:-1]:
        small_list.append(jnp.stack([small_grads[l][name] for l in range(DEPTH)]))
    small_list += [g_final[0], loss[0, :1]]
    small_shapes = [a.shape for a in small_list]
    pack = _pack(small_list, SMALL_ROWS).reshape(N_DEV, SMALL_ROWS_PER_DEV, D)
    summed = _unpack(_all_reduce_small(pack, name="all_reduce_small"), small_shapes)
    loss_total = summed[-1][0]
    grads = dict(zip(SMALL, summed[:-1]))
    cw = grads["conv_w"].reshape(DEPTH, CONV_WIDTH, N_DEV, D_RNN // N_DEV)
    grads["conv_w"] = lax.dynamic_index_in_dim(cw, dev, axis=2, keepdims=False)
    for t, name in enumerate(BIG):
        per_layer = []
        for l in range(DEPTH):
            g = reduced[l * len(BIG) + t]
            per_layer.append(g.T if name in TRANSPOSED else g)
        grads[name] = jnp.stack(per_layer)

    delta, new_m, new_v = {}, {}, {}
    for name in BIG:
        shape = given[name].shape
        two_d = lambda a: a.reshape(-1, shape[-1])
        d, m2, v2 = _adamw(
            two_d(given[name]), two_d(grads[name]), two_d(given["m_" + name]), two_d(given["v_" + name]), tr=TR, name=f"adamw_{name}"
        )
        delta[name], new_m[name], new_v[name] = d.reshape(shape), m2.reshape(shape), v2.reshape(shape)
    shapes = [given[name].shape for name in SMALL]
    rows = SMALL_ROWS
    packs = [_pack([src[name] for name in SMALL], rows) for src in (
        {n: given[n] for n in SMALL}, grads, {n: given["m_" + n] for n in SMALL}, {n: given["v_" + n] for n in SMALL}
    )]
    d, m2, v2 = _adamw(*packs, tr=TR, name="adamw_small")
    for res, out in ((d, delta), (m2, new_m), (v2, new_v)):
        out.update(zip(SMALL, _unpack(res, shapes)))

    return (
        loss_total, dx[None],
        *[grads[n] for n in WEIGHTS], *[delta[n] for n in WEIGHTS], *[new_m[n] for n in WEIGHTS], *[new_v[n] for n in WEIGHTS],
    )


def kernel(x, norm_mix_g, w_in, conv_w, conv_b, lru_w_a, lru_b_a, lru_w_x, lru_b_x, lru_lambda, sgu_ln_g, sgu_ln_b, sgu_w_s, sgu_b_s, w_branch_a, w_branch_b, w_out, norm_ffn_g, w_up, w_down, final_norm_g, loss_target, m_norm_mix_g, m_w_in, m_conv_w, m_conv_b, m_lru_w_a, m_lru_b_a, m_lru_w_x, m_lru_b_x, m_lru_lambda, m_sgu_ln_g, m_sgu_ln_b, m_sgu_w_s, m_sgu_b_s, m_w_branch_a, m_w_branch_b, m_w_out, m_norm_ffn_g, m_w_up, m_w_down, m_final_norm_g, v_norm_mix_g, v_w_in, v_conv_w, v_conv_b, v_lru_w_a, v_lru_b_a, v_lru_w_x, v_lru_b_x, v_lru_lambda, v_sgu_ln_g, v_sgu_ln_b, v_sgu_w_s, v_sgu_b_s, v_w_branch_a, v_w_branch_b, v_w_out, v_norm_ffn_g, v_w_up, v_w_down, v_final_norm_g):
    return _step(dict(locals()))
```

```python
import jax
import jax.numpy as jnp
from jax import lax
from jax.experimental import pallas as pl
from jax.experimental.pallas import tpu as pltpu

F32 = jnp.float32
BF16 = jnp.bfloat16
SDS = jax.ShapeDtypeStruct
MESH = pl.DeviceIdType.MESH

D = 1024
D_RNN = 1280
D_SGU = 1024
D_FF = 4096
D_IN = 2 * D_RNN + 2 * D_SGU + 2 * D
DEPTH = 2
RNN_HEADS = 20
HEAD_DIM = 64
CONV_WIDTH = 4
LRU_C = 8.0
SGU_GROUPS = 8
SGU_BLOCK = 128
CHUNK = 64
EPS = 1e-6
N_DEV = 8

ADAM_LR = 0.001
ADAM_B1 = 0.9
ADAM_B2 = 0.999
ADAM_EPS = 1e-08
ADAM_WD = 0.01
ADAM_STEP = 10

LANES = 128
SUBLANES = 8
VMEM_LIMIT_BYTES = 56 * 1024 * 1024

N_RNN_TILES = D_RNN // LANES
GRNN_BLK128 = D_RNN // LANES
U_BLK512 = (2 * D_RNN) // 512
V_BLK512 = (2 * D_RNN + D_SGU) // 512
GA_BLK512 = (2 * D_RNN + 2 * D_SGU) // 512
GB_BLK512 = (2 * D_RNN + 2 * D_SGU + D) // 512

SMALL_ROWS_PER_DEV = 80
SMALL_ROWS = N_DEV * SMALL_ROWS_PER_DEV


def _params(*sem):
    return pltpu.CompilerParams(dimension_semantics=sem, vmem_limit_bytes=VMEM_LIMIT_BYTES)


def _sigmoid(x):
    return 1.0 / (1.0 + jnp.exp(-x))


_GELU_C = 0.7978845608028654
_GELU_K = 0.044715


def _gelu(x):
    t = jnp.tanh(_GELU_C * (x + _GELU_K * x * x * x))
    return 0.5 * x * (1.0 + t)


def _gelu_and_grad(x):
    t = jnp.tanh(_GELU_C * (x + _GELU_K * x * x * x))
    val = 0.5 * x * (1.0 + t)
    grad = 0.5 * (1.0 + t) + 0.5 * x * (1.0 - t * t) * _GELU_C * (1.0 + 3.0 * _GELU_K * x * x)
    return val, grad


def _one_minus_exp(y):
    series = -(y * (1.0 + y * (0.5 + y * (1.0 / 6.0 + y * (1.0 / 24.0)))))
    return jnp.where(y > -0.03, series, 1.0 - jnp.exp(y))


def _dot(a, b):
    return jnp.dot(a, b, preferred_element_type=F32)


def _dot_nt(a, b):
    return lax.dot_general(a, b, (((1,), (1,)), ((), ())), preferred_element_type=F32)


def _dot_tn(a, b):
    return lax.dot_general(a, b, (((0,), (0,)), ((), ())), preferred_element_type=F32)


def _norm_matmul_nt(x, g, w, *, tm, tn, name, comm=None):
    s, n = x.shape[0], w.shape[0]
    tm, tn = min(tm, s), min(tn, n)

    def body(x_ref, g_ref, w_ref, o_ref, h_ref):
        @pl.when(pl.program_id(1) == 0)
        def _():
            xv = x_ref[...]
            r = lax.rsqrt(jnp.mean(xv * xv, axis=-1, keepdims=True) + EPS)
            h_ref[...] = (xv * r * g_ref[...]).astype(BF16)

        o_ref[...] = _dot_nt(h_ref[...], w_ref[...]).astype(o_ref.dtype)

    return _call(
        body,
        (x, g, w),
        name=name,
        grid=(s // tm, n // tn),
        in_specs=[
            pl.BlockSpec((tm, D), lambda i, j: (i, 0)),
            pl.BlockSpec((1, D), lambda i, j: (0, 0)),
            pl.BlockSpec((tn, D), lambda i, j: (j, 0)),
        ],
        out_specs=[pl.BlockSpec((tm, tn), lambda i, j: (i, j)), pl.BlockSpec((tm, D), lambda i, j: (i, 0))],
        out_shape=[SDS((s, n), BF16), SDS((s, D), BF16)],
        semantics=("parallel", "arbitrary"),
        comm=comm,
    )


def _matmul_nn_res(a, w, res, *, relu2, tm, name, comm=None):
    s, k = a.shape
    tm = min(tm, s)

    def body(a_ref, w_ref, r_ref, o_ref):
        av = a_ref[...]
        if relu2:
            t = jnp.maximum(av.astype(F32), 0.0)
            av = (t * t).astype(BF16)
        o_ref[...] = r_ref[...] + _dot(av, w_ref[...])

    return _call(
        body,
        (a, w, res),
        name=name,
        grid=(s // tm,),
        in_specs=[
            pl.BlockSpec((tm, k), lambda i: (i, 0)),
            pl.BlockSpec((k, D), lambda i: (0, 0)),
            pl.BlockSpec((tm, D), lambda i: (i, 0)),
        ],
        out_specs=pl.BlockSpec((tm, D), lambda i: (i, 0)),
        out_shape=SDS((s, D), F32),
        semantics=("parallel",),
        comm=comm,
    )


def _matmul_nt_drelu2(a, w, pre, *, tm, tn, name):
    s, n = a.shape[0], w.shape[0]
    tm, tn = min(tm, s), min(tn, n)

    def body(a_ref, w_ref, p_ref, o_ref):
        d = _dot_nt(a_ref[...], w_ref[...])
        o_ref[...] = (d * (2.0 * jnp.maximum(p_ref[...].astype(F32), 0.0))).astype(o_ref.dtype)

    return pl.pallas_call(
        body,
        name=name,
        grid=(s // tm, n // tn),
        in_specs=[
            pl.BlockSpec((tm, D), lambda i, j: (i, 0)),
            pl.BlockSpec((tn, D), lambda i, j: (j, 0)),
            pl.BlockSpec((tm, tn), lambda i, j: (i, j)),
        ],
        out_specs=pl.BlockSpec((tm, tn), lambda i, j: (i, j)),
        out_shape=SDS((s, n), BF16),
        compiler_params=_params("parallel", "arbitrary"),
    )(a, w, pre)


def _matmul_tn(a_list, b, *, relu2, tka, name, comm=None):
    s = b.shape[0]
    n = len(a_list)
    nblk = [a.shape[1] // tka for a in a_list]
    starts = [sum(nblk[:p]) for p in range(n)]

    def body(*refs):
        a_refs, b_ref, o_ref = refs[:n], refs[n], refs[n + 1]
        i = pl.program_id(0)
        for p in range(n):

            @pl.when((i >= starts[p]) & (i < starts[p] + nblk[p]))
            def _(p=p):
                av = a_refs[p][...]
                if relu2:
                    t = jnp.maximum(av.astype(F32), 0.0)
                    av = (t * t).astype(BF16)
                o_ref[...] = _dot_tn(av, b_ref[...]).astype(o_ref.dtype)

    def piece_spec(p):
        return pl.BlockSpec((s, tka), lambda i: (0, jnp.clip(i - starts[p], 0, nblk[p] - 1)))

    return _call(
        body,
        (*a_list, b),
        name=name,
        grid=(sum(nblk),),
        in_specs=[piece_spec(p) for p in range(n)] + [pl.BlockSpec((s, D), lambda i: (0, 0))],
        out_specs=pl.BlockSpec((tka, D), lambda i: (i, 0)),
        out_shape=SDS((sum(nblk) * tka, D), BF16),
        semantics=("parallel",),
        comm=comm,
    )


def _matmul_nn_rmsnorm_bwd(a_list, w, x, g, res, *, tm, name, comm=None):
    s = x.shape[0]
    tm = min(tm, s)
    n = len(a_list)
    widths = [a.shape[1] for a in a_list]
    offs = [sum(widths[:p]) for p in range(n)]
    k = sum(widths)

    def body(*refs):
        a_refs = refs[:n]
        w_ref, x_ref, g_ref, r_ref, dx_ref, dxb_ref, dg_ref = refs[n:]

        @pl.when(pl.program_id(0) == 0)
        def _():
            dg_ref[...] = jnp.zeros_like(dg_ref)

        dh = _dot(a_refs[0][...], w_ref[0 : widths[0], :])
        for p in range(1, n):
            dh += _dot(a_refs[p][...], w_ref[offs[p] : offs[p] + widths[p], :])
        xv = x_ref[...]
        r = lax.rsqrt(jnp.mean(xv * xv, axis=-1, keepdims=True) + EPS)
        xhat = xv * r
        dxh = dh * g_ref[...]
        dx = r_ref[...] + r * (dxh - xhat * jnp.mean(dxh * xhat, axis=-1, keepdims=True))
        dx_ref[...] = dx
        dxb_ref[...] = dx.astype(BF16)
        dg_ref[...] += jnp.sum(dh * xhat, axis=0, keepdims=True)

    act = pl.BlockSpec((tm, D), lambda i: (i, 0))
    vec = pl.BlockSpec((1, D), lambda i: (0, 0))
    return _call(
        body,
        (*a_list, w, x, g, res),
        name=name,
        grid=(s // tm,),
        in_specs=[pl.BlockSpec((tm, wd), lambda i: (i, 0)) for wd in widths]
        + [pl.BlockSpec((k, D), lambda i: (0, 0), pipeline_mode=pl.Buffered(1)), act, vec, act],
        out_specs=[act, act, vec],
        out_shape=[SDS((s, D), F32), SDS((s, D), BF16), SDS((1, D), F32)],
        semantics=("arbitrary",),
        comm=comm,
    )


def _rows_before(ext, k):
    if k == 0:
        return ext[SUBLANES:, :]
    return pltpu.roll(ext, k, 0)[SUBLANES:, :]


def _rows_after(ext, k, n):
    if k == 0:
        return ext[:n, :]
    return pltpu.roll(ext, n + SUBLANES - k, 0)[:n, :]


def _scan_forward(a, b, n):
    row = lax.broadcasted_iota(jnp.int32, a.shape, 0)
    d = 1
    while d < n:
        m = row >= d
        a_s = jnp.where(m, pltpu.roll(a, d, 0), 1.0)
        b_s = jnp.where(m, pltpu.roll(b, d, 0), 0.0)
        b = a * b_s + b
        a = a * a_s
        d *= 2
    return a, b


def _scan_backward(a, b, n):
    row = lax.broadcasted_iota(jnp.int32, a.shape, 0)
    d = 1
    while d < n:
        m = row < n - d
        a_s = jnp.where(m, pltpu.roll(a, n - d, 0), 1.0)
        b_s = jnp.where(m, pltpu.roll(b, n - d, 0), 0.0)
        b = a * b_s + b
        a = a * a_s
        d *= 2
    return b


def _softplus_neg(lam):
    z = -lam
    return jnp.maximum(z, 0.0) + jnp.log1p(jnp.exp(-jnp.abs(z)))


def _conv_and_gates(xc, xprev, cw_ref, cb_ref, wa_ref, ba_ref, wx_ref, bx_ref, lam_ref):
    ext = jnp.concatenate([xprev, xc], axis=0)
    x1, x2, x3 = _rows_before(ext, 1), _rows_before(ext, 2), _rows_before(ext, 3)
    xr = cb_ref[...] + x3 * cw_ref[0:1, :] + x2 * cw_ref[1:2, :] + x1 * cw_ref[2:3, :] + xc * cw_ref[3:4, :]
    xrb = xr.astype(BF16)
    r = _sigmoid(_dot(xrb, wa_ref[...]) + ba_ref[...])
    i = _sigmoid(_dot(xrb, wx_ref[...]) + bx_ref[...])
    sp = _softplus_neg(lam_ref[...])
    log_a = (-LRU_C * r) * sp
    a = jnp.exp(log_a)
    one_minus_a2 = _one_minus_exp(2.0 * log_a)
    return xr, (x1, x2, x3), r, i, a, one_minus_a2


def _branch_a_fwd(proj, cw, cb, wa2, ba, wx2, bx, lam, *, tc, name, comm=None):
    s = proj.shape[0]
    tc = min(tc, s)

    def body(x_ref, g_ref, cw_ref, cb_ref, wa_ref, ba_ref, wx_ref, bx_ref, lam_ref, h_ref, y_ref, xprev, hlast):
        @pl.when(pl.program_id(1) == 0)
        def _():
            xprev[...] = jnp.zeros_like(xprev)
            hlast[...] = jnp.zeros_like(hlast)

        xc = x_ref[...].astype(F32)
        xr, _, r, i, a, om = _conv_and_gates(xc, xprev[...], cw_ref, cb_ref, wa_ref, ba_ref, wx_ref, bx_ref, lam_ref)
        xprev[...] = xc[tc - SUBLANES :, :]
        u = jnp.sqrt(om) * (i * xr)
        acum, b = _scan_forward(a, u, tc)
        h = b + acum * hlast[SUBLANES - 1 : SUBLANES, :]
        hlast[...] = h[tc - SUBLANES :, :]
        h_ref[...] = h
        y_ref[...] = (h * _gelu(g_ref[...].astype(F32))).astype(BF16)

    tile = lambda j, c: (0, j)
    return _call(
        body,
        (proj, proj, cw, cb, wa2, ba, wx2, bx, lam),
        name=name,
        grid=(N_RNN_TILES, s // tc),
        in_specs=[
            pl.BlockSpec((tc, LANES), lambda j, c: (c, j)),
            pl.BlockSpec((tc, LANES), lambda j, c: (c, GRNN_BLK128 + j)),
            pl.BlockSpec((CONV_WIDTH, LANES), tile),
            pl.BlockSpec((1, LANES), tile),
            pl.BlockSpec((None, LANES, LANES), lambda j, c: (j, 0, 0)),
            pl.BlockSpec((1, LANES), tile),
            pl.BlockSpec((None, LANES, LANES), lambda j, c: (j, 0, 0)),
            pl.BlockSpec((1, LANES), tile),
            pl.BlockSpec((1, LANES), tile),
        ],
        out_specs=[pl.BlockSpec((tc, LANES), lambda j, c: (c, j)), pl.BlockSpec((tc, LANES), lambda j, c: (c, j))],
        out_shape=[SDS((s, D_RNN), F32), SDS((s, D_RNN), BF16)],
        scratch_shapes=[pltpu.VMEM((SUBLANES, LANES), F32), pltpu.VMEM((SUBLANES, LANES), F32)],
        semantics=("parallel", "arbitrary"),
        comm=comm,
    )


def _branch_a_bwd(dy, proj, h, cw, cb, wa2, ba, wx2, bx, lam, wa2t, wx2t, *, tc, name, comm=None):
    s = proj.shape[0]
    tc = min(tc, s)
    nc = s // tc
    halo16 = tc // 16
    halo8 = tc // SUBLANES

    def body(dy_ref, x_ref, xh_ref, g_ref, h_ref, hh_ref, cw_ref, cb_ref, wa_ref, ba_ref, wx_ref, bx_ref, lam_ref,
             wat_ref, wxt_ref, dx_ref, dg_ref, dcw_ref, dcb_ref, dba_ref, dbx_ref, dlam_ref, dwa_ref, dwx_ref,
             carry, dxr_next):
        cc = pl.program_id(1)
        ct = nc - 1 - cc

        @pl.when(cc == 0)
        def _():
            carry[...] = jnp.zeros_like(carry)
            dxr_next[...] = jnp.zeros_like(dxr_next)
            for ref in (dcw_ref, dcb_ref, dba_ref, dbx_ref, dlam_ref, dwa_ref, dwx_ref):
                ref[...] = jnp.zeros_like(ref)

        xc = x_ref[...].astype(F32)
        xprev = jnp.where(ct > 0, xh_ref[SUBLANES:, :].astype(F32), 0.0)
        xr, (x1, x2, x3), r, i, a, om = _conv_and_gates(
            xc, xprev, cw_ref, cb_ref, wa_ref, ba_ref, wx_ref, bx_ref, lam_ref
        )
        norm = jnp.sqrt(om)
        row = lax.broadcasted_iota(jnp.int32, xc.shape, 0)

        hv = h_ref[...]
        ge, ge_grad = _gelu_and_grad(g_ref[...].astype(F32))
        dyv = dy_ref[...].astype(F32)
        dg_ref[...] = (dyv * hv * ge_grad).astype(dg_ref.dtype)
        dh = dyv * ge

        b = dh + jnp.where(row == tc - 1, carry[0:1, :], 0.0)
        a_next = jnp.where(row < tc - 1, pltpu.roll(a, tc - 1, 0), 0.0)
        gadj = _scan_backward(a_next, b, tc)
        carry[...] = (a * gadj)[:SUBLANES, :]

        hprev_first = jnp.where(ct > 0, hh_ref[SUBLANES - 1 : SUBLANES, :], 0.0)
        hprev = jnp.where(row >= 1, pltpu.roll(hv, 1, 0), hprev_first)
        da = gadj * hprev
        ix = i * xr
        dnorm = gadj * ix
        di = gadj * norm * xr
        dlog_a = da * a - dnorm * (1.0 - om) / norm
        sp = _softplus_neg(lam_ref[...])
        dr = dlog_a * (-LRU_C * sp)
        dsp = jnp.sum(dlog_a * (-LRU_C * r), axis=0, keepdims=True)
        dlam_ref[...] += dsp * (-_sigmoid(-lam_ref[...]))
        dza = dr * r * (1.0 - r)
        dzx = di * i * (1.0 - i)
        dzab, dzxb = dza.astype(BF16), dzx.astype(BF16)
        dxr = gadj * norm * i + _dot(dzab, wat_ref[...]) + _dot(dzxb, wxt_ref[...])
        xrb = xr.astype(BF16)
        dwa_ref[...] += _dot_tn(xrb, dzab)
        dwx_ref[...] += _dot_tn(xrb, dzxb)
        dba_ref[...] += jnp.sum(dza, axis=0, keepdims=True)
        dbx_ref[...] += jnp.sum(dzx, axis=0, keepdims=True)

        ext = jnp.concatenate([dxr, dxr_next[...]], axis=0)
        dx = (
            dxr * cw_ref[3:4, :]
            + _rows_after(ext, 1, tc) * cw_ref[2:3, :]
            + _rows_after(ext, 2, tc) * cw_ref[1:2, :]
            + _rows_after(ext, 3, tc) * cw_ref[0:1, :]
        )
        dxr_next[...] = dxr[:SUBLANES, :]
        dx_ref[...] = dx.astype(dx_ref.dtype)
        dcb_ref[...] += jnp.sum(dxr, axis=0, keepdims=True)
        dcw_ref[3:4, :] += jnp.sum(dxr * xc, axis=0, keepdims=True)
        dcw_ref[2:3, :] += jnp.sum(dxr * x1, axis=0, keepdims=True)
        dcw_ref[1:2, :] += jnp.sum(dxr * x2, axis=0, keepdims=True)
        dcw_ref[0:1, :] += jnp.sum(dxr * x3, axis=0, keepdims=True)

    tile = lambda j, c: (0, j)
    mat = lambda j, c: (j, 0, 0)
    cur = lambda j, c: (nc - 1 - c, j)
    vec = pl.BlockSpec((1, LANES), tile)
    matspec = pl.BlockSpec((None, LANES, LANES), mat)
    return _call(
        body,
        (dy, proj, proj, proj, h, h, cw, cb, wa2, ba, wx2, bx, lam, wa2t, wx2t),
        name=name,
        grid=(N_RNN_TILES, nc),
        in_specs=[
            pl.BlockSpec((tc, LANES), cur),
            pl.BlockSpec((tc, LANES), cur),
            pl.BlockSpec((16, LANES), lambda j, c: (jnp.maximum((nc - 1 - c) * halo16 - 1, 0), j)),
            pl.BlockSpec((tc, LANES), lambda j, c: (nc - 1 - c, GRNN_BLK128 + j)),
            pl.BlockSpec((tc, LANES), cur),
            pl.BlockSpec((SUBLANES, LANES), lambda j, c: (jnp.maximum((nc - 1 - c) * halo8 - 1, 0), j)),
            pl.BlockSpec((CONV_WIDTH, LANES), tile),
            vec,
            matspec,
            vec,
            matspec,
            vec,
            vec,
            matspec,
            matspec,
        ],
        out_specs=[
            pl.BlockSpec((tc, LANES), cur),
            pl.BlockSpec((tc, LANES), cur),
            pl.BlockSpec((CONV_WIDTH, LANES), tile),
            vec,
            vec,
            vec,
            vec,
            matspec,
            matspec,
        ],
        out_shape=[
            SDS((s, D_RNN), BF16),
            SDS((s, D_RNN), BF16),
            SDS((CONV_WIDTH, D_RNN), F32),
            SDS((1, D_RNN), F32),
            SDS((1, D_RNN), F32),
            SDS((1, D_RNN), F32),
            SDS((1, D_RNN), F32),
            SDS((N_RNN_TILES, LANES, LANES), F32),
            SDS((N_RNN_TILES, LANES, LANES), F32),
        ],
        scratch_shapes=[pltpu.VMEM((SUBLANES, LANES), F32), pltpu.VMEM((SUBLANES, LANES), F32)],
        semantics=("parallel", "arbitrary"),
        comm=comm,
    )


def _sgu_specs(tb):
    half = lambda blk: pl.BlockSpec((tb, 512), lambda n: (n, blk))
    return [half(U_BLK512), half(U_BLK512 + 1), half(V_BLK512), half(V_BLK512 + 1)]


def _sgu_normed(v, lng_ref, lnb_ref):
    gv, gv_grad = _gelu_and_grad(v)
    mu = jnp.mean(gv, axis=-1, keepdims=True)
    xc = gv - mu
    rs = lax.rsqrt(jnp.mean(xc * xc, axis=-1, keepdims=True) + EPS)
    xhat = xc * rs
    return xhat * lng_ref[...] + lnb_ref[...], xhat, rs, gv_grad


def _sgu_fwd(proj, lng, lnb, wm, bias, *, tb, name):
    s = proj.shape[0]
    tb = min(tb, s)

    def body(u0_ref, u1_ref, v0_ref, v1_ref, lng_ref, lnb_ref, wm_ref, bias_ref, y_ref):
        u = jnp.concatenate([u0_ref[...], u1_ref[...]], axis=1).astype(F32)
        v = jnp.concatenate([v0_ref[...], v1_ref[...]], axis=1).astype(F32)
        gu = _gelu(u)
        vn, _, _, _ = _sgu_normed(v, lng_ref, lnb_ref)
        vnb = vn.astype(BF16)
        for blk in range(tb // SGU_BLOCK):
            rows = slice(blk * SGU_BLOCK, (blk + 1) * SGU_BLOCK)
            for g in range(SGU_GROUPS):
                cols = slice(g * LANES, (g + 1) * LANES)
                mixed = _dot(wm_ref[g], vnb[rows, cols]) + bias_ref[g]
                y_ref[rows, cols] = (gu[rows, cols] * mixed).astype(BF16)

    const2 = lambda n: (0, 0)
    const3 = lambda n: (0, 0, 0)
    return pl.pallas_call(
        body,
        name=name,
        grid=(s // tb,),
        in_specs=_sgu_specs(tb)
        + [
            pl.BlockSpec((1, D_SGU), const2),
            pl.BlockSpec((1, D_SGU), const2),
            pl.BlockSpec((SGU_GROUPS, SGU_BLOCK, SGU_BLOCK), const3),
            pl.BlockSpec((SGU_GROUPS, SGU_BLOCK, LANES), const3),
        ],
        out_specs=pl.BlockSpec((tb, D_SGU), lambda n: (n, 0)),
        out_shape=SDS((s, D_SGU), BF16),
        compiler_params=_params("parallel"),
    )(proj, proj, proj, proj, lng, lnb, wm, bias)


def _sgu_bwd(dy, proj, lng, lnb, wm, wmt, bias, mask, *, tb, name, comm=None):
    s = proj.shape[0]
    tb = min(tb, s)
    nb = s // tb

    def body(dy_ref, u0_ref, u1_ref, v0_ref, v1_ref, lng_ref, lnb_ref, wm_ref, wmt_ref, bias_ref, mask_ref,
             du_ref, dv_ref, dws_ref, dbs_ref, dlng_ref, dlnb_ref, dvn_scr, dbs_acc):
        n = pl.program_id(0)

        @pl.when(n == 0)
        def _():
            dbs_acc[...] = jnp.zeros_like(dbs_acc)
            for ref in (dws_ref, dlng_ref, dlnb_ref):
                ref[...] = jnp.zeros_like(ref)

        u = jnp.concatenate([u0_ref[...], u1_ref[...]], axis=1).astype(F32)
        v = jnp.concatenate([v0_ref[...], v1_ref[...]], axis=1).astype(F32)
        gu, gu_grad = _gelu_and_grad(u)
        vn, xhat, rs, gv_grad = _sgu_normed(v, lng_ref, lnb_ref)
        vnb = vn.astype(BF16)
        dyv = dy_ref[...].astype(F32)
        for blk in range(tb // SGU_BLOCK):
            rows = slice(blk * SGU_BLOCK, (blk + 1) * SGU_BLOCK)
            for g in range(SGU_GROUPS):
                cols = slice(g * LANES, (g + 1) * LANES)
                vt = vnb[rows, cols]
                mixed = _dot(wm_ref[g], vt) + bias_ref[g]
                dyt = dyv[rows, cols]
                du_ref[rows, cols] = (dyt * mixed * gu_grad[rows, cols]).astype(BF16)
                dmix = dyt * gu[rows, cols]
                dmixb = dmix.astype(BF16)
                dvn_scr[rows, cols] = _dot(wmt_ref[g], dmixb)
                dws_ref[g] += _dot_nt(dmixb, vt) * mask_ref[...]
                dbs_acc[g] += dmix
        dvn = dvn_scr[...]
        dlng_ref[...] += jnp.sum(dvn * xhat, axis=0, keepdims=True)
        dlnb_ref[...] += jnp.sum(dvn, axis=0, keepdims=True)
        dxh = dvn * lng_ref[...]
        dgv = rs * (
            dxh - jnp.mean(dxh, axis=-1, keepdims=True) - xhat * jnp.mean(dxh * xhat, axis=-1, keepdims=True)
        )
        dv_ref[...] = (dgv * gv_grad).astype(BF16)

        @pl.when(n == nb - 1)
        def _():
            for g in range(SGU_GROUPS):
                dbs_ref[g] = jnp.broadcast_to(jnp.sum(dbs_acc[g], axis=-1, keepdims=True), (SGU_BLOCK, LANES))

    const2 = lambda n: (0, 0)
    const3 = lambda n: (0, 0, 0)
    gmat = pl.BlockSpec((SGU_GROUPS, SGU_BLOCK, SGU_BLOCK), const3)
    vec = pl.BlockSpec((1, D_SGU), const2)
    act = pl.BlockSpec((tb, D_SGU), lambda n: (n, 0))
    return _call(
        body,
        (dy, proj, proj, proj, proj, lng, lnb, wm, wmt, bias, mask),
        name=name,
        grid=(nb,),
        in_specs=[act] + _sgu_specs(tb) + [vec, vec, gmat, gmat, gmat, pl.BlockSpec((SGU_BLOCK, SGU_BLOCK), const2)],
        out_specs=[act, act, gmat, gmat, vec, vec],
        out_shape=[
            SDS((s, D_SGU), BF16),
            SDS((s, D_SGU), BF16),
            SDS((SGU_GROUPS, SGU_BLOCK, SGU_BLOCK), F32),
            SDS((SGU_GROUPS, SGU_BLOCK, LANES), F32),
            SDS((1, D_SGU), F32),
            SDS((1, D_SGU), F32),
        ],
        scratch_shapes=[pltpu.VMEM((tb, D_SGU), F32), pltpu.VMEM((SGU_GROUPS, SGU_BLOCK, LANES), F32)],
        semantics=("arbitrary",),
        comm=comm,
    )


def _gate_specs(tm):
    half = lambda blk: pl.BlockSpec((tm, 512), lambda i: (i, blk))
    return [half(GA_BLK512), half(GA_BLK512 + 1), half(GB_BLK512), half(GB_BLK512 + 1)]


def _merge_fwd(ya_pre, yb_pre, proj, x, w_ba, w_bb, w_out, *, tm, name):
    s = x.shape[0]
    tm = min(tm, s)

    def body(ya_ref, yb_ref, a0, a1, b0, b1, x_ref, wa_ref, wb_ref, wo_ref, x1_ref, yao_ref, ybo_ref):
        ya = _dot(ya_ref[...], wa_ref[...])
        yb = _dot(yb_ref[...], wb_ref[...])
        sa = _sigmoid(jnp.concatenate([a0[...], a1[...]], axis=1).astype(F32))
        sb = _sigmoid(jnp.concatenate([b0[...], b1[...]], axis=1).astype(F32))
        merged = sa * ya + sb * yb
        x1_ref[...] = x_ref[...] + _dot(merged.astype(BF16), wo_ref[...])
        yao_ref[...] = ya.astype(BF16)
        ybo_ref[...] = yb.astype(BF16)

    whole = lambda r: pl.BlockSpec((r, D), lambda i: (0, 0))
    act = pl.BlockSpec((tm, D), lambda i: (i, 0))
    return pl.pallas_call(
        body,
        name=name,
        grid=(s // tm,),
        in_specs=[pl.BlockSpec((tm, D_RNN), lambda i: (i, 0)), act] + _gate_specs(tm) + [act, whole(D_RNN), whole(D_SGU), whole(D)],
        out_specs=[act, act, act],
        out_shape=[SDS((s, D), F32), SDS((s, D), BF16), SDS((s, D), BF16)],
        compiler_params=_params("parallel"),
    )(ya_pre, yb_pre, proj, proj, proj, proj, x, w_ba, w_bb, w_out)


def _merge_bwd(dx1, ya, yb, proj, w_ba, w_bb, w_out, *, tm, name, comm=None):
    s = dx1.shape[0]
    tm = min(tm, s)

    def body(dx_ref, ya_ref, yb_ref, a0, a1, b0, b1, wa_ref, wb_ref, wo_ref,
             mg_ref, dya_ref, dyb_ref, dga_ref, dgb_ref, dyap_ref, dybp_ref):
        dm = _dot_nt(dx_ref[...], wo_ref[...])
        ya = ya_ref[...].astype(F32)
        yb = yb_ref[...].astype(F32)
        sa = _sigmoid(jnp.concatenate([a0[...], a1[...]], axis=1).astype(F32))
        sb = _sigmoid(jnp.concatenate([b0[...], b1[...]], axis=1).astype(F32))
        mg_ref[...] = (sa * ya + sb * yb).astype(BF16)
        dya = (dm * sa).astype(BF16)
        dyb = (dm * sb).astype(BF16)
        dya_ref[...] = dya
        dyb_ref[...] = dyb
        dga_ref[...] = (dm * ya * sa * (1.0 - sa)).astype(BF16)
        dgb_ref[...] = (dm * yb * sb * (1.0 - sb)).astype(BF16)
        dyap_ref[...] = _dot_nt(dya, wa_ref[...]).astype(BF16)
        dybp_ref[...] = _dot_nt(dyb, wb_ref[...]).astype(BF16)

    whole = lambda r: pl.BlockSpec((r, D), lambda i: (0, 0))
    act = pl.BlockSpec((tm, D), lambda i: (i, 0))
    act_rnn = pl.BlockSpec((tm, D_RNN), lambda i: (i, 0))
    return _call(
        body,
        (dx1, ya, yb, proj, proj, proj, proj, w_ba, w_bb, w_out),
        name=name,
        grid=(s // tm,),
        in_specs=[act, act, act] + _gate_specs(tm) + [whole(D_RNN), whole(D_SGU), whole(D)],
        out_specs=[act, act, act, act, act, act_rnn, act],
        out_shape=[SDS((s, D), BF16)] * 5 + [SDS((s, D_RNN), BF16), SDS((s, D_SGU), BF16)],
        semantics=("parallel",),
        comm=comm,
    )


def _final_loss(x, g, target, *, tm, name):
    s = x.shape[0]
    tm = min(tm, s)

    def body(x_ref, g_ref, t_ref, dx_ref, dxb_ref, dg_ref, loss_ref):
        @pl.when(pl.program_id(0) == 0)
        def _():
            dg_ref[...] = jnp.zeros_like(dg_ref)
            loss_ref[...] = jnp.zeros_like(loss_ref)

        xv = x_ref[...]
        r = lax.rsqrt(jnp.mean(xv * xv, axis=-1, keepdims=True) + EPS)
        xhat = xv * r
        e = xhat * g_ref[...] - t_ref[...]
        loss_ref[...] += 0.5 * jnp.sum(jnp.mean(e * e, axis=-1, keepdims=True), axis=0, keepdims=True)
        dy = e * (1.0 / D)
        dxh = dy * g_ref[...]
        dx = r * (dxh - xhat * jnp.mean(dxh * xhat, axis=-1, keepdims=True))
        dx_ref[...] = dx
        dxb_ref[...] = dx.astype(BF16)
        dg_ref[...] += jnp.sum(dy * xhat, axis=0, keepdims=True)

    act = pl.BlockSpec((tm, D), lambda i: (i, 0))
    vec = pl.BlockSpec((1, D), lambda i: (0, 0))
    return pl.pallas_call(
        body,
        name=name,
        grid=(s // tm,),
        in_specs=[act, vec, act],
        out_specs=[act, act, vec, pl.BlockSpec((SUBLANES, LANES), lambda i: (0, 0))],
        out_shape=[SDS((s, D), F32), SDS((s, D), BF16), SDS((1, D), F32), SDS((SUBLANES, LANES), F32)],
        compiler_params=_params("arbitrary"),
    )(x, g, target)


def _adamw_math(w, g, m, v):
    m2 = ADAM_B1 * m + (1.0 - ADAM_B1) * g
    v2 = ADAM_B2 * v + (1.0 - ADAM_B2) * (g * g)
    m_hat = m2 / (1.0 - ADAM_B1**ADAM_STEP)
    v_hat = v2 / (1.0 - ADAM_B2**ADAM_STEP)
    delta = -ADAM_LR * (m_hat / (jnp.sqrt(v_hat) + ADAM_EPS) + ADAM_WD * w)
    return delta, m2, v2


def _adamw(w, g, m, v, *, tr, name):
    r, c = w.shape
    tr = max(t for t in range(SUBLANES, min(tr, r) + 1, SUBLANES) if r % t == 0)

    def body(w_ref, g_ref, m_ref, v_ref, d_ref, mo_ref, vo_ref):
        d_ref[...], mo_ref[...], vo_ref[...] = _adamw_math(w_ref[...], g_ref[...], m_ref[...], v_ref[...])

    blk = pl.BlockSpec((tr, c), lambda i: (i, 0))
    return pl.pallas_call(
        body,
        name=name,
        grid=(r // tr,),
        in_specs=[blk] * 4,
        out_specs=[blk] * 3,
        out_shape=[SDS((r, c), F32)] * 3,
        compiler_params=_params("parallel"),
    )(w, g, m, v)


ANY = pl.BlockSpec(memory_space=pl.ANY)


def _position():
    return lax.axis_index("x"), lax.axis_index("y"), lax.axis_index("c")


def _other_chips(x, y):
    return [(1 - x, y), (x, 1 - y), (1 - x, 1 - y)]


class _Comm:
    def __init__(self, inputs, out_shapes, sem_counts, start, middle, finish):
        self.inputs, self.out_shapes, self.sem_counts = list(inputs), list(out_shapes), list(sem_counts)
        self.start, self.middle, self.finish = start, middle, finish

    def sem_shapes(self):
        return [pltpu.SemaphoreType.DMA((n,)) for n in self.sem_counts]


def _merge_comms(comms):
    bounds, i, o, s = [], 0, 0, 0
    for cm in comms:
        bounds.append((i, i + len(cm.inputs), o, o + len(cm.out_shapes), s, s + len(cm.sem_counts)))
        i, o, s = bounds[-1][1], bounds[-1][3], bounds[-1][5]

    def phase(which):
        def run(ins, outs, sems):
            for cm, (i0, i1, o0, o1, s0, s1) in zip(comms, bounds):
                getattr(cm, which)(ins[i0:i1], outs[o0:o1], sems[s0:s1])

        return run

    return _Comm(
        [a for cm in comms for a in cm.inputs],
        [a for cm in comms for a in cm.out_shapes],
        [a for cm in comms for a in cm.sem_counts],
        phase("start"),
        phase("middle"),
        phase("finish"),
    )


def _call(body, args, *, semantics, comm=None, **kw):
    if comm is None:
        return pl.pallas_call(body, compiler_params=_params(*semantics), **kw)(*args)
    grid, in_specs, out_specs, out_shape = kw["grid"], kw["in_specs"], kw["out_specs"], kw["out_shape"]
    scratch = list(kw.get("scratch_shapes", ()))
    single = not isinstance(out_shape, (list, tuple))
    core_specs = [out_specs] if single else list(out_specs)
    core_shapes = [out_shape] if single else list(out_shape)
    n_in, n_out, n_scr = len(in_specs), len(core_shapes), len(scratch)
    n_cin, n_cout = len(comm.inputs), len(comm.out_shapes)
    steps = 1
    for g in grid:
        steps *= g
    middle = min((2 * steps) // 3, steps - 1)

    def hosted(*refs):
        core_in, c_in = refs[:n_in], refs[n_in : n_in + n_cin]
        o0 = n_in + n_cin
        core_out, c_out = refs[o0 : o0 + n_out], refs[o0 + n_out : o0 + n_out + n_cout]
        s0 = o0 + n_out + n_cout
        core_scr, sems = refs[s0 : s0 + n_scr], refs[s0 + n_scr :]
        step = pl.program_id(0)
        for d in range(1, len(grid)):
            step = step * grid[d] + pl.program_id(d)

        @pl.when(step == 0)
        def _():
            comm.start(c_in, c_out, sems)

        body(*core_in, *core_out, *core_scr)

        @pl.when(step == middle)
        def _():
            comm.middle(c_in, c_out, sems)

        @pl.when(step == steps - 1)
        def _():
            comm.finish(c_in, c_out, sems)

    outs = pl.pallas_call(
        hosted,
        name=kw["name"],
        grid=grid,
        in_specs=list(in_specs) + [ANY] * n_cin,
        out_specs=core_specs + [ANY] * n_cout,
        out_shape=core_shapes + comm.out_shapes,
        scratch_shapes=scratch + comm.sem_shapes(),
        compiler_params=_params(*(["arbitrary"] * len(grid))),
    )(*args, *comm.inputs)
    return (outs[0] if single else outs[:n_out]), outs[n_out:]


def _comm_only(comm, *, name):
    n_cin, n_cout = len(comm.inputs), len(comm.out_shapes)

    def body(*refs):
        ins, outs, sems = refs[:n_cin], refs[n_cin : n_cin + n_cout], refs[n_cin + n_cout :]
        comm.start(ins, outs, sems)
        comm.middle(ins, outs, sems)
        comm.finish(ins, outs, sems)

    return pl.pallas_call(
        body,
        name=name,
        in_specs=[ANY] * n_cin,
        out_specs=[ANY] * n_cout,
        out_shape=comm.out_shapes,
        scratch_shapes=comm.sem_shapes(),
    )(*comm.inputs)


def _gather_comm(shards):
    n = len(shards)
    per = 7

    def plan(ins, outs, sems):
        send, recv, local = sems
        x, y, c = _position()
        me, sibling = (x, y, c), (x, y, 1 - c)
        chips = _other_chips(x, y)

        def block(t, px, py, pc):
            return outs[t].at[pl.ds(4 * px + 2 * py + pc, 1)]

        def copy(t, k, blk, to, src=None):
            return pltpu.make_async_remote_copy(
                src_ref=block(t, *blk) if src is None else src,
                dst_ref=block(t, *blk),
                send_sem=send.at[t * per + k],
                recv_sem=recv.at[t * per + k],
                device_id=to,
                device_id_type=MESH,
            )

        mine = [pltpu.make_async_copy(ins[t], block(t, *me), local.at[t]) for t in range(n)]
        to_chips = [copy(t, 1 + j, me, (*chip, c), src=ins[t]) for t in range(n) for j, chip in enumerate(chips)]
        to_sibling = [copy(t, 0, me, sibling, src=ins[t]) for t in range(n)]
        from_chips = [copy(t, 1 + j, (*chip, c), me) for t in range(n) for j, chip in enumerate(chips)]
        passed_on = [copy(t, 4 + j, (*chip, c), sibling) for t in range(n) for j, chip in enumerate(chips)]
        from_sibling = [copy(t, 0, sibling, me) for t in range(n)]
        from_sibling += [copy(t, 4 + j, (*chip, 1 - c), me) for t in range(n) for j, chip in enumerate(chips)]
        return mine, to_chips, to_sibling, from_chips, passed_on, from_sibling

    def start(ins, outs, sems):
        mine, to_chips, to_sibling, _, _, _ = plan(ins, outs, sems)
        for cp in mine + to_chips + to_sibling:
            cp.start()

    def middle(ins, outs, sems):
        _, _, _, from_chips, passed_on, _ = plan(ins, outs, sems)
        for arrived, onward in zip(from_chips, passed_on):
            arrived.wait_recv()
            onward.start()

    def finish(ins, outs, sems):
        mine, to_chips, to_sibling, _, passed_on, from_sibling = plan(ins, outs, sems)
        for cp in from_sibling:
            cp.wait_recv()
        for cp in to_chips + to_sibling + passed_on:
            cp.wait_send()
        for cp in mine:
            cp.wait()

    return _Comm(shards, [SDS((N_DEV,) + sh.shape[1:], sh.dtype) for sh in shards], [n * per, n * per, n], start, middle, finish)


def _exchange_comm(arrays, out_shapes, n_copies, copies_of):
    def start(ins, outs, sems):
        for cp in copies_of(ins, outs, *sems):
            cp.start()

    def middle(ins, outs, sems):
        pass

    def finish(ins, outs, sems):
        for cp in copies_of(ins, outs, *sems):
            cp.wait()

    return _Comm(arrays, out_shapes, [n_copies, n_copies], start, middle, finish)


def _sibling_comm(grads):
    def copies_of(ins, outs, send, recv):
        x, y, c = _position()
        return [
            pltpu.make_async_remote_copy(
                src_ref=ins[t].at[:, pl.ds(1 - c, 1)],
                dst_ref=outs[t],
                send_sem=send.at[t],
                recv_sem=recv.at[t],
                device_id=(x, y, 1 - c),
                device_id_type=MESH,
            )
            for t in range(len(ins))
        ]

    return _exchange_comm(grads, [SDS((4, 1) + g.shape[2:], g.dtype) for g in grads], len(grads), copies_of)


def _chips_comm(parts):
    def copies_of(ins, outs, send, recv):
        x, y, c = _position()
        return [
            pltpu.make_async_remote_copy(
                src_ref=ins[t].at[pl.ds(2 * px + py, 1)],
                dst_ref=outs[t].at[pl.ds(k, 1)],
                send_sem=send.at[3 * t + k],
                recv_sem=recv.at[3 * t + k],
                device_id=(px, py, c),
                device_id_type=MESH,
            )
            for t in range(len(ins))
            for k, (px, py) in enumerate(_other_chips(x, y))
        ]

    return _exchange_comm(parts, [SDS((3,) + p.shape[1:], p.dtype) for p in parts], 3 * len(parts), copies_of)


def _sum_with_sibling(grad, got, core, *, name):
    rows = grad.shape[2]

    def body(core_ref, a_ref, b_ref, o_ref):
        o_ref[...] = (a_ref[...].astype(F32) + b_ref[...].astype(F32)).astype(o_ref.dtype)

    return pl.pallas_call(
        body,
        name=name,
        grid_spec=pltpu.PrefetchScalarGridSpec(
            num_scalar_prefetch=1,
            grid=(4,),
            in_specs=[
                pl.BlockSpec((None, None, rows, D), lambda q, core_ref: (q, core_ref[0], 0, 0)),
                pl.BlockSpec((None, None, rows, D), lambda q, core_ref: (q, 0, 0, 0)),
            ],
            out_specs=pl.BlockSpec((None, rows, D), lambda q, core_ref: (q, 0, 0)),
        ),
        out_shape=SDS((4, rows, D), grad.dtype),
        compiler_params=_params("parallel"),
    )(core, grad, got)


def _sum_chips(part, got, chip, *, name):
    rows = part.shape[1]

    def body(chip_ref, a_ref, b_ref, o_ref):
        o_ref[...] = ((a_ref[...].astype(F32) + b_ref[0].astype(F32)) + b_ref[1].astype(F32)) + b_ref[2].astype(F32)

    return pl.pallas_call(
        body,
        name=name,
        grid_spec=pltpu.PrefetchScalarGridSpec(
            num_scalar_prefetch=1,
            grid=(1,),
            in_specs=[
                pl.BlockSpec((None, rows, D), lambda i, chip_ref: (chip_ref[0], 0, 0)),
                pl.BlockSpec((3, rows, D), lambda i, chip_ref: (0, 0, 0)),
            ],
            out_specs=pl.BlockSpec((rows, D), lambda i, chip_ref: (0, 0)),
        ),
        out_shape=SDS((rows, D), F32),
        compiler_params=_params("arbitrary"),
    )(chip, part, got)


def _all_reduce_small(pack, *, name):
    rows = pack.shape[1]
    relations = [(kx, ky, kc) for kx in (0, 1) for ky in (0, 1) for kc in (0, 1)][1:]

    def body(in_ref, out_ref, landed, send1, recv1, send2, recv2):
        x, y, c = _position()
        mine = 4 * x + 2 * y + c

        def peer(rel):
            kx, ky, kc = rel
            return (1 - x if kx else x, 1 - y if ky else y, 1 - c if kc else c)

        first = []
        for k, rel in enumerate(relations):
            px, py, pc = peer(rel)
            cp = pltpu.make_async_remote_copy(
                src_ref=in_ref.at[4 * px + 2 * py + pc],
                dst_ref=landed.at[k],
                send_sem=send1.at[k],
                recv_sem=recv1.at[k],
                device_id=(px, py, pc),
                device_id_type=MESH,
            )
            cp.start()
            first.append(cp)
        total = in_ref[mine]
        for k, cp in enumerate(first):
            cp.wait_recv()
            total = total + landed[k]
        out_ref[mine] = total
        second = []
        for k, rel in enumerate(relations):
            cp = pltpu.make_async_remote_copy(
                src_ref=out_ref.at[mine],
                dst_ref=out_ref.at[mine],
                send_sem=send2.at[k],
                recv_sem=recv2.at[k],
                device_id=peer(rel),
                device_id_type=MESH,
            )
            cp.start()
            second.append(cp)
        for k, rel in enumerate(relations):
            px, py, pc = peer(rel)
            got = out_ref.at[4 * px + 2 * py + pc]
            pltpu.make_async_remote_copy(
                src_ref=got, dst_ref=got, send_sem=send2.at[k], recv_sem=recv2.at[k], device_id=peer(rel), device_id_type=MESH
            ).wait_recv()
        for cp in first + second:
            cp.wait_send()

    vmem = pl.BlockSpec(memory_space=pltpu.VMEM)
    return pl.pallas_call(
        body,
        name=name,
        in_specs=[vmem],
        out_specs=vmem,
        out_shape=SDS(pack.shape, F32),
        scratch_shapes=[
            pltpu.VMEM((7, rows, D), F32),
            pltpu.SemaphoreType.DMA((7,)),
            pltpu.SemaphoreType.DMA((7,)),
            pltpu.SemaphoreType.DMA((7,)),
            pltpu.SemaphoreType.DMA((7,)),
        ],
        compiler_params=pltpu.CompilerParams(vmem_limit_bytes=VMEM_LIMIT_BYTES),
    )(pack)


def _pack(arrays, rows):
    flat = jnp.concatenate([a.reshape(-1).astype(F32) for a in arrays])
    return jnp.pad(flat, (0, rows * D - flat.shape[0])).reshape(rows, D)


def _unpack(pack, shapes):
    flat = pack.reshape(-1)
    out, off = [], 0
    for sh in shapes:
        size = 1
        for dim in sh:
            size *= dim
        out.append(flat[off : off + size].reshape(sh))
        off += size
    return out


def _block_diag_pairs(w):
    w = w.reshape(N_RNN_TILES, 2, HEAD_DIM, HEAD_DIM)
    z = jnp.zeros_like(w[:, 0])
    top = jnp.concatenate([w[:, 0], z], axis=2)
    bot = jnp.concatenate([z, w[:, 1]], axis=2)
    return jnp.concatenate([top, bot], axis=1)


def _diag_blocks(w2):
    a = w2[:, :HEAD_DIM, :HEAD_DIM]
    b = w2[:, HEAD_DIM:, HEAD_DIM:]
    return jnp.stack([a, b], axis=1).reshape(RNN_HEADS, HEAD_DIM, HEAD_DIM)


BIG = ("w_in", "w_branch_a", "w_branch_b", "w_out", "w_up", "w_down")
TRANSPOSED = ("w_in", "w_up")
SMALL = (
    "norm_mix_g", "conv_w", "conv_b", "lru_w_a", "lru_b_a", "lru_w_x", "lru_b_x", "lru_lambda",
    "sgu_ln_g", "sgu_ln_b", "sgu_w_s", "sgu_b_s", "norm_ffn_g", "final_norm_g",
)
WEIGHTS = (
    "norm_mix_g", "w_in", "conv_w", "conv_b", "lru_w_a", "lru_b_a", "lru_w_x", "lru_b_x", "lru_lambda", "sgu_ln_g",
    "sgu_ln_b", "sgu_w_s", "sgu_b_s", "w_branch_a", "w_branch_b", "w_out", "norm_ffn_g", "w_up", "w_down", "final_norm_g",
)

TM = 512
TM_NT = 1024
TN_IN = 1664
TN_UP = 2048
TKA = 512
TKA_PIECES = 256
TC = 512
TB = 256
TR = 256


GATHERS_RIDING = (
    {
        "in_proj": [(0, "w_branch_a"), (0, "w_branch_b"), (0, "w_out"), (0, "w_up")],
        "branch_a_fwd": [(1, "w_in")],
        "ffn_up": [(0, "w_down")],
        "ffn_down": [(1, "w_branch_a"), (1, "w_branch_b"), (1, "w_out")],
    },
    {"in_proj": [(1, "w_up")], "branch_a_fwd": [(1, "w_down")]},
)


def _layer_forward(l, x, p, w, shards):
    def run(key, fn, *args, **kw):
        riding = GATHERS_RIDING[l].get(key, ())
        if not riding:
            return fn(*args, **kw)
        out, got = fn(*args, comm=_gather_comm([shards[l2][n2] for l2, n2 in riding]), **kw)
        for (l2, n2), full in zip(riding, got):
            w[l2][n2] = full.reshape(-1, D)
        return out

    proj, h = run("in_proj", _norm_matmul_nt, x, p["norm_mix_g"], w[l]["w_in"], tm=TM_NT, tn=TN_IN, name=f"in_proj_{l}")
    hseq, ya_pre = run(
        "branch_a_fwd", _branch_a_fwd, proj, p["conv_w"], p["conv_b"], p["wa2"], p["lru_b_a"], p["wx2"], p["lru_b_x"],
        p["lru_lambda"], tc=TC, name=f"branch_a_fwd_{l}",
    )
    yb_pre = _sgu_fwd(proj, p["sgu_ln_g"], p["sgu_ln_b"], p["wm"], p["sgu_bias"], tb=TB, name=f"sgu_fwd_{l}")
    x1, ya, yb = _merge_fwd(
        ya_pre, yb_pre, proj, x, w[l]["w_branch_a"], w[l]["w_branch_b"], w[l]["w_out"], tm=TM, name=f"merge_fwd_{l}"
    )
    f_pre, h2 = run("ffn_up", _norm_matmul_nt, x1, p["norm_ffn_g"], w[l]["w_up"], tm=TM_NT, tn=TN_UP, name=f"ffn_up_{l}")
    x2 = run("ffn_down", _matmul_nn_res, f_pre, w[l]["w_down"], x1, relu2=True, tm=TM, name=f"ffn_down_{l}")
    saved = dict(x=x, h=h, proj=proj, hseq=hseq, ya_pre=ya_pre, yb_pre=yb_pre, ya=ya, yb=yb, x1=x1, h2=h2, f_pre=f_pre)
    return x2, saved


def _layer_backward(l, dx2, dx2b, sv, p, w, core, waiting):
    parts, from_chips = {}, {}

    def by_device(g):
        return g.reshape(4, 2, -1, D)

    def with_sibling(name, g, got):
        parts[name] = _sum_with_sibling(by_device(g), got, core, name=f"sum_sibling_{name}_{l}")

    df_pre = _matmul_nt_drelu2(dx2b, w["w_down"], sv["f_pre"], tm=TM_NT, tn=TN_UP, name=f"ffn_down_bwd_{l}")
    g_down = _matmul_tn([sv["f_pre"]], dx2b, relu2=True, tka=TKA, name=f"grad_w_down_{l}")
    g_up, (got,) = _matmul_tn(
        [df_pre], sv["h2"], relu2=False, tka=TKA, name=f"grad_w_up_{l}", comm=_sibling_comm([by_device(g_down)])
    )
    with_sibling("w_down", g_down, got)
    (dx1, dx1b, g_norm_ffn), (got, from_chips[l, "w_down"]) = _matmul_nn_rmsnorm_bwd(
        [df_pre], w["w_up"], sv["x1"], p["norm_ffn_g"], dx2, tm=TM, name=f"ffn_up_bwd_{l}",
        comm=_merge_comms([_sibling_comm([by_device(g_up)]), _chips_comm([parts["w_down"]])]),
    )
    with_sibling("w_up", g_up, got)
    (merged, dya, dyb, dga, dgb, dya_pre, dyb_pre), (from_chips[l, "w_up"],) = _merge_bwd(
        dx1b, sv["ya"], sv["yb"], sv["proj"], w["w_branch_a"], w["w_branch_b"], w["w_out"], tm=TM, name=f"merge_bwd_{l}",
        comm=_chips_comm([parts["w_up"]]),
    )
    g_out = _matmul_tn([merged], dx1b, relu2=False, tka=TKA, name=f"grad_w_out_{l}")
    g_ba = _matmul_tn([sv["ya_pre"]], dya, relu2=False, tka=TKA_PIECES, name=f"grad_w_branch_a_{l}")
    g_bb = _matmul_tn([sv["yb_pre"]], dyb, relu2=False, tka=TKA, name=f"grad_w_branch_b_{l}")
    branch = (("w_out", g_out), ("w_branch_a", g_ba), ("w_branch_b", g_bb))
    (du, dv, g_ws, g_bs, g_lng, g_lnb), got = _sgu_bwd(
        dyb_pre, sv["proj"], p["sgu_ln_g"], p["sgu_ln_b"], p["wm"], p["wmt"], p["sgu_bias"], p["mask"], tb=TB,
        name=f"sgu_bwd_{l}", comm=_sibling_comm([by_device(g) for _, g in branch]),
    )
    for (name, g), landed in zip(branch, got):
        with_sibling(name, g, landed)
    riding = [((l, name), parts[name]) for name, _ in branch] + list(waiting)
    (dxr, dgr, g_cw, g_cb, g_ba_, g_bx, g_lam, g_wa2, g_wx2), got = _branch_a_bwd(
        dya_pre, sv["proj"], sv["hseq"], p["conv_w"], p["conv_b"], p["wa2"], p["lru_b_a"], p["wx2"], p["lru_b_x"],
        p["lru_lambda"], p["wa2t"], p["wx2t"], tc=TC, name=f"branch_a_bwd_{l}", comm=_chips_comm([part for _, part in riding]),
    )
    for (key, _), landed in zip(riding, got):
        from_chips[key] = landed
    dproj = [dxr, dgr, du, dv, dga, dgb]
    g_in = _matmul_tn(dproj, sv["h"], relu2=False, tka=TKA_PIECES, name=f"grad_w_in_{l}")
    (dx, dxb, g_norm_mix), (got,) = _matmul_nn_rmsnorm_bwd(
        dproj, w["w_in"], sv["x"], p["norm_mix_g"], dx1, tm=TM, name=f"in_proj_bwd_{l}", comm=_sibling_comm([by_device(g_in)])
    )
    with_sibling("w_in", g_in, got)
    small = dict(
        norm_mix_g=g_norm_mix[0], conv_w=g_cw, conv_b=g_cb[0], lru_w_a=_diag_blocks(g_wa2), lru_b_a=g_ba_.reshape(RNN_HEADS, HEAD_DIM),
        lru_w_x=_diag_blocks(g_wx2), lru_b_x=g_bx.reshape(RNN_HEADS, HEAD_DIM), lru_lambda=g_lam[0], sgu_ln_g=g_lng[0],
        sgu_ln_b=g_lnb[0], sgu_w_s=g_ws, sgu_b_s=g_bs[:, :, 0], norm_ffn_g=g_norm_ffn[0],
    )
    return dx, dxb, small, parts, from_chips


def _prepare_small(l, given):
    chunk_id = jnp.arange(SGU_BLOCK) // CHUNK
    mask = (chunk_id[:, None] >= chunk_id[None, :]).astype(F32)
    wm = given["sgu_w_s"][l] * mask
    wa2 = _block_diag_pairs(given["lru_w_a"][l])
    wx2 = _block_diag_pairs(given["lru_w_x"][l])
    row = lambda a: a.reshape(1, -1)
    return dict(
        norm_mix_g=row(given["norm_mix_g"][l]),
        norm_ffn_g=row(given["norm_ffn_g"][l]),
        conv_w=given["conv_w_full"][l],
        conv_b=row(given["conv_b"][l]),
        wa2=wa2.astype(BF16),
        wx2=wx2.astype(BF16),
        wa2t=jnp.swapaxes(wa2, 1, 2).astype(BF16),
        wx2t=jnp.swapaxes(wx2, 1, 2).astype(BF16),
        lru_b_a=row(given["lru_b_a"][l]),
        lru_b_x=row(given["lru_b_x"][l]),
        lru_lambda=row(given["lru_lambda"][l]),
        sgu_ln_g=row(given["sgu_ln_g"][l]),
        sgu_ln_b=row(given["sgu_ln_b"][l]),
        wm=wm.astype(BF16),
        wmt=jnp.swapaxes(wm, 1, 2).astype(BF16),
        sgu_bias=jnp.broadcast_to(given["sgu_b_s"][l][:, :, None], (SGU_GROUPS, SGU_BLOCK, LANES)),
        mask=mask,
    )


def _step(given):
    x_idx, y_idx, c_idx = _position()
    dev = 4 * x_idx + 2 * y_idx + c_idx
    core = c_idx.astype(jnp.int32).reshape(1)
    chip = (2 * x_idx + y_idx).astype(jnp.int32).reshape(1)

    shards = []
    for l in range(DEPTH):
        shards.append({name: (given[name][l].T if name in TRANSPOSED else given[name][l]).astype(BF16)[None] for name in BIG})
    conv_mine = given["conv_w"].reshape(1, DEPTH * CONV_WIDTH, D_RNN // N_DEV)
    w_in_first, conv_all = _comm_only(_gather_comm([shards[0]["w_in"], conv_mine]), name="gather_first")
    weights = [{"w_in": w_in_first.reshape(-1, D)}, {}]
    conv_all = conv_all.reshape(N_DEV, DEPTH, CONV_WIDTH, D_RNN // N_DEV)
    given = dict(given, conv_w_full=jnp.moveaxis(conv_all, 0, 2).reshape(DEPTH, CONV_WIDTH, D_RNN))

    small_params = [_prepare_small(l, given) for l in range(DEPTH)]
    x = given["x"][0]
    saved = []
    for l in range(DEPTH):
        x, sv = _layer_forward(l, x, small_params[l], weights, shards)
        saved.append(sv)
    dx, dxb, g_final, loss = _final_loss(x, given["final_norm_g"].reshape(1, D), given["loss_target"][0], tm=TM, name="final_loss")
    small_grads, parts, from_chips, waiting = [None] * DEPTH, [None] * DEPTH, {}, []
    for l in reversed(range(DEPTH)):
        dx, dxb, small_grads[l], parts[l], got = _layer_backward(
            l, dx, dxb, saved[l], small_params[l], weights[l], core, waiting
        )
        from_chips.update(got)
        waiting = [((l, "w_in"), parts[l]["w_in"])]
    (from_chips[0, "w_in"],) = _comm_only(_chips_comm([parts[0]["w_in"]]), name="grads_to_chips_last")
    reduced = [
        _sum_chips(parts[l][name], from_chips[l, name], chip, name=f"sum_chips_{name}_{l}") for l in range(DEPTH) for name in BIG
    ]

    small_list = []
    for name in SMALL[:-1]:
        small_list.append(jnp.stack([small_grads[l][name] for l in range(DEPTH)]))
    small_list += [g_final[0], loss[0, :1]]
    small_shapes = [a.shape for a in small_list]
    pack = _pack(small_list, SMALL_ROWS).reshape(N_DEV, SMALL_ROWS_PER_DEV, D)
    summed = _unpack(_all_reduce_small(pack, name="all_reduce_small"), small_shapes)
    loss_total = summed[-1][0]
    grads = dict(zip(SMALL, summed[:-1]))
    cw = grads["conv_w"].reshape(DEPTH, CONV_WIDTH, N_DEV, D_RNN // N_DEV)
    grads["conv_w"] = lax.dynamic_index_in_dim(cw, dev, axis=2, keepdims=False)
    for t, name in enumerate(BIG):
        per_layer = []
        for l in range(DEPTH):
            g = reduced[l * len(BIG) + t]
            per_layer.append(g.T if name in TRANSPOSED else g)
        grads[name] = jnp.stack(per_layer)

    delta, new_m, new_v = {}, {}, {}
    for name in BIG:
        shape = given[name].shape
        two_d = lambda a: a.reshape(-1, shape[-1])
        d, m2, v2 = _adamw(
            two_d(given[name]), two_d(grads[name]), two_d(given["m_" + name]), two_d(given["v_" + name]), tr=TR, name=f"adamw_{name}"
        )
        delta[name], new_m[name], new_v[name] = d.reshape(shape), m2.reshape(shape), v2.reshape(shape)
    shapes = [given[name].shape for name in SMALL]
    rows = SMALL_ROWS
    packs = [_pack([src[name] for name in SMALL], rows) for src in (
        {n: given[n] for n in SMALL}, grads, {n: given["m_" + n] for n in SMALL}, {n: given["v_" + n] for n in SMALL}
    )]
    d, m2, v2 = _adamw(*packs, tr=TR, name="adamw_small")
    for res, out in ((d, delta), (m2, new_m), (v2, new_v)):
        out.update(zip(SMALL, _unpack(res, shapes)))

    return (
        loss_total, dx[None],
        *[grads[n] for n in WEIGHTS], *[delta[n] for n in WEIGHTS], *[new_m[n] for n in WEIGHTS], *[new_v[n] for n in WEIGHTS],
    )


def kernel(x, norm_mix_g, w_in, conv_w, conv_b, lru_w_a, lru_b_a, lru_w_x, lru_b_x, lru_lambda, sgu_ln_g, sgu_ln_b, sgu_w_s, sgu_b_s, w_branch_a, w_branch_b, w_out, norm_ffn_g, w_up, w_down, final_norm_g, loss_target, m_norm_mix_g, m_w_in, m_conv_w, m_conv_b, m_lru_w_a, m_lru_b_a, m_lru_w_x, m_lru_b_x, m_lru_lambda, m_sgu_ln_g, m_sgu_ln_b, m_sgu_w_s, m_sgu_b_s, m_w_branch_a, m_w_branch_b, m_w_out, m_norm_ffn_g, m_w_up, m_w_down, m_final_norm_g, v_norm_mix_g, v_w_in, v_conv_w, v_conv_b, v_lru_w_a, v_lru_b_a, v_lru_w_x, v_lru_b_x, v_lru_lambda, v_sgu_ln_g, v_sgu_ln_b, v_sgu_w_s, v_sgu_b_s, v_w_branch_a, v_w_branch_b, v_w_out, v_norm_ffn_g, v_w_up, v_w_down, v_final_norm_g):
    return _step(dict(locals()))
```

```python
import jax
import jax.numpy as jnp
from jax import lax
from jax.experimental import pallas as pl
from jax.experimental.pallas import tpu as pltpu

F32 = jnp.float32
BF16 = jnp.bfloat16
SDS = jax.ShapeDtypeStruct
MESH = pl.DeviceIdType.MESH

D = 1024
D_RNN = 1280
D_SGU = 1024
D_FF = 4096
D_IN = 2 * D_RNN + 2 * D_SGU + 2 * D
DEPTH = 2
RNN_HEADS = 20
HEAD_DIM = 64
CONV_WIDTH = 4
LRU_C = 8.0
SGU_GROUPS = 8
SGU_BLOCK = 128
CHUNK = 64
EPS = 1e-6
N_DEV = 8

ADAM_LR = 0.001
ADAM_B1 = 0.9
ADAM_B2 = 0.999
ADAM_EPS = 1e-08
ADAM_WD = 0.01
ADAM_STEP = 10

LANES = 128
SUBLANES = 8
VMEM_LIMIT_BYTES = 56 * 1024 * 1024

N_RNN_TILES = D_RNN // LANES
GRNN_BLK128 = D_RNN // LANES
U_BLK512 = (2 * D_RNN) // 512
V_BLK512 = (2 * D_RNN + D_SGU) // 512
GA_BLK512 = (2 * D_RNN + 2 * D_SGU) // 512
GB_BLK512 = (2 * D_RNN + 2 * D_SGU + D) // 512

SMALL_ROWS_PER_DEV = 80
SMALL_ROWS = N_DEV * SMALL_ROWS_PER_DEV


def _params(*sem):
    return pltpu.CompilerParams(dimension_semantics=sem, vmem_limit_bytes=VMEM_LIMIT_BYTES)


def _sigmoid(x):
    return 0.5 + 0.5 * jnp.tanh(0.5 * x)


_GELU_C = 0.7978845608028654
_GELU_K = 0.044715


def _gelu(x):
    t = jnp.tanh(_GELU_C * (x + _GELU_K * x * x * x))
    return 0.5 * x * (1.0 + t)


def _gelu_and_grad(x):
    t = jnp.tanh(_GELU_C * (x + _GELU_K * x * x * x))
    val = 0.5 * x * (1.0 + t)
    grad = 0.5 * (1.0 + t) + 0.5 * x * (1.0 - t * t) * _GELU_C * (1.0 + 3.0 * _GELU_K * x * x)
    return val, grad


def _one_minus_square(log_a, a):
    return -jnp.tanh(log_a) * (1.0 + a * a)


def _dot(a, b):
    return jnp.dot(a, b, preferred_element_type=F32)


def _dot_nt(a, b):
    return lax.dot_general(a, b, (((1,), (1,)), ((), ())), preferred_element_type=F32)


def _dot_tn(a, b):
    return lax.dot_general(a, b, (((0,), (0,)), ((), ())), preferred_element_type=F32)


def _norm_matmul_nt(x, g, w, *, tm, tn, name, comm=None):
    s, n = x.shape[0], w.shape[0]
    tm, tn = min(tm, s), min(tn, n)

    def body(x_ref, g_ref, w_ref, o_ref, h_ref):
        @pl.when(pl.program_id(1) == 0)
        def _():
            xv = x_ref[...]
            r = lax.rsqrt(jnp.mean(xv * xv, axis=-1, keepdims=True) + EPS)
            h_ref[...] = (xv * r * g_ref[...]).astype(BF16)

        o_ref[...] = _dot_nt(h_ref[...], w_ref[...]).astype(o_ref.dtype)

    return _call(
        body,
        (x, g, w),
        name=name,
        grid=(s // tm, n // tn),
        in_specs=[
            pl.BlockSpec((tm, D), lambda i, j: (i, 0)),
            pl.BlockSpec((1, D), lambda i, j: (0, 0)),
            pl.BlockSpec((tn, D), lambda i, j: (j, 0)),
        ],
        out_specs=[pl.BlockSpec((tm, tn), lambda i, j: (i, j)), pl.BlockSpec((tm, D), lambda i, j: (i, 0))],
        out_shape=[SDS((s, n), BF16), SDS((s, D), BF16)],
        semantics=("parallel", "arbitrary"),
        comm=comm,
    )


def _matmul_nn_res(a, w, res, *, relu2, tm, name, comm=None):
    s, k = a.shape
    tm = min(tm, s)

    def body(a_ref, w_ref, r_ref, o_ref):
        av = a_ref[...]
        if relu2:
            t = jnp.maximum(av.astype(F32), 0.0)
            av = (t * t).astype(BF16)
        o_ref[...] = r_ref[...] + _dot(av, w_ref[...])

    return _call(
        body,
        (a, w, res),
        name=name,
        grid=(s // tm,),
        in_specs=[
            pl.BlockSpec((tm, k), lambda i: (i, 0)),
            pl.BlockSpec((k, D), lambda i: (0, 0)),
            pl.BlockSpec((tm, D), lambda i: (i, 0)),
        ],
        out_specs=pl.BlockSpec((tm, D), lambda i: (i, 0)),
        out_shape=SDS((s, D), F32),
        semantics=("parallel",),
        comm=comm,
    )


def _matmul_nt_drelu2(a, w, pre, *, tm, tn, name):
    s, n = a.shape[0], w.shape[0]
    tm, tn = min(tm, s), min(tn, n)

    def body(a_ref, w_ref, p_ref, o_ref):
        d = _dot_nt(a_ref[...], w_ref[...])
        o_ref[...] = (d * (2.0 * jnp.maximum(p_ref[...].astype(F32), 0.0))).astype(o_ref.dtype)

    return pl.pallas_call(
        body,
        name=name,
        grid=(s // tm, n // tn),
        in_specs=[
            pl.BlockSpec((tm, D), lambda i, j: (i, 0)),
            pl.BlockSpec((tn, D), lambda i, j: (j, 0)),
            pl.BlockSpec((tm, tn), lambda i, j: (i, j)),
        ],
        out_specs=pl.BlockSpec((tm, tn), lambda i, j: (i, j)),
        out_shape=SDS((s, n), BF16),
        compiler_params=_params("parallel", "arbitrary"),
    )(a, w, pre)


def _matmul_tn(a_list, b, *, relu2, tka, name, comm=None):
    s = b.shape[0]
    n = len(a_list)
    nblk = [a.shape[1] // tka for a in a_list]
    starts = [sum(nblk[:p]) for p in range(n)]

    def body(*refs):
        a_refs, b_ref, o_ref = refs[:n], refs[n], refs[n + 1]
        i = pl.program_id(0)
        for p in range(n):

            @pl.when((i >= starts[p]) & (i < starts[p] + nblk[p]))
            def _(p=p):
                av = a_refs[p][...]
                if relu2:
                    t = jnp.maximum(av.astype(F32), 0.0)
                    av = (t * t).astype(BF16)
                o_ref[...] = _dot_tn(av, b_ref[...]).astype(o_ref.dtype)

    def piece_spec(p):
        return pl.BlockSpec((s, tka), lambda i: (0, jnp.clip(i - starts[p], 0, nblk[p] - 1)))

    return _call(
        body,
        (*a_list, b),
        name=name,
        grid=(sum(nblk),),
        in_specs=[piece_spec(p) for p in range(n)] + [pl.BlockSpec((s, D), lambda i: (0, 0))],
        out_specs=pl.BlockSpec((tka, D), lambda i: (i, 0)),
        out_shape=SDS((sum(nblk) * tka, D), BF16),
        semantics=("parallel",),
        comm=comm,
    )


def _matmul_nn_rmsnorm_bwd(a_list, w, x, g, res, *, tm, name, comm=None):
    s = x.shape[0]
    tm = min(tm, s)
    n = len(a_list)
    widths = [a.shape[1] for a in a_list]
    offs = [sum(widths[:p]) for p in range(n)]
    k = sum(widths)

    def body(*refs):
        a_refs = refs[:n]
        w_ref, x_ref, g_ref, r_ref, dx_ref, dxb_ref, dg_ref = refs[n:]

        @pl.when(pl.program_id(0) == 0)
        def _():
            dg_ref[...] = jnp.zeros_like(dg_ref)

        dh = _dot(a_refs[0][...], w_ref[0 : widths[0], :])
        for p in range(1, n):
            dh += _dot(a_refs[p][...], w_ref[offs[p] : offs[p] + widths[p], :])
        xv = x_ref[...]
        r = lax.rsqrt(jnp.mean(xv * xv, axis=-1, keepdims=True) + EPS)
        xhat = xv * r
        dxh = dh * g_ref[...]
        dx = r_ref[...] + r * (dxh - xhat * jnp.mean(dxh * xhat, axis=-1, keepdims=True))
        dx_ref[...] = dx
        dxb_ref[...] = dx.astype(BF16)
        dg_ref[...] += jnp.sum(dh * xhat, axis=0, keepdims=True)

    act = pl.BlockSpec((tm, D), lambda i: (i, 0))
    vec = pl.BlockSpec((1, D), lambda i: (0, 0))
    return _call(
        body,
        (*a_list, w, x, g, res),
        name=name,
        grid=(s // tm,),
        in_specs=[pl.BlockSpec((tm, wd), lambda i: (i, 0)) for wd in widths]
        + [pl.BlockSpec((k, D), lambda i: (0, 0), pipeline_mode=pl.Buffered(1)), act, vec, act],
        out_specs=[act, act, vec],
        out_shape=[SDS((s, D), F32), SDS((s, D), BF16), SDS((1, D), F32)],
        semantics=("arbitrary",),
        comm=comm,
    )


def _rows_before(ext, k):
    if k == 0:
        return ext[SUBLANES:, :]
    return pltpu.roll(ext, k, 0)[SUBLANES:, :]


def _rows_after(ext, k, n):
    if k == 0:
        return ext[:n, :]
    return pltpu.roll(ext, n + SUBLANES - k, 0)[:n, :]


def _scan_forward(a, b, n):
    row = lax.broadcasted_iota(jnp.int32, a.shape, 0)
    d = 1
    while d < n:
        if d < SUBLANES:
            m = row >= d
            a_s = jnp.where(m, pltpu.roll(a, d, 0), 1.0)
            b_s = jnp.where(m, pltpu.roll(b, d, 0), 0.0)
            b = a * b_s + b
            a = a * a_s
        else:
            b = jnp.concatenate([b[:d], a[d:] * b[: n - d] + b[d:]], axis=0)
            a = jnp.concatenate([a[:d], a[d:] * a[: n - d]], axis=0)
        d *= 2
    return a, b


def _scan_backward(a, b, n):
    row = lax.broadcasted_iota(jnp.int32, a.shape, 0)
    d = 1
    while d < n:
        if d < SUBLANES:
            m = row < n - d
            a_s = jnp.where(m, pltpu.roll(a, n - d, 0), 1.0)
            b_s = jnp.where(m, pltpu.roll(b, n - d, 0), 0.0)
            b = a * b_s + b
            a = a * a_s
        else:
            b = jnp.concatenate([a[: n - d] * b[d:] + b[: n - d], b[n - d :]], axis=0)
            a = jnp.concatenate([a[: n - d] * a[d:], a[n - d :]], axis=0)
        d *= 2
    return b


def _softplus_neg(lam):
    z = -lam
    return jnp.maximum(z, 0.0) + jnp.log1p(jnp.exp(-jnp.abs(z)))


def _conv_and_gates(xc, xprev, cw_ref, cb_ref, wa_ref, ba_ref, wx_ref, bx_ref, lam_ref):
    ext = jnp.concatenate([xprev, xc], axis=0)
    x1, x2, x3 = _rows_before(ext, 1), _rows_before(ext, 2), _rows_before(ext, 3)
    xr = cb_ref[...] + x3 * cw_ref[0:1, :] + x2 * cw_ref[1:2, :] + x1 * cw_ref[2:3, :] + xc * cw_ref[3:4, :]
    xrb = xr.astype(BF16)
    r = _sigmoid(_dot(xrb, wa_ref[...]) + ba_ref[...])
    i = _sigmoid(_dot(xrb, wx_ref[...]) + bx_ref[...])
    sp = _softplus_neg(lam_ref[...])
    log_a = (-LRU_C * r) * sp
    a = jnp.exp(log_a)
    return xr, (x1, x2, x3), r, i, a, _one_minus_square(log_a, a)


def _branch_a_fwd(proj, cw, cb, wa2, ba, wx2, bx, lam, *, tc, name, comm=None):
    s = proj.shape[0]
    tc = min(tc, s)

    def body(x_ref, g_ref, cw_ref, cb_ref, wa_ref, ba_ref, wx_ref, bx_ref, lam_ref, h_ref, y_ref, xprev, hlast):
        @pl.when(pl.program_id(1) == 0)
        def _():
            xprev[...] = jnp.zeros_like(xprev)
            hlast[...] = jnp.zeros_like(hlast)

        xc = x_ref[...].astype(F32)
        xr, _, r, i, a, om = _conv_and_gates(xc, xprev[...], cw_ref, cb_ref, wa_ref, ba_ref, wx_ref, bx_ref, lam_ref)
        xprev[...] = xc[tc - SUBLANES :, :]
        u = jnp.sqrt(om) * (i * xr)
        acum, b = _scan_forward(a, u, tc)
        h = b + acum * hlast[SUBLANES - 1 : SUBLANES, :]
        hlast[...] = h[tc - SUBLANES :, :]
        h_ref[...] = h
        y_ref[...] = (h * _gelu(g_ref[...].astype(F32))).astype(BF16)

    tile = lambda j, c: (0, j)
    return _call(
        body,
        (proj, proj, cw, cb, wa2, ba, wx2, bx, lam),
        name=name,
        grid=(N_RNN_TILES, s // tc),
        in_specs=[
            pl.BlockSpec((tc, LANES), lambda j, c: (c, j)),
            pl.BlockSpec((tc, LANES), lambda j, c: (c, GRNN_BLK128 + j)),
            pl.BlockSpec((CONV_WIDTH, LANES), tile),
            pl.BlockSpec((1, LANES), tile),
            pl.BlockSpec((None, LANES, LANES), lambda j, c: (j, 0, 0)),
            pl.BlockSpec((1, LANES), tile),
            pl.BlockSpec((None, LANES, LANES), lambda j, c: (j, 0, 0)),
            pl.BlockSpec((1, LANES), tile),
            pl.BlockSpec((1, LANES), tile),
        ],
        out_specs=[pl.BlockSpec((tc, LANES), lambda j, c: (c, j)), pl.BlockSpec((tc, LANES), lambda j, c: (c, j))],
        out_shape=[SDS((s, D_RNN), F32), SDS((s, D_RNN), BF16)],
        scratch_shapes=[pltpu.VMEM((SUBLANES, LANES), F32), pltpu.VMEM((SUBLANES, LANES), F32)],
        semantics=("parallel", "arbitrary"),
        comm=comm,
    )


def _branch_a_bwd(dy, proj, h, cw, cb, wa2, ba, wx2, bx, lam, wa2t, wx2t, *, tc, name, comm=None):
    s = proj.shape[0]
    tc = min(tc, s)
    nc = s // tc
    halo16 = tc // 16
    halo8 = tc // SUBLANES

    def body(dy_ref, x_ref, xh_ref, g_ref, h_ref, hh_ref, cw_ref, cb_ref, wa_ref, ba_ref, wx_ref, bx_ref, lam_ref,
             wat_ref, wxt_ref, dx_ref, dg_ref, dcw_ref, dcb_ref, dba_ref, dbx_ref, dlam_ref, dwa_ref, dwx_ref,
             carry, dxr_next):
        cc = pl.program_id(1)
        ct = nc - 1 - cc

        @pl.when(cc == 0)
        def _():
            carry[...] = jnp.zeros_like(carry)
            dxr_next[...] = jnp.zeros_like(dxr_next)
            for ref in (dcw_ref, dcb_ref, dba_ref, dbx_ref, dlam_ref, dwa_ref, dwx_ref):
                ref[...] = jnp.zeros_like(ref)

        xc = x_ref[...].astype(F32)
        xprev = jnp.where(ct > 0, xh_ref[SUBLANES:, :].astype(F32), 0.0)
        xr, (x1, x2, x3), r, i, a, om = _conv_and_gates(
            xc, xprev, cw_ref, cb_ref, wa_ref, ba_ref, wx_ref, bx_ref, lam_ref
        )
        inv_norm = lax.rsqrt(om)
        norm = om * inv_norm
        row = lax.broadcasted_iota(jnp.int32, xc.shape, 0)

        hv = h_ref[...]
        ge, ge_grad = _gelu_and_grad(g_ref[...].astype(F32))
        dyv = dy_ref[...].astype(F32)
        dg_ref[...] = (dyv * hv * ge_grad).astype(dg_ref.dtype)
        dh = dyv * ge

        b = dh + jnp.where(row == tc - 1, carry[0:1, :], 0.0)
        a_next = jnp.where(row < tc - 1, pltpu.roll(a, tc - 1, 0), 0.0)
        gadj = _scan_backward(a_next, b, tc)
        carry[...] = (a * gadj)[:SUBLANES, :]

        hprev_first = jnp.where(ct > 0, hh_ref[SUBLANES - 1 : SUBLANES, :], 0.0)
        hprev = jnp.where(row >= 1, pltpu.roll(hv, 1, 0), hprev_first)
        da = gadj * hprev
        ix = i * xr
        dnorm = gadj * ix
        di = gadj * norm * xr
        dlog_a = da * a - dnorm * (1.0 - om) * inv_norm
        sp = _softplus_neg(lam_ref[...])
        dr = dlog_a * (-LRU_C * sp)
        dsp = jnp.sum(dlog_a * (-LRU_C * r), axis=0, keepdims=True)
        dlam_ref[...] += dsp * (-_sigmoid(-lam_ref[...]))
        dza = dr * r * (1.0 - r)
        dzx = di * i * (1.0 - i)
        dzab, dzxb = dza.astype(BF16), dzx.astype(BF16)
        dxr = gadj * norm * i + _dot(dzab, wat_ref[...]) + _dot(dzxb, wxt_ref[...])
        xrb = xr.astype(BF16)
        dwa_ref[...] += _dot_tn(xrb, dzab)
        dwx_ref[...] += _dot_tn(xrb, dzxb)
        dba_ref[...] += jnp.sum(dza, axis=0, keepdims=True)
        dbx_ref[...] += jnp.sum(dzx, axis=0, keepdims=True)

        ext = jnp.concatenate([dxr, dxr_next[...]], axis=0)
        dx = (
            dxr * cw_ref[3:4, :]
            + _rows_after(ext, 1, tc) * cw_ref[2:3, :]
            + _rows_after(ext, 2, tc) * cw_ref[1:2, :]
            + _rows_after(ext, 3, tc) * cw_ref[0:1, :]
        )
        dxr_next[...] = dxr[:SUBLANES, :]
        dx_ref[...] = dx.astype(dx_ref.dtype)
        dcb_ref[...] += jnp.sum(dxr, axis=0, keepdims=True)
        dcw_ref[3:4, :] += jnp.sum(dxr * xc, axis=0, keepdims=True)
        dcw_ref[2:3, :] += jnp.sum(dxr * x1, axis=0, keepdims=True)
        dcw_ref[1:2, :] += jnp.sum(dxr * x2, axis=0, keepdims=True)
        dcw_ref[0:1, :] += jnp.sum(dxr * x3, axis=0, keepdims=True)

    tile = lambda j, c: (0, j)
    mat = lambda j, c: (j, 0, 0)
    cur = lambda j, c: (nc - 1 - c, j)
    vec = pl.BlockSpec((1, LANES), tile)
    matspec = pl.BlockSpec((None, LANES, LANES), mat)
    return _call(
        body,
        (dy, proj, proj, proj, h, h, cw, cb, wa2, ba, wx2, bx, lam, wa2t, wx2t),
        name=name,
        grid=(N_RNN_TILES, nc),
        in_specs=[
            pl.BlockSpec((tc, LANES), cur),
            pl.BlockSpec((tc, LANES), cur),
            pl.BlockSpec((16, LANES), lambda j, c: (jnp.maximum((nc - 1 - c) * halo16 - 1, 0), j)),
            pl.BlockSpec((tc, LANES), lambda j, c: (nc - 1 - c, GRNN_BLK128 + j)),
            pl.BlockSpec((tc, LANES), cur),
            pl.BlockSpec((SUBLANES, LANES), lambda j, c: (jnp.maximum((nc - 1 - c) * halo8 - 1, 0), j)),
            pl.BlockSpec((CONV_WIDTH, LANES), tile),
            vec,
            matspec,
            vec,
            matspec,
            vec,
            vec,
            matspec,
            matspec,
        ],
        out_specs=[
            pl.BlockSpec((tc, LANES), cur),
            pl.BlockSpec((tc, LANES), cur),
            pl.BlockSpec((CONV_WIDTH, LANES), tile),
            vec,
            vec,
            vec,
            vec,
            matspec,
            matspec,
        ],
        out_shape=[
            SDS((s, D_RNN), BF16),
            SDS((s, D_RNN), BF16),
            SDS((CONV_WIDTH, D_RNN), F32),
            SDS((1, D_RNN), F32),
            SDS((1, D_RNN), F32),
            SDS((1, D_RNN), F32),
            SDS((1, D_RNN), F32),
            SDS((N_RNN_TILES, LANES, LANES), F32),
            SDS((N_RNN_TILES, LANES, LANES), F32),
        ],
        scratch_shapes=[pltpu.VMEM((SUBLANES, LANES), F32), pltpu.VMEM((SUBLANES, LANES), F32)],
        semantics=("parallel", "arbitrary"),
        comm=comm,
    )


def _sgu_specs(tb):
    half = lambda blk: pl.BlockSpec((tb, 512), lambda n: (n, blk))
    return [half(U_BLK512), half(U_BLK512 + 1), half(V_BLK512), half(V_BLK512 + 1)]


def _sgu_normed(v, lng_ref, lnb_ref):
    gv, gv_grad = _gelu_and_grad(v)
    mu = jnp.mean(gv, axis=-1, keepdims=True)
    xc = gv - mu
    rs = lax.rsqrt(jnp.mean(xc * xc, axis=-1, keepdims=True) + EPS)
    xhat = xc * rs
    return xhat * lng_ref[...] + lnb_ref[...], xhat, rs, gv_grad


def _sgu_fwd(proj, lng, lnb, wm, bias, *, tb, name):
    s = proj.shape[0]
    tb = min(tb, s)

    def body(u0_ref, u1_ref, v0_ref, v1_ref, lng_ref, lnb_ref, wm_ref, bias_ref, y_ref):
        u = jnp.concatenate([u0_ref[...], u1_ref[...]], axis=1).astype(F32)
        v = jnp.concatenate([v0_ref[...], v1_ref[...]], axis=1).astype(F32)
        gu = _gelu(u)
        vn, _, _, _ = _sgu_normed(v, lng_ref, lnb_ref)
        vnb = vn.astype(BF16)
        for blk in range(tb // SGU_BLOCK):
            rows = slice(blk * SGU_BLOCK, (blk + 1) * SGU_BLOCK)
            for g in range(SGU_GROUPS):
                cols = slice(g * LANES, (g + 1) * LANES)
                mixed = _dot(wm_ref[g], vnb[rows, cols]) + bias_ref[g]
                y_ref[rows, cols] = (gu[rows, cols] * mixed).astype(BF16)

    const2 = lambda n: (0, 0)
    const3 = lambda n: (0, 0, 0)
    return pl.pallas_call(
        body,
        name=name,
        grid=(s // tb,),
        in_specs=_sgu_specs(tb)
        + [
            pl.BlockSpec((1, D_SGU), const2),
            pl.BlockSpec((1, D_SGU), const2),
            pl.BlockSpec((SGU_GROUPS, SGU_BLOCK, SGU_BLOCK), const3),
            pl.BlockSpec((SGU_GROUPS, SGU_BLOCK, LANES), const3),
        ],
        out_specs=pl.BlockSpec((tb, D_SGU), lambda n: (n, 0)),
        out_shape=SDS((s, D_SGU), BF16),
        compiler_params=_params("parallel"),
    )(proj, proj, proj, proj, lng, lnb, wm, bias)


def _sgu_bwd(dy, proj, lng, lnb, wm, wmt, bias, mask, *, tb, name, comm=None):
    s = proj.shape[0]
    tb = min(tb, s)
    nb = s // tb

    def body(dy_ref, u0_ref, u1_ref, v0_ref, v1_ref, lng_ref, lnb_ref, wm_ref, wmt_ref, bias_ref, mask_ref,
             du_ref, dv_ref, dws_ref, dbs_ref, dlng_ref, dlnb_ref, dvn_scr, dbs_acc):
        n = pl.program_id(0)

        @pl.when(n == 0)
        def _():
            dbs_acc[...] = jnp.zeros_like(dbs_acc)
            for ref in (dws_ref, dlng_ref, dlnb_ref):
                ref[...] = jnp.zeros_like(ref)

        u = jnp.concatenate([u0_ref[...], u1_ref[...]], axis=1).astype(F32)
        v = jnp.concatenate([v0_ref[...], v1_ref[...]], axis=1).astype(F32)
        gu, gu_grad = _gelu_and_grad(u)
        vn, xhat, rs, gv_grad = _sgu_normed(v, lng_ref, lnb_ref)
        vnb = vn.astype(BF16)
        dyv = dy_ref[...].astype(F32)
        for blk in range(tb // SGU_BLOCK):
            rows = slice(blk * SGU_BLOCK, (blk + 1) * SGU_BLOCK)
            for g in range(SGU_GROUPS):
                cols = slice(g * LANES, (g + 1) * LANES)
                vt = vnb[rows, cols]
                mixed = _dot(wm_ref[g], vt) + bias_ref[g]
                dyt = dyv[rows, cols]
                du_ref[rows, cols] = (dyt * mixed * gu_grad[rows, cols]).astype(BF16)
                dmix = dyt * gu[rows, cols]
                dmixb = dmix.astype(BF16)
                dvn_scr[rows, cols] = _dot(wmt_ref[g], dmixb)
                dws_ref[g] += _dot_nt(dmixb, vt) * mask_ref[...]
                dbs_acc[g] += dmix
        dvn = dvn_scr[...]
        dlng_ref[...] += jnp.sum(dvn * xhat, axis=0, keepdims=True)
        dlnb_ref[...] += jnp.sum(dvn, axis=0, keepdims=True)
        dxh = dvn * lng_ref[...]
        dgv = rs * (
            dxh - jnp.mean(dxh, axis=-1, keepdims=True) - xhat * jnp.mean(dxh * xhat, axis=-1, keepdims=True)
        )
        dv_ref[...] = (dgv * gv_grad).astype(BF16)

        @pl.when(n == nb - 1)
        def _():
            for g in range(SGU_GROUPS):
                dbs_ref[g] = jnp.broadcast_to(jnp.sum(dbs_acc[g], axis=-1, keepdims=True), (SGU_BLOCK, LANES))

    const2 = lambda n: (0, 0)
    const3 = lambda n: (0, 0, 0)
    gmat = pl.BlockSpec((SGU_GROUPS, SGU_BLOCK, SGU_BLOCK), const3)
    vec = pl.BlockSpec((1, D_SGU), const2)
    act = pl.BlockSpec((tb, D_SGU), lambda n: (n, 0))
    return _call(
        body,
        (dy, proj, proj, proj, proj, lng, lnb, wm, wmt, bias, mask),
        name=name,
        grid=(nb,),
        in_specs=[act] + _sgu_specs(tb) + [vec, vec, gmat, gmat, gmat, pl.BlockSpec((SGU_BLOCK, SGU_BLOCK), const2)],
        out_specs=[act, act, gmat, gmat, vec, vec],
        out_shape=[
            SDS((s, D_SGU), BF16),
            SDS((s, D_SGU), BF16),
            SDS((SGU_GROUPS, SGU_BLOCK, SGU_BLOCK), F32),
            SDS((SGU_GROUPS, SGU_BLOCK, LANES), F32),
            SDS((1, D_SGU), F32),
            SDS((1, D_SGU), F32),
        ],
        scratch_shapes=[pltpu.VMEM((tb, D_SGU), F32), pltpu.VMEM((SGU_GROUPS, SGU_BLOCK, LANES), F32)],
        semantics=("arbitrary",),
        comm=comm,
    )


def _gate_specs(tm):
    half = lambda blk: pl.BlockSpec((tm, 512), lambda i: (i, blk))
    return [half(GA_BLK512), half(GA_BLK512 + 1), half(GB_BLK512), half(GB_BLK512 + 1)]


def _merge_fwd(ya_pre, yb_pre, proj, x, w_ba, w_bb, w_out, *, tm, name):
    s = x.shape[0]
    tm = min(tm, s)

    def body(ya_ref, yb_ref, a0, a1, b0, b1, x_ref, wa_ref, wb_ref, wo_ref, x1_ref, yao_ref, ybo_ref):
        ya = _dot(ya_ref[...], wa_ref[...])
        yb = _dot(yb_ref[...], wb_ref[...])
        sa = _sigmoid(jnp.concatenate([a0[...], a1[...]], axis=1).astype(F32))
        sb = _sigmoid(jnp.concatenate([b0[...], b1[...]], axis=1).astype(F32))
        merged = sa * ya + sb * yb
        x1_ref[...] = x_ref[...] + _dot(merged.astype(BF16), wo_ref[...])
        yao_ref[...] = ya.astype(BF16)
        ybo_ref[...] = yb.astype(BF16)

    whole = lambda r: pl.BlockSpec((r, D), lambda i: (0, 0))
    act = pl.BlockSpec((tm, D), lambda i: (i, 0))
    return pl.pallas_call(
        body,
        name=name,
        grid=(s // tm,),
        in_specs=[pl.BlockSpec((tm, D_RNN), lambda i: (i, 0)), act] + _gate_specs(tm) + [act, whole(D_RNN), whole(D_SGU), whole(D)],
        out_specs=[act, act, act],
        out_shape=[SDS((s, D), F32), SDS((s, D), BF16), SDS((s, D), BF16)],
        compiler_params=_params("parallel"),
    )(ya_pre, yb_pre, proj, proj, proj, proj, x, w_ba, w_bb, w_out)


def _merge_bwd(dx1, ya, yb, proj, w_ba, w_bb, w_out, *, tm, name, comm=None):
    s = dx1.shape[0]
    tm = min(tm, s)

    def body(dx_ref, ya_ref, yb_ref, a0, a1, b0, b1, wa_ref, wb_ref, wo_ref,
             mg_ref, dya_ref, dyb_ref, dga_ref, dgb_ref, dyap_ref, dybp_ref):
        dm = _dot_nt(dx_ref[...], wo_ref[...])
        ya = ya_ref[...].astype(F32)
        yb = yb_ref[...].astype(F32)
        sa = _sigmoid(jnp.concatenate([a0[...], a1[...]], axis=1).astype(F32))
        sb = _sigmoid(jnp.concatenate([b0[...], b1[...]], axis=1).astype(F32))
        mg_ref[...] = (sa * ya + sb * yb).astype(BF16)
        dya = (dm * sa).astype(BF16)
        dyb = (dm * sb).astype(BF16)
        dya_ref[...] = dya
        dyb_ref[...] = dyb
        dga_ref[...] = (dm * ya * sa * (1.0 - sa)).astype(BF16)
        dgb_ref[...] = (dm * yb * sb * (1.0 - sb)).astype(BF16)
        dyap_ref[...] = _dot_nt(dya, wa_ref[...]).astype(BF16)
        dybp_ref[...] = _dot_nt(dyb, wb_ref[...]).astype(BF16)

    whole = lambda r: pl.BlockSpec((r, D), lambda i: (0, 0))
    act = pl.BlockSpec((tm, D), lambda i: (i, 0))
    act_rnn = pl.BlockSpec((tm, D_RNN), lambda i: (i, 0))
    return _call(
        body,
        (dx1, ya, yb, proj, proj, proj, proj, w_ba, w_bb, w_out),
        name=name,
        grid=(s // tm,),
        in_specs=[act, act, act] + _gate_specs(tm) + [whole(D_RNN), whole(D_SGU), whole(D)],
        out_specs=[act, act, act, act, act, act_rnn, act],
        out_shape=[SDS((s, D), BF16)] * 5 + [SDS((s, D_RNN), BF16), SDS((s, D_SGU), BF16)],
        semantics=("parallel",),
        comm=comm,
    )


def _final_loss(x, g, target, *, tm, name):
    s = x.shape[0]
    tm = min(tm, s)

    def body(x_ref, g_ref, t_ref, dx_ref, dxb_ref, dg_ref, loss_ref):
        @pl.when(pl.program_id(0) == 0)
        def _():
            dg_ref[...] = jnp.zeros_like(dg_ref)
            loss_ref[...] = jnp.zeros_like(loss_ref)

        xv = x_ref[...]
        r = lax.rsqrt(jnp.mean(xv * xv, axis=-1, keepdims=True) + EPS)
        xhat = xv * r
        e = xhat * g_ref[...] - t_ref[...]
        loss_ref[...] += 0.5 * jnp.sum(jnp.mean(e * e, axis=-1, keepdims=True), axis=0, keepdims=True)
        dy = e * (1.0 / D)
        dxh = dy * g_ref[...]
        dx = r * (dxh - xhat * jnp.mean(dxh * xhat, axis=-1, keepdims=True))
        dx_ref[...] = dx
        dxb_ref[...] = dx.astype(BF16)
        dg_ref[...] += jnp.sum(dy * xhat, axis=0, keepdims=True)

    act = pl.BlockSpec((tm, D), lambda i: (i, 0))
    vec = pl.BlockSpec((1, D), lambda i: (0, 0))
    return pl.pallas_call(
        body,
        name=name,
        grid=(s // tm,),
        in_specs=[act, vec, act],
        out_specs=[act, act, vec, pl.BlockSpec((SUBLANES, LANES), lambda i: (0, 0))],
        out_shape=[SDS((s, D), F32), SDS((s, D), BF16), SDS((1, D), F32), SDS((SUBLANES, LANES), F32)],
        compiler_params=_params("arbitrary"),
    )(x, g, target)


def _adamw_math(w, g, m, v):
    m2 = ADAM_B1 * m + (1.0 - ADAM_B1) * g
    v2 = ADAM_B2 * v + (1.0 - ADAM_B2) * (g * g)
    m_hat = m2 / (1.0 - ADAM_B1**ADAM_STEP)
    v_hat = v2 / (1.0 - ADAM_B2**ADAM_STEP)
    delta = -ADAM_LR * (m_hat / (jnp.sqrt(v_hat) + ADAM_EPS) + ADAM_WD * w)
    return delta, m2, v2


def _row_tile(rows, cap):
    return max(t for t in range(SUBLANES, min(cap, rows) + 1, SUBLANES) if rows % t == 0)


def _adamw_layers(w, grads, m, v, *, tr, name):
    depth, r, c = w.shape
    tr = _row_tile(r, tr)

    def body(*refs):
        g_refs = refs[:depth]
        w_ref, m_ref, v_ref, g_out, d_ref, mo_ref, vo_ref = refs[depth:]
        for l in range(depth):

            @pl.when(pl.program_id(0) == l)
            def _(l=l):
                g = g_refs[l][...]
                g_out[...] = g
                d_ref[...], mo_ref[...], vo_ref[...] = _adamw_math(w_ref[...], g, m_ref[...], v_ref[...])

    def of_layer(ll):
        return pl.BlockSpec((tr, c), lambda l, i: (jnp.where(l == ll, i, 0), 0))

    stacked = pl.BlockSpec((None, tr, c), lambda l, i: (l, i, 0))
    return pl.pallas_call(
        body,
        name=name,
        grid=(depth, r // tr),
        in_specs=[of_layer(ll) for ll in range(depth)] + [stacked] * 3,
        out_specs=[stacked] * 4,
        out_shape=[SDS((depth, r, c), F32)] * 4,
        compiler_params=_params("parallel", "parallel"),
    )(*grads, w, m, v)


def _adamw_reduced(w, parts, from_chips, m, v, chip, *, tr, name):
    depth, r, _ = w.shape
    tr = _row_tile(r, tr)

    def body(chip_ref, *refs):
        p_refs, c_refs = refs[:depth], refs[depth : 2 * depth]
        w_ref, m_ref, v_ref, g_out, d_ref, mo_ref, vo_ref = refs[2 * depth :]
        for l in range(depth):

            @pl.when(pl.program_id(0) == l)
            def _(l=l):
                got = c_refs[l]
                g = ((p_refs[l][...].astype(F32) + got[0].astype(F32)) + got[1].astype(F32)) + got[2].astype(F32)
                g_out[...] = g
                d_ref[...], mo_ref[...], vo_ref[...] = _adamw_math(w_ref[...], g, m_ref[...], v_ref[...])

    def mine_of_layer(ll):
        return pl.BlockSpec((None, tr, D), lambda l, i, chip_ref: (chip_ref[0], jnp.where(l == ll, i, 0), 0))

    def theirs_of_layer(ll):
        return pl.BlockSpec((3, tr, D), lambda l, i, chip_ref: (0, jnp.where(l == ll, i, 0), 0))

    stacked = pl.BlockSpec((None, tr, D), lambda l, i, chip_ref: (l, i, 0))
    return pl.pallas_call(
        body,
        name=name,
        grid_spec=pltpu.PrefetchScalarGridSpec(
            num_scalar_prefetch=1,
            grid=(depth, r // tr),
            in_specs=[mine_of_layer(ll) for ll in range(depth)]
            + [theirs_of_layer(ll) for ll in range(depth)]
            + [stacked] * 3,
            out_specs=[stacked] * 4,
        ),
        out_shape=[SDS((depth, r, D), F32)] * 4,
        compiler_params=_params("parallel", "parallel"),
    )(chip, *parts, *from_chips, w, m, v)


def _adamw_small(groups, *, name):
    n = len(groups)

    def body(*refs):
        ins, outs = refs[: 4 * n], refs[4 * n :]
        for i in range(n):
            w, g, m, v = (ref[...] for ref in ins[4 * i : 4 * i + 4])
            outs[3 * i][...], outs[3 * i + 1][...], outs[3 * i + 2][...] = _adamw_math(w, g, m, v)

    vmem = pl.BlockSpec(memory_space=pltpu.VMEM)
    outs = pl.pallas_call(
        body,
        name=name,
        in_specs=[vmem] * (4 * n),
        out_specs=[vmem] * (3 * n),
        out_shape=[SDS(grp[0].shape, F32) for grp in groups for _ in range(3)],
        compiler_params=pltpu.CompilerParams(vmem_limit_bytes=VMEM_LIMIT_BYTES),
    )(*[a for grp in groups for a in grp])
    return [tuple(outs[3 * i : 3 * i + 3]) for i in range(n)]


ANY = pl.BlockSpec(memory_space=pl.ANY)


def _position():
    return lax.axis_index("x"), lax.axis_index("y"), lax.axis_index("c")


def _other_chips(x, y):
    return [(1 - x, y), (x, 1 - y), (1 - x, 1 - y)]


class _Comm:
    def __init__(self, inputs, out_shapes, sem_counts, start, middle, finish, middle_at=1.0):
        self.inputs, self.out_shapes, self.sem_counts = list(inputs), list(out_shapes), list(sem_counts)
        self.start, self.middle, self.finish = start, middle, finish
        self.middle_at = middle_at

    def sem_shapes(self):
        return [pltpu.SemaphoreType.DMA((n,)) for n in self.sem_counts]


def _merge_comms(comms):
    bounds, i, o, s = [], 0, 0, 0
    for cm in comms:
        bounds.append((i, i + len(cm.inputs), o, o + len(cm.out_shapes), s, s + len(cm.sem_counts)))
        i, o, s = bounds[-1][1], bounds[-1][3], bounds[-1][5]

    def phase(which):
        def run(ins, outs, sems):
            for cm, (i0, i1, o0, o1, s0, s1) in zip(comms, bounds):
                getattr(cm, which)(ins[i0:i1], outs[o0:o1], sems[s0:s1])

        return run

    return _Comm(
        [a for cm in comms for a in cm.inputs],
        [a for cm in comms for a in cm.out_shapes],
        [a for cm in comms for a in cm.sem_counts],
        phase("start"),
        phase("middle"),
        phase("finish"),
        middle_at=max(cm.middle_at for cm in comms),
    )


def _call(body, args, *, semantics, comm=None, **kw):
    if comm is None:
        return pl.pallas_call(body, compiler_params=_params(*semantics), **kw)(*args)
    grid, in_specs, out_specs, out_shape = kw["grid"], kw["in_specs"], kw["out_specs"], kw["out_shape"]
    scratch = list(kw.get("scratch_shapes", ()))
    single = not isinstance(out_shape, (list, tuple))
    core_specs = [out_specs] if single else list(out_specs)
    core_shapes = [out_shape] if single else list(out_shape)
    n_in, n_out, n_scr = len(in_specs), len(core_shapes), len(scratch)
    n_cin, n_cout = len(comm.inputs), len(comm.out_shapes)
    steps = 1
    for g in grid:
        steps *= g
    middle = min(int(comm.middle_at * steps), steps - 1)

    def hosted(*refs):
        core_in, c_in = refs[:n_in], refs[n_in : n_in + n_cin]
        o0 = n_in + n_cin
        core_out, c_out = refs[o0 : o0 + n_out], refs[o0 + n_out : o0 + n_out + n_cout]
        s0 = o0 + n_out + n_cout
        core_scr, sems = refs[s0 : s0 + n_scr], refs[s0 + n_scr :]
        step = pl.program_id(0)
        for d in range(1, len(grid)):
            step = step * grid[d] + pl.program_id(d)

        @pl.when(step == 0)
        def _():
            comm.start(c_in, c_out, sems)

        body(*core_in, *core_out, *core_scr)

        @pl.when(step == middle)
        def _():
            comm.middle(c_in, c_out, sems)

        @pl.when(step == steps - 1)
        def _():
            comm.finish(c_in, c_out, sems)

    outs = pl.pallas_call(
        hosted,
        name=kw["name"],
        grid=grid,
        in_specs=list(in_specs) + [ANY] * n_cin,
        out_specs=core_specs + [ANY] * n_cout,
        out_shape=core_shapes + comm.out_shapes,
        scratch_shapes=scratch + comm.sem_shapes(),
        compiler_params=_params(*(["arbitrary"] * len(grid))),
    )(*args, *comm.inputs)
    return (outs[0] if single else outs[:n_out]), outs[n_out:]


def _comm_only(comm, *, name):
    n_cin, n_cout = len(comm.inputs), len(comm.out_shapes)

    def body(*refs):
        ins, outs, sems = refs[:n_cin], refs[n_cin : n_cin + n_cout], refs[n_cin + n_cout :]
        comm.start(ins, outs, sems)
        comm.middle(ins, outs, sems)
        comm.finish(ins, outs, sems)

    return pl.pallas_call(
        body,
        name=name,
        in_specs=[ANY] * n_cin,
        out_specs=[ANY] * n_cout,
        out_shape=comm.out_shapes,
        scratch_shapes=comm.sem_shapes(),
    )(*comm.inputs)


def _gather_comm(shards, pass_on_at=1.0):
    n = len(shards)
    per = 7

    def plan(ins, outs, sems):
        send, recv, local = sems
        x, y, c = _position()
        me, sibling = (x, y, c), (x, y, 1 - c)
        chips = _other_chips(x, y)

        def block(t, px, py, pc):
            return outs[t].at[pl.ds(4 * px + 2 * py + pc, 1)]

        def copy(t, k, blk, to, src=None):
            return pltpu.make_async_remote_copy(
                src_ref=block(t, *blk) if src is None else src,
                dst_ref=block(t, *blk),
                send_sem=send.at[t * per + k],
                recv_sem=recv.at[t * per + k],
                device_id=to,
                device_id_type=MESH,
            )

        mine = [pltpu.make_async_copy(ins[t], block(t, *me), local.at[t]) for t in range(n)]
        to_chips = [copy(t, 1 + j, me, (*chip, c), src=ins[t]) for t in range(n) for j, chip in enumerate(chips)]
        to_sibling = [copy(t, 0, me, sibling, src=ins[t]) for t in range(n)]
        from_chips = [copy(t, 1 + j, (*chip, c), me) for t in range(n) for j, chip in enumerate(chips)]
        passed_on = [copy(t, 4 + j, (*chip, c), sibling) for t in range(n) for j, chip in enumerate(chips)]
        from_sibling = [copy(t, 0, sibling, me) for t in range(n)]
        from_sibling += [copy(t, 4 + j, (*chip, 1 - c), me) for t in range(n) for j, chip in enumerate(chips)]
        return mine, to_chips, to_sibling, from_chips, passed_on, from_sibling

    def start(ins, outs, sems):
        mine, to_chips, to_sibling, _, _, _ = plan(ins, outs, sems)
        for cp in mine + to_chips + to_sibling:
            cp.start()

    def middle(ins, outs, sems):
        _, _, _, from_chips, passed_on, _ = plan(ins, outs, sems)
        for arrived, onward in zip(from_chips, passed_on):
            arrived.wait_recv()
            onward.start()

    def finish(ins, outs, sems):
        mine, to_chips, to_sibling, _, passed_on, from_sibling = plan(ins, outs, sems)
        for cp in from_sibling:
            cp.wait_recv()
        for cp in to_chips + to_sibling + passed_on:
            cp.wait_send()
        for cp in mine:
            cp.wait()

    out_shapes = [SDS((N_DEV,) + sh.shape[1:], sh.dtype) for sh in shards]
    return _Comm(shards, out_shapes, [n * per, n * per, n], start, middle, finish, middle_at=pass_on_at)


def _exchange_comm(arrays, out_shapes, n_copies, copies_of):
    def start(ins, outs, sems):
        for cp in copies_of(ins, outs, *sems):
            cp.start()

    def middle(ins, outs, sems):
        pass

    def finish(ins, outs, sems):
        for cp in copies_of(ins, outs, *sems):
            cp.wait()

    return _Comm(arrays, out_shapes, [n_copies, n_copies], start, middle, finish)


def _sibling_comm(grads):
    def copies_of(ins, outs, send, recv):
        x, y, c = _position()
        return [
            pltpu.make_async_remote_copy(
                src_ref=ins[t].at[:, pl.ds(1 - c, 1)],
                dst_ref=outs[t],
                send_sem=send.at[t],
                recv_sem=recv.at[t],
                device_id=(x, y, 1 - c),
                device_id_type=MESH,
            )
            for t in range(len(ins))
        ]

    return _exchange_comm(grads, [SDS((4, 1) + g.shape[2:], g.dtype) for g in grads], len(grads), copies_of)


def _chips_comm(parts):
    def copies_of(ins, outs, send, recv):
        x, y, c = _position()
        return [
            pltpu.make_async_remote_copy(
                src_ref=ins[t].at[pl.ds(2 * px + py, 1)],
                dst_ref=outs[t].at[pl.ds(k, 1)],
                send_sem=send.at[3 * t + k],
                recv_sem=recv.at[3 * t + k],
                device_id=(px, py, c),
                device_id_type=MESH,
            )
            for t in range(len(ins))
            for k, (px, py) in enumerate(_other_chips(x, y))
        ]

    return _exchange_comm(parts, [SDS((3,) + p.shape[1:], p.dtype) for p in parts], 3 * len(parts), copies_of)


def _sum_with_sibling(grad, got, core, *, name):
    rows = grad.shape[2]

    def body(core_ref, a_ref, b_ref, o_ref):
        o_ref[...] = (a_ref[...].astype(F32) + b_ref[...].astype(F32)).astype(o_ref.dtype)

    return pl.pallas_call(
        body,
        name=name,
        grid_spec=pltpu.PrefetchScalarGridSpec(
            num_scalar_prefetch=1,
            grid=(4,),
            in_specs=[
                pl.BlockSpec((None, None, rows, D), lambda q, core_ref: (q, core_ref[0], 0, 0)),
                pl.BlockSpec((None, None, rows, D), lambda q, core_ref: (q, 0, 0, 0)),
            ],
            out_specs=pl.BlockSpec((None, rows, D), lambda q, core_ref: (q, 0, 0)),
        ),
        out_shape=SDS((4, rows, D), grad.dtype),
        compiler_params=_params("parallel"),
    )(core, grad, got)


def _sum_chips(part, got, chip, *, name):
    rows = part.shape[1]

    def body(chip_ref, a_ref, b_ref, o_ref):
        o_ref[...] = ((a_ref[...].astype(F32) + b_ref[0].astype(F32)) + b_ref[1].astype(F32)) + b_ref[2].astype(F32)

    return pl.pallas_call(
        body,
        name=name,
        grid_spec=pltpu.PrefetchScalarGridSpec(
            num_scalar_prefetch=1,
            grid=(1,),
            in_specs=[
                pl.BlockSpec((None, rows, D), lambda i, chip_ref: (chip_ref[0], 0, 0)),
                pl.BlockSpec((3, rows, D), lambda i, chip_ref: (0, 0, 0)),
            ],
            out_specs=pl.BlockSpec((rows, D), lambda i, chip_ref: (0, 0)),
        ),
        out_shape=SDS((rows, D), F32),
        compiler_params=_params("arbitrary"),
    )(chip, part, got)


def _all_reduce_small(pack, *, name):
    rows = pack.shape[1]
    relations = [(kx, ky, kc) for kx in (0, 1) for ky in (0, 1) for kc in (0, 1)][1:]

    def body(in_ref, out_ref, landed, send1, recv1, send2, recv2):
        x, y, c = _position()
        mine = 4 * x + 2 * y + c

        def peer(rel):
            kx, ky, kc = rel
            return (1 - x if kx else x, 1 - y if ky else y, 1 - c if kc else c)

        first = []
        for k, rel in enumerate(relations):
            px, py, pc = peer(rel)
            cp = pltpu.make_async_remote_copy(
                src_ref=in_ref.at[4 * px + 2 * py + pc],
                dst_ref=landed.at[k],
                send_sem=send1.at[k],
                recv_sem=recv1.at[k],
                device_id=(px, py, pc),
                device_id_type=MESH,
            )
            cp.start()
            first.append(cp)
        total = in_ref[mine]
        for k, cp in enumerate(first):
            cp.wait_recv()
            total = total + landed[k]
        out_ref[mine] = total
        second = []
        for k, rel in enumerate(relations):
            cp = pltpu.make_async_remote_copy(
                src_ref=out_ref.at[mine],
                dst_ref=out_ref.at[mine],
                send_sem=send2.at[k],
                recv_sem=recv2.at[k],
                device_id=peer(rel),
                device_id_type=MESH,
            )
            cp.start()
            second.append(cp)
        for k, rel in enumerate(relations):
            px, py, pc = peer(rel)
            got = out_ref.at[4 * px + 2 * py + pc]
            pltpu.make_async_remote_copy(
                src_ref=got, dst_ref=got, send_sem=send2.at[k], recv_sem=recv2.at[k], device_id=peer(rel), device_id_type=MESH
            ).wait_recv()
        for cp in first + second:
            cp.wait_send()

    vmem = pl.BlockSpec(memory_space=pltpu.VMEM)
    return pl.pallas_call(
        body,
        name=name,
        in_specs=[vmem],
        out_specs=vmem,
        out_shape=SDS(pack.shape, F32),
        scratch_shapes=[
            pltpu.VMEM((7, rows, D), F32),
            pltpu.SemaphoreType.DMA((7,)),
            pltpu.SemaphoreType.DMA((7,)),
            pltpu.SemaphoreType.DMA((7,)),
            pltpu.SemaphoreType.DMA((7,)),
        ],
        compiler_params=pltpu.CompilerParams(vmem_limit_bytes=VMEM_LIMIT_BYTES),
    )(pack)


def _pack(arrays, rows):
    flat = jnp.concatenate([a.reshape(-1).astype(F32) for a in arrays])
    return jnp.pad(flat, (0, rows * D - flat.shape[0])).reshape(rows, D)


def _unpack(pack, shapes):
    flat = pack.reshape(-1)
    out, off = [], 0
    for sh in shapes:
        size = 1
        for dim in sh:
            size *= dim
        out.append(flat[off : off + size].reshape(sh))
        off += size
    return out


def _block_diag_pairs(w):
    w = w.reshape(N_RNN_TILES, 2, HEAD_DIM, HEAD_DIM)
    z = jnp.zeros_like(w[:, 0])
    top = jnp.concatenate([w[:, 0], z], axis=2)
    bot = jnp.concatenate([z, w[:, 1]], axis=2)
    return jnp.concatenate([top, bot], axis=1)


def _diag_blocks(w2):
    a = w2[:, :HEAD_DIM, :HEAD_DIM]
    b = w2[:, HEAD_DIM:, HEAD_DIM:]
    return jnp.stack([a, b], axis=1).reshape(RNN_HEADS, HEAD_DIM, HEAD_DIM)


BIG = ("w_in", "w_branch_a", "w_branch_b", "w_out", "w_up", "w_down")
TRANSPOSED = ("w_in", "w_up")
SMALL = (
    "norm_mix_g", "conv_w", "conv_b", "lru_w_a", "lru_b_a", "lru_w_x", "lru_b_x", "lru_lambda",
    "sgu_ln_g", "sgu_ln_b", "sgu_w_s", "sgu_b_s", "norm_ffn_g", "final_norm_g",
)
WEIGHTS = (
    "norm_mix_g", "w_in", "conv_w", "conv_b", "lru_w_a", "lru_b_a", "lru_w_x", "lru_b_x", "lru_lambda", "sgu_ln_g",
    "sgu_ln_b", "sgu_w_s", "sgu_b_s", "w_branch_a", "w_branch_b", "w_out", "norm_ffn_g", "w_up", "w_down", "final_norm_g",
)

TM = 512
TM_NT = 1024
TN_IN = 1664
TN_UP = 2048
TKA = 512
TKA_PIECES = 256
TC = 512
TB = 256
TR = 256


GATHERS_RIDING = (
    {
        "in_proj": (1.0, [(0, "w_branch_a"), (0, "w_branch_b"), (0, "w_out"), (0, "w_up")]),
        "branch_a_fwd": (0.85, [(1, "w_in")]),
        "ffn_up": (1.0, [(0, "w_down")]),
        "ffn_down": (0.9, [(1, "w_branch_a"), (1, "w_branch_b"), (1, "w_out")]),
    },
    {"in_proj": (0.7, [(1, "w_up")]), "branch_a_fwd": (0.7, [(1, "w_down")])},
)


def _layer_forward(l, x, p, w, shards):
    def run(key, fn, *args, **kw):
        if key not in GATHERS_RIDING[l]:
            return fn(*args, **kw)
        pass_on_at, riding = GATHERS_RIDING[l][key]
        out, got = fn(*args, comm=_gather_comm([shards[l2][n2] for l2, n2 in riding], pass_on_at), **kw)
        for (l2, n2), full in zip(riding, got):
            w[l2][n2] = full.reshape(-1, D)
        return out

    proj, h = run("in_proj", _norm_matmul_nt, x, p["norm_mix_g"], w[l]["w_in"], tm=TM_NT, tn=TN_IN, name=f"in_proj_{l}")
    hseq, ya_pre = run(
        "branch_a_fwd", _branch_a_fwd, proj, p["conv_w"], p["conv_b"], p["wa2"], p["lru_b_a"], p["wx2"], p["lru_b_x"],
        p["lru_lambda"], tc=TC, name=f"branch_a_fwd_{l}",
    )
    yb_pre = _sgu_fwd(proj, p["sgu_ln_g"], p["sgu_ln_b"], p["wm"], p["sgu_bias"], tb=TB, name=f"sgu_fwd_{l}")
    x1, ya, yb = _merge_fwd(
        ya_pre, yb_pre, proj, x, w[l]["w_branch_a"], w[l]["w_branch_b"], w[l]["w_out"], tm=TM, name=f"merge_fwd_{l}"
    )
    f_pre, h2 = run("ffn_up", _norm_matmul_nt, x1, p["norm_ffn_g"], w[l]["w_up"], tm=TM_NT, tn=TN_UP, name=f"ffn_up_{l}")
    x2 = run("ffn_down", _matmul_nn_res, f_pre, w[l]["w_down"], x1, relu2=True, tm=TM, name=f"ffn_down_{l}")
    saved = dict(x=x, h=h, proj=proj, hseq=hseq, ya_pre=ya_pre, yb_pre=yb_pre, ya=ya, yb=yb, x1=x1, h2=h2, f_pre=f_pre)
    return x2, saved


def _layer_backward(l, dx2, dx2b, sv, p, w, core, waiting):
    parts, from_chips = {}, {}

    def by_device(g):
        return g.reshape(4, 2, -1, D)

    def with_sibling(name, g, got):
        parts[name] = _sum_with_sibling(by_device(g), got, core, name=f"sum_sibling_{name}_{l}")

    df_pre = _matmul_nt_drelu2(dx2b, w["w_down"], sv["f_pre"], tm=TM_NT, tn=TN_UP, name=f"ffn_down_bwd_{l}")
    g_down = _matmul_tn([sv["f_pre"]], dx2b, relu2=True, tka=TKA, name=f"grad_w_down_{l}")
    g_up, (got,) = _matmul_tn(
        [df_pre], sv["h2"], relu2=False, tka=TKA, name=f"grad_w_up_{l}", comm=_sibling_comm([by_device(g_down)])
    )
    with_sibling("w_down", g_down, got)
    (dx1, dx1b, g_norm_ffn), (got, from_chips[l, "w_down"]) = _matmul_nn_rmsnorm_bwd(
        [df_pre], w["w_up"], sv["x1"], p["norm_ffn_g"], dx2, tm=TM, name=f"ffn_up_bwd_{l}",
        comm=_merge_comms([_sibling_comm([by_device(g_up)]), _chips_comm([parts["w_down"]])]),
    )
    with_sibling("w_up", g_up, got)
    (merged, dya, dyb, dga, dgb, dya_pre, dyb_pre), (from_chips[l, "w_up"],) = _merge_bwd(
        dx1b, sv["ya"], sv["yb"], sv["proj"], w["w_branch_a"], w["w_branch_b"], w["w_out"], tm=TM, name=f"merge_bwd_{l}",
        comm=_chips_comm([parts["w_up"]]),
    )
    g_out = _matmul_tn([merged], dx1b, relu2=False, tka=TKA, name=f"grad_w_out_{l}")
    g_ba = _matmul_tn([sv["ya_pre"]], dya, relu2=False, tka=TKA_PIECES, name=f"grad_w_branch_a_{l}")
    g_bb = _matmul_tn([sv["yb_pre"]], dyb, relu2=False, tka=TKA, name=f"grad_w_branch_b_{l}")
    branch = (("w_out", g_out), ("w_branch_a", g_ba), ("w_branch_b", g_bb))
    (du, dv, g_ws, g_bs, g_lng, g_lnb), got = _sgu_bwd(
        dyb_pre, sv["proj"], p["sgu_ln_g"], p["sgu_ln_b"], p["wm"], p["wmt"], p["sgu_bias"], p["mask"], tb=TB,
        name=f"sgu_bwd_{l}", comm=_sibling_comm([by_device(g) for _, g in branch]),
    )
    for (name, g), landed in zip(branch, got):
        with_sibling(name, g, landed)
    riding = [((l, name), parts[name]) for name, _ in branch] + list(waiting)
    (dxr, dgr, g_cw, g_cb, g_ba_, g_bx, g_lam, g_wa2, g_wx2), got = _branch_a_bwd(
        dya_pre, sv["proj"], sv["hseq"], p["conv_w"], p["conv_b"], p["wa2"], p["lru_b_a"], p["wx2"], p["lru_b_x"],
        p["lru_lambda"], p["wa2t"], p["wx2t"], tc=TC, name=f"branch_a_bwd_{l}", comm=_chips_comm([part for _, part in riding]),
    )
    for (key, _), landed in zip(riding, got):
        from_chips[key] = landed
    dproj = [dxr, dgr, du, dv, dga, dgb]
    g_in = _matmul_tn(dproj, sv["h"], relu2=False, tka=TKA_PIECES, name=f"grad_w_in_{l}")
    (dx, dxb, g_norm_mix), (got,) = _matmul_nn_rmsnorm_bwd(
        dproj, w["w_in"], sv["x"], p["norm_mix_g"], dx1, tm=TM, name=f"in_proj_bwd_{l}", comm=_sibling_comm([by_device(g_in)])
    )
    with_sibling("w_in", g_in, got)
    small = dict(
        norm_mix_g=g_norm_mix[0], conv_w=g_cw, conv_b=g_cb[0], lru_w_a=_diag_blocks(g_wa2), lru_b_a=g_ba_.reshape(RNN_HEADS, HEAD_DIM),
        lru_w_x=_diag_blocks(g_wx2), lru_b_x=g_bx.reshape(RNN_HEADS, HEAD_DIM), lru_lambda=g_lam[0], sgu_ln_g=g_lng[0],
        sgu_ln_b=g_lnb[0], sgu_w_s=g_ws, sgu_b_s=g_bs[:, :, 0], norm_ffn_g=g_norm_ffn[0],
    )
    return dx, dxb, small, parts, from_chips


def _prepare_small(l, given):
    chunk_id = jnp.arange(SGU_BLOCK) // CHUNK
    mask = (chunk_id[:, None] >= chunk_id[None, :]).astype(F32)
    wm = given["sgu_w_s"][l] * mask
    wa2 = _block_diag_pairs(given["lru_w_a"][l])
    wx2 = _block_diag_pairs(given["lru_w_x"][l])
    row = lambda a: a.reshape(1, -1)
    return dict(
        norm_mix_g=row(given["norm_mix_g"][l]),
        norm_ffn_g=row(given["norm_ffn_g"][l]),
        conv_w=given["conv_w_full"][l],
        conv_b=row(given["conv_b"][l]),
        wa2=wa2.astype(BF16),
        wx2=wx2.astype(BF16),
        wa2t=jnp.swapaxes(wa2, 1, 2).astype(BF16),
        wx2t=jnp.swapaxes(wx2, 1, 2).astype(BF16),
        lru_b_a=row(given["lru_b_a"][l]),
        lru_b_x=row(given["lru_b_x"][l]),
        lru_lambda=row(given["lru_lambda"][l]),
        sgu_ln_g=row(given["sgu_ln_g"][l]),
        sgu_ln_b=row(given["sgu_ln_b"][l]),
        wm=wm.astype(BF16),
        wmt=jnp.swapaxes(wm, 1, 2).astype(BF16),
        sgu_bias=jnp.broadcast_to(given["sgu_b_s"][l][:, :, None], (SGU_GROUPS, SGU_BLOCK, LANES)),
        mask=mask,
    )


def _step(given):
    x_idx, y_idx, c_idx = _position()
    dev = 4 * x_idx + 2 * y_idx + c_idx
    core = c_idx.astype(jnp.int32).reshape(1)
    chip = (2 * x_idx + y_idx).astype(jnp.int32).reshape(1)

    shards = []
    for l in range(DEPTH):
        shards.append({name: (given[name][l].T if name in TRANSPOSED else given[name][l]).astype(BF16)[None] for name in BIG})
    conv_mine = given["conv_w"].reshape(1, DEPTH * CONV_WIDTH, D_RNN // N_DEV)
    w_in_first, conv_all = _comm_only(_gather_comm([shards[0]["w_in"], conv_mine]), name="gather_first")
    weights = [{"w_in": w_in_first.reshape(-1, D)}, {}]
    conv_all = conv_all.reshape(N_DEV, DEPTH, CONV_WIDTH, D_RNN // N_DEV)
    given = dict(given, conv_w_full=jnp.moveaxis(conv_all, 0, 2).reshape(DEPTH, CONV_WIDTH, D_RNN))

    small_params = [_prepare_small(l, given) for l in range(DEPTH)]
    x = given["x"][0]
    saved = []
    for l in range(DEPTH):
        x, sv = _layer_forward(l, x, small_params[l], weights, shards)
        saved.append(sv)
    dx, dxb, g_final, loss = _final_loss(x, given["final_norm_g"].reshape(1, D), given["loss_target"][0], tm=TM, name="final_loss")
    small_grads, parts, from_chips, waiting = [None] * DEPTH, [None] * DEPTH, {}, []
    for l in reversed(range(DEPTH)):
        dx, dxb, small_grads[l], parts[l], got = _layer_backward(
            l, dx, dxb, saved[l], small_params[l], weights[l], core, waiting
        )
        from_chips.update(got)
        waiting = [((l, "w_in"), parts[l]["w_in"])]
    (from_chips[0, "w_in"],) = _comm_only(_chips_comm([parts[0]["w_in"]]), name="grads_to_chips_last")

    small_list = []
    for name in SMALL[:-1]:
        small_list.append(jnp.stack([small_grads[l][name] for l in range(DEPTH)]))
    small_list += [g_final[0], loss[0, :1]]
    small_shapes = [a.shape for a in small_list]
    pack = _pack(small_list, SMALL_ROWS).reshape(N_DEV, SMALL_ROWS_PER_DEV, D)
    summed = _unpack(_all_reduce_small(pack, name="all_reduce_small"), small_shapes)
    loss_total = summed[-1][0]
    grads = dict(zip(SMALL, summed[:-1]))
    cw = grads["conv_w"].reshape(DEPTH, CONV_WIDTH, N_DEV, D_RNN // N_DEV)
    grads["conv_w"] = lax.dynamic_index_in_dim(cw, dev, axis=2, keepdims=False)

    delta, new_m, new_v = {}, {}, {}
    for name in BIG:
        w, m, v = given[name], given["m_" + name], given["v_" + name]
        mine = [parts[l][name] for l in range(DEPTH)]
        theirs = [from_chips[l, name] for l in range(DEPTH)]
        if name in TRANSPOSED:
            sums = [_sum_chips(mine[l], theirs[l], chip, name=f"sum_chips_{name}_{l}").T for l in range(DEPTH)]
            out = _adamw_layers(w, sums, m, v, tr=TR, name=f"adamw_{name}")
        else:
            out = _adamw_reduced(w, mine, theirs, m, v, chip, tr=TR, name=f"adamw_{name}")
        grads[name], delta[name], new_m[name], new_v[name] = out
    two_d = lambda a: a.reshape(1, -1) if a.ndim == 1 else a
    groups = [tuple(two_d(a) for a in (given[n], grads[n], given["m_" + n], given["v_" + n])) for n in SMALL]
    for n, (d, m2, v2) in zip(SMALL, _adamw_small(groups, name="adamw_small")):
        shape = given[n].shape
        delta[n], new_m[n], new_v[n] = d.reshape(shape), m2.reshape(shape), v2.reshape(shape)

    return (
        loss_total, dx[None],
        *[grads[n] for n in WEIGHTS], *[delta[n] for n in WEIGHTS], *[new_m[n] for n in WEIGHTS], *[new_v[n] for n in WEIGHTS],
    )


def kernel(x, norm_mix_g, w_in, conv_w, conv_b, lru_w_a, lru_b_a, lru_w_x, lru_b_x, lru_lambda, sgu_ln_g, sgu_ln_b, sgu_w_s, sgu_b_s, w_branch_a, w_branch_b, w_out, norm_ffn_g, w_up, w_down, final_norm_g, loss_target, m_norm_mix_g, m_w_in, m_conv_w, m_conv_b, m_lru_w_a, m_lru_b_a, m_lru_w_x, m_lru_b_x, m_lru_lambda, m_sgu_ln_g, m_sgu_ln_b, m_sgu_w_s, m_sgu_b_s, m_w_branch_a, m_w_branch_b, m_w_out, m_norm_ffn_g, m_w_up, m_w_down, m_final_norm_g, v_norm_mix_g, v_w_in, v_conv_w, v_conv_b, v_lru_w_a, v_lru_b_a, v_lru_w_x, v_lru_b_x, v_lru_lambda, v_sgu_ln_g, v_sgu_ln_b, v_sgu_w_s, v_sgu_b_s, v_w_branch_a, v_w_branch_b, v_w_out, v_norm_ffn_g, v_w_up, v_w_down, v_final_norm_g):
    return _step(dict(locals()))
```

```python
import jax
import jax.numpy as jnp
from jax import lax
from jax.experimental import pallas as pl
from jax.experimental.pallas import tpu as pltpu

F32 = jnp.float32
BF16 = jnp.bfloat16
SDS = jax.ShapeDtypeStruct
MESH = pl.DeviceIdType.MESH

D = 1024
D_RNN = 1280
D_SGU = 1024
D_FF = 4096
D_IN = 2 * D_RNN + 2 * D_SGU + 2 * D
DEPTH = 2
RNN_HEADS = 20
HEAD_DIM = 64
CONV_WIDTH = 4
LRU_C = 8.0
SGU_GROUPS = 8
SGU_BLOCK = 128
CHUNK = 64
EPS = 1e-6
N_DEV = 8

ADAM_LR = 0.001
ADAM_B1 = 0.9
ADAM_B2 = 0.999
ADAM_EPS = 1e-08
ADAM_WD = 0.01
ADAM_STEP = 10

LANES = 128
SUBLANES = 8
VMEM_LIMIT_BYTES = 56 * 1024 * 1024

N_RNN_TILES = D_RNN // LANES
GRNN_BLK128 = D_RNN // LANES
U_BLK512 = (2 * D_RNN) // 512
V_BLK512 = (2 * D_RNN + D_SGU) // 512
GA_BLK512 = (2 * D_RNN + 2 * D_SGU) // 512
GB_BLK512 = (2 * D_RNN + 2 * D_SGU + D) // 512

SMALL_ROWS_PER_DEV = 80
SMALL_ROWS = N_DEV * SMALL_ROWS_PER_DEV


def _params(*sem):
    return pltpu.CompilerParams(dimension_semantics=sem, vmem_limit_bytes=VMEM_LIMIT_BYTES)


def _sigmoid(x):
    return 0.5 + 0.5 * jnp.tanh(0.5 * x)


_GELU_C = 0.7978845608028654
_GELU_K = 0.044715


def _gelu(x):
    t = jnp.tanh(_GELU_C * (x + _GELU_K * x * x * x))
    return 0.5 * x * (1.0 + t)


def _gelu_and_grad(x):
    t = jnp.tanh(_GELU_C * (x + _GELU_K * x * x * x))
    val = 0.5 * x * (1.0 + t)
    grad = 0.5 * (1.0 + t) + 0.5 * x * (1.0 - t * t) * _GELU_C * (1.0 + 3.0 * _GELU_K * x * x)
    return val, grad


def _one_minus_square(log_a, a):
    return -jnp.tanh(log_a) * (1.0 + a * a)


def _dot(a, b):
    return jnp.dot(a, b, preferred_element_type=F32)


def _dot_nt(a, b):
    return lax.dot_general(a, b, (((1,), (1,)), ((), ())), preferred_element_type=F32)


def _dot_tn(a, b):
    return lax.dot_general(a, b, (((0,), (0,)), ((), ())), preferred_element_type=F32)


def _norm_matmul_nt(x, g, w, *, tm, tn, name, comm=None):
    s, n = x.shape[0], w.shape[0]
    tm, tn = min(tm, s), min(tn, n)

    def body(x_ref, g_ref, w_ref, o_ref, h_ref):
        @pl.when(pl.program_id(1) == 0)
        def _():
            xv = x_ref[...]
            r = lax.rsqrt(jnp.mean(xv * xv, axis=-1, keepdims=True) + EPS)
            h_ref[...] = (xv * r * g_ref[...]).astype(BF16)

        o_ref[...] = _dot_nt(h_ref[...], w_ref[...]).astype(o_ref.dtype)

    return _call(
        body,
        (x, g, w),
        name=name,
        grid=(s // tm, n // tn),
        in_specs=[
            pl.BlockSpec((tm, D), lambda i, j: (i, 0)),
            pl.BlockSpec((1, D), lambda i, j: (0, 0)),
            pl.BlockSpec((tn, D), lambda i, j: (j, 0)),
        ],
        out_specs=[pl.BlockSpec((tm, tn), lambda i, j: (i, j)), pl.BlockSpec((tm, D), lambda i, j: (i, 0))],
        out_shape=[SDS((s, n), BF16), SDS((s, D), BF16)],
        semantics=("parallel", "arbitrary"),
        comm=comm,
    )


def _matmul_nn_res(a, w, res, *, relu2, tm, name, comm=None):
    s, k = a.shape
    tm = min(tm, s)

    def body(a_ref, w_ref, r_ref, o_ref):
        av = a_ref[...]
        if relu2:
            t = jnp.maximum(av.astype(F32), 0.0)
            av = (t * t).astype(BF16)
        o_ref[...] = r_ref[...] + _dot(av, w_ref[...])

    return _call(
        body,
        (a, w, res),
        name=name,
        grid=(s // tm,),
        in_specs=[
            pl.BlockSpec((tm, k), lambda i: (i, 0)),
            pl.BlockSpec((k, D), lambda i: (0, 0)),
            pl.BlockSpec((tm, D), lambda i: (i, 0)),
        ],
        out_specs=pl.BlockSpec((tm, D), lambda i: (i, 0)),
        out_shape=SDS((s, D), F32),
        semantics=("parallel",),
        comm=comm,
    )


def _matmul_nt_drelu2(a, w, pre, *, tm, tn, name):
    s, n = a.shape[0], w.shape[0]
    tm, tn = min(tm, s), min(tn, n)

    def body(a_ref, w_ref, p_ref, o_ref):
        d = _dot_nt(a_ref[...], w_ref[...])
        o_ref[...] = (d * (2.0 * jnp.maximum(p_ref[...].astype(F32), 0.0))).astype(o_ref.dtype)

    return pl.pallas_call(
        body,
        name=name,
        grid=(s // tm, n // tn),
        in_specs=[
            pl.BlockSpec((tm, D), lambda i, j: (i, 0)),
            pl.BlockSpec((tn, D), lambda i, j: (j, 0)),
            pl.BlockSpec((tm, tn), lambda i, j: (i, j)),
        ],
        out_specs=pl.BlockSpec((tm, tn), lambda i, j: (i, j)),
        out_shape=SDS((s, n), BF16),
        compiler_params=_params("parallel", "arbitrary"),
    )(a, w, pre)


def _matmul_tn(a_list, b, *, relu2, tka, name, comm=None):
    s = b.shape[0]
    n = len(a_list)
    nblk = [a.shape[1] // tka for a in a_list]
    starts = [sum(nblk[:p]) for p in range(n)]

    def body(*refs):
        a_refs, b_ref, o_ref = refs[:n], refs[n], refs[n + 1]
        i = pl.program_id(0)
        for p in range(n):

            @pl.when((i >= starts[p]) & (i < starts[p] + nblk[p]))
            def _(p=p):
                av = a_refs[p][...]
                if relu2:
                    t = jnp.maximum(av.astype(F32), 0.0)
                    av = (t * t).astype(BF16)
                o_ref[...] = _dot_tn(av, b_ref[...]).astype(o_ref.dtype)

    def piece_spec(p):
        return pl.BlockSpec((s, tka), lambda i: (0, jnp.clip(i - starts[p], 0, nblk[p] - 1)))

    return _call(
        body,
        (*a_list, b),
        name=name,
        grid=(sum(nblk),),
        in_specs=[piece_spec(p) for p in range(n)] + [pl.BlockSpec((s, D), lambda i: (0, 0))],
        out_specs=pl.BlockSpec((tka, D), lambda i: (i, 0)),
        out_shape=SDS((sum(nblk) * tka, D), BF16),
        semantics=("parallel",),
        comm=comm,
    )


def _matmul_nn_rmsnorm_bwd(a_list, w, x, g, res, *, tm, name, comm=None):
    s = x.shape[0]
    tm = min(tm, s)
    n = len(a_list)
    widths = [a.shape[1] for a in a_list]
    offs = [sum(widths[:p]) for p in range(n)]
    k = sum(widths)

    def body(*refs):
        a_refs = refs[:n]
        w_ref, x_ref, g_ref, r_ref, dx_ref, dxb_ref, dg_ref = refs[n:]

        @pl.when(pl.program_id(0) == 0)
        def _():
            dg_ref[...] = jnp.zeros_like(dg_ref)

        dh = _dot(a_refs[0][...], w_ref[0 : widths[0], :])
        for p in range(1, n):
            dh += _dot(a_refs[p][...], w_ref[offs[p] : offs[p] + widths[p], :])
        xv = x_ref[...]
        r = lax.rsqrt(jnp.mean(xv * xv, axis=-1, keepdims=True) + EPS)
        xhat = xv * r
        dxh = dh * g_ref[...]
        dx = r_ref[...] + r * (dxh - xhat * jnp.mean(dxh * xhat, axis=-1, keepdims=True))
        dx_ref[...] = dx
        dxb_ref[...] = dx.astype(BF16)
        dg_ref[...] += jnp.sum(dh * xhat, axis=0, keepdims=True)

    act = pl.BlockSpec((tm, D), lambda i: (i, 0))
    vec = pl.BlockSpec((1, D), lambda i: (0, 0))
    return _call(
        body,
        (*a_list, w, x, g, res),
        name=name,
        grid=(s // tm,),
        in_specs=[pl.BlockSpec((tm, wd), lambda i: (i, 0)) for wd in widths]
        + [pl.BlockSpec((k, D), lambda i: (0, 0), pipeline_mode=pl.Buffered(1)), act, vec, act],
        out_specs=[act, act, vec],
        out_shape=[SDS((s, D), F32), SDS((s, D), BF16), SDS((1, D), F32)],
        semantics=("arbitrary",),
        comm=comm,
    )


def _rows_before(ext, k):
    if k == 0:
        return ext[SUBLANES:, :]
    return pltpu.roll(ext, k, 0)[SUBLANES:, :]


def _rows_after(ext, k, n):
    if k == 0:
        return ext[:n, :]
    return pltpu.roll(ext, n + SUBLANES - k, 0)[:n, :]


def _scan_forward(a, b, n):
    row = lax.broadcasted_iota(jnp.int32, a.shape, 0)
    d = 1
    while d < n:
        if d < SUBLANES:
            m = row >= d
            a_s = jnp.where(m, pltpu.roll(a, d, 0), 1.0)
            b_s = jnp.where(m, pltpu.roll(b, d, 0), 0.0)
            b = a * b_s + b
            a = a * a_s
        else:
            b = jnp.concatenate([b[:d], a[d:] * b[: n - d] + b[d:]], axis=0)
            a = jnp.concatenate([a[:d], a[d:] * a[: n - d]], axis=0)
        d *= 2
    return a, b


def _scan_backward(a, b, n):
    row = lax.broadcasted_iota(jnp.int32, a.shape, 0)
    d = 1
    while d < n:
        if d < SUBLANES:
            m = row < n - d
            a_s = jnp.where(m, pltpu.roll(a, n - d, 0), 1.0)
            b_s = jnp.where(m, pltpu.roll(b, n - d, 0), 0.0)
            b = a * b_s + b
            a = a * a_s
        else:
            b = jnp.concatenate([a[: n - d] * b[d:] + b[: n - d], b[n - d :]], axis=0)
            a = jnp.concatenate([a[: n - d] * a[d:], a[n - d :]], axis=0)
        d *= 2
    return b


def _softplus_neg(lam):
    z = -lam
    return jnp.maximum(z, 0.0) + jnp.log1p(jnp.exp(-jnp.abs(z)))


def _conv_and_gates(xc, xprev, cw_ref, cb_ref, wa_ref, ba_ref, wx_ref, bx_ref, lam_ref):
    ext = jnp.concatenate([xprev, xc], axis=0)
    x1, x2, x3 = _rows_before(ext, 1), _rows_before(ext, 2), _rows_before(ext, 3)
    xr = cb_ref[...] + x3 * cw_ref[0:1, :] + x2 * cw_ref[1:2, :] + x1 * cw_ref[2:3, :] + xc * cw_ref[3:4, :]
    xrb = xr.astype(BF16)
    r = _sigmoid(_dot(xrb, wa_ref[...]) + ba_ref[...])
    i = _sigmoid(_dot(xrb, wx_ref[...]) + bx_ref[...])
    sp = _softplus_neg(lam_ref[...])
    log_a = (-LRU_C * r) * sp
    a = jnp.exp(log_a)
    return xr, (x1, x2, x3), r, i, a, _one_minus_square(log_a, a)


def _branch_a_fwd(proj, cw, cb, wa2, ba, wx2, bx, lam, *, tc, name, comm=None):
    s = proj.shape[0]
    tc = min(tc, s)

    def body(x_ref, g_ref, cw_ref, cb_ref, wa_ref, ba_ref, wx_ref, bx_ref, lam_ref, h_ref, y_ref, xprev, hlast):
        @pl.when(pl.program_id(1) == 0)
        def _():
            xprev[...] = jnp.zeros_like(xprev)
            hlast[...] = jnp.zeros_like(hlast)

        xc = x_ref[...].astype(F32)
        xr, _, r, i, a, om = _conv_and_gates(xc, xprev[...], cw_ref, cb_ref, wa_ref, ba_ref, wx_ref, bx_ref, lam_ref)
        xprev[...] = xc[tc - SUBLANES :, :]
        u = jnp.sqrt(om) * (i * xr)
        acum, b = _scan_forward(a, u, tc)
        h = b + acum * hlast[SUBLANES - 1 : SUBLANES, :]
        hlast[...] = h[tc - SUBLANES :, :]
        h_ref[...] = h
        y_ref[...] = (h * _gelu(g_ref[...].astype(F32))).astype(BF16)

    tile = lambda j, c: (0, j)
    return _call(
        body,
        (proj, proj, cw, cb, wa2, ba, wx2, bx, lam),
        name=name,
        grid=(N_RNN_TILES, s // tc),
        in_specs=[
            pl.BlockSpec((tc, LANES), lambda j, c: (c, j)),
            pl.BlockSpec((tc, LANES), lambda j, c: (c, GRNN_BLK128 + j)),
            pl.BlockSpec((CONV_WIDTH, LANES), tile),
            pl.BlockSpec((1, LANES), tile),
            pl.BlockSpec((None, LANES, LANES), lambda j, c: (j, 0, 0)),
            pl.BlockSpec((1, LANES), tile),
            pl.BlockSpec((None, LANES, LANES), lambda j, c: (j, 0, 0)),
            pl.BlockSpec((1, LANES), tile),
            pl.BlockSpec((1, LANES), tile),
        ],
        out_specs=[pl.BlockSpec((tc, LANES), lambda j, c: (c, j)), pl.BlockSpec((tc, LANES), lambda j, c: (c, j))],
        out_shape=[SDS((s, D_RNN), F32), SDS((s, D_RNN), BF16)],
        scratch_shapes=[pltpu.VMEM((SUBLANES, LANES), F32), pltpu.VMEM((SUBLANES, LANES), F32)],
        semantics=("parallel", "arbitrary"),
        comm=comm,
    )


def _branch_a_bwd(dy, proj, h, cw, cb, wa2, ba, wx2, bx, lam, wa2t, wx2t, *, tc, name, comm=None):
    s = proj.shape[0]
    tc = min(tc, s)
    nc = s // tc
    halo16 = tc // 16
    halo8 = tc // SUBLANES

    def body(dy_ref, x_ref, xh_ref, g_ref, h_ref, hh_ref, cw_ref, cb_ref, wa_ref, ba_ref, wx_ref, bx_ref, lam_ref,
             wat_ref, wxt_ref, dx_ref, dg_ref, dcw_ref, dcb_ref, dba_ref, dbx_ref, dlam_ref, dwa_ref, dwx_ref,
             carry, dxr_next):
        cc = pl.program_id(1)
        ct = nc - 1 - cc

        @pl.when(cc == 0)
        def _():
            carry[...] = jnp.zeros_like(carry)
            dxr_next[...] = jnp.zeros_like(dxr_next)
            for ref in (dcw_ref, dcb_ref, dba_ref, dbx_ref, dlam_ref, dwa_ref, dwx_ref):
                ref[...] = jnp.zeros_like(ref)

        xc = x_ref[...].astype(F32)
        xprev = jnp.where(ct > 0, xh_ref[SUBLANES:, :].astype(F32), 0.0)
        xr, (x1, x2, x3), r, i, a, om = _conv_and_gates(
            xc, xprev, cw_ref, cb_ref, wa_ref, ba_ref, wx_ref, bx_ref, lam_ref
        )
        inv_norm = lax.rsqrt(om)
        norm = om * inv_norm
        row = lax.broadcasted_iota(jnp.int32, xc.shape, 0)

        hv = h_ref[...]
        ge, ge_grad = _gelu_and_grad(g_ref[...].astype(F32))
        dyv = dy_ref[...].astype(F32)
        dg_ref[...] = (dyv * hv * ge_grad).astype(dg_ref.dtype)
        dh = dyv * ge

        b = dh + jnp.where(row == tc - 1, carry[0:1, :], 0.0)
        a_next = jnp.where(row < tc - 1, pltpu.roll(a, tc - 1, 0), 0.0)
        gadj = _scan_backward(a_next, b, tc)
        carry[...] = (a * gadj)[:SUBLANES, :]

        hprev_first = jnp.where(ct > 0, hh_ref[SUBLANES - 1 : SUBLANES, :], 0.0)
        hprev = jnp.where(row >= 1, pltpu.roll(hv, 1, 0), hprev_first)
        da = gadj * hprev
        ix = i * xr
        dnorm = gadj * ix
        di = gadj * norm * xr
        dlog_a = da * a - dnorm * (1.0 - om) * inv_norm
        sp = _softplus_neg(lam_ref[...])
        dr = dlog_a * (-LRU_C * sp)
        dsp = jnp.sum(dlog_a * (-LRU_C * r), axis=0, keepdims=True)
        dlam_ref[...] += dsp * (-_sigmoid(-lam_ref[...]))
        dza = dr * r * (1.0 - r)
        dzx = di * i * (1.0 - i)
        dzab, dzxb = dza.astype(BF16), dzx.astype(BF16)
        dxr = gadj * norm * i + _dot(dzab, wat_ref[...]) + _dot(dzxb, wxt_ref[...])
        xrb = xr.astype(BF16)
        dwa_ref[...] += _dot_tn(xrb, dzab)
        dwx_ref[...] += _dot_tn(xrb, dzxb)
        dba_ref[...] += jnp.sum(dza, axis=0, keepdims=True)
        dbx_ref[...] += jnp.sum(dzx, axis=0, keepdims=True)

        ext = jnp.concatenate([dxr, dxr_next[...]], axis=0)
        dx = (
            dxr * cw_ref[3:4, :]
            + _rows_after(ext, 1, tc) * cw_ref[2:3, :]
            + _rows_after(ext, 2, tc) * cw_ref[1:2, :]
            + _rows_after(ext, 3, tc) * cw_ref[0:1, :]
        )
        dxr_next[...] = dxr[:SUBLANES, :]
        dx_ref[...] = dx.astype(dx_ref.dtype)
        dcb_ref[...] += jnp.sum(dxr, axis=0, keepdims=True)
        dcw_ref[3:4, :] += jnp.sum(dxr * xc, axis=0, keepdims=True)
        dcw_ref[2:3, :] += jnp.sum(dxr * x1, axis=0, keepdims=True)
        dcw_ref[1:2, :] += jnp.sum(dxr * x2, axis=0, keepdims=True)
        dcw_ref[0:1, :] += jnp.sum(dxr * x3, axis=0, keepdims=True)

    tile = lambda j, c: (0, j)
    mat = lambda j, c: (j, 0, 0)
    cur = lambda j, c: (nc - 1 - c, j)
    vec = pl.BlockSpec((1, LANES), tile)
    matspec = pl.BlockSpec((None, LANES, LANES), mat)
    return _call(
        body,
        (dy, proj, proj, proj, h, h, cw, cb, wa2, ba, wx2, bx, lam, wa2t, wx2t),
        name=name,
        grid=(N_RNN_TILES, nc),
        in_specs=[
            pl.BlockSpec((tc, LANES), cur),
            pl.BlockSpec((tc, LANES), cur),
            pl.BlockSpec((16, LANES), lambda j, c: (jnp.maximum((nc - 1 - c) * halo16 - 1, 0), j)),
            pl.BlockSpec((tc, LANES), lambda j, c: (nc - 1 - c, GRNN_BLK128 + j)),
            pl.BlockSpec((tc, LANES), cur),
            pl.BlockSpec((SUBLANES, LANES), lambda j, c: (jnp.maximum((nc - 1 - c) * halo8 - 1, 0), j)),
            pl.BlockSpec((CONV_WIDTH, LANES), tile),
            vec,
            matspec,
            vec,
            matspec,
            vec,
            vec,
            matspec,
            matspec,
        ],
        out_specs=[
            pl.BlockSpec((tc, LANES), cur),
            pl.BlockSpec((tc, LANES), cur),
            pl.BlockSpec((CONV_WIDTH, LANES), tile),
            vec,
            vec,
            vec,
            vec,
            matspec,
            matspec,
        ],
        out_shape=[
            SDS((s, D_RNN), BF16),
            SDS((s, D_RNN), BF16),
            SDS((CONV_WIDTH, D_RNN), F32),
            SDS((1, D_RNN), F32),
            SDS((1, D_RNN), F32),
            SDS((1, D_RNN), F32),
            SDS((1, D_RNN), F32),
            SDS((N_RNN_TILES, LANES, LANES), F32),
            SDS((N_RNN_TILES, LANES, LANES), F32),
        ],
        scratch_shapes=[pltpu.VMEM((SUBLANES, LANES), F32), pltpu.VMEM((SUBLANES, LANES), F32)],
        semantics=("parallel", "arbitrary"),
        comm=comm,
    )


def _sgu_specs(tb):
    half = lambda blk: pl.BlockSpec((tb, 512), lambda n: (n, blk))
    return [half(U_BLK512), half(U_BLK512 + 1), half(V_BLK512), half(V_BLK512 + 1)]


def _sgu_normed(v, lng_ref, lnb_ref):
    gv, gv_grad = _gelu_and_grad(v)
    mu = jnp.mean(gv, axis=-1, keepdims=True)
    xc = gv - mu
    rs = lax.rsqrt(jnp.mean(xc * xc, axis=-1, keepdims=True) + EPS)
    xhat = xc * rs
    return xhat * lng_ref[...] + lnb_ref[...], xhat, rs, gv_grad


def _sgu_fwd(proj, lng, lnb, wm, bias, *, tb, name):
    s = proj.shape[0]
    tb = min(tb, s)

    def body(u0_ref, u1_ref, v0_ref, v1_ref, lng_ref, lnb_ref, wm_ref, bias_ref, y_ref):
        u = jnp.concatenate([u0_ref[...], u1_ref[...]], axis=1).astype(F32)
        v = jnp.concatenate([v0_ref[...], v1_ref[...]], axis=1).astype(F32)
        gu = _gelu(u)
        vn, _, _, _ = _sgu_normed(v, lng_ref, lnb_ref)
        vnb = vn.astype(BF16)
        for blk in range(tb // SGU_BLOCK):
            rows = slice(blk * SGU_BLOCK, (blk + 1) * SGU_BLOCK)
            for g in range(SGU_GROUPS):
                cols = slice(g * LANES, (g + 1) * LANES)
                mixed = _dot(wm_ref[g], vnb[rows, cols]) + bias_ref[g]
                y_ref[rows, cols] = (gu[rows, cols] * mixed).astype(BF16)

    const2 = lambda n: (0, 0)
    const3 = lambda n: (0, 0, 0)
    return pl.pallas_call(
        body,
        name=name,
        grid=(s // tb,),
        in_specs=_sgu_specs(tb)
        + [
            pl.BlockSpec((1, D_SGU), const2),
            pl.BlockSpec((1, D_SGU), const2),
            pl.BlockSpec((SGU_GROUPS, SGU_BLOCK, SGU_BLOCK), const3),
            pl.BlockSpec((SGU_GROUPS, SGU_BLOCK, LANES), const3),
        ],
        out_specs=pl.BlockSpec((tb, D_SGU), lambda n: (n, 0)),
        out_shape=SDS((s, D_SGU), BF16),
        compiler_params=_params("parallel"),
    )(proj, proj, proj, proj, lng, lnb, wm, bias)


def _sgu_bwd(dy, proj, lng, lnb, wm, wmt, bias, mask, *, tb, name, comm=None):
    s = proj.shape[0]
    tb = min(tb, s)
    nb = s // tb

    def body(dy_ref, u0_ref, u1_ref, v0_ref, v1_ref, lng_ref, lnb_ref, wm_ref, wmt_ref, bias_ref, mask_ref,
             du_ref, dv_ref, dws_ref, dbs_ref, dlng_ref, dlnb_ref, dvn_scr, dbs_acc):
        n = pl.program_id(0)

        @pl.when(n == 0)
        def _():
            dbs_acc[...] = jnp.zeros_like(dbs_acc)
            for ref in (dws_ref, dlng_ref, dlnb_ref):
                ref[...] = jnp.zeros_like(ref)

        u = jnp.concatenate([u0_ref[...], u1_ref[...]], axis=1).astype(F32)
        v = jnp.concatenate([v0_ref[...], v1_ref[...]], axis=1).astype(F32)
        gu, gu_grad = _gelu_and_grad(u)
        vn, xhat, rs, gv_grad = _sgu_normed(v, lng_ref, lnb_ref)
        vnb = vn.astype(BF16)
        dyv = dy_ref[...].astype(F32)
        for blk in range(tb // SGU_BLOCK):
            rows = slice(blk * SGU_BLOCK, (blk + 1) * SGU_BLOCK)
            for g in range(SGU_GROUPS):
                cols = slice(g * LANES, (g + 1) * LANES)
                vt = vnb[rows, cols]
                mixed = _dot(wm_ref[g], vt) + bias_ref[g]
                dyt = dyv[rows, cols]
                du_ref[rows, cols] = (dyt * mixed * gu_grad[rows, cols]).astype(BF16)
                dmix = dyt * gu[rows, cols]
                dmixb = dmix.astype(BF16)
                dvn_scr[rows, cols] = _dot(wmt_ref[g], dmixb)
                dws_ref[g] += _dot_nt(dmixb, vt) * mask_ref[...]
                dbs_acc[g] += dmix
        dvn = dvn_scr[...]
        dlng_ref[...] += jnp.sum(dvn * xhat, axis=0, keepdims=True)
        dlnb_ref[...] += jnp.sum(dvn, axis=0, keepdims=True)
        dxh = dvn * lng_ref[...]
        dgv = rs * (
            dxh - jnp.mean(dxh, axis=-1, keepdims=True) - xhat * jnp.mean(dxh * xhat, axis=-1, keepdims=True)
        )
        dv_ref[...] = (dgv * gv_grad).astype(BF16)

        @pl.when(n == nb - 1)
        def _():
            for g in range(SGU_GROUPS):
                dbs_ref[g] = jnp.broadcast_to(jnp.sum(dbs_acc[g], axis=-1, keepdims=True), (SGU_BLOCK, LANES))

    const2 = lambda n: (0, 0)
    const3 = lambda n: (0, 0, 0)
    gmat = pl.BlockSpec((SGU_GROUPS, SGU_BLOCK, SGU_BLOCK), const3)
    vec = pl.BlockSpec((1, D_SGU), const2)
    act = pl.BlockSpec((tb, D_SGU), lambda n: (n, 0))
    return _call(
        body,
        (dy, proj, proj, proj, proj, lng, lnb, wm, wmt, bias, mask),
        name=name,
        grid=(nb,),
        in_specs=[act] + _sgu_specs(tb) + [vec, vec, gmat, gmat, gmat, pl.BlockSpec((SGU_BLOCK, SGU_BLOCK), const2)],
        out_specs=[act, act, gmat, gmat, vec, vec],
        out_shape=[
            SDS((s, D_SGU), BF16),
            SDS((s, D_SGU), BF16),
            SDS((SGU_GROUPS, SGU_BLOCK, SGU_BLOCK), F32),
            SDS((SGU_GROUPS, SGU_BLOCK, LANES), F32),
            SDS((1, D_SGU), F32),
            SDS((1, D_SGU), F32),
        ],
        scratch_shapes=[pltpu.VMEM((tb, D_SGU), F32), pltpu.VMEM((SGU_GROUPS, SGU_BLOCK, LANES), F32)],
        semantics=("arbitrary",),
        comm=comm,
    )


def _gate_specs(tm):
    half = lambda blk: pl.BlockSpec((tm, 512), lambda i: (i, blk))
    return [half(GA_BLK512), half(GA_BLK512 + 1), half(GB_BLK512), half(GB_BLK512 + 1)]


def _merge_fwd(ya_pre, yb_pre, proj, x, w_ba, w_bb, w_out, *, tm, name):
    s = x.shape[0]
    tm = min(tm, s)

    def body(ya_ref, yb_ref, a0, a1, b0, b1, x_ref, wa_ref, wb_ref, wo_ref, x1_ref, yao_ref, ybo_ref):
        ya = _dot(ya_ref[...], wa_ref[...])
        yb = _dot(yb_ref[...], wb_ref[...])
        sa = _sigmoid(jnp.concatenate([a0[...], a1[...]], axis=1).astype(F32))
        sb = _sigmoid(jnp.concatenate([b0[...], b1[...]], axis=1).astype(F32))
        merged = sa * ya + sb * yb
        x1_ref[...] = x_ref[...] + _dot(merged.astype(BF16), wo_ref[...])
        yao_ref[...] = ya.astype(BF16)
        ybo_ref[...] = yb.astype(BF16)

    whole = lambda r: pl.BlockSpec((r, D), lambda i: (0, 0))
    act = pl.BlockSpec((tm, D), lambda i: (i, 0))
    return pl.pallas_call(
        body,
        name=name,
        grid=(s // tm,),
        in_specs=[pl.BlockSpec((tm, D_RNN), lambda i: (i, 0)), act] + _gate_specs(tm) + [act, whole(D_RNN), whole(D_SGU), whole(D)],
        out_specs=[act, act, act],
        out_shape=[SDS((s, D), F32), SDS((s, D), BF16), SDS((s, D), BF16)],
        compiler_params=_params("parallel"),
    )(ya_pre, yb_pre, proj, proj, proj, proj, x, w_ba, w_bb, w_out)


def _merge_bwd(dx1, ya, yb, proj, w_ba, w_bb, w_out, *, tm, name, comm=None):
    s = dx1.shape[0]
    tm = min(tm, s)

    def body(dx_ref, ya_ref, yb_ref, a0, a1, b0, b1, wa_ref, wb_ref, wo_ref,
             mg_ref, dya_ref, dyb_ref, dga_ref, dgb_ref, dyap_ref, dybp_ref):
        dm = _dot_nt(dx_ref[...], wo_ref[...])
        ya = ya_ref[...].astype(F32)
        yb = yb_ref[...].astype(F32)
        sa = _sigmoid(jnp.concatenate([a0[...], a1[...]], axis=1).astype(F32))
        sb = _sigmoid(jnp.concatenate([b0[...], b1[...]], axis=1).astype(F32))
        mg_ref[...] = (sa * ya + sb * yb).astype(BF16)
        dya = (dm * sa).astype(BF16)
        dyb = (dm * sb).astype(BF16)
        dya_ref[...] = dya
        dyb_ref[...] = dyb
        dga_ref[...] = (dm * ya * sa * (1.0 - sa)).astype(BF16)
        dgb_ref[...] = (dm * yb * sb * (1.0 - sb)).astype(BF16)
        dyap_ref[...] = _dot_nt(dya, wa_ref[...]).astype(BF16)
        dybp_ref[...] = _dot_nt(dyb, wb_ref[...]).astype(BF16)

    whole = lambda r: pl.BlockSpec((r, D), lambda i: (0, 0))
    act = pl.BlockSpec((tm, D), lambda i: (i, 0))
    act_rnn = pl.BlockSpec((tm, D_RNN), lambda i: (i, 0))
    return _call(
        body,
        (dx1, ya, yb, proj, proj, proj, proj, w_ba, w_bb, w_out),
        name=name,
        grid=(s // tm,),
        in_specs=[act, act, act] + _gate_specs(tm) + [whole(D_RNN), whole(D_SGU), whole(D)],
        out_specs=[act, act, act, act, act, act_rnn, act],
        out_shape=[SDS((s, D), BF16)] * 5 + [SDS((s, D_RNN), BF16), SDS((s, D_SGU), BF16)],
        semantics=("parallel",),
        comm=comm,
    )


def _final_loss(x, g, target, *, tm, name):
    s = x.shape[0]
    tm = min(tm, s)

    def body(x_ref, g_ref, t_ref, dx_ref, dxb_ref, dg_ref, loss_ref):
        @pl.when(pl.program_id(0) == 0)
        def _():
            dg_ref[...] = jnp.zeros_like(dg_ref)
            loss_ref[...] = jnp.zeros_like(loss_ref)

        xv = x_ref[...]
        r = lax.rsqrt(jnp.mean(xv * xv, axis=-1, keepdims=True) + EPS)
        xhat = xv * r
        e = xhat * g_ref[...] - t_ref[...]
        loss_ref[...] += 0.5 * jnp.sum(jnp.mean(e * e, axis=-1, keepdims=True), axis=0, keepdims=True)
        dy = e * (1.0 / D)
        dxh = dy * g_ref[...]
        dx = r * (dxh - xhat * jnp.mean(dxh * xhat, axis=-1, keepdims=True))
        dx_ref[...] = dx
        dxb_ref[...] = dx.astype(BF16)
        dg_ref[...] += jnp.sum(dy * xhat, axis=0, keepdims=True)

    act = pl.BlockSpec((tm, D), lambda i: (i, 0))
    vec = pl.BlockSpec((1, D), lambda i: (0, 0))
    return pl.pallas_call(
        body,
        name=name,
        grid=(s // tm,),
        in_specs=[act, vec, act],
        out_specs=[act, act, vec, pl.BlockSpec((SUBLANES, LANES), lambda i: (0, 0))],
        out_shape=[SDS((s, D), F32), SDS((s, D), BF16), SDS((1, D), F32), SDS((SUBLANES, LANES), F32)],
        compiler_params=_params("arbitrary"),
    )(x, g, target)


def _adamw_math(w, g, m, v):
    m2 = ADAM_B1 * m + (1.0 - ADAM_B1) * g
    v2 = ADAM_B2 * v + (1.0 - ADAM_B2) * (g * g)
    m_hat = m2 / (1.0 - ADAM_B1**ADAM_STEP)
    v_hat = v2 / (1.0 - ADAM_B2**ADAM_STEP)
    delta = -ADAM_LR * (m_hat / (jnp.sqrt(v_hat) + ADAM_EPS) + ADAM_WD * w)
    return delta, m2, v2


def _row_tile(rows, cap):
    return max(t for t in range(SUBLANES, min(cap, rows) + 1, SUBLANES) if rows % t == 0)


def _adamw_layers(w, grads, m, v, *, tr, name):
    depth, r, c = w.shape
    tr = _row_tile(r, tr)

    def body(*refs):
        g_refs = refs[:depth]
        w_ref, m_ref, v_ref, g_out, d_ref, mo_ref, vo_ref = refs[depth:]
        for l in range(depth):

            @pl.when(pl.program_id(0) == l)
            def _(l=l):
                g = g_refs[l][...]
                g_out[...] = g
                d_ref[...], mo_ref[...], vo_ref[...] = _adamw_math(w_ref[...], g, m_ref[...], v_ref[...])

    def of_layer(ll):
        return pl.BlockSpec((tr, c), lambda l, i: (jnp.where(l == ll, i, 0), 0))

    stacked = pl.BlockSpec((None, tr, c), lambda l, i: (l, i, 0))
    return pl.pallas_call(
        body,
        name=name,
        grid=(depth, r // tr),
        in_specs=[of_layer(ll) for ll in range(depth)] + [stacked] * 3,
        out_specs=[stacked] * 4,
        out_shape=[SDS((depth, r, c), F32)] * 4,
        compiler_params=_params("parallel", "parallel"),
    )(*grads, w, m, v)


def _adamw_reduced(w, parts, from_chips, m, v, chip, *, tr, name):
    depth, r, _ = w.shape
    tr = _row_tile(r, tr)

    def body(chip_ref, *refs):
        p_refs, c_refs = refs[:depth], refs[depth : 2 * depth]
        w_ref, m_ref, v_ref, g_out, d_ref, mo_ref, vo_ref = refs[2 * depth :]
        for l in range(depth):

            @pl.when(pl.program_id(0) == l)
            def _(l=l):
                got = c_refs[l]
                g = ((p_refs[l][...].astype(F32) + got[0].astype(F32)) + got[1].astype(F32)) + got[2].astype(F32)
                g_out[...] = g
                d_ref[...], mo_ref[...], vo_ref[...] = _adamw_math(w_ref[...], g, m_ref[...], v_ref[...])

    def mine_of_layer(ll):
        return pl.BlockSpec((None, tr, D), lambda l, i, chip_ref: (chip_ref[0], jnp.where(l == ll, i, 0), 0))

    def theirs_of_layer(ll):
        return pl.BlockSpec((3, tr, D), lambda l, i, chip_ref: (0, jnp.where(l == ll, i, 0), 0))

    stacked = pl.BlockSpec((None, tr, D), lambda l, i, chip_ref: (l, i, 0))
    return pl.pallas_call(
        body,
        name=name,
        grid_spec=pltpu.PrefetchScalarGridSpec(
            num_scalar_prefetch=1,
            grid=(depth, r // tr),
            in_specs=[mine_of_layer(ll) for ll in range(depth)]
            + [theirs_of_layer(ll) for ll in range(depth)]
            + [stacked] * 3,
            out_specs=[stacked] * 4,
        ),
        out_shape=[SDS((depth, r, D), F32)] * 4,
        compiler_params=_params("parallel", "parallel"),
    )(chip, *parts, *from_chips, w, m, v)


def _adamw_small(groups, *, name):
    n = len(groups)

    def body(*refs):
        ins, outs = refs[: 4 * n], refs[4 * n :]
        for i in range(n):
            w, g, m, v = (ref[...] for ref in ins[4 * i : 4 * i + 4])
            outs[3 * i][...], outs[3 * i + 1][...], outs[3 * i + 2][...] = _adamw_math(w, g, m, v)

    vmem = pl.BlockSpec(memory_space=pltpu.VMEM)
    outs = pl.pallas_call(
        body,
        name=name,
        in_specs=[vmem] * (4 * n),
        out_specs=[vmem] * (3 * n),
        out_shape=[SDS(grp[0].shape, F32) for grp in groups for _ in range(3)],
        compiler_params=pltpu.CompilerParams(vmem_limit_bytes=VMEM_LIMIT_BYTES),
    )(*[a for grp in groups for a in grp])
    return [tuple(outs[3 * i : 3 * i + 3]) for i in range(n)]


ANY = pl.BlockSpec(memory_space=pl.ANY)


def _position():
    return lax.axis_index("x"), lax.axis_index("y"), lax.axis_index("c")


def _other_chips(x, y):
    return [(1 - x, y), (x, 1 - y), (1 - x, 1 - y)]


class _Comm:
    def __init__(self, inputs, out_shapes, sem_counts, start, middle, finish, middle_at=1.0):
        self.inputs, self.out_shapes, self.sem_counts = list(inputs), list(out_shapes), list(sem_counts)
        self.start, self.middle, self.finish = start, middle, finish
        self.middle_at = middle_at

    def sem_shapes(self):
        return [pltpu.SemaphoreType.DMA((n,)) for n in self.sem_counts]


def _merge_comms(comms):
    bounds, i, o, s = [], 0, 0, 0
    for cm in comms:
        bounds.append((i, i + len(cm.inputs), o, o + len(cm.out_shapes), s, s + len(cm.sem_counts)))
        i, o, s = bounds[-1][1], bounds[-1][3], bounds[-1][5]

    def phase(which):
        def run(ins, outs, sems):
            for cm, (i0, i1, o0, o1, s0, s1) in zip(comms, bounds):
                getattr(cm, which)(ins[i0:i1], outs[o0:o1], sems[s0:s1])

        return run

    return _Comm(
        [a for cm in comms for a in cm.inputs],
        [a for cm in comms for a in cm.out_shapes],
        [a for cm in comms for a in cm.sem_counts],
        phase("start"),
        phase("middle"),
        phase("finish"),
        middle_at=max(cm.middle_at for cm in comms),
    )


def _call(body, args, *, semantics, comm=None, **kw):
    if comm is None:
        return pl.pallas_call(body, compiler_params=_params(*semantics), **kw)(*args)
    grid, in_specs, out_specs, out_shape = kw["grid"], kw["in_specs"], kw["out_specs"], kw["out_shape"]
    scratch = list(kw.get("scratch_shapes", ()))
    single = not isinstance(out_shape, (list, tuple))
    core_specs = [out_specs] if single else list(out_specs)
    core_shapes = [out_shape] if single else list(out_shape)
    n_in, n_out, n_scr = len(in_specs), len(core_shapes), len(scratch)
    n_cin, n_cout = len(comm.inputs), len(comm.out_shapes)
    steps = 1
    for g in grid:
        steps *= g
    middle = min(int(comm.middle_at * steps), steps - 1)

    def hosted(*refs):
        core_in, c_in = refs[:n_in], refs[n_in : n_in + n_cin]
        o0 = n_in + n_cin
        core_out, c_out = refs[o0 : o0 + n_out], refs[o0 + n_out : o0 + n_out + n_cout]
        s0 = o0 + n_out + n_cout
        core_scr, sems = refs[s0 : s0 + n_scr], refs[s0 + n_scr :]
        step = pl.program_id(0)
        for d in range(1, len(grid)):
            step = step * grid[d] + pl.program_id(d)

        @pl.when(step == 0)
        def _():
            comm.start(c_in, c_out, sems)

        body(*core_in, *core_out, *core_scr)

        @pl.when(step == middle)
        def _():
            comm.middle(c_in, c_out, sems)

        @pl.when(step == steps - 1)
        def _():
            comm.finish(c_in, c_out, sems)

    outs = pl.pallas_call(
        hosted,
        name=kw["name"],
        grid=grid,
        in_specs=list(in_specs) + [ANY] * n_cin,
        out_specs=core_specs + [ANY] * n_cout,
        out_shape=core_shapes + comm.out_shapes,
        scratch_shapes=scratch + comm.sem_shapes(),
        compiler_params=_params(*(["arbitrary"] * len(grid))),
    )(*args, *comm.inputs)
    return (outs[0] if single else outs[:n_out]), outs[n_out:]


def _comm_only(comm, *, name):
    n_cin, n_cout = len(comm.inputs), len(comm.out_shapes)

    def body(*refs):
        ins, outs, sems = refs[:n_cin], refs[n_cin : n_cin + n_cout], refs[n_cin + n_cout :]
        comm.start(ins, outs, sems)
        comm.middle(ins, outs, sems)
        comm.finish(ins, outs, sems)

    return pl.pallas_call(
        body,
        name=name,
        in_specs=[ANY] * n_cin,
        out_specs=[ANY] * n_cout,
        out_shape=comm.out_shapes,
        scratch_shapes=comm.sem_shapes(),
    )(*comm.inputs)


def _gather_comm(shards, pass_on_at=1.0):
    n = len(shards)
    per = 7

    def plan(ins, outs, sems):
        send, recv, local = sems
        x, y, c = _position()
        me, sibling = (x, y, c), (x, y, 1 - c)
        chips = _other_chips(x, y)

        def block(t, px, py, pc):
            return outs[t].at[pl.ds(4 * px + 2 * py + pc, 1)]

        def copy(t, k, blk, to, src=None):
            return pltpu.make_async_remote_copy(
                src_ref=block(t, *blk) if src is None else src,
                dst_ref=block(t, *blk),
                send_sem=send.at[t * per + k],
                recv_sem=recv.at[t * per + k],
                device_id=to,
                device_id_type=MESH,
            )

        mine = [pltpu.make_async_copy(ins[t], block(t, *me), local.at[t]) for t in range(n)]
        to_chips = [copy(t, 1 + j, me, (*chip, c), src=ins[t]) for t in range(n) for j, chip in enumerate(chips)]
        to_sibling = [copy(t, 0, me, sibling, src=ins[t]) for t in range(n)]
        from_chips = [copy(t, 1 + j, (*chip, c), me) for t in range(n) for j, chip in enumerate(chips)]
        passed_on = [copy(t, 4 + j, (*chip, c), sibling) for t in range(n) for j, chip in enumerate(chips)]
        from_sibling = [copy(t, 0, sibling, me) for t in range(n)]
        from_sibling += [copy(t, 4 + j, (*chip, 1 - c), me) for t in range(n) for j, chip in enumerate(chips)]
        return mine, to_chips, to_sibling, from_chips, passed_on, from_sibling

    def start(ins, outs, sems):
        mine, to_chips, to_sibling, _, _, _ = plan(ins, outs, sems)
        for cp in mine + to_chips + to_sibling:
            cp.start()

    def middle(ins, outs, sems):
        _, _, _, from_chips, passed_on, _ = plan(ins, outs, sems)
        for arrived, onward in zip(from_chips, passed_on):
            arrived.wait_recv()
            onward.start()

    def finish(ins, outs, sems):
        mine, to_chips, to_sibling, _, passed_on, from_sibling = plan(ins, outs, sems)
        for cp in from_sibling:
            cp.wait_recv()
        for cp in to_chips + to_sibling + passed_on:
            cp.wait_send()
        for cp in mine:
            cp.wait()

    out_shapes = [SDS((N_DEV,) + sh.shape[1:], sh.dtype) for sh in shards]
    return _Comm(shards, out_shapes, [n * per, n * per, n], start, middle, finish, middle_at=pass_on_at)


def _exchange_comm(arrays, out_shapes, n_copies, copies_of):
    def start(ins, outs, sems):
        for cp in copies_of(ins, outs, *sems):
            cp.start()

    def middle(ins, outs, sems):
        pass

    def finish(ins, outs, sems):
        for cp in copies_of(ins, outs, *sems):
            cp.wait()

    return _Comm(arrays, out_shapes, [n_copies, n_copies], start, middle, finish)


def _sibling_comm(grads):
    def copies_of(ins, outs, send, recv):
        x, y, c = _position()
        return [
            pltpu.make_async_remote_copy(
                src_ref=ins[t].at[:, pl.ds(1 - c, 1)],
                dst_ref=outs[t],
                send_sem=send.at[t],
                recv_sem=recv.at[t],
                device_id=(x, y, 1 - c),
                device_id_type=MESH,
            )
            for t in range(len(ins))
        ]

    return _exchange_comm(grads, [SDS((4, 1) + g.shape[2:], g.dtype) for g in grads], len(grads), copies_of)


def _chips_comm(parts):
    def copies_of(ins, outs, send, recv):
        x, y, c = _position()
        return [
            pltpu.make_async_remote_copy(
                src_ref=ins[t].at[pl.ds(2 * px + py, 1)],
                dst_ref=outs[t].at[pl.ds(k, 1)],
                send_sem=send.at[3 * t + k],
                recv_sem=recv.at[3 * t + k],
                device_id=(px, py, c),
                device_id_type=MESH,
            )
            for t in range(len(ins))
            for k, (px, py) in enumerate(_other_chips(x, y))
        ]

    return _exchange_comm(parts, [SDS((3,) + p.shape[1:], p.dtype) for p in parts], 3 * len(parts), copies_of)


def _sum_with_sibling(grad, got, core, *, name):
    rows = grad.shape[2]

    def body(core_ref, a_ref, b_ref, o_ref):
        o_ref[...] = (a_ref[...].astype(F32) + b_ref[...].astype(F32)).astype(o_ref.dtype)

    return pl.pallas_call(
        body,
        name=name,
        grid_spec=pltpu.PrefetchScalarGridSpec(
            num_scalar_prefetch=1,
            grid=(4,),
            in_specs=[
                pl.BlockSpec((None, None, rows, D), lambda q, core_ref: (q, core_ref[0], 0, 0)),
                pl.BlockSpec((None, None, rows, D), lambda q, core_ref: (q, 0, 0, 0)),
            ],
            out_specs=pl.BlockSpec((None, rows, D), lambda q, core_ref: (q, 0, 0)),
        ),
        out_shape=SDS((4, rows, D), grad.dtype),
        compiler_params=_params("parallel"),
    )(core, grad, got)


def _sum_chips(part, got, chip, *, name):
    rows = part.shape[1]

    def body(chip_ref, a_ref, b_ref, o_ref):
        o_ref[...] = ((a_ref[...].astype(F32) + b_ref[0].astype(F32)) + b_ref[1].astype(F32)) + b_ref[2].astype(F32)

    return pl.pallas_call(
        body,
        name=name,
        grid_spec=pltpu.PrefetchScalarGridSpec(
            num_scalar_prefetch=1,
            grid=(1,),
            in_specs=[
                pl.BlockSpec((None, rows, D), lambda i, chip_ref: (chip_ref[0], 0, 0)),
                pl.BlockSpec((3, rows, D), lambda i, chip_ref: (0, 0, 0)),
            ],
            out_specs=pl.BlockSpec((rows, D), lambda i, chip_ref: (0, 0)),
        ),
        out_shape=SDS((rows, D), F32),
        compiler_params=_params("arbitrary"),
    )(chip, part, got)


def _all_reduce_small(pack, comm, *, name):
    rows = pack.shape[1]
    relations = [(kx, ky, kc) for kx in (0, 1) for ky in (0, 1) for kc in (0, 1)][1:]
    n_cin, n_cout = len(comm.inputs), len(comm.out_shapes)

    def body(*refs):
        in_ref, c_in = refs[0], refs[1 : 1 + n_cin]
        out_ref, c_out = refs[1 + n_cin], refs[2 + n_cin : 2 + n_cin + n_cout]
        landed, send1, recv1, send2, recv2 = refs[2 + n_cin + n_cout : 7 + n_cin + n_cout]
        sems = refs[7 + n_cin + n_cout :]
        comm.start(c_in, c_out, sems)
        x, y, c = _position()
        mine = 4 * x + 2 * y + c

        def peer(rel):
            kx, ky, kc = rel
            return (1 - x if kx else x, 1 - y if ky else y, 1 - c if kc else c)

        first = []
        for k, rel in enumerate(relations):
            px, py, pc = peer(rel)
            cp = pltpu.make_async_remote_copy(
                src_ref=in_ref.at[4 * px + 2 * py + pc],
                dst_ref=landed.at[k],
                send_sem=send1.at[k],
                recv_sem=recv1.at[k],
                device_id=(px, py, pc),
                device_id_type=MESH,
            )
            cp.start()
            first.append(cp)
        total = in_ref[mine]
        for k, cp in enumerate(first):
            cp.wait_recv()
            total = total + landed[k]
        out_ref[mine] = total
        second = []
        for k, rel in enumerate(relations):
            cp = pltpu.make_async_remote_copy(
                src_ref=out_ref.at[mine],
                dst_ref=out_ref.at[mine],
                send_sem=send2.at[k],
                recv_sem=recv2.at[k],
                device_id=peer(rel),
                device_id_type=MESH,
            )
            cp.start()
            second.append(cp)
        for k, rel in enumerate(relations):
            px, py, pc = peer(rel)
            got = out_ref.at[4 * px + 2 * py + pc]
            pltpu.make_async_remote_copy(
                src_ref=got, dst_ref=got, send_sem=send2.at[k], recv_sem=recv2.at[k], device_id=peer(rel), device_id_type=MESH
            ).wait_recv()
        for cp in first + second:
            cp.wait_send()
        comm.middle(c_in, c_out, sems)
        comm.finish(c_in, c_out, sems)

    vmem = pl.BlockSpec(memory_space=pltpu.VMEM)
    outs = pl.pallas_call(
        body,
        name=name,
        in_specs=[vmem] + [ANY] * n_cin,
        out_specs=[vmem] + [ANY] * n_cout,
        out_shape=[SDS(pack.shape, F32)] + comm.out_shapes,
        scratch_shapes=[
            pltpu.VMEM((7, rows, D), F32),
            pltpu.SemaphoreType.DMA((7,)),
            pltpu.SemaphoreType.DMA((7,)),
            pltpu.SemaphoreType.DMA((7,)),
            pltpu.SemaphoreType.DMA((7,)),
        ]
        + comm.sem_shapes(),
        compiler_params=pltpu.CompilerParams(vmem_limit_bytes=VMEM_LIMIT_BYTES),
    )(pack, *comm.inputs)
    return outs[0], outs[1:]


def _pack(arrays, rows):
    flat = jnp.concatenate([a.reshape(-1).astype(F32) for a in arrays])
    return jnp.pad(flat, (0, rows * D - flat.shape[0])).reshape(rows, D)


def _unpack(pack, shapes):
    flat = pack.reshape(-1)
    out, off = [], 0
    for sh in shapes:
        size = 1
        for dim in sh:
            size *= dim
        out.append(flat[off : off + size].reshape(sh))
        off += size
    return out


def _block_diag_pairs(w):
    w = w.reshape(N_RNN_TILES, 2, HEAD_DIM, HEAD_DIM)
    z = jnp.zeros_like(w[:, 0])
    top = jnp.concatenate([w[:, 0], z], axis=2)
    bot = jnp.concatenate([z, w[:, 1]], axis=2)
    return jnp.concatenate([top, bot], axis=1)


def _diag_blocks(w2):
    a = w2[:, :HEAD_DIM, :HEAD_DIM]
    b = w2[:, HEAD_DIM:, HEAD_DIM:]
    return jnp.stack([a, b], axis=1).reshape(RNN_HEADS, HEAD_DIM, HEAD_DIM)


BIG = ("w_in", "w_branch_a", "w_branch_b", "w_out", "w_up", "w_down")
TRANSPOSED = ("w_in", "w_up")
SMALL = (
    "norm_mix_g", "conv_w", "conv_b", "lru_w_a", "lru_b_a", "lru_w_x", "lru_b_x", "lru_lambda",
    "sgu_ln_g", "sgu_ln_b", "sgu_w_s", "sgu_b_s", "norm_ffn_g", "final_norm_g",
)
WEIGHTS = (
    "norm_mix_g", "w_in", "conv_w", "conv_b", "lru_w_a", "lru_b_a", "lru_w_x", "lru_b_x", "lru_lambda", "sgu_ln_g",
    "sgu_ln_b", "sgu_w_s", "sgu_b_s", "w_branch_a", "w_branch_b", "w_out", "norm_ffn_g", "w_up", "w_down", "final_norm_g",
)

TM = 512
TM_NT = 1024
TN_IN = 1664
TN_UP = 2048
TKA = 512
TKA_PIECES = 256
TC = 512
TB = 256
TR = 256


GATHERS_RIDING = (
    {
        "in_proj": (1.0, [(0, "w_branch_a"), (0, "w_branch_b"), (0, "w_out"), (0, "w_up")]),
        "branch_a_fwd": (0.85, [(1, "w_in")]),
        "ffn_up": (1.0, [(0, "w_down")]),
        "ffn_down": (0.9, [(1, "w_branch_a"), (1, "w_branch_b"), (1, "w_out")]),
    },
    {"in_proj": (0.7, [(1, "w_up")]), "branch_a_fwd": (0.7, [(1, "w_down")])},
)


def _layer_forward(l, x, p, w, shards):
    def run(key, fn, *args, **kw):
        if key not in GATHERS_RIDING[l]:
            return fn(*args, **kw)
        pass_on_at, riding = GATHERS_RIDING[l][key]
        out, got = fn(*args, comm=_gather_comm([shards[l2][n2] for l2, n2 in riding], pass_on_at), **kw)
        for (l2, n2), full in zip(riding, got):
            w[l2][n2] = full.reshape(-1, D)
        return out

    proj, h = run("in_proj", _norm_matmul_nt, x, p["norm_mix_g"], w[l]["w_in"], tm=TM_NT, tn=TN_IN, name=f"in_proj_{l}")
    hseq, ya_pre = run(
        "branch_a_fwd", _branch_a_fwd, proj, p["conv_w"], p["conv_b"], p["wa2"], p["lru_b_a"], p["wx2"], p["lru_b_x"],
        p["lru_lambda"], tc=TC, name=f"branch_a_fwd_{l}",
    )
    yb_pre = _sgu_fwd(proj, p["sgu_ln_g"], p["sgu_ln_b"], p["wm"], p["sgu_bias"], tb=TB, name=f"sgu_fwd_{l}")
    x1, ya, yb = _merge_fwd(
        ya_pre, yb_pre, proj, x, w[l]["w_branch_a"], w[l]["w_branch_b"], w[l]["w_out"], tm=TM, name=f"merge_fwd_{l}"
    )
    f_pre, h2 = run("ffn_up", _norm_matmul_nt, x1, p["norm_ffn_g"], w[l]["w_up"], tm=TM_NT, tn=TN_UP, name=f"ffn_up_{l}")
    x2 = run("ffn_down", _matmul_nn_res, f_pre, w[l]["w_down"], x1, relu2=True, tm=TM, name=f"ffn_down_{l}")
    saved = dict(x=x, h=h, proj=proj, hseq=hseq, ya_pre=ya_pre, yb_pre=yb_pre, ya=ya, yb=yb, x1=x1, h2=h2, f_pre=f_pre)
    return x2, saved


def _layer_backward(l, dx2, dx2b, sv, p, w, core, waiting):
    parts, from_chips = {}, {}

    def by_device(g):
        return g.reshape(4, 2, -1, D)

    def with_sibling(name, g, got):
        parts[name] = _sum_with_sibling(by_device(g), got, core, name=f"sum_sibling_{name}_{l}")

    df_pre = _matmul_nt_drelu2(dx2b, w["w_down"], sv["f_pre"], tm=TM_NT, tn=TN_UP, name=f"ffn_down_bwd_{l}")
    g_down = _matmul_tn([sv["f_pre"]], dx2b, relu2=True, tka=TKA, name=f"grad_w_down_{l}")
    g_up, (got,) = _matmul_tn(
        [df_pre], sv["h2"], relu2=False, tka=TKA, name=f"grad_w_up_{l}", comm=_sibling_comm([by_device(g_down)])
    )
    with_sibling("w_down", g_down, got)
    (dx1, dx1b, g_norm_ffn), (got, from_chips[l, "w_down"]) = _matmul_nn_rmsnorm_bwd(
        [df_pre], w["w_up"], sv["x1"], p["norm_ffn_g"], dx2, tm=TM, name=f"ffn_up_bwd_{l}",
        comm=_merge_comms([_sibling_comm([by_device(g_up)]), _chips_comm([parts["w_down"]])]),
    )
    with_sibling("w_up", g_up, got)
    (merged, dya, dyb, dga, dgb, dya_pre, dyb_pre), (from_chips[l, "w_up"],) = _merge_bwd(
        dx1b, sv["ya"], sv["yb"], sv["proj"], w["w_branch_a"], w["w_branch_b"], w["w_out"], tm=TM, name=f"merge_bwd_{l}",
        comm=_chips_comm([parts["w_up"]]),
    )
    g_out = _matmul_tn([merged], dx1b, relu2=False, tka=TKA, name=f"grad_w_out_{l}")
    g_ba = _matmul_tn([sv["ya_pre"]], dya, relu2=False, tka=TKA_PIECES, name=f"grad_w_branch_a_{l}")
    g_bb = _matmul_tn([sv["yb_pre"]], dyb, relu2=False, tka=TKA, name=f"grad_w_branch_b_{l}")
    branch = (("w_out", g_out), ("w_branch_a", g_ba), ("w_branch_b", g_bb))
    (du, dv, g_ws, g_bs, g_lng, g_lnb), got = _sgu_bwd(
        dyb_pre, sv["proj"], p["sgu_ln_g"], p["sgu_ln_b"], p["wm"], p["wmt"], p["sgu_bias"], p["mask"], tb=TB,
        name=f"sgu_bwd_{l}", comm=_sibling_comm([by_device(g) for _, g in branch]),
    )
    for (name, g), landed in zip(branch, got):
        with_sibling(name, g, landed)
    riding = [((l, name), parts[name]) for name, _ in branch] + list(waiting)
    (dxr, dgr, g_cw, g_cb, g_ba_, g_bx, g_lam, g_wa2, g_wx2), got = _branch_a_bwd(
        dya_pre, sv["proj"], sv["hseq"], p["conv_w"], p["conv_b"], p["wa2"], p["lru_b_a"], p["wx2"], p["lru_b_x"],
        p["lru_lambda"], p["wa2t"], p["wx2t"], tc=TC, name=f"branch_a_bwd_{l}", comm=_chips_comm([part for _, part in riding]),
    )
    for (key, _), landed in zip(riding, got):
        from_chips[key] = landed
    dproj = [dxr, dgr, du, dv, dga, dgb]
    g_in = _matmul_tn(dproj, sv["h"], relu2=False, tka=TKA_PIECES, name=f"grad_w_in_{l}")
    (dx, dxb, g_norm_mix), (got,) = _matmul_nn_rmsnorm_bwd(
        dproj, w["w_in"], sv["x"], p["norm_mix_g"], dx1, tm=TM, name=f"in_proj_bwd_{l}", comm=_sibling_comm([by_device(g_in)])
    )
    with_sibling("w_in", g_in, got)
    small = dict(
        norm_mix_g=g_norm_mix[0], conv_w=g_cw, conv_b=g_cb[0], lru_w_a=_diag_blocks(g_wa2), lru_b_a=g_ba_.reshape(RNN_HEADS, HEAD_DIM),
        lru_w_x=_diag_blocks(g_wx2), lru_b_x=g_bx.reshape(RNN_HEADS, HEAD_DIM), lru_lambda=g_lam[0], sgu_ln_g=g_lng[0],
        sgu_ln_b=g_lnb[0], sgu_w_s=g_ws, sgu_b_s=g_bs[:, :, 0], norm_ffn_g=g_norm_ffn[0],
    )
    return dx, dxb, small, parts, from_chips


def _prepare_small(l, given):
    chunk_id = jnp.arange(SGU_BLOCK) // CHUNK
    mask = (chunk_id[:, None] >= chunk_id[None, :]).astype(F32)
    wm = given["sgu_w_s"][l] * mask
    wa2 = _block_diag_pairs(given["lru_w_a"][l])
    wx2 = _block_diag_pairs(given["lru_w_x"][l])
    row = lambda a: a.reshape(1, -1)
    return dict(
        norm_mix_g=row(given["norm_mix_g"][l]),
        norm_ffn_g=row(given["norm_ffn_g"][l]),
        conv_w=given["conv_w_full"][l],
        conv_b=row(given["conv_b"][l]),
        wa2=wa2.astype(BF16),
        wx2=wx2.astype(BF16),
        wa2t=jnp.swapaxes(wa2, 1, 2).astype(BF16),
        wx2t=jnp.swapaxes(wx2, 1, 2).astype(BF16),
        lru_b_a=row(given["lru_b_a"][l]),
        lru_b_x=row(given["lru_b_x"][l]),
        lru_lambda=row(given["lru_lambda"][l]),
        sgu_ln_g=row(given["sgu_ln_g"][l]),
        sgu_ln_b=row(given["sgu_ln_b"][l]),
        wm=wm.astype(BF16),
        wmt=jnp.swapaxes(wm, 1, 2).astype(BF16),
        sgu_bias=jnp.broadcast_to(given["sgu_b_s"][l][:, :, None], (SGU_GROUPS, SGU_BLOCK, LANES)),
        mask=mask,
    )


def _step(given):
    x_idx, y_idx, c_idx = _position()
    dev = 4 * x_idx + 2 * y_idx + c_idx
    core = c_idx.astype(jnp.int32).reshape(1)
    chip = (2 * x_idx + y_idx).astype(jnp.int32).reshape(1)

    def rows_first(name, a):
        return jnp.swapaxes(a, 1, 2) if name in TRANSPOSED else a

    shards = []
    for l in range(DEPTH):
        shards.append({name: rows_first(name, given[name])[l].astype(BF16)[None] for name in BIG})
    conv_mine = given["conv_w"].reshape(1, DEPTH * CONV_WIDTH, D_RNN // N_DEV)
    w_in_first, conv_all = _comm_only(_gather_comm([shards[0]["w_in"], conv_mine]), name="gather_first")
    weights = [{"w_in": w_in_first.reshape(-1, D)}, {}]
    conv_all = conv_all.reshape(N_DEV, DEPTH, CONV_WIDTH, D_RNN // N_DEV)
    given = dict(given, conv_w_full=jnp.moveaxis(conv_all, 0, 2).reshape(DEPTH, CONV_WIDTH, D_RNN))

    small_params = [_prepare_small(l, given) for l in range(DEPTH)]
    x = given["x"][0]
    saved = []
    for l in range(DEPTH):
        x, sv = _layer_forward(l, x, small_params[l], weights, shards)
        saved.append(sv)
    dx, dxb, g_final, loss = _final_loss(x, given["final_norm_g"].reshape(1, D), given["loss_target"][0], tm=TM, name="final_loss")
    small_grads, parts, from_chips, waiting = [None] * DEPTH, [None] * DEPTH, {}, []
    for l in reversed(range(DEPTH)):
        dx, dxb, small_grads[l], parts[l], got = _layer_backward(
            l, dx, dxb, saved[l], small_params[l], weights[l], core, waiting
        )
        from_chips.update(got)
        waiting = [((l, "w_in"), parts[l]["w_in"])]

    small_list = []
    for name in SMALL[:-1]:
        small_list.append(jnp.stack([small_grads[l][name] for l in range(DEPTH)]))
    small_list += [g_final[0], loss[0, :1]]
    small_shapes = [a.shape for a in small_list]
    pack = _pack(small_list, SMALL_ROWS).reshape(N_DEV, SMALL_ROWS_PER_DEV, D)
    summed, (from_chips[0, "w_in"],) = _all_reduce_small(pack, _chips_comm([parts[0]["w_in"]]), name="all_reduce_small")
    summed = _unpack(summed, small_shapes)
    loss_total = summed[-1][0]
    grads = dict(zip(SMALL, summed[:-1]))
    cw = grads["conv_w"].reshape(DEPTH, CONV_WIDTH, N_DEV, D_RNN // N_DEV)
    grads["conv_w"] = lax.dynamic_index_in_dim(cw, dev, axis=2, keepdims=False)

    delta, new_m, new_v = {}, {}, {}
    for name in BIG:
        w, m, v = given[name], given["m_" + name], given["v_" + name]
        mine = [parts[l][name] for l in range(DEPTH)]
        theirs = [from_chips[l, name] for l in range(DEPTH)]
        if name == "w_up":
            sums = [_sum_chips(mine[l], theirs[l], chip, name=f"sum_chips_{name}_{l}").T for l in range(DEPTH)]
            out = _adamw_layers(w, sums, m, v, tr=TR, name=f"adamw_{name}")
        else:
            out = _adamw_reduced(
                rows_first(name, w), mine, theirs, rows_first(name, m), rows_first(name, v), chip, tr=TR, name=f"adamw_{name}"
            )
            out = [rows_first(name, a) for a in out]
        grads[name], delta[name], new_m[name], new_v[name] = out
    two_d = lambda a: a.reshape(1, -1) if a.ndim == 1 else a
    groups = [tuple(two_d(a) for a in (given[n], grads[n], given["m_" + n], given["v_" + n])) for n in SMALL]
    for n, (d, m2, v2) in zip(SMALL, _adamw_small(groups, name="adamw_small")):
        shape = given[n].shape
        delta[n], new_m[n], new_v[n] = d.reshape(shape), m2.reshape(shape), v2.reshape(shape)

    return (
        loss_total, dx[None],
        *[grads[n] for n in WEIGHTS], *[delta[n] for n in WEIGHTS], *[new_m[n] for n in WEIGHTS], *[new_v[n] for n in WEIGHTS],
    )


def kernel(x, norm_mix_g, w_in, conv_w, conv_b, lru_w_a, lru_b_a, lru_w_x, lru_b_x, lru_lambda, sgu_ln_g, sgu_ln_b, sgu_w_s, sgu_b_s, w_branch_a, w_branch_b, w_out, norm_ffn_g, w_up, w_down, final_norm_g, loss_target, m_norm_mix_g, m_w_in, m_conv_w, m_conv_b, m_lru_w_a, m_lru_b_a, m_lru_w_x, m_lru_b_x, m_lru_lambda, m_sgu_ln_g, m_sgu_ln_b, m_sgu_w_s, m_sgu_b_s, m_w_branch_a, m_w_branch_b, m_w_out, m_norm_ffn_g, m_w_up, m_w_down, m_final_norm_g, v_norm_mix_g, v_w_in, v_conv_w, v_conv_b, v_lru_w_a, v_lru_b_a, v_lru_w_x, v_lru_b_x, v_lru_lambda, v_sgu_ln_g, v_sgu_ln_b, v_sgu_w_s, v_sgu_b_s, v_w_branch_a, v_w_branch_b, v_w_out, v_norm_ffn_g, v_w_up, v_w_down, v_final_norm_g):
    return _step(dict(locals()))
```

```python
import jax
import jax.numpy as jnp
from jax import lax
from jax.experimental import pallas as pl
from jax.experimental.pallas import tpu as pltpu

F32 = jnp.float32
BF16 = jnp.bfloat16
SDS = jax.ShapeDtypeStruct
MESH = pl.DeviceIdType.MESH

D = 1024
D_RNN = 1280
D_SGU = 1024
D_FF = 4096
D_IN = 2 * D_RNN + 2 * D_SGU + 2 * D
DEPTH = 2
RNN_HEADS = 20
HEAD_DIM = 64
CONV_WIDTH = 4
LRU_C = 8.0
SGU_GROUPS = 8
SGU_BLOCK = 128
CHUNK = 64
EPS = 1e-6
N_DEV = 8

ADAM_LR = 0.001
ADAM_B1 = 0.9
ADAM_B2 = 0.999
ADAM_EPS = 1e-08
ADAM_WD = 0.01
ADAM_STEP = 10

LANES = 128
SUBLANES = 8
VMEM_LIMIT_BYTES = 56 * 1024 * 1024

N_RNN_TILES = D_RNN // LANES
RNN_TILES_PER_STEP = 5
GRNN_BLK128 = D_RNN // LANES
U_BLK512 = (2 * D_RNN) // 512
V_BLK512 = (2 * D_RNN + D_SGU) // 512
GA_BLK512 = (2 * D_RNN + 2 * D_SGU) // 512
GB_BLK512 = (2 * D_RNN + 2 * D_SGU + D) // 512

SMALL_ROWS_PER_DEV = 80
SMALL_ROWS = N_DEV * SMALL_ROWS_PER_DEV


def _params(*sem):
    return pltpu.CompilerParams(dimension_semantics=sem, vmem_limit_bytes=VMEM_LIMIT_BYTES)


def _sigmoid(x):
    return 0.5 + 0.5 * jnp.tanh(0.5 * x)


_GELU_C = 0.7978845608028654
_GELU_K = 0.044715


def _gelu(x):
    t = jnp.tanh(_GELU_C * (x + _GELU_K * x * x * x))
    return 0.5 * x * (1.0 + t)


def _gelu_and_grad(x):
    t = jnp.tanh(_GELU_C * (x + _GELU_K * x * x * x))
    val = 0.5 * x * (1.0 + t)
    grad = 0.5 * (1.0 + t) + 0.5 * x * (1.0 - t * t) * _GELU_C * (1.0 + 3.0 * _GELU_K * x * x)
    return val, grad


def _one_minus_square(log_a, a):
    return -jnp.tanh(log_a) * (1.0 + a * a)


def _dot(a, b):
    return jnp.dot(a, b, preferred_element_type=F32)


def _dot_nt(a, b):
    return lax.dot_general(a, b, (((1,), (1,)), ((), ())), preferred_element_type=F32)


def _dot_tn(a, b):
    return lax.dot_general(a, b, (((0,), (0,)), ((), ())), preferred_element_type=F32)


def _norm_matmul_nt(x, g, w, *, tm, tn, name, comm=None):
    s, n = x.shape[0], w.shape[0]
    tm, tn = min(tm, s), min(tn, n)

    def body(x_ref, g_ref, w_ref, o_ref, h_ref):
        @pl.when(pl.program_id(1) == 0)
        def _():
            xv = x_ref[...]
            r = lax.rsqrt(jnp.mean(xv * xv, axis=-1, keepdims=True) + EPS)
            h_ref[...] = (xv * r * g_ref[...]).astype(BF16)

        o_ref[...] = _dot_nt(h_ref[...], w_ref[...]).astype(o_ref.dtype)

    return _call(
        body,
        (x, g, w),
        name=name,
        grid=(s // tm, n // tn),
        in_specs=[
            pl.BlockSpec((tm, D), lambda i, j: (i, 0)),
            pl.BlockSpec((1, D), lambda i, j: (0, 0)),
            pl.BlockSpec((tn, D), lambda i, j: (j, 0)),
        ],
        out_specs=[pl.BlockSpec((tm, tn), lambda i, j: (i, j)), pl.BlockSpec((tm, D), lambda i, j: (i, 0))],
        out_shape=[SDS((s, n), BF16), SDS((s, D), BF16)],
        semantics=("parallel", "arbitrary"),
        comm=comm,
    )


def _matmul_nn_res(a, w, res, *, relu2, tm, name, comm=None):
    s, k = a.shape
    tm = min(tm, s)

    def body(a_ref, w_ref, r_ref, o_ref):
        av = a_ref[...]
        if relu2:
            t = jnp.maximum(av.astype(F32), 0.0)
            av = (t * t).astype(BF16)
        o_ref[...] = r_ref[...] + _dot(av, w_ref[...])

    return _call(
        body,
        (a, w, res),
        name=name,
        grid=(s // tm,),
        in_specs=[
            pl.BlockSpec((tm, k), lambda i: (i, 0)),
            pl.BlockSpec((k, D), lambda i: (0, 0)),
            pl.BlockSpec((tm, D), lambda i: (i, 0)),
        ],
        out_specs=pl.BlockSpec((tm, D), lambda i: (i, 0)),
        out_shape=SDS((s, D), F32),
        semantics=("parallel",),
        comm=comm,
    )


def _matmul_nt_drelu2(a, w, pre, *, tm, tn, name):
    s, n = a.shape[0], w.shape[0]
    tm, tn = min(tm, s), min(tn, n)

    def body(a_ref, w_ref, p_ref, o_ref):
        d = _dot_nt(a_ref[...], w_ref[...])
        o_ref[...] = (d * (2.0 * jnp.maximum(p_ref[...].astype(F32), 0.0))).astype(o_ref.dtype)

    return pl.pallas_call(
        body,
        name=name,
        grid=(s // tm, n // tn),
        in_specs=[
            pl.BlockSpec((tm, D), lambda i, j: (i, 0)),
            pl.BlockSpec((tn, D), lambda i, j: (j, 0)),
            pl.BlockSpec((tm, tn), lambda i, j: (i, j)),
        ],
        out_specs=pl.BlockSpec((tm, tn), lambda i, j: (i, j)),
        out_shape=SDS((s, n), BF16),
        compiler_params=_params("parallel", "arbitrary"),
    )(a, w, pre)


def _matmul_tn(a_list, b, *, relu2, tka, name, comm=None):
    s = b.shape[0]
    n = len(a_list)
    nblk = [a.shape[1] // tka for a in a_list]
    starts = [sum(nblk[:p]) for p in range(n)]

    def body(*refs):
        a_refs, b_ref, o_ref = refs[:n], refs[n], refs[n + 1]
        i = pl.program_id(0)
        for p in range(n):

            @pl.when((i >= starts[p]) & (i < starts[p] + nblk[p]))
            def _(p=p):
                av = a_refs[p][...]
                if relu2:
                    t = jnp.maximum(av.astype(F32), 0.0)
                    av = (t * t).astype(BF16)
                o_ref[...] = _dot_tn(av, b_ref[...]).astype(o_ref.dtype)

    def piece_spec(p):
        return pl.BlockSpec((s, tka), lambda i: (0, jnp.clip(i - starts[p], 0, nblk[p] - 1)))

    return _call(
        body,
        (*a_list, b),
        name=name,
        grid=(sum(nblk),),
        in_specs=[piece_spec(p) for p in range(n)] + [pl.BlockSpec((s, D), lambda i: (0, 0))],
        out_specs=pl.BlockSpec((tka, D), lambda i: (i, 0)),
        out_shape=SDS((sum(nblk) * tka, D), BF16),
        semantics=("parallel",),
        comm=comm,
    )


def _matmul_nn_rmsnorm_bwd(a_list, w, x, g, res, *, tm, name, comm=None):
    s = x.shape[0]
    tm = min(tm, s)
    n = len(a_list)
    widths = [a.shape[1] for a in a_list]
    offs = [sum(widths[:p]) for p in range(n)]
    k = sum(widths)

    def body(*refs):
        a_refs = refs[:n]
        w_ref, x_ref, g_ref, r_ref, dx_ref, dxb_ref, dg_ref = refs[n:]

        @pl.when(pl.program_id(0) == 0)
        def _():
            dg_ref[...] = jnp.zeros_like(dg_ref)

        dh = _dot(a_refs[0][...], w_ref[0 : widths[0], :])
        for p in range(1, n):
            dh += _dot(a_refs[p][...], w_ref[offs[p] : offs[p] + widths[p], :])
        xv = x_ref[...]
        r = lax.rsqrt(jnp.mean(xv * xv, axis=-1, keepdims=True) + EPS)
        xhat = xv * r
        dxh = dh * g_ref[...]
        dx = r_ref[...] + r * (dxh - xhat * jnp.mean(dxh * xhat, axis=-1, keepdims=True))
        dx_ref[...] = dx
        dxb_ref[...] = dx.astype(BF16)
        dg_ref[...] += jnp.sum(dh * xhat, axis=0, keepdims=True)

    act = pl.BlockSpec((tm, D), lambda i: (i, 0))
    vec = pl.BlockSpec((1, D), lambda i: (0, 0))
    return _call(
        body,
        (*a_list, w, x, g, res),
        name=name,
        grid=(s // tm,),
        in_specs=[pl.BlockSpec((tm, wd), lambda i: (i, 0)) for wd in widths]
        + [pl.BlockSpec((k, D), lambda i: (0, 0), pipeline_mode=pl.Buffered(1)), act, vec, act],
        out_specs=[act, act, vec],
        out_shape=[SDS((s, D), F32), SDS((s, D), BF16), SDS((1, D), F32)],
        semantics=("arbitrary",),
        comm=comm,
    )


def _rows_before(ext, k):
    if k == 0:
        return ext[SUBLANES:, :]
    return pltpu.roll(ext, k, 0)[SUBLANES:, :]


def _rows_after(ext, k, n):
    if k == 0:
        return ext[:n, :]
    return pltpu.roll(ext, n + SUBLANES - k, 0)[:n, :]


def _scan_forward(a, b, n):
    row = lax.broadcasted_iota(jnp.int32, a.shape, 0)
    d = 1
    while d < n:
        if d < SUBLANES:
            m = row >= d
            a_s = jnp.where(m, pltpu.roll(a, d, 0), 1.0)
            b_s = jnp.where(m, pltpu.roll(b, d, 0), 0.0)
            b = a * b_s + b
            a = a * a_s
        else:
            b = jnp.concatenate([b[:d], a[d:] * b[: n - d] + b[d:]], axis=0)
            a = jnp.concatenate([a[:d], a[d:] * a[: n - d]], axis=0)
        d *= 2
    return a, b


def _scan_backward(a, b, n):
    row = lax.broadcasted_iota(jnp.int32, a.shape, 0)
    d = 1
    while d < n:
        if d < SUBLANES:
            m = row < n - d
            a_s = jnp.where(m, pltpu.roll(a, n - d, 0), 1.0)
            b_s = jnp.where(m, pltpu.roll(b, n - d, 0), 0.0)
            b = a * b_s + b
            a = a * a_s
        else:
            b = jnp.concatenate([a[: n - d] * b[d:] + b[: n - d], b[n - d :]], axis=0)
            a = jnp.concatenate([a[: n - d] * a[d:], a[n - d :]], axis=0)
        d *= 2
    return b


def _repeat_matrix(n):
    groups = n // SUBLANES
    return (jnp.arange(n)[:, None] // SUBLANES == jnp.arange(3 * groups)[None, :] % groups).astype(BF16)


def _scan_rows(a, b, n, repeat_ref, a_scr, b_scr, reverse):
    groups = n // SUBLANES
    a3 = a.reshape(groups, SUBLANES, LANES)
    b3 = b.reshape(groups, SUBLANES, LANES)
    sub = lax.broadcasted_iota(jnp.int32, a3.shape, 1)
    for d in (1, 2, 4):
        m = (sub < SUBLANES - d) if reverse else (sub >= d)
        shift = SUBLANES - d if reverse else d
        a_s = jnp.where(m, pltpu.roll(a3, shift, 1), 1.0)
        b_s = jnp.where(m, pltpu.roll(b3, shift, 1), 0.0)
        b3 = a3 * b_s + b3
        a3 = a3 * a_s
    a_scr[...] = a3.reshape(n, LANES)
    b_scr[...] = b3.reshape(n, LANES)
    edge = 0 if reverse else SUBLANES - 1
    a_tot = a_scr[pl.ds(edge, groups, stride=SUBLANES), :]
    b_tot = b_scr[pl.ds(edge, groups, stride=SUBLANES), :]
    row = lax.broadcasted_iota(jnp.int32, a_tot.shape, 0)
    if reverse:
        through = _scan_backward(a_tot, b_tot, groups)
        entering = jnp.where(row < groups - 1, pltpu.roll(through, groups - 1, 0), 0.0)
    else:
        _, through = _scan_forward(a_tot, b_tot, groups)
        entering = jnp.where(row >= 1, pltpu.roll(through, 1, 0), 0.0)
    hi = entering.astype(BF16)
    rest = entering - hi.astype(F32)
    mid = rest.astype(BF16)
    lo = (rest - mid.astype(F32)).astype(BF16)
    repeated = _dot(repeat_ref[...], jnp.concatenate([hi, mid, lo], axis=0))
    return b_scr[...] + a_scr[...] * repeated


def _softplus_neg(lam):
    z = -lam
    return jnp.maximum(z, 0.0) + jnp.log1p(jnp.exp(-jnp.abs(z)))


def _conv_and_gates(xc, xprev, cw_ref, cb_ref, wa_ref, ba_ref, wx_ref, bx_ref, lam_ref):
    ext = jnp.concatenate([xprev, xc], axis=0)
    x1, x2, x3 = _rows_before(ext, 1), _rows_before(ext, 2), _rows_before(ext, 3)
    xr = cb_ref[...] + x3 * cw_ref[0:1, :] + x2 * cw_ref[1:2, :] + x1 * cw_ref[2:3, :] + xc * cw_ref[3:4, :]
    xrb = xr.astype(BF16)
    r = _sigmoid(_dot(xrb, wa_ref[...]) + ba_ref[...])
    i = _sigmoid(_dot(xrb, wx_ref[...]) + bx_ref[...])
    sp = _softplus_neg(lam_ref[...])
    log_a = (-LRU_C * r) * sp
    a = jnp.exp(log_a)
    return xr, (x1, x2, x3), r, i, a, _one_minus_square(log_a, a)


def _branch_a_fwd(proj, cw, cb, wa2, ba, wx2, bx, lam, *, tc, name, comm=None):
    s = proj.shape[0]
    tc = min(tc, s)

    def body(x_ref, g_ref, cw_ref, cb_ref, wa_ref, ba_ref, wx_ref, bx_ref, lam_ref, rep_ref, h_ref, y_ref,
             xprev, hlast, a_scr, b_scr):
        @pl.when(pl.program_id(1) == 0)
        def _():
            xprev[...] = jnp.zeros_like(xprev)
            hlast[...] = jnp.zeros_like(hlast)

        for t in range(RNN_TILES_PER_STEP):
            cols = lambda ref: ref.at[:, pl.ds(t * LANES, LANES)]
            one_tile(
                cols(x_ref), cols(g_ref), cols(cw_ref), cols(cb_ref), wa_ref.at[t], cols(ba_ref), wx_ref.at[t], cols(bx_ref),
                cols(lam_ref), rep_ref, cols(h_ref), cols(y_ref), cols(xprev), cols(hlast), a_scr.at[t], b_scr.at[t],
            )

    def one_tile(x_ref, g_ref, cw_ref, cb_ref, wa_ref, ba_ref, wx_ref, bx_ref, lam_ref, rep_ref, h_ref, y_ref,
                 xprev, hlast, a_scr, b_scr):
        xc = x_ref[...].astype(F32)
        xr, _, r, i, a, om = _conv_and_gates(xc, xprev[...], cw_ref, cb_ref, wa_ref, ba_ref, wx_ref, bx_ref, lam_ref)
        xprev[...] = xc[tc - SUBLANES :, :]
        u = jnp.sqrt(om) * (i * xr)
        row8 = lax.broadcasted_iota(jnp.int32, (SUBLANES, LANES), 0)
        first = u[:SUBLANES] + jnp.where(row8 == 0, a[:SUBLANES] * hlast[SUBLANES - 1 : SUBLANES, :], 0.0)
        h = _scan_rows(a, jnp.concatenate([first, u[SUBLANES:]], axis=0), tc, rep_ref, a_scr, b_scr, reverse=False)
        hlast[...] = h[tc - SUBLANES :, :]
        h_ref[...] = h
        y_ref[...] = (h * _gelu(g_ref[...].astype(F32))).astype(BF16)

    wide = RNN_TILES_PER_STEP * LANES
    tile = lambda j, c: (0, j)
    vec = pl.BlockSpec((1, wide), tile)
    mats = pl.BlockSpec((RNN_TILES_PER_STEP, LANES, LANES), lambda j, c: (j, 0, 0))
    repeat = _repeat_matrix(tc)
    return _call(
        body,
        (proj, proj, cw, cb, wa2, ba, wx2, bx, lam, repeat),
        name=name,
        grid=(N_RNN_TILES // RNN_TILES_PER_STEP, s // tc),
        in_specs=[
            pl.BlockSpec((tc, wide), lambda j, c: (c, j)),
            pl.BlockSpec((tc, wide), lambda j, c: (c, D_RNN // wide + j)),
            pl.BlockSpec((CONV_WIDTH, wide), tile),
            vec,
            mats,
            vec,
            mats,
            vec,
            vec,
            pl.BlockSpec(repeat.shape, lambda j, c: (0, 0)),
        ],
        out_specs=[pl.BlockSpec((tc, wide), lambda j, c: (c, j)), pl.BlockSpec((tc, wide), lambda j, c: (c, j))],
        out_shape=[SDS((s, D_RNN), F32), SDS((s, D_RNN), BF16)],
        scratch_shapes=[pltpu.VMEM((SUBLANES, wide), F32)] * 2 + [pltpu.VMEM((RNN_TILES_PER_STEP, tc, LANES), F32)] * 2,
        semantics=("parallel", "arbitrary"),
        comm=comm,
    )


def _branch_a_bwd(dy, proj, h, cw, cb, wa2, ba, wx2, bx, lam, wa2t, wx2t, *, tc, name, comm=None):
    s = proj.shape[0]
    tc = min(tc, s)
    nc = s // tc
    halo16 = tc // 16
    halo8 = tc // SUBLANES

    def body(dy_ref, x_ref, xh_ref, g_ref, h_ref, hh_ref, cw_ref, cb_ref, wa_ref, ba_ref, wx_ref, bx_ref, lam_ref,
             wat_ref, wxt_ref, rep_ref, dx_ref, dg_ref, dcw_ref, dcb_ref, dba_ref, dbx_ref, dlam_ref, dwa_ref, dwx_ref,
             carry, dxr_next, a_scr, b_scr):
        cc = pl.program_id(1)
        ct = nc - 1 - cc

        @pl.when(cc == 0)
        def _():
            carry[...] = jnp.zeros_like(carry)
            dxr_next[...] = jnp.zeros_like(dxr_next)
            for ref in (dcw_ref, dcb_ref, dba_ref, dbx_ref, dlam_ref, dwa_ref, dwx_ref):
                ref[...] = jnp.zeros_like(ref)

        for t in range(RNN_TILES_PER_STEP):
            cols = lambda ref: ref.at[:, pl.ds(t * LANES, LANES)]
            one_tile(
                ct, cols(dy_ref), cols(x_ref), cols(xh_ref), cols(g_ref), cols(h_ref), cols(hh_ref), cols(cw_ref), cols(cb_ref),
                wa_ref.at[t], cols(ba_ref), wx_ref.at[t], cols(bx_ref), cols(lam_ref), wat_ref.at[t], wxt_ref.at[t], rep_ref,
                cols(dx_ref), cols(dg_ref), cols(dcw_ref), cols(dcb_ref), cols(dba_ref), cols(dbx_ref), cols(dlam_ref),
                dwa_ref.at[t], dwx_ref.at[t], cols(carry), cols(dxr_next), a_scr.at[t], b_scr.at[t],
            )

    def one_tile(ct, dy_ref, x_ref, xh_ref, g_ref, h_ref, hh_ref, cw_ref, cb_ref, wa_ref, ba_ref, wx_ref, bx_ref, lam_ref,
                 wat_ref, wxt_ref, rep_ref, dx_ref, dg_ref, dcw_ref, dcb_ref, dba_ref, dbx_ref, dlam_ref, dwa_ref, dwx_ref,
                 carry, dxr_next, a_scr, b_scr):
        xc = x_ref[...].astype(F32)
        xprev = jnp.where(ct > 0, xh_ref[SUBLANES:, :].astype(F32), 0.0)
        xr, (x1, x2, x3), r, i, a, om = _conv_and_gates(
            xc, xprev, cw_ref, cb_ref, wa_ref, ba_ref, wx_ref, bx_ref, lam_ref
        )
        inv_norm = lax.rsqrt(om)
        norm = om * inv_norm
        row = lax.broadcasted_iota(jnp.int32, xc.shape, 0)

        hv = h_ref[...]
        ge, ge_grad = _gelu_and_grad(g_ref[...].astype(F32))
        dyv = dy_ref[...].astype(F32)
        dg_ref[...] = (dyv * hv * ge_grad).astype(dg_ref.dtype)
        dh = dyv * ge

        b = dh + jnp.where(row == tc - 1, carry[0:1, :], 0.0)
        a_next = jnp.where(row < tc - 1, pltpu.roll(a, tc - 1, 0), 0.0)
        gadj = _scan_rows(a_next, b, tc, rep_ref, a_scr, b_scr, reverse=True)
        carry[...] = (a * gadj)[:SUBLANES, :]

        hprev_first = jnp.where(ct > 0, hh_ref[SUBLANES - 1 : SUBLANES, :], 0.0)
        hprev = jnp.where(row >= 1, pltpu.roll(hv, 1, 0), hprev_first)
        da = gadj * hprev
        ix = i * xr
        dnorm = gadj * ix
        di = gadj * norm * xr
        dlog_a = da * a - dnorm * (1.0 - om) * inv_norm
        sp = _softplus_neg(lam_ref[...])
        dr = dlog_a * (-LRU_C * sp)
        dsp = jnp.sum(dlog_a * (-LRU_C * r), axis=0, keepdims=True)
        dlam_ref[...] += dsp * (-_sigmoid(-lam_ref[...]))
        dza = dr * r * (1.0 - r)
        dzx = di * i * (1.0 - i)
        dzab, dzxb = dza.astype(BF16), dzx.astype(BF16)
        dxr = gadj * norm * i + _dot(dzab, wat_ref[...]) + _dot(dzxb, wxt_ref[...])
        xrb = xr.astype(BF16)
        dwa_ref[...] += _dot_tn(xrb, dzab)
        dwx_ref[...] += _dot_tn(xrb, dzxb)
        dba_ref[...] += jnp.sum(dza, axis=0, keepdims=True)
        dbx_ref[...] += jnp.sum(dzx, axis=0, keepdims=True)

        ext = jnp.concatenate([dxr, dxr_next[...]], axis=0)
        dx = (
            dxr * cw_ref[3:4, :]
            + _rows_after(ext, 1, tc) * cw_ref[2:3, :]
            + _rows_after(ext, 2, tc) * cw_ref[1:2, :]
            + _rows_after(ext, 3, tc) * cw_ref[0:1, :]
        )
        dxr_next[...] = dxr[:SUBLANES, :]
        dx_ref[...] = dx.astype(dx_ref.dtype)
        dcb_ref[...] += jnp.sum(dxr, axis=0, keepdims=True)
        dcw_ref[3:4, :] += jnp.sum(dxr * xc, axis=0, keepdims=True)
        dcw_ref[2:3, :] += jnp.sum(dxr * x1, axis=0, keepdims=True)
        dcw_ref[1:2, :] += jnp.sum(dxr * x2, axis=0, keepdims=True)
        dcw_ref[0:1, :] += jnp.sum(dxr * x3, axis=0, keepdims=True)

    wide = RNN_TILES_PER_STEP * LANES
    tile = lambda j, c: (0, j)
    mat = lambda j, c: (j, 0, 0)
    cur = lambda j, c: (nc - 1 - c, j)
    vec = pl.BlockSpec((1, wide), tile)
    matspec = pl.BlockSpec((RNN_TILES_PER_STEP, LANES, LANES), mat)
    repeat = _repeat_matrix(tc)
    return _call(
        body,
        (dy, proj, proj, proj, h, h, cw, cb, wa2, ba, wx2, bx, lam, wa2t, wx2t, repeat),
        name=name,
        grid=(N_RNN_TILES // RNN_TILES_PER_STEP, nc),
        in_specs=[
            pl.BlockSpec((tc, wide), cur),
            pl.BlockSpec((tc, wide), cur),
            pl.BlockSpec((16, wide), lambda j, c: (jnp.maximum((nc - 1 - c) * halo16 - 1, 0), j)),
            pl.BlockSpec((tc, wide), lambda j, c: (nc - 1 - c, D_RNN // wide + j)),
            pl.BlockSpec((tc, wide), cur),
            pl.BlockSpec((SUBLANES, wide), lambda j, c: (jnp.maximum((nc - 1 - c) * halo8 - 1, 0), j)),
            pl.BlockSpec((CONV_WIDTH, wide), tile),
            vec,
            matspec,
            vec,
            matspec,
            vec,
            vec,
            matspec,
            matspec,
            pl.BlockSpec(repeat.shape, lambda j, c: (0, 0)),
        ],
        out_specs=[
            pl.BlockSpec((tc, wide), cur),
            pl.BlockSpec((tc, wide), cur),
            pl.BlockSpec((CONV_WIDTH, wide), tile),
            vec,
            vec,
            vec,
            vec,
            matspec,
            matspec,
        ],
        out_shape=[
            SDS((s, D_RNN), BF16),
            SDS((s, D_RNN), BF16),
            SDS((CONV_WIDTH, D_RNN), F32),
            SDS((1, D_RNN), F32),
            SDS((1, D_RNN), F32),
            SDS((1, D_RNN), F32),
            SDS((1, D_RNN), F32),
            SDS((N_RNN_TILES, LANES, LANES), F32),
            SDS((N_RNN_TILES, LANES, LANES), F32),
        ],
        scratch_shapes=[pltpu.VMEM((SUBLANES, wide), F32)] * 2 + [pltpu.VMEM((RNN_TILES_PER_STEP, tc, LANES), F32)] * 2,
        semantics=("parallel", "arbitrary"),
        comm=comm,
    )


def _sgu_specs(tb):
    half = lambda blk: pl.BlockSpec((tb, 512), lambda n: (n, blk))
    return [half(U_BLK512), half(U_BLK512 + 1), half(V_BLK512), half(V_BLK512 + 1)]


def _sgu_normed(v, lng_ref, lnb_ref):
    gv, gv_grad = _gelu_and_grad(v)
    mu = jnp.mean(gv, axis=-1, keepdims=True)
    xc = gv - mu
    rs = lax.rsqrt(jnp.mean(xc * xc, axis=-1, keepdims=True) + EPS)
    xhat = xc * rs
    return xhat * lng_ref[...] + lnb_ref[...], xhat, rs, gv_grad


def _sgu_fwd(proj, lng, lnb, wm, bias, *, tb, name):
    s = proj.shape[0]
    tb = min(tb, s)

    def body(u0_ref, u1_ref, v0_ref, v1_ref, lng_ref, lnb_ref, wm_ref, bias_ref, y_ref):
        u = jnp.concatenate([u0_ref[...], u1_ref[...]], axis=1).astype(F32)
        v = jnp.concatenate([v0_ref[...], v1_ref[...]], axis=1).astype(F32)
        gu = _gelu(u)
        vn, _, _, _ = _sgu_normed(v, lng_ref, lnb_ref)
        vnb = vn.astype(BF16)
        for blk in range(tb // SGU_BLOCK):
            rows = slice(blk * SGU_BLOCK, (blk + 1) * SGU_BLOCK)
            for g in range(SGU_GROUPS):
                cols = slice(g * LANES, (g + 1) * LANES)
                mixed = _dot(wm_ref[g], vnb[rows, cols]) + bias_ref[g]
                y_ref[rows, cols] = (gu[rows, cols] * mixed).astype(BF16)

    const2 = lambda n: (0, 0)
    const3 = lambda n: (0, 0, 0)
    return pl.pallas_call(
        body,
        name=name,
        grid=(s // tb,),
        in_specs=_sgu_specs(tb)
        + [
            pl.BlockSpec((1, D_SGU), const2),
            pl.BlockSpec((1, D_SGU), const2),
            pl.BlockSpec((SGU_GROUPS, SGU_BLOCK, SGU_BLOCK), const3),
            pl.BlockSpec((SGU_GROUPS, SGU_BLOCK, LANES), const3),
        ],
        out_specs=pl.BlockSpec((tb, D_SGU), lambda n: (n, 0)),
        out_shape=SDS((s, D_SGU), BF16),
        compiler_params=_params("parallel"),
    )(proj, proj, proj, proj, lng, lnb, wm, bias)


def _sgu_bwd(dy, proj, lng, lnb, wm, wmt, bias, mask, *, tb, name, comm=None):
    s = proj.shape[0]
    tb = min(tb, s)
    nb = s // tb

    def body(dy_ref, u0_ref, u1_ref, v0_ref, v1_ref, lng_ref, lnb_ref, wm_ref, wmt_ref, bias_ref, mask_ref,
             du_ref, dv_ref, dws_ref, dbs_ref, dlng_ref, dlnb_ref, dvn_scr, dbs_acc):
        n = pl.program_id(0)

        @pl.when(n == 0)
        def _():
            dbs_acc[...] = jnp.zeros_like(dbs_acc)
            for ref in (dws_ref, dlng_ref, dlnb_ref):
                ref[...] = jnp.zeros_like(ref)

        u = jnp.concatenate([u0_ref[...], u1_ref[...]], axis=1).astype(F32)
        v = jnp.concatenate([v0_ref[...], v1_ref[...]], axis=1).astype(F32)
        gu, gu_grad = _gelu_and_grad(u)
        vn, xhat, rs, gv_grad = _sgu_normed(v, lng_ref, lnb_ref)
        vnb = vn.astype(BF16)
        dyv = dy_ref[...].astype(F32)
        for blk in range(tb // SGU_BLOCK):
            rows = slice(blk * SGU_BLOCK, (blk + 1) * SGU_BLOCK)
            for g in range(SGU_GROUPS):
                cols = slice(g * LANES, (g + 1) * LANES)
                vt = vnb[rows, cols]
                mixed = _dot(wm_ref[g], vt) + bias_ref[g]
                dyt = dyv[rows, cols]
                du_ref[rows, cols] = (dyt * mixed * gu_grad[rows, cols]).astype(BF16)
                dmix = dyt * gu[rows, cols]
                dmixb = dmix.astype(BF16)
                dvn_scr[rows, cols] = _dot(wmt_ref[g], dmixb)
                dws_ref[g] += _dot_nt(dmixb, vt) * mask_ref[...]
                dbs_acc[g] += dmix
        dvn = dvn_scr[...]
        dlng_ref[...] += jnp.sum(dvn * xhat, axis=0, keepdims=True)
        dlnb_ref[...] += jnp.sum(dvn, axis=0, keepdims=True)
        dxh = dvn * lng_ref[...]
        dgv = rs * (
            dxh - jnp.mean(dxh, axis=-1, keepdims=True) - xhat * jnp.mean(dxh * xhat, axis=-1, keepdims=True)
        )
        dv_ref[...] = (dgv * gv_grad).astype(BF16)

        @pl.when(n == nb - 1)
        def _():
            for g in range(SGU_GROUPS):
                dbs_ref[g] = jnp.broadcast_to(jnp.sum(dbs_acc[g], axis=-1, keepdims=True), (SGU_BLOCK, LANES))

    const2 = lambda n: (0, 0)
    const3 = lambda n: (0, 0, 0)
    gmat = pl.BlockSpec((SGU_GROUPS, SGU_BLOCK, SGU_BLOCK), const3)
    vec = pl.BlockSpec((1, D_SGU), const2)
    act = pl.BlockSpec((tb, D_SGU), lambda n: (n, 0))
    return _call(
        body,
        (dy, proj, proj, proj, proj, lng, lnb, wm, wmt, bias, mask),
        name=name,
        grid=(nb,),
        in_specs=[act] + _sgu_specs(tb) + [vec, vec, gmat, gmat, gmat, pl.BlockSpec((SGU_BLOCK, SGU_BLOCK), const2)],
        out_specs=[act, act, gmat, gmat, vec, vec],
        out_shape=[
            SDS((s, D_SGU), BF16),
            SDS((s, D_SGU), BF16),
            SDS((SGU_GROUPS, SGU_BLOCK, SGU_BLOCK), F32),
            SDS((SGU_GROUPS, SGU_BLOCK, LANES), F32),
            SDS((1, D_SGU), F32),
            SDS((1, D_SGU), F32),
        ],
        scratch_shapes=[pltpu.VMEM((tb, D_SGU), F32), pltpu.VMEM((SGU_GROUPS, SGU_BLOCK, LANES), F32)],
        semantics=("arbitrary",),
        comm=comm,
    )


def _gate_specs(tm):
    half = lambda blk: pl.BlockSpec((tm, 512), lambda i: (i, blk))
    return [half(GA_BLK512), half(GA_BLK512 + 1), half(GB_BLK512), half(GB_BLK512 + 1)]


def _merge_fwd(ya_pre, yb_pre, proj, x, w_ba, w_bb, w_out, *, tm, name):
    s = x.shape[0]
    tm = min(tm, s)

    def body(ya_ref, yb_ref, a0, a1, b0, b1, x_ref, wa_ref, wb_ref, wo_ref, x1_ref, yao_ref, ybo_ref):
        ya = _dot(ya_ref[...], wa_ref[...])
        yb = _dot(yb_ref[...], wb_ref[...])
        sa = _sigmoid(jnp.concatenate([a0[...], a1[...]], axis=1).astype(F32))
        sb = _sigmoid(jnp.concatenate([b0[...], b1[...]], axis=1).astype(F32))
        merged = sa * ya + sb * yb
        x1_ref[...] = x_ref[...] + _dot(merged.astype(BF16), wo_ref[...])
        yao_ref[...] = ya.astype(BF16)
        ybo_ref[...] = yb.astype(BF16)

    whole = lambda r: pl.BlockSpec((r, D), lambda i: (0, 0))
    act = pl.BlockSpec((tm, D), lambda i: (i, 0))
    return pl.pallas_call(
        body,
        name=name,
        grid=(s // tm,),
        in_specs=[pl.BlockSpec((tm, D_RNN), lambda i: (i, 0)), act] + _gate_specs(tm) + [act, whole(D_RNN), whole(D_SGU), whole(D)],
        out_specs=[act, act, act],
        out_shape=[SDS((s, D), F32), SDS((s, D), BF16), SDS((s, D), BF16)],
        compiler_params=_params("parallel"),
    )(ya_pre, yb_pre, proj, proj, proj, proj, x, w_ba, w_bb, w_out)


def _merge_bwd(dx1, ya, yb, proj, w_ba, w_bb, w_out, *, tm, name, comm=None):
    s = dx1.shape[0]
    tm = min(tm, s)

    def body(dx_ref, ya_ref, yb_ref, a0, a1, b0, b1, wa_ref, wb_ref, wo_ref,
             mg_ref, dya_ref, dyb_ref, dga_ref, dgb_ref, dyap_ref, dybp_ref):
        dm = _dot_nt(dx_ref[...], wo_ref[...])
        ya = ya_ref[...].astype(F32)
        yb = yb_ref[...].astype(F32)
        sa = _sigmoid(jnp.concatenate([a0[...], a1[...]], axis=1).astype(F32))
        sb = _sigmoid(jnp.concatenate([b0[...], b1[...]], axis=1).astype(F32))
        mg_ref[...] = (sa * ya + sb * yb).astype(BF16)
        dya = (dm * sa).astype(BF16)
        dyb = (dm * sb).astype(BF16)
        dya_ref[...] = dya
        dyb_ref[...] = dyb
        dga_ref[...] = (dm * ya * sa * (1.0 - sa)).astype(BF16)
        dgb_ref[...] = (dm * yb * sb * (1.0 - sb)).astype(BF16)
        dyap_ref[...] = _dot_nt(dya, wa_ref[...]).astype(BF16)
        dybp_ref[...] = _dot_nt(dyb, wb_ref[...]).astype(BF16)

    whole = lambda r: pl.BlockSpec((r, D), lambda i: (0, 0))
    act = pl.BlockSpec((tm, D), lambda i: (i, 0))
    act_rnn = pl.BlockSpec((tm, D_RNN), lambda i: (i, 0))
    return _call(
        body,
        (dx1, ya, yb, proj, proj, proj, proj, w_ba, w_bb, w_out),
        name=name,
        grid=(s // tm,),
        in_specs=[act, act, act] + _gate_specs(tm) + [whole(D_RNN), whole(D_SGU), whole(D)],
        out_specs=[act, act, act, act, act, act_rnn, act],
        out_shape=[SDS((s, D), BF16)] * 5 + [SDS((s, D_RNN), BF16), SDS((s, D_SGU), BF16)],
        semantics=("parallel",),
        comm=comm,
    )


def _final_loss(x, g, target, *, tm, name):
    s = x.shape[0]
    tm = min(tm, s)

    def body(x_ref, g_ref, t_ref, dx_ref, dxb_ref, dg_ref, loss_ref):
        @pl.when(pl.program_id(0) == 0)
        def _():
            dg_ref[...] = jnp.zeros_like(dg_ref)
            loss_ref[...] = jnp.zeros_like(loss_ref)

        xv = x_ref[...]
        r = lax.rsqrt(jnp.mean(xv * xv, axis=-1, keepdims=True) + EPS)
        xhat = xv * r
        e = xhat * g_ref[...] - t_ref[...]
        loss_ref[...] += 0.5 * jnp.sum(jnp.mean(e * e, axis=-1, keepdims=True), axis=0, keepdims=True)
        dy = e * (1.0 / D)
        dxh = dy * g_ref[...]
        dx = r * (dxh - xhat * jnp.mean(dxh * xhat, axis=-1, keepdims=True))
        dx_ref[...] = dx
        dxb_ref[...] = dx.astype(BF16)
        dg_ref[...] += jnp.sum(dy * xhat, axis=0, keepdims=True)

    act = pl.BlockSpec((tm, D), lambda i: (i, 0))
    vec = pl.BlockSpec((1, D), lambda i: (0, 0))
    return pl.pallas_call(
        body,
        name=name,
        grid=(s // tm,),
        in_specs=[act, vec, act],
        out_specs=[act, act, vec, pl.BlockSpec((SUBLANES, LANES), lambda i: (0, 0))],
        out_shape=[SDS((s, D), F32), SDS((s, D), BF16), SDS((1, D), F32), SDS((SUBLANES, LANES), F32)],
        compiler_params=_params("arbitrary"),
    )(x, g, target)


def _adamw_math(w, g, m, v):
    m2 = ADAM_B1 * m + (1.0 - ADAM_B1) * g
    v2 = ADAM_B2 * v + (1.0 - ADAM_B2) * (g * g)
    m_hat = m2 / (1.0 - ADAM_B1**ADAM_STEP)
    v_hat = v2 / (1.0 - ADAM_B2**ADAM_STEP)
    delta = -ADAM_LR * (m_hat / (jnp.sqrt(v_hat) + ADAM_EPS) + ADAM_WD * w)
    return delta, m2, v2


def _row_tile(rows, cap):
    return max(t for t in range(SUBLANES, min(cap, rows) + 1, SUBLANES) if rows % t == 0)


def _adamw_layers(w, grads, m, v, *, tr, name):
    depth, r, c = w.shape
    tr = _row_tile(r, tr)

    def body(*refs):
        g_refs = refs[:depth]
        w_ref, m_ref, v_ref, g_out, d_ref, mo_ref, vo_ref = refs[depth:]
        for l in range(depth):

            @pl.when(pl.program_id(0) == l)
            def _(l=l):
                g = g_refs[l][...]
                g_out[...] = g
                d_ref[...], mo_ref[...], vo_ref[...] = _adamw_math(w_ref[...], g, m_ref[...], v_ref[...])

    def of_layer(ll):
        return pl.BlockSpec((tr, c), lambda l, i: (jnp.where(l == ll, i, 0), 0))

    stacked = pl.BlockSpec((None, tr, c), lambda l, i: (l, i, 0))
    return pl.pallas_call(
        body,
        name=name,
        grid=(depth, r // tr),
        in_specs=[of_layer(ll) for ll in range(depth)] + [stacked] * 3,
        out_specs=[stacked] * 4,
        out_shape=[SDS((depth, r, c), F32)] * 4,
        compiler_params=_params("parallel", "parallel"),
    )(*grads, w, m, v)


def _adamw_reduced(w, parts, from_chips, m, v, chip, *, tr, name):
    depth, r, _ = w.shape
    tr = _row_tile(r, tr)

    def body(chip_ref, *refs):
        p_refs, c_refs = refs[:depth], refs[depth : 2 * depth]
        w_ref, m_ref, v_ref, g_out, d_ref, mo_ref, vo_ref = refs[2 * depth :]
        for l in range(depth):

            @pl.when(pl.program_id(0) == l)
            def _(l=l):
                got = c_refs[l]
                g = ((p_refs[l][...].astype(F32) + got[0].astype(F32)) + got[1].astype(F32)) + got[2].astype(F32)
                g_out[...] = g
                d_ref[...], mo_ref[...], vo_ref[...] = _adamw_math(w_ref[...], g, m_ref[...], v_ref[...])

    def mine_of_layer(ll):
        return pl.BlockSpec((None, tr, D), lambda l, i, chip_ref: (chip_ref[0], jnp.where(l == ll, i, 0), 0))

    def theirs_of_layer(ll):
        return pl.BlockSpec((3, tr, D), lambda l, i, chip_ref: (0, jnp.where(l == ll, i, 0), 0))

    stacked = pl.BlockSpec((None, tr, D), lambda l, i, chip_ref: (l, i, 0))
    return pl.pallas_call(
        body,
        name=name,
        grid_spec=pltpu.PrefetchScalarGridSpec(
            num_scalar_prefetch=1,
            grid=(depth, r // tr),
            in_specs=[mine_of_layer(ll) for ll in range(depth)]
            + [theirs_of_layer(ll) for ll in range(depth)]
            + [stacked] * 3,
            out_specs=[stacked] * 4,
        ),
        out_shape=[SDS((depth, r, D), F32)] * 4,
        compiler_params=_params("parallel", "parallel"),
    )(chip, *parts, *from_chips, w, m, v)


def _adamw_small(groups, *, name):
    n = len(groups)

    def body(*refs):
        ins, outs = refs[: 4 * n], refs[4 * n :]
        for i in range(n):
            w, g, m, v = (ref[...] for ref in ins[4 * i : 4 * i + 4])
            outs[3 * i][...], outs[3 * i + 1][...], outs[3 * i + 2][...] = _adamw_math(w, g, m, v)

    vmem = pl.BlockSpec(memory_space=pltpu.VMEM)
    outs = pl.pallas_call(
        body,
        name=name,
        in_specs=[vmem] * (4 * n),
        out_specs=[vmem] * (3 * n),
        out_shape=[SDS(grp[0].shape, F32) for grp in groups for _ in range(3)],
        compiler_params=pltpu.CompilerParams(vmem_limit_bytes=VMEM_LIMIT_BYTES),
    )(*[a for grp in groups for a in grp])
    return [tuple(outs[3 * i : 3 * i + 3]) for i in range(n)]


ANY = pl.BlockSpec(memory_space=pl.ANY)


def _position():
    return lax.axis_index("x"), lax.axis_index("y"), lax.axis_index("c")


def _other_chips(x, y):
    return [(1 - x, y), (x, 1 - y), (1 - x, 1 - y)]


class _Comm:
    def __init__(self, inputs, out_shapes, sem_counts, start, middle, finish, middle_at=1.0):
        self.inputs, self.out_shapes, self.sem_counts = list(inputs), list(out_shapes), list(sem_counts)
        self.start, self.middle, self.finish = start, middle, finish
        self.middle_at = middle_at

    def sem_shapes(self):
        return [pltpu.SemaphoreType.DMA((n,)) for n in self.sem_counts]


def _merge_comms(comms):
    bounds, i, o, s = [], 0, 0, 0
    for cm in comms:
        bounds.append((i, i + len(cm.inputs), o, o + len(cm.out_shapes), s, s + len(cm.sem_counts)))
        i, o, s = bounds[-1][1], bounds[-1][3], bounds[-1][5]

    def phase(which):
        def run(ins, outs, sems):
            for cm, (i0, i1, o0, o1, s0, s1) in zip(comms, bounds):
                getattr(cm, which)(ins[i0:i1], outs[o0:o1], sems[s0:s1])

        return run

    return _Comm(
        [a for cm in comms for a in cm.inputs],
        [a for cm in comms for a in cm.out_shapes],
        [a for cm in comms for a in cm.sem_counts],
        phase("start"),
        phase("middle"),
        phase("finish"),
        middle_at=max(cm.middle_at for cm in comms),
    )


def _call(body, args, *, semantics, comm=None, **kw):
    if comm is None:
        return pl.pallas_call(body, compiler_params=_params(*semantics), **kw)(*args)
    grid, in_specs, out_specs, out_shape = kw["grid"], kw["in_specs"], kw["out_specs"], kw["out_shape"]
    scratch = list(kw.get("scratch_shapes", ()))
    single = not isinstance(out_shape, (list, tuple))
    core_specs = [out_specs] if single else list(out_specs)
    core_shapes = [out_shape] if single else list(out_shape)
    n_in, n_out, n_scr = len(in_specs), len(core_shapes), len(scratch)
    n_cin, n_cout = len(comm.inputs), len(comm.out_shapes)
    steps = 1
    for g in grid:
        steps *= g
    middle = min(int(comm.middle_at * steps), steps - 1)

    def hosted(*refs):
        core_in, c_in = refs[:n_in], refs[n_in : n_in + n_cin]
        o0 = n_in + n_cin
        core_out, c_out = refs[o0 : o0 + n_out], refs[o0 + n_out : o0 + n_out + n_cout]
        s0 = o0 + n_out + n_cout
        core_scr, sems = refs[s0 : s0 + n_scr], refs[s0 + n_scr :]
        step = pl.program_id(0)
        for d in range(1, len(grid)):
            step = step * grid[d] + pl.program_id(d)

        @pl.when(step == 0)
        def _():
            comm.start(c_in, c_out, sems)

        body(*core_in, *core_out, *core_scr)

        @pl.when(step == middle)
        def _():
            comm.middle(c_in, c_out, sems)

        @pl.when(step == steps - 1)
        def _():
            comm.finish(c_in, c_out, sems)

    outs = pl.pallas_call(
        hosted,
        name=kw["name"],
        grid=grid,
        in_specs=list(in_specs) + [ANY] * n_cin,
        out_specs=core_specs + [ANY] * n_cout,
        out_shape=core_shapes + comm.out_shapes,
        scratch_shapes=scratch + comm.sem_shapes(),
        compiler_params=_params(*(["arbitrary"] * len(grid))),
    )(*args, *comm.inputs)
    return (outs[0] if single else outs[:n_out]), outs[n_out:]


def _comm_only(comm, *, name):
    n_cin, n_cout = len(comm.inputs), len(comm.out_shapes)

    def body(*refs):
        ins, outs, sems = refs[:n_cin], refs[n_cin : n_cin + n_cout], refs[n_cin + n_cout :]
        comm.start(ins, outs, sems)
        comm.middle(ins, outs, sems)
        comm.finish(ins, outs, sems)

    return pl.pallas_call(
        body,
        name=name,
        in_specs=[ANY] * n_cin,
        out_specs=[ANY] * n_cout,
        out_shape=comm.out_shapes,
        scratch_shapes=comm.sem_shapes(),
    )(*comm.inputs)


def _gather_comm(shards, pass_on_at=1.0):
    n = len(shards)
    per = 7

    def plan(ins, outs, sems):
        send, recv, local = sems
        x, y, c = _position()
        me, sibling = (x, y, c), (x, y, 1 - c)
        chips = _other_chips(x, y)

        def block(t, px, py, pc):
            return outs[t].at[pl.ds(4 * px + 2 * py + pc, 1)]

        def copy(t, k, blk, to, src=None):
            return pltpu.make_async_remote_copy(
                src_ref=block(t, *blk) if src is None else src,
                dst_ref=block(t, *blk),
                send_sem=send.at[t * per + k],
                recv_sem=recv.at[t * per + k],
                device_id=to,
                device_id_type=MESH,
            )

        mine = [pltpu.make_async_copy(ins[t], block(t, *me), local.at[t]) for t in range(n)]
        to_chips = [copy(t, 1 + j, me, (*chip, c), src=ins[t]) for t in range(n) for j, chip in enumerate(chips)]
        to_sibling = [copy(t, 0, me, sibling, src=ins[t]) for t in range(n)]
        from_chips = [copy(t, 1 + j, (*chip, c), me) for t in range(n) for j, chip in enumerate(chips)]
        passed_on = [copy(t, 4 + j, (*chip, c), sibling) for t in range(n) for j, chip in enumerate(chips)]
        from_sibling = [copy(t, 0, sibling, me) for t in range(n)]
        from_sibling += [copy(t, 4 + j, (*chip, 1 - c), me) for t in range(n) for j, chip in enumerate(chips)]
        return mine, to_chips, to_sibling, from_chips, passed_on, from_sibling

    def start(ins, outs, sems):
        mine, to_chips, to_sibling, _, _, _ = plan(ins, outs, sems)
        for cp in mine + to_chips + to_sibling:
            cp.start()

    def middle(ins, outs, sems):
        _, _, _, from_chips, passed_on, _ = plan(ins, outs, sems)
        for arrived, onward in zip(from_chips, passed_on):
            arrived.wait_recv()
            onward.start()

    def finish(ins, outs, sems):
        mine, to_chips, to_sibling, _, passed_on, from_sibling = plan(ins, outs, sems)
        for cp in from_sibling:
            cp.wait_recv()
        for cp in to_chips + to_sibling + passed_on:
            cp.wait_send()
        for cp in mine:
            cp.wait()

    out_shapes = [SDS((N_DEV,) + sh.shape[1:], sh.dtype) for sh in shards]
    return _Comm(shards, out_shapes, [n * per, n * per, n], start, middle, finish, middle_at=pass_on_at)


def _exchange_comm(arrays, out_shapes, n_copies, copies_of):
    def start(ins, outs, sems):
        for cp in copies_of(ins, outs, *sems):
            cp.start()

    def middle(ins, outs, sems):
        pass

    def finish(ins, outs, sems):
        for cp in copies_of(ins, outs, *sems):
            cp.wait()

    return _Comm(arrays, out_shapes, [n_copies, n_copies], start, middle, finish)


def _sibling_comm(grads):
    def copies_of(ins, outs, send, recv):
        x, y, c = _position()
        return [
            pltpu.make_async_remote_copy(
                src_ref=ins[t].at[:, pl.ds(1 - c, 1)],
                dst_ref=outs[t],
                send_sem=send.at[t],
                recv_sem=recv.at[t],
                device_id=(x, y, 1 - c),
                device_id_type=MESH,
            )
            for t in range(len(ins))
        ]

    return _exchange_comm(grads, [SDS((4, 1) + g.shape[2:], g.dtype) for g in grads], len(grads), copies_of)


def _chips_comm(parts):
    def copies_of(ins, outs, send, recv):
        x, y, c = _position()
        return [
            pltpu.make_async_remote_copy(
                src_ref=ins[t].at[pl.ds(2 * px + py, 1)],
                dst_ref=outs[t].at[pl.ds(k, 1)],
                send_sem=send.at[3 * t + k],
                recv_sem=recv.at[3 * t + k],
                device_id=(px, py, c),
                device_id_type=MESH,
            )
            for t in range(len(ins))
            for k, (px, py) in enumerate(_other_chips(x, y))
        ]

    return _exchange_comm(parts, [SDS((3,) + p.shape[1:], p.dtype) for p in parts], 3 * len(parts), copies_of)


def _sum_with_sibling(grad, got, core, *, name):
    rows = grad.shape[2]

    def body(core_ref, a_ref, b_ref, o_ref):
        o_ref[...] = (a_ref[...].astype(F32) + b_ref[...].astype(F32)).astype(o_ref.dtype)

    return pl.pallas_call(
        body,
        name=name,
        grid_spec=pltpu.PrefetchScalarGridSpec(
            num_scalar_prefetch=1,
            grid=(4,),
            in_specs=[
                pl.BlockSpec((None, None, rows, D), lambda q, core_ref: (q, core_ref[0], 0, 0)),
                pl.BlockSpec((None, None, rows, D), lambda q, core_ref: (q, 0, 0, 0)),
            ],
            out_specs=pl.BlockSpec((None, rows, D), lambda q, core_ref: (q, 0, 0)),
        ),
        out_shape=SDS((4, rows, D), grad.dtype),
        compiler_params=_params("parallel"),
    )(core, grad, got)


def _sum_chips(part, got, chip, *, name):
    rows = part.shape[1]

    def body(chip_ref, a_ref, b_ref, o_ref):
        o_ref[...] = ((a_ref[...].astype(F32) + b_ref[0].astype(F32)) + b_ref[1].astype(F32)) + b_ref[2].astype(F32)

    return pl.pallas_call(
        body,
        name=name,
        grid_spec=pltpu.PrefetchScalarGridSpec(
            num_scalar_prefetch=1,
            grid=(1,),
            in_specs=[
                pl.BlockSpec((None, rows, D), lambda i, chip_ref: (chip_ref[0], 0, 0)),
                pl.BlockSpec((3, rows, D), lambda i, chip_ref: (0, 0, 0)),
            ],
            out_specs=pl.BlockSpec((rows, D), lambda i, chip_ref: (0, 0)),
        ),
        out_shape=SDS((rows, D), F32),
        compiler_params=_params("arbitrary"),
    )(chip, part, got)


def _all_reduce_small(pack, comm, *, name):
    rows = pack.shape[1]
    relations = [(kx, ky, kc) for kx in (0, 1) for ky in (0, 1) for kc in (0, 1)][1:]
    n_cin, n_cout = len(comm.inputs), len(comm.out_shapes)

    def body(*refs):
        in_ref, c_in = refs[0], refs[1 : 1 + n_cin]
        out_ref, c_out = refs[1 + n_cin], refs[2 + n_cin : 2 + n_cin + n_cout]
        landed, send1, recv1, send2, recv2 = refs[2 + n_cin + n_cout : 7 + n_cin + n_cout]
        sems = refs[7 + n_cin + n_cout :]
        comm.start(c_in, c_out, sems)
        x, y, c = _position()
        mine = 4 * x + 2 * y + c

        def peer(rel):
            kx, ky, kc = rel
            return (1 - x if kx else x, 1 - y if ky else y, 1 - c if kc else c)

        first = []
        for k, rel in enumerate(relations):
            px, py, pc = peer(rel)
            cp = pltpu.make_async_remote_copy(
                src_ref=in_ref.at[4 * px + 2 * py + pc],
                dst_ref=landed.at[k],
                send_sem=send1.at[k],
                recv_sem=recv1.at[k],
                device_id=(px, py, pc),
                device_id_type=MESH,
            )
            cp.start()
            first.append(cp)
        total = in_ref[mine]
        for k, cp in enumerate(first):
            cp.wait_recv()
            total = total + landed[k]
        out_ref[mine] = total
        second = []
        for k, rel in enumerate(relations):
            cp = pltpu.make_async_remote_copy(
                src_ref=out_ref.at[mine],
                dst_ref=out_ref.at[mine],
                send_sem=send2.at[k],
                recv_sem=recv2.at[k],
                device_id=peer(rel),
                device_id_type=MESH,
            )
            cp.start()
            second.append(cp)
        for k, rel in enumerate(relations):
            px, py, pc = peer(rel)
            got = out_ref.at[4 * px + 2 * py + pc]
            pltpu.make_async_remote_copy(
                src_ref=got, dst_ref=got, send_sem=send2.at[k], recv_sem=recv2.at[k], device_id=peer(rel), device_id_type=MESH
            ).wait_recv()
        for cp in first + second:
            cp.wait_send()
        comm.middle(c_in, c_out, sems)
        comm.finish(c_in, c_out, sems)

    vmem = pl.BlockSpec(memory_space=pltpu.VMEM)
    outs = pl.pallas_call(
        body,
        name=name,
        in_specs=[vmem] + [ANY] * n_cin,
        out_specs=[vmem] + [ANY] * n_cout,
        out_shape=[SDS(pack.shape, F32)] + comm.out_shapes,
        scratch_shapes=[
            pltpu.VMEM((7, rows, D), F32),
            pltpu.SemaphoreType.DMA((7,)),
            pltpu.SemaphoreType.DMA((7,)),
            pltpu.SemaphoreType.DMA((7,)),
            pltpu.SemaphoreType.DMA((7,)),
        ]
        + comm.sem_shapes(),
        compiler_params=pltpu.CompilerParams(vmem_limit_bytes=VMEM_LIMIT_BYTES),
    )(pack, *comm.inputs)
    return outs[0], outs[1:]


def _pack(arrays, rows):
    flat = jnp.concatenate([a.reshape(-1).astype(F32) for a in arrays])
    return jnp.pad(flat, (0, rows * D - flat.shape[0])).reshape(rows, D)


def _unpack(pack, shapes):
    flat = pack.reshape(-1)
    out, off = [], 0
    for sh in shapes:
        size = 1
        for dim in sh:
            size *= dim
        out.append(flat[off : off + size].reshape(sh))
        off += size
    return out


def _block_diag_pairs(w):
    w = w.reshape(N_RNN_TILES, 2, HEAD_DIM, HEAD_DIM)
    z = jnp.zeros_like(w[:, 0])
    top = jnp.concatenate([w[:, 0], z], axis=2)
    bot = jnp.concatenate([z, w[:, 1]], axis=2)
    return jnp.concatenate([top, bot], axis=1)


def _diag_blocks(w2):
    a = w2[:, :HEAD_DIM, :HEAD_DIM]
    b = w2[:, HEAD_DIM:, HEAD_DIM:]
    return jnp.stack([a, b], axis=1).reshape(RNN_HEADS, HEAD_DIM, HEAD_DIM)


BIG = ("w_in", "w_branch_a", "w_branch_b", "w_out", "w_up", "w_down")
TRANSPOSED = ("w_in", "w_up")
SMALL = (
    "norm_mix_g", "conv_w", "conv_b", "lru_w_a", "lru_b_a", "lru_w_x", "lru_b_x", "lru_lambda",
    "sgu_ln_g", "sgu_ln_b", "sgu_w_s", "sgu_b_s", "norm_ffn_g", "final_norm_g",
)
WEIGHTS = (
    "norm_mix_g", "w_in", "conv_w", "conv_b", "lru_w_a", "lru_b_a", "lru_w_x", "lru_b_x", "lru_lambda", "sgu_ln_g",
    "sgu_ln_b", "sgu_w_s", "sgu_b_s", "w_branch_a", "w_branch_b", "w_out", "norm_ffn_g", "w_up", "w_down", "final_norm_g",
)

TM = 512
TM_NT = 2048
TN_IN = 1664
TN_UP = 1024
TKA = 512
TKA_PIECES = 256
TC = 512
TB = 256
TR = 256


GATHERS_RIDING = (
    {
        "in_proj": (1.0, [(0, "w_branch_a"), (0, "w_branch_b"), (0, "w_out"), (0, "w_up")]),
        "branch_a_fwd": (0.85, [(1, "w_in")]),
        "ffn_up": (1.0, [(0, "w_down")]),
        "ffn_down": (0.9, [(1, "w_branch_a"), (1, "w_branch_b"), (1, "w_out")]),
    },
    {"in_proj": (0.7, [(1, "w_up")]), "branch_a_fwd": (0.7, [(1, "w_down")])},
)


def _layer_forward(l, x, p, w, shards):
    def run(key, fn, *args, **kw):
        if key not in GATHERS_RIDING[l]:
            return fn(*args, **kw)
        pass_on_at, riding = GATHERS_RIDING[l][key]
        out, got = fn(*args, comm=_gather_comm([shards[l2][n2] for l2, n2 in riding], pass_on_at), **kw)
        for (l2, n2), full in zip(riding, got):
            w[l2][n2] = full.reshape(-1, D)
        return out

    proj, h = run("in_proj", _norm_matmul_nt, x, p["norm_mix_g"], w[l]["w_in"], tm=TM_NT, tn=TN_IN, name=f"in_proj_{l}")
    hseq, ya_pre = run(
        "branch_a_fwd", _branch_a_fwd, proj, p["conv_w"], p["conv_b"], p["wa2"], p["lru_b_a"], p["wx2"], p["lru_b_x"],
        p["lru_lambda"], tc=TC, name=f"branch_a_fwd_{l}",
    )
    yb_pre = _sgu_fwd(proj, p["sgu_ln_g"], p["sgu_ln_b"], p["wm"], p["sgu_bias"], tb=TB, name=f"sgu_fwd_{l}")
    x1, ya, yb = _merge_fwd(
        ya_pre, yb_pre, proj, x, w[l]["w_branch_a"], w[l]["w_branch_b"], w[l]["w_out"], tm=TM, name=f"merge_fwd_{l}"
    )
    f_pre, h2 = run("ffn_up", _norm_matmul_nt, x1, p["norm_ffn_g"], w[l]["w_up"], tm=TM_NT, tn=TN_UP, name=f"ffn_up_{l}")
    x2 = run("ffn_down", _matmul_nn_res, f_pre, w[l]["w_down"], x1, relu2=True, tm=TM, name=f"ffn_down_{l}")
    saved = dict(x=x, h=h, proj=proj, hseq=hseq, ya_pre=ya_pre, yb_pre=yb_pre, ya=ya, yb=yb, x1=x1, h2=h2, f_pre=f_pre)
    return x2, saved


def _layer_backward(l, dx2, dx2b, sv, p, w, core, waiting):
    parts, from_chips = {}, {}

    def by_device(g):
        return g.reshape(4, 2, -1, D)

    def with_sibling(name, g, got):
        parts[name] = _sum_with_sibling(by_device(g), got, core, name=f"sum_sibling_{name}_{l}")

    df_pre = _matmul_nt_drelu2(dx2b, w["w_down"], sv["f_pre"], tm=TM_NT, tn=TN_UP, name=f"ffn_down_bwd_{l}")
    g_down = _matmul_tn([sv["f_pre"]], dx2b, relu2=True, tka=TKA, name=f"grad_w_down_{l}")
    g_up, (got,) = _matmul_tn(
        [df_pre], sv["h2"], relu2=False, tka=TKA, name=f"grad_w_up_{l}", comm=_sibling_comm([by_device(g_down)])
    )
    with_sibling("w_down", g_down, got)
    (dx1, dx1b, g_norm_ffn), (got, from_chips[l, "w_down"]) = _matmul_nn_rmsnorm_bwd(
        [df_pre], w["w_up"], sv["x1"], p["norm_ffn_g"], dx2, tm=TM, name=f"ffn_up_bwd_{l}",
        comm=_merge_comms([_sibling_comm([by_device(g_up)]), _chips_comm([parts["w_down"]])]),
    )
    with_sibling("w_up", g_up, got)
    (merged, dya, dyb, dga, dgb, dya_pre, dyb_pre), (from_chips[l, "w_up"],) = _merge_bwd(
        dx1b, sv["ya"], sv["yb"], sv["proj"], w["w_branch_a"], w["w_branch_b"], w["w_out"], tm=TM, name=f"merge_bwd_{l}",
        comm=_chips_comm([parts["w_up"]]),
    )
    g_out = _matmul_tn([merged], dx1b, relu2=False, tka=TKA, name=f"grad_w_out_{l}")
    g_ba = _matmul_tn([sv["ya_pre"]], dya, relu2=False, tka=TKA_PIECES, name=f"grad_w_branch_a_{l}")
    g_bb = _matmul_tn([sv["yb_pre"]], dyb, relu2=False, tka=TKA, name=f"grad_w_branch_b_{l}")
    branch = (("w_out", g_out), ("w_branch_a", g_ba), ("w_branch_b", g_bb))
    (du, dv, g_ws, g_bs, g_lng, g_lnb), got = _sgu_bwd(
        dyb_pre, sv["proj"], p["sgu_ln_g"], p["sgu_ln_b"], p["wm"], p["wmt"], p["sgu_bias"], p["mask"], tb=TB,
        name=f"sgu_bwd_{l}", comm=_sibling_comm([by_device(g) for _, g in branch]),
    )
    for (name, g), landed in zip(branch, got):
        with_sibling(name, g, landed)
    riding = [((l, name), parts[name]) for name, _ in branch] + list(waiting)
    (dxr, dgr, g_cw, g_cb, g_ba_, g_bx, g_lam, g_wa2, g_wx2), got = _branch_a_bwd(
        dya_pre, sv["proj"], sv["hseq"], p["conv_w"], p["conv_b"], p["wa2"], p["lru_b_a"], p["wx2"], p["lru_b_x"],
        p["lru_lambda"], p["wa2t"], p["wx2t"], tc=TC, name=f"branch_a_bwd_{l}", comm=_chips_comm([part for _, part in riding]),
    )
    for (key, _), landed in zip(riding, got):
        from_chips[key] = landed
    dproj = [dxr, dgr, du, dv, dga, dgb]
    g_in = _matmul_tn(dproj, sv["h"], relu2=False, tka=TKA_PIECES, name=f"grad_w_in_{l}")
    (dx, dxb, g_norm_mix), (got,) = _matmul_nn_rmsnorm_bwd(
        dproj, w["w_in"], sv["x"], p["norm_mix_g"], dx1, tm=TM, name=f"in_proj_bwd_{l}", comm=_sibling_comm([by_device(g_in)])
    )
    with_sibling("w_in", g_in, got)
    small = dict(
        norm_mix_g=g_norm_mix[0], conv_w=g_cw, conv_b=g_cb[0], lru_w_a=_diag_blocks(g_wa2), lru_b_a=g_ba_.reshape(RNN_HEADS, HEAD_DIM),
        lru_w_x=_diag_blocks(g_wx2), lru_b_x=g_bx.reshape(RNN_HEADS, HEAD_DIM), lru_lambda=g_lam[0], sgu_ln_g=g_lng[0],
        sgu_ln_b=g_lnb[0], sgu_w_s=g_ws, sgu_b_s=g_bs[:, :, 0], norm_ffn_g=g_norm_ffn[0],
    )
    return dx, dxb, small, parts, from_chips


def _prepare_small(l, given):
    chunk_id = jnp.arange(SGU_BLOCK) // CHUNK
    mask = (chunk_id[:, None] >= chunk_id[None, :]).astype(F32)
    wm = given["sgu_w_s"][l] * mask
    wa2 = _block_diag_pairs(given["lru_w_a"][l])
    wx2 = _block_diag_pairs(given["lru_w_x"][l])
    row = lambda a: a.reshape(1, -1)
    return dict(
        norm_mix_g=row(given["norm_mix_g"][l]),
        norm_ffn_g=row(given["norm_ffn_g"][l]),
        conv_w=given["conv_w_full"][l],
        conv_b=row(given["conv_b"][l]),
        wa2=wa2.astype(BF16),
        wx2=wx2.astype(BF16),
        wa2t=jnp.swapaxes(wa2, 1, 2).astype(BF16),
        wx2t=jnp.swapaxes(wx2, 1, 2).astype(BF16),
        lru_b_a=row(given["lru_b_a"][l]),
        lru_b_x=row(given["lru_b_x"][l]),
        lru_lambda=row(given["lru_lambda"][l]),
        sgu_ln_g=row(given["sgu_ln_g"][l]),
        sgu_ln_b=row(given["sgu_ln_b"][l]),
        wm=wm.astype(BF16),
        wmt=jnp.swapaxes(wm, 1, 2).astype(BF16),
        sgu_bias=jnp.broadcast_to(given["sgu_b_s"][l][:, :, None], (SGU_GROUPS, SGU_BLOCK, LANES)),
        mask=mask,
    )


def _step(given):
    x_idx, y_idx, c_idx = _position()
    dev = 4 * x_idx + 2 * y_idx + c_idx
    core = c_idx.astype(jnp.int32).reshape(1)
    chip = (2 * x_idx + y_idx).astype(jnp.int32).reshape(1)

    def rows_first(name, a):
        return jnp.swapaxes(a, 1, 2) if name in TRANSPOSED else a

    shards = []
    for l in range(DEPTH):
        shards.append({name: rows_first(name, given[name])[l].astype(BF16)[None] for name in BIG})
    conv_mine = given["conv_w"].reshape(1, DEPTH * CONV_WIDTH, D_RNN // N_DEV)
    w_in_first, conv_all = _comm_only(_gather_comm([shards[0]["w_in"], conv_mine]), name="gather_first")
    weights = [{"w_in": w_in_first.reshape(-1, D)}, {}]
    conv_all = conv_all.reshape(N_DEV, DEPTH, CONV_WIDTH, D_RNN // N_DEV)
    given = dict(given, conv_w_full=jnp.moveaxis(conv_all, 0, 2).reshape(DEPTH, CONV_WIDTH, D_RNN))

    small_params = [_prepare_small(l, given) for l in range(DEPTH)]
    x = given["x"][0]
    saved = []
    for l in range(DEPTH):
        x, sv = _layer_forward(l, x, small_params[l], weights, shards)
        saved.append(sv)
    dx, dxb, g_final, loss = _final_loss(x, given["final_norm_g"].reshape(1, D), given["loss_target"][0], tm=TM, name="final_loss")
    small_grads, parts, from_chips, waiting = [None] * DEPTH, [None] * DEPTH, {}, []
    for l in reversed(range(DEPTH)):
        dx, dxb, small_grads[l], parts[l], got = _layer_backward(
            l, dx, dxb, saved[l], small_params[l], weights[l], core, waiting
        )
        from_chips.update(got)
        waiting = [((l, "w_in"), parts[l]["w_in"])]

    small_list = []
    for name in SMALL[:-1]:
        small_list.append(jnp.stack([small_grads[l][name] for l in range(DEPTH)]))
    small_list += [g_final[0], loss[0, :1]]
    small_shapes = [a.shape for a in small_list]
    pack = _pack(small_list, SMALL_ROWS).reshape(N_DEV, SMALL_ROWS_PER_DEV, D)
    summed, (from_chips[0, "w_in"],) = _all_reduce_small(pack, _chips_comm([parts[0]["w_in"]]), name="all_reduce_small")
    summed = _unpack(summed, small_shapes)
    loss_total = summed[-1][0]
    grads = dict(zip(SMALL, summed[:-1]))
    cw = grads["conv_w"].reshape(DEPTH, CONV_WIDTH, N_DEV, D_RNN // N_DEV)
    grads["conv_w"] = lax.dynamic_index_in_dim(cw, dev, axis=2, keepdims=False)

    delta, new_m, new_v = {}, {}, {}
    for name in BIG:
        w, m, v = given[name], given["m_" + name], given["v_" + name]
        mine = [parts[l][name] for l in range(DEPTH)]
        theirs = [from_chips[l, name] for l in range(DEPTH)]
        if name == "w_up":
            sums = [_sum_chips(mine[l], theirs[l], chip, name=f"sum_chips_{name}_{l}").T for l in range(DEPTH)]
            out = _adamw_layers(w, sums, m, v, tr=TR, name=f"adamw_{name}")
        else:
            out = _adamw_reduced(
                rows_first(name, w), mine, theirs, rows_first(name, m), rows_first(name, v), chip, tr=TR, name=f"adamw_{name}"
            )
            out = [rows_first(name, a) for a in out]
        grads[name], delta[name], new_m[name], new_v[name] = out
    two_d = lambda a: a.reshape(1, -1) if a.ndim == 1 else a
    groups = [tuple(two_d(a) for a in (given[n], grads[n], given["m_" + n], given["v_" + n])) for n in SMALL]
    for n, (d, m2, v2) in zip(SMALL, _adamw_small(groups, name="adamw_small")):
        shape = given[n].shape
        delta[n], new_m[n], new_v[n] = d.reshape(shape), m2.reshape(shape), v2.reshape(shape)

    return (
        loss_total, dx[None],
        *[grads[n] for n in WEIGHTS], *[delta[n] for n in WEIGHTS], *[new_m[n] for n in WEIGHTS], *[new_v[n] for n in WEIGHTS],
    )


def kernel(x, norm_mix_g, w_in, conv_w, conv_b, lru_w_a, lru_b_a, lru_w_x, lru_b_x, lru_lambda, sgu_ln_g, sgu_ln_b, sgu_w_s, sgu_b_s, w_branch_a, w_branch_b, w_out, norm_ffn_g, w_up, w_down, final_norm_g, loss_target, m_norm_mix_g, m_w_in, m_conv_w, m_conv_b, m_lru_w_a, m_lru_b_a, m_lru_w_x, m_lru_b_x, m_lru_lambda, m_sgu_ln_g, m_sgu_ln_b, m_sgu_w_s, m_sgu_b_s, m_w_branch_a, m_w_branch_b, m_w_out, m_norm_ffn_g, m_w_up, m_w_down, m_final_norm_g, v_norm_mix_g, v_w_in, v_conv_w, v_conv_b, v_lru_w_a, v_lru_b_a, v_lru_w_x, v_lru_b_x, v_lru_lambda, v_sgu_ln_g, v_sgu_ln_b, v_sgu_w_s, v_sgu_b_s, v_w_branch_a, v_w_branch_b, v_w_out, v_norm_ffn_g, v_w_up, v_w_down, v_final_norm_g):
    return _step(dict(locals()))
```

```python
import jax
import jax.numpy as jnp
from jax import lax
from jax.experimental import pallas as pl
from jax.experimental.pallas import tpu as pltpu

F32 = jnp.float32
BF16 = jnp.bfloat16
SDS = jax.ShapeDtypeStruct
MESH = pl.DeviceIdType.MESH

D = 1024
D_RNN = 1280
D_SGU = 1024
D_FF = 4096
D_IN = 2 * D_RNN + 2 * D_SGU + 2 * D
DEPTH = 2
RNN_HEADS = 20
HEAD_DIM = 64
CONV_WIDTH = 4
LRU_C = 8.0
SGU_GROUPS = 8
SGU_BLOCK = 128
CHUNK = 64
EPS = 1e-6
N_DEV = 8

ADAM_LR = 0.001
ADAM_B1 = 0.9
ADAM_B2 = 0.999
ADAM_EPS = 1e-08
ADAM_WD = 0.01
ADAM_STEP = 10

LANES = 128
SUBLANES = 8
VMEM_LIMIT_BYTES = 56 * 1024 * 1024

N_RNN_TILES = D_RNN // LANES
RNN_TILES_PER_STEP = 5
GRNN_BLK128 = D_RNN // LANES
U_BLK512 = (2 * D_RNN) // 512
V_BLK512 = (2 * D_RNN + D_SGU) // 512
GA_BLK512 = (2 * D_RNN + 2 * D_SGU) // 512
GB_BLK512 = (2 * D_RNN + 2 * D_SGU + D) // 512

SMALL_ROWS_PER_DEV = 80
SMALL_ROWS = N_DEV * SMALL_ROWS_PER_DEV


def _params(*sem):
    return pltpu.CompilerParams(dimension_semantics=sem, vmem_limit_bytes=VMEM_LIMIT_BYTES)


def _sigmoid(x):
    return 0.5 + 0.5 * jnp.tanh(0.5 * x)


_GELU_C = 0.7978845608028654
_GELU_K = 0.044715


def _gelu(x):
    t = jnp.tanh(_GELU_C * (x + _GELU_K * x * x * x))
    return 0.5 * x * (1.0 + t)


def _gelu_and_grad(x):
    t = jnp.tanh(_GELU_C * (x + _GELU_K * x * x * x))
    val = 0.5 * x * (1.0 + t)
    grad = 0.5 * (1.0 + t) + 0.5 * x * (1.0 - t * t) * _GELU_C * (1.0 + 3.0 * _GELU_K * x * x)
    return val, grad


def _one_minus_square(log_a, a):
    return -jnp.tanh(log_a) * (1.0 + a * a)


def _dot(a, b):
    return jnp.dot(a, b, preferred_element_type=F32)


def _dot_nt(a, b):
    return lax.dot_general(a, b, (((1,), (1,)), ((), ())), preferred_element_type=F32)


def _dot_tn(a, b):
    return lax.dot_general(a, b, (((0,), (0,)), ((), ())), preferred_element_type=F32)


def _norm_matmul_nt(x, g, w, *, tm, tn, name, comm=None):
    s, n = x.shape[0], w.shape[0]
    tm, tn = min(tm, s), min(tn, n)

    def body(x_ref, g_ref, w_ref, o_ref, h_ref):
        @pl.when(pl.program_id(1) == 0)
        def _():
            xv = x_ref[...]
            r = lax.rsqrt(jnp.mean(xv * xv, axis=-1, keepdims=True) + EPS)
            h_ref[...] = (xv * r * g_ref[...]).astype(BF16)

        o_ref[...] = _dot_nt(h_ref[...], w_ref[...]).astype(o_ref.dtype)

    return _call(
        body,
        (x, g, w),
        name=name,
        grid=(s // tm, n // tn),
        in_specs=[
            pl.BlockSpec((tm, D), lambda i, j: (i, 0)),
            pl.BlockSpec((1, D), lambda i, j: (0, 0)),
            pl.BlockSpec((tn, D), lambda i, j: (j, 0)),
        ],
        out_specs=[pl.BlockSpec((tm, tn), lambda i, j: (i, j)), pl.BlockSpec((tm, D), lambda i, j: (i, 0))],
        out_shape=[SDS((s, n), BF16), SDS((s, D), BF16)],
        semantics=("parallel", "arbitrary"),
        comm=comm,
    )


def _matmul_nn_res(a, w, res, *, relu2, tm, name, comm=None):
    s, k = a.shape
    tm = min(tm, s)

    def body(a_ref, w_ref, r_ref, o_ref):
        av = a_ref[...]
        if relu2:
            t = jnp.maximum(av.astype(F32), 0.0)
            av = (t * t).astype(BF16)
        o_ref[...] = r_ref[...] + _dot(av, w_ref[...])

    return _call(
        body,
        (a, w, res),
        name=name,
        grid=(s // tm,),
        in_specs=[
            pl.BlockSpec((tm, k), lambda i: (i, 0)),
            pl.BlockSpec((k, D), lambda i: (0, 0)),
            pl.BlockSpec((tm, D), lambda i: (i, 0)),
        ],
        out_specs=pl.BlockSpec((tm, D), lambda i: (i, 0)),
        out_shape=SDS((s, D), F32),
        semantics=("parallel",),
        comm=comm,
    )


def _matmul_nt_drelu2(a, w, pre, *, tm, tn, name):
    s, n = a.shape[0], w.shape[0]
    tm, tn = min(tm, s), min(tn, n)

    def body(a_ref, w_ref, p_ref, o_ref):
        d = _dot_nt(a_ref[...], w_ref[...])
        o_ref[...] = (d * (2.0 * jnp.maximum(p_ref[...].astype(F32), 0.0))).astype(o_ref.dtype)

    return pl.pallas_call(
        body,
        name=name,
        grid=(s // tm, n // tn),
        in_specs=[
            pl.BlockSpec((tm, D), lambda i, j: (i, 0)),
            pl.BlockSpec((tn, D), lambda i, j: (j, 0)),
            pl.BlockSpec((tm, tn), lambda i, j: (i, j)),
        ],
        out_specs=pl.BlockSpec((tm, tn), lambda i, j: (i, j)),
        out_shape=SDS((s, n), BF16),
        compiler_params=_params("parallel", "arbitrary"),
    )(a, w, pre)


def _matmul_tn(a_list, b, *, relu2, tka, name, comm=None):
    s = b.shape[0]
    n = len(a_list)
    nblk = [a.shape[1] // tka for a in a_list]
    starts = [sum(nblk[:p]) for p in range(n)]

    def body(*refs):
        a_refs, b_ref, o_ref = refs[:n], refs[n], refs[n + 1]
        i = pl.program_id(0)
        for p in range(n):

            @pl.when((i >= starts[p]) & (i < starts[p] + nblk[p]))
            def _(p=p):
                av = a_refs[p][...]
                if relu2:
                    t = jnp.maximum(av.astype(F32), 0.0)
                    av = (t * t).astype(BF16)
                o_ref[...] = _dot_tn(av, b_ref[...]).astype(o_ref.dtype)

    def piece_spec(p):
        return pl.BlockSpec((s, tka), lambda i: (0, jnp.clip(i - starts[p], 0, nblk[p] - 1)))

    return _call(
        body,
        (*a_list, b),
        name=name,
        grid=(sum(nblk),),
        in_specs=[piece_spec(p) for p in range(n)] + [pl.BlockSpec((s, D), lambda i: (0, 0))],
        out_specs=pl.BlockSpec((tka, D), lambda i: (i, 0)),
        out_shape=SDS((sum(nblk) * tka, D), BF16),
        semantics=("parallel",),
        comm=comm,
    )


def _matmul_nn_rmsnorm_bwd(a_list, w, x, g, res, *, tm, name, comm=None):
    s = x.shape[0]
    tm = min(tm, s)
    n = len(a_list)
    widths = [a.shape[1] for a in a_list]
    offs = [sum(widths[:p]) for p in range(n)]
    k = sum(widths)

    def body(*refs):
        a_refs = refs[:n]
        w_ref, x_ref, g_ref, r_ref, dx_ref, dxb_ref, dg_ref = refs[n:]

        @pl.when(pl.program_id(0) == 0)
        def _():
            dg_ref[...] = jnp.zeros_like(dg_ref)

        dh = _dot(a_refs[0][...], w_ref[0 : widths[0], :])
        for p in range(1, n):
            dh += _dot(a_refs[p][...], w_ref[offs[p] : offs[p] + widths[p], :])
        xv = x_ref[...]
        r = lax.rsqrt(jnp.mean(xv * xv, axis=-1, keepdims=True) + EPS)
        xhat = xv * r
        dxh = dh * g_ref[...]
        dx = r_ref[...] + r * (dxh - xhat * jnp.mean(dxh * xhat, axis=-1, keepdims=True))
        dx_ref[...] = dx
        dxb_ref[...] = dx.astype(BF16)
        dg_ref[...] += jnp.sum(dh * xhat, axis=0, keepdims=True)

    act = pl.BlockSpec((tm, D), lambda i: (i, 0))
    vec = pl.BlockSpec((1, D), lambda i: (0, 0))
    return _call(
        body,
        (*a_list, w, x, g, res),
        name=name,
        grid=(s // tm,),
        in_specs=[pl.BlockSpec((tm, wd), lambda i: (i, 0)) for wd in widths]
        + [pl.BlockSpec((k, D), lambda i: (0, 0), pipeline_mode=pl.Buffered(1)), act, vec, act],
        out_specs=[act, act, vec],
        out_shape=[SDS((s, D), F32), SDS((s, D), BF16), SDS((1, D), F32)],
        semantics=("arbitrary",),
        comm=comm,
    )


def _rows_before(ext, k):
    if k == 0:
        return ext[SUBLANES:, :]
    return pltpu.roll(ext, k, 0)[SUBLANES:, :]


def _rows_after(ext, k, n):
    if k == 0:
        return ext[:n, :]
    return pltpu.roll(ext, n + SUBLANES - k, 0)[:n, :]


def _scan_forward(a, b, n):
    row = lax.broadcasted_iota(jnp.int32, a.shape, 0)
    d = 1
    while d < n:
        if d < SUBLANES:
            m = row >= d
            a_s = jnp.where(m, pltpu.roll(a, d, 0), 1.0)
            b_s = jnp.where(m, pltpu.roll(b, d, 0), 0.0)
            b = a * b_s + b
            a = a * a_s
        else:
            b = jnp.concatenate([b[:d], a[d:] * b[: n - d] + b[d:]], axis=0)
            a = jnp.concatenate([a[:d], a[d:] * a[: n - d]], axis=0)
        d *= 2
    return a, b


def _scan_backward(a, b, n):
    row = lax.broadcasted_iota(jnp.int32, a.shape, 0)
    d = 1
    while d < n:
        if d < SUBLANES:
            m = row < n - d
            a_s = jnp.where(m, pltpu.roll(a, n - d, 0), 1.0)
            b_s = jnp.where(m, pltpu.roll(b, n - d, 0), 0.0)
            b = a * b_s + b
            a = a * a_s
        else:
            b = jnp.concatenate([a[: n - d] * b[d:] + b[: n - d], b[n - d :]], axis=0)
            a = jnp.concatenate([a[: n - d] * a[d:], a[n - d :]], axis=0)
        d *= 2
    return b


def _repeat_matrix(n):
    groups = n // SUBLANES
    return (jnp.arange(n)[:, None] // SUBLANES == jnp.arange(3 * groups)[None, :] % groups).astype(BF16)


def _scan_rows(a, b, n, repeat_ref, a_scr, b_scr, reverse):
    groups = n // SUBLANES
    a3 = a.reshape(groups, SUBLANES, LANES)
    b3 = b.reshape(groups, SUBLANES, LANES)
    sub = lax.broadcasted_iota(jnp.int32, a3.shape, 1)
    for d in (1, 2, 4):
        m = (sub < SUBLANES - d) if reverse else (sub >= d)
        shift = SUBLANES - d if reverse else d
        a_s = jnp.where(m, pltpu.roll(a3, shift, 1), 1.0)
        b_s = jnp.where(m, pltpu.roll(b3, shift, 1), 0.0)
        b3 = a3 * b_s + b3
        a3 = a3 * a_s
    a_scr[...] = a3.reshape(n, LANES)
    b_scr[...] = b3.reshape(n, LANES)
    edge = 0 if reverse else SUBLANES - 1
    a_tot = a_scr[pl.ds(edge, groups, stride=SUBLANES), :]
    b_tot = b_scr[pl.ds(edge, groups, stride=SUBLANES), :]
    row = lax.broadcasted_iota(jnp.int32, a_tot.shape, 0)
    if reverse:
        through = _scan_backward(a_tot, b_tot, groups)
        entering = jnp.where(row < groups - 1, pltpu.roll(through, groups - 1, 0), 0.0)
    else:
        _, through = _scan_forward(a_tot, b_tot, groups)
        entering = jnp.where(row >= 1, pltpu.roll(through, 1, 0), 0.0)
    hi = entering.astype(BF16)
    rest = entering - hi.astype(F32)
    mid = rest.astype(BF16)
    lo = (rest - mid.astype(F32)).astype(BF16)
    repeated = _dot(repeat_ref[...], jnp.concatenate([hi, mid, lo], axis=0))
    return b_scr[...] + a_scr[...] * repeated


def _softplus_neg(lam):
    z = -lam
    return jnp.maximum(z, 0.0) + jnp.log1p(jnp.exp(-jnp.abs(z)))


def _conv_and_gates(xc, xprev, cw_ref, cb_ref, wa_ref, ba_ref, wx_ref, bx_ref, lam_ref):
    ext = jnp.concatenate([xprev, xc], axis=0)
    x1, x2, x3 = _rows_before(ext, 1), _rows_before(ext, 2), _rows_before(ext, 3)
    xr = cb_ref[...] + x3 * cw_ref[0:1, :] + x2 * cw_ref[1:2, :] + x1 * cw_ref[2:3, :] + xc * cw_ref[3:4, :]
    xrb = xr.astype(BF16)
    r = _sigmoid(_dot(xrb, wa_ref[...]) + ba_ref[...])
    i = _sigmoid(_dot(xrb, wx_ref[...]) + bx_ref[...])
    sp = _softplus_neg(lam_ref[...])
    log_a = (-LRU_C * r) * sp
    a = jnp.exp(log_a)
    return xr, (x1, x2, x3), r, i, a, _one_minus_square(log_a, a)


def _branch_a_fwd(proj, cw, cb, wa2, ba, wx2, bx, lam, *, tc, name, comm=None):
    s = proj.shape[0]
    tc = min(tc, s)

    def body(x_ref, g_ref, cw_ref, cb_ref, wa_ref, ba_ref, wx_ref, bx_ref, lam_ref, rep_ref, h_ref, y_ref,
             xprev, hlast, a_scr, b_scr):
        @pl.when(pl.program_id(1) == 0)
        def _():
            xprev[...] = jnp.zeros_like(xprev)
            hlast[...] = jnp.zeros_like(hlast)

        for t in range(RNN_TILES_PER_STEP):
            cols = lambda ref: ref.at[:, pl.ds(t * LANES, LANES)]
            one_tile(
                cols(x_ref), cols(g_ref), cols(cw_ref), cols(cb_ref), wa_ref.at[t], cols(ba_ref), wx_ref.at[t], cols(bx_ref),
                cols(lam_ref), rep_ref, cols(h_ref), cols(y_ref), cols(xprev), cols(hlast), a_scr.at[t], b_scr.at[t],
            )

    def one_tile(x_ref, g_ref, cw_ref, cb_ref, wa_ref, ba_ref, wx_ref, bx_ref, lam_ref, rep_ref, h_ref, y_ref,
                 xprev, hlast, a_scr, b_scr):
        xc = x_ref[...].astype(F32)
        xr, _, r, i, a, om = _conv_and_gates(xc, xprev[...], cw_ref, cb_ref, wa_ref, ba_ref, wx_ref, bx_ref, lam_ref)
        xprev[...] = xc[tc - SUBLANES :, :]
        u = jnp.sqrt(om) * (i * xr)
        row8 = lax.broadcasted_iota(jnp.int32, (SUBLANES, LANES), 0)
        first = u[:SUBLANES] + jnp.where(row8 == 0, a[:SUBLANES] * hlast[SUBLANES - 1 : SUBLANES, :], 0.0)
        h = _scan_rows(a, jnp.concatenate([first, u[SUBLANES:]], axis=0), tc, rep_ref, a_scr, b_scr, reverse=False)
        hlast[...] = h[tc - SUBLANES :, :]
        h_ref[...] = h
        y_ref[...] = (h * _gelu(g_ref[...].astype(F32))).astype(BF16)

    wide = RNN_TILES_PER_STEP * LANES
    tile = lambda j, c: (0, j)
    vec = pl.BlockSpec((1, wide), tile)
    mats = pl.BlockSpec((RNN_TILES_PER_STEP, LANES, LANES), lambda j, c: (j, 0, 0))
    repeat = _repeat_matrix(tc)
    return _call(
        body,
        (proj, proj, cw, cb, wa2, ba, wx2, bx, lam, repeat),
        name=name,
        grid=(N_RNN_TILES // RNN_TILES_PER_STEP, s // tc),
        in_specs=[
            pl.BlockSpec((tc, wide), lambda j, c: (c, j)),
            pl.BlockSpec((tc, wide), lambda j, c: (c, D_RNN // wide + j)),
            pl.BlockSpec((CONV_WIDTH, wide), tile),
            vec,
            mats,
            vec,
            mats,
            vec,
            vec,
            pl.BlockSpec(repeat.shape, lambda j, c: (0, 0)),
        ],
        out_specs=[pl.BlockSpec((tc, wide), lambda j, c: (c, j)), pl.BlockSpec((tc, wide), lambda j, c: (c, j))],
        out_shape=[SDS((s, D_RNN), F32), SDS((s, D_RNN), BF16)],
        scratch_shapes=[pltpu.VMEM((SUBLANES, wide), F32)] * 2 + [pltpu.VMEM((RNN_TILES_PER_STEP, tc, LANES), F32)] * 2,
        semantics=("parallel", "arbitrary"),
        comm=comm,
    )


def _branch_a_bwd(dy, proj, h, cw, cb, wa2, ba, wx2, bx, lam, wa2t, wx2t, *, tc, name, comm=None):
    s = proj.shape[0]
    tc = min(tc, s)
    nc = s // tc
    halo16 = tc // 16
    halo8 = tc // SUBLANES

    def body(dy_ref, x_ref, xh_ref, g_ref, h_ref, hh_ref, cw_ref, cb_ref, wa_ref, ba_ref, wx_ref, bx_ref, lam_ref,
             wat_ref, wxt_ref, rep_ref, dx_ref, dg_ref, dcw_ref, dcb_ref, dba_ref, dbx_ref, dlam_ref, dwa_ref, dwx_ref,
             carry, dxr_next, a_scr, b_scr):
        cc = pl.program_id(1)
        ct = nc - 1 - cc

        @pl.when(cc == 0)
        def _():
            carry[...] = jnp.zeros_like(carry)
            dxr_next[...] = jnp.zeros_like(dxr_next)
            for ref in (dcw_ref, dcb_ref, dba_ref, dbx_ref, dlam_ref, dwa_ref, dwx_ref):
                ref[...] = jnp.zeros_like(ref)

        for t in range(RNN_TILES_PER_STEP):
            cols = lambda ref: ref.at[:, pl.ds(t * LANES, LANES)]
            one_tile(
                ct, cols(dy_ref), cols(x_ref), cols(xh_ref), cols(g_ref), cols(h_ref), cols(hh_ref), cols(cw_ref), cols(cb_ref),
                wa_ref.at[t], cols(ba_ref), wx_ref.at[t], cols(bx_ref), cols(lam_ref), wat_ref.at[t], wxt_ref.at[t], rep_ref,
                cols(dx_ref), cols(dg_ref), cols(dcw_ref), cols(dcb_ref), cols(dba_ref), cols(dbx_ref), cols(dlam_ref),
                dwa_ref.at[t], dwx_ref.at[t], cols(carry), cols(dxr_next), a_scr.at[t], b_scr.at[t],
            )

    def one_tile(ct, dy_ref, x_ref, xh_ref, g_ref, h_ref, hh_ref, cw_ref, cb_ref, wa_ref, ba_ref, wx_ref, bx_ref, lam_ref,
                 wat_ref, wxt_ref, rep_ref, dx_ref, dg_ref, dcw_ref, dcb_ref, dba_ref, dbx_ref, dlam_ref, dwa_ref, dwx_ref,
                 carry, dxr_next, a_scr, b_scr):
        xc = x_ref[...].astype(F32)
        xprev = jnp.where(ct > 0, xh_ref[SUBLANES:, :].astype(F32), 0.0)
        xr, (x1, x2, x3), r, i, a, om = _conv_and_gates(
            xc, xprev, cw_ref, cb_ref, wa_ref, ba_ref, wx_ref, bx_ref, lam_ref
        )
        inv_norm = lax.rsqrt(om)
        norm = om * inv_norm
        row = lax.broadcasted_iota(jnp.int32, xc.shape, 0)

        hv = h_ref[...]
        ge, ge_grad = _gelu_and_grad(g_ref[...].astype(F32))
        dyv = dy_ref[...].astype(F32)
        dg_ref[...] = (dyv * hv * ge_grad).astype(dg_ref.dtype)
        dh = dyv * ge

        b = dh + jnp.where(row == tc - 1, carry[0:1, :], 0.0)
        a_next = jnp.where(row < tc - 1, pltpu.roll(a, tc - 1, 0), 0.0)
        gadj = _scan_rows(a_next, b, tc, rep_ref, a_scr, b_scr, reverse=True)
        carry[...] = (a * gadj)[:SUBLANES, :]

        hprev_first = jnp.where(ct > 0, hh_ref[SUBLANES - 1 : SUBLANES, :], 0.0)
        hprev = jnp.where(row >= 1, pltpu.roll(hv, 1, 0), hprev_first)
        da = gadj * hprev
        ix = i * xr
        dnorm = gadj * ix
        di = gadj * norm * xr
        dlog_a = da * a - dnorm * (1.0 - om) * inv_norm
        sp = _softplus_neg(lam_ref[...])
        dr = dlog_a * (-LRU_C * sp)
        dsp = jnp.sum(dlog_a * (-LRU_C * r), axis=0, keepdims=True)
        dlam_ref[...] += dsp * (-_sigmoid(-lam_ref[...]))
        dza = dr * r * (1.0 - r)
        dzx = di * i * (1.0 - i)
        dzab, dzxb = dza.astype(BF16), dzx.astype(BF16)
        dxr = gadj * norm * i + _dot(dzab, wat_ref[...]) + _dot(dzxb, wxt_ref[...])
        xrb = xr.astype(BF16)
        dwa_ref[...] += _dot_tn(xrb, dzab)
        dwx_ref[...] += _dot_tn(xrb, dzxb)
        dba_ref[...] += jnp.sum(dza, axis=0, keepdims=True)
        dbx_ref[...] += jnp.sum(dzx, axis=0, keepdims=True)

        ext = jnp.concatenate([dxr, dxr_next[...]], axis=0)
        dx = (
            dxr * cw_ref[3:4, :]
            + _rows_after(ext, 1, tc) * cw_ref[2:3, :]
            + _rows_after(ext, 2, tc) * cw_ref[1:2, :]
            + _rows_after(ext, 3, tc) * cw_ref[0:1, :]
        )
        dxr_next[...] = dxr[:SUBLANES, :]
        dx_ref[...] = dx.astype(dx_ref.dtype)
        dcb_ref[...] += jnp.sum(dxr, axis=0, keepdims=True)
        dcw_ref[3:4, :] += jnp.sum(dxr * xc, axis=0, keepdims=True)
        dcw_ref[2:3, :] += jnp.sum(dxr * x1, axis=0, keepdims=True)
        dcw_ref[1:2, :] += jnp.sum(dxr * x2, axis=0, keepdims=True)
        dcw_ref[0:1, :] += jnp.sum(dxr * x3, axis=0, keepdims=True)

    wide = RNN_TILES_PER_STEP * LANES
    tile = lambda j, c: (0, j)
    mat = lambda j, c: (j, 0, 0)
    cur = lambda j, c: (nc - 1 - c, j)
    vec = pl.BlockSpec((1, wide), tile)
    matspec = pl.BlockSpec((RNN_TILES_PER_STEP, LANES, LANES), mat)
    repeat = _repeat_matrix(tc)
    return _call(
        body,
        (dy, proj, proj, proj, h, h, cw, cb, wa2, ba, wx2, bx, lam, wa2t, wx2t, repeat),
        name=name,
        grid=(N_RNN_TILES // RNN_TILES_PER_STEP, nc),
        in_specs=[
            pl.BlockSpec((tc, wide), cur),
            pl.BlockSpec((tc, wide), cur),
            pl.BlockSpec((16, wide), lambda j, c: (jnp.maximum((nc - 1 - c) * halo16 - 1, 0), j)),
            pl.BlockSpec((tc, wide), lambda j, c: (nc - 1 - c, D_RNN // wide + j)),
            pl.BlockSpec((tc, wide), cur),
            pl.BlockSpec((SUBLANES, wide), lambda j, c: (jnp.maximum((nc - 1 - c) * halo8 - 1, 0), j)),
            pl.BlockSpec((CONV_WIDTH, wide), tile),
            vec,
            matspec,
            vec,
            matspec,
            vec,
            vec,
            matspec,
            matspec,
            pl.BlockSpec(repeat.shape, lambda j, c: (0, 0)),
        ],
        out_specs=[
            pl.BlockSpec((tc, wide), cur),
            pl.BlockSpec((tc, wide), cur),
            pl.BlockSpec((CONV_WIDTH, wide), tile),
            vec,
            vec,
            vec,
            vec,
            matspec,
            matspec,
        ],
        out_shape=[
            SDS((s, D_RNN), BF16),
            SDS((s, D_RNN), BF16),
            SDS((CONV_WIDTH, D_RNN), F32),
            SDS((1, D_RNN), F32),
            SDS((1, D_RNN), F32),
            SDS((1, D_RNN), F32),
            SDS((1, D_RNN), F32),
            SDS((N_RNN_TILES, LANES, LANES), F32),
            SDS((N_RNN_TILES, LANES, LANES), F32),
        ],
        scratch_shapes=[pltpu.VMEM((SUBLANES, wide), F32)] * 2 + [pltpu.VMEM((RNN_TILES_PER_STEP, tc, LANES), F32)] * 2,
        semantics=("parallel", "arbitrary"),
        comm=comm,
    )


def _sgu_specs(tb):
    half = lambda blk: pl.BlockSpec((tb, 512), lambda n: (n, blk))
    return [half(U_BLK512), half(U_BLK512 + 1), half(V_BLK512), half(V_BLK512 + 1)]


def _sgu_normed(v, lng_ref, lnb_ref):
    gv, gv_grad = _gelu_and_grad(v)
    mu = jnp.mean(gv, axis=-1, keepdims=True)
    xc = gv - mu
    rs = lax.rsqrt(jnp.mean(xc * xc, axis=-1, keepdims=True) + EPS)
    xhat = xc * rs
    return xhat * lng_ref[...] + lnb_ref[...], xhat, rs, gv_grad


def _sgu_fwd(proj, lng, lnb, wm, bias, *, tb, name):
    s = proj.shape[0]
    tb = min(tb, s)

    def body(u0_ref, u1_ref, v0_ref, v1_ref, lng_ref, lnb_ref, wm_ref, bias_ref, y_ref):
        u = jnp.concatenate([u0_ref[...], u1_ref[...]], axis=1).astype(F32)
        v = jnp.concatenate([v0_ref[...], v1_ref[...]], axis=1).astype(F32)
        gu = _gelu(u)
        vn, _, _, _ = _sgu_normed(v, lng_ref, lnb_ref)
        vnb = vn.astype(BF16)
        for blk in range(tb // SGU_BLOCK):
            rows = slice(blk * SGU_BLOCK, (blk + 1) * SGU_BLOCK)
            for g in range(SGU_GROUPS):
                cols = slice(g * LANES, (g + 1) * LANES)
                mixed = _dot(wm_ref[g], vnb[rows, cols]) + bias_ref[g]
                y_ref[rows, cols] = (gu[rows, cols] * mixed).astype(BF16)

    const2 = lambda n: (0, 0)
    const3 = lambda n: (0, 0, 0)
    return pl.pallas_call(
        body,
        name=name,
        grid=(s // tb,),
        in_specs=_sgu_specs(tb)
        + [
            pl.BlockSpec((1, D_SGU), const2),
            pl.BlockSpec((1, D_SGU), const2),
            pl.BlockSpec((SGU_GROUPS, SGU_BLOCK, SGU_BLOCK), const3),
            pl.BlockSpec((SGU_GROUPS, SGU_BLOCK, LANES), const3),
        ],
        out_specs=pl.BlockSpec((tb, D_SGU), lambda n: (n, 0)),
        out_shape=SDS((s, D_SGU), BF16),
        compiler_params=_params("parallel"),
    )(proj, proj, proj, proj, lng, lnb, wm, bias)


def _sgu_bwd(dy, proj, lng, lnb, wm, wmt, bias, mask, *, tb, name, comm=None):
    s = proj.shape[0]
    tb = min(tb, s)
    nb = s // tb

    def body(dy_ref, u0_ref, u1_ref, v0_ref, v1_ref, lng_ref, lnb_ref, wm_ref, wmt_ref, bias_ref, mask_ref,
             du_ref, dv_ref, dws_ref, dbs_ref, dlng_ref, dlnb_ref, dvn_scr, dbs_acc):
        n = pl.program_id(0)

        @pl.when(n == 0)
        def _():
            dbs_acc[...] = jnp.zeros_like(dbs_acc)
            for ref in (dws_ref, dlng_ref, dlnb_ref):
                ref[...] = jnp.zeros_like(ref)

        u = jnp.concatenate([u0_ref[...], u1_ref[...]], axis=1).astype(F32)
        v = jnp.concatenate([v0_ref[...], v1_ref[...]], axis=1).astype(F32)
        gu, gu_grad = _gelu_and_grad(u)
        vn, xhat, rs, gv_grad = _sgu_normed(v, lng_ref, lnb_ref)
        vnb = vn.astype(BF16)
        dyv = dy_ref[...].astype(F32)
        for blk in range(tb // SGU_BLOCK):
            rows = slice(blk * SGU_BLOCK, (blk + 1) * SGU_BLOCK)
            for g in range(SGU_GROUPS):
                cols = slice(g * LANES, (g + 1) * LANES)
                vt = vnb[rows, cols]
                mixed = _dot(wm_ref[g], vt) + bias_ref[g]
                dyt = dyv[rows, cols]
                du_ref[rows, cols] = (dyt * mixed * gu_grad[rows, cols]).astype(BF16)
                dmix = dyt * gu[rows, cols]
                dmixb = dmix.astype(BF16)
                dvn_scr[rows, cols] = _dot(wmt_ref[g], dmixb)
                dws_ref[g] += _dot_nt(dmixb, vt) * mask_ref[...]
                dbs_acc[g] += dmix
        dvn = dvn_scr[...]
        dlng_ref[...] += jnp.sum(dvn * xhat, axis=0, keepdims=True)
        dlnb_ref[...] += jnp.sum(dvn, axis=0, keepdims=True)
        dxh = dvn * lng_ref[...]
        dgv = rs * (
            dxh - jnp.mean(dxh, axis=-1, keepdims=True) - xhat * jnp.mean(dxh * xhat, axis=-1, keepdims=True)
        )
        dv_ref[...] = (dgv * gv_grad).astype(BF16)

        @pl.when(n == nb - 1)
        def _():
            for g in range(SGU_GROUPS):
                dbs_ref[g] = jnp.broadcast_to(jnp.sum(dbs_acc[g], axis=-1, keepdims=True), (SGU_BLOCK, LANES))

    const2 = lambda n: (0, 0)
    const3 = lambda n: (0, 0, 0)
    gmat = pl.BlockSpec((SGU_GROUPS, SGU_BLOCK, SGU_BLOCK), const3)
    vec = pl.BlockSpec((1, D_SGU), const2)
    act = pl.BlockSpec((tb, D_SGU), lambda n: (n, 0))
    return _call(
        body,
        (dy, proj, proj, proj, proj, lng, lnb, wm, wmt, bias, mask),
        name=name,
        grid=(nb,),
        in_specs=[act] + _sgu_specs(tb) + [vec, vec, gmat, gmat, gmat, pl.BlockSpec((SGU_BLOCK, SGU_BLOCK), const2)],
        out_specs=[act, act, gmat, gmat, vec, vec],
        out_shape=[
            SDS((s, D_SGU), BF16),
            SDS((s, D_SGU), BF16),
            SDS((SGU_GROUPS, SGU_BLOCK, SGU_BLOCK), F32),
            SDS((SGU_GROUPS, SGU_BLOCK, LANES), F32),
            SDS((1, D_SGU), F32),
            SDS((1, D_SGU), F32),
        ],
        scratch_shapes=[pltpu.VMEM((tb, D_SGU), F32), pltpu.VMEM((SGU_GROUPS, SGU_BLOCK, LANES), F32)],
        semantics=("arbitrary",),
        comm=comm,
    )


def _gate_specs(tm):
    half = lambda blk: pl.BlockSpec((tm, 512), lambda i: (i, blk))
    return [half(GA_BLK512), half(GA_BLK512 + 1), half(GB_BLK512), half(GB_BLK512 + 1)]


def _merge_fwd(ya_pre, yb_pre, proj, x, w_ba, w_bb, w_out, *, tm, name, comm=None):
    s = x.shape[0]
    tm = min(tm, s)

    def body(ya_ref, yb_ref, a0, a1, b0, b1, x_ref, wa_ref, wb_ref, wo_ref, x1_ref, yao_ref, ybo_ref):
        ya = _dot(ya_ref[...], wa_ref[...])
        yb = _dot(yb_ref[...], wb_ref[...])
        sa = _sigmoid(jnp.concatenate([a0[...], a1[...]], axis=1).astype(F32))
        sb = _sigmoid(jnp.concatenate([b0[...], b1[...]], axis=1).astype(F32))
        merged = sa * ya + sb * yb
        x1_ref[...] = x_ref[...] + _dot(merged.astype(BF16), wo_ref[...])
        yao_ref[...] = ya.astype(BF16)
        ybo_ref[...] = yb.astype(BF16)

    whole = lambda r: pl.BlockSpec((r, D), lambda i: (0, 0))
    act = pl.BlockSpec((tm, D), lambda i: (i, 0))
    return _call(
        body,
        (ya_pre, yb_pre, proj, proj, proj, proj, x, w_ba, w_bb, w_out),
        name=name,
        grid=(s // tm,),
        in_specs=[pl.BlockSpec((tm, D_RNN), lambda i: (i, 0)), act] + _gate_specs(tm) + [act, whole(D_RNN), whole(D_SGU), whole(D)],
        out_specs=[act, act, act],
        out_shape=[SDS((s, D), F32), SDS((s, D), BF16), SDS((s, D), BF16)],
        semantics=("parallel",),
        comm=comm,
    )


def _merge_bwd(dx1, ya, yb, proj, w_ba, w_bb, w_out, *, tm, name, comm=None):
    s = dx1.shape[0]
    tm = min(tm, s)

    def body(dx_ref, ya_ref, yb_ref, a0, a1, b0, b1, wa_ref, wb_ref, wo_ref,
             mg_ref, dya_ref, dyb_ref, dga_ref, dgb_ref, dyap_ref, dybp_ref):
        dm = _dot_nt(dx_ref[...], wo_ref[...])
        ya = ya_ref[...].astype(F32)
        yb = yb_ref[...].astype(F32)
        sa = _sigmoid(jnp.concatenate([a0[...], a1[...]], axis=1).astype(F32))
        sb = _sigmoid(jnp.concatenate([b0[...], b1[...]], axis=1).astype(F32))
        mg_ref[...] = (sa * ya + sb * yb).astype(BF16)
        dya = (dm * sa).astype(BF16)
        dyb = (dm * sb).astype(BF16)
        dya_ref[...] = dya
        dyb_ref[...] = dyb
        dga_ref[...] = (dm * ya * sa * (1.0 - sa)).astype(BF16)
        dgb_ref[...] = (dm * yb * sb * (1.0 - sb)).astype(BF16)
        dyap_ref[...] = _dot_nt(dya, wa_ref[...]).astype(BF16)
        dybp_ref[...] = _dot_nt(dyb, wb_ref[...]).astype(BF16)

    whole = lambda r: pl.BlockSpec((r, D), lambda i: (0, 0))
    act = pl.BlockSpec((tm, D), lambda i: (i, 0))
    act_rnn = pl.BlockSpec((tm, D_RNN), lambda i: (i, 0))
    return _call(
        body,
        (dx1, ya, yb, proj, proj, proj, proj, w_ba, w_bb, w_out),
        name=name,
        grid=(s // tm,),
        in_specs=[act, act, act] + _gate_specs(tm) + [whole(D_RNN), whole(D_SGU), whole(D)],
        out_specs=[act, act, act, act, act, act_rnn, act],
        out_shape=[SDS((s, D), BF16)] * 5 + [SDS((s, D_RNN), BF16), SDS((s, D_SGU), BF16)],
        semantics=("parallel",),
        comm=comm,
    )


def _final_loss(x, g, target, *, tm, name):
    s = x.shape[0]
    tm = min(tm, s)

    def body(x_ref, g_ref, t_ref, dx_ref, dxb_ref, dg_ref, loss_ref):
        @pl.when(pl.program_id(0) == 0)
        def _():
            dg_ref[...] = jnp.zeros_like(dg_ref)
            loss_ref[...] = jnp.zeros_like(loss_ref)

        xv = x_ref[...]
        r = lax.rsqrt(jnp.mean(xv * xv, axis=-1, keepdims=True) + EPS)
        xhat = xv * r
        e = xhat * g_ref[...] - t_ref[...]
        loss_ref[...] += 0.5 * jnp.sum(jnp.mean(e * e, axis=-1, keepdims=True), axis=0, keepdims=True)
        dy = e * (1.0 / D)
        dxh = dy * g_ref[...]
        dx = r * (dxh - xhat * jnp.mean(dxh * xhat, axis=-1, keepdims=True))
        dx_ref[...] = dx
        dxb_ref[...] = dx.astype(BF16)
        dg_ref[...] += jnp.sum(dy * xhat, axis=0, keepdims=True)

    act = pl.BlockSpec((tm, D), lambda i: (i, 0))
    vec = pl.BlockSpec((1, D), lambda i: (0, 0))
    return pl.pallas_call(
        body,
        name=name,
        grid=(s // tm,),
        in_specs=[act, vec, act],
        out_specs=[act, act, vec, pl.BlockSpec((SUBLANES, LANES), lambda i: (0, 0))],
        out_shape=[SDS((s, D), F32), SDS((s, D), BF16), SDS((1, D), F32), SDS((SUBLANES, LANES), F32)],
        compiler_params=_params("arbitrary"),
    )(x, g, target)


def _adamw_math(w, g, m, v):
    m2 = ADAM_B1 * m + (1.0 - ADAM_B1) * g
    v2 = ADAM_B2 * v + (1.0 - ADAM_B2) * (g * g)
    m_hat = m2 / (1.0 - ADAM_B1**ADAM_STEP)
    v_hat = v2 / (1.0 - ADAM_B2**ADAM_STEP)
    delta = -ADAM_LR * (m_hat / (jnp.sqrt(v_hat) + ADAM_EPS) + ADAM_WD * w)
    return delta, m2, v2


def _row_tile(rows, cap):
    return max(t for t in range(SUBLANES, min(cap, rows) + 1, SUBLANES) if rows % t == 0)


def _adamw_layers(w, grads, m, v, *, tr, name):
    depth, r, c = w.shape
    tr = _row_tile(r, tr)

    def body(*refs):
        g_refs = refs[:depth]
        w_ref, m_ref, v_ref, g_out, d_ref, mo_ref, vo_ref = refs[depth:]
        for l in range(depth):

            @pl.when(pl.program_id(0) == l)
            def _(l=l):
                g = g_refs[l][...]
                g_out[...] = g
                d_ref[...], mo_ref[...], vo_ref[...] = _adamw_math(w_ref[...], g, m_ref[...], v_ref[...])

    def of_layer(ll):
        return pl.BlockSpec((tr, c), lambda l, i: (jnp.where(l == ll, i, 0), 0))

    stacked = pl.BlockSpec((None, tr, c), lambda l, i: (l, i, 0))
    return pl.pallas_call(
        body,
        name=name,
        grid=(depth, r // tr),
        in_specs=[of_layer(ll) for ll in range(depth)] + [stacked] * 3,
        out_specs=[stacked] * 4,
        out_shape=[SDS((depth, r, c), F32)] * 4,
        compiler_params=_params("parallel", "parallel"),
    )(*grads, w, m, v)


def _adamw_reduced(w, parts, from_chips, m, v, chip, *, tr, name):
    depth, r, _ = w.shape
    tr = _row_tile(r, tr)

    def body(chip_ref, *refs):
        p_refs, c_refs = refs[:depth], refs[depth : 2 * depth]
        w_ref, m_ref, v_ref, g_out, d_ref, mo_ref, vo_ref = refs[2 * depth :]
        for l in range(depth):

            @pl.when(pl.program_id(0) == l)
            def _(l=l):
                got = c_refs[l]
                g = ((p_refs[l][...].astype(F32) + got[0].astype(F32)) + got[1].astype(F32)) + got[2].astype(F32)
                g_out[...] = g
                d_ref[...], mo_ref[...], vo_ref[...] = _adamw_math(w_ref[...], g, m_ref[...], v_ref[...])

    def mine_of_layer(ll):
        return pl.BlockSpec((None, tr, D), lambda l, i, chip_ref: (chip_ref[0], jnp.where(l == ll, i, 0), 0))

    def theirs_of_layer(ll):
        return pl.BlockSpec((3, tr, D), lambda l, i, chip_ref: (0, jnp.where(l == ll, i, 0), 0))

    stacked = pl.BlockSpec((None, tr, D), lambda l, i, chip_ref: (l, i, 0))
    return pl.pallas_call(
        body,
        name=name,
        grid_spec=pltpu.PrefetchScalarGridSpec(
            num_scalar_prefetch=1,
            grid=(depth, r // tr),
            in_specs=[mine_of_layer(ll) for ll in range(depth)]
            + [theirs_of_layer(ll) for ll in range(depth)]
            + [stacked] * 3,
            out_specs=[stacked] * 4,
        ),
        out_shape=[SDS((depth, r, D), F32)] * 4,
        compiler_params=_params("parallel", "parallel"),
    )(chip, *parts, *from_chips, w, m, v)


def _adamw_small(groups, *, name):
    n = len(groups)

    def body(*refs):
        ins, outs = refs[: 4 * n], refs[4 * n :]
        for i in range(n):
            w, g, m, v = (ref[...] for ref in ins[4 * i : 4 * i + 4])
            outs[3 * i][...], outs[3 * i + 1][...], outs[3 * i + 2][...] = _adamw_math(w, g, m, v)

    vmem = pl.BlockSpec(memory_space=pltpu.VMEM)
    outs = pl.pallas_call(
        body,
        name=name,
        in_specs=[vmem] * (4 * n),
        out_specs=[vmem] * (3 * n),
        out_shape=[SDS(grp[0].shape, F32) for grp in groups for _ in range(3)],
        compiler_params=pltpu.CompilerParams(vmem_limit_bytes=VMEM_LIMIT_BYTES),
    )(*[a for grp in groups for a in grp])
    return [tuple(outs[3 * i : 3 * i + 3]) for i in range(n)]


ANY = pl.BlockSpec(memory_space=pl.ANY)


def _position():
    return lax.axis_index("x"), lax.axis_index("y"), lax.axis_index("c")


def _other_chips(x, y):
    return [(1 - x, y), (x, 1 - y), (1 - x, 1 - y)]


class _Comm:
    def __init__(self, inputs, out_shapes, sem_counts, start, middle, finish, middle_at=1.0):
        self.inputs, self.out_shapes, self.sem_counts = list(inputs), list(out_shapes), list(sem_counts)
        self.start, self.middle, self.finish = start, middle, finish
        self.middle_at = middle_at

    def sem_shapes(self):
        return [pltpu.SemaphoreType.DMA((n,)) for n in self.sem_counts]


def _merge_comms(comms):
    bounds, i, o, s = [], 0, 0, 0
    for cm in comms:
        bounds.append((i, i + len(cm.inputs), o, o + len(cm.out_shapes), s, s + len(cm.sem_counts)))
        i, o, s = bounds[-1][1], bounds[-1][3], bounds[-1][5]

    def phase(which):
        def run(ins, outs, sems):
            for cm, (i0, i1, o0, o1, s0, s1) in zip(comms, bounds):
                getattr(cm, which)(ins[i0:i1], outs[o0:o1], sems[s0:s1])

        return run

    return _Comm(
        [a for cm in comms for a in cm.inputs],
        [a for cm in comms for a in cm.out_shapes],
        [a for cm in comms for a in cm.sem_counts],
        phase("start"),
        phase("middle"),
        phase("finish"),
        middle_at=max(cm.middle_at for cm in comms),
    )


def _call(body, args, *, semantics, comm=None, **kw):
    if comm is None:
        return pl.pallas_call(body, compiler_params=_params(*semantics), **kw)(*args)
    grid, in_specs, out_specs, out_shape = kw["grid"], kw["in_specs"], kw["out_specs"], kw["out_shape"]
    scratch = list(kw.get("scratch_shapes", ()))
    single = not isinstance(out_shape, (list, tuple))
    core_specs = [out_specs] if single else list(out_specs)
    core_shapes = [out_shape] if single else list(out_shape)
    n_in, n_out, n_scr = len(in_specs), len(core_shapes), len(scratch)
    n_cin, n_cout = len(comm.inputs), len(comm.out_shapes)
    steps = 1
    for g in grid:
        steps *= g
    middle = min(int(comm.middle_at * steps), steps - 1)

    def hosted(*refs):
        core_in, c_in = refs[:n_in], refs[n_in : n_in + n_cin]
        o0 = n_in + n_cin
        core_out, c_out = refs[o0 : o0 + n_out], refs[o0 + n_out : o0 + n_out + n_cout]
        s0 = o0 + n_out + n_cout
        core_scr, sems = refs[s0 : s0 + n_scr], refs[s0 + n_scr :]
        step = pl.program_id(0)
        for d in range(1, len(grid)):
            step = step * grid[d] + pl.program_id(d)

        @pl.when(step == 0)
        def _():
            comm.start(c_in, c_out, sems)

        body(*core_in, *core_out, *core_scr)

        @pl.when(step == middle)
        def _():
            comm.middle(c_in, c_out, sems)

        @pl.when(step == steps - 1)
        def _():
            comm.finish(c_in, c_out, sems)

    outs = pl.pallas_call(
        hosted,
        name=kw["name"],
        grid=grid,
        in_specs=list(in_specs) + [ANY] * n_cin,
        out_specs=core_specs + [ANY] * n_cout,
        out_shape=core_shapes + comm.out_shapes,
        scratch_shapes=scratch + comm.sem_shapes(),
        compiler_params=_params(*(["arbitrary"] * len(grid))),
    )(*args, *comm.inputs)
    return (outs[0] if single else outs[:n_out]), outs[n_out:]


def _comm_only(comm, *, name):
    n_cin, n_cout = len(comm.inputs), len(comm.out_shapes)

    def body(*refs):
        ins, outs, sems = refs[:n_cin], refs[n_cin : n_cin + n_cout], refs[n_cin + n_cout :]
        comm.start(ins, outs, sems)
        comm.middle(ins, outs, sems)
        comm.finish(ins, outs, sems)

    return pl.pallas_call(
        body,
        name=name,
        in_specs=[ANY] * n_cin,
        out_specs=[ANY] * n_cout,
        out_shape=comm.out_shapes,
        scratch_shapes=comm.sem_shapes(),
    )(*comm.inputs)


def _gather_comm(shards, pass_on_at=1.0):
    n = len(shards)
    per = 7

    def plan(ins, outs, sems):
        send, recv, local = sems
        x, y, c = _position()
        me, sibling = (x, y, c), (x, y, 1 - c)
        chips = _other_chips(x, y)

        def block(t, px, py, pc):
            return outs[t].at[pl.ds(4 * px + 2 * py + pc, 1)]

        def copy(t, k, blk, to, src=None):
            return pltpu.make_async_remote_copy(
                src_ref=block(t, *blk) if src is None else src,
                dst_ref=block(t, *blk),
                send_sem=send.at[t * per + k],
                recv_sem=recv.at[t * per + k],
                device_id=to,
                device_id_type=MESH,
            )

        mine = [pltpu.make_async_copy(ins[t], block(t, *me), local.at[t]) for t in range(n)]
        to_chips = [copy(t, 1 + j, me, (*chip, c), src=ins[t]) for t in range(n) for j, chip in enumerate(chips)]
        to_sibling = [copy(t, 0, me, sibling, src=ins[t]) for t in range(n)]
        from_chips = [copy(t, 1 + j, (*chip, c), me) for t in range(n) for j, chip in enumerate(chips)]
        passed_on = [copy(t, 4 + j, (*chip, c), sibling) for t in range(n) for j, chip in enumerate(chips)]
        from_sibling = [copy(t, 0, sibling, me) for t in range(n)]
        from_sibling += [copy(t, 4 + j, (*chip, 1 - c), me) for t in range(n) for j, chip in enumerate(chips)]
        return mine, to_chips, to_sibling, from_chips, passed_on, from_sibling

    def start(ins, outs, sems):
        mine, to_chips, to_sibling, _, _, _ = plan(ins, outs, sems)
        for cp in mine + to_chips + to_sibling:
            cp.start()

    def middle(ins, outs, sems):
        _, _, _, from_chips, passed_on, _ = plan(ins, outs, sems)
        for arrived, onward in zip(from_chips, passed_on):
            arrived.wait_recv()
            onward.start()

    def finish(ins, outs, sems):
        mine, to_chips, to_sibling, _, passed_on, from_sibling = plan(ins, outs, sems)
        for cp in from_sibling:
            cp.wait_recv()
        for cp in to_chips + to_sibling + passed_on:
            cp.wait_send()
        for cp in mine:
            cp.wait()

    out_shapes = [SDS((N_DEV,) + sh.shape[1:], sh.dtype) for sh in shards]
    return _Comm(shards, out_shapes, [n * per, n * per, n], start, middle, finish, middle_at=pass_on_at)


def _exchange_comm(arrays, out_shapes, n_copies, copies_of):
    def start(ins, outs, sems):
        for cp in copies_of(ins, outs, *sems):
            cp.start()

    def middle(ins, outs, sems):
        pass

    def finish(ins, outs, sems):
        for cp in copies_of(ins, outs, *sems):
            cp.wait()

    return _Comm(arrays, out_shapes, [n_copies, n_copies], start, middle, finish)


def _sibling_comm(grads):
    def copies_of(ins, outs, send, recv):
        x, y, c = _position()
        return [
            pltpu.make_async_remote_copy(
                src_ref=ins[t].at[:, pl.ds(1 - c, 1)],
                dst_ref=outs[t],
                send_sem=send.at[t],
                recv_sem=recv.at[t],
                device_id=(x, y, 1 - c),
                device_id_type=MESH,
            )
            for t in range(len(ins))
        ]

    return _exchange_comm(grads, [SDS((4, 1) + g.shape[2:], g.dtype) for g in grads], len(grads), copies_of)


def _chips_comm(parts):
    def copies_of(ins, outs, send, recv):
        x, y, c = _position()
        return [
            pltpu.make_async_remote_copy(
                src_ref=ins[t].at[pl.ds(2 * px + py, 1)],
                dst_ref=outs[t].at[pl.ds(k, 1)],
                send_sem=send.at[3 * t + k],
                recv_sem=recv.at[3 * t + k],
                device_id=(px, py, c),
                device_id_type=MESH,
            )
            for t in range(len(ins))
            for k, (px, py) in enumerate(_other_chips(x, y))
        ]

    return _exchange_comm(parts, [SDS((3,) + p.shape[1:], p.dtype) for p in parts], 3 * len(parts), copies_of)


def _sum_with_sibling(grad, got, core, *, name):
    rows = grad.shape[2]

    def body(core_ref, a_ref, b_ref, o_ref):
        o_ref[...] = (a_ref[...].astype(F32) + b_ref[...].astype(F32)).astype(o_ref.dtype)

    return pl.pallas_call(
        body,
        name=name,
        grid_spec=pltpu.PrefetchScalarGridSpec(
            num_scalar_prefetch=1,
            grid=(4,),
            in_specs=[
                pl.BlockSpec((None, None, rows, D), lambda q, core_ref: (q, core_ref[0], 0, 0)),
                pl.BlockSpec((None, None, rows, D), lambda q, core_ref: (q, 0, 0, 0)),
            ],
            out_specs=pl.BlockSpec((None, rows, D), lambda q, core_ref: (q, 0, 0)),
        ),
        out_shape=SDS((4, rows, D), grad.dtype),
        compiler_params=_params("parallel"),
    )(core, grad, got)


def _sum_chips(part, got, chip, *, name):
    rows = part.shape[1]

    def body(chip_ref, a_ref, b_ref, o_ref):
        o_ref[...] = ((a_ref[...].astype(F32) + b_ref[0].astype(F32)) + b_ref[1].astype(F32)) + b_ref[2].astype(F32)

    return pl.pallas_call(
        body,
        name=name,
        grid_spec=pltpu.PrefetchScalarGridSpec(
            num_scalar_prefetch=1,
            grid=(1,),
            in_specs=[
                pl.BlockSpec((None, rows, D), lambda i, chip_ref: (chip_ref[0], 0, 0)),
                pl.BlockSpec((3, rows, D), lambda i, chip_ref: (0, 0, 0)),
            ],
            out_specs=pl.BlockSpec((rows, D), lambda i, chip_ref: (0, 0)),
        ),
        out_shape=SDS((rows, D), F32),
        compiler_params=_params("arbitrary"),
    )(chip, part, got)


def _all_reduce_small(pack, *, name):
    rows = pack.shape[1]
    relations = [(kx, ky, kc) for kx in (0, 1) for ky in (0, 1) for kc in (0, 1)][1:]

    def body(in_ref, out_ref, landed, send1, recv1, send2, recv2):
        x, y, c = _position()
        mine = 4 * x + 2 * y + c

        def peer(rel):
            kx, ky, kc = rel
            return (1 - x if kx else x, 1 - y if ky else y, 1 - c if kc else c)

        first = []
        for k, rel in enumerate(relations):
            px, py, pc = peer(rel)
            cp = pltpu.make_async_remote_copy(
                src_ref=in_ref.at[4 * px + 2 * py + pc],
                dst_ref=landed.at[k],
                send_sem=send1.at[k],
                recv_sem=recv1.at[k],
                device_id=(px, py, pc),
                device_id_type=MESH,
            )
            cp.start()
            first.append(cp)
        total = in_ref[mine]
        for k, cp in enumerate(first):
            cp.wait_recv()
            total = total + landed[k]
        out_ref[mine] = total
        second = []
        for k, rel in enumerate(relations):
            cp = pltpu.make_async_remote_copy(
                src_ref=out_ref.at[mine],
                dst_ref=out_ref.at[mine],
                send_sem=send2.at[k],
                recv_sem=recv2.at[k],
                device_id=peer(rel),
                device_id_type=MESH,
            )
            cp.start()
            second.append(cp)
        for k, rel in enumerate(relations):
            px, py, pc = peer(rel)
            got = out_ref.at[4 * px + 2 * py + pc]
            pltpu.make_async_remote_copy(
                src_ref=got, dst_ref=got, send_sem=send2.at[k], recv_sem=recv2.at[k], device_id=peer(rel), device_id_type=MESH
            ).wait_recv()
        for cp in first + second:
            cp.wait_send()

    vmem = pl.BlockSpec(memory_space=pltpu.VMEM)
    return pl.pallas_call(
        body,
        name=name,
        in_specs=[vmem],
        out_specs=vmem,
        out_shape=SDS(pack.shape, F32),
        scratch_shapes=[
            pltpu.VMEM((7, rows, D), F32),
            pltpu.SemaphoreType.DMA((7,)),
            pltpu.SemaphoreType.DMA((7,)),
            pltpu.SemaphoreType.DMA((7,)),
            pltpu.SemaphoreType.DMA((7,)),
        ],
        compiler_params=pltpu.CompilerParams(vmem_limit_bytes=VMEM_LIMIT_BYTES),
    )(pack)


def _pack(arrays, rows):
    flat = jnp.concatenate([a.reshape(-1).astype(F32) for a in arrays])
    return jnp.pad(flat, (0, rows * D - flat.shape[0])).reshape(rows, D)


def _unpack(pack, shapes):
    flat = pack.reshape(-1)
    out, off = [], 0
    for sh in shapes:
        size = 1
        for dim in sh:
            size *= dim
        out.append(flat[off : off + size].reshape(sh))
        off += size
    return out


def _block_diag_pairs(w):
    w = w.reshape(N_RNN_TILES, 2, HEAD_DIM, HEAD_DIM)
    z = jnp.zeros_like(w[:, 0])
    top = jnp.concatenate([w[:, 0], z], axis=2)
    bot = jnp.concatenate([z, w[:, 1]], axis=2)
    return jnp.concatenate([top, bot], axis=1)


def _diag_blocks(w2):
    a = w2[:, :HEAD_DIM, :HEAD_DIM]
    b = w2[:, HEAD_DIM:, HEAD_DIM:]
    return jnp.stack([a, b], axis=1).reshape(RNN_HEADS, HEAD_DIM, HEAD_DIM)


BIG = ("w_in", "w_branch_a", "w_branch_b", "w_out", "w_up", "w_down")
TRANSPOSED = ("w_in", "w_up")
SMALL = (
    "norm_mix_g", "conv_w", "conv_b", "lru_w_a", "lru_b_a", "lru_w_x", "lru_b_x", "lru_lambda",
    "sgu_ln_g", "sgu_ln_b", "sgu_w_s", "sgu_b_s", "norm_ffn_g", "final_norm_g",
)
WEIGHTS = (
    "norm_mix_g", "w_in", "conv_w", "conv_b", "lru_w_a", "lru_b_a", "lru_w_x", "lru_b_x", "lru_lambda", "sgu_ln_g",
    "sgu_ln_b", "sgu_w_s", "sgu_b_s", "w_branch_a", "w_branch_b", "w_out", "norm_ffn_g", "w_up", "w_down", "final_norm_g",
)

TM = 512
TM_NT = 1024
TN_IN = 1664
TN_UP = 2048
TKA = 512
TKA_PIECES = 256
TC = 512
TB = 256
TR = 256


GATHERS_RIDING = (
    {
        "in_proj": (1.0, [(1, "w_in")]),
        "branch_a_fwd": (1.0, [(0, "w_branch_a"), (0, "w_branch_b"), (0, "w_out"), (1, "w_branch_a"), (1, "w_branch_b"), (1, "w_out")]),
        "merge_fwd": (1.0, [(0, "w_up")]),
        "ffn_up": (1.0, [(0, "w_down")]),
        "ffn_down": (1.0, [(1, "w_up")]),
    },
    {"in_proj": (0.7, [(1, "w_down")])},
)


def _layer_forward(l, x, p, w, shards):
    def run(key, fn, *args, **kw):
        if key not in GATHERS_RIDING[l]:
            return fn(*args, **kw)
        pass_on_at, riding = GATHERS_RIDING[l][key]
        out, got = fn(*args, comm=_gather_comm([shards[l2][n2] for l2, n2 in riding], pass_on_at), **kw)
        for (l2, n2), full in zip(riding, got):
            w[l2][n2] = full.reshape(-1, D)
        return out

    proj, h = run("in_proj", _norm_matmul_nt, x, p["norm_mix_g"], w[l]["w_in"], tm=TM_NT, tn=TN_IN, name=f"in_proj_{l}")
    hseq, ya_pre = run(
        "branch_a_fwd", _branch_a_fwd, proj, p["conv_w"], p["conv_b"], p["wa2"], p["lru_b_a"], p["wx2"], p["lru_b_x"],
        p["lru_lambda"], tc=TC, name=f"branch_a_fwd_{l}",
    )
    yb_pre = _sgu_fwd(proj, p["sgu_ln_g"], p["sgu_ln_b"], p["wm"], p["sgu_bias"], tb=TB, name=f"sgu_fwd_{l}")
    x1, ya, yb = run(
        "merge_fwd", _merge_fwd, ya_pre, yb_pre, proj, x, w[l]["w_branch_a"], w[l]["w_branch_b"], w[l]["w_out"], tm=TM,
        name=f"merge_fwd_{l}",
    )
    f_pre, h2 = run("ffn_up", _norm_matmul_nt, x1, p["norm_ffn_g"], w[l]["w_up"], tm=TM_NT, tn=TN_UP, name=f"ffn_up_{l}")
    x2 = run("ffn_down", _matmul_nn_res, f_pre, w[l]["w_down"], x1, relu2=True, tm=TM, name=f"ffn_down_{l}")
    saved = dict(x=x, h=h, proj=proj, hseq=hseq, ya_pre=ya_pre, yb_pre=yb_pre, ya=ya, yb=yb, x1=x1, h2=h2, f_pre=f_pre)
    return x2, saved


def _layer_backward(l, dx2, dx2b, sv, p, w, core, waiting, last):
    parts, from_chips = {}, {}

    def by_device(g):
        return g.reshape(4, 2, -1, D)

    def with_sibling(name, g, got):
        parts[name] = _sum_with_sibling(by_device(g), got, core, name=f"sum_sibling_{name}_{l}")

    df_pre = _matmul_nt_drelu2(dx2b, w["w_down"], sv["f_pre"], tm=TM_NT, tn=TN_UP, name=f"ffn_down_bwd_{l}")
    g_down = _matmul_tn([sv["f_pre"]], dx2b, relu2=True, tka=TKA, name=f"grad_w_down_{l}")
    g_up, (got,) = _matmul_tn(
        [df_pre], sv["h2"], relu2=False, tka=TKA, name=f"grad_w_up_{l}", comm=_sibling_comm([by_device(g_down)])
    )
    with_sibling("w_down", g_down, got)
    (dx1, dx1b, g_norm_ffn), (got, from_chips[l, "w_down"]) = _matmul_nn_rmsnorm_bwd(
        [df_pre], w["w_up"], sv["x1"], p["norm_ffn_g"], dx2, tm=TM, name=f"ffn_up_bwd_{l}",
        comm=_merge_comms([_sibling_comm([by_device(g_up)]), _chips_comm([parts["w_down"]])]),
    )
    with_sibling("w_up", g_up, got)
    (merged, dya, dyb, dga, dgb, dya_pre, dyb_pre), (from_chips[l, "w_up"],) = _merge_bwd(
        dx1b, sv["ya"], sv["yb"], sv["proj"], w["w_branch_a"], w["w_branch_b"], w["w_out"], tm=TM, name=f"merge_bwd_{l}",
        comm=_chips_comm([parts["w_up"]]),
    )
    g_out = _matmul_tn([merged], dx1b, relu2=False, tka=TKA, name=f"grad_w_out_{l}")
    g_ba = _matmul_tn([sv["ya_pre"]], dya, relu2=False, tka=TKA_PIECES, name=f"grad_w_branch_a_{l}")
    g_bb = _matmul_tn([sv["yb_pre"]], dyb, relu2=False, tka=TKA, name=f"grad_w_branch_b_{l}")
    branch = (("w_out", g_out), ("w_branch_a", g_ba), ("w_branch_b", g_bb))
    (du, dv, g_ws, g_bs, g_lng, g_lnb), got = _sgu_bwd(
        dyb_pre, sv["proj"], p["sgu_ln_g"], p["sgu_ln_b"], p["wm"], p["wmt"], p["sgu_bias"], p["mask"], tb=TB,
        name=f"sgu_bwd_{l}", comm=_sibling_comm([by_device(g) for _, g in branch]),
    )
    for (name, g), landed in zip(branch, got):
        with_sibling(name, g, landed)
    riding = [((l, name), parts[name]) for name, _ in branch] + list(waiting)
    (dxr, dgr, g_cw, g_cb, g_ba_, g_bx, g_lam, g_wa2, g_wx2), got = _branch_a_bwd(
        dya_pre, sv["proj"], sv["hseq"], p["conv_w"], p["conv_b"], p["wa2"], p["lru_b_a"], p["wx2"], p["lru_b_x"],
        p["lru_lambda"], p["wa2t"], p["wx2t"], tc=TC, name=f"branch_a_bwd_{l}", comm=_chips_comm([part for _, part in riding]),
    )
    for (key, _), landed in zip(riding, got):
        from_chips[key] = landed
    dproj = [dxr, dgr, du, dv, dga, dgb]
    g_in = _matmul_tn(dproj, sv["h"], relu2=False, tka=TKA_PIECES, name=f"grad_w_in_{l}")
    if last:
        (got,) = _comm_only(_sibling_comm([by_device(g_in)]), name=f"grad_w_in_to_sibling_{l}")
        with_sibling("w_in", g_in, got)
        riding = _chips_comm([parts["w_in"]])
    else:
        riding = _sibling_comm([by_device(g_in)])
    (dx, dxb, g_norm_mix), (got,) = _matmul_nn_rmsnorm_bwd(
        dproj, w["w_in"], sv["x"], p["norm_mix_g"], dx1, tm=TM, name=f"in_proj_bwd_{l}", comm=riding
    )
    if last:
        from_chips[l, "w_in"] = got
    else:
        with_sibling("w_in", g_in, got)
    small = dict(
        norm_mix_g=g_norm_mix[0], conv_w=g_cw, conv_b=g_cb[0], lru_w_a=_diag_blocks(g_wa2), lru_b_a=g_ba_.reshape(RNN_HEADS, HEAD_DIM),
        lru_w_x=_diag_blocks(g_wx2), lru_b_x=g_bx.reshape(RNN_HEADS, HEAD_DIM), lru_lambda=g_lam[0], sgu_ln_g=g_lng[0],
        sgu_ln_b=g_lnb[0], sgu_w_s=g_ws, sgu_b_s=g_bs[:, :, 0], norm_ffn_g=g_norm_ffn[0],
    )
    return dx, dxb, small, parts, from_chips


def _prepare_small(l, given):
    chunk_id = jnp.arange(SGU_BLOCK) // CHUNK
    mask = (chunk_id[:, None] >= chunk_id[None, :]).astype(F32)
    wm = given["sgu_w_s"][l] * mask
    wa2 = _block_diag_pairs(given["lru_w_a"][l])
    wx2 = _block_diag_pairs(given["lru_w_x"][l])
    row = lambda a: a.reshape(1, -1)
    return dict(
        norm_mix_g=row(given["norm_mix_g"][l]),
        norm_ffn_g=row(given["norm_ffn_g"][l]),
        conv_w=given["conv_w_full"][l],
        conv_b=row(given["conv_b"][l]),
        wa2=wa2.astype(BF16),
        wx2=wx2.astype(BF16),
        wa2t=jnp.swapaxes(wa2, 1, 2).astype(BF16),
        wx2t=jnp.swapaxes(wx2, 1, 2).astype(BF16),
        lru_b_a=row(given["lru_b_a"][l]),
        lru_b_x=row(given["lru_b_x"][l]),
        lru_lambda=row(given["lru_lambda"][l]),
        sgu_ln_g=row(given["sgu_ln_g"][l]),
        sgu_ln_b=row(given["sgu_ln_b"][l]),
        wm=wm.astype(BF16),
        wmt=jnp.swapaxes(wm, 1, 2).astype(BF16),
        sgu_bias=jnp.broadcast_to(given["sgu_b_s"][l][:, :, None], (SGU_GROUPS, SGU_BLOCK, LANES)),
        mask=mask,
    )


def _step(given):
    x_idx, y_idx, c_idx = _position()
    dev = 4 * x_idx + 2 * y_idx + c_idx
    core = c_idx.astype(jnp.int32).reshape(1)
    chip = (2 * x_idx + y_idx).astype(jnp.int32).reshape(1)

    def rows_first(name, a):
        return jnp.swapaxes(a, 1, 2) if name in TRANSPOSED else a

    shards = []
    for l in range(DEPTH):
        shards.append({name: rows_first(name, given[name])[l].astype(BF16)[None] for name in BIG})
    conv_mine = given["conv_w"].reshape(1, DEPTH * CONV_WIDTH, D_RNN // N_DEV)
    w_in_first, conv_all = _comm_only(_gather_comm([shards[0]["w_in"], conv_mine]), name="gather_first")
    weights = [{"w_in": w_in_first.reshape(-1, D)}, {}]
    conv_all = conv_all.reshape(N_DEV, DEPTH, CONV_WIDTH, D_RNN // N_DEV)
    given = dict(given, conv_w_full=jnp.moveaxis(conv_all, 0, 2).reshape(DEPTH, CONV_WIDTH, D_RNN))

    small_params = [_prepare_small(l, given) for l in range(DEPTH)]
    x = given["x"][0]
    saved = []
    for l in range(DEPTH):
        x, sv = _layer_forward(l, x, small_params[l], weights, shards)
        saved.append(sv)
    dx, dxb, g_final, loss = _final_loss(x, given["final_norm_g"].reshape(1, D), given["loss_target"][0], tm=TM, name="final_loss")
    small_grads, parts, from_chips, waiting = [None] * DEPTH, [None] * DEPTH, {}, []
    for l in reversed(range(DEPTH)):
        dx, dxb, small_grads[l], parts[l], got = _layer_backward(
            l, dx, dxb, saved[l], small_params[l], weights[l], core, waiting, last=l == 0
        )
        from_chips.update(got)
        waiting = [((l, "w_in"), parts[l]["w_in"])]

    small_list = []
    for name in SMALL[:-1]:
        small_list.append(jnp.stack([small_grads[l][name] for l in range(DEPTH)]))
    small_list += [g_final[0], loss[0, :1]]
    small_shapes = [a.shape for a in small_list]
    pack = _pack(small_list, SMALL_ROWS).reshape(N_DEV, SMALL_ROWS_PER_DEV, D)
    summed = _unpack(_all_reduce_small(pack, name="all_reduce_small"), small_shapes)
    loss_total = summed[-1][0]
    grads = dict(zip(SMALL, summed[:-1]))
    cw = grads["conv_w"].reshape(DEPTH, CONV_WIDTH, N_DEV, D_RNN // N_DEV)
    grads["conv_w"] = lax.dynamic_index_in_dim(cw, dev, axis=2, keepdims=False)

    delta, new_m, new_v = {}, {}, {}
    for name in BIG:
        w, m, v = given[name], given["m_" + name], given["v_" + name]
        mine = [parts[l][name] for l in range(DEPTH)]
        theirs = [from_chips[l, name] for l in range(DEPTH)]
        if name == "w_up":
            sums = [_sum_chips(mine[l], theirs[l], chip, name=f"sum_chips_{name}_{l}").T for l in range(DEPTH)]
            out = _adamw_layers(w, sums, m, v, tr=TR, name=f"adamw_{name}")
        else:
            out = _adamw_reduced(
                rows_first(name, w), mine, theirs, rows_first(name, m), rows_first(name, v), chip, tr=TR, name=f"adamw_{name}"
            )
            out = [rows_first(name, a) for a in out]
        grads[name], delta[name], new_m[name], new_v[name] = out
    two_d = lambda a: a.reshape(1, -1) if a.ndim == 1 else a
    groups = [tuple(two_d(a) for a in (given[n], grads[n], given["m_" + n], given["v_" + n])) for n in SMALL]
    for n, (d, m2, v2) in zip(SMALL, _adamw_small(groups, name="adamw_small")):
        shape = given[n].shape
        delta[n], new_m[n], new_v[n] = d.reshape(shape), m2.reshape(shape), v2.reshape(shape)

    return (
        loss_total, dx[None],
        *[grads[n] for n in WEIGHTS], *[delta[n] for n in WEIGHTS], *[new_m[n] for n in WEIGHTS], *[new_v[n] for n in WEIGHTS],
    )


def kernel(x, norm_mix_g, w_in, conv_w, conv_b, lru_w_a, lru_b_a, lru_w_x, lru_b_x, lru_lambda, sgu_ln_g, sgu_ln_b, sgu_w_s, sgu_b_s, w_branch_a, w_branch_b, w_out, norm_ffn_g, w_up, w_down, final_norm_g, loss_target, m_norm_mix_g, m_w_in, m_conv_w, m_conv_b, m_lru_w_a, m_lru_b_a, m_lru_w_x, m_lru_b_x, m_lru_lambda, m_sgu_ln_g, m_sgu_ln_b, m_sgu_w_s, m_sgu_b_s, m_w_branch_a, m_w_branch_b, m_w_out, m_norm_ffn_g, m_w_up, m_w_down, m_final_norm_g, v_norm_mix_g, v_w_in, v_conv_w, v_conv_b, v_lru_w_a, v_lru_b_a, v_lru_w_x, v_lru_b_x, v_lru_lambda, v_sgu_ln_g, v_sgu_ln_b, v_sgu_w_s, v_sgu_b_s, v_w_branch_a, v_w_branch_b, v_w_out, v_norm_ffn_g, v_w_up, v_w_down, v_final_norm_g):
    return _step(dict(locals()))
```

```python
import jax
import jax.numpy as jnp
from jax import lax
from jax.experimental import pallas as pl
from jax.experimental.pallas import tpu as pltpu

F32 = jnp.float32
BF16 = jnp.bfloat16
SDS = jax.ShapeDtypeStruct
MESH = pl.DeviceIdType.MESH

D = 1024
D_RNN = 1280
D_SGU = 1024
D_FF = 4096
D_IN = 2 * D_RNN + 2 * D_SGU + 2 * D
DEPTH = 2
RNN_HEADS = 20
HEAD_DIM = 64
CONV_WIDTH = 4
LRU_C = 8.0
SGU_GROUPS = 8
SGU_BLOCK = 128
CHUNK = 64
EPS = 1e-6
N_DEV = 8

ADAM_LR = 0.001
ADAM_B1 = 0.9
ADAM_B2 = 0.999
ADAM_EPS = 1e-08
ADAM_WD = 0.01
ADAM_STEP = 10

LANES = 128
SUBLANES = 8
VMEM_LIMIT_BYTES = 56 * 1024 * 1024

N_RNN_TILES = D_RNN // LANES
RNN_TILES_PER_STEP = 5
GRNN_BLK128 = D_RNN // LANES
U_BLK512 = (2 * D_RNN) // 512
V_BLK512 = (2 * D_RNN + D_SGU) // 512
GA_BLK512 = (2 * D_RNN + 2 * D_SGU) // 512
GB_BLK512 = (2 * D_RNN + 2 * D_SGU + D) // 512

SMALL_ROWS_PER_DEV = 80
SMALL_ROWS = N_DEV * SMALL_ROWS_PER_DEV


def _params(*sem):
    return pltpu.CompilerParams(dimension_semantics=sem, vmem_limit_bytes=VMEM_LIMIT_BYTES)


def _sigmoid(x):
    return 0.5 + 0.5 * jnp.tanh(0.5 * x)


_GELU_C = 0.7978845608028654
_GELU_K = 0.044715


def _gelu(x):
    t = jnp.tanh(_GELU_C * (x + _GELU_K * x * x * x))
    return 0.5 * x * (1.0 + t)


def _gelu_and_grad(x):
    t = jnp.tanh(_GELU_C * (x + _GELU_K * x * x * x))
    val = 0.5 * x * (1.0 + t)
    grad = 0.5 * (1.0 + t) + 0.5 * x * (1.0 - t * t) * _GELU_C * (1.0 + 3.0 * _GELU_K * x * x)
    return val, grad


def _one_minus_square(log_a, a):
    return -jnp.tanh(log_a) * (1.0 + a * a)


def _dot(a, b):
    return jnp.dot(a, b, preferred_element_type=F32)


def _dot_nt(a, b):
    return lax.dot_general(a, b, (((1,), (1,)), ((), ())), preferred_element_type=F32)


def _dot_tn(a, b):
    return lax.dot_general(a, b, (((0,), (0,)), ((), ())), preferred_element_type=F32)


def _norm_matmul_nt(x, g, w, *, tm, tn, name, comm=None):
    s, n = x.shape[0], w.shape[0]
    tm, tn = min(tm, s), min(tn, n)

    def body(x_ref, g_ref, w_ref, o_ref, h_ref):
        @pl.when(pl.program_id(1) == 0)
        def _():
            xv = x_ref[...]
            r = lax.rsqrt(jnp.mean(xv * xv, axis=-1, keepdims=True) + EPS)
            h_ref[...] = (xv * r * g_ref[...]).astype(BF16)

        o_ref[...] = _dot_nt(h_ref[...], w_ref[...]).astype(o_ref.dtype)

    return _call(
        body,
        (x, g, w),
        name=name,
        grid=(s // tm, n // tn),
        in_specs=[
            pl.BlockSpec((tm, D), lambda i, j: (i, 0)),
            pl.BlockSpec((1, D), lambda i, j: (0, 0)),
            pl.BlockSpec((tn, D), lambda i, j: (j, 0)),
        ],
        out_specs=[pl.BlockSpec((tm, tn), lambda i, j: (i, j)), pl.BlockSpec((tm, D), lambda i, j: (i, 0))],
        out_shape=[SDS((s, n), BF16), SDS((s, D), BF16)],
        semantics=("parallel", "arbitrary"),
        comm=comm,
    )


def _matmul_nn_res(a, w, res, *, relu2, tm, name, comm=None):
    s, k = a.shape
    tm = min(tm, s)

    def body(a_ref, w_ref, r_ref, o_ref):
        av = a_ref[...]
        if relu2:
            t = jnp.maximum(av.astype(F32), 0.0)
            av = (t * t).astype(BF16)
        o_ref[...] = r_ref[...] + _dot(av, w_ref[...])

    return _call(
        body,
        (a, w, res),
        name=name,
        grid=(s // tm,),
        in_specs=[
            pl.BlockSpec((tm, k), lambda i: (i, 0)),
            pl.BlockSpec((k, D), lambda i: (0, 0)),
            pl.BlockSpec((tm, D), lambda i: (i, 0)),
        ],
        out_specs=pl.BlockSpec((tm, D), lambda i: (i, 0)),
        out_shape=SDS((s, D), F32),
        semantics=("parallel",),
        comm=comm,
    )


def _matmul_nt_drelu2(a, w, pre, *, tm, tn, name):
    s, n = a.shape[0], w.shape[0]
    tm, tn = min(tm, s), min(tn, n)

    def body(a_ref, w_ref, p_ref, o_ref):
        d = _dot_nt(a_ref[...], w_ref[...])
        o_ref[...] = (d * (2.0 * jnp.maximum(p_ref[...].astype(F32), 0.0))).astype(o_ref.dtype)

    return pl.pallas_call(
        body,
        name=name,
        grid=(s // tm, n // tn),
        in_specs=[
            pl.BlockSpec((tm, D), lambda i, j: (i, 0)),
            pl.BlockSpec((tn, D), lambda i, j: (j, 0)),
            pl.BlockSpec((tm, tn), lambda i, j: (i, j)),
        ],
        out_specs=pl.BlockSpec((tm, tn), lambda i, j: (i, j)),
        out_shape=SDS((s, n), BF16),
        compiler_params=_params("parallel", "arbitrary"),
    )(a, w, pre)


def _matmul_tn(a_list, b, *, relu2, tka, name, comm=None):
    s = b.shape[0]
    n = len(a_list)
    nblk = [a.shape[1] // tka for a in a_list]
    starts = [sum(nblk[:p]) for p in range(n)]

    def body(*refs):
        a_refs, b_ref, o_ref = refs[:n], refs[n], refs[n + 1]
        i = pl.program_id(0)
        for p in range(n):

            @pl.when((i >= starts[p]) & (i < starts[p] + nblk[p]))
            def _(p=p):
                av = a_refs[p][...]
                if relu2:
                    t = jnp.maximum(av.astype(F32), 0.0)
                    av = (t * t).astype(BF16)
                o_ref[...] = _dot_tn(av, b_ref[...]).astype(o_ref.dtype)

    def piece_spec(p):
        return pl.BlockSpec((s, tka), lambda i: (0, jnp.clip(i - starts[p], 0, nblk[p] - 1)))

    return _call(
        body,
        (*a_list, b),
        name=name,
        grid=(sum(nblk),),
        in_specs=[piece_spec(p) for p in range(n)] + [pl.BlockSpec((s, D), lambda i: (0, 0))],
        out_specs=pl.BlockSpec((tka, D), lambda i: (i, 0)),
        out_shape=SDS((sum(nblk) * tka, D), BF16),
        semantics=("parallel",),
        comm=comm,
    )


def _matmul_nn_rmsnorm_bwd(a_list, w, x, g, res, *, tm, name, comm=None):
    s = x.shape[0]
    tm = min(tm, s)
    n = len(a_list)
    widths = [a.shape[1] for a in a_list]
    offs = [sum(widths[:p]) for p in range(n)]
    k = sum(widths)

    def body(*refs):
        a_refs = refs[:n]
        w_ref, x_ref, g_ref, r_ref, dx_ref, dxb_ref, dg_ref = refs[n:]

        @pl.when(pl.program_id(0) == 0)
        def _():
            dg_ref[...] = jnp.zeros_like(dg_ref)

        dh = _dot(a_refs[0][...], w_ref[0 : widths[0], :])
        for p in range(1, n):
            dh += _dot(a_refs[p][...], w_ref[offs[p] : offs[p] + widths[p], :])
        xv = x_ref[...]
        r = lax.rsqrt(jnp.mean(xv * xv, axis=-1, keepdims=True) + EPS)
        xhat = xv * r
        dxh = dh * g_ref[...]
        dx = r_ref[...] + r * (dxh - xhat * jnp.mean(dxh * xhat, axis=-1, keepdims=True))
        dx_ref[...] = dx
        dxb_ref[...] = dx.astype(BF16)
        dg_ref[...] += jnp.sum(dh * xhat, axis=0, keepdims=True)

    act = pl.BlockSpec((tm, D), lambda i: (i, 0))
    vec = pl.BlockSpec((1, D), lambda i: (0, 0))
    return _call(
        body,
        (*a_list, w, x, g, res),
        name=name,
        grid=(s // tm,),
        in_specs=[pl.BlockSpec((tm, wd), lambda i: (i, 0)) for wd in widths]
        + [pl.BlockSpec((k, D), lambda i: (0, 0), pipeline_mode=pl.Buffered(1)), act, vec, act],
        out_specs=[act, act, vec],
        out_shape=[SDS((s, D), F32), SDS((s, D), BF16), SDS((1, D), F32)],
        semantics=("arbitrary",),
        comm=comm,
    )


def _rows_before(ext, k):
    if k == 0:
        return ext[SUBLANES:, :]
    return pltpu.roll(ext, k, 0)[SUBLANES:, :]


def _rows_after(ext, k, n):
    if k == 0:
        return ext[:n, :]
    return pltpu.roll(ext, n + SUBLANES - k, 0)[:n, :]


def _scan_forward(a, b, n):
    row = lax.broadcasted_iota(jnp.int32, a.shape, 0)
    d = 1
    while d < n:
        if d < SUBLANES:
            m = row >= d
            a_s = jnp.where(m, pltpu.roll(a, d, 0), 1.0)
            b_s = jnp.where(m, pltpu.roll(b, d, 0), 0.0)
            b = a * b_s + b
            a = a * a_s
        else:
            b = jnp.concatenate([b[:d], a[d:] * b[: n - d] + b[d:]], axis=0)
            a = jnp.concatenate([a[:d], a[d:] * a[: n - d]], axis=0)
        d *= 2
    return a, b


def _scan_backward(a, b, n):
    row = lax.broadcasted_iota(jnp.int32, a.shape, 0)
    d = 1
    while d < n:
        if d < SUBLANES:
            m = row < n - d
            a_s = jnp.where(m, pltpu.roll(a, n - d, 0), 1.0)
            b_s = jnp.where(m, pltpu.roll(b, n - d, 0), 0.0)
            b = a * b_s + b
            a = a * a_s
        else:
            b = jnp.concatenate([a[: n - d] * b[d:] + b[: n - d], b[n - d :]], axis=0)
            a = jnp.concatenate([a[: n - d] * a[d:], a[n - d :]], axis=0)
        d *= 2
    return b


def _repeat_matrix(n):
    groups = n // SUBLANES
    return (jnp.arange(n)[:, None] // SUBLANES == jnp.arange(3 * groups)[None, :] % groups).astype(BF16)


def _scan_rows(a, b, n, repeat_ref, a_scr, b_scr, reverse):
    groups = n // SUBLANES
    a3 = a.reshape(groups, SUBLANES, LANES)
    b3 = b.reshape(groups, SUBLANES, LANES)
    sub = lax.broadcasted_iota(jnp.int32, a3.shape, 1)
    for d in (1, 2, 4):
        m = (sub < SUBLANES - d) if reverse else (sub >= d)
        shift = SUBLANES - d if reverse else d
        a_s = jnp.where(m, pltpu.roll(a3, shift, 1), 1.0)
        b_s = jnp.where(m, pltpu.roll(b3, shift, 1), 0.0)
        b3 = a3 * b_s + b3
        a3 = a3 * a_s
    a_scr[...] = a3.reshape(n, LANES)
    b_scr[...] = b3.reshape(n, LANES)
    edge = 0 if reverse else SUBLANES - 1
    a_tot = a_scr[pl.ds(edge, groups, stride=SUBLANES), :]
    b_tot = b_scr[pl.ds(edge, groups, stride=SUBLANES), :]
    row = lax.broadcasted_iota(jnp.int32, a_tot.shape, 0)
    if reverse:
        through = _scan_backward(a_tot, b_tot, groups)
        entering = jnp.where(row < groups - 1, pltpu.roll(through, groups - 1, 0), 0.0)
    else:
        _, through = _scan_forward(a_tot, b_tot, groups)
        entering = jnp.where(row >= 1, pltpu.roll(through, 1, 0), 0.0)
    hi = entering.astype(BF16)
    rest = entering - hi.astype(F32)
    mid = rest.astype(BF16)
    lo = (rest - mid.astype(F32)).astype(BF16)
    repeated = _dot(repeat_ref[...], jnp.concatenate([hi, mid, lo], axis=0))
    return b_scr[...] + a_scr[...] * repeated


def _softplus_neg(lam):
    z = -lam
    return jnp.maximum(z, 0.0) + jnp.log1p(jnp.exp(-jnp.abs(z)))


def _conv_and_gates(xc, xprev, cw_ref, cb_ref, wa_ref, ba_ref, wx_ref, bx_ref, lam_ref):
    ext = jnp.concatenate([xprev, xc], axis=0)
    x1, x2, x3 = _rows_before(ext, 1), _rows_before(ext, 2), _rows_before(ext, 3)
    xr = cb_ref[...] + x3 * cw_ref[0:1, :] + x2 * cw_ref[1:2, :] + x1 * cw_ref[2:3, :] + xc * cw_ref[3:4, :]
    xrb = xr.astype(BF16)
    r = _sigmoid(_dot(xrb, wa_ref[...]) + ba_ref[...])
    i = _sigmoid(_dot(xrb, wx_ref[...]) + bx_ref[...])
    sp = _softplus_neg(lam_ref[...])
    log_a = (-LRU_C * r) * sp
    a = jnp.exp(log_a)
    return xr, (x1, x2, x3), r, i, a, _one_minus_square(log_a, a)


def _branch_a_fwd(proj, cw, cb, wa2, ba, wx2, bx, lam, *, tc, name, comm=None):
    s = proj.shape[0]
    tc = min(tc, s)

    def body(x_ref, g_ref, cw_ref, cb_ref, wa_ref, ba_ref, wx_ref, bx_ref, lam_ref, rep_ref, h_ref, y_ref,
             xprev, hlast, a_scr, b_scr):
        @pl.when(pl.program_id(1) == 0)
        def _():
            xprev[...] = jnp.zeros_like(xprev)
            hlast[...] = jnp.zeros_like(hlast)

        for t in range(RNN_TILES_PER_STEP):
            cols = lambda ref: ref.at[:, pl.ds(t * LANES, LANES)]
            one_tile(
                cols(x_ref), cols(g_ref), cols(cw_ref), cols(cb_ref), wa_ref.at[t], cols(ba_ref), wx_ref.at[t], cols(bx_ref),
                cols(lam_ref), rep_ref, cols(h_ref), cols(y_ref), cols(xprev), cols(hlast), a_scr.at[t], b_scr.at[t],
            )

    def one_tile(x_ref, g_ref, cw_ref, cb_ref, wa_ref, ba_ref, wx_ref, bx_ref, lam_ref, rep_ref, h_ref, y_ref,
                 xprev, hlast, a_scr, b_scr):
        xc = x_ref[...].astype(F32)
        xr, _, r, i, a, om = _conv_and_gates(xc, xprev[...], cw_ref, cb_ref, wa_ref, ba_ref, wx_ref, bx_ref, lam_ref)
        xprev[...] = xc[tc - SUBLANES :, :]
        u = jnp.sqrt(om) * (i * xr)
        row8 = lax.broadcasted_iota(jnp.int32, (SUBLANES, LANES), 0)
        first = u[:SUBLANES] + jnp.where(row8 == 0, a[:SUBLANES] * hlast[SUBLANES - 1 : SUBLANES, :], 0.0)
        h = _scan_rows(a, jnp.concatenate([first, u[SUBLANES:]], axis=0), tc, rep_ref, a_scr, b_scr, reverse=False)
        hlast[...] = h[tc - SUBLANES :, :]
        h_ref[...] = h
        y_ref[...] = (h * _gelu(g_ref[...].astype(F32))).astype(BF16)

    wide = RNN_TILES_PER_STEP * LANES
    tile = lambda j, c: (0, j)
    vec = pl.BlockSpec((1, wide), tile)
    mats = pl.BlockSpec((RNN_TILES_PER_STEP, LANES, LANES), lambda j, c: (j, 0, 0))
    repeat = _repeat_matrix(tc)
    return _call(
        body,
        (proj, proj, cw, cb, wa2, ba, wx2, bx, lam, repeat),
        name=name,
        grid=(N_RNN_TILES // RNN_TILES_PER_STEP, s // tc),
        in_specs=[
            pl.BlockSpec((tc, wide), lambda j, c: (c, j)),
            pl.BlockSpec((tc, wide), lambda j, c: (c, D_RNN // wide + j)),
            pl.BlockSpec((CONV_WIDTH, wide), tile),
            vec,
            mats,
            vec,
            mats,
            vec,
            vec,
            pl.BlockSpec(repeat.shape, lambda j, c: (0, 0)),
        ],
        out_specs=[pl.BlockSpec((tc, wide), lambda j, c: (c, j)), pl.BlockSpec((tc, wide), lambda j, c: (c, j))],
        out_shape=[SDS((s, D_RNN), F32), SDS((s, D_RNN), BF16)],
        scratch_shapes=[pltpu.VMEM((SUBLANES, wide), F32)] * 2 + [pltpu.VMEM((RNN_TILES_PER_STEP, tc, LANES), F32)] * 2,
        semantics=("parallel", "arbitrary"),
        comm=comm,
    )


def _branch_a_bwd(dy, proj, h, cw, cb, wa2, ba, wx2, bx, lam, wa2t, wx2t, *, tc, name, comm=None):
    s = proj.shape[0]
    tc = min(tc, s)
    nc = s // tc
    halo16 = tc // 16
    halo8 = tc // SUBLANES

    def body(dy_ref, x_ref, xh_ref, g_ref, h_ref, hh_ref, cw_ref, cb_ref, wa_ref, ba_ref, wx_ref, bx_ref, lam_ref,
             wat_ref, wxt_ref, rep_ref, dx_ref, dg_ref, dcw_ref, dcb_ref, dba_ref, dbx_ref, dlam_ref, dwa_ref, dwx_ref,
             carry, dxr_next, a_scr, b_scr):
        cc = pl.program_id(1)
        ct = nc - 1 - cc

        @pl.when(cc == 0)
        def _():
            carry[...] = jnp.zeros_like(carry)
            dxr_next[...] = jnp.zeros_like(dxr_next)
            for ref in (dcw_ref, dcb_ref, dba_ref, dbx_ref, dlam_ref, dwa_ref, dwx_ref):
                ref[...] = jnp.zeros_like(ref)

        for t in range(RNN_TILES_PER_STEP):
            cols = lambda ref: ref.at[:, pl.ds(t * LANES, LANES)]
            one_tile(
                ct, cols(dy_ref), cols(x_ref), cols(xh_ref), cols(g_ref), cols(h_ref), cols(hh_ref), cols(cw_ref), cols(cb_ref),
                wa_ref.at[t], cols(ba_ref), wx_ref.at[t], cols(bx_ref), cols(lam_ref), wat_ref.at[t], wxt_ref.at[t], rep_ref,
                cols(dx_ref), cols(dg_ref), cols(dcw_ref), cols(dcb_ref), cols(dba_ref), cols(dbx_ref), cols(dlam_ref),
                dwa_ref.at[t], dwx_ref.at[t], cols(carry), cols(dxr_next), a_scr.at[t], b_scr.at[t],
            )

    def one_tile(ct, dy_ref, x_ref, xh_ref, g_ref, h_ref, hh_ref, cw_ref, cb_ref, wa_ref, ba_ref, wx_ref, bx_ref, lam_ref,
                 wat_ref, wxt_ref, rep_ref, dx_ref, dg_ref, dcw_ref, dcb_ref, dba_ref, dbx_ref, dlam_ref, dwa_ref, dwx_ref,
                 carry, dxr_next, a_scr, b_scr):
        xc = x_ref[...].astype(F32)
        xprev = jnp.where(ct > 0, xh_ref[SUBLANES:, :].astype(F32), 0.0)
        xr, (x1, x2, x3), r, i, a, om = _conv_and_gates(
            xc, xprev, cw_ref, cb_ref, wa_ref, ba_ref, wx_ref, bx_ref, lam_ref
        )
        inv_norm = lax.rsqrt(om)
        norm = om * inv_norm
        row = lax.broadcasted_iota(jnp.int32, xc.shape, 0)

        hv = h_ref[...]
        ge, ge_grad = _gelu_and_grad(g_ref[...].astype(F32))
        dyv = dy_ref[...].astype(F32)
        dg_ref[...] = (dyv * hv * ge_grad).astype(dg_ref.dtype)
        dh = dyv * ge

        b = dh + jnp.where(row == tc - 1, carry[0:1, :], 0.0)
        a_next = jnp.where(row < tc - 1, pltpu.roll(a, tc - 1, 0), 0.0)
        gadj = _scan_rows(a_next, b, tc, rep_ref, a_scr, b_scr, reverse=True)
        carry[...] = (a * gadj)[:SUBLANES, :]

        hprev_first = jnp.where(ct > 0, hh_ref[SUBLANES - 1 : SUBLANES, :], 0.0)
        hprev = jnp.where(row >= 1, pltpu.roll(hv, 1, 0), hprev_first)
        da = gadj * hprev
        ix = i * xr
        dnorm = gadj * ix
        di = gadj * norm * xr
        dlog_a = da * a - dnorm * (1.0 - om) * inv_norm
        sp = _softplus_neg(lam_ref[...])
        dr = dlog_a * (-LRU_C * sp)
        dsp = jnp.sum(dlog_a * (-LRU_C * r), axis=0, keepdims=True)
        dlam_ref[...] += dsp * (-_sigmoid(-lam_ref[...]))
        dza = dr * r * (1.0 - r)
        dzx = di * i * (1.0 - i)
        dzab, dzxb = dza.astype(BF16), dzx.astype(BF16)
        dxr = gadj * norm * i + _dot(dzab, wat_ref[...]) + _dot(dzxb, wxt_ref[...])
        xrb = xr.astype(BF16)
        dwa_ref[...] += _dot_tn(xrb, dzab)
        dwx_ref[...] += _dot_tn(xrb, dzxb)
        dba_ref[...] += jnp.sum(dza, axis=0, keepdims=True)
        dbx_ref[...] += jnp.sum(dzx, axis=0, keepdims=True)

        ext = jnp.concatenate([dxr, dxr_next[...]], axis=0)
        dx = (
            dxr * cw_ref[3:4, :]
            + _rows_after(ext, 1, tc) * cw_ref[2:3, :]
            + _rows_after(ext, 2, tc) * cw_ref[1:2, :]
            + _rows_after(ext, 3, tc) * cw_ref[0:1, :]
        )
        dxr_next[...] = dxr[:SUBLANES, :]
        dx_ref[...] = dx.astype(dx_ref.dtype)
        dcb_ref[...] += jnp.sum(dxr, axis=0, keepdims=True)
        dcw_ref[3:4, :] += jnp.sum(dxr * xc, axis=0, keepdims=True)
        dcw_ref[2:3, :] += jnp.sum(dxr * x1, axis=0, keepdims=True)
        dcw_ref[1:2, :] += jnp.sum(dxr * x2, axis=0, keepdims=True)
        dcw_ref[0:1, :] += jnp.sum(dxr * x3, axis=0, keepdims=True)

    wide = RNN_TILES_PER_STEP * LANES
    tile = lambda j, c: (0, j)
    mat = lambda j, c: (j, 0, 0)
    cur = lambda j, c: (nc - 1 - c, j)
    vec = pl.BlockSpec((1, wide), tile)
    matspec = pl.BlockSpec((RNN_TILES_PER_STEP, LANES, LANES), mat)
    repeat = _repeat_matrix(tc)
    return _call(
        body,
        (dy, proj, proj, proj, h, h, cw, cb, wa2, ba, wx2, bx, lam, wa2t, wx2t, repeat),
        name=name,
        grid=(N_RNN_TILES // RNN_TILES_PER_STEP, nc),
        in_specs=[
            pl.BlockSpec((tc, wide), cur),
            pl.BlockSpec((tc, wide), cur),
            pl.BlockSpec((16, wide), lambda j, c: (jnp.maximum((nc - 1 - c) * halo16 - 1, 0), j)),
            pl.BlockSpec((tc, wide), lambda j, c: (nc - 1 - c, D_RNN // wide + j)),
            pl.BlockSpec((tc, wide), cur),
            pl.BlockSpec((SUBLANES, wide), lambda j, c: (jnp.maximum((nc - 1 - c) * halo8 - 1, 0), j)),
            pl.BlockSpec((CONV_WIDTH, wide), tile),
            vec,
            matspec,
            vec,
            matspec,
            vec,
            vec,
            matspec,
            matspec,
            pl.BlockSpec(repeat.shape, lambda j, c: (0, 0)),
        ],
        out_specs=[
            pl.BlockSpec((tc, wide), cur),
            pl.BlockSpec((tc, wide), cur),
            pl.BlockSpec((CONV_WIDTH, wide), tile),
            vec,
            vec,
            vec,
            vec,
            matspec,
            matspec,
        ],
        out_shape=[
            SDS((s, D_RNN), BF16),
            SDS((s, D_RNN), BF16),
            SDS((CONV_WIDTH, D_RNN), F32),
            SDS((1, D_RNN), F32),
            SDS((1, D_RNN), F32),
            SDS((1, D_RNN), F32),
            SDS((1, D_RNN), F32),
            SDS((N_RNN_TILES, LANES, LANES), F32),
            SDS((N_RNN_TILES, LANES, LANES), F32),
        ],
        scratch_shapes=[pltpu.VMEM((SUBLANES, wide), F32)] * 2 + [pltpu.VMEM((RNN_TILES_PER_STEP, tc, LANES), F32)] * 2,
        semantics=("parallel", "arbitrary"),
        comm=comm,
    )


def _sgu_specs(tb):
    half = lambda blk: pl.BlockSpec((tb, 512), lambda n: (n, blk))
    return [half(U_BLK512), half(U_BLK512 + 1), half(V_BLK512), half(V_BLK512 + 1)]


def _sgu_normed(v, lng_ref, lnb_ref):
    gv, gv_grad = _gelu_and_grad(v)
    mu = jnp.mean(gv, axis=-1, keepdims=True)
    xc = gv - mu
    rs = lax.rsqrt(jnp.mean(xc * xc, axis=-1, keepdims=True) + EPS)
    xhat = xc * rs
    return xhat * lng_ref[...] + lnb_ref[...], xhat, rs, gv_grad


def _sgu_fwd(proj, lng, lnb, wm, bias, *, tb, name, comm=None):
    s = proj.shape[0]
    tb = min(tb, s)

    def body(u0_ref, u1_ref, v0_ref, v1_ref, lng_ref, lnb_ref, wm_ref, bias_ref, y_ref):
        u = jnp.concatenate([u0_ref[...], u1_ref[...]], axis=1).astype(F32)
        v = jnp.concatenate([v0_ref[...], v1_ref[...]], axis=1).astype(F32)
        gu = _gelu(u)
        vn, _, _, _ = _sgu_normed(v, lng_ref, lnb_ref)
        vnb = vn.astype(BF16)
        for blk in range(tb // SGU_BLOCK):
            rows = slice(blk * SGU_BLOCK, (blk + 1) * SGU_BLOCK)
            for g in range(SGU_GROUPS):
                cols = slice(g * LANES, (g + 1) * LANES)
                mixed = _dot(wm_ref[g], vnb[rows, cols]) + bias_ref[g]
                y_ref[rows, cols] = (gu[rows, cols] * mixed).astype(BF16)

    const2 = lambda n: (0, 0)
    const3 = lambda n: (0, 0, 0)
    return _call(
        body,
        (proj, proj, proj, proj, lng, lnb, wm, bias),
        name=name,
        grid=(s // tb,),
        in_specs=_sgu_specs(tb)
        + [
            pl.BlockSpec((1, D_SGU), const2),
            pl.BlockSpec((1, D_SGU), const2),
            pl.BlockSpec((SGU_GROUPS, SGU_BLOCK, SGU_BLOCK), const3),
            pl.BlockSpec((SGU_GROUPS, SGU_BLOCK, LANES), const3),
        ],
        out_specs=pl.BlockSpec((tb, D_SGU), lambda n: (n, 0)),
        out_shape=SDS((s, D_SGU), BF16),
        semantics=("parallel",),
        comm=comm,
    )


def _sgu_bwd(dy, proj, lng, lnb, wm, wmt, bias, mask, *, tb, name, comm=None):
    s = proj.shape[0]
    tb = min(tb, s)
    nb = s // tb

    def body(dy_ref, u0_ref, u1_ref, v0_ref, v1_ref, lng_ref, lnb_ref, wm_ref, wmt_ref, bias_ref, mask_ref,
             du_ref, dv_ref, dws_ref, dbs_ref, dlng_ref, dlnb_ref, dvn_scr, dbs_acc):
        n = pl.program_id(0)

        @pl.when(n == 0)
        def _():
            dbs_acc[...] = jnp.zeros_like(dbs_acc)
            for ref in (dws_ref, dlng_ref, dlnb_ref):
                ref[...] = jnp.zeros_like(ref)

        u = jnp.concatenate([u0_ref[...], u1_ref[...]], axis=1).astype(F32)
        v = jnp.concatenate([v0_ref[...], v1_ref[...]], axis=1).astype(F32)
        gu, gu_grad = _gelu_and_grad(u)
        vn, xhat, rs, gv_grad = _sgu_normed(v, lng_ref, lnb_ref)
        vnb = vn.astype(BF16)
        dyv = dy_ref[...].astype(F32)
        for blk in range(tb // SGU_BLOCK):
            rows = slice(blk * SGU_BLOCK, (blk + 1) * SGU_BLOCK)
            for g in range(SGU_GROUPS):
                cols = slice(g * LANES, (g + 1) * LANES)
                vt = vnb[rows, cols]
                mixed = _dot(wm_ref[g], vt) + bias_ref[g]
                dyt = dyv[rows, cols]
                du_ref[rows, cols] = (dyt * mixed * gu_grad[rows, cols]).astype(BF16)
                dmix = dyt * gu[rows, cols]
                dmixb = dmix.astype(BF16)
                dvn_scr[rows, cols] = _dot(wmt_ref[g], dmixb)
                dws_ref[g] += _dot_nt(dmixb, vt) * mask_ref[...]
                dbs_acc[g] += dmix
        dvn = dvn_scr[...]
        dlng_ref[...] += jnp.sum(dvn * xhat, axis=0, keepdims=True)
        dlnb_ref[...] += jnp.sum(dvn, axis=0, keepdims=True)
        dxh = dvn * lng_ref[...]
        dgv = rs * (
            dxh - jnp.mean(dxh, axis=-1, keepdims=True) - xhat * jnp.mean(dxh * xhat, axis=-1, keepdims=True)
        )
        dv_ref[...] = (dgv * gv_grad).astype(BF16)

        @pl.when(n == nb - 1)
        def _():
            for g in range(SGU_GROUPS):
                dbs_ref[g] = jnp.broadcast_to(jnp.sum(dbs_acc[g], axis=-1, keepdims=True), (SGU_BLOCK, LANES))

    const2 = lambda n: (0, 0)
    const3 = lambda n: (0, 0, 0)
    gmat = pl.BlockSpec((SGU_GROUPS, SGU_BLOCK, SGU_BLOCK), const3)
    vec = pl.BlockSpec((1, D_SGU), const2)
    act = pl.BlockSpec((tb, D_SGU), lambda n: (n, 0))
    return _call(
        body,
        (dy, proj, proj, proj, proj, lng, lnb, wm, wmt, bias, mask),
        name=name,
        grid=(nb,),
        in_specs=[act] + _sgu_specs(tb) + [vec, vec, gmat, gmat, gmat, pl.BlockSpec((SGU_BLOCK, SGU_BLOCK), const2)],
        out_specs=[act, act, gmat, gmat, vec, vec],
        out_shape=[
            SDS((s, D_SGU), BF16),
            SDS((s, D_SGU), BF16),
            SDS((SGU_GROUPS, SGU_BLOCK, SGU_BLOCK), F32),
            SDS((SGU_GROUPS, SGU_BLOCK, LANES), F32),
            SDS((1, D_SGU), F32),
            SDS((1, D_SGU), F32),
        ],
        scratch_shapes=[pltpu.VMEM((tb, D_SGU), F32), pltpu.VMEM((SGU_GROUPS, SGU_BLOCK, LANES), F32)],
        semantics=("arbitrary",),
        comm=comm,
    )


def _gate_specs(tm):
    half = lambda blk: pl.BlockSpec((tm, 512), lambda i: (i, blk))
    return [half(GA_BLK512), half(GA_BLK512 + 1), half(GB_BLK512), half(GB_BLK512 + 1)]


def _merge_fwd(ya_pre, yb_pre, proj, x, w_ba, w_bb, w_out, *, tm, name, comm=None):
    s = x.shape[0]
    tm = min(tm, s)

    def body(ya_ref, yb_ref, a0, a1, b0, b1, x_ref, wa_ref, wb_ref, wo_ref, x1_ref, yao_ref, ybo_ref):
        ya = _dot(ya_ref[...], wa_ref[...])
        yb = _dot(yb_ref[...], wb_ref[...])
        sa = _sigmoid(jnp.concatenate([a0[...], a1[...]], axis=1).astype(F32))
        sb = _sigmoid(jnp.concatenate([b0[...], b1[...]], axis=1).astype(F32))
        merged = sa * ya + sb * yb
        x1_ref[...] = x_ref[...] + _dot(merged.astype(BF16), wo_ref[...])
        yao_ref[...] = ya.astype(BF16)
        ybo_ref[...] = yb.astype(BF16)

    whole = lambda r: pl.BlockSpec((r, D), lambda i: (0, 0))
    act = pl.BlockSpec((tm, D), lambda i: (i, 0))
    return _call(
        body,
        (ya_pre, yb_pre, proj, proj, proj, proj, x, w_ba, w_bb, w_out),
        name=name,
        grid=(s // tm,),
        in_specs=[pl.BlockSpec((tm, D_RNN), lambda i: (i, 0)), act] + _gate_specs(tm) + [act, whole(D_RNN), whole(D_SGU), whole(D)],
        out_specs=[act, act, act],
        out_shape=[SDS((s, D), F32), SDS((s, D), BF16), SDS((s, D), BF16)],
        semantics=("parallel",),
        comm=comm,
    )


def _merge_bwd(dx1, ya, yb, proj, w_ba, w_bb, w_out, *, tm, name, comm=None):
    s = dx1.shape[0]
    tm = min(tm, s)

    def body(dx_ref, ya_ref, yb_ref, a0, a1, b0, b1, wa_ref, wb_ref, wo_ref,
             mg_ref, dya_ref, dyb_ref, dga_ref, dgb_ref, dyap_ref, dybp_ref):
        dm = _dot_nt(dx_ref[...], wo_ref[...])
        ya = ya_ref[...].astype(F32)
        yb = yb_ref[...].astype(F32)
        sa = _sigmoid(jnp.concatenate([a0[...], a1[...]], axis=1).astype(F32))
        sb = _sigmoid(jnp.concatenate([b0[...], b1[...]], axis=1).astype(F32))
        mg_ref[...] = (sa * ya + sb * yb).astype(BF16)
        dya = (dm * sa).astype(BF16)
        dyb = (dm * sb).astype(BF16)
        dya_ref[...] = dya
        dyb_ref[...] = dyb
        dga_ref[...] = (dm * ya * sa * (1.0 - sa)).astype(BF16)
        dgb_ref[...] = (dm * yb * sb * (1.0 - sb)).astype(BF16)
        dyap_ref[...] = _dot_nt(dya, wa_ref[...]).astype(BF16)
        dybp_ref[...] = _dot_nt(dyb, wb_ref[...]).astype(BF16)

    whole = lambda r: pl.BlockSpec((r, D), lambda i: (0, 0))
    act = pl.BlockSpec((tm, D), lambda i: (i, 0))
    act_rnn = pl.BlockSpec((tm, D_RNN), lambda i: (i, 0))
    return _call(
        body,
        (dx1, ya, yb, proj, proj, proj, proj, w_ba, w_bb, w_out),
        name=name,
        grid=(s // tm,),
        in_specs=[act, act, act] + _gate_specs(tm) + [whole(D_RNN), whole(D_SGU), whole(D)],
        out_specs=[act, act, act, act, act, act_rnn, act],
        out_shape=[SDS((s, D), BF16)] * 5 + [SDS((s, D_RNN), BF16), SDS((s, D_SGU), BF16)],
        semantics=("parallel",),
        comm=comm,
    )


def _final_loss(x, g, target, *, tm, name):
    s = x.shape[0]
    tm = min(tm, s)

    def body(x_ref, g_ref, t_ref, dx_ref, dxb_ref, dg_ref, loss_ref):
        @pl.when(pl.program_id(0) == 0)
        def _():
            dg_ref[...] = jnp.zeros_like(dg_ref)
            loss_ref[...] = jnp.zeros_like(loss_ref)

        xv = x_ref[...]
        r = lax.rsqrt(jnp.mean(xv * xv, axis=-1, keepdims=True) + EPS)
        xhat = xv * r
        e = xhat * g_ref[...] - t_ref[...]
        loss_ref[...] += 0.5 * jnp.sum(jnp.mean(e * e, axis=-1, keepdims=True), axis=0, keepdims=True)
        dy = e * (1.0 / D)
        dxh = dy * g_ref[...]
        dx = r * (dxh - xhat * jnp.mean(dxh * xhat, axis=-1, keepdims=True))
        dx_ref[...] = dx
        dxb_ref[...] = dx.astype(BF16)
        dg_ref[...] += jnp.sum(dy * xhat, axis=0, keepdims=True)

    act = pl.BlockSpec((tm, D), lambda i: (i, 0))
    vec = pl.BlockSpec((1, D), lambda i: (0, 0))
    return pl.pallas_call(
        body,
        name=name,
        grid=(s // tm,),
        in_specs=[act, vec, act],
        out_specs=[act, act, vec, pl.BlockSpec((SUBLANES, LANES), lambda i: (0, 0))],
        out_shape=[SDS((s, D), F32), SDS((s, D), BF16), SDS((1, D), F32), SDS((SUBLANES, LANES), F32)],
        compiler_params=_params("arbitrary"),
    )(x, g, target)


def _adamw_math(w, g, m, v):
    m2 = ADAM_B1 * m + (1.0 - ADAM_B1) * g
    v2 = ADAM_B2 * v + (1.0 - ADAM_B2) * (g * g)
    m_hat = m2 / (1.0 - ADAM_B1**ADAM_STEP)
    v_hat = v2 / (1.0 - ADAM_B2**ADAM_STEP)
    delta = -ADAM_LR * (m_hat / (jnp.sqrt(v_hat) + ADAM_EPS) + ADAM_WD * w)
    return delta, m2, v2


def _row_tile(rows, cap):
    return max(t for t in range(SUBLANES, min(cap, rows) + 1, SUBLANES) if rows % t == 0)


def _adamw_layers(w, grads, m, v, *, tr, name):
    depth, r, c = w.shape
    tr = _row_tile(r, tr)

    def body(*refs):
        g_refs = refs[:depth]
        w_ref, m_ref, v_ref, g_out, d_ref, mo_ref, vo_ref = refs[depth:]
        for l in range(depth):

            @pl.when(pl.program_id(0) == l)
            def _(l=l):
                g = g_refs[l][...]
                g_out[...] = g
                d_ref[...], mo_ref[...], vo_ref[...] = _adamw_math(w_ref[...], g, m_ref[...], v_ref[...])

    def of_layer(ll):
        return pl.BlockSpec((tr, c), lambda l, i: (jnp.where(l == ll, i, 0), 0))

    stacked = pl.BlockSpec((None, tr, c), lambda l, i: (l, i, 0))
    return pl.pallas_call(
        body,
        name=name,
        grid=(depth, r // tr),
        in_specs=[of_layer(ll) for ll in range(depth)] + [stacked] * 3,
        out_specs=[stacked] * 4,
        out_shape=[SDS((depth, r, c), F32)] * 4,
        compiler_params=_params("parallel", "parallel"),
    )(*grads, w, m, v)


def _adamw_reduced(w, parts, from_chips, m, v, chip, *, tr, name):
    depth, r, _ = w.shape
    tr = _row_tile(r, tr)

    def body(chip_ref, *refs):
        p_refs, c_refs = refs[:depth], refs[depth : 2 * depth]
        w_ref, m_ref, v_ref, g_out, d_ref, mo_ref, vo_ref = refs[2 * depth :]
        for l in range(depth):

            @pl.when(pl.program_id(0) == l)
            def _(l=l):
                got = c_refs[l]
                g = ((p_refs[l][...].astype(F32) + got[0].astype(F32)) + got[1].astype(F32)) + got[2].astype(F32)
                g_out[...] = g
                d_ref[...], mo_ref[...], vo_ref[...] = _adamw_math(w_ref[...], g, m_ref[...], v_ref[...])

    def mine_of_layer(ll):
        return pl.BlockSpec((None, tr, D), lambda l, i, chip_ref: (chip_ref[0], jnp.where(l == ll, i, 0), 0))

    def theirs_of_layer(ll):
        return pl.BlockSpec((3, tr, D), lambda l, i, chip_ref: (0, jnp.where(l == ll, i, 0), 0))

    stacked = pl.BlockSpec((None, tr, D), lambda l, i, chip_ref: (l, i, 0))
    return pl.pallas_call(
        body,
        name=name,
        grid_spec=pltpu.PrefetchScalarGridSpec(
            num_scalar_prefetch=1,
            grid=(depth, r // tr),
            in_specs=[mine_of_layer(ll) for ll in range(depth)]
            + [theirs_of_layer(ll) for ll in range(depth)]
            + [stacked] * 3,
            out_specs=[stacked] * 4,
        ),
        out_shape=[SDS((depth, r, D), F32)] * 4,
        compiler_params=_params("parallel", "parallel"),
    )(chip, *parts, *from_chips, w, m, v)


def _adamw_small(groups, *, name):
    n = len(groups)

    def body(*refs):
        ins, outs = refs[: 4 * n], refs[4 * n :]
        for i in range(n):
            w, g, m, v = (ref[...] for ref in ins[4 * i : 4 * i + 4])
            outs[3 * i][...], outs[3 * i + 1][...], outs[3 * i + 2][...] = _adamw_math(w, g, m, v)

    vmem = pl.BlockSpec(memory_space=pltpu.VMEM)
    outs = pl.pallas_call(
        body,
        name=name,
        in_specs=[vmem] * (4 * n),
        out_specs=[vmem] * (3 * n),
        out_shape=[SDS(grp[0].shape, F32) for grp in groups for _ in range(3)],
        compiler_params=pltpu.CompilerParams(vmem_limit_bytes=VMEM_LIMIT_BYTES),
    )(*[a for grp in groups for a in grp])
    return [tuple(outs[3 * i : 3 * i + 3]) for i in range(n)]


ANY = pl.BlockSpec(memory_space=pl.ANY)


def _position():
    return lax.axis_index("x"), lax.axis_index("y"), lax.axis_index("c")


def _other_chips(x, y):
    return [(1 - x, y), (x, 1 - y), (1 - x, 1 - y)]


class _Comm:
    def __init__(self, inputs, out_shapes, sem_counts, start, middle, finish, middle_at=1.0, aliases=()):
        self.inputs, self.out_shapes, self.sem_counts = list(inputs), list(out_shapes), list(sem_counts)
        self.start, self.middle, self.finish = start, middle, finish
        self.middle_at = middle_at
        self.aliases = list(aliases)

    def sem_shapes(self):
        return [pltpu.SemaphoreType.DMA((n,)) for n in self.sem_counts]


def _merge_comms(comms):
    bounds, i, o, s = [], 0, 0, 0
    for cm in comms:
        bounds.append((i, i + len(cm.inputs), o, o + len(cm.out_shapes), s, s + len(cm.sem_counts)))
        i, o, s = bounds[-1][1], bounds[-1][3], bounds[-1][5]

    def phase(which):
        def run(ins, outs, sems):
            for cm, (i0, i1, o0, o1, s0, s1) in zip(comms, bounds):
                getattr(cm, which)(ins[i0:i1], outs[o0:o1], sems[s0:s1])

        return run

    return _Comm(
        [a for cm in comms for a in cm.inputs],
        [a for cm in comms for a in cm.out_shapes],
        [a for cm in comms for a in cm.sem_counts],
        phase("start"),
        phase("middle"),
        phase("finish"),
        middle_at=max(cm.middle_at for cm in comms),
        aliases=[(i0 + i, o0 + o) for cm, (i0, _, o0, _, _, _) in zip(comms, bounds) for i, o in cm.aliases],
    )


def _call(body, args, *, semantics, comm=None, **kw):
    if comm is None:
        return pl.pallas_call(body, compiler_params=_params(*semantics), **kw)(*args)
    grid, in_specs, out_specs, out_shape = kw["grid"], kw["in_specs"], kw["out_specs"], kw["out_shape"]
    scratch = list(kw.get("scratch_shapes", ()))
    single = not isinstance(out_shape, (list, tuple))
    core_specs = [out_specs] if single else list(out_specs)
    core_shapes = [out_shape] if single else list(out_shape)
    n_in, n_out, n_scr = len(in_specs), len(core_shapes), len(scratch)
    n_cin, n_cout = len(comm.inputs), len(comm.out_shapes)
    steps = 1
    for g in grid:
        steps *= g
    middle = min(int(comm.middle_at * steps), steps - 1)

    def hosted(*refs):
        core_in, c_in = refs[:n_in], refs[n_in : n_in + n_cin]
        o0 = n_in + n_cin
        core_out, c_out = refs[o0 : o0 + n_out], refs[o0 + n_out : o0 + n_out + n_cout]
        s0 = o0 + n_out + n_cout
        core_scr, sems = refs[s0 : s0 + n_scr], refs[s0 + n_scr :]
        step = pl.program_id(0)
        for d in range(1, len(grid)):
            step = step * grid[d] + pl.program_id(d)

        @pl.when(step == 0)
        def _():
            comm.start(c_in, c_out, sems)

        body(*core_in, *core_out, *core_scr)

        @pl.when(step == middle)
        def _():
            comm.middle(c_in, c_out, sems)

        @pl.when(step == steps - 1)
        def _():
            comm.finish(c_in, c_out, sems)

    outs = pl.pallas_call(
        hosted,
        name=kw["name"],
        grid=grid,
        in_specs=list(in_specs) + [ANY] * n_cin,
        out_specs=core_specs + [ANY] * n_cout,
        out_shape=core_shapes + comm.out_shapes,
        scratch_shapes=scratch + comm.sem_shapes(),
        input_output_aliases={n_in + i: n_out + o for i, o in comm.aliases},
        compiler_params=_params(*(["arbitrary"] * len(grid))),
    )(*args, *comm.inputs)
    return (outs[0] if single else outs[:n_out]), outs[n_out:]


def _comm_only(comm, *, name):
    n_cin, n_cout = len(comm.inputs), len(comm.out_shapes)

    def body(*refs):
        ins, outs, sems = refs[:n_cin], refs[n_cin : n_cin + n_cout], refs[n_cin + n_cout :]
        comm.start(ins, outs, sems)
        comm.middle(ins, outs, sems)
        comm.finish(ins, outs, sems)

    return pl.pallas_call(
        body,
        name=name,
        in_specs=[ANY] * n_cin,
        out_specs=[ANY] * n_cout,
        out_shape=comm.out_shapes,
        scratch_shapes=comm.sem_shapes(),
    )(*comm.inputs)


def _gather_comm(shards, pass_on_at=1.0):
    n = len(shards)
    per = 7

    def plan(ins, outs, sems):
        send, recv, local = sems
        x, y, c = _position()
        me, sibling = (x, y, c), (x, y, 1 - c)
        chips = _other_chips(x, y)

        def block(t, px, py, pc):
            return outs[t].at[pl.ds(4 * px + 2 * py + pc, 1)]

        def copy(t, k, blk, to, src=None):
            return pltpu.make_async_remote_copy(
                src_ref=block(t, *blk) if src is None else src,
                dst_ref=block(t, *blk),
                send_sem=send.at[t * per + k],
                recv_sem=recv.at[t * per + k],
                device_id=to,
                device_id_type=MESH,
            )

        mine = [pltpu.make_async_copy(ins[t], block(t, *me), local.at[t]) for t in range(n)]
        to_chips = [copy(t, 1 + j, me, (*chip, c), src=ins[t]) for t in range(n) for j, chip in enumerate(chips)]
        to_sibling = [copy(t, 0, me, sibling, src=ins[t]) for t in range(n)]
        from_chips = [copy(t, 1 + j, (*chip, c), me) for t in range(n) for j, chip in enumerate(chips)]
        passed_on = [copy(t, 4 + j, (*chip, c), sibling) for t in range(n) for j, chip in enumerate(chips)]
        from_sibling = [copy(t, 0, sibling, me) for t in range(n)]
        from_sibling += [copy(t, 4 + j, (*chip, 1 - c), me) for t in range(n) for j, chip in enumerate(chips)]
        return mine, to_chips, to_sibling, from_chips, passed_on, from_sibling

    def start(ins, outs, sems):
        mine, to_chips, to_sibling, _, _, _ = plan(ins, outs, sems)
        for cp in mine + to_chips + to_sibling:
            cp.start()

    def middle(ins, outs, sems):
        _, _, _, from_chips, passed_on, _ = plan(ins, outs, sems)
        for arrived, onward in zip(from_chips, passed_on):
            arrived.wait_recv()
            onward.start()

    def finish(ins, outs, sems):
        mine, to_chips, to_sibling, _, passed_on, from_sibling = plan(ins, outs, sems)
        for cp in from_sibling:
            cp.wait_recv()
        for cp in to_chips + to_sibling + passed_on:
            cp.wait_send()
        for cp in mine:
            cp.wait()

    out_shapes = [SDS((N_DEV,) + sh.shape[1:], sh.dtype) for sh in shards]
    return _Comm(shards, out_shapes, [n * per, n * per, n], start, middle, finish, middle_at=pass_on_at)


def _gather_halves(shards=None, arrived=None):
    first_half = arrived is None
    arrays = shards if first_half else arrived
    n = len(arrays)
    per = 4 if first_half else 3

    def plan(ins, outs, sems):
        x, y, c = _position()
        me, sibling = (x, y, c), (x, y, 1 - c)
        chips = _other_chips(x, y)

        def block(t, px, py, pc):
            return outs[t].at[pl.ds(4 * px + 2 * py + pc, 1)]

        def copy(t, k, blk, to, src=None):
            return pltpu.make_async_remote_copy(
                src_ref=block(t, *blk) if src is None else src,
                dst_ref=block(t, *blk),
                send_sem=sems[0].at[t * per + k],
                recv_sem=sems[1].at[t * per + k],
                device_id=to,
                device_id_type=MESH,
            )

        if first_half:
            local = [pltpu.make_async_copy(ins[t], block(t, *me), sems[2].at[t]) for t in range(n)]
            sent = [copy(t, 1 + j, me, (*chip, c), src=ins[t]) for t in range(n) for j, chip in enumerate(chips)]
            sent += [copy(t, 0, me, sibling, src=ins[t]) for t in range(n)]
            landing = [copy(t, 1 + j, (*chip, c), me) for t in range(n) for j, chip in enumerate(chips)]
            landing += [copy(t, 0, sibling, me) for t in range(n)]
        else:
            local = []
            sent = [copy(t, j, (*chip, c), sibling) for t in range(n) for j, chip in enumerate(chips)]
            landing = [copy(t, j, (*chip, 1 - c), me) for t in range(n) for j, chip in enumerate(chips)]
        return local, sent, landing

    def start(ins, outs, sems):
        local, sent, _ = plan(ins, outs, sems)
        for cp in local + sent:
            cp.start()

    def middle(ins, outs, sems):
        pass

    def finish(ins, outs, sems):
        local, sent, landing = plan(ins, outs, sems)
        for cp in landing:
            cp.wait_recv()
        for cp in sent:
            cp.wait_send()
        for cp in local:
            cp.wait()

    if first_half:
        out_shapes = [SDS((N_DEV,) + sh.shape[1:], sh.dtype) for sh in shards]
        return _Comm(shards, out_shapes, [n * per, n * per, n], start, middle, finish)
    out_shapes = [SDS(a.shape, a.dtype) for a in arrived]
    return _Comm(arrived, out_shapes, [n * per, n * per], start, middle, finish, aliases=[(t, t) for t in range(n)])


def _exchange_comm(arrays, out_shapes, n_copies, copies_of):
    def start(ins, outs, sems):
        for cp in copies_of(ins, outs, *sems):
            cp.start()

    def middle(ins, outs, sems):
        pass

    def finish(ins, outs, sems):
        for cp in copies_of(ins, outs, *sems):
            cp.wait()

    return _Comm(arrays, out_shapes, [n_copies, n_copies], start, middle, finish)


def _sibling_comm(grads):
    def copies_of(ins, outs, send, recv):
        x, y, c = _position()
        return [
            pltpu.make_async_remote_copy(
                src_ref=ins[t].at[:, pl.ds(1 - c, 1)],
                dst_ref=outs[t],
                send_sem=send.at[t],
                recv_sem=recv.at[t],
                device_id=(x, y, 1 - c),
                device_id_type=MESH,
            )
            for t in range(len(ins))
        ]

    return _exchange_comm(grads, [SDS((4, 1) + g.shape[2:], g.dtype) for g in grads], len(grads), copies_of)


def _chips_comm(parts):
    def copies_of(ins, outs, send, recv):
        x, y, c = _position()
        return [
            pltpu.make_async_remote_copy(
                src_ref=ins[t].at[pl.ds(2 * px + py, 1)],
                dst_ref=outs[t].at[pl.ds(k, 1)],
                send_sem=send.at[3 * t + k],
                recv_sem=recv.at[3 * t + k],
                device_id=(px, py, c),
                device_id_type=MESH,
            )
            for t in range(len(ins))
            for k, (px, py) in enumerate(_other_chips(x, y))
        ]

    return _exchange_comm(parts, [SDS((3,) + p.shape[1:], p.dtype) for p in parts], 3 * len(parts), copies_of)


def _sum_with_sibling(grad, got, core, *, name):
    rows = grad.shape[2]

    def body(core_ref, a_ref, b_ref, o_ref):
        o_ref[...] = (a_ref[...].astype(F32) + b_ref[...].astype(F32)).astype(o_ref.dtype)

    return pl.pallas_call(
        body,
        name=name,
        grid_spec=pltpu.PrefetchScalarGridSpec(
            num_scalar_prefetch=1,
            grid=(4,),
            in_specs=[
                pl.BlockSpec((None, None, rows, D), lambda q, core_ref: (q, core_ref[0], 0, 0)),
                pl.BlockSpec((None, None, rows, D), lambda q, core_ref: (q, 0, 0, 0)),
            ],
            out_specs=pl.BlockSpec((None, rows, D), lambda q, core_ref: (q, 0, 0)),
        ),
        out_shape=SDS((4, rows, D), grad.dtype),
        compiler_params=_params("parallel"),
    )(core, grad, got)


def _sum_chips(part, got, chip, *, name):
    rows = part.shape[1]

    def body(chip_ref, a_ref, b_ref, o_ref):
        o_ref[...] = ((a_ref[...].astype(F32) + b_ref[0].astype(F32)) + b_ref[1].astype(F32)) + b_ref[2].astype(F32)

    return pl.pallas_call(
        body,
        name=name,
        grid_spec=pltpu.PrefetchScalarGridSpec(
            num_scalar_prefetch=1,
            grid=(1,),
            in_specs=[
                pl.BlockSpec((None, rows, D), lambda i, chip_ref: (chip_ref[0], 0, 0)),
                pl.BlockSpec((3, rows, D), lambda i, chip_ref: (0, 0, 0)),
            ],
            out_specs=pl.BlockSpec((rows, D), lambda i, chip_ref: (0, 0)),
        ),
        out_shape=SDS((rows, D), F32),
        compiler_params=_params("arbitrary"),
    )(chip, part, got)


def _all_reduce_small(pack, *, name):
    rows = pack.shape[1]
    relations = [(kx, ky, kc) for kx in (0, 1) for ky in (0, 1) for kc in (0, 1)][1:]

    def body(in_ref, out_ref, landed, send1, recv1, send2, recv2):
        x, y, c = _position()
        mine = 4 * x + 2 * y + c

        def peer(rel):
            kx, ky, kc = rel
            return (1 - x if kx else x, 1 - y if ky else y, 1 - c if kc else c)

        first = []
        for k, rel in enumerate(relations):
            px, py, pc = peer(rel)
            cp = pltpu.make_async_remote_copy(
                src_ref=in_ref.at[4 * px + 2 * py + pc],
                dst_ref=landed.at[k],
                send_sem=send1.at[k],
                recv_sem=recv1.at[k],
                device_id=(px, py, pc),
                device_id_type=MESH,
            )
            cp.start()
            first.append(cp)
        total = in_ref[mine]
        for k, cp in enumerate(first):
            cp.wait_recv()
            total = total + landed[k]
        out_ref[mine] = total
        second = []
        for k, rel in enumerate(relations):
            cp = pltpu.make_async_remote_copy(
                src_ref=out_ref.at[mine],
                dst_ref=out_ref.at[mine],
                send_sem=send2.at[k],
                recv_sem=recv2.at[k],
                device_id=peer(rel),
                device_id_type=MESH,
            )
            cp.start()
            second.append(cp)
        for k, rel in enumerate(relations):
            px, py, pc = peer(rel)
            got = out_ref.at[4 * px + 2 * py + pc]
            pltpu.make_async_remote_copy(
                src_ref=got, dst_ref=got, send_sem=send2.at[k], recv_sem=recv2.at[k], device_id=peer(rel), device_id_type=MESH
            ).wait_recv()
        for cp in first + second:
            cp.wait_send()

    vmem = pl.BlockSpec(memory_space=pltpu.VMEM)
    return pl.pallas_call(
        body,
        name=name,
        in_specs=[vmem],
        out_specs=vmem,
        out_shape=SDS(pack.shape, F32),
        scratch_shapes=[
            pltpu.VMEM((7, rows, D), F32),
            pltpu.SemaphoreType.DMA((7,)),
            pltpu.SemaphoreType.DMA((7,)),
            pltpu.SemaphoreType.DMA((7,)),
            pltpu.SemaphoreType.DMA((7,)),
        ],
        compiler_params=pltpu.CompilerParams(vmem_limit_bytes=VMEM_LIMIT_BYTES),
    )(pack)


def _pack(arrays, rows):
    flat = jnp.concatenate([a.reshape(-1).astype(F32) for a in arrays])
    return jnp.pad(flat, (0, rows * D - flat.shape[0])).reshape(rows, D)


def _unpack(pack, shapes):
    flat = pack.reshape(-1)
    out, off = [], 0
    for sh in shapes:
        size = 1
        for dim in sh:
            size *= dim
        out.append(flat[off : off + size].reshape(sh))
        off += size
    return out


def _block_diag_pairs(w):
    w = w.reshape(N_RNN_TILES, 2, HEAD_DIM, HEAD_DIM)
    z = jnp.zeros_like(w[:, 0])
    top = jnp.concatenate([w[:, 0], z], axis=2)
    bot = jnp.concatenate([z, w[:, 1]], axis=2)
    return jnp.concatenate([top, bot], axis=1)


def _diag_blocks(w2):
    a = w2[:, :HEAD_DIM, :HEAD_DIM]
    b = w2[:, HEAD_DIM:, HEAD_DIM:]
    return jnp.stack([a, b], axis=1).reshape(RNN_HEADS, HEAD_DIM, HEAD_DIM)


BIG = ("w_in", "w_branch_a", "w_branch_b", "w_out", "w_up", "w_down")
TRANSPOSED = ("w_in", "w_up")
SMALL = (
    "norm_mix_g", "conv_w", "conv_b", "lru_w_a", "lru_b_a", "lru_w_x", "lru_b_x", "lru_lambda",
    "sgu_ln_g", "sgu_ln_b", "sgu_w_s", "sgu_b_s", "norm_ffn_g", "final_norm_g",
)
WEIGHTS = (
    "norm_mix_g", "w_in", "conv_w", "conv_b", "lru_w_a", "lru_b_a", "lru_w_x", "lru_b_x", "lru_lambda", "sgu_ln_g",
    "sgu_ln_b", "sgu_w_s", "sgu_b_s", "w_branch_a", "w_branch_b", "w_out", "norm_ffn_g", "w_up", "w_down", "final_norm_g",
)

TM = 512
TM_NT = 1024
TN_IN = 1664
TN_UP = 2048
TKA = 512
TKA_PIECES = 256
TC = 512
TB = 256
TR = 256


_BRANCH_WEIGHTS = ("w_branch_a", "w_branch_b", "w_out")
GATHERS_RIDING = (
    {
        "in_proj": ([(0, name) for name in _BRANCH_WEIGHTS] + [(0, "w_up")], []),
        "branch_a_fwd": ([(1, "w_in")], [(0, name) for name in _BRANCH_WEIGHTS] + [(0, "w_up")]),
        "sgu_fwd": ([], [(1, "w_in")]),
        "merge_fwd": ([(0, "w_down")], []),
        "ffn_up": ([(1, name) for name in _BRANCH_WEIGHTS], [(0, "w_down")]),
        "ffn_down": ([(1, "w_up")], [(1, name) for name in _BRANCH_WEIGHTS]),
    },
    {"in_proj": ([(1, "w_down")], [(1, "w_up")]), "branch_a_fwd": ([], [(1, "w_down")])},
)


def _layer_forward(l, x, p, w, shards, arriving):
    def run(key, fn, *args, **kw):
        first, second = GATHERS_RIDING[l].get(key, ((), ()))
        comms = []
        if first:
            comms.append(_gather_halves(shards=[shards[l2][n2] for l2, n2 in first]))
        if second:
            comms.append(_gather_halves(arrived=[arriving.pop(k) for k in second]))
        if not comms:
            return fn(*args, **kw)
        out, got = fn(*args, comm=_merge_comms(comms), **kw)
        arriving.update(zip(first, got[: len(first)]))
        for (l2, n2), full in zip(second, got[len(first) :]):
            w[l2][n2] = full.reshape(-1, D)
        return out

    proj, h = run("in_proj", _norm_matmul_nt, x, p["norm_mix_g"], w[l]["w_in"], tm=TM_NT, tn=TN_IN, name=f"in_proj_{l}")
    hseq, ya_pre = run(
        "branch_a_fwd", _branch_a_fwd, proj, p["conv_w"], p["conv_b"], p["wa2"], p["lru_b_a"], p["wx2"], p["lru_b_x"],
        p["lru_lambda"], tc=TC, name=f"branch_a_fwd_{l}",
    )
    yb_pre = run("sgu_fwd", _sgu_fwd, proj, p["sgu_ln_g"], p["sgu_ln_b"], p["wm"], p["sgu_bias"], tb=TB, name=f"sgu_fwd_{l}")
    x1, ya, yb = run(
        "merge_fwd", _merge_fwd, ya_pre, yb_pre, proj, x, w[l]["w_branch_a"], w[l]["w_branch_b"], w[l]["w_out"], tm=TM,
        name=f"merge_fwd_{l}",
    )
    f_pre, h2 = run("ffn_up", _norm_matmul_nt, x1, p["norm_ffn_g"], w[l]["w_up"], tm=TM_NT, tn=TN_UP, name=f"ffn_up_{l}")
    x2 = run("ffn_down", _matmul_nn_res, f_pre, w[l]["w_down"], x1, relu2=True, tm=TM, name=f"ffn_down_{l}")
    saved = dict(x=x, h=h, proj=proj, hseq=hseq, ya_pre=ya_pre, yb_pre=yb_pre, ya=ya, yb=yb, x1=x1, h2=h2, f_pre=f_pre)
    return x2, saved


def _layer_backward(l, dx2, dx2b, sv, p, w, core, waiting, last):
    parts, from_chips = {}, {}

    def by_device(g):
        return g.reshape(4, 2, -1, D)

    def with_sibling(name, g, got):
        parts[name] = _sum_with_sibling(by_device(g), got, core, name=f"sum_sibling_{name}_{l}")

    df_pre = _matmul_nt_drelu2(dx2b, w["w_down"], sv["f_pre"], tm=TM_NT, tn=TN_UP, name=f"ffn_down_bwd_{l}")
    g_down = _matmul_tn([sv["f_pre"]], dx2b, relu2=True, tka=TKA, name=f"grad_w_down_{l}")
    g_up, (got,) = _matmul_tn(
        [df_pre], sv["h2"], relu2=False, tka=TKA, name=f"grad_w_up_{l}", comm=_sibling_comm([by_device(g_down)])
    )
    with_sibling("w_down", g_down, got)
    (dx1, dx1b, g_norm_ffn), (got, from_chips[l, "w_down"]) = _matmul_nn_rmsnorm_bwd(
        [df_pre], w["w_up"], sv["x1"], p["norm_ffn_g"], dx2, tm=TM, name=f"ffn_up_bwd_{l}",
        comm=_merge_comms([_sibling_comm([by_device(g_up)]), _chips_comm([parts["w_down"]])]),
    )
    with_sibling("w_up", g_up, got)
    (merged, dya, dyb, dga, dgb, dya_pre, dyb_pre), (from_chips[l, "w_up"],) = _merge_bwd(
        dx1b, sv["ya"], sv["yb"], sv["proj"], w["w_branch_a"], w["w_branch_b"], w["w_out"], tm=TM, name=f"merge_bwd_{l}",
        comm=_chips_comm([parts["w_up"]]),
    )
    g_out = _matmul_tn([merged], dx1b, relu2=False, tka=TKA, name=f"grad_w_out_{l}")
    g_ba = _matmul_tn([sv["ya_pre"]], dya, relu2=False, tka=TKA_PIECES, name=f"grad_w_branch_a_{l}")
    g_bb = _matmul_tn([sv["yb_pre"]], dyb, relu2=False, tka=TKA, name=f"grad_w_branch_b_{l}")
    branch = (("w_out", g_out), ("w_branch_a", g_ba), ("w_branch_b", g_bb))
    (du, dv, g_ws, g_bs, g_lng, g_lnb), got = _sgu_bwd(
        dyb_pre, sv["proj"], p["sgu_ln_g"], p["sgu_ln_b"], p["wm"], p["wmt"], p["sgu_bias"], p["mask"], tb=TB,
        name=f"sgu_bwd_{l}", comm=_sibling_comm([by_device(g) for _, g in branch]),
    )
    for (name, g), landed in zip(branch, got):
        with_sibling(name, g, landed)
    riding = [((l, name), parts[name]) for name, _ in branch] + list(waiting)
    (dxr, dgr, g_cw, g_cb, g_ba_, g_bx, g_lam, g_wa2, g_wx2), got = _branch_a_bwd(
        dya_pre, sv["proj"], sv["hseq"], p["conv_w"], p["conv_b"], p["wa2"], p["lru_b_a"], p["wx2"], p["lru_b_x"],
        p["lru_lambda"], p["wa2t"], p["wx2t"], tc=TC, name=f"branch_a_bwd_{l}", comm=_chips_comm([part for _, part in riding]),
    )
    for (key, _), landed in zip(riding, got):
        from_chips[key] = landed
    dproj = [dxr, dgr, du, dv, dga, dgb]
    g_in = _matmul_tn(dproj, sv["h"], relu2=False, tka=TKA_PIECES, name=f"grad_w_in_{l}")
    if last:
        (got,) = _comm_only(_sibling_comm([by_device(g_in)]), name=f"grad_w_in_to_sibling_{l}")
        with_sibling("w_in", g_in, got)
        riding = _chips_comm([parts["w_in"]])
    else:
        riding = _sibling_comm([by_device(g_in)])
    (dx, dxb, g_norm_mix), (got,) = _matmul_nn_rmsnorm_bwd(
        dproj, w["w_in"], sv["x"], p["norm_mix_g"], dx1, tm=TM, name=f"in_proj_bwd_{l}", comm=riding
    )
    if last:
        from_chips[l, "w_in"] = got
    else:
        with_sibling("w_in", g_in, got)
    small = dict(
        norm_mix_g=g_norm_mix[0], conv_w=g_cw, conv_b=g_cb[0], lru_w_a=_diag_blocks(g_wa2), lru_b_a=g_ba_.reshape(RNN_HEADS, HEAD_DIM),
        lru_w_x=_diag_blocks(g_wx2), lru_b_x=g_bx.reshape(RNN_HEADS, HEAD_DIM), lru_lambda=g_lam[0], sgu_ln_g=g_lng[0],
        sgu_ln_b=g_lnb[0], sgu_w_s=g_ws, sgu_b_s=g_bs[:, :, 0], norm_ffn_g=g_norm_ffn[0],
    )
    return dx, dxb, small, parts, from_chips


def _prepare_small(l, given):
    chunk_id = jnp.arange(SGU_BLOCK) // CHUNK
    mask = (chunk_id[:, None] >= chunk_id[None, :]).astype(F32)
    wm = given["sgu_w_s"][l] * mask
    wa2 = _block_diag_pairs(given["lru_w_a"][l])
    wx2 = _block_diag_pairs(given["lru_w_x"][l])
    row = lambda a: a.reshape(1, -1)
    return dict(
        norm_mix_g=row(given["norm_mix_g"][l]),
        norm_ffn_g=row(given["norm_ffn_g"][l]),
        conv_w=given["conv_w_full"][l],
        conv_b=row(given["conv_b"][l]),
        wa2=wa2.astype(BF16),
        wx2=wx2.astype(BF16),
        wa2t=jnp.swapaxes(wa2, 1, 2).astype(BF16),
        wx2t=jnp.swapaxes(wx2, 1, 2).astype(BF16),
        lru_b_a=row(given["lru_b_a"][l]),
        lru_b_x=row(given["lru_b_x"][l]),
        lru_lambda=row(given["lru_lambda"][l]),
        sgu_ln_g=row(given["sgu_ln_g"][l]),
        sgu_ln_b=row(given["sgu_ln_b"][l]),
        wm=wm.astype(BF16),
        wmt=jnp.swapaxes(wm, 1, 2).astype(BF16),
        sgu_bias=jnp.broadcast_to(given["sgu_b_s"][l][:, :, None], (SGU_GROUPS, SGU_BLOCK, LANES)),
        mask=mask,
    )


def _step(given):
    x_idx, y_idx, c_idx = _position()
    dev = 4 * x_idx + 2 * y_idx + c_idx
    core = c_idx.astype(jnp.int32).reshape(1)
    chip = (2 * x_idx + y_idx).astype(jnp.int32).reshape(1)

    def rows_first(name, a):
        return jnp.swapaxes(a, 1, 2) if name in TRANSPOSED else a

    shards = []
    for l in range(DEPTH):
        shards.append({name: rows_first(name, given[name])[l].astype(BF16)[None] for name in BIG})
    conv_mine = given["conv_w"].reshape(1, DEPTH * CONV_WIDTH, D_RNN // N_DEV)
    w_in_first, conv_all = _comm_only(_gather_comm([shards[0]["w_in"], conv_mine]), name="gather_first")
    weights = [{"w_in": w_in_first.reshape(-1, D)}, {}]
    conv_all = conv_all.reshape(N_DEV, DEPTH, CONV_WIDTH, D_RNN // N_DEV)
    given = dict(given, conv_w_full=jnp.moveaxis(conv_all, 0, 2).reshape(DEPTH, CONV_WIDTH, D_RNN))

    small_params = [_prepare_small(l, given) for l in range(DEPTH)]
    x = given["x"][0]
    saved, arriving = [], {}
    for l in range(DEPTH):
        x, sv = _layer_forward(l, x, small_params[l], weights, shards, arriving)
        saved.append(sv)
    dx, dxb, g_final, loss = _final_loss(x, given["final_norm_g"].reshape(1, D), given["loss_target"][0], tm=TM, name="final_loss")
    small_grads, parts, from_chips, waiting = [None] * DEPTH, [None] * DEPTH, {}, []
    for l in reversed(range(DEPTH)):
        dx, dxb, small_grads[l], parts[l], got = _layer_backward(
            l, dx, dxb, saved[l], small_params[l], weights[l], core, waiting, last=l == 0
        )
        from_chips.update(got)
        waiting = [((l, "w_in"), parts[l]["w_in"])]

    small_list = []
    for name in SMALL[:-1]:
        small_list.append(jnp.stack([small_grads[l][name] for l in range(DEPTH)]))
    small_list += [g_final[0], loss[0, :1]]
    small_shapes = [a.shape for a in small_list]
    pack = _pack(small_list, SMALL_ROWS).reshape(N_DEV, SMALL_ROWS_PER_DEV, D)
    summed = _unpack(_all_reduce_small(pack, name="all_reduce_small"), small_shapes)
    loss_total = summed[-1][0]
    grads = dict(zip(SMALL, summed[:-1]))
    cw = grads["conv_w"].reshape(DEPTH, CONV_WIDTH, N_DEV, D_RNN // N_DEV)
    grads["conv_w"] = lax.dynamic_index_in_dim(cw, dev, axis=2, keepdims=False)

    delta, new_m, new_v = {}, {}, {}
    for name in BIG:
        w, m, v = given[name], given["m_" + name], given["v_" + name]
        mine = [parts[l][name] for l in range(DEPTH)]
        theirs = [from_chips[l, name] for l in range(DEPTH)]
        if name == "w_up":
            sums = [_sum_chips(mine[l], theirs[l], chip, name=f"sum_chips_{name}_{l}").T for l in range(DEPTH)]
            out = _adamw_layers(w, sums, m, v, tr=TR, name=f"adamw_{name}")
        else:
            out = _adamw_reduced(
                rows_first(name, w), mine, theirs, rows_first(name, m), rows_first(name, v), chip, tr=TR, name=f"adamw_{name}"
            )
            out = [rows_first(name, a) for a in out]
        grads[name], delta[name], new_m[name], new_v[name] = out
    two_d = lambda a: a.reshape(1, -1) if a.ndim == 1 else a
    groups = [tuple(two_d(a) for a in (given[n], grads[n], given["m_" + n], given["v_" + n])) for n in SMALL]
    for n, (d, m2, v2) in zip(SMALL, _adamw_small(groups, name="adamw_small")):
        shape = given[n].shape
        delta[n], new_m[n], new_v[n] = d.reshape(shape), m2.reshape(shape), v2.reshape(shape)

    return (
        loss_total, dx[None],
        *[grads[n] for n in WEIGHTS], *[delta[n] for n in WEIGHTS], *[new_m[n] for n in WEIGHTS], *[new_v[n] for n in WEIGHTS],
    )


def kernel(x, norm_mix_g, w_in, conv_w, conv_b, lru_w_a, lru_b_a, lru_w_x, lru_b_x, lru_lambda, sgu_ln_g, sgu_ln_b, sgu_w_s, sgu_b_s, w_branch_a, w_branch_b, w_out, norm_ffn_g, w_up, w_down, final_norm_g, loss_target, m_norm_mix_g, m_w_in, m_conv_w, m_conv_b, m_lru_w_a, m_lru_b_a, m_lru_w_x, m_lru_b_x, m_lru_lambda, m_sgu_ln_g, m_sgu_ln_b, m_sgu_w_s, m_sgu_b_s, m_w_branch_a, m_w_branch_b, m_w_out, m_norm_ffn_g, m_w_up, m_w_down, m_final_norm_g, v_norm_mix_g, v_w_in, v_conv_w, v_conv_b, v_lru_w_a, v_lru_b_a, v_lru_w_x, v_lru_b_x, v_lru_lambda, v_sgu_ln_g, v_sgu_ln_b, v_sgu_w_s, v_sgu_b_s, v_w_branch_a, v_w_branch_b, v_w_out, v_norm_ffn_g, v_w_up, v_w_down, v_final_norm_g):
    return _step(dict(locals()))
```

```python
import jax
import jax.numpy as jnp
from jax import lax
from jax.experimental import pallas as pl
from jax.experimental.pallas import tpu as pltpu

F32 = jnp.float32
BF16 = jnp.bfloat16
SDS = jax.ShapeDtypeStruct
MESH = pl.DeviceIdType.MESH

D = 1024
D_RNN = 1280
D_SGU = 1024
D_FF = 4096
D_IN = 2 * D_RNN + 2 * D_SGU + 2 * D
DEPTH = 2
RNN_HEADS = 20
HEAD_DIM = 64
CONV_WIDTH = 4
LRU_C = 8.0
SGU_GROUPS = 8
SGU_BLOCK = 128
CHUNK = 64
EPS = 1e-6
N_DEV = 8

ADAM_LR = 0.001
ADAM_B1 = 0.9
ADAM_B2 = 0.999
ADAM_EPS = 1e-08
ADAM_WD = 0.01
ADAM_STEP = 10

LANES = 128
SUBLANES = 8
VMEM_LIMIT_BYTES = 56 * 1024 * 1024

N_RNN_TILES = D_RNN // LANES
RNN_TILES_PER_STEP = 5
GRNN_BLK128 = D_RNN // LANES
U_BLK512 = (2 * D_RNN) // 512
V_BLK512 = (2 * D_RNN + D_SGU) // 512
GA_BLK512 = (2 * D_RNN + 2 * D_SGU) // 512
GB_BLK512 = (2 * D_RNN + 2 * D_SGU + D) // 512

SMALL_ROWS_PER_DEV = 80
SMALL_ROWS = N_DEV * SMALL_ROWS_PER_DEV


def _params(*sem):
    return pltpu.CompilerParams(dimension_semantics=sem, vmem_limit_bytes=VMEM_LIMIT_BYTES)


def _sigmoid(x):
    return 0.5 + 0.5 * jnp.tanh(0.5 * x)


_GELU_C = 0.7978845608028654
_GELU_K = 0.044715


def _gelu(x):
    t = jnp.tanh(_GELU_C * (x + _GELU_K * x * x * x))
    return 0.5 * x * (1.0 + t)


def _gelu_and_grad(x):
    t = jnp.tanh(_GELU_C * (x + _GELU_K * x * x * x))
    val = 0.5 * x * (1.0 + t)
    grad = 0.5 * (1.0 + t) + 0.5 * x * (1.0 - t * t) * _GELU_C * (1.0 + 3.0 * _GELU_K * x * x)
    return val, grad


def _one_minus_square(log_a, a):
    return -jnp.tanh(log_a) * (1.0 + a * a)


def _dot(a, b):
    return jnp.dot(a, b, preferred_element_type=F32)


def _dot_nt(a, b):
    return lax.dot_general(a, b, (((1,), (1,)), ((), ())), preferred_element_type=F32)


def _dot_tn(a, b):
    return lax.dot_general(a, b, (((0,), (0,)), ((), ())), preferred_element_type=F32)


def _norm_matmul_nt(x, g, w, *, tm, tn, name, comm=None):
    s, n = x.shape[0], w.shape[0]
    tm, tn = min(tm, s), min(tn, n)

    def body(x_ref, g_ref, w_ref, o_ref, h_ref):
        @pl.when(pl.program_id(1) == 0)
        def _():
            xv = x_ref[...]
            r = lax.rsqrt(jnp.mean(xv * xv, axis=-1, keepdims=True) + EPS)
            h_ref[...] = (xv * r * g_ref[...]).astype(BF16)

        o_ref[...] = _dot_nt(h_ref[...], w_ref[...]).astype(o_ref.dtype)

    return _call(
        body,
        (x, g, w),
        name=name,
        grid=(s // tm, n // tn),
        in_specs=[
            pl.BlockSpec((tm, D), lambda i, j: (i, 0)),
            pl.BlockSpec((1, D), lambda i, j: (0, 0)),
            pl.BlockSpec((tn, D), lambda i, j: (j, 0)),
        ],
        out_specs=[pl.BlockSpec((tm, tn), lambda i, j: (i, j)), pl.BlockSpec((tm, D), lambda i, j: (i, 0))],
        out_shape=[SDS((s, n), BF16), SDS((s, D), BF16)],
        semantics=("parallel", "arbitrary"),
        comm=comm,
    )


def _matmul_nn_res(a, w, res, *, relu2, tm, name, comm=None):
    s, k = a.shape
    tm = min(tm, s)

    def body(a_ref, w_ref, r_ref, o_ref):
        av = a_ref[...]
        if relu2:
            t = jnp.maximum(av.astype(F32), 0.0)
            av = (t * t).astype(BF16)
        o_ref[...] = r_ref[...] + _dot(av, w_ref[...])

    return _call(
        body,
        (a, w, res),
        name=name,
        grid=(s // tm,),
        in_specs=[
            pl.BlockSpec((tm, k), lambda i: (i, 0)),
            pl.BlockSpec((k, D), lambda i: (0, 0)),
            pl.BlockSpec((tm, D), lambda i: (i, 0)),
        ],
        out_specs=pl.BlockSpec((tm, D), lambda i: (i, 0)),
        out_shape=SDS((s, D), F32),
        semantics=("parallel",),
        comm=comm,
    )


def _matmul_nt_drelu2(a, w, pre, *, tm, tn, name):
    s, n = a.shape[0], w.shape[0]
    tm, tn = min(tm, s), min(tn, n)

    def body(a_ref, w_ref, p_ref, o_ref):
        d = _dot_nt(a_ref[...], w_ref[...])
        o_ref[...] = (d * (2.0 * jnp.maximum(p_ref[...].astype(F32), 0.0))).astype(o_ref.dtype)

    return pl.pallas_call(
        body,
        name=name,
        grid=(s // tm, n // tn),
        in_specs=[
            pl.BlockSpec((tm, D), lambda i, j: (i, 0)),
            pl.BlockSpec((tn, D), lambda i, j: (j, 0)),
            pl.BlockSpec((tm, tn), lambda i, j: (i, j)),
        ],
        out_specs=pl.BlockSpec((tm, tn), lambda i, j: (i, j)),
        out_shape=SDS((s, n), BF16),
        compiler_params=_params("parallel", "arbitrary"),
    )(a, w, pre)


def _matmul_tn(a_list, b, *, relu2, tka, name, comm=None):
    s = b.shape[0]
    n = len(a_list)
    nblk = [a.shape[1] // tka for a in a_list]
    starts = [sum(nblk[:p]) for p in range(n)]

    def body(*refs):
        a_refs, b_ref, o_ref = refs[:n], refs[n], refs[n + 1]
        i = pl.program_id(0)
        for p in range(n):

            @pl.when((i >= starts[p]) & (i < starts[p] + nblk[p]))
            def _(p=p):
                av = a_refs[p][...]
                if relu2:
                    t = jnp.maximum(av.astype(F32), 0.0)
                    av = (t * t).astype(BF16)
                o_ref[...] = _dot_tn(av, b_ref[...]).astype(o_ref.dtype)

    def piece_spec(p):
        return pl.BlockSpec((s, tka), lambda i: (0, jnp.clip(i - starts[p], 0, nblk[p] - 1)))

    return _call(
        body,
        (*a_list, b),
        name=name,
        grid=(sum(nblk),),
        in_specs=[piece_spec(p) for p in range(n)] + [pl.BlockSpec((s, D), lambda i: (0, 0))],
        out_specs=pl.BlockSpec((tka, D), lambda i: (i, 0)),
        out_shape=SDS((sum(nblk) * tka, D), BF16),
        semantics=("parallel",),
        comm=comm,
    )


def _matmul_nn_rmsnorm_bwd(a_list, w, x, g, res, *, tm, name, comm=None):
    s = x.shape[0]
    tm = min(tm, s)
    n = len(a_list)
    widths = [a.shape[1] for a in a_list]
    offs = [sum(widths[:p]) for p in range(n)]
    k = sum(widths)

    def body(*refs):
        a_refs = refs[:n]
        w_ref, x_ref, g_ref, r_ref, dx_ref, dxb_ref, dg_ref = refs[n:]

        @pl.when(pl.program_id(0) == 0)
        def _():
            dg_ref[...] = jnp.zeros_like(dg_ref)

        dh = _dot(a_refs[0][...], w_ref[0 : widths[0], :])
        for p in range(1, n):
            dh += _dot(a_refs[p][...], w_ref[offs[p] : offs[p] + widths[p], :])
        xv = x_ref[...]
        r = lax.rsqrt(jnp.mean(xv * xv, axis=-1, keepdims=True) + EPS)
        xhat = xv * r
        dxh = dh * g_ref[...]
        dx = r_ref[...] + r * (dxh - xhat * jnp.mean(dxh * xhat, axis=-1, keepdims=True))
        dx_ref[...] = dx
        dxb_ref[...] = dx.astype(BF16)
        dg_ref[...] += jnp.sum(dh * xhat, axis=0, keepdims=True)

    act = pl.BlockSpec((tm, D), lambda i: (i, 0))
    vec = pl.BlockSpec((1, D), lambda i: (0, 0))
    return _call(
        body,
        (*a_list, w, x, g, res),
        name=name,
        grid=(s // tm,),
        in_specs=[pl.BlockSpec((tm, wd), lambda i: (i, 0)) for wd in widths]
        + [pl.BlockSpec((k, D), lambda i: (0, 0), pipeline_mode=pl.Buffered(1)), act, vec, act],
        out_specs=[act, act, vec],
        out_shape=[SDS((s, D), F32), SDS((s, D), BF16), SDS((1, D), F32)],
        semantics=("arbitrary",),
        comm=comm,
    )


def _rows_before(ext, k):
    if k == 0:
        return ext[SUBLANES:, :]
    return pltpu.roll(ext, k, 0)[SUBLANES:, :]


def _rows_after(ext, k, n):
    if k == 0:
        return ext[:n, :]
    return pltpu.roll(ext, n + SUBLANES - k, 0)[:n, :]


def _scan_forward(a, b, n):
    row = lax.broadcasted_iota(jnp.int32, a.shape, 0)
    d = 1
    while d < n:
        if d < SUBLANES:
            m = row >= d
            a_s = jnp.where(m, pltpu.roll(a, d, 0), 1.0)
            b_s = jnp.where(m, pltpu.roll(b, d, 0), 0.0)
            b = a * b_s + b
            a = a * a_s
        else:
            b = jnp.concatenate([b[:d], a[d:] * b[: n - d] + b[d:]], axis=0)
            a = jnp.concatenate([a[:d], a[d:] * a[: n - d]], axis=0)
        d *= 2
    return a, b


def _scan_backward(a, b, n):
    row = lax.broadcasted_iota(jnp.int32, a.shape, 0)
    d = 1
    while d < n:
        if d < SUBLANES:
            m = row < n - d
            a_s = jnp.where(m, pltpu.roll(a, n - d, 0), 1.0)
            b_s = jnp.where(m, pltpu.roll(b, n - d, 0), 0.0)
            b = a * b_s + b
            a = a * a_s
        else:
            b = jnp.concatenate([a[: n - d] * b[d:] + b[: n - d], b[n - d :]], axis=0)
            a = jnp.concatenate([a[: n - d] * a[d:], a[n - d :]], axis=0)
        d *= 2
    return b


def _repeat_matrix(n):
    groups = n // SUBLANES
    return (jnp.arange(n)[:, None] // SUBLANES == jnp.arange(3 * groups)[None, :] % groups).astype(BF16)


def _scan_rows(a, b, n, repeat_ref, a_scr, b_scr, reverse):
    groups = n // SUBLANES
    a3 = a.reshape(groups, SUBLANES, LANES)
    b3 = b.reshape(groups, SUBLANES, LANES)
    sub = lax.broadcasted_iota(jnp.int32, a3.shape, 1)
    for d in (1, 2, 4):
        m = (sub < SUBLANES - d) if reverse else (sub >= d)
        shift = SUBLANES - d if reverse else d
        a_s = jnp.where(m, pltpu.roll(a3, shift, 1), 1.0)
        b_s = jnp.where(m, pltpu.roll(b3, shift, 1), 0.0)
        b3 = a3 * b_s + b3
        a3 = a3 * a_s
    a_scr[...] = a3.reshape(n, LANES)
    b_scr[...] = b3.reshape(n, LANES)
    edge = 0 if reverse else SUBLANES - 1
    a_tot = a_scr[pl.ds(edge, groups, stride=SUBLANES), :]
    b_tot = b_scr[pl.ds(edge, groups, stride=SUBLANES), :]
    row = lax.broadcasted_iota(jnp.int32, a_tot.shape, 0)
    if reverse:
        through = _scan_backward(a_tot, b_tot, groups)
        entering = jnp.where(row < groups - 1, pltpu.roll(through, groups - 1, 0), 0.0)
    else:
        _, through = _scan_forward(a_tot, b_tot, groups)
        entering = jnp.where(row >= 1, pltpu.roll(through, 1, 0), 0.0)
    hi = entering.astype(BF16)
    rest = entering - hi.astype(F32)
    mid = rest.astype(BF16)
    lo = (rest - mid.astype(F32)).astype(BF16)
    repeated = _dot(repeat_ref[...], jnp.concatenate([hi, mid, lo], axis=0))
    return b_scr[...] + a_scr[...] * repeated


def _softplus_neg(lam):
    z = -lam
    return jnp.maximum(z, 0.0) + jnp.log1p(jnp.exp(-jnp.abs(z)))


def _conv_and_gates(xc, xprev, cw_ref, cb_ref, wa_ref, ba_ref, wx_ref, bx_ref, lam_ref):
    ext = jnp.concatenate([xprev, xc], axis=0)
    x1, x2, x3 = _rows_before(ext, 1), _rows_before(ext, 2), _rows_before(ext, 3)
    xr = cb_ref[...] + x3 * cw_ref[0:1, :] + x2 * cw_ref[1:2, :] + x1 * cw_ref[2:3, :] + xc * cw_ref[3:4, :]
    xrb = xr.astype(BF16)
    r = _sigmoid(_dot(xrb, wa_ref[...]) + ba_ref[...])
    i = _sigmoid(_dot(xrb, wx_ref[...]) + bx_ref[...])
    sp = _softplus_neg(lam_ref[...])
    log_a = (-LRU_C * r) * sp
    a = jnp.exp(log_a)
    return xr, (x1, x2, x3), r, i, a, _one_minus_square(log_a, a)


def _branch_a_fwd(proj, cw, cb, wa2, ba, wx2, bx, lam, *, tc, name, comm=None):
    s = proj.shape[0]
    tc = min(tc, s)

    def body(x_ref, g_ref, cw_ref, cb_ref, wa_ref, ba_ref, wx_ref, bx_ref, lam_ref, rep_ref, h_ref, y_ref,
             xprev, hlast, a_scr, b_scr):
        @pl.when(pl.program_id(1) == 0)
        def _():
            xprev[...] = jnp.zeros_like(xprev)
            hlast[...] = jnp.zeros_like(hlast)

        for t in range(RNN_TILES_PER_STEP):
            cols = lambda ref: ref.at[:, pl.ds(t * LANES, LANES)]
            one_tile(
                cols(x_ref), cols(g_ref), cols(cw_ref), cols(cb_ref), wa_ref.at[t], cols(ba_ref), wx_ref.at[t], cols(bx_ref),
                cols(lam_ref), rep_ref, cols(h_ref), cols(y_ref), cols(xprev), cols(hlast), a_scr.at[t], b_scr.at[t],
            )

    def one_tile(x_ref, g_ref, cw_ref, cb_ref, wa_ref, ba_ref, wx_ref, bx_ref, lam_ref, rep_ref, h_ref, y_ref,
                 xprev, hlast, a_scr, b_scr):
        xc = x_ref[...].astype(F32)
        xr, _, r, i, a, om = _conv_and_gates(xc, xprev[...], cw_ref, cb_ref, wa_ref, ba_ref, wx_ref, bx_ref, lam_ref)
        xprev[...] = xc[tc - SUBLANES :, :]
        u = jnp.sqrt(om) * (i * xr)
        row8 = lax.broadcasted_iota(jnp.int32, (SUBLANES, LANES), 0)
        first = u[:SUBLANES] + jnp.where(row8 == 0, a[:SUBLANES] * hlast[SUBLANES - 1 : SUBLANES, :], 0.0)
        h = _scan_rows(a, jnp.concatenate([first, u[SUBLANES:]], axis=0), tc, rep_ref, a_scr, b_scr, reverse=False)
        hlast[...] = h[tc - SUBLANES :, :]
        h_ref[...] = h
        y_ref[...] = (h * _gelu(g_ref[...].astype(F32))).astype(BF16)

    wide = RNN_TILES_PER_STEP * LANES
    tile = lambda j, c: (0, j)
    vec = pl.BlockSpec((1, wide), tile)
    mats = pl.BlockSpec((RNN_TILES_PER_STEP, LANES, LANES), lambda j, c: (j, 0, 0))
    repeat = _repeat_matrix(tc)
    return _call(
        body,
        (proj, proj, cw, cb, wa2, ba, wx2, bx, lam, repeat),
        name=name,
        grid=(N_RNN_TILES // RNN_TILES_PER_STEP, s // tc),
        in_specs=[
            pl.BlockSpec((tc, wide), lambda j, c: (c, j)),
            pl.BlockSpec((tc, wide), lambda j, c: (c, D_RNN // wide + j)),
            pl.BlockSpec((CONV_WIDTH, wide), tile),
            vec,
            mats,
            vec,
            mats,
            vec,
            vec,
            pl.BlockSpec(repeat.shape, lambda j, c: (0, 0)),
        ],
        out_specs=[pl.BlockSpec((tc, wide), lambda j, c: (c, j)), pl.BlockSpec((tc, wide), lambda j, c: (c, j))],
        out_shape=[SDS((s, D_RNN), F32), SDS((s, D_RNN), BF16)],
        scratch_shapes=[pltpu.VMEM((SUBLANES, wide), F32)] * 2 + [pltpu.VMEM((RNN_TILES_PER_STEP, tc, LANES), F32)] * 2,
        semantics=("parallel", "arbitrary"),
        comm=comm,
    )


def _branch_a_bwd(dy, proj, h, cw, cb, wa2, ba, wx2, bx, lam, wa2t, wx2t, *, tc, name, comm=None):
    s = proj.shape[0]
    tc = min(tc, s)
    nc = s // tc
    halo16 = tc // 16
    halo8 = tc // SUBLANES

    def body(dy_ref, x_ref, xh_ref, g_ref, h_ref, hh_ref, cw_ref, cb_ref, wa_ref, ba_ref, wx_ref, bx_ref, lam_ref,
             wat_ref, wxt_ref, rep_ref, dx_ref, dg_ref, dcw_ref, dcb_ref, dba_ref, dbx_ref, dlam_ref, dwa_ref, dwx_ref,
             carry, dxr_next, a_scr, b_scr):
        cc = pl.program_id(1)
        ct = nc - 1 - cc

        @pl.when(cc == 0)
        def _():
            carry[...] = jnp.zeros_like(carry)
            dxr_next[...] = jnp.zeros_like(dxr_next)
            for ref in (dcw_ref, dcb_ref, dba_ref, dbx_ref, dlam_ref, dwa_ref, dwx_ref):
                ref[...] = jnp.zeros_like(ref)

        for t in range(RNN_TILES_PER_STEP):
            cols = lambda ref: ref.at[:, pl.ds(t * LANES, LANES)]
            one_tile(
                ct, cols(dy_ref), cols(x_ref), cols(xh_ref), cols(g_ref), cols(h_ref), cols(hh_ref), cols(cw_ref), cols(cb_ref),
                wa_ref.at[t], cols(ba_ref), wx_ref.at[t], cols(bx_ref), cols(lam_ref), wat_ref.at[t], wxt_ref.at[t], rep_ref,
                cols(dx_ref), cols(dg_ref), cols(dcw_ref), cols(dcb_ref), cols(dba_ref), cols(dbx_ref), cols(dlam_ref),
                dwa_ref.at[t], dwx_ref.at[t], cols(carry), cols(dxr_next), a_scr.at[t], b_scr.at[t],
            )

    def one_tile(ct, dy_ref, x_ref, xh_ref, g_ref, h_ref, hh_ref, cw_ref, cb_ref, wa_ref, ba_ref, wx_ref, bx_ref, lam_ref,
                 wat_ref, wxt_ref, rep_ref, dx_ref, dg_ref, dcw_ref, dcb_ref, dba_ref, dbx_ref, dlam_ref, dwa_ref, dwx_ref,
                 carry, dxr_next, a_scr, b_scr):
        xc = x_ref[...].astype(F32)
        xprev = jnp.where(ct > 0, xh_ref[SUBLANES:, :].astype(F32), 0.0)
        xr, (x1, x2, x3), r, i, a, om = _conv_and_gates(
            xc, xprev, cw_ref, cb_ref, wa_ref, ba_ref, wx_ref, bx_ref, lam_ref
        )
        inv_norm = lax.rsqrt(om)
        norm = om * inv_norm
        row = lax.broadcasted_iota(jnp.int32, xc.shape, 0)

        hv = h_ref[...]
        ge, ge_grad = _gelu_and_grad(g_ref[...].astype(F32))
        dyv = dy_ref[...].astype(F32)
        dg_ref[...] = (dyv * hv * ge_grad).astype(dg_ref.dtype)
        dh = dyv * ge

        b = dh + jnp.where(row == tc - 1, carry[0:1, :], 0.0)
        a_next = jnp.where(row < tc - 1, pltpu.roll(a, tc - 1, 0), 0.0)
        gadj = _scan_rows(a_next, b, tc, rep_ref, a_scr, b_scr, reverse=True)
        carry[...] = (a * gadj)[:SUBLANES, :]

        hprev_first = jnp.where(ct > 0, hh_ref[SUBLANES - 1 : SUBLANES, :], 0.0)
        hprev = jnp.where(row >= 1, pltpu.roll(hv, 1, 0), hprev_first)
        da = gadj * hprev
        ix = i * xr
        dnorm = gadj * ix
        di = gadj * norm * xr
        dlog_a = da * a - dnorm * (1.0 - om) * inv_norm
        sp = _softplus_neg(lam_ref[...])
        dr = dlog_a * (-LRU_C * sp)
        dsp = jnp.sum(dlog_a * (-LRU_C * r), axis=0, keepdims=True)
        dlam_ref[...] += dsp * (-_sigmoid(-lam_ref[...]))
        dza = dr * r * (1.0 - r)
        dzx = di * i * (1.0 - i)
        dzab, dzxb = dza.astype(BF16), dzx.astype(BF16)
        dxr = gadj * norm * i + _dot(dzab, wat_ref[...]) + _dot(dzxb, wxt_ref[...])
        xrb = xr.astype(BF16)
        dwa_ref[...] += _dot_tn(xrb, dzab)
        dwx_ref[...] += _dot_tn(xrb, dzxb)
        dba_ref[...] += jnp.sum(dza, axis=0, keepdims=True)
        dbx_ref[...] += jnp.sum(dzx, axis=0, keepdims=True)

        ext = jnp.concatenate([dxr, dxr_next[...]], axis=0)
        dx = (
            dxr * cw_ref[3:4, :]
            + _rows_after(ext, 1, tc) * cw_ref[2:3, :]
            + _rows_after(ext, 2, tc) * cw_ref[1:2, :]
            + _rows_after(ext, 3, tc) * cw_ref[0:1, :]
        )
        dxr_next[...] = dxr[:SUBLANES, :]
        dx_ref[...] = dx.astype(dx_ref.dtype)
        dcb_ref[...] += jnp.sum(dxr, axis=0, keepdims=True)
        dcw_ref[3:4, :] += jnp.sum(dxr * xc, axis=0, keepdims=True)
        dcw_ref[2:3, :] += jnp.sum(dxr * x1, axis=0, keepdims=True)
        dcw_ref[1:2, :] += jnp.sum(dxr * x2, axis=0, keepdims=True)
        dcw_ref[0:1, :] += jnp.sum(dxr * x3, axis=0, keepdims=True)

    wide = RNN_TILES_PER_STEP * LANES
    tile = lambda j, c: (0, j)
    mat = lambda j, c: (j, 0, 0)
    cur = lambda j, c: (nc - 1 - c, j)
    vec = pl.BlockSpec((1, wide), tile)
    matspec = pl.BlockSpec((RNN_TILES_PER_STEP, LANES, LANES), mat)
    repeat = _repeat_matrix(tc)
    return _call(
        body,
        (dy, proj, proj, proj, h, h, cw, cb, wa2, ba, wx2, bx, lam, wa2t, wx2t, repeat),
        name=name,
        grid=(N_RNN_TILES // RNN_TILES_PER_STEP, nc),
        in_specs=[
            pl.BlockSpec((tc, wide), cur),
            pl.BlockSpec((tc, wide), cur),
            pl.BlockSpec((16, wide), lambda j, c: (jnp.maximum((nc - 1 - c) * halo16 - 1, 0), j)),
            pl.BlockSpec((tc, wide), lambda j, c: (nc - 1 - c, D_RNN // wide + j)),
            pl.BlockSpec((tc, wide), cur),
            pl.BlockSpec((SUBLANES, wide), lambda j, c: (jnp.maximum((nc - 1 - c) * halo8 - 1, 0), j)),
            pl.BlockSpec((CONV_WIDTH, wide), tile),
            vec,
            matspec,
            vec,
            matspec,
            vec,
            vec,
            matspec,
            matspec,
            pl.BlockSpec(repeat.shape, lambda j, c: (0, 0)),
        ],
        out_specs=[
            pl.BlockSpec((tc, wide), cur),
            pl.BlockSpec((tc, wide), cur),
            pl.BlockSpec((CONV_WIDTH, wide), tile),
            vec,
            vec,
            vec,
            vec,
            matspec,
            matspec,
        ],
        out_shape=[
            SDS((s, D_RNN), BF16),
            SDS((s, D_RNN), BF16),
            SDS((CONV_WIDTH, D_RNN), F32),
            SDS((1, D_RNN), F32),
            SDS((1, D_RNN), F32),
            SDS((1, D_RNN), F32),
            SDS((1, D_RNN), F32),
            SDS((N_RNN_TILES, LANES, LANES), F32),
            SDS((N_RNN_TILES, LANES, LANES), F32),
        ],
        scratch_shapes=[pltpu.VMEM((SUBLANES, wide), F32)] * 2 + [pltpu.VMEM((RNN_TILES_PER_STEP, tc, LANES), F32)] * 2,
        semantics=("parallel", "arbitrary"),
        comm=comm,
    )


def _sgu_specs(tb):
    half = lambda blk: pl.BlockSpec((tb, 512), lambda n: (n, blk))
    return [half(U_BLK512), half(U_BLK512 + 1), half(V_BLK512), half(V_BLK512 + 1)]


def _sgu_normed(v, lng_ref, lnb_ref):
    gv, gv_grad = _gelu_and_grad(v)
    mu = jnp.mean(gv, axis=-1, keepdims=True)
    xc = gv - mu
    rs = lax.rsqrt(jnp.mean(xc * xc, axis=-1, keepdims=True) + EPS)
    xhat = xc * rs
    return xhat * lng_ref[...] + lnb_ref[...], xhat, rs, gv_grad


def _sgu_fwd(proj, lng, lnb, wm, bias, *, tb, name, comm=None):
    s = proj.shape[0]
    tb = min(tb, s)

    def body(u0_ref, u1_ref, v0_ref, v1_ref, lng_ref, lnb_ref, wm_ref, bias_ref, y_ref):
        u = jnp.concatenate([u0_ref[...], u1_ref[...]], axis=1).astype(F32)
        v = jnp.concatenate([v0_ref[...], v1_ref[...]], axis=1).astype(F32)
        gu = _gelu(u)
        vn, _, _, _ = _sgu_normed(v, lng_ref, lnb_ref)
        vnb = vn.astype(BF16)
        for blk in range(tb // SGU_BLOCK):
            rows = slice(blk * SGU_BLOCK, (blk + 1) * SGU_BLOCK)
            for g in range(SGU_GROUPS):
                cols = slice(g * LANES, (g + 1) * LANES)
                mixed = _dot(wm_ref[g], vnb[rows, cols]) + bias_ref[g]
                y_ref[rows, cols] = (gu[rows, cols] * mixed).astype(BF16)

    const2 = lambda n: (0, 0)
    const3 = lambda n: (0, 0, 0)
    return _call(
        body,
        (proj, proj, proj, proj, lng, lnb, wm, bias),
        name=name,
        grid=(s // tb,),
        in_specs=_sgu_specs(tb)
        + [
            pl.BlockSpec((1, D_SGU), const2),
            pl.BlockSpec((1, D_SGU), const2),
            pl.BlockSpec((SGU_GROUPS, SGU_BLOCK, SGU_BLOCK), const3),
            pl.BlockSpec((SGU_GROUPS, SGU_BLOCK, LANES), const3),
        ],
        out_specs=pl.BlockSpec((tb, D_SGU), lambda n: (n, 0)),
        out_shape=SDS((s, D_SGU), BF16),
        semantics=("parallel",),
        comm=comm,
    )


def _sgu_bwd(dy, proj, lng, lnb, wm, wmt, bias, mask, *, tb, name, comm=None):
    s = proj.shape[0]
    tb = min(tb, s)
    nb = s // tb

    def body(dy_ref, u0_ref, u1_ref, v0_ref, v1_ref, lng_ref, lnb_ref, wm_ref, wmt_ref, bias_ref, mask_ref,
             du_ref, dv_ref, dws_ref, dbs_ref, dlng_ref, dlnb_ref, dvn_scr, dbs_acc):
        n = pl.program_id(0)

        @pl.when(n == 0)
        def _():
            dbs_acc[...] = jnp.zeros_like(dbs_acc)
            for ref in (dws_ref, dlng_ref, dlnb_ref):
                ref[...] = jnp.zeros_like(ref)

        u = jnp.concatenate([u0_ref[...], u1_ref[...]], axis=1).astype(F32)
        v = jnp.concatenate([v0_ref[...], v1_ref[...]], axis=1).astype(F32)
        gu, gu_grad = _gelu_and_grad(u)
        vn, xhat, rs, gv_grad = _sgu_normed(v, lng_ref, lnb_ref)
        vnb = vn.astype(BF16)
        dyv = dy_ref[...].astype(F32)
        for blk in range(tb // SGU_BLOCK):
            rows = slice(blk * SGU_BLOCK, (blk + 1) * SGU_BLOCK)
            for g in range(SGU_GROUPS):
                cols = slice(g * LANES, (g + 1) * LANES)
                vt = vnb[rows, cols]
                mixed = _dot(wm_ref[g], vt) + bias_ref[g]
                dyt = dyv[rows, cols]
                du_ref[rows, cols] = (dyt * mixed * gu_grad[rows, cols]).astype(BF16)
                dmix = dyt * gu[rows, cols]
                dmixb = dmix.astype(BF16)
                dvn_scr[rows, cols] = _dot(wmt_ref[g], dmixb)
                dws_ref[g] += _dot_nt(dmixb, vt) * mask_ref[...]
                dbs_acc[g] += dmix
        dvn = dvn_scr[...]
        dlng_ref[...] += jnp.sum(dvn * xhat, axis=0, keepdims=True)
        dlnb_ref[...] += jnp.sum(dvn, axis=0, keepdims=True)
        dxh = dvn * lng_ref[...]
        dgv = rs * (
            dxh - jnp.mean(dxh, axis=-1, keepdims=True) - xhat * jnp.mean(dxh * xhat, axis=-1, keepdims=True)
        )
        dv_ref[...] = (dgv * gv_grad).astype(BF16)

        @pl.when(n == nb - 1)
        def _():
            for g in range(SGU_GROUPS):
                dbs_ref[g] = jnp.broadcast_to(jnp.sum(dbs_acc[g], axis=-1, keepdims=True), (SGU_BLOCK, LANES))

    const2 = lambda n: (0, 0)
    const3 = lambda n: (0, 0, 0)
    gmat = pl.BlockSpec((SGU_GROUPS, SGU_BLOCK, SGU_BLOCK), const3)
    vec = pl.BlockSpec((1, D_SGU), const2)
    act = pl.BlockSpec((tb, D_SGU), lambda n: (n, 0))
    return _call(
        body,
        (dy, proj, proj, proj, proj, lng, lnb, wm, wmt, bias, mask),
        name=name,
        grid=(nb,),
        in_specs=[act] + _sgu_specs(tb) + [vec, vec, gmat, gmat, gmat, pl.BlockSpec((SGU_BLOCK, SGU_BLOCK), const2)],
        out_specs=[act, act, gmat, gmat, vec, vec],
        out_shape=[
            SDS((s, D_SGU), BF16),
            SDS((s, D_SGU), BF16),
            SDS((SGU_GROUPS, SGU_BLOCK, SGU_BLOCK), F32),
            SDS((SGU_GROUPS, SGU_BLOCK, LANES), F32),
            SDS((1, D_SGU), F32),
            SDS((1, D_SGU), F32),
        ],
        scratch_shapes=[pltpu.VMEM((tb, D_SGU), F32), pltpu.VMEM((SGU_GROUPS, SGU_BLOCK, LANES), F32)],
        semantics=("arbitrary",),
        comm=comm,
    )


def _gate_specs(tm):
    half = lambda blk: pl.BlockSpec((tm, 512), lambda i: (i, blk))
    return [half(GA_BLK512), half(GA_BLK512 + 1), half(GB_BLK512), half(GB_BLK512 + 1)]


def _merge_fwd(ya_pre, yb_pre, proj, x, w_ba, w_bb, w_out, *, tm, name, comm=None):
    s = x.shape[0]
    tm = min(tm, s)

    def body(ya_ref, yb_ref, a0, a1, b0, b1, x_ref, wa_ref, wb_ref, wo_ref, x1_ref, yao_ref, ybo_ref):
        ya = _dot(ya_ref[...], wa_ref[...])
        yb = _dot(yb_ref[...], wb_ref[...])
        sa = _sigmoid(jnp.concatenate([a0[...], a1[...]], axis=1).astype(F32))
        sb = _sigmoid(jnp.concatenate([b0[...], b1[...]], axis=1).astype(F32))
        merged = sa * ya + sb * yb
        x1_ref[...] = x_ref[...] + _dot(merged.astype(BF16), wo_ref[...])
        yao_ref[...] = ya.astype(BF16)
        ybo_ref[...] = yb.astype(BF16)

    whole = lambda r: pl.BlockSpec((r, D), lambda i: (0, 0))
    act = pl.BlockSpec((tm, D), lambda i: (i, 0))
    return _call(
        body,
        (ya_pre, yb_pre, proj, proj, proj, proj, x, w_ba, w_bb, w_out),
        name=name,
        grid=(s // tm,),
        in_specs=[pl.BlockSpec((tm, D_RNN), lambda i: (i, 0)), act] + _gate_specs(tm) + [act, whole(D_RNN), whole(D_SGU), whole(D)],
        out_specs=[act, act, act],
        out_shape=[SDS((s, D), F32), SDS((s, D), BF16), SDS((s, D), BF16)],
        semantics=("parallel",),
        comm=comm,
    )


def _merge_bwd(dx1, ya, yb, proj, w_ba, w_bb, w_out, *, tm, name, comm=None):
    s = dx1.shape[0]
    tm = min(tm, s)

    def body(dx_ref, ya_ref, yb_ref, a0, a1, b0, b1, wa_ref, wb_ref, wo_ref,
             mg_ref, dya_ref, dyb_ref, dga_ref, dgb_ref, dyap_ref, dybp_ref):
        dm = _dot_nt(dx_ref[...], wo_ref[...])
        ya = ya_ref[...].astype(F32)
        yb = yb_ref[...].astype(F32)
        sa = _sigmoid(jnp.concatenate([a0[...], a1[...]], axis=1).astype(F32))
        sb = _sigmoid(jnp.concatenate([b0[...], b1[...]], axis=1).astype(F32))
        mg_ref[...] = (sa * ya + sb * yb).astype(BF16)
        dya = (dm * sa).astype(BF16)
        dyb = (dm * sb).astype(BF16)
        dya_ref[...] = dya
        dyb_ref[...] = dyb
        dga_ref[...] = (dm * ya * sa * (1.0 - sa)).astype(BF16)
        dgb_ref[...] = (dm * yb * sb * (1.0 - sb)).astype(BF16)
        dyap_ref[...] = _dot_nt(dya, wa_ref[...]).astype(BF16)
        dybp_ref[...] = _dot_nt(dyb, wb_ref[...]).astype(BF16)

    whole = lambda r: pl.BlockSpec((r, D), lambda i: (0, 0))
    act = pl.BlockSpec((tm, D), lambda i: (i, 0))
    act_rnn = pl.BlockSpec((tm, D_RNN), lambda i: (i, 0))
    return _call(
        body,
        (dx1, ya, yb, proj, proj, proj, proj, w_ba, w_bb, w_out),
        name=name,
        grid=(s // tm,),
        in_specs=[act, act, act] + _gate_specs(tm) + [whole(D_RNN), whole(D_SGU), whole(D)],
        out_specs=[act, act, act, act, act, act_rnn, act],
        out_shape=[SDS((s, D), BF16)] * 5 + [SDS((s, D_RNN), BF16), SDS((s, D_SGU), BF16)],
        semantics=("parallel",),
        comm=comm,
    )


def _ffn_down_loss(a, w, res, g, target, *, tm, name):
    s, k = a.shape
    tm = min(tm, s)

    def body(a_ref, w_ref, r_ref, g_ref, t_ref, dx_ref, dxb_ref, dg_ref, loss_ref):
        @pl.when(pl.program_id(0) == 0)
        def _():
            dg_ref[...] = jnp.zeros_like(dg_ref)
            loss_ref[...] = jnp.zeros_like(loss_ref)

        t = jnp.maximum(a_ref[...].astype(F32), 0.0)
        xv = r_ref[...] + _dot((t * t).astype(BF16), w_ref[...])
        r = lax.rsqrt(jnp.mean(xv * xv, axis=-1, keepdims=True) + EPS)
        xhat = xv * r
        e = xhat * g_ref[...] - t_ref[...]
        loss_ref[...] += 0.5 * jnp.sum(jnp.mean(e * e, axis=-1, keepdims=True), axis=0, keepdims=True)
        dy = e * (1.0 / D)
        dxh = dy * g_ref[...]
        dx = r * (dxh - xhat * jnp.mean(dxh * xhat, axis=-1, keepdims=True))
        dx_ref[...] = dx
        dxb_ref[...] = dx.astype(BF16)
        dg_ref[...] += jnp.sum(dy * xhat, axis=0, keepdims=True)

    act = pl.BlockSpec((tm, D), lambda i: (i, 0))
    vec = pl.BlockSpec((1, D), lambda i: (0, 0))
    return pl.pallas_call(
        body,
        name=name,
        grid=(s // tm,),
        in_specs=[pl.BlockSpec((tm, k), lambda i: (i, 0)), pl.BlockSpec((k, D), lambda i: (0, 0)), act, vec, act],
        out_specs=[act, act, vec, pl.BlockSpec((SUBLANES, LANES), lambda i: (0, 0))],
        out_shape=[SDS((s, D), F32), SDS((s, D), BF16), SDS((1, D), F32), SDS((SUBLANES, LANES), F32)],
        compiler_params=_params("arbitrary"),
    )(a, w, res, g, target)


def _adamw_math(w, g, m, v):
    m2 = ADAM_B1 * m + (1.0 - ADAM_B1) * g
    v2 = ADAM_B2 * v + (1.0 - ADAM_B2) * (g * g)
    m_hat = m2 / (1.0 - ADAM_B1**ADAM_STEP)
    v_hat = v2 / (1.0 - ADAM_B2**ADAM_STEP)
    delta = -ADAM_LR * (m_hat / (jnp.sqrt(v_hat) + ADAM_EPS) + ADAM_WD * w)
    return delta, m2, v2


def _row_tile(rows, cap):
    return max(t for t in range(SUBLANES, min(cap, rows) + 1, SUBLANES) if rows % t == 0)


def _adamw_layers(w, grads, m, v, *, tr, name):
    depth, r, c = w.shape
    tr = _row_tile(r, tr)

    def body(*refs):
        g_refs = refs[:depth]
        w_ref, m_ref, v_ref, g_out, d_ref, mo_ref, vo_ref = refs[depth:]
        for l in range(depth):

            @pl.when(pl.program_id(0) == l)
            def _(l=l):
                g = g_refs[l][...]
                g_out[...] = g
                d_ref[...], mo_ref[...], vo_ref[...] = _adamw_math(w_ref[...], g, m_ref[...], v_ref[...])

    def of_layer(ll):
        return pl.BlockSpec((tr, c), lambda l, i: (jnp.where(l == ll, i, 0), 0))

    stacked = pl.BlockSpec((None, tr, c), lambda l, i: (l, i, 0))
    return pl.pallas_call(
        body,
        name=name,
        grid=(depth, r // tr),
        in_specs=[of_layer(ll) for ll in range(depth)] + [stacked] * 3,
        out_specs=[stacked] * 4,
        out_shape=[SDS((depth, r, c), F32)] * 4,
        compiler_params=_params("parallel", "parallel"),
    )(*grads, w, m, v)


def _adamw_reduced(w, parts, from_chips, m, v, chip, *, tr, name):
    depth, r, _ = w.shape
    tr = _row_tile(r, tr)

    def body(chip_ref, *refs):
        p_refs, c_refs = refs[:depth], refs[depth : 2 * depth]
        w_ref, m_ref, v_ref, g_out, d_ref, mo_ref, vo_ref = refs[2 * depth :]
        for l in range(depth):

            @pl.when(pl.program_id(0) == l)
            def _(l=l):
                got = c_refs[l]
                g = ((p_refs[l][...].astype(F32) + got[0].astype(F32)) + got[1].astype(F32)) + got[2].astype(F32)
                g_out[...] = g
                d_ref[...], mo_ref[...], vo_ref[...] = _adamw_math(w_ref[...], g, m_ref[...], v_ref[...])

    def mine_of_layer(ll):
        return pl.BlockSpec((None, tr, D), lambda l, i, chip_ref: (chip_ref[0], jnp.where(l == ll, i, 0), 0))

    def theirs_of_layer(ll):
        return pl.BlockSpec((3, tr, D), lambda l, i, chip_ref: (0, jnp.where(l == ll, i, 0), 0))

    stacked = pl.BlockSpec((None, tr, D), lambda l, i, chip_ref: (l, i, 0))
    return pl.pallas_call(
        body,
        name=name,
        grid_spec=pltpu.PrefetchScalarGridSpec(
            num_scalar_prefetch=1,
            grid=(depth, r // tr),
            in_specs=[mine_of_layer(ll) for ll in range(depth)]
            + [theirs_of_layer(ll) for ll in range(depth)]
            + [stacked] * 3,
            out_specs=[stacked] * 4,
        ),
        out_shape=[SDS((depth, r, D), F32)] * 4,
        compiler_params=_params("parallel", "parallel"),
    )(chip, *parts, *from_chips, w, m, v)


def _adamw_small(groups, *, name):
    n = len(groups)

    def body(*refs):
        ins, outs = refs[: 4 * n], refs[4 * n :]
        for i in range(n):
            w, g, m, v = (ref[...] for ref in ins[4 * i : 4 * i + 4])
            outs[3 * i][...], outs[3 * i + 1][...], outs[3 * i + 2][...] = _adamw_math(w, g, m, v)

    vmem = pl.BlockSpec(memory_space=pltpu.VMEM)
    outs = pl.pallas_call(
        body,
        name=name,
        in_specs=[vmem] * (4 * n),
        out_specs=[vmem] * (3 * n),
        out_shape=[SDS(grp[0].shape, F32) for grp in groups for _ in range(3)],
        compiler_params=pltpu.CompilerParams(vmem_limit_bytes=VMEM_LIMIT_BYTES),
    )(*[a for grp in groups for a in grp])
    return [tuple(outs[3 * i : 3 * i + 3]) for i in range(n)]


ANY = pl.BlockSpec(memory_space=pl.ANY)


def _position():
    return lax.axis_index("x"), lax.axis_index("y"), lax.axis_index("c")


def _other_chips(x, y):
    return [(1 - x, y), (x, 1 - y), (1 - x, 1 - y)]


class _Comm:
    def __init__(self, inputs, out_shapes, sem_counts, start, middle, finish, middle_at=1.0, aliases=()):
        self.inputs, self.out_shapes, self.sem_counts = list(inputs), list(out_shapes), list(sem_counts)
        self.start, self.middle, self.finish = start, middle, finish
        self.middle_at = middle_at
        self.aliases = list(aliases)

    def sem_shapes(self):
        return [pltpu.SemaphoreType.DMA((n,)) for n in self.sem_counts]


def _merge_comms(comms):
    bounds, i, o, s = [], 0, 0, 0
    for cm in comms:
        bounds.append((i, i + len(cm.inputs), o, o + len(cm.out_shapes), s, s + len(cm.sem_counts)))
        i, o, s = bounds[-1][1], bounds[-1][3], bounds[-1][5]

    def phase(which):
        def run(ins, outs, sems):
            for cm, (i0, i1, o0, o1, s0, s1) in zip(comms, bounds):
                getattr(cm, which)(ins[i0:i1], outs[o0:o1], sems[s0:s1])

        return run

    return _Comm(
        [a for cm in comms for a in cm.inputs],
        [a for cm in comms for a in cm.out_shapes],
        [a for cm in comms for a in cm.sem_counts],
        phase("start"),
        phase("middle"),
        phase("finish"),
        middle_at=max(cm.middle_at for cm in comms),
        aliases=[(i0 + i, o0 + o) for cm, (i0, _, o0, _, _, _) in zip(comms, bounds) for i, o in cm.aliases],
    )


def _call(body, args, *, semantics, comm=None, **kw):
    if comm is None:
        return pl.pallas_call(body, compiler_params=_params(*semantics), **kw)(*args)
    grid, in_specs, out_specs, out_shape = kw["grid"], kw["in_specs"], kw["out_specs"], kw["out_shape"]
    scratch = list(kw.get("scratch_shapes", ()))
    single = not isinstance(out_shape, (list, tuple))
    core_specs = [out_specs] if single else list(out_specs)
    core_shapes = [out_shape] if single else list(out_shape)
    n_in, n_out, n_scr = len(in_specs), len(core_shapes), len(scratch)
    n_cin, n_cout = len(comm.inputs), len(comm.out_shapes)
    steps = 1
    for g in grid:
        steps *= g
    middle = min(int(comm.middle_at * steps), steps - 1)

    def hosted(*refs):
        core_in, c_in = refs[:n_in], refs[n_in : n_in + n_cin]
        o0 = n_in + n_cin
        core_out, c_out = refs[o0 : o0 + n_out], refs[o0 + n_out : o0 + n_out + n_cout]
        s0 = o0 + n_out + n_cout
        core_scr, sems = refs[s0 : s0 + n_scr], refs[s0 + n_scr :]
        step = pl.program_id(0)
        for d in range(1, len(grid)):
            step = step * grid[d] + pl.program_id(d)

        @pl.when(step == 0)
        def _():
            comm.start(c_in, c_out, sems)

        body(*core_in, *core_out, *core_scr)

        @pl.when(step == middle)
        def _():
            comm.middle(c_in, c_out, sems)

        @pl.when(step == steps - 1)
        def _():
            comm.finish(c_in, c_out, sems)

    outs = pl.pallas_call(
        hosted,
        name=kw["name"],
        grid=grid,
        in_specs=list(in_specs) + [ANY] * n_cin,
        out_specs=core_specs + [ANY] * n_cout,
        out_shape=core_shapes + comm.out_shapes,
        scratch_shapes=scratch + comm.sem_shapes(),
        input_output_aliases={n_in + i: n_out + o for i, o in comm.aliases},
        compiler_params=_params(*(["arbitrary"] * len(grid))),
    )(*args, *comm.inputs)
    return (outs[0] if single else outs[:n_out]), outs[n_out:]


def _comm_only(comm, *, name):
    n_cin, n_cout = len(comm.inputs), len(comm.out_shapes)

    def body(*refs):
        ins, outs, sems = refs[:n_cin], refs[n_cin : n_cin + n_cout], refs[n_cin + n_cout :]
        comm.start(ins, outs, sems)
        comm.middle(ins, outs, sems)
        comm.finish(ins, outs, sems)

    return pl.pallas_call(
        body,
        name=name,
        in_specs=[ANY] * n_cin,
        out_specs=[ANY] * n_cout,
        out_shape=comm.out_shapes,
        scratch_shapes=comm.sem_shapes(),
    )(*comm.inputs)


def _gather_comm(shards, pass_on_at=1.0):
    n = len(shards)
    per = 7

    def plan(ins, outs, sems):
        send, recv, local = sems
        x, y, c = _position()
        me, sibling = (x, y, c), (x, y, 1 - c)
        chips = _other_chips(x, y)

        def block(t, px, py, pc):
            return outs[t].at[pl.ds(4 * px + 2 * py + pc, 1)]

        def copy(t, k, blk, to, src=None):
            return pltpu.make_async_remote_copy(
                src_ref=block(t, *blk) if src is None else src,
                dst_ref=block(t, *blk),
                send_sem=send.at[t * per + k],
                recv_sem=recv.at[t * per + k],
                device_id=to,
                device_id_type=MESH,
            )

        mine = [pltpu.make_async_copy(ins[t], block(t, *me), local.at[t]) for t in range(n)]
        to_chips = [copy(t, 1 + j, me, (*chip, c), src=ins[t]) for t in range(n) for j, chip in enumerate(chips)]
        to_sibling = [copy(t, 0, me, sibling, src=ins[t]) for t in range(n)]
        from_chips = [copy(t, 1 + j, (*chip, c), me) for t in range(n) for j, chip in enumerate(chips)]
        passed_on = [copy(t, 4 + j, (*chip, c), sibling) for t in range(n) for j, chip in enumerate(chips)]
        from_sibling = [copy(t, 0, sibling, me) for t in range(n)]
        from_sibling += [copy(t, 4 + j, (*chip, 1 - c), me) for t in range(n) for j, chip in enumerate(chips)]
        return mine, to_chips, to_sibling, from_chips, passed_on, from_sibling

    def start(ins, outs, sems):
        mine, to_chips, to_sibling, _, _, _ = plan(ins, outs, sems)
        for cp in mine + to_chips + to_sibling:
            cp.start()

    def middle(ins, outs, sems):
        _, _, _, from_chips, passed_on, _ = plan(ins, outs, sems)
        for arrived, onward in zip(from_chips, passed_on):
            arrived.wait_recv()
            onward.start()

    def finish(ins, outs, sems):
        mine, to_chips, to_sibling, _, passed_on, from_sibling = plan(ins, outs, sems)
        for cp in from_sibling:
            cp.wait_recv()
        for cp in to_chips + to_sibling + passed_on:
            cp.wait_send()
        for cp in mine:
            cp.wait()

    out_shapes = [SDS((N_DEV,) + sh.shape[1:], sh.dtype) for sh in shards]
    return _Comm(shards, out_shapes, [n * per, n * per, n], start, middle, finish, middle_at=pass_on_at)


def _gather_halves(shards=None, arrived=None):
    first_half = arrived is None
    arrays = shards if first_half else arrived
    n = len(arrays)
    per = 4 if first_half else 3

    def plan(ins, outs, sems):
        x, y, c = _position()
        me, sibling = (x, y, c), (x, y, 1 - c)
        chips = _other_chips(x, y)

        def block(t, px, py, pc):
            return outs[t].at[pl.ds(4 * px + 2 * py + pc, 1)]

        def copy(t, k, blk, to, src=None):
            return pltpu.make_async_remote_copy(
                src_ref=block(t, *blk) if src is None else src,
                dst_ref=block(t, *blk),
                send_sem=sems[0].at[t * per + k],
                recv_sem=sems[1].at[t * per + k],
                device_id=to,
                device_id_type=MESH,
            )

        if first_half:
            local = [pltpu.make_async_copy(ins[t], block(t, *me), sems[2].at[t]) for t in range(n)]
            sent = [copy(t, 1 + j, me, (*chip, c), src=ins[t]) for t in range(n) for j, chip in enumerate(chips)]
            sent += [copy(t, 0, me, sibling, src=ins[t]) for t in range(n)]
            landing = [copy(t, 1 + j, (*chip, c), me) for t in range(n) for j, chip in enumerate(chips)]
            landing += [copy(t, 0, sibling, me) for t in range(n)]
        else:
            local = []
            sent = [copy(t, j, (*chip, c), sibling) for t in range(n) for j, chip in enumerate(chips)]
            landing = [copy(t, j, (*chip, 1 - c), me) for t in range(n) for j, chip in enumerate(chips)]
        return local, sent, landing

    def start(ins, outs, sems):
        local, sent, _ = plan(ins, outs, sems)
        for cp in local + sent:
            cp.start()

    def middle(ins, outs, sems):
        pass

    def finish(ins, outs, sems):
        local, sent, landing = plan(ins, outs, sems)
        for cp in landing:
            cp.wait_recv()
        for cp in sent:
            cp.wait_send()
        for cp in local:
            cp.wait()

    if first_half:
        out_shapes = [SDS((N_DEV,) + sh.shape[1:], sh.dtype) for sh in shards]
        return _Comm(shards, out_shapes, [n * per, n * per, n], start, middle, finish)
    out_shapes = [SDS(a.shape, a.dtype) for a in arrived]
    return _Comm(arrived, out_shapes, [n * per, n * per], start, middle, finish, aliases=[(t, t) for t in range(n)])


def _exchange_comm(arrays, out_shapes, n_copies, copies_of):
    def start(ins, outs, sems):
        for cp in copies_of(ins, outs, *sems):
            cp.start()

    def middle(ins, outs, sems):
        pass

    def finish(ins, outs, sems):
        for cp in copies_of(ins, outs, *sems):
            cp.wait()

    return _Comm(arrays, out_shapes, [n_copies, n_copies], start, middle, finish)


def _sibling_comm(grads):
    def copies_of(ins, outs, send, recv):
        x, y, c = _position()
        return [
            pltpu.make_async_remote_copy(
                src_ref=ins[t].at[:, pl.ds(1 - c, 1)],
                dst_ref=outs[t],
                send_sem=send.at[t],
                recv_sem=recv.at[t],
                device_id=(x, y, 1 - c),
                device_id_type=MESH,
            )
            for t in range(len(ins))
        ]

    return _exchange_comm(grads, [SDS((4, 1) + g.shape[2:], g.dtype) for g in grads], len(grads), copies_of)


def _chips_comm(parts):
    def copies_of(ins, outs, send, recv):
        x, y, c = _position()
        return [
            pltpu.make_async_remote_copy(
                src_ref=ins[t].at[pl.ds(2 * px + py, 1)],
                dst_ref=outs[t].at[pl.ds(k, 1)],
                send_sem=send.at[3 * t + k],
                recv_sem=recv.at[3 * t + k],
                device_id=(px, py, c),
                device_id_type=MESH,
            )
            for t in range(len(ins))
            for k, (px, py) in enumerate(_other_chips(x, y))
        ]

    return _exchange_comm(parts, [SDS((3,) + p.shape[1:], p.dtype) for p in parts], 3 * len(parts), copies_of)


def _sum_with_sibling(grad, got, core, *, name):
    rows = grad.shape[2]

    def body(core_ref, a_ref, b_ref, o_ref):
        o_ref[...] = (a_ref[...].astype(F32) + b_ref[...].astype(F32)).astype(o_ref.dtype)

    return pl.pallas_call(
        body,
        name=name,
        grid_spec=pltpu.PrefetchScalarGridSpec(
            num_scalar_prefetch=1,
            grid=(4,),
            in_specs=[
                pl.BlockSpec((None, None, rows, D), lambda q, core_ref: (q, core_ref[0], 0, 0)),
                pl.BlockSpec((None, None, rows, D), lambda q, core_ref: (q, 0, 0, 0)),
            ],
            out_specs=pl.BlockSpec((None, rows, D), lambda q, core_ref: (q, 0, 0)),
        ),
        out_shape=SDS((4, rows, D), grad.dtype),
        compiler_params=_params("parallel"),
    )(core, grad, got)


def _sum_chips(part, got, chip, *, name):
    rows = part.shape[1]

    def body(chip_ref, a_ref, b_ref, o_ref):
        o_ref[...] = ((a_ref[...].astype(F32) + b_ref[0].astype(F32)) + b_ref[1].astype(F32)) + b_ref[2].astype(F32)

    return pl.pallas_call(
        body,
        name=name,
        grid_spec=pltpu.PrefetchScalarGridSpec(
            num_scalar_prefetch=1,
            grid=(1,),
            in_specs=[
                pl.BlockSpec((None, rows, D), lambda i, chip_ref: (chip_ref[0], 0, 0)),
                pl.BlockSpec((3, rows, D), lambda i, chip_ref: (0, 0, 0)),
            ],
            out_specs=pl.BlockSpec((rows, D), lambda i, chip_ref: (0, 0)),
        ),
        out_shape=SDS((rows, D), F32),
        compiler_params=_params("arbitrary"),
    )(chip, part, got)


def _all_reduce_small(pack, *, name):
    rows = pack.shape[1]

    def body(in_ref, out_ref, from_sibling, part, from_chips, send, recv):
        x, y, c = _position()
        me, sibling = (x, y, c), (x, y, 1 - c)
        chips = _other_chips(x, y)
        waiting = []

        def copy(k, src, dst, to):
            return pltpu.make_async_remote_copy(
                src_ref=src, dst_ref=dst, send_sem=send.at[k], recv_sem=recv.at[k], device_id=to, device_id_type=MESH
            )

        def exchange(copies):
            for cp in copies:
                cp.start()
            for cp in copies:
                cp.wait_recv()
            waiting.extend(copies)

        def block(px, py, pc):
            return out_ref.at[4 * px + 2 * py + pc]

        exchange([copy(q, in_ref.at[2 * q + 1 - c], from_sibling.at[q], sibling) for q in range(4)])
        for q in range(4):
            part[q] = in_ref[2 * q + c] + from_sibling[q]
        exchange([copy(4 + k, part.at[2 * px + py], from_chips.at[k], (px, py, c)) for k, (px, py) in enumerate(chips)])
        out_ref[4 * x + 2 * y + c] = ((part[2 * x + y] + from_chips[0]) + from_chips[1]) + from_chips[2]
        exchange(
            [copy(7, block(*me), block(*me), sibling)]
            + [copy(8 + k, block(*me), block(*me), (px, py, c)) for k, (px, py) in enumerate(chips)]
        )
        exchange([copy(11 + k, block(px, py, c), block(px, py, c), sibling) for k, (px, py) in enumerate(chips)])
        for cp in waiting:
            cp.wait_send()

    vmem = pl.BlockSpec(memory_space=pltpu.VMEM)
    return pl.pallas_call(
        body,
        name=name,
        in_specs=[vmem],
        out_specs=vmem,
        out_shape=SDS(pack.shape, F32),
        scratch_shapes=[
            pltpu.VMEM((4, rows, D), F32),
            pltpu.VMEM((4, rows, D), F32),
            pltpu.VMEM((3, rows, D), F32),
            pltpu.SemaphoreType.DMA((14,)),
            pltpu.SemaphoreType.DMA((14,)),
        ],
        compiler_params=pltpu.CompilerParams(vmem_limit_bytes=VMEM_LIMIT_BYTES),
    )(pack)


def _pack(arrays, rows):
    flat = jnp.concatenate([a.reshape(-1).astype(F32) for a in arrays])
    return jnp.pad(flat, (0, rows * D - flat.shape[0])).reshape(rows, D)


def _unpack(pack, shapes):
    flat = pack.reshape(-1)
    out, off = [], 0
    for sh in shapes:
        size = 1
        for dim in sh:
            size *= dim
        out.append(flat[off : off + size].reshape(sh))
        off += size
    return out


def _block_diag_pairs(w):
    w = w.reshape(N_RNN_TILES, 2, HEAD_DIM, HEAD_DIM)
    z = jnp.zeros_like(w[:, 0])
    top = jnp.concatenate([w[:, 0], z], axis=2)
    bot = jnp.concatenate([z, w[:, 1]], axis=2)
    return jnp.concatenate([top, bot], axis=1)


def _diag_blocks(w2):
    a = w2[:, :HEAD_DIM, :HEAD_DIM]
    b = w2[:, HEAD_DIM:, HEAD_DIM:]
    return jnp.stack([a, b], axis=1).reshape(RNN_HEADS, HEAD_DIM, HEAD_DIM)


BIG = ("w_in", "w_branch_a", "w_branch_b", "w_out", "w_up", "w_down")
TRANSPOSED = ("w_in", "w_up")
SMALL = (
    "norm_mix_g", "conv_w", "conv_b", "lru_w_a", "lru_b_a", "lru_w_x", "lru_b_x", "lru_lambda",
    "sgu_ln_g", "sgu_ln_b", "sgu_w_s", "sgu_b_s", "norm_ffn_g", "final_norm_g",
)
WEIGHTS = (
    "norm_mix_g", "w_in", "conv_w", "conv_b", "lru_w_a", "lru_b_a", "lru_w_x", "lru_b_x", "lru_lambda", "sgu_ln_g",
    "sgu_ln_b", "sgu_w_s", "sgu_b_s", "w_branch_a", "w_branch_b", "w_out", "norm_ffn_g", "w_up", "w_down", "final_norm_g",
)

TM = 512
TM_NT = 1024
TN_IN = 1664
TN_UP = 2048
TKA = 512
TKA_PIECES = 256
TC = 512
TB = 256
TR = 256


_BRANCH_WEIGHTS = ("w_branch_a", "w_branch_b", "w_out")
GATHERS_RIDING = (
    {
        "in_proj": ([(0, name) for name in _BRANCH_WEIGHTS] + [(0, "w_up")], []),
        "branch_a_fwd": ([(1, "w_in")], [(0, name) for name in _BRANCH_WEIGHTS] + [(0, "w_up")]),
        "sgu_fwd": ([], [(1, "w_in")]),
        "merge_fwd": ([(0, "w_down")], []),
        "ffn_up": ([(1, name) for name in _BRANCH_WEIGHTS], [(0, "w_down")]),
        "ffn_down": ([(1, "w_up")], [(1, name) for name in _BRANCH_WEIGHTS]),
    },
    {"in_proj": ([(1, "w_down")], [(1, "w_up")]), "branch_a_fwd": ([], [(1, "w_down")])},
)


def _layer_forward(l, x, p, w, shards, arriving, loss_head=None):
    def run(key, fn, *args, **kw):
        first, second = GATHERS_RIDING[l].get(key, ((), ()))
        comms = []
        if first:
            comms.append(_gather_halves(shards=[shards[l2][n2] for l2, n2 in first]))
        if second:
            comms.append(_gather_halves(arrived=[arriving.pop(k) for k in second]))
        if not comms:
            return fn(*args, **kw)
        out, got = fn(*args, comm=_merge_comms(comms), **kw)
        arriving.update(zip(first, got[: len(first)]))
        for (l2, n2), full in zip(second, got[len(first) :]):
            w[l2][n2] = full.reshape(-1, D)
        return out

    proj, h = run("in_proj", _norm_matmul_nt, x, p["norm_mix_g"], w[l]["w_in"], tm=TM_NT, tn=TN_IN, name=f"in_proj_{l}")
    hseq, ya_pre = run(
        "branch_a_fwd", _branch_a_fwd, proj, p["conv_w"], p["conv_b"], p["wa2"], p["lru_b_a"], p["wx2"], p["lru_b_x"],
        p["lru_lambda"], tc=TC, name=f"branch_a_fwd_{l}",
    )
    yb_pre = run("sgu_fwd", _sgu_fwd, proj, p["sgu_ln_g"], p["sgu_ln_b"], p["wm"], p["sgu_bias"], tb=TB, name=f"sgu_fwd_{l}")
    x1, ya, yb = run(
        "merge_fwd", _merge_fwd, ya_pre, yb_pre, proj, x, w[l]["w_branch_a"], w[l]["w_branch_b"], w[l]["w_out"], tm=TM,
        name=f"merge_fwd_{l}",
    )
    f_pre, h2 = run("ffn_up", _norm_matmul_nt, x1, p["norm_ffn_g"], w[l]["w_up"], tm=TM_NT, tn=TN_UP, name=f"ffn_up_{l}")
    saved = dict(x=x, h=h, proj=proj, hseq=hseq, ya_pre=ya_pre, yb_pre=yb_pre, ya=ya, yb=yb, x1=x1, h2=h2, f_pre=f_pre)
    if loss_head is None:
        return run("ffn_down", _matmul_nn_res, f_pre, w[l]["w_down"], x1, relu2=True, tm=TM, name=f"ffn_down_{l}"), saved
    return _ffn_down_loss(f_pre, w[l]["w_down"], x1, *loss_head, tm=TM, name=f"ffn_down_loss_{l}"), saved


def _layer_backward(l, dx2, dx2b, sv, p, w, core, waiting, last):
    parts, from_chips = {}, {}

    def by_device(g):
        return g.reshape(4, 2, -1, D)

    def with_sibling(name, g, got):
        parts[name] = _sum_with_sibling(by_device(g), got, core, name=f"sum_sibling_{name}_{l}")

    df_pre = _matmul_nt_drelu2(dx2b, w["w_down"], sv["f_pre"], tm=TM_NT, tn=TN_UP, name=f"ffn_down_bwd_{l}")
    g_down = _matmul_tn([sv["f_pre"]], dx2b, relu2=True, tka=TKA, name=f"grad_w_down_{l}")
    g_up, (got,) = _matmul_tn(
        [df_pre], sv["h2"], relu2=False, tka=TKA, name=f"grad_w_up_{l}", comm=_sibling_comm([by_device(g_down)])
    )
    with_sibling("w_down", g_down, got)
    (dx1, dx1b, g_norm_ffn), (got, from_chips[l, "w_down"]) = _matmul_nn_rmsnorm_bwd(
        [df_pre], w["w_up"], sv["x1"], p["norm_ffn_g"], dx2, tm=TM, name=f"ffn_up_bwd_{l}",
        comm=_merge_comms([_sibling_comm([by_device(g_up)]), _chips_comm([parts["w_down"]])]),
    )
    with_sibling("w_up", g_up, got)
    (merged, dya, dyb, dga, dgb, dya_pre, dyb_pre), (from_chips[l, "w_up"],) = _merge_bwd(
        dx1b, sv["ya"], sv["yb"], sv["proj"], w["w_branch_a"], w["w_branch_b"], w["w_out"], tm=TM, name=f"merge_bwd_{l}",
        comm=_chips_comm([parts["w_up"]]),
    )
    g_out = _matmul_tn([merged], dx1b, relu2=False, tka=TKA, name=f"grad_w_out_{l}")
    g_ba = _matmul_tn([sv["ya_pre"]], dya, relu2=False, tka=TKA_PIECES, name=f"grad_w_branch_a_{l}")
    g_bb = _matmul_tn([sv["yb_pre"]], dyb, relu2=False, tka=TKA, name=f"grad_w_branch_b_{l}")
    branch = (("w_out", g_out), ("w_branch_a", g_ba), ("w_branch_b", g_bb))
    (du, dv, g_ws, g_bs, g_lng, g_lnb), got = _sgu_bwd(
        dyb_pre, sv["proj"], p["sgu_ln_g"], p["sgu_ln_b"], p["wm"], p["wmt"], p["sgu_bias"], p["mask"], tb=TB,
        name=f"sgu_bwd_{l}", comm=_sibling_comm([by_device(g) for _, g in branch]),
    )
    for (name, g), landed in zip(branch, got):
        with_sibling(name, g, landed)
    riding = [((l, name), parts[name]) for name, _ in branch] + list(waiting)
    (dxr, dgr, g_cw, g_cb, g_ba_, g_bx, g_lam, g_wa2, g_wx2), got = _branch_a_bwd(
        dya_pre, sv["proj"], sv["hseq"], p["conv_w"], p["conv_b"], p["wa2"], p["lru_b_a"], p["wx2"], p["lru_b_x"],
        p["lru_lambda"], p["wa2t"], p["wx2t"], tc=TC, name=f"branch_a_bwd_{l}", comm=_chips_comm([part for _, part in riding]),
    )
    for (key, _), landed in zip(riding, got):
        from_chips[key] = landed
    dproj = [dxr, dgr, du, dv, dga, dgb]
    g_in = _matmul_tn(dproj, sv["h"], relu2=False, tka=TKA_PIECES, name=f"grad_w_in_{l}")
    if last:
        (got,) = _comm_only(_sibling_comm([by_device(g_in)]), name=f"grad_w_in_to_sibling_{l}")
        with_sibling("w_in", g_in, got)
        riding = _chips_comm([parts["w_in"]])
    else:
        riding = _sibling_comm([by_device(g_in)])
    (dx, dxb, g_norm_mix), (got,) = _matmul_nn_rmsnorm_bwd(
        dproj, w["w_in"], sv["x"], p["norm_mix_g"], dx1, tm=TM, name=f"in_proj_bwd_{l}", comm=riding
    )
    if last:
        from_chips[l, "w_in"] = got
    else:
        with_sibling("w_in", g_in, got)
    small = dict(
        norm_mix_g=g_norm_mix[0], conv_w=g_cw, conv_b=g_cb[0], lru_w_a=_diag_blocks(g_wa2), lru_b_a=g_ba_.reshape(RNN_HEADS, HEAD_DIM),
        lru_w_x=_diag_blocks(g_wx2), lru_b_x=g_bx.reshape(RNN_HEADS, HEAD_DIM), lru_lambda=g_lam[0], sgu_ln_g=g_lng[0],
        sgu_ln_b=g_lnb[0], sgu_w_s=g_ws, sgu_b_s=g_bs[:, :, 0], norm_ffn_g=g_norm_ffn[0],
    )
    return dx, dxb, small, parts, from_chips


def _prepare_small(l, given):
    chunk_id = jnp.arange(SGU_BLOCK) // CHUNK
    mask = (chunk_id[:, None] >= chunk_id[None, :]).astype(F32)
    wm = given["sgu_w_s"][l] * mask
    wa2 = _block_diag_pairs(given["lru_w_a"][l])
    wx2 = _block_diag_pairs(given["lru_w_x"][l])
    row = lambda a: a.reshape(1, -1)
    return dict(
        norm_mix_g=row(given["norm_mix_g"][l]),
        norm_ffn_g=row(given["norm_ffn_g"][l]),
        conv_w=given["conv_w_full"][l],
        conv_b=row(given["conv_b"][l]),
        wa2=wa2.astype(BF16),
        wx2=wx2.astype(BF16),
        wa2t=jnp.swapaxes(wa2, 1, 2).astype(BF16),
        wx2t=jnp.swapaxes(wx2, 1, 2).astype(BF16),
        lru_b_a=row(given["lru_b_a"][l]),
        lru_b_x=row(given["lru_b_x"][l]),
        lru_lambda=row(given["lru_lambda"][l]),
        sgu_ln_g=row(given["sgu_ln_g"][l]),
        sgu_ln_b=row(given["sgu_ln_b"][l]),
        wm=wm.astype(BF16),
        wmt=jnp.swapaxes(wm, 1, 2).astype(BF16),
        sgu_bias=jnp.broadcast_to(given["sgu_b_s"][l][:, :, None], (SGU_GROUPS, SGU_BLOCK, LANES)),
        mask=mask,
    )


def _step(given):
    x_idx, y_idx, c_idx = _position()
    dev = 4 * x_idx + 2 * y_idx + c_idx
    core = c_idx.astype(jnp.int32).reshape(1)
    chip = (2 * x_idx + y_idx).astype(jnp.int32).reshape(1)

    def rows_first(name, a):
        return jnp.swapaxes(a, 1, 2) if name in TRANSPOSED else a

    shards = []
    for l in range(DEPTH):
        shards.append({name: rows_first(name, given[name])[l].astype(BF16)[None] for name in BIG})
    conv_mine = given["conv_w"].reshape(1, DEPTH * CONV_WIDTH, D_RNN // N_DEV)
    w_in_first, conv_all = _comm_only(_gather_comm([shards[0]["w_in"], conv_mine]), name="gather_first")
    weights = [{"w_in": w_in_first.reshape(-1, D)}, {}]
    conv_all = conv_all.reshape(N_DEV, DEPTH, CONV_WIDTH, D_RNN // N_DEV)
    given = dict(given, conv_w_full=jnp.moveaxis(conv_all, 0, 2).reshape(DEPTH, CONV_WIDTH, D_RNN))

    small_params = [_prepare_small(l, given) for l in range(DEPTH)]
    x = given["x"][0]
    saved, arriving = [], {}
    loss_head = (given["final_norm_g"].reshape(1, D), given["loss_target"][0])
    for l in range(DEPTH):
        x, sv = _layer_forward(
            l, x, small_params[l], weights, shards, arriving, loss_head=loss_head if l == DEPTH - 1 else None
        )
        saved.append(sv)
    dx, dxb, g_final, loss = x
    small_grads, parts, from_chips, waiting = [None] * DEPTH, [None] * DEPTH, {}, []
    for l in reversed(range(DEPTH)):
        dx, dxb, small_grads[l], parts[l], got = _layer_backward(
            l, dx, dxb, saved[l], small_params[l], weights[l], core, waiting, last=l == 0
        )
        from_chips.update(got)
        waiting = [((l, "w_in"), parts[l]["w_in"])]

    small_list = []
    for name in SMALL[:-1]:
        small_list.append(jnp.stack([small_grads[l][name] for l in range(DEPTH)]))
    small_list += [g_final[0], loss[0, :1]]
    small_shapes = [a.shape for a in small_list]
    pack = _pack(small_list, SMALL_ROWS).reshape(N_DEV, SMALL_ROWS_PER_DEV, D)
    summed = _unpack(_all_reduce_small(pack, name="all_reduce_small"), small_shapes)
    loss_total = summed[-1][0]
    grads = dict(zip(SMALL, summed[:-1]))
    cw = grads["conv_w"].reshape(DEPTH, CONV_WIDTH, N_DEV, D_RNN // N_DEV)
    grads["conv_w"] = lax.dynamic_index_in_dim(cw, dev, axis=2, keepdims=False)

    delta, new_m, new_v = {}, {}, {}
    for name in BIG:
        w, m, v = given[name], given["m_" + name], given["v_" + name]
        mine = [parts[l][name] for l in range(DEPTH)]
        theirs = [from_chips[l, name] for l in range(DEPTH)]
        if name == "w_up":
            sums = [_sum_chips(mine[l], theirs[l], chip, name=f"sum_chips_{name}_{l}").T for l in range(DEPTH)]
            out = _adamw_layers(w, sums, m, v, tr=TR, name=f"adamw_{name}")
        else:
            out = _adamw_reduced(
                rows_first(name, w), mine, theirs, rows_first(name, m), rows_first(name, v), chip, tr=TR, name=f"adamw_{name}"
            )
            out = [rows_first(name, a) for a in out]
        grads[name], delta[name], new_m[name], new_v[name] = out
    two_d = lambda a: a.reshape(1, -1) if a.ndim == 1 else a
    groups = [tuple(two_d(a) for a in (given[n], grads[n], given["m_" + n], given["v_" + n])) for n in SMALL]
    for n, (d, m2, v2) in zip(SMALL, _adamw_small(groups, name="adamw_small")):
        shape = given[n].shape
        delta[n], new_m[n], new_v[n] = d.reshape(shape), m2.reshape(shape), v2.reshape(shape)

    return (
        loss_total, dx[None],
        *[grads[n] for n in WEIGHTS], *[delta[n] for n in WEIGHTS], *[new_m[n] for n in WEIGHTS], *[new_v[n] for n in WEIGHTS],
    )


def kernel(x, norm_mix_g, w_in, conv_w, conv_b, lru_w_a, lru_b_a, lru_w_x, lru_b_x, lru_lambda, sgu_ln_g, sgu_ln_b, sgu_w_s, sgu_b_s, w_branch_a, w_branch_b, w_out, norm_ffn_g, w_up, w_down, final_norm_g, loss_target, m_norm_mix_g, m_w_in, m_conv_w, m_conv_b, m_lru_w_a, m_lru_b_a, m_lru_w_x, m_lru_b_x, m_lru_lambda, m_sgu_ln_g, m_sgu_ln_b, m_sgu_w_s, m_sgu_b_s, m_w_branch_a, m_w_branch_b, m_w_out, m_norm_ffn_g, m_w_up, m_w_down, m_final_norm_g, v_norm_mix_g, v_w_in, v_conv_w, v_conv_b, v_lru_w_a, v_lru_b_a, v_lru_w_x, v_lru_b_x, v_lru_lambda, v_sgu_ln_g, v_sgu_ln_b, v_sgu_w_s, v_sgu_b_s, v_w_branch_a, v_w_branch_b, v_w_out, v_norm_ffn_g, v_w_up, v_w_down, v_final_norm_g):
    return _step(dict(locals()))
```

```python
import jax
import jax.numpy as jnp
from jax import lax
from jax.experimental import pallas as pl
from jax.experimental.pallas import tpu as pltpu

F32 = jnp.float32
BF16 = jnp.bfloat16
SDS = jax.ShapeDtypeStruct
MESH = pl.DeviceIdType.MESH

D = 1024
D_RNN = 1280
D_SGU = 1024
D_FF = 4096
D_IN = 2 * D_RNN + 2 * D_SGU + 2 * D
DEPTH = 2
RNN_HEADS = 20
HEAD_DIM = 64
CONV_WIDTH = 4
LRU_C = 8.0
SGU_GROUPS = 8
SGU_BLOCK = 128
CHUNK = 64
EPS = 1e-6
N_DEV = 8

ADAM_LR = 0.001
ADAM_B1 = 0.9
ADAM_B2 = 0.999
ADAM_EPS = 1e-08
ADAM_WD = 0.01
ADAM_STEP = 10

LANES = 128
SUBLANES = 8
VMEM_LIMIT_BYTES = 56 * 1024 * 1024

N_RNN_TILES = D_RNN // LANES
RNN_TILES_PER_STEP = 5
U_BLK512 = (2 * D_RNN) // 512
V_BLK512 = (2 * D_RNN + D_SGU) // 512
GA_BLK512 = (2 * D_RNN + 2 * D_SGU) // 512
GB_BLK512 = (2 * D_RNN + 2 * D_SGU + D) // 512

SMALL_ROWS_PER_DEV = 80
SMALL_ROWS = N_DEV * SMALL_ROWS_PER_DEV


def _params(*sem):
    return pltpu.CompilerParams(dimension_semantics=sem, vmem_limit_bytes=VMEM_LIMIT_BYTES)


def _sigmoid(x):
    return 0.5 + 0.5 * jnp.tanh(0.5 * x)


_GELU_C = 0.7978845608028654
_GELU_K = 0.044715


def _gelu(x):
    t = jnp.tanh(_GELU_C * (x + _GELU_K * x * x * x))
    return 0.5 * x * (1.0 + t)


def _gelu_and_grad(x):
    t = jnp.tanh(_GELU_C * (x + _GELU_K * x * x * x))
    val = 0.5 * x * (1.0 + t)
    grad = 0.5 * (1.0 + t) + 0.5 * x * (1.0 - t * t) * _GELU_C * (1.0 + 3.0 * _GELU_K * x * x)
    return val, grad


def _one_minus_square(log_a, a):
    return -jnp.tanh(log_a) * (1.0 + a * a)


def _dot(a, b):
    return jnp.dot(a, b, preferred_element_type=F32)


def _dot_nt(a, b):
    return lax.dot_general(a, b, (((1,), (1,)), ((), ())), preferred_element_type=F32)


def _dot_tn(a, b):
    return lax.dot_general(a, b, (((0,), (0,)), ((), ())), preferred_element_type=F32)


def _norm_matmul_nt(x, g, w, *, tm, tn, name, comm=None):
    s, n = x.shape[0], w.shape[0]
    tm, tn = min(tm, s), min(tn, n)

    def body(x_ref, g_ref, w_ref, o_ref, h_ref):
        @pl.when(pl.program_id(1) == 0)
        def _():
            xv = x_ref[...]
            r = lax.rsqrt(jnp.mean(xv * xv, axis=-1, keepdims=True) + EPS)
            h_ref[...] = (xv * r * g_ref[...]).astype(BF16)

        o_ref[...] = _dot_nt(h_ref[...], w_ref[...]).astype(o_ref.dtype)

    return _call(
        body,
        (x, g, w),
        name=name,
        grid=(s // tm, n // tn),
        in_specs=[
            pl.BlockSpec((tm, D), lambda i, j: (i, 0)),
            pl.BlockSpec((1, D), lambda i, j: (0, 0)),
            pl.BlockSpec((tn, D), lambda i, j: (j, 0)),
        ],
        out_specs=[pl.BlockSpec((tm, tn), lambda i, j: (i, j)), pl.BlockSpec((tm, D), lambda i, j: (i, 0))],
        out_shape=[SDS((s, n), BF16), SDS((s, D), BF16)],
        semantics=("parallel", "arbitrary"),
        comm=comm,
    )


def _matmul_nn_res(a, w, res, *, relu2, tm, name, comm=None):
    s, k = a.shape
    tm = min(tm, s)

    def body(a_ref, w_ref, r_ref, o_ref):
        av = a_ref[...]
        if relu2:
            t = jnp.maximum(av.astype(F32), 0.0)
            av = (t * t).astype(BF16)
        o_ref[...] = r_ref[...] + _dot(av, w_ref[...])

    return _call(
        body,
        (a, w, res),
        name=name,
        grid=(s // tm,),
        in_specs=[
            pl.BlockSpec((tm, k), lambda i: (i, 0)),
            pl.BlockSpec((k, D), lambda i: (0, 0)),
            pl.BlockSpec((tm, D), lambda i: (i, 0)),
        ],
        out_specs=pl.BlockSpec((tm, D), lambda i: (i, 0)),
        out_shape=SDS((s, D), F32),
        semantics=("parallel",),
        comm=comm,
    )


def _matmul_nt_drelu2(a, w, pre, *, tm, tn, name):
    s, n = a.shape[0], w.shape[0]
    tm, tn = min(tm, s), min(tn, n)

    def body(a_ref, w_ref, p_ref, o_ref):
        d = _dot_nt(a_ref[...], w_ref[...])
        o_ref[...] = (d * (2.0 * jnp.maximum(p_ref[...].astype(F32), 0.0))).astype(o_ref.dtype)

    return pl.pallas_call(
        body,
        name=name,
        grid=(s // tm, n // tn),
        in_specs=[
            pl.BlockSpec((tm, D), lambda i, j: (i, 0)),
            pl.BlockSpec((tn, D), lambda i, j: (j, 0)),
            pl.BlockSpec((tm, tn), lambda i, j: (i, j)),
        ],
        out_specs=pl.BlockSpec((tm, tn), lambda i, j: (i, j)),
        out_shape=SDS((s, n), BF16),
        compiler_params=_params("parallel", "arbitrary"),
    )(a, w, pre)


def _matmul_tn(a_list, b, *, relu2, tka, name, comm=None):
    s = b.shape[0]
    n = len(a_list)
    nblk = [a.shape[1] // tka for a in a_list]
    starts = [sum(nblk[:p]) for p in range(n)]

    def body(*refs):
        a_refs, b_ref, o_ref = refs[:n], refs[n], refs[n + 1]
        i = pl.program_id(0)
        for p in range(n):

            @pl.when((i >= starts[p]) & (i < starts[p] + nblk[p]))
            def _(p=p):
                av = a_refs[p][...]
                if relu2:
                    t = jnp.maximum(av.astype(F32), 0.0)
                    av = (t * t).astype(BF16)
                o_ref[...] = _dot_tn(av, b_ref[...]).astype(o_ref.dtype)

    def piece_spec(p):
        return pl.BlockSpec((s, tka), lambda i: (0, jnp.clip(i - starts[p], 0, nblk[p] - 1)))

    return _call(
        body,
        (*a_list, b),
        name=name,
        grid=(sum(nblk),),
        in_specs=[piece_spec(p) for p in range(n)] + [pl.BlockSpec((s, D), lambda i: (0, 0))],
        out_specs=pl.BlockSpec((tka, D), lambda i: (i, 0)),
        out_shape=SDS((sum(nblk) * tka, D), BF16),
        semantics=("parallel",),
        comm=comm,
    )


def _matmul_nn_rmsnorm_bwd(a_list, w, x, g, res, *, tm, name, comm=None):
    s = x.shape[0]
    tm = min(tm, s)
    n = len(a_list)
    widths = [a.shape[1] for a in a_list]
    offs = [sum(widths[:p]) for p in range(n)]
    k = sum(widths)

    def body(*refs):
        a_refs = refs[:n]
        w_ref, x_ref, g_ref, r_ref, dx_ref, dxb_ref, dg_ref = refs[n:]

        @pl.when(pl.program_id(0) == 0)
        def _():
            dg_ref[...] = jnp.zeros_like(dg_ref)

        dh = _dot(a_refs[0][...], w_ref[0 : widths[0], :])
        for p in range(1, n):
            dh += _dot(a_refs[p][...], w_ref[offs[p] : offs[p] + widths[p], :])
        xv = x_ref[...]
        r = lax.rsqrt(jnp.mean(xv * xv, axis=-1, keepdims=True) + EPS)
        xhat = xv * r
        dxh = dh * g_ref[...]
        dx = r_ref[...] + r * (dxh - xhat * jnp.mean(dxh * xhat, axis=-1, keepdims=True))
        dx_ref[...] = dx
        dxb_ref[...] = dx.astype(BF16)
        dg_ref[...] += jnp.sum(dh * xhat, axis=0, keepdims=True)

    act = pl.BlockSpec((tm, D), lambda i: (i, 0))
    vec = pl.BlockSpec((1, D), lambda i: (0, 0))
    return _call(
        body,
        (*a_list, w, x, g, res),
        name=name,
        grid=(s // tm,),
        in_specs=[pl.BlockSpec((tm, wd), lambda i: (i, 0)) for wd in widths]
        + [pl.BlockSpec((k, D), lambda i: (0, 0), pipeline_mode=pl.Buffered(1)), act, vec, act],
        out_specs=[act, act, vec],
        out_shape=[SDS((s, D), F32), SDS((s, D), BF16), SDS((1, D), F32)],
        semantics=("arbitrary",),
        comm=comm,
    )


def _rows_before(ext, k):
    if k == 0:
        return ext[SUBLANES:, :]
    return pltpu.roll(ext, k, 0)[SUBLANES:, :]


def _rows_after(ext, k, n):
    if k == 0:
        return ext[:n, :]
    return pltpu.roll(ext, n + SUBLANES - k, 0)[:n, :]


def _scan_forward(a, b, n):
    row = lax.broadcasted_iota(jnp.int32, a.shape, 0)
    d = 1
    while d < n:
        if d < SUBLANES:
            m = row >= d
            a_s = jnp.where(m, pltpu.roll(a, d, 0), 1.0)
            b_s = jnp.where(m, pltpu.roll(b, d, 0), 0.0)
            b = a * b_s + b
            a = a * a_s
        else:
            b = jnp.concatenate([b[:d], a[d:] * b[: n - d] + b[d:]], axis=0)
            a = jnp.concatenate([a[:d], a[d:] * a[: n - d]], axis=0)
        d *= 2
    return a, b


def _scan_backward(a, b, n):
    row = lax.broadcasted_iota(jnp.int32, a.shape, 0)
    d = 1
    while d < n:
        if d < SUBLANES:
            m = row < n - d
            a_s = jnp.where(m, pltpu.roll(a, n - d, 0), 1.0)
            b_s = jnp.where(m, pltpu.roll(b, n - d, 0), 0.0)
            b = a * b_s + b
            a = a * a_s
        else:
            b = jnp.concatenate([a[: n - d] * b[d:] + b[: n - d], b[n - d :]], axis=0)
            a = jnp.concatenate([a[: n - d] * a[d:], a[n - d :]], axis=0)
        d *= 2
    return b


def _repeat_matrix(n):
    groups = n // SUBLANES
    return (jnp.arange(n)[:, None] // SUBLANES == jnp.arange(3 * groups)[None, :] % groups).astype(BF16)


def _scan_rows(a, b, n, repeat_ref, a_scr, b_scr, reverse):
    groups = n // SUBLANES
    a3 = a.reshape(groups, SUBLANES, LANES)
    b3 = b.reshape(groups, SUBLANES, LANES)
    sub = lax.broadcasted_iota(jnp.int32, a3.shape, 1)
    for d in (1, 2, 4):
        m = (sub < SUBLANES - d) if reverse else (sub >= d)
        shift = SUBLANES - d if reverse else d
        a_s = jnp.where(m, pltpu.roll(a3, shift, 1), 1.0)
        b_s = jnp.where(m, pltpu.roll(b3, shift, 1), 0.0)
        b3 = a3 * b_s + b3
        a3 = a3 * a_s
    a_scr[...] = a3.reshape(n, LANES)
    b_scr[...] = b3.reshape(n, LANES)
    edge = 0 if reverse else SUBLANES - 1
    a_tot = a_scr[pl.ds(edge, groups, stride=SUBLANES), :]
    b_tot = b_scr[pl.ds(edge, groups, stride=SUBLANES), :]
    row = lax.broadcasted_iota(jnp.int32, a_tot.shape, 0)
    if reverse:
        through = _scan_backward(a_tot, b_tot, groups)
        entering = jnp.where(row < groups - 1, pltpu.roll(through, groups - 1, 0), 0.0)
    else:
        _, through = _scan_forward(a_tot, b_tot, groups)
        entering = jnp.where(row >= 1, pltpu.roll(through, 1, 0), 0.0)
    hi = entering.astype(BF16)
    rest = entering - hi.astype(F32)
    mid = rest.astype(BF16)
    lo = (rest - mid.astype(F32)).astype(BF16)
    repeated = _dot(repeat_ref[...], jnp.concatenate([hi, mid, lo], axis=0))
    return b_scr[...] + a_scr[...] * repeated


def _softplus_neg(lam):
    z = -lam
    return jnp.maximum(z, 0.0) + jnp.log1p(jnp.exp(-jnp.abs(z)))


def _conv_and_gates(xc, xprev, cw_ref, cb_ref, wa_ref, ba_ref, wx_ref, bx_ref, lam_ref):
    ext = jnp.concatenate([xprev, xc], axis=0)
    x1, x2, x3 = _rows_before(ext, 1), _rows_before(ext, 2), _rows_before(ext, 3)
    xr = cb_ref[...] + x3 * cw_ref[0:1, :] + x2 * cw_ref[1:2, :] + x1 * cw_ref[2:3, :] + xc * cw_ref[3:4, :]
    xrb = xr.astype(BF16)
    r = _sigmoid(_dot(xrb, wa_ref[...]) + ba_ref[...])
    i = _sigmoid(_dot(xrb, wx_ref[...]) + bx_ref[...])
    sp = _softplus_neg(lam_ref[...])
    log_a = (-LRU_C * r) * sp
    a = jnp.exp(log_a)
    return xr, (x1, x2, x3), r, i, a, _one_minus_square(log_a, a)


def _branch_a_fwd(proj, cw, cb, wa2, ba, wx2, bx, lam, *, tc, name, comm=None):
    s = proj.shape[0]
    tc = min(tc, s)

    def body(x_ref, g_ref, cw_ref, cb_ref, wa_ref, ba_ref, wx_ref, bx_ref, lam_ref, rep_ref, h_ref, y_ref,
             xprev, hlast, a_scr, b_scr):
        @pl.when(pl.program_id(1) == 0)
        def _():
            xprev[...] = jnp.zeros_like(xprev)
            hlast[...] = jnp.zeros_like(hlast)

        for t in range(RNN_TILES_PER_STEP):
            cols = lambda ref: ref.at[:, pl.ds(t * LANES, LANES)]
            one_tile(
                cols(x_ref), cols(g_ref), cols(cw_ref), cols(cb_ref), wa_ref.at[t], cols(ba_ref), wx_ref.at[t], cols(bx_ref),
                cols(lam_ref), rep_ref, cols(h_ref), cols(y_ref), cols(xprev), cols(hlast), a_scr.at[t], b_scr.at[t],
            )

    def one_tile(x_ref, g_ref, cw_ref, cb_ref, wa_ref, ba_ref, wx_ref, bx_ref, lam_ref, rep_ref, h_ref, y_ref,
                 xprev, hlast, a_scr, b_scr):
        xc = x_ref[...].astype(F32)
        xr, _, r, i, a, om = _conv_and_gates(xc, xprev[...], cw_ref, cb_ref, wa_ref, ba_ref, wx_ref, bx_ref, lam_ref)
        xprev[...] = xc[tc - SUBLANES :, :]
        u = jnp.sqrt(om) * (i * xr)
        row8 = lax.broadcasted_iota(jnp.int32, (SUBLANES, LANES), 0)
        first = u[:SUBLANES] + jnp.where(row8 == 0, a[:SUBLANES] * hlast[SUBLANES - 1 : SUBLANES, :], 0.0)
        h = _scan_rows(a, jnp.concatenate([first, u[SUBLANES:]], axis=0), tc, rep_ref, a_scr, b_scr, reverse=False)
        hlast[...] = h[tc - SUBLANES :, :]
        h_ref[...] = h
        y_ref[...] = (h * _gelu(g_ref[...].astype(F32))).astype(BF16)

    wide = RNN_TILES_PER_STEP * LANES
    tile = lambda j, c: (0, j)
    vec = pl.BlockSpec((1, wide), tile)
    mats = pl.BlockSpec((RNN_TILES_PER_STEP, LANES, LANES), lambda j, c: (j, 0, 0))
    repeat = _repeat_matrix(tc)
    return _call(
        body,
        (proj, proj, cw, cb, wa2, ba, wx2, bx, lam, repeat),
        name=name,
        grid=(N_RNN_TILES // RNN_TILES_PER_STEP, s // tc),
        in_specs=[
            pl.BlockSpec((tc, wide), lambda j, c: (c, j)),
            pl.BlockSpec((tc, wide), lambda j, c: (c, D_RNN // wide + j)),
            pl.BlockSpec((CONV_WIDTH, wide), tile),
            vec,
            mats,
            vec,
            mats,
            vec,
            vec,
            pl.BlockSpec(repeat.shape, lambda j, c: (0, 0)),
        ],
        out_specs=[pl.BlockSpec((tc, wide), lambda j, c: (c, j)), pl.BlockSpec((tc, wide), lambda j, c: (c, j))],
        out_shape=[SDS((s, D_RNN), F32), SDS((s, D_RNN), BF16)],
        scratch_shapes=[pltpu.VMEM((SUBLANES, wide), F32)] * 2 + [pltpu.VMEM((RNN_TILES_PER_STEP, tc, LANES), F32)] * 2,
        semantics=("parallel", "arbitrary"),
        comm=comm,
    )


def _branch_a_bwd(dy, proj, h, cw, cb, wa2, ba, wx2, bx, lam, wa2t, wx2t, *, tc, name, comm=None):
    s = proj.shape[0]
    tc = min(tc, s)
    nc = s // tc
    halo16 = tc // 16
    halo8 = tc // SUBLANES

    def body(dy_ref, x_ref, xh_ref, g_ref, h_ref, hh_ref, cw_ref, cb_ref, wa_ref, ba_ref, wx_ref, bx_ref, lam_ref,
             wat_ref, wxt_ref, rep_ref, dx_ref, dg_ref, dcw_ref, dcb_ref, dba_ref, dbx_ref, dlam_ref, dwa_ref, dwx_ref,
             carry, dxr_next, a_scr, b_scr):
        cc = pl.program_id(1)
        ct = nc - 1 - cc

        @pl.when(cc == 0)
        def _():
            carry[...] = jnp.zeros_like(carry)
            dxr_next[...] = jnp.zeros_like(dxr_next)
            for ref in (dcw_ref, dcb_ref, dba_ref, dbx_ref, dlam_ref, dwa_ref, dwx_ref):
                ref[...] = jnp.zeros_like(ref)

        for t in range(RNN_TILES_PER_STEP):
            cols = lambda ref: ref.at[:, pl.ds(t * LANES, LANES)]
            one_tile(
                ct, cols(dy_ref), cols(x_ref), cols(xh_ref), cols(g_ref), cols(h_ref), cols(hh_ref), cols(cw_ref), cols(cb_ref),
                wa_ref.at[t], cols(ba_ref), wx_ref.at[t], cols(bx_ref), cols(lam_ref), wat_ref.at[t], wxt_ref.at[t], rep_ref,
                cols(dx_ref), cols(dg_ref), cols(dcw_ref), cols(dcb_ref), cols(dba_ref), cols(dbx_ref), cols(dlam_ref),
                dwa_ref.at[t], dwx_ref.at[t], cols(carry), cols(dxr_next), a_scr.at[t], b_scr.at[t],
            )

    def one_tile(ct, dy_ref, x_ref, xh_ref, g_ref, h_ref, hh_ref, cw_ref, cb_ref, wa_ref, ba_ref, wx_ref, bx_ref, lam_ref,
                 wat_ref, wxt_ref, rep_ref, dx_ref, dg_ref, dcw_ref, dcb_ref, dba_ref, dbx_ref, dlam_ref, dwa_ref, dwx_ref,
                 carry, dxr_next, a_scr, b_scr):
        xc = x_ref[...].astype(F32)
        xprev = jnp.where(ct > 0, xh_ref[SUBLANES:, :].astype(F32), 0.0)
        xr, (x1, x2, x3), r, i, a, om = _conv_and_gates(
            xc, xprev, cw_ref, cb_ref, wa_ref, ba_ref, wx_ref, bx_ref, lam_ref
        )
        inv_norm = lax.rsqrt(om)
        norm = om * inv_norm
        row = lax.broadcasted_iota(jnp.int32, xc.shape, 0)

        hv = h_ref[...]
        ge, ge_grad = _gelu_and_grad(g_ref[...].astype(F32))
        dyv = dy_ref[...].astype(F32)
        dg_ref[...] = (dyv * hv * ge_grad).astype(dg_ref.dtype)
        dh = dyv * ge

        b = dh + jnp.where(row == tc - 1, carry[0:1, :], 0.0)
        a_next = jnp.where(row < tc - 1, pltpu.roll(a, tc - 1, 0), 0.0)
        gadj = _scan_rows(a_next, b, tc, rep_ref, a_scr, b_scr, reverse=True)
        carry[...] = (a * gadj)[:SUBLANES, :]

        hprev_first = jnp.where(ct > 0, hh_ref[SUBLANES - 1 : SUBLANES, :], 0.0)
        hprev = jnp.where(row >= 1, pltpu.roll(hv, 1, 0), hprev_first)
        da = gadj * hprev
        ix = i * xr
        dnorm = gadj * ix
        di = gadj * norm * xr
        dlog_a = da * a - dnorm * (1.0 - om) * inv_norm
        sp = _softplus_neg(lam_ref[...])
        dr = dlog_a * (-LRU_C * sp)
        dsp = jnp.sum(dlog_a * (-LRU_C * r), axis=0, keepdims=True)
        dlam_ref[...] += dsp * (-_sigmoid(-lam_ref[...]))
        dza = dr * r * (1.0 - r)
        dzx = di * i * (1.0 - i)
        dzab, dzxb = dza.astype(BF16), dzx.astype(BF16)
        dxr = gadj * norm * i + _dot(dzab, wat_ref[...]) + _dot(dzxb, wxt_ref[...])
        xrb = xr.astype(BF16)
        dwa_ref[...] += _dot_tn(xrb, dzab)
        dwx_ref[...] += _dot_tn(xrb, dzxb)
        dba_ref[...] += jnp.sum(dza, axis=0, keepdims=True)
        dbx_ref[...] += jnp.sum(dzx, axis=0, keepdims=True)

        ext = jnp.concatenate([dxr, dxr_next[...]], axis=0)
        dx = (
            dxr * cw_ref[3:4, :]
            + _rows_after(ext, 1, tc) * cw_ref[2:3, :]
            + _rows_after(ext, 2, tc) * cw_ref[1:2, :]
            + _rows_after(ext, 3, tc) * cw_ref[0:1, :]
        )
        dxr_next[...] = dxr[:SUBLANES, :]
        dx_ref[...] = dx.astype(dx_ref.dtype)
        dcb_ref[...] += jnp.sum(dxr, axis=0, keepdims=True)
        dcw_ref[3:4, :] += jnp.sum(dxr * xc, axis=0, keepdims=True)
        dcw_ref[2:3, :] += jnp.sum(dxr * x1, axis=0, keepdims=True)
        dcw_ref[1:2, :] += jnp.sum(dxr * x2, axis=0, keepdims=True)
        dcw_ref[0:1, :] += jnp.sum(dxr * x3, axis=0, keepdims=True)

    wide = RNN_TILES_PER_STEP * LANES
    tile = lambda j, c: (0, j)
    mat = lambda j, c: (j, 0, 0)
    cur = lambda j, c: (nc - 1 - c, j)
    vec = pl.BlockSpec((1, wide), tile)
    matspec = pl.BlockSpec((RNN_TILES_PER_STEP, LANES, LANES), mat)
    repeat = _repeat_matrix(tc)
    return _call(
        body,
        (dy, proj, proj, proj, h, h, cw, cb, wa2, ba, wx2, bx, lam, wa2t, wx2t, repeat),
        name=name,
        grid=(N_RNN_TILES // RNN_TILES_PER_STEP, nc),
        in_specs=[
            pl.BlockSpec((tc, wide), cur),
            pl.BlockSpec((tc, wide), cur),
            pl.BlockSpec((16, wide), lambda j, c: (jnp.maximum((nc - 1 - c) * halo16 - 1, 0), j)),
            pl.BlockSpec((tc, wide), lambda j, c: (nc - 1 - c, D_RNN // wide + j)),
            pl.BlockSpec((tc, wide), cur),
            pl.BlockSpec((SUBLANES, wide), lambda j, c: (jnp.maximum((nc - 1 - c) * halo8 - 1, 0), j)),
            pl.BlockSpec((CONV_WIDTH, wide), tile),
            vec,
            matspec,
            vec,
            matspec,
            vec,
            vec,
            matspec,
            matspec,
            pl.BlockSpec(repeat.shape, lambda j, c: (0, 0)),
        ],
        out_specs=[
            pl.BlockSpec((tc, wide), cur),
            pl.BlockSpec((tc, wide), cur),
            pl.BlockSpec((CONV_WIDTH, wide), tile),
            vec,
            vec,
            vec,
            vec,
            matspec,
            matspec,
        ],
        out_shape=[
            SDS((s, D_RNN), BF16),
            SDS((s, D_RNN), BF16),
            SDS((CONV_WIDTH, D_RNN), F32),
            SDS((1, D_RNN), F32),
            SDS((1, D_RNN), F32),
            SDS((1, D_RNN), F32),
            SDS((1, D_RNN), F32),
            SDS((N_RNN_TILES, LANES, LANES), F32),
            SDS((N_RNN_TILES, LANES, LANES), F32),
        ],
        scratch_shapes=[pltpu.VMEM((SUBLANES, wide), F32)] * 2 + [pltpu.VMEM((RNN_TILES_PER_STEP, tc, LANES), F32)] * 2,
        semantics=("parallel", "arbitrary"),
        comm=comm,
    )


def _sgu_specs(tb):
    half = lambda blk: pl.BlockSpec((tb, 512), lambda n: (n, blk))
    return [half(U_BLK512), half(U_BLK512 + 1), half(V_BLK512), half(V_BLK512 + 1)]


def _sgu_normed(v, lng_ref, lnb_ref):
    gv, gv_grad = _gelu_and_grad(v)
    mu = jnp.mean(gv, axis=-1, keepdims=True)
    xc = gv - mu
    rs = lax.rsqrt(jnp.mean(xc * xc, axis=-1, keepdims=True) + EPS)
    xhat = xc * rs
    return xhat * lng_ref[...] + lnb_ref[...], xhat, rs, gv_grad


def _sgu_fwd(proj, lng, lnb, wm, bias, *, tb, name, comm=None):
    s = proj.shape[0]
    tb = min(tb, s)

    def body(u0_ref, u1_ref, v0_ref, v1_ref, lng_ref, lnb_ref, wm_ref, bias_ref, y_ref):
        u = jnp.concatenate([u0_ref[...], u1_ref[...]], axis=1).astype(F32)
        v = jnp.concatenate([v0_ref[...], v1_ref[...]], axis=1).astype(F32)
        gu = _gelu(u)
        vn, _, _, _ = _sgu_normed(v, lng_ref, lnb_ref)
        vnb = vn.astype(BF16)
        for blk in range(tb // SGU_BLOCK):
            rows = slice(blk * SGU_BLOCK, (blk + 1) * SGU_BLOCK)
            for g in range(SGU_GROUPS):
                cols = slice(g * LANES, (g + 1) * LANES)
                mixed = _dot(wm_ref[g], vnb[rows, cols]) + bias_ref[g]
                y_ref[rows, cols] = (gu[rows, cols] * mixed).astype(BF16)

    const2 = lambda n: (0, 0)
    const3 = lambda n: (0, 0, 0)
    return _call(
        body,
        (proj, proj, proj, proj, lng, lnb, wm, bias),
        name=name,
        grid=(s // tb,),
        in_specs=_sgu_specs(tb)
        + [
            pl.BlockSpec((1, D_SGU), const2),
            pl.BlockSpec((1, D_SGU), const2),
            pl.BlockSpec((SGU_GROUPS, SGU_BLOCK, SGU_BLOCK), const3),
            pl.BlockSpec((SGU_GROUPS, SGU_BLOCK, LANES), const3),
        ],
        out_specs=pl.BlockSpec((tb, D_SGU), lambda n: (n, 0)),
        out_shape=SDS((s, D_SGU), BF16),
        semantics=("parallel",),
        comm=comm,
    )


def _sgu_bwd(dy, proj, lng, lnb, wm, wmt, bias, mask, *, tb, name, comm=None):
    s = proj.shape[0]
    tb = min(tb, s)
    nb = s // tb

    def body(dy_ref, u0_ref, u1_ref, v0_ref, v1_ref, lng_ref, lnb_ref, wm_ref, wmt_ref, bias_ref, mask_ref,
             du_ref, dv_ref, dws_ref, dbs_ref, dlng_ref, dlnb_ref, dvn_scr, dbs_acc):
        n = pl.program_id(0)

        @pl.when(n == 0)
        def _():
            dbs_acc[...] = jnp.zeros_like(dbs_acc)
            for ref in (dws_ref, dlng_ref, dlnb_ref):
                ref[...] = jnp.zeros_like(ref)

        u = jnp.concatenate([u0_ref[...], u1_ref[...]], axis=1).astype(F32)
        v = jnp.concatenate([v0_ref[...], v1_ref[...]], axis=1).astype(F32)
        gu, gu_grad = _gelu_and_grad(u)
        vn, xhat, rs, gv_grad = _sgu_normed(v, lng_ref, lnb_ref)
        vnb = vn.astype(BF16)
        dyv = dy_ref[...].astype(F32)
        for blk in range(tb // SGU_BLOCK):
            rows = slice(blk * SGU_BLOCK, (blk + 1) * SGU_BLOCK)
            for g in range(SGU_GROUPS):
                cols = slice(g * LANES, (g + 1) * LANES)
                vt = vnb[rows, cols]
                mixed = _dot(wm_ref[g], vt) + bias_ref[g]
                dyt = dyv[rows, cols]
                du_ref[rows, cols] = (dyt * mixed * gu_grad[rows, cols]).astype(BF16)
                dmix = dyt * gu[rows, cols]
                dmixb = dmix.astype(BF16)
                dvn_scr[rows, cols] = _dot(wmt_ref[g], dmixb)
                dws_ref[g] += _dot_nt(dmixb, vt) * mask_ref[...]
                dbs_acc[g] += dmix
        dvn = dvn_scr[...]
        dlng_ref[...] += jnp.sum(dvn * xhat, axis=0, keepdims=True)
        dlnb_ref[...] += jnp.sum(dvn, axis=0, keepdims=True)
        dxh = dvn * lng_ref[...]
        dgv = rs * (
            dxh - jnp.mean(dxh, axis=-1, keepdims=True) - xhat * jnp.mean(dxh * xhat, axis=-1, keepdims=True)
        )
        dv_ref[...] = (dgv * gv_grad).astype(BF16)

        @pl.when(n == nb - 1)
        def _():
            for g in range(SGU_GROUPS):
                dbs_ref[g] = jnp.broadcast_to(jnp.sum(dbs_acc[g], axis=-1, keepdims=True), (SGU_BLOCK, LANES))

    const2 = lambda n: (0, 0)
    const3 = lambda n: (0, 0, 0)
    gmat = pl.BlockSpec((SGU_GROUPS, SGU_BLOCK, SGU_BLOCK), const3)
    vec = pl.BlockSpec((1, D_SGU), const2)
    act = pl.BlockSpec((tb, D_SGU), lambda n: (n, 0))
    return _call(
        body,
        (dy, proj, proj, proj, proj, lng, lnb, wm, wmt, bias, mask),
        name=name,
        grid=(nb,),
        in_specs=[act] + _sgu_specs(tb) + [vec, vec, gmat, gmat, gmat, pl.BlockSpec((SGU_BLOCK, SGU_BLOCK), const2)],
        out_specs=[act, act, gmat, gmat, vec, vec],
        out_shape=[
            SDS((s, D_SGU), BF16),
            SDS((s, D_SGU), BF16),
            SDS((SGU_GROUPS, SGU_BLOCK, SGU_BLOCK), F32),
            SDS((SGU_GROUPS, SGU_BLOCK, LANES), F32),
            SDS((1, D_SGU), F32),
            SDS((1, D_SGU), F32),
        ],
        scratch_shapes=[pltpu.VMEM((tb, D_SGU), F32), pltpu.VMEM((SGU_GROUPS, SGU_BLOCK, LANES), F32)],
        semantics=("arbitrary",),
        comm=comm,
    )


def _gate_specs(tm):
    half = lambda blk: pl.BlockSpec((tm, 512), lambda i: (i, blk))
    return [half(GA_BLK512), half(GA_BLK512 + 1), half(GB_BLK512), half(GB_BLK512 + 1)]


def _merge_fwd(ya_pre, yb_pre, proj, x, w_ba, w_bb, w_out, *, tm, name, comm=None):
    s = x.shape[0]
    tm = min(tm, s)

    def body(ya_ref, yb_ref, a0, a1, b0, b1, x_ref, wa_ref, wb_ref, wo_ref, x1_ref, yao_ref, ybo_ref):
        ya = _dot(ya_ref[...], wa_ref[...])
        yb = _dot(yb_ref[...], wb_ref[...])
        sa = _sigmoid(jnp.concatenate([a0[...], a1[...]], axis=1).astype(F32))
        sb = _sigmoid(jnp.concatenate([b0[...], b1[...]], axis=1).astype(F32))
        merged = sa * ya + sb * yb
        x1_ref[...] = x_ref[...] + _dot(merged.astype(BF16), wo_ref[...])
        yao_ref[...] = ya.astype(BF16)
        ybo_ref[...] = yb.astype(BF16)

    whole = lambda r: pl.BlockSpec((r, D), lambda i: (0, 0))
    act = pl.BlockSpec((tm, D), lambda i: (i, 0))
    return _call(
        body,
        (ya_pre, yb_pre, proj, proj, proj, proj, x, w_ba, w_bb, w_out),
        name=name,
        grid=(s // tm,),
        in_specs=[pl.BlockSpec((tm, D_RNN), lambda i: (i, 0)), act] + _gate_specs(tm) + [act, whole(D_RNN), whole(D_SGU), whole(D)],
        out_specs=[act, act, act],
        out_shape=[SDS((s, D), F32), SDS((s, D), BF16), SDS((s, D), BF16)],
        semantics=("parallel",),
        comm=comm,
    )


def _merge_bwd(dx1, ya, yb, proj, w_ba, w_bb, w_out, *, tm, name, comm=None):
    s = dx1.shape[0]
    tm = min(tm, s)

    def body(dx_ref, ya_ref, yb_ref, a0, a1, b0, b1, wa_ref, wb_ref, wo_ref,
             mg_ref, dya_ref, dyb_ref, dga_ref, dgb_ref, dyap_ref, dybp_ref):
        dm = _dot_nt(dx_ref[...], wo_ref[...])
        ya = ya_ref[...].astype(F32)
        yb = yb_ref[...].astype(F32)
        sa = _sigmoid(jnp.concatenate([a0[...], a1[...]], axis=1).astype(F32))
        sb = _sigmoid(jnp.concatenate([b0[...], b1[...]], axis=1).astype(F32))
        mg_ref[...] = (sa * ya + sb * yb).astype(BF16)
        dya = (dm * sa).astype(BF16)
        dyb = (dm * sb).astype(BF16)
        dya_ref[...] = dya
        dyb_ref[...] = dyb
        dga_ref[...] = (dm * ya * sa * (1.0 - sa)).astype(BF16)
        dgb_ref[...] = (dm * yb * sb * (1.0 - sb)).astype(BF16)
        dyap_ref[...] = _dot_nt(dya, wa_ref[...]).astype(BF16)
        dybp_ref[...] = _dot_nt(dyb, wb_ref[...]).astype(BF16)

    whole = lambda r: pl.BlockSpec((r, D), lambda i: (0, 0))
    act = pl.BlockSpec((tm, D), lambda i: (i, 0))
    act_rnn = pl.BlockSpec((tm, D_RNN), lambda i: (i, 0))
    return _call(
        body,
        (dx1, ya, yb, proj, proj, proj, proj, w_ba, w_bb, w_out),
        name=name,
        grid=(s // tm,),
        in_specs=[act, act, act] + _gate_specs(tm) + [whole(D_RNN), whole(D_SGU), whole(D)],
        out_specs=[act, act, act, act, act, act_rnn, act],
        out_shape=[SDS((s, D), BF16)] * 5 + [SDS((s, D_RNN), BF16), SDS((s, D_SGU), BF16)],
        semantics=("parallel",),
        comm=comm,
    )


def _ffn_down_loss(a, w, res, g, target, *, tm, name):
    s, k = a.shape
    tm = min(tm, s)

    def body(a_ref, w_ref, r_ref, g_ref, t_ref, dx_ref, dxb_ref, dg_ref, loss_ref):
        @pl.when(pl.program_id(0) == 0)
        def _():
            dg_ref[...] = jnp.zeros_like(dg_ref)
            loss_ref[...] = jnp.zeros_like(loss_ref)

        t = jnp.maximum(a_ref[...].astype(F32), 0.0)
        xv = r_ref[...] + _dot((t * t).astype(BF16), w_ref[...])
        r = lax.rsqrt(jnp.mean(xv * xv, axis=-1, keepdims=True) + EPS)
        xhat = xv * r
        e = xhat * g_ref[...] - t_ref[...]
        loss_ref[...] += 0.5 * jnp.sum(jnp.mean(e * e, axis=-1, keepdims=True), axis=0, keepdims=True)
        dy = e * (1.0 / D)
        dxh = dy * g_ref[...]
        dx = r * (dxh - xhat * jnp.mean(dxh * xhat, axis=-1, keepdims=True))
        dx_ref[...] = dx
        dxb_ref[...] = dx.astype(BF16)
        dg_ref[...] += jnp.sum(dy * xhat, axis=0, keepdims=True)

    act = pl.BlockSpec((tm, D), lambda i: (i, 0))
    vec = pl.BlockSpec((1, D), lambda i: (0, 0))
    return pl.pallas_call(
        body,
        name=name,
        grid=(s // tm,),
        in_specs=[pl.BlockSpec((tm, k), lambda i: (i, 0)), pl.BlockSpec((k, D), lambda i: (0, 0)), act, vec, act],
        out_specs=[act, act, vec, pl.BlockSpec((SUBLANES, LANES), lambda i: (0, 0))],
        out_shape=[SDS((s, D), F32), SDS((s, D), BF16), SDS((1, D), F32), SDS((SUBLANES, LANES), F32)],
        compiler_params=_params("arbitrary"),
    )(a, w, res, g, target)


def _adamw_math(w, g, m, v):
    m2 = ADAM_B1 * m + (1.0 - ADAM_B1) * g
    v2 = ADAM_B2 * v + (1.0 - ADAM_B2) * (g * g)
    m_hat = m2 / (1.0 - ADAM_B1**ADAM_STEP)
    v_hat = v2 / (1.0 - ADAM_B2**ADAM_STEP)
    delta = -ADAM_LR * (m_hat / (jnp.sqrt(v_hat) + ADAM_EPS) + ADAM_WD * w)
    return delta, m2, v2


def _row_tile(rows, cap):
    return max(t for t in range(SUBLANES, min(cap, rows) + 1, SUBLANES) if rows % t == 0)


def _adamw_layers(w, grads, m, v, *, tr, name):
    depth, r, c = w.shape
    tr = _row_tile(r, tr)

    def body(*refs):
        g_refs = refs[:depth]
        w_ref, m_ref, v_ref, g_out, d_ref, mo_ref, vo_ref = refs[depth:]
        for l in range(depth):

            @pl.when(pl.program_id(0) == l)
            def _(l=l):
                g = g_refs[l][...]
                g_out[...] = g
                d_ref[...], mo_ref[...], vo_ref[...] = _adamw_math(w_ref[...], g, m_ref[...], v_ref[...])

    def of_layer(ll):
        return pl.BlockSpec((tr, c), lambda l, i: (jnp.where(l == ll, i, 0), 0))

    stacked = pl.BlockSpec((None, tr, c), lambda l, i: (l, i, 0))
    return pl.pallas_call(
        body,
        name=name,
        grid=(depth, r // tr),
        in_specs=[of_layer(ll) for ll in range(depth)] + [stacked] * 3,
        out_specs=[stacked] * 4,
        out_shape=[SDS((depth, r, c), F32)] * 4,
        compiler_params=_params("parallel", "parallel"),
    )(*grads, w, m, v)


def _adamw_reduced(w, parts, from_chips, m, v, chip, *, tr, name):
    depth, r, _ = w.shape
    tr = _row_tile(r, tr)

    def body(chip_ref, *refs):
        p_refs, c_refs = refs[:depth], refs[depth : 2 * depth]
        w_ref, m_ref, v_ref, g_out, d_ref, mo_ref, vo_ref = refs[2 * depth :]
        for l in range(depth):

            @pl.when(pl.program_id(0) == l)
            def _(l=l):
                got = c_refs[l]
                g = ((p_refs[l][...].astype(F32) + got[0].astype(F32)) + got[1].astype(F32)) + got[2].astype(F32)
                g_out[...] = g
                d_ref[...], mo_ref[...], vo_ref[...] = _adamw_math(w_ref[...], g, m_ref[...], v_ref[...])

    def mine_of_layer(ll):
        return pl.BlockSpec((None, tr, D), lambda l, i, chip_ref: (chip_ref[0], jnp.where(l == ll, i, 0), 0))

    def theirs_of_layer(ll):
        return pl.BlockSpec((3, tr, D), lambda l, i, chip_ref: (0, jnp.where(l == ll, i, 0), 0))

    stacked = pl.BlockSpec((None, tr, D), lambda l, i, chip_ref: (l, i, 0))
    return pl.pallas_call(
        body,
        name=name,
        grid_spec=pltpu.PrefetchScalarGridSpec(
            num_scalar_prefetch=1,
            grid=(depth, r // tr),
            in_specs=[mine_of_layer(ll) for ll in range(depth)]
            + [theirs_of_layer(ll) for ll in range(depth)]
            + [stacked] * 3,
            out_specs=[stacked] * 4,
        ),
        out_shape=[SDS((depth, r, D), F32)] * 4,
        compiler_params=_params("parallel", "parallel"),
    )(chip, *parts, *from_chips, w, m, v)


def _adamw_small(groups, *, name):
    n = len(groups)

    def body(*refs):
        ins, outs = refs[: 4 * n], refs[4 * n :]
        for i in range(n):
            w, g, m, v = (ref[...] for ref in ins[4 * i : 4 * i + 4])
            outs[3 * i][...], outs[3 * i + 1][...], outs[3 * i + 2][...] = _adamw_math(w, g, m, v)

    vmem = pl.BlockSpec(memory_space=pltpu.VMEM)
    outs = pl.pallas_call(
        body,
        name=name,
        in_specs=[vmem] * (4 * n),
        out_specs=[vmem] * (3 * n),
        out_shape=[SDS(grp[0].shape, F32) for grp in groups for _ in range(3)],
        compiler_params=pltpu.CompilerParams(vmem_limit_bytes=VMEM_LIMIT_BYTES),
    )(*[a for grp in groups for a in grp])
    return [tuple(outs[3 * i : 3 * i + 3]) for i in range(n)]


ANY = pl.BlockSpec(memory_space=pl.ANY)


def _position():
    return lax.axis_index("x"), lax.axis_index("y"), lax.axis_index("c")


def _other_chips(x, y):
    return [(1 - x, y), (x, 1 - y), (1 - x, 1 - y)]


class _Comm:
    def __init__(self, inputs, out_shapes, sem_counts, start, middle, finish, middle_at=1.0, aliases=()):
        self.inputs, self.out_shapes, self.sem_counts = list(inputs), list(out_shapes), list(sem_counts)
        self.start, self.middle, self.finish = start, middle, finish
        self.middle_at = middle_at
        self.aliases = list(aliases)

    def sem_shapes(self):
        return [pltpu.SemaphoreType.DMA((n,)) for n in self.sem_counts]


def _merge_comms(comms):
    bounds, i, o, s = [], 0, 0, 0
    for cm in comms:
        bounds.append((i, i + len(cm.inputs), o, o + len(cm.out_shapes), s, s + len(cm.sem_counts)))
        i, o, s = bounds[-1][1], bounds[-1][3], bounds[-1][5]

    def phase(which):
        def run(ins, outs, sems):
            for cm, (i0, i1, o0, o1, s0, s1) in zip(comms, bounds):
                getattr(cm, which)(ins[i0:i1], outs[o0:o1], sems[s0:s1])

        return run

    return _Comm(
        [a for cm in comms for a in cm.inputs],
        [a for cm in comms for a in cm.out_shapes],
        [a for cm in comms for a in cm.sem_counts],
        phase("start"),
        phase("middle"),
        phase("finish"),
        middle_at=max(cm.middle_at for cm in comms),
        aliases=[(i0 + i, o0 + o) for cm, (i0, _, o0, _, _, _) in zip(comms, bounds) for i, o in cm.aliases],
    )


def _call(body, args, *, semantics, comm=None, **kw):
    if comm is None:
        return pl.pallas_call(body, compiler_params=_params(*semantics), **kw)(*args)
    grid, in_specs, out_specs, out_shape = kw["grid"], kw["in_specs"], kw["out_specs"], kw["out_shape"]
    scratch = list(kw.get("scratch_shapes", ()))
    single = not isinstance(out_shape, (list, tuple))
    core_specs = [out_specs] if single else list(out_specs)
    core_shapes = [out_shape] if single else list(out_shape)
    n_in, n_out, n_scr = len(in_specs), len(core_shapes), len(scratch)
    n_cin, n_cout = len(comm.inputs), len(comm.out_shapes)
    steps = 1
    for g in grid:
        steps *= g
    middle = min(int(comm.middle_at * steps), steps - 1)

    def hosted(*refs):
        core_in, c_in = refs[:n_in], refs[n_in : n_in + n_cin]
        o0 = n_in + n_cin
        core_out, c_out = refs[o0 : o0 + n_out], refs[o0 + n_out : o0 + n_out + n_cout]
        s0 = o0 + n_out + n_cout
        core_scr, sems = refs[s0 : s0 + n_scr], refs[s0 + n_scr :]
        step = pl.program_id(0)
        for d in range(1, len(grid)):
            step = step * grid[d] + pl.program_id(d)

        @pl.when(step == 0)
        def _():
            comm.start(c_in, c_out, sems)

        body(*core_in, *core_out, *core_scr)

        @pl.when(step == middle)
        def _():
            comm.middle(c_in, c_out, sems)

        @pl.when(step == steps - 1)
        def _():
            comm.finish(c_in, c_out, sems)

    outs = pl.pallas_call(
        hosted,
        name=kw["name"],
        grid=grid,
        in_specs=list(in_specs) + [ANY] * n_cin,
        out_specs=core_specs + [ANY] * n_cout,
        out_shape=core_shapes + comm.out_shapes,
        scratch_shapes=scratch + comm.sem_shapes(),
        input_output_aliases={n_in + i: n_out + o for i, o in comm.aliases},
        compiler_params=_params(*(["arbitrary"] * len(grid))),
    )(*args, *comm.inputs)
    return (outs[0] if single else outs[:n_out]), outs[n_out:]


def _comm_only(comm, *, name):
    n_cin, n_cout = len(comm.inputs), len(comm.out_shapes)

    def body(*refs):
        ins, outs, sems = refs[:n_cin], refs[n_cin : n_cin + n_cout], refs[n_cin + n_cout :]
        comm.start(ins, outs, sems)
        comm.middle(ins, outs, sems)
        comm.finish(ins, outs, sems)

    return pl.pallas_call(
        body,
        name=name,
        in_specs=[ANY] * n_cin,
        out_specs=[ANY] * n_cout,
        out_shape=comm.out_shapes,
        scratch_shapes=comm.sem_shapes(),
    )(*comm.inputs)


def _gather_comm(shards, pass_on_at=1.0):
    n = len(shards)
    per = 7

    def plan(ins, outs, sems):
        send, recv, local = sems
        x, y, c = _position()
        me, sibling = (x, y, c), (x, y, 1 - c)
        chips = _other_chips(x, y)

        def block(t, px, py, pc):
            return outs[t].at[pl.ds(4 * px + 2 * py + pc, 1)]

        def copy(t, k, blk, to, src=None):
            return pltpu.make_async_remote_copy(
                src_ref=block(t, *blk) if src is None else src,
                dst_ref=block(t, *blk),
                send_sem=send.at[t * per + k],
                recv_sem=recv.at[t * per + k],
                device_id=to,
                device_id_type=MESH,
            )

        mine = [pltpu.make_async_copy(ins[t], block(t, *me), local.at[t]) for t in range(n)]
        to_chips = [copy(t, 1 + j, me, (*chip, c), src=ins[t]) for t in range(n) for j, chip in enumerate(chips)]
        to_sibling = [copy(t, 0, me, sibling, src=ins[t]) for t in range(n)]
        from_chips = [copy(t, 1 + j, (*chip, c), me) for t in range(n) for j, chip in enumerate(chips)]
        passed_on = [copy(t, 4 + j, (*chip, c), sibling) for t in range(n) for j, chip in enumerate(chips)]
        from_sibling = [copy(t, 0, sibling, me) for t in range(n)]
        from_sibling += [copy(t, 4 + j, (*chip, 1 - c), me) for t in range(n) for j, chip in enumerate(chips)]
        return mine, to_chips, to_sibling, from_chips, passed_on, from_sibling

    def start(ins, outs, sems):
        mine, to_chips, to_sibling, _, _, _ = plan(ins, outs, sems)
        for cp in mine + to_chips + to_sibling:
            cp.start()

    def middle(ins, outs, sems):
        _, _, _, from_chips, passed_on, _ = plan(ins, outs, sems)
        for arrived, onward in zip(from_chips, passed_on):
            arrived.wait_recv()
            onward.start()

    def finish(ins, outs, sems):
        mine, to_chips, to_sibling, _, passed_on, from_sibling = plan(ins, outs, sems)
        for cp in from_sibling:
            cp.wait_recv()
        for cp in to_chips + to_sibling + passed_on:
            cp.wait_send()
        for cp in mine:
            cp.wait()

    out_shapes = [SDS((N_DEV,) + sh.shape[1:], sh.dtype) for sh in shards]
    return _Comm(shards, out_shapes, [n * per, n * per, n], start, middle, finish, middle_at=pass_on_at)


def _gather_halves(shards=None, arrived=None):
    first_half = arrived is None
    arrays = shards if first_half else arrived
    n = len(arrays)
    per = 4 if first_half else 3

    def plan(ins, outs, sems):
        x, y, c = _position()
        me, sibling = (x, y, c), (x, y, 1 - c)
        chips = _other_chips(x, y)

        def block(t, px, py, pc):
            return outs[t].at[pl.ds(4 * px + 2 * py + pc, 1)]

        def copy(t, k, blk, to, src=None):
            return pltpu.make_async_remote_copy(
                src_ref=block(t, *blk) if src is None else src,
                dst_ref=block(t, *blk),
                send_sem=sems[0].at[t * per + k],
                recv_sem=sems[1].at[t * per + k],
                device_id=to,
                device_id_type=MESH,
            )

        if first_half:
            local = [pltpu.make_async_copy(ins[t], block(t, *me), sems[2].at[t]) for t in range(n)]
            sent = [copy(t, 1 + j, me, (*chip, c), src=ins[t]) for t in range(n) for j, chip in enumerate(chips)]
            sent += [copy(t, 0, me, sibling, src=ins[t]) for t in range(n)]
            landing = [copy(t, 1 + j, (*chip, c), me) for t in range(n) for j, chip in enumerate(chips)]
            landing += [copy(t, 0, sibling, me) for t in range(n)]
        else:
            local = []
            sent = [copy(t, j, (*chip, c), sibling) for t in range(n) for j, chip in enumerate(chips)]
            landing = [copy(t, j, (*chip, 1 - c), me) for t in range(n) for j, chip in enumerate(chips)]
        return local, sent, landing

    def start(ins, outs, sems):
        local, sent, _ = plan(ins, outs, sems)
        for cp in local + sent:
            cp.start()

    def middle(ins, outs, sems):
        pass

    def finish(ins, outs, sems):
        local, sent, landing = plan(ins, outs, sems)
        for cp in landing:
            cp.wait_recv()
        for cp in sent:
            cp.wait_send()
        for cp in local:
            cp.wait()

    if first_half:
        out_shapes = [SDS((N_DEV,) + sh.shape[1:], sh.dtype) for sh in shards]
        return _Comm(shards, out_shapes, [n * per, n * per, n], start, middle, finish)
    out_shapes = [SDS(a.shape, a.dtype) for a in arrived]
    return _Comm(arrived, out_shapes, [n * per, n * per], start, middle, finish, aliases=[(t, t) for t in range(n)])


def _exchange_comm(arrays, out_shapes, n_copies, copies_of):
    def start(ins, outs, sems):
        for cp in copies_of(ins, outs, *sems):
            cp.start()

    def middle(ins, outs, sems):
        pass

    def finish(ins, outs, sems):
        for cp in copies_of(ins, outs, *sems):
            cp.wait()

    return _Comm(arrays, out_shapes, [n_copies, n_copies], start, middle, finish)


def _sibling_comm(grads):
    def copies_of(ins, outs, send, recv):
        x, y, c = _position()
        return [
            pltpu.make_async_remote_copy(
                src_ref=ins[t].at[:, pl.ds(1 - c, 1)],
                dst_ref=outs[t],
                send_sem=send.at[t],
                recv_sem=recv.at[t],
                device_id=(x, y, 1 - c),
                device_id_type=MESH,
            )
            for t in range(len(ins))
        ]

    return _exchange_comm(grads, [SDS((4, 1) + g.shape[2:], g.dtype) for g in grads], len(grads), copies_of)


def _chips_comm(parts):
    def copies_of(ins, outs, send, recv):
        x, y, c = _position()
        return [
            pltpu.make_async_remote_copy(
                src_ref=ins[t].at[pl.ds(2 * px + py, 1)],
                dst_ref=outs[t].at[pl.ds(k, 1)],
                send_sem=send.at[3 * t + k],
                recv_sem=recv.at[3 * t + k],
                device_id=(px, py, c),
                device_id_type=MESH,
            )
            for t in range(len(ins))
            for k, (px, py) in enumerate(_other_chips(x, y))
        ]

    return _exchange_comm(parts, [SDS((3,) + p.shape[1:], p.dtype) for p in parts], 3 * len(parts), copies_of)


def _sum_with_sibling(grad, got, core, *, name):
    rows = grad.shape[2]

    def body(core_ref, a_ref, b_ref, o_ref):
        o_ref[...] = (a_ref[...].astype(F32) + b_ref[...].astype(F32)).astype(o_ref.dtype)

    return pl.pallas_call(
        body,
        name=name,
        grid_spec=pltpu.PrefetchScalarGridSpec(
            num_scalar_prefetch=1,
            grid=(4,),
            in_specs=[
                pl.BlockSpec((None, None, rows, D), lambda q, core_ref: (q, core_ref[0], 0, 0)),
                pl.BlockSpec((None, None, rows, D), lambda q, core_ref: (q, 0, 0, 0)),
            ],
            out_specs=pl.BlockSpec((None, rows, D), lambda q, core_ref: (q, 0, 0)),
        ),
        out_shape=SDS((4, rows, D), grad.dtype),
        compiler_params=_params("parallel"),
    )(core, grad, got)


def _sum_chips(part, got, chip, *, name):
    rows = part.shape[1]

    def body(chip_ref, a_ref, b_ref, o_ref):
        o_ref[...] = ((a_ref[...].astype(F32) + b_ref[0].astype(F32)) + b_ref[1].astype(F32)) + b_ref[2].astype(F32)

    return pl.pallas_call(
        body,
        name=name,
        grid_spec=pltpu.PrefetchScalarGridSpec(
            num_scalar_prefetch=1,
            grid=(1,),
            in_specs=[
                pl.BlockSpec((None, rows, D), lambda i, chip_ref: (chip_ref[0], 0, 0)),
                pl.BlockSpec((3, rows, D), lambda i, chip_ref: (0, 0, 0)),
            ],
            out_specs=pl.BlockSpec((rows, D), lambda i, chip_ref: (0, 0)),
        ),
        out_shape=SDS((rows, D), F32),
        compiler_params=_params("arbitrary"),
    )(chip, part, got)


def _all_reduce_small(pack, *, name):
    rows = pack.shape[1]

    def body(in_ref, out_ref, from_sibling, part, from_chips, send, recv):
        x, y, c = _position()
        me, sibling = (x, y, c), (x, y, 1 - c)
        chips = _other_chips(x, y)
        waiting = []

        def copy(k, src, dst, to):
            return pltpu.make_async_remote_copy(
                src_ref=src, dst_ref=dst, send_sem=send.at[k], recv_sem=recv.at[k], device_id=to, device_id_type=MESH
            )

        def exchange(copies):
            for cp in copies:
                cp.start()
            for cp in copies:
                cp.wait_recv()
            waiting.extend(copies)

        def block(px, py, pc):
            return out_ref.at[4 * px + 2 * py + pc]

        exchange([copy(q, in_ref.at[2 * q + 1 - c], from_sibling.at[q], sibling) for q in range(4)])
        for q in range(4):
            part[q] = in_ref[2 * q + c] + from_sibling[q]
        exchange([copy(4 + k, part.at[2 * px + py], from_chips.at[k], (px, py, c)) for k, (px, py) in enumerate(chips)])
        out_ref[4 * x + 2 * y + c] = ((part[2 * x + y] + from_chips[0]) + from_chips[1]) + from_chips[2]
        exchange(
            [copy(7, block(*me), block(*me), sibling)]
            + [copy(8 + k, block(*me), block(*me), (px, py, c)) for k, (px, py) in enumerate(chips)]
        )
        exchange([copy(11 + k, block(px, py, c), block(px, py, c), sibling) for k, (px, py) in enumerate(chips)])
        for cp in waiting:
            cp.wait_send()

    vmem = pl.BlockSpec(memory_space=pltpu.VMEM)
    return pl.pallas_call(
        body,
        name=name,
        in_specs=[vmem],
        out_specs=vmem,
        out_shape=SDS(pack.shape, F32),
        scratch_shapes=[
            pltpu.VMEM((4, rows, D), F32),
            pltpu.VMEM((4, rows, D), F32),
            pltpu.VMEM((3, rows, D), F32),
            pltpu.SemaphoreType.DMA((14,)),
            pltpu.SemaphoreType.DMA((14,)),
        ],
        compiler_params=pltpu.CompilerParams(vmem_limit_bytes=VMEM_LIMIT_BYTES),
    )(pack)


def _pack(arrays, rows):
    flat = jnp.concatenate([a.reshape(-1).astype(F32) for a in arrays])
    return jnp.pad(flat, (0, rows * D - flat.shape[0])).reshape(rows, D)


def _unpack(pack, shapes):
    flat = pack.reshape(-1)
    out, off = [], 0
    for sh in shapes:
        size = 1
        for dim in sh:
            size *= dim
        out.append(flat[off : off + size].reshape(sh))
        off += size
    return out


def _block_diag_pairs(w):
    w = w.reshape(N_RNN_TILES, 2, HEAD_DIM, HEAD_DIM)
    z = jnp.zeros_like(w[:, 0])
    top = jnp.concatenate([w[:, 0], z], axis=2)
    bot = jnp.concatenate([z, w[:, 1]], axis=2)
    return jnp.concatenate([top, bot], axis=1)


def _diag_blocks(w2):
    a = w2[:, :HEAD_DIM, :HEAD_DIM]
    b = w2[:, HEAD_DIM:, HEAD_DIM:]
    return jnp.stack([a, b], axis=1).reshape(RNN_HEADS, HEAD_DIM, HEAD_DIM)


BIG = ("w_in", "w_branch_a", "w_branch_b", "w_out", "w_up", "w_down")
TRANSPOSED = ("w_in", "w_up")
SMALL = (
    "norm_mix_g", "conv_w", "conv_b", "lru_w_a", "lru_b_a", "lru_w_x", "lru_b_x", "lru_lambda",
    "sgu_ln_g", "sgu_ln_b", "sgu_w_s", "sgu_b_s", "norm_ffn_g", "final_norm_g",
)
WEIGHTS = (
    "norm_mix_g", "w_in", "conv_w", "conv_b", "lru_w_a", "lru_b_a", "lru_w_x", "lru_b_x", "lru_lambda", "sgu_ln_g",
    "sgu_ln_b", "sgu_w_s", "sgu_b_s", "w_branch_a", "w_branch_b", "w_out", "norm_ffn_g", "w_up", "w_down", "final_norm_g",
)

TM = 512
TM_NT = 1024
TN_IN = 1664
TN_UP = 2048
TKA = 512
TKA_PIECES = 256
TC = 512
TB = 256
TR = 256


_BRANCH_WEIGHTS = ("w_branch_a", "w_branch_b", "w_out")
GATHERS_RIDING = (
    {
        "in_proj": ([(0, name) for name in _BRANCH_WEIGHTS] + [(0, "w_up")], []),
        "branch_a_fwd": ([(1, "w_in")], [(0, name) for name in _BRANCH_WEIGHTS] + [(0, "w_up")]),
        "sgu_fwd": ([], [(1, "w_in")]),
        "merge_fwd": ([(0, "w_down")], []),
        "ffn_up": ([(1, name) for name in _BRANCH_WEIGHTS], [(0, "w_down")]),
        "ffn_down": ([], [(1, name) for name in _BRANCH_WEIGHTS]),
    },
    {"in_proj": ([(1, "w_down")], []), "branch_a_fwd": ([(1, "w_up")], [(1, "w_down")]), "sgu_fwd": ([], [(1, "w_up")])},
)


def _layer_forward(l, x, p, w, shards, arriving, loss_head=None):
    def run(key, fn, *args, **kw):
        first, second = GATHERS_RIDING[l].get(key, ((), ()))
        comms = []
        if first:
            comms.append(_gather_halves(shards=[shards[l2][n2] for l2, n2 in first]))
        if second:
            comms.append(_gather_halves(arrived=[arriving.pop(k) for k in second]))
        if not comms:
            return fn(*args, **kw)
        out, got = fn(*args, comm=_merge_comms(comms), **kw)
        arriving.update(zip(first, got[: len(first)]))
        for (l2, n2), full in zip(second, got[len(first) :]):
            w[l2][n2] = full.reshape(-1, D)
        return out

    proj, h = run("in_proj", _norm_matmul_nt, x, p["norm_mix_g"], w[l]["w_in"], tm=TM_NT, tn=TN_IN, name=f"in_proj_{l}")
    hseq, ya_pre = run(
        "branch_a_fwd", _branch_a_fwd, proj, p["conv_w"], p["conv_b"], p["wa2"], p["lru_b_a"], p["wx2"], p["lru_b_x"],
        p["lru_lambda"], tc=TC, name=f"branch_a_fwd_{l}",
    )
    yb_pre = run("sgu_fwd", _sgu_fwd, proj, p["sgu_ln_g"], p["sgu_ln_b"], p["wm"], p["sgu_bias"], tb=TB, name=f"sgu_fwd_{l}")
    x1, ya, yb = run(
        "merge_fwd", _merge_fwd, ya_pre, yb_pre, proj, x, w[l]["w_branch_a"], w[l]["w_branch_b"], w[l]["w_out"], tm=TM,
        name=f"merge_fwd_{l}",
    )
    f_pre, h2 = run("ffn_up", _norm_matmul_nt, x1, p["norm_ffn_g"], w[l]["w_up"], tm=TM_NT, tn=TN_UP, name=f"ffn_up_{l}")
    saved = dict(x=x, h=h, proj=proj, hseq=hseq, ya_pre=ya_pre, yb_pre=yb_pre, ya=ya, yb=yb, x1=x1, h2=h2, f_pre=f_pre)
    if loss_head is None:
        return run("ffn_down", _matmul_nn_res, f_pre, w[l]["w_down"], x1, relu2=True, tm=TM, name=f"ffn_down_{l}"), saved
    return _ffn_down_loss(f_pre, w[l]["w_down"], x1, *loss_head, tm=TM, name=f"ffn_down_loss_{l}"), saved


def _layer_backward(l, dx2, dx2b, sv, p, w, core, waiting, last):
    parts, from_chips = {}, {}

    def by_device(g):
        return g.reshape(4, 2, -1, D)

    def with_sibling(name, g, got):
        parts[name] = _sum_with_sibling(by_device(g), got, core, name=f"sum_sibling_{name}_{l}")

    df_pre = _matmul_nt_drelu2(dx2b, w["w_down"], sv["f_pre"], tm=TM_NT, tn=TN_UP, name=f"ffn_down_bwd_{l}")
    g_down = _matmul_tn([sv["f_pre"]], dx2b, relu2=True, tka=TKA, name=f"grad_w_down_{l}")
    g_up, (got,) = _matmul_tn(
        [df_pre], sv["h2"], relu2=False, tka=TKA, name=f"grad_w_up_{l}", comm=_sibling_comm([by_device(g_down)])
    )
    with_sibling("w_down", g_down, got)
    (dx1, dx1b, g_norm_ffn), (got,) = _matmul_nn_rmsnorm_bwd(
        [df_pre], w["w_up"], sv["x1"], p["norm_ffn_g"], dx2, tm=TM, name=f"ffn_up_bwd_{l}",
        comm=_sibling_comm([by_device(g_up)]),
    )
    with_sibling("w_up", g_up, got)
    (merged, dya, dyb, dga, dgb, dya_pre, dyb_pre), (from_chips[l, "w_up"],) = _merge_bwd(
        dx1b, sv["ya"], sv["yb"], sv["proj"], w["w_branch_a"], w["w_branch_b"], w["w_out"], tm=TM, name=f"merge_bwd_{l}",
        comm=_chips_comm([parts["w_up"]]),
    )
    g_out = _matmul_tn([merged], dx1b, relu2=False, tka=TKA, name=f"grad_w_out_{l}")
    g_ba = _matmul_tn([sv["ya_pre"]], dya, relu2=False, tka=TKA_PIECES, name=f"grad_w_branch_a_{l}")
    g_bb = _matmul_tn([sv["yb_pre"]], dyb, relu2=False, tka=TKA, name=f"grad_w_branch_b_{l}")
    branch = (("w_out", g_out), ("w_branch_a", g_ba), ("w_branch_b", g_bb))
    (du, dv, g_ws, g_bs, g_lng, g_lnb), got = _sgu_bwd(
        dyb_pre, sv["proj"], p["sgu_ln_g"], p["sgu_ln_b"], p["wm"], p["wmt"], p["sgu_bias"], p["mask"], tb=TB,
        name=f"sgu_bwd_{l}",
        comm=_merge_comms([_sibling_comm([by_device(g) for _, g in branch]), _chips_comm([parts["w_down"]])]),
    )
    from_chips[l, "w_down"] = got[-1]
    for (name, g), landed in zip(branch, got):
        with_sibling(name, g, landed)
    riding = [((l, name), parts[name]) for name, _ in branch] + list(waiting)
    (dxr, dgr, g_cw, g_cb, g_ba_, g_bx, g_lam, g_wa2, g_wx2), got = _branch_a_bwd(
        dya_pre, sv["proj"], sv["hseq"], p["conv_w"], p["conv_b"], p["wa2"], p["lru_b_a"], p["wx2"], p["lru_b_x"],
        p["lru_lambda"], p["wa2t"], p["wx2t"], tc=TC, name=f"branch_a_bwd_{l}", comm=_chips_comm([part for _, part in riding]),
    )
    for (key, _), landed in zip(riding, got):
        from_chips[key] = landed
    dproj = [dxr, dgr, du, dv, dga, dgb]
    g_in = _matmul_tn(dproj, sv["h"], relu2=False, tka=TKA_PIECES, name=f"grad_w_in_{l}")
    if last:
        (got,) = _comm_only(_sibling_comm([by_device(g_in)]), name=f"grad_w_in_to_sibling_{l}")
        with_sibling("w_in", g_in, got)
        riding = _chips_comm([parts["w_in"]])
    else:
        riding = _sibling_comm([by_device(g_in)])
    (dx, dxb, g_norm_mix), (got,) = _matmul_nn_rmsnorm_bwd(
        dproj, w["w_in"], sv["x"], p["norm_mix_g"], dx1, tm=TM, name=f"in_proj_bwd_{l}", comm=riding
    )
    if last:
        from_chips[l, "w_in"] = got
    else:
        with_sibling("w_in", g_in, got)
    small = dict(
        norm_mix_g=g_norm_mix[0], conv_w=g_cw, conv_b=g_cb[0], lru_w_a=_diag_blocks(g_wa2), lru_b_a=g_ba_.reshape(RNN_HEADS, HEAD_DIM),
        lru_w_x=_diag_blocks(g_wx2), lru_b_x=g_bx.reshape(RNN_HEADS, HEAD_DIM), lru_lambda=g_lam[0], sgu_ln_g=g_lng[0],
        sgu_ln_b=g_lnb[0], sgu_w_s=g_ws, sgu_b_s=g_bs[:, :, 0], norm_ffn_g=g_norm_ffn[0],
    )
    return dx, dxb, small, parts, from_chips


def _prepare_small(l, given):
    chunk_id = jnp.arange(SGU_BLOCK) // CHUNK
    mask = (chunk_id[:, None] >= chunk_id[None, :]).astype(F32)
    wm = given["sgu_w_s"][l] * mask
    wa2 = _block_diag_pairs(given["lru_w_a"][l])
    wx2 = _block_diag_pairs(given["lru_w_x"][l])
    row = lambda a: a.reshape(1, -1)
    return dict(
        norm_mix_g=row(given["norm_mix_g"][l]),
        norm_ffn_g=row(given["norm_ffn_g"][l]),
        conv_w=given["conv_w_full"][l],
        conv_b=row(given["conv_b"][l]),
        wa2=wa2.astype(BF16),
        wx2=wx2.astype(BF16),
        wa2t=jnp.swapaxes(wa2, 1, 2).astype(BF16),
        wx2t=jnp.swapaxes(wx2, 1, 2).astype(BF16),
        lru_b_a=row(given["lru_b_a"][l]),
        lru_b_x=row(given["lru_b_x"][l]),
        lru_lambda=row(given["lru_lambda"][l]),
        sgu_ln_g=row(given["sgu_ln_g"][l]),
        sgu_ln_b=row(given["sgu_ln_b"][l]),
        wm=wm.astype(BF16),
        wmt=jnp.swapaxes(wm, 1, 2).astype(BF16),
        sgu_bias=jnp.broadcast_to(given["sgu_b_s"][l][:, :, None], (SGU_GROUPS, SGU_BLOCK, LANES)),
        mask=mask,
    )


def _step(given):
    x_idx, y_idx, c_idx = _position()
    dev = 4 * x_idx + 2 * y_idx + c_idx
    core = c_idx.astype(jnp.int32).reshape(1)
    chip = (2 * x_idx + y_idx).astype(jnp.int32).reshape(1)

    def rows_first(name, a):
        return jnp.swapaxes(a, 1, 2) if name in TRANSPOSED else a

    shards = []
    for l in range(DEPTH):
        shards.append({name: rows_first(name, given[name])[l].astype(BF16)[None] for name in BIG})
    conv_mine = given["conv_w"].reshape(1, DEPTH * CONV_WIDTH, D_RNN // N_DEV)
    w_in_first, conv_all = _comm_only(_gather_comm([shards[0]["w_in"], conv_mine]), name="gather_first")
    weights = [{"w_in": w_in_first.reshape(-1, D)}, {}]
    conv_all = conv_all.reshape(N_DEV, DEPTH, CONV_WIDTH, D_RNN // N_DEV)
    given = dict(given, conv_w_full=jnp.moveaxis(conv_all, 0, 2).reshape(DEPTH, CONV_WIDTH, D_RNN))

    small_params = [_prepare_small(l, given) for l in range(DEPTH)]
    x = given["x"][0]
    saved, arriving = [], {}
    loss_head = (given["final_norm_g"].reshape(1, D), given["loss_target"][0])
    for l in range(DEPTH):
        x, sv = _layer_forward(
            l, x, small_params[l], weights, shards, arriving, loss_head=loss_head if l == DEPTH - 1 else None
        )
        saved.append(sv)
    dx, dxb, g_final, loss = x
    small_grads, parts, from_chips, waiting = [None] * DEPTH, [None] * DEPTH, {}, []
    for l in reversed(range(DEPTH)):
        dx, dxb, small_grads[l], parts[l], got = _layer_backward(
            l, dx, dxb, saved[l], small_params[l], weights[l], core, waiting, last=l == 0
        )
        from_chips.update(got)
        waiting = [((l, "w_in"), parts[l]["w_in"])]

    small_list = []
    for name in SMALL[:-1]:
        small_list.append(jnp.stack([small_grads[l][name] for l in range(DEPTH)]))
    small_list += [g_final[0], loss[0, :1]]
    small_shapes = [a.shape for a in small_list]
    pack = _pack(small_list, SMALL_ROWS).reshape(N_DEV, SMALL_ROWS_PER_DEV, D)
    summed = _unpack(_all_reduce_small(pack, name="all_reduce_small"), small_shapes)
    loss_total = summed[-1][0]
    grads = dict(zip(SMALL, summed[:-1]))
    cw = grads["conv_w"].reshape(DEPTH, CONV_WIDTH, N_DEV, D_RNN // N_DEV)
    grads["conv_w"] = lax.dynamic_index_in_dim(cw, dev, axis=2, keepdims=False)

    delta, new_m, new_v = {}, {}, {}
    for name in BIG:
        w, m, v = given[name], given["m_" + name], given["v_" + name]
        mine = [parts[l][name] for l in range(DEPTH)]
        theirs = [from_chips[l, name] for l in range(DEPTH)]
        if name == "w_up":
            sums = [_sum_chips(mine[l], theirs[l], chip, name=f"sum_chips_{name}_{l}").T for l in range(DEPTH)]
            out = _adamw_layers(w, sums, m, v, tr=TR, name=f"adamw_{name}")
        else:
            out = _adamw_reduced(
                rows_first(name, w), mine, theirs, rows_first(name, m), rows_first(name, v), chip, tr=TR, name=f"adamw_{name}"
            )
            out = [rows_first(name, a) for a in out]
        grads[name], delta[name], new_m[name], new_v[name] = out
    two_d = lambda a: a.reshape(1, -1) if a.ndim == 1 else a
    groups = [tuple(two_d(a) for a in (given[n], grads[n], given["m_" + n], given["v_" + n])) for n in SMALL]
    for n, (d, m2, v2) in zip(SMALL, _adamw_small(groups, name="adamw_small")):
        shape = given[n].shape
        delta[n], new_m[n], new_v[n] = d.reshape(shape), m2.reshape(shape), v2.reshape(shape)

    return (
        loss_total, dx[None],
        *[grads[n] for n in WEIGHTS], *[delta[n] for n in WEIGHTS], *[new_m[n] for n in WEIGHTS], *[new_v[n] for n in WEIGHTS],
    )


def kernel(x, norm_mix_g, w_in, conv_w, conv_b, lru_w_a, lru_b_a, lru_w_x, lru_b_x, lru_lambda, sgu_ln_g, sgu_ln_b, sgu_w_s, sgu_b_s, w_branch_a, w_branch_b, w_out, norm_ffn_g, w_up, w_down, final_norm_g, loss_target, m_norm_mix_g, m_w_in, m_conv_w, m_conv_b, m_lru_w_a, m_lru_b_a, m_lru_w_x, m_lru_b_x, m_lru_lambda, m_sgu_ln_g, m_sgu_ln_b, m_sgu_w_s, m_sgu_b_s, m_w_branch_a, m_w_branch_b, m_w_out, m_norm_ffn_g, m_w_up, m_w_down, m_final_norm_g, v_norm_mix_g, v_w_in, v_conv_w, v_conv_b, v_lru_w_a, v_lru_b_a, v_lru_w_x, v_lru_b_x, v_lru_lambda, v_sgu_ln_g, v_sgu_ln_b, v_sgu_w_s, v_sgu_b_s, v_w_branch_a, v_w_branch_b, v_w_out, v_norm_ffn_g, v_w_up, v_w_down, v_final_norm_g):
    return _step(dict(locals()))
```

```python
import jax
import jax.numpy as jnp
from jax import lax
from jax.experimental import pallas as pl
from jax.experimental.pallas import tpu as pltpu

F32 = jnp.float32
BF16 = jnp.bfloat16
SDS = jax.ShapeDtypeStruct
MESH = pl.DeviceIdType.MESH

D = 1024
D_RNN = 1280
D_SGU = 1024
D_FF = 4096
D_IN = 2 * D_RNN + 2 * D_SGU + 2 * D
DEPTH = 2
RNN_HEADS = 20
HEAD_DIM = 64
CONV_WIDTH = 4
LRU_C = 8.0
SGU_GROUPS = 8
SGU_BLOCK = 128
CHUNK = 64
EPS = 1e-6
N_DEV = 8

ADAM_LR = 0.001
ADAM_B1 = 0.9
ADAM_B2 = 0.999
ADAM_EPS = 1e-08
ADAM_WD = 0.01
ADAM_STEP = 10

LANES = 128
SUBLANES = 8
VMEM_LIMIT_BYTES = 56 * 1024 * 1024

N_RNN_TILES = D_RNN // LANES
RNN_TILES_PER_STEP = 5
U_BLK512 = (2 * D_RNN) // 512
V_BLK512 = (2 * D_RNN + D_SGU) // 512
GA_BLK512 = (2 * D_RNN + 2 * D_SGU) // 512
GB_BLK512 = (2 * D_RNN + 2 * D_SGU + D) // 512

SMALL_ROWS_PER_DEV = 80
SMALL_ROWS = N_DEV * SMALL_ROWS_PER_DEV


def _params(*sem):
    return pltpu.CompilerParams(dimension_semantics=sem, vmem_limit_bytes=VMEM_LIMIT_BYTES)


def _sigmoid(x):
    return 0.5 + 0.5 * jnp.tanh(0.5 * x)


_GELU_C = 0.7978845608028654
_GELU_K = 0.044715


def _gelu(x):
    t = jnp.tanh(_GELU_C * (x + _GELU_K * x * x * x))
    return 0.5 * x * (1.0 + t)


def _gelu_and_grad(x):
    t = jnp.tanh(_GELU_C * (x + _GELU_K * x * x * x))
    val = 0.5 * x * (1.0 + t)
    grad = 0.5 * (1.0 + t) + 0.5 * x * (1.0 - t * t) * _GELU_C * (1.0 + 3.0 * _GELU_K * x * x)
    return val, grad


def _one_minus_square(log_a, a):
    return -jnp.tanh(log_a) * (1.0 + a * a)


def _dot(a, b):
    return jnp.dot(a, b, preferred_element_type=F32)


def _dot_nt(a, b):
    return lax.dot_general(a, b, (((1,), (1,)), ((), ())), preferred_element_type=F32)


def _dot_tn(a, b):
    return lax.dot_general(a, b, (((0,), (0,)), ((), ())), preferred_element_type=F32)


def _norm_matmul_nt(x, g, w, *, tm, tn, name, comm=None):
    s, n = x.shape[0], w.shape[0]
    tm, tn = min(tm, s), min(tn, n)

    def body(x_ref, g_ref, w_ref, o_ref, h_ref):
        @pl.when(pl.program_id(1) == 0)
        def _():
            xv = x_ref[...]
            r = lax.rsqrt(jnp.mean(xv * xv, axis=-1, keepdims=True) + EPS)
            h_ref[...] = (xv * r * g_ref[...]).astype(BF16)

        o_ref[...] = _dot_nt(h_ref[...], w_ref[...]).astype(o_ref.dtype)

    return _call(
        body,
        (x, g, w),
        name=name,
        grid=(s // tm, n // tn),
        in_specs=[
            pl.BlockSpec((tm, D), lambda i, j: (i, 0)),
            pl.BlockSpec((1, D), lambda i, j: (0, 0)),
            pl.BlockSpec((tn, D), lambda i, j: (j, 0)),
        ],
        out_specs=[pl.BlockSpec((tm, tn), lambda i, j: (i, j)), pl.BlockSpec((tm, D), lambda i, j: (i, 0))],
        out_shape=[SDS((s, n), BF16), SDS((s, D), BF16)],
        semantics=("parallel", "arbitrary"),
        comm=comm,
    )


def _matmul_nn_res(a, w, res, *, relu2, tm, name, comm=None):
    s, k = a.shape
    tm = min(tm, s)

    def body(a_ref, w_ref, r_ref, o_ref):
        av = a_ref[...]
        if relu2:
            t = jnp.maximum(av.astype(F32), 0.0)
            av = (t * t).astype(BF16)
        o_ref[...] = r_ref[...] + _dot(av, w_ref[...])

    return _call(
        body,
        (a, w, res),
        name=name,
        grid=(s // tm,),
        in_specs=[
            pl.BlockSpec((tm, k), lambda i: (i, 0)),
            pl.BlockSpec((k, D), lambda i: (0, 0)),
            pl.BlockSpec((tm, D), lambda i: (i, 0)),
        ],
        out_specs=pl.BlockSpec((tm, D), lambda i: (i, 0)),
        out_shape=SDS((s, D), F32),
        semantics=("parallel",),
        comm=comm,
    )


def _matmul_nt_drelu2(a, w, pre, *, tm, tn, name):
    s, n = a.shape[0], w.shape[0]
    tm, tn = min(tm, s), min(tn, n)

    def body(a_ref, w_ref, p_ref, o_ref):
        d = _dot_nt(a_ref[...], w_ref[...])
        o_ref[...] = (d * (2.0 * jnp.maximum(p_ref[...].astype(F32), 0.0))).astype(o_ref.dtype)

    return pl.pallas_call(
        body,
        name=name,
        grid=(s // tm, n // tn),
        in_specs=[
            pl.BlockSpec((tm, D), lambda i, j: (i, 0)),
            pl.BlockSpec((tn, D), lambda i, j: (j, 0)),
            pl.BlockSpec((tm, tn), lambda i, j: (i, j)),
        ],
        out_specs=pl.BlockSpec((tm, tn), lambda i, j: (i, j)),
        out_shape=SDS((s, n), BF16),
        compiler_params=_params("parallel", "arbitrary"),
    )(a, w, pre)


def _matmul_tn(a_list, b, *, relu2, tka, name, comm=None):
    s = b.shape[0]
    n = len(a_list)
    nblk = [a.shape[1] // tka for a in a_list]
    starts = [sum(nblk[:p]) for p in range(n)]

    def body(*refs):
        a_refs, b_ref, o_ref = refs[:n], refs[n], refs[n + 1]
        i = pl.program_id(0)
        for p in range(n):

            @pl.when((i >= starts[p]) & (i < starts[p] + nblk[p]))
            def _(p=p):
                av = a_refs[p][...]
                if relu2:
                    t = jnp.maximum(av.astype(F32), 0.0)
                    av = (t * t).astype(BF16)
                o_ref[...] = _dot_tn(av, b_ref[...]).astype(o_ref.dtype)

    def piece_spec(p):
        return pl.BlockSpec((s, tka), lambda i: (0, jnp.clip(i - starts[p], 0, nblk[p] - 1)))

    return _call(
        body,
        (*a_list, b),
        name=name,
        grid=(sum(nblk),),
        in_specs=[piece_spec(p) for p in range(n)] + [pl.BlockSpec((s, D), lambda i: (0, 0))],
        out_specs=pl.BlockSpec((tka, D), lambda i: (i, 0)),
        out_shape=SDS((sum(nblk) * tka, D), BF16),
        semantics=("parallel",),
        comm=comm,
    )


def _matmul_nn_rmsnorm_bwd(a_list, w, x, g, res, *, tm, name, comm=None):
    s = x.shape[0]
    tm = min(tm, s)
    n = len(a_list)
    widths = [a.shape[1] for a in a_list]
    offs = [sum(widths[:p]) for p in range(n)]
    k = sum(widths)

    def body(*refs):
        a_refs = refs[:n]
        w_ref, x_ref, g_ref, r_ref, dx_ref, dxb_ref, dg_ref = refs[n:]

        @pl.when(pl.program_id(0) == 0)
        def _():
            dg_ref[...] = jnp.zeros_like(dg_ref)

        dh = _dot(a_refs[0][...], w_ref[0 : widths[0], :])
        for p in range(1, n):
            dh += _dot(a_refs[p][...], w_ref[offs[p] : offs[p] + widths[p], :])
        xv = x_ref[...]
        r = lax.rsqrt(jnp.mean(xv * xv, axis=-1, keepdims=True) + EPS)
        xhat = xv * r
        dxh = dh * g_ref[...]
        dx = r_ref[...] + r * (dxh - xhat * jnp.mean(dxh * xhat, axis=-1, keepdims=True))
        dx_ref[...] = dx
        dxb_ref[...] = dx.astype(BF16)
        dg_ref[...] += jnp.sum(dh * xhat, axis=0, keepdims=True)

    act = pl.BlockSpec((tm, D), lambda i: (i, 0))
    vec = pl.BlockSpec((1, D), lambda i: (0, 0))
    return _call(
        body,
        (*a_list, w, x, g, res),
        name=name,
        grid=(s // tm,),
        in_specs=[pl.BlockSpec((tm, wd), lambda i: (i, 0)) for wd in widths]
        + [pl.BlockSpec((k, D), lambda i: (0, 0), pipeline_mode=pl.Buffered(1)), act, vec, act],
        out_specs=[act, act, vec],
        out_shape=[SDS((s, D), F32), SDS((s, D), BF16), SDS((1, D), F32)],
        semantics=("arbitrary",),
        comm=comm,
    )


def _rows_before(ext, k):
    if k == 0:
        return ext[SUBLANES:, :]
    return pltpu.roll(ext, k, 0)[SUBLANES:, :]


def _rows_after(ext, k, n):
    if k == 0:
        return ext[:n, :]
    return pltpu.roll(ext, n + SUBLANES - k, 0)[:n, :]


def _scan_forward(a, b, n):
    row = lax.broadcasted_iota(jnp.int32, a.shape, 0)
    d = 1
    while d < n:
        if d < SUBLANES:
            m = row >= d
            a_s = jnp.where(m, pltpu.roll(a, d, 0), 1.0)
            b_s = jnp.where(m, pltpu.roll(b, d, 0), 0.0)
            b = a * b_s + b
            a = a * a_s
        else:
            b = jnp.concatenate([b[:d], a[d:] * b[: n - d] + b[d:]], axis=0)
            a = jnp.concatenate([a[:d], a[d:] * a[: n - d]], axis=0)
        d *= 2
    return a, b


def _scan_backward(a, b, n):
    row = lax.broadcasted_iota(jnp.int32, a.shape, 0)
    d = 1
    while d < n:
        if d < SUBLANES:
            m = row < n - d
            a_s = jnp.where(m, pltpu.roll(a, n - d, 0), 1.0)
            b_s = jnp.where(m, pltpu.roll(b, n - d, 0), 0.0)
            b = a * b_s + b
            a = a * a_s
        else:
            b = jnp.concatenate([a[: n - d] * b[d:] + b[: n - d], b[n - d :]], axis=0)
            a = jnp.concatenate([a[: n - d] * a[d:], a[n - d :]], axis=0)
        d *= 2
    return b


def _repeat_matrix(n):
    groups = n // SUBLANES
    return (jnp.arange(n)[:, None] // SUBLANES == jnp.arange(3 * groups)[None, :] % groups).astype(BF16)


def _scan_rows(a, b, n, repeat_ref, a_scr, b_scr, reverse):
    groups = n // SUBLANES
    a3 = a.reshape(groups, SUBLANES, LANES)
    b3 = b.reshape(groups, SUBLANES, LANES)
    sub = lax.broadcasted_iota(jnp.int32, a3.shape, 1)
    for d in (1, 2, 4):
        m = (sub < SUBLANES - d) if reverse else (sub >= d)
        shift = SUBLANES - d if reverse else d
        a_s = jnp.where(m, pltpu.roll(a3, shift, 1), 1.0)
        b_s = jnp.where(m, pltpu.roll(b3, shift, 1), 0.0)
        b3 = a3 * b_s + b3
        a3 = a3 * a_s
    a_scr[...] = a3.reshape(n, LANES)
    b_scr[...] = b3.reshape(n, LANES)
    edge = 0 if reverse else SUBLANES - 1
    a_tot = a_scr[pl.ds(edge, groups, stride=SUBLANES), :]
    b_tot = b_scr[pl.ds(edge, groups, stride=SUBLANES), :]
    row = lax.broadcasted_iota(jnp.int32, a_tot.shape, 0)
    if reverse:
        through = _scan_backward(a_tot, b_tot, groups)
        entering = jnp.where(row < groups - 1, pltpu.roll(through, groups - 1, 0), 0.0)
    else:
        _, through = _scan_forward(a_tot, b_tot, groups)
        entering = jnp.where(row >= 1, pltpu.roll(through, 1, 0), 0.0)
    hi = entering.astype(BF16)
    rest = entering - hi.astype(F32)
    mid = rest.astype(BF16)
    lo = (rest - mid.astype(F32)).astype(BF16)
    repeated = _dot(repeat_ref[...], jnp.concatenate([hi, mid, lo], axis=0))
    return b_scr[...] + a_scr[...] * repeated


def _softplus_neg(lam):
    z = -lam
    return jnp.maximum(z, 0.0) + jnp.log1p(jnp.exp(-jnp.abs(z)))


def _conv_and_gates(xc, xprev, cw_ref, cb_ref, wa_ref, ba_ref, wx_ref, bx_ref, lam_ref):
    ext = jnp.concatenate([xprev, xc], axis=0)
    x1, x2, x3 = _rows_before(ext, 1), _rows_before(ext, 2), _rows_before(ext, 3)
    xr = cb_ref[...] + x3 * cw_ref[0:1, :] + x2 * cw_ref[1:2, :] + x1 * cw_ref[2:3, :] + xc * cw_ref[3:4, :]
    xrb = xr.astype(BF16)
    r = _sigmoid(_dot(xrb, wa_ref[...]) + ba_ref[...])
    i = _sigmoid(_dot(xrb, wx_ref[...]) + bx_ref[...])
    sp = _softplus_neg(lam_ref[...])
    log_a = (-LRU_C * r) * sp
    a = jnp.exp(log_a)
    return xr, (x1, x2, x3), r, i, a, _one_minus_square(log_a, a)


def _branch_a_fwd(proj, cw, cb, wa2, ba, wx2, bx, lam, *, tc, name, comm=None):
    s = proj.shape[0]
    tc = min(tc, s)

    def body(x_ref, g_ref, cw_ref, cb_ref, wa_ref, ba_ref, wx_ref, bx_ref, lam_ref, rep_ref, h_ref, y_ref,
             xprev, hlast, a_scr, b_scr):
        @pl.when(pl.program_id(1) == 0)
        def _():
            xprev[...] = jnp.zeros_like(xprev)
            hlast[...] = jnp.zeros_like(hlast)

        for t in range(RNN_TILES_PER_STEP):
            cols = lambda ref: ref.at[:, pl.ds(t * LANES, LANES)]
            one_tile(
                cols(x_ref), cols(g_ref), cols(cw_ref), cols(cb_ref), wa_ref.at[t], cols(ba_ref), wx_ref.at[t], cols(bx_ref),
                cols(lam_ref), rep_ref, cols(h_ref), cols(y_ref), cols(xprev), cols(hlast), a_scr.at[t], b_scr.at[t],
            )

    def one_tile(x_ref, g_ref, cw_ref, cb_ref, wa_ref, ba_ref, wx_ref, bx_ref, lam_ref, rep_ref, h_ref, y_ref,
                 xprev, hlast, a_scr, b_scr):
        xc = x_ref[...].astype(F32)
        xr, _, r, i, a, om = _conv_and_gates(xc, xprev[...], cw_ref, cb_ref, wa_ref, ba_ref, wx_ref, bx_ref, lam_ref)
        xprev[...] = xc[tc - SUBLANES :, :]
        u = jnp.sqrt(om) * (i * xr)
        row8 = lax.broadcasted_iota(jnp.int32, (SUBLANES, LANES), 0)
        first = u[:SUBLANES] + jnp.where(row8 == 0, a[:SUBLANES] * hlast[SUBLANES - 1 : SUBLANES, :], 0.0)
        h = _scan_rows(a, jnp.concatenate([first, u[SUBLANES:]], axis=0), tc, rep_ref, a_scr, b_scr, reverse=False)
        hlast[...] = h[tc - SUBLANES :, :]
        h_ref[...] = h
        y_ref[...] = (h * _gelu(g_ref[...].astype(F32))).astype(BF16)

    wide = RNN_TILES_PER_STEP * LANES
    tile = lambda j, c: (0, j)
    vec = pl.BlockSpec((1, wide), tile)
    mats = pl.BlockSpec((RNN_TILES_PER_STEP, LANES, LANES), lambda j, c: (j, 0, 0))
    repeat = _repeat_matrix(tc)
    return _call(
        body,
        (proj, proj, cw, cb, wa2, ba, wx2, bx, lam, repeat),
        name=name,
        grid=(N_RNN_TILES // RNN_TILES_PER_STEP, s // tc),
        in_specs=[
            pl.BlockSpec((tc, wide), lambda j, c: (c, j)),
            pl.BlockSpec((tc, wide), lambda j, c: (c, D_RNN // wide + j)),
            pl.BlockSpec((CONV_WIDTH, wide), tile),
            vec,
            mats,
            vec,
            mats,
            vec,
            vec,
            pl.BlockSpec(repeat.shape, lambda j, c: (0, 0)),
        ],
        out_specs=[pl.BlockSpec((tc, wide), lambda j, c: (c, j)), pl.BlockSpec((tc, wide), lambda j, c: (c, j))],
        out_shape=[SDS((s, D_RNN), F32), SDS((s, D_RNN), BF16)],
        scratch_shapes=[pltpu.VMEM((SUBLANES, wide), F32)] * 2 + [pltpu.VMEM((RNN_TILES_PER_STEP, tc, LANES), F32)] * 2,
        semantics=("parallel", "arbitrary"),
        comm=comm,
    )


def _branch_a_bwd(dy, proj, h, cw, cb, wa2, ba, wx2, bx, lam, wa2t, wx2t, *, tc, name, comm=None):
    s = proj.shape[0]
    tc = min(tc, s)
    nc = s // tc
    halo16 = tc // 16
    halo8 = tc // SUBLANES

    def body(dy_ref, x_ref, xh_ref, g_ref, h_ref, hh_ref, cw_ref, cb_ref, wa_ref, ba_ref, wx_ref, bx_ref, lam_ref,
             wat_ref, wxt_ref, rep_ref, dx_ref, dg_ref, dcw_ref, dcb_ref, dba_ref, dbx_ref, dlam_ref, dwa_ref, dwx_ref,
             carry, dxr_next, a_scr, b_scr):
        cc = pl.program_id(1)
        ct = nc - 1 - cc

        @pl.when(cc == 0)
        def _():
            carry[...] = jnp.zeros_like(carry)
            dxr_next[...] = jnp.zeros_like(dxr_next)
            for ref in (dcw_ref, dcb_ref, dba_ref, dbx_ref, dlam_ref, dwa_ref, dwx_ref):
                ref[...] = jnp.zeros_like(ref)

        for t in range(RNN_TILES_PER_STEP):
            cols = lambda ref: ref.at[:, pl.ds(t * LANES, LANES)]
            one_tile(
                ct, cols(dy_ref), cols(x_ref), cols(xh_ref), cols(g_ref), cols(h_ref), cols(hh_ref), cols(cw_ref), cols(cb_ref),
                wa_ref.at[t], cols(ba_ref), wx_ref.at[t], cols(bx_ref), cols(lam_ref), wat_ref.at[t], wxt_ref.at[t], rep_ref,
                cols(dx_ref), cols(dg_ref), cols(dcw_ref), cols(dcb_ref), cols(dba_ref), cols(dbx_ref), cols(dlam_ref),
                dwa_ref.at[t], dwx_ref.at[t], cols(carry), cols(dxr_next), a_scr.at[t], b_scr.at[t],
            )

    def one_tile(ct, dy_ref, x_ref, xh_ref, g_ref, h_ref, hh_ref, cw_ref, cb_ref, wa_ref, ba_ref, wx_ref, bx_ref, lam_ref,
                 wat_ref, wxt_ref, rep_ref, dx_ref, dg_ref, dcw_ref, dcb_ref, dba_ref, dbx_ref, dlam_ref, dwa_ref, dwx_ref,
                 carry, dxr_next, a_scr, b_scr):
        xc = x_ref[...].astype(F32)
        xprev = jnp.where(ct > 0, xh_ref[SUBLANES:, :].astype(F32), 0.0)
        xr, (x1, x2, x3), r, i, a, om = _conv_and_gates(
            xc, xprev, cw_ref, cb_ref, wa_ref, ba_ref, wx_ref, bx_ref, lam_ref
        )
        inv_norm = lax.rsqrt(om)
        norm = om * inv_norm
        row = lax.broadcasted_iota(jnp.int32, xc.shape, 0)

        hv = h_ref[...]
        ge, ge_grad = _gelu_and_grad(g_ref[...].astype(F32))
        dyv = dy_ref[...].astype(F32)
        dg_ref[...] = (dyv * hv * ge_grad).astype(dg_ref.dtype)
        dh = dyv * ge

        b = dh + jnp.where(row == tc - 1, carry[0:1, :], 0.0)
        a_next = jnp.where(row < tc - 1, pltpu.roll(a, tc - 1, 0), 0.0)
        gadj = _scan_rows(a_next, b, tc, rep_ref, a_scr, b_scr, reverse=True)
        carry[...] = (a * gadj)[:SUBLANES, :]

        hprev_first = jnp.where(ct > 0, hh_ref[SUBLANES - 1 : SUBLANES, :], 0.0)
        hprev = jnp.where(row >= 1, pltpu.roll(hv, 1, 0), hprev_first)
        da = gadj * hprev
        ix = i * xr
        dnorm = gadj * ix
        di = gadj * norm * xr
        dlog_a = da * a - dnorm * (1.0 - om) * inv_norm
        sp = _softplus_neg(lam_ref[...])
        dr = dlog_a * (-LRU_C * sp)
        dsp = jnp.sum(dlog_a * (-LRU_C * r), axis=0, keepdims=True)
        dlam_ref[...] += dsp * (-_sigmoid(-lam_ref[...]))
        dza = dr * r * (1.0 - r)
        dzx = di * i * (1.0 - i)
        dzab, dzxb = dza.astype(BF16), dzx.astype(BF16)
        dxr = gadj * norm * i + _dot(dzab, wat_ref[...]) + _dot(dzxb, wxt_ref[...])
        xrb = xr.astype(BF16)
        dwa_ref[...] += _dot_tn(xrb, dzab)
        dwx_ref[...] += _dot_tn(xrb, dzxb)
        dba_ref[...] += jnp.sum(dza, axis=0, keepdims=True)
        dbx_ref[...] += jnp.sum(dzx, axis=0, keepdims=True)

        ext = jnp.concatenate([dxr, dxr_next[...]], axis=0)
        dx = (
            dxr * cw_ref[3:4, :]
            + _rows_after(ext, 1, tc) * cw_ref[2:3, :]
            + _rows_after(ext, 2, tc) * cw_ref[1:2, :]
            + _rows_after(ext, 3, tc) * cw_ref[0:1, :]
        )
        dxr_next[...] = dxr[:SUBLANES, :]
        dx_ref[...] = dx.astype(dx_ref.dtype)
        dcb_ref[...] += jnp.sum(dxr, axis=0, keepdims=True)
        dcw_ref[3:4, :] += jnp.sum(dxr * xc, axis=0, keepdims=True)
        dcw_ref[2:3, :] += jnp.sum(dxr * x1, axis=0, keepdims=True)
        dcw_ref[1:2, :] += jnp.sum(dxr * x2, axis=0, keepdims=True)
        dcw_ref[0:1, :] += jnp.sum(dxr * x3, axis=0, keepdims=True)

    wide = RNN_TILES_PER_STEP * LANES
    tile = lambda j, c: (0, j)
    mat = lambda j, c: (j, 0, 0)
    cur = lambda j, c: (nc - 1 - c, j)
    vec = pl.BlockSpec((1, wide), tile)
    matspec = pl.BlockSpec((RNN_TILES_PER_STEP, LANES, LANES), mat)
    repeat = _repeat_matrix(tc)
    return _call(
        body,
        (dy, proj, proj, proj, h, h, cw, cb, wa2, ba, wx2, bx, lam, wa2t, wx2t, repeat),
        name=name,
        grid=(N_RNN_TILES // RNN_TILES_PER_STEP, nc),
        in_specs=[
            pl.BlockSpec((tc, wide), cur),
            pl.BlockSpec((tc, wide), cur),
            pl.BlockSpec((16, wide), lambda j, c: (jnp.maximum((nc - 1 - c) * halo16 - 1, 0), j)),
            pl.BlockSpec((tc, wide), lambda j, c: (nc - 1 - c, D_RNN // wide + j)),
            pl.BlockSpec((tc, wide), cur),
            pl.BlockSpec((SUBLANES, wide), lambda j, c: (jnp.maximum((nc - 1 - c) * halo8 - 1, 0), j)),
            pl.BlockSpec((CONV_WIDTH, wide), tile),
            vec,
            matspec,
            vec,
            matspec,
            vec,
            vec,
            matspec,
            matspec,
            pl.BlockSpec(repeat.shape, lambda j, c: (0, 0)),
        ],
        out_specs=[
            pl.BlockSpec((tc, wide), cur),
            pl.BlockSpec((tc, wide), cur),
            pl.BlockSpec((CONV_WIDTH, wide), tile),
            vec,
            vec,
            vec,
            vec,
            matspec,
            matspec,
        ],
        out_shape=[
            SDS((s, D_RNN), BF16),
            SDS((s, D_RNN), BF16),
            SDS((CONV_WIDTH, D_RNN), F32),
            SDS((1, D_RNN), F32),
            SDS((1, D_RNN), F32),
            SDS((1, D_RNN), F32),
            SDS((1, D_RNN), F32),
            SDS((N_RNN_TILES, LANES, LANES), F32),
            SDS((N_RNN_TILES, LANES, LANES), F32),
        ],
        scratch_shapes=[pltpu.VMEM((SUBLANES, wide), F32)] * 2 + [pltpu.VMEM((RNN_TILES_PER_STEP, tc, LANES), F32)] * 2,
        semantics=("parallel", "arbitrary"),
        comm=comm,
    )


def _sgu_specs(tb):
    half = lambda blk: pl.BlockSpec((tb, 512), lambda n: (n, blk))
    return [half(U_BLK512), half(U_BLK512 + 1), half(V_BLK512), half(V_BLK512 + 1)]


def _sgu_normed(v, lng_ref, lnb_ref):
    gv, gv_grad = _gelu_and_grad(v)
    mu = jnp.mean(gv, axis=-1, keepdims=True)
    xc = gv - mu
    rs = lax.rsqrt(jnp.mean(xc * xc, axis=-1, keepdims=True) + EPS)
    xhat = xc * rs
    return xhat * lng_ref[...] + lnb_ref[...], xhat, rs, gv_grad


def _sgu_fwd(proj, lng, lnb, wm, bias, *, tb, name, comm=None):
    s = proj.shape[0]
    tb = min(tb, s)

    def body(u0_ref, u1_ref, v0_ref, v1_ref, lng_ref, lnb_ref, wm_ref, bias_ref, y_ref):
        u = jnp.concatenate([u0_ref[...], u1_ref[...]], axis=1).astype(F32)
        v = jnp.concatenate([v0_ref[...], v1_ref[...]], axis=1).astype(F32)
        gu = _gelu(u)
        vn, _, _, _ = _sgu_normed(v, lng_ref, lnb_ref)
        vnb = vn.astype(BF16)
        for blk in range(tb // SGU_BLOCK):
            rows = slice(blk * SGU_BLOCK, (blk + 1) * SGU_BLOCK)
            for g in range(SGU_GROUPS):
                cols = slice(g * LANES, (g + 1) * LANES)
                mixed = _dot(wm_ref[g], vnb[rows, cols]) + bias_ref[g]
                y_ref[rows, cols] = (gu[rows, cols] * mixed).astype(BF16)

    const2 = lambda n: (0, 0)
    const3 = lambda n: (0, 0, 0)
    return _call(
        body,
        (proj, proj, proj, proj, lng, lnb, wm, bias),
        name=name,
        grid=(s // tb,),
        in_specs=_sgu_specs(tb)
        + [
            pl.BlockSpec((1, D_SGU), const2),
            pl.BlockSpec((1, D_SGU), const2),
            pl.BlockSpec((SGU_GROUPS, SGU_BLOCK, SGU_BLOCK), const3),
            pl.BlockSpec((SGU_GROUPS, SGU_BLOCK, LANES), const3),
        ],
        out_specs=pl.BlockSpec((tb, D_SGU), lambda n: (n, 0)),
        out_shape=SDS((s, D_SGU), BF16),
        semantics=("parallel",),
        comm=comm,
    )


def _sgu_bwd(dy, proj, lng, lnb, wm, wmt, bias, mask, *, tb, name, comm=None):
    s = proj.shape[0]
    tb = min(tb, s)
    nb = s // tb

    def body(dy_ref, u0_ref, u1_ref, v0_ref, v1_ref, lng_ref, lnb_ref, wm_ref, wmt_ref, bias_ref, mask_ref,
             du_ref, dv_ref, dws_ref, dbs_ref, dlng_ref, dlnb_ref, dvn_scr, dbs_acc):
        n = pl.program_id(0)

        @pl.when(n == 0)
        def _():
            dbs_acc[...] = jnp.zeros_like(dbs_acc)
            for ref in (dws_ref, dlng_ref, dlnb_ref):
                ref[...] = jnp.zeros_like(ref)

        u = jnp.concatenate([u0_ref[...], u1_ref[...]], axis=1).astype(F32)
        v = jnp.concatenate([v0_ref[...], v1_ref[...]], axis=1).astype(F32)
        gu, gu_grad = _gelu_and_grad(u)
        vn, xhat, rs, gv_grad = _sgu_normed(v, lng_ref, lnb_ref)
        vnb = vn.astype(BF16)
        dyv = dy_ref[...].astype(F32)
        for blk in range(tb // SGU_BLOCK):
            rows = slice(blk * SGU_BLOCK, (blk + 1) * SGU_BLOCK)
            for g in range(SGU_GROUPS):
                cols = slice(g * LANES, (g + 1) * LANES)
                vt = vnb[rows, cols]
                mixed = _dot(wm_ref[g], vt) + bias_ref[g]
                dyt = dyv[rows, cols]
                du_ref[rows, cols] = (dyt * mixed * gu_grad[rows, cols]).astype(BF16)
                dmix = dyt * gu[rows, cols]
                dmixb = dmix.astype(BF16)
                dvn_scr[rows, cols] = _dot(wmt_ref[g], dmixb)
                dws_ref[g] += _dot_nt(dmixb, vt) * mask_ref[...]
                dbs_acc[g] += dmix
        dvn = dvn_scr[...]
        dlng_ref[...] += jnp.sum(dvn * xhat, axis=0, keepdims=True)
        dlnb_ref[...] += jnp.sum(dvn, axis=0, keepdims=True)
        dxh = dvn * lng_ref[...]
        dgv = rs * (
            dxh - jnp.mean(dxh, axis=-1, keepdims=True) - xhat * jnp.mean(dxh * xhat, axis=-1, keepdims=True)
        )
        dv_ref[...] = (dgv * gv_grad).astype(BF16)

        @pl.when(n == nb - 1)
        def _():
            for g in range(SGU_GROUPS):
                dbs_ref[g] = jnp.broadcast_to(jnp.sum(dbs_acc[g], axis=-1, keepdims=True), (SGU_BLOCK, LANES))

    const2 = lambda n: (0, 0)
    const3 = lambda n: (0, 0, 0)
    gmat = pl.BlockSpec((SGU_GROUPS, SGU_BLOCK, SGU_BLOCK), const3)
    vec = pl.BlockSpec((1, D_SGU), const2)
    act = pl.BlockSpec((tb, D_SGU), lambda n: (n, 0))
    return _call(
        body,
        (dy, proj, proj, proj, proj, lng, lnb, wm, wmt, bias, mask),
        name=name,
        grid=(nb,),
        in_specs=[act] + _sgu_specs(tb) + [vec, vec, gmat, gmat, gmat, pl.BlockSpec((SGU_BLOCK, SGU_BLOCK), const2)],
        out_specs=[act, act, gmat, gmat, vec, vec],
        out_shape=[
            SDS((s, D_SGU), BF16),
            SDS((s, D_SGU), BF16),
            SDS((SGU_GROUPS, SGU_BLOCK, SGU_BLOCK), F32),
            SDS((SGU_GROUPS, SGU_BLOCK, LANES), F32),
            SDS((1, D_SGU), F32),
            SDS((1, D_SGU), F32),
        ],
        scratch_shapes=[pltpu.VMEM((tb, D_SGU), F32), pltpu.VMEM((SGU_GROUPS, SGU_BLOCK, LANES), F32)],
        semantics=("arbitrary",),
        comm=comm,
    )


def _gate_specs(tm):
    half = lambda blk: pl.BlockSpec((tm, 512), lambda i: (i, blk))
    return [half(GA_BLK512), half(GA_BLK512 + 1), half(GB_BLK512), half(GB_BLK512 + 1)]


def _merge_fwd(ya_pre, yb_pre, proj, x, w_ba, w_bb, w_out, *, tm, name, comm=None):
    s = x.shape[0]
    tm = min(tm, s)

    def body(ya_ref, yb_ref, a0, a1, b0, b1, x_ref, wa_ref, wb_ref, wo_ref, x1_ref, yao_ref, ybo_ref):
        ya = _dot(ya_ref[...], wa_ref[...])
        yb = _dot(yb_ref[...], wb_ref[...])
        sa = _sigmoid(jnp.concatenate([a0[...], a1[...]], axis=1).astype(F32))
        sb = _sigmoid(jnp.concatenate([b0[...], b1[...]], axis=1).astype(F32))
        merged = sa * ya + sb * yb
        x1_ref[...] = x_ref[...] + _dot(merged.astype(BF16), wo_ref[...])
        yao_ref[...] = ya.astype(BF16)
        ybo_ref[...] = yb.astype(BF16)

    whole = lambda r: pl.BlockSpec((r, D), lambda i: (0, 0))
    act = pl.BlockSpec((tm, D), lambda i: (i, 0))
    return _call(
        body,
        (ya_pre, yb_pre, proj, proj, proj, proj, x, w_ba, w_bb, w_out),
        name=name,
        grid=(s // tm,),
        in_specs=[pl.BlockSpec((tm, D_RNN), lambda i: (i, 0)), act] + _gate_specs(tm) + [act, whole(D_RNN), whole(D_SGU), whole(D)],
        out_specs=[act, act, act],
        out_shape=[SDS((s, D), F32), SDS((s, D), BF16), SDS((s, D), BF16)],
        semantics=("parallel",),
        comm=comm,
    )


def _merge_bwd(dx1, ya, yb, proj, w_ba, w_bb, w_out, *, tm, name, comm=None):
    s = dx1.shape[0]
    tm = min(tm, s)

    def body(dx_ref, ya_ref, yb_ref, a0, a1, b0, b1, wa_ref, wb_ref, wo_ref,
             mg_ref, dya_ref, dyb_ref, dga_ref, dgb_ref, dyap_ref, dybp_ref):
        dm = _dot_nt(dx_ref[...], wo_ref[...])
        ya = ya_ref[...].astype(F32)
        yb = yb_ref[...].astype(F32)
        sa = _sigmoid(jnp.concatenate([a0[...], a1[...]], axis=1).astype(F32))
        sb = _sigmoid(jnp.concatenate([b0[...], b1[...]], axis=1).astype(F32))
        mg_ref[...] = (sa * ya + sb * yb).astype(BF16)
        dya = (dm * sa).astype(BF16)
        dyb = (dm * sb).astype(BF16)
        dya_ref[...] = dya
        dyb_ref[...] = dyb
        dga_ref[...] = (dm * ya * sa * (1.0 - sa)).astype(BF16)
        dgb_ref[...] = (dm * yb * sb * (1.0 - sb)).astype(BF16)
        dyap_ref[...] = _dot_nt(dya, wa_ref[...]).astype(BF16)
        dybp_ref[...] = _dot_nt(dyb, wb_ref[...]).astype(BF16)

    whole = lambda r: pl.BlockSpec((r, D), lambda i: (0, 0))
    act = pl.BlockSpec((tm, D), lambda i: (i, 0))
    act_rnn = pl.BlockSpec((tm, D_RNN), lambda i: (i, 0))
    return _call(
        body,
        (dx1, ya, yb, proj, proj, proj, proj, w_ba, w_bb, w_out),
        name=name,
        grid=(s // tm,),
        in_specs=[act, act, act] + _gate_specs(tm) + [whole(D_RNN), whole(D_SGU), whole(D)],
        out_specs=[act, act, act, act, act, act_rnn, act],
        out_shape=[SDS((s, D), BF16)] * 5 + [SDS((s, D_RNN), BF16), SDS((s, D_SGU), BF16)],
        semantics=("parallel",),
        comm=comm,
    )


def _ffn_down_loss(a, w, res, g, target, *, tm, name):
    s, k = a.shape
    tm = min(tm, s)

    def body(a_ref, w_ref, r_ref, g_ref, t_ref, dx_ref, dxb_ref, dg_ref, loss_ref):
        @pl.when(pl.program_id(0) == 0)
        def _():
            dg_ref[...] = jnp.zeros_like(dg_ref)
            loss_ref[...] = jnp.zeros_like(loss_ref)

        t = jnp.maximum(a_ref[...].astype(F32), 0.0)
        xv = r_ref[...] + _dot((t * t).astype(BF16), w_ref[...])
        r = lax.rsqrt(jnp.mean(xv * xv, axis=-1, keepdims=True) + EPS)
        xhat = xv * r
        e = xhat * g_ref[...] - t_ref[...]
        loss_ref[...] += 0.5 * jnp.sum(jnp.mean(e * e, axis=-1, keepdims=True), axis=0, keepdims=True)
        dy = e * (1.0 / D)
        dxh = dy * g_ref[...]
        dx = r * (dxh - xhat * jnp.mean(dxh * xhat, axis=-1, keepdims=True))
        dx_ref[...] = dx
        dxb_ref[...] = dx.astype(BF16)
        dg_ref[...] += jnp.sum(dy * xhat, axis=0, keepdims=True)

    act = pl.BlockSpec((tm, D), lambda i: (i, 0))
    vec = pl.BlockSpec((1, D), lambda i: (0, 0))
    return pl.pallas_call(
        body,
        name=name,
        grid=(s // tm,),
        in_specs=[pl.BlockSpec((tm, k), lambda i: (i, 0)), pl.BlockSpec((k, D), lambda i: (0, 0)), act, vec, act],
        out_specs=[act, act, vec, pl.BlockSpec((SUBLANES, LANES), lambda i: (0, 0))],
        out_shape=[SDS((s, D), F32), SDS((s, D), BF16), SDS((1, D), F32), SDS((SUBLANES, LANES), F32)],
        compiler_params=_params("arbitrary"),
    )(a, w, res, g, target)


def _adamw_math(w, g, m, v):
    m2 = ADAM_B1 * m + (1.0 - ADAM_B1) * g
    v2 = ADAM_B2 * v + (1.0 - ADAM_B2) * (g * g)
    m_hat = m2 / (1.0 - ADAM_B1**ADAM_STEP)
    v_hat = v2 / (1.0 - ADAM_B2**ADAM_STEP)
    delta = -ADAM_LR * (m_hat / (jnp.sqrt(v_hat) + ADAM_EPS) + ADAM_WD * w)
    return delta, m2, v2


def _row_tile(rows, cap):
    return max(t for t in range(SUBLANES, min(cap, rows) + 1, SUBLANES) if rows % t == 0)


def _adamw_layers(w, grads, m, v, *, tr, name):
    depth, r, c = w.shape
    tr = _row_tile(r, tr)

    def body(*refs):
        g_refs = refs[:depth]
        w_ref, m_ref, v_ref, g_out, d_ref, mo_ref, vo_ref = refs[depth:]
        for l in range(depth):

            @pl.when(pl.program_id(0) == l)
            def _(l=l):
                g = g_refs[l][...]
                g_out[...] = g
                d_ref[...], mo_ref[...], vo_ref[...] = _adamw_math(w_ref[...], g, m_ref[...], v_ref[...])

    def of_layer(ll):
        return pl.BlockSpec((tr, c), lambda l, i: (jnp.where(l == ll, i, 0), 0))

    stacked = pl.BlockSpec((None, tr, c), lambda l, i: (l, i, 0))
    return pl.pallas_call(
        body,
        name=name,
        grid=(depth, r // tr),
        in_specs=[of_layer(ll) for ll in range(depth)] + [stacked] * 3,
        out_specs=[stacked] * 4,
        out_shape=[SDS((depth, r, c), F32)] * 4,
        compiler_params=_params("parallel", "parallel"),
    )(*grads, w, m, v)


def _adamw_reduced(w, parts, from_chips, m, v, chip, *, tr, name):
    depth, r, _ = w.shape
    tr = _row_tile(r, tr)

    def body(chip_ref, *refs):
        p_refs, c_refs = refs[:depth], refs[depth : 2 * depth]
        w_ref, m_ref, v_ref, g_out, d_ref, mo_ref, vo_ref = refs[2 * depth :]
        for l in range(depth):

            @pl.when(pl.program_id(0) == l)
            def _(l=l):
                got = c_refs[l]
                g = ((p_refs[l][...].astype(F32) + got[0].astype(F32)) + got[1].astype(F32)) + got[2].astype(F32)
                g_out[...] = g
                d_ref[...], mo_ref[...], vo_ref[...] = _adamw_math(w_ref[...], g, m_ref[...], v_ref[...])

    def mine_of_layer(ll):
        return pl.BlockSpec((None, tr, D), lambda l, i, chip_ref: (chip_ref[0], jnp.where(l == ll, i, 0), 0))

    def theirs_of_layer(ll):
        return pl.BlockSpec((3, tr, D), lambda l, i, chip_ref: (0, jnp.where(l == ll, i, 0), 0))

    stacked = pl.BlockSpec((None, tr, D), lambda l, i, chip_ref: (l, i, 0))
    return pl.pallas_call(
        body,
        name=name,
        grid_spec=pltpu.PrefetchScalarGridSpec(
            num_scalar_prefetch=1,
            grid=(depth, r // tr),
            in_specs=[mine_of_layer(ll) for ll in range(depth)]
            + [theirs_of_layer(ll) for ll in range(depth)]
            + [stacked] * 3,
            out_specs=[stacked] * 4,
        ),
        out_shape=[SDS((depth, r, D), F32)] * 4,
        compiler_params=_params("parallel", "parallel"),
    )(chip, *parts, *from_chips, w, m, v)


def _adamw_small(groups, *, name):
    n = len(groups)

    def body(*refs):
        ins, outs = refs[: 4 * n], refs[4 * n :]
        for i in range(n):
            w, g, m, v = (ref[...] for ref in ins[4 * i : 4 * i + 4])
            outs[3 * i][...], outs[3 * i + 1][...], outs[3 * i + 2][...] = _adamw_math(w, g, m, v)

    vmem = pl.BlockSpec(memory_space=pltpu.VMEM)
    outs = pl.pallas_call(
        body,
        name=name,
        in_specs=[vmem] * (4 * n),
        out_specs=[vmem] * (3 * n),
        out_shape=[SDS(grp[0].shape, F32) for grp in groups for _ in range(3)],
        compiler_params=pltpu.CompilerParams(vmem_limit_bytes=VMEM_LIMIT_BYTES),
    )(*[a for grp in groups for a in grp])
    return [tuple(outs[3 * i : 3 * i + 3]) for i in range(n)]


ANY = pl.BlockSpec(memory_space=pl.ANY)


def _position():
    return lax.axis_index("x"), lax.axis_index("y"), lax.axis_index("c")


def _other_chips(x, y):
    return [(1 - x, y), (x, 1 - y), (1 - x, 1 - y)]


class _Comm:
    def __init__(self, inputs, out_shapes, sem_counts, start, middle, finish, middle_at=1.0, aliases=()):
        self.inputs, self.out_shapes, self.sem_counts = list(inputs), list(out_shapes), list(sem_counts)
        self.start, self.middle, self.finish = start, middle, finish
        self.middle_at = middle_at
        self.aliases = list(aliases)

    def sem_shapes(self):
        return [pltpu.SemaphoreType.DMA((n,)) for n in self.sem_counts]


def _merge_comms(comms):
    bounds, i, o, s = [], 0, 0, 0
    for cm in comms:
        bounds.append((i, i + len(cm.inputs), o, o + len(cm.out_shapes), s, s + len(cm.sem_counts)))
        i, o, s = bounds[-1][1], bounds[-1][3], bounds[-1][5]

    def phase(which):
        def run(ins, outs, sems):
            for cm, (i0, i1, o0, o1, s0, s1) in zip(comms, bounds):
                getattr(cm, which)(ins[i0:i1], outs[o0:o1], sems[s0:s1])

        return run

    return _Comm(
        [a for cm in comms for a in cm.inputs],
        [a for cm in comms for a in cm.out_shapes],
        [a for cm in comms for a in cm.sem_counts],
        phase("start"),
        phase("middle"),
        phase("finish"),
        middle_at=max(cm.middle_at for cm in comms),
        aliases=[(i0 + i, o0 + o) for cm, (i0, _, o0, _, _, _) in zip(comms, bounds) for i, o in cm.aliases],
    )


def _call(body, args, *, semantics, comm=None, **kw):
    if comm is None:
        return pl.pallas_call(body, compiler_params=_params(*semantics), **kw)(*args)
    grid, in_specs, out_specs, out_shape = kw["grid"], kw["in_specs"], kw["out_specs"], kw["out_shape"]
    scratch = list(kw.get("scratch_shapes", ()))
    single = not isinstance(out_shape, (list, tuple))
    core_specs = [out_specs] if single else list(out_specs)
    core_shapes = [out_shape] if single else list(out_shape)
    n_in, n_out, n_scr = len(in_specs), len(core_shapes), len(scratch)
    n_cin, n_cout = len(comm.inputs), len(comm.out_shapes)
    steps = 1
    for g in grid:
        steps *= g
    middle = min(int(comm.middle_at * steps), steps - 1)

    def hosted(*refs):
        core_in, c_in = refs[:n_in], refs[n_in : n_in + n_cin]
        o0 = n_in + n_cin
        core_out, c_out = refs[o0 : o0 + n_out], refs[o0 + n_out : o0 + n_out + n_cout]
        s0 = o0 + n_out + n_cout
        core_scr, sems = refs[s0 : s0 + n_scr], refs[s0 + n_scr :]
        step = pl.program_id(0)
        for d in range(1, len(grid)):
            step = step * grid[d] + pl.program_id(d)

        @pl.when(step == 0)
        def _():
            comm.start(c_in, c_out, sems)

        body(*core_in, *core_out, *core_scr)

        @pl.when(step == middle)
        def _():
            comm.middle(c_in, c_out, sems)

        @pl.when(step == steps - 1)
        def _():
            comm.finish(c_in, c_out, sems)

    outs = pl.pallas_call(
        hosted,
        name=kw["name"],
        grid=grid,
        in_specs=list(in_specs) + [ANY] * n_cin,
        out_specs=core_specs + [ANY] * n_cout,
        out_shape=core_shapes + comm.out_shapes,
        scratch_shapes=scratch + comm.sem_shapes(),
        input_output_aliases={n_in + i: n_out + o for i, o in comm.aliases},
        compiler_params=_params(*(["arbitrary"] * len(grid))),
    )(*args, *comm.inputs)
    return (outs[0] if single else outs[:n_out]), outs[n_out:]


def _comm_only(comm, *, name):
    n_cin, n_cout = len(comm.inputs), len(comm.out_shapes)

    def body(*refs):
        ins, outs, sems = refs[:n_cin], refs[n_cin : n_cin + n_cout], refs[n_cin + n_cout :]
        comm.start(ins, outs, sems)
        comm.middle(ins, outs, sems)
        comm.finish(ins, outs, sems)

    return pl.pallas_call(
        body,
        name=name,
        in_specs=[ANY] * n_cin,
        out_specs=[ANY] * n_cout,
        out_shape=comm.out_shapes,
        scratch_shapes=comm.sem_shapes(),
    )(*comm.inputs)


def _gather_comm(shards, pass_on_at=1.0):
    n = len(shards)
    per = 7

    def plan(ins, outs, sems):
        send, recv, local = sems
        x, y, c = _position()
        me, sibling = (x, y, c), (x, y, 1 - c)
        chips = _other_chips(x, y)

        def block(t, px, py, pc):
            return outs[t].at[pl.ds(4 * px + 2 * py + pc, 1)]

        def copy(t, k, blk, to, src=None):
            return pltpu.make_async_remote_copy(
                src_ref=block(t, *blk) if src is None else src,
                dst_ref=block(t, *blk),
                send_sem=send.at[t * per + k],
                recv_sem=recv.at[t * per + k],
                device_id=to,
                device_id_type=MESH,
            )

        mine = [pltpu.make_async_copy(ins[t], block(t, *me), local.at[t]) for t in range(n)]
        to_chips = [copy(t, 1 + j, me, (*chip, c), src=ins[t]) for t in range(n) for j, chip in enumerate(chips)]
        to_sibling = [copy(t, 0, me, sibling, src=ins[t]) for t in range(n)]
        from_chips = [copy(t, 1 + j, (*chip, c), me) for t in range(n) for j, chip in enumerate(chips)]
        passed_on = [copy(t, 4 + j, (*chip, c), sibling) for t in range(n) for j, chip in enumerate(chips)]
        from_sibling = [copy(t, 0, sibling, me) for t in range(n)]
        from_sibling += [copy(t, 4 + j, (*chip, 1 - c), me) for t in range(n) for j, chip in enumerate(chips)]
        return mine, to_chips, to_sibling, from_chips, passed_on, from_sibling

    def start(ins, outs, sems):
        mine, to_chips, to_sibling, _, _, _ = plan(ins, outs, sems)
        for cp in mine + to_chips + to_sibling:
            cp.start()

    def middle(ins, outs, sems):
        _, _, _, from_chips, passed_on, _ = plan(ins, outs, sems)
        for arrived, onward in zip(from_chips, passed_on):
            arrived.wait_recv()
            onward.start()

    def finish(ins, outs, sems):
        mine, to_chips, to_sibling, _, passed_on, from_sibling = plan(ins, outs, sems)
        for cp in from_sibling:
            cp.wait_recv()
        for cp in to_chips + to_sibling + passed_on:
            cp.wait_send()
        for cp in mine:
            cp.wait()

    out_shapes = [SDS((N_DEV,) + sh.shape[1:], sh.dtype) for sh in shards]
    return _Comm(shards, out_shapes, [n * per, n * per, n], start, middle, finish, middle_at=pass_on_at)


def _gather_halves(shards=None, arrived=None):
    first_half = arrived is None
    arrays = shards if first_half else arrived
    n = len(arrays)
    per = 4 if first_half else 3

    def plan(ins, outs, sems):
        x, y, c = _position()
        me, sibling = (x, y, c), (x, y, 1 - c)
        chips = _other_chips(x, y)

        def block(t, px, py, pc):
            return outs[t].at[pl.ds(4 * px + 2 * py + pc, 1)]

        def copy(t, k, blk, to, src=None):
            return pltpu.make_async_remote_copy(
                src_ref=block(t, *blk) if src is None else src,
                dst_ref=block(t, *blk),
                send_sem=sems[0].at[t * per + k],
                recv_sem=sems[1].at[t * per + k],
                device_id=to,
                device_id_type=MESH,
            )

        if first_half:
            local = [pltpu.make_async_copy(ins[t], block(t, *me), sems[2].at[t]) for t in range(n)]
            sent = [copy(t, 1 + j, me, (*chip, c), src=ins[t]) for t in range(n) for j, chip in enumerate(chips)]
            sent += [copy(t, 0, me, sibling, src=ins[t]) for t in range(n)]
            landing = [copy(t, 1 + j, (*chip, c), me) for t in range(n) for j, chip in enumerate(chips)]
            landing += [copy(t, 0, sibling, me) for t in range(n)]
        else:
            local = []
            sent = [copy(t, j, (*chip, c), sibling) for t in range(n) for j, chip in enumerate(chips)]
            landing = [copy(t, j, (*chip, 1 - c), me) for t in range(n) for j, chip in enumerate(chips)]
        return local, sent, landing

    def start(ins, outs, sems):
        local, sent, _ = plan(ins, outs, sems)
        for cp in local + sent:
            cp.start()

    def middle(ins, outs, sems):
        pass

    def finish(ins, outs, sems):
        local, sent, landing = plan(ins, outs, sems)
        for cp in landing:
            cp.wait_recv()
        for cp in sent:
            cp.wait_send()
        for cp in local:
            cp.wait()

    if first_half:
        out_shapes = [SDS((N_DEV,) + sh.shape[1:], sh.dtype) for sh in shards]
        return _Comm(shards, out_shapes, [n * per, n * per, n], start, middle, finish)
    out_shapes = [SDS(a.shape, a.dtype) for a in arrived]
    return _Comm(arrived, out_shapes, [n * per, n * per], start, middle, finish, aliases=[(t, t) for t in range(n)])


def _exchange_comm(arrays, out_shapes, n_copies, copies_of):
    def start(ins, outs, sems):
        for cp in copies_of(ins, outs, *sems):
            cp.start()

    def middle(ins, outs, sems):
        pass

    def finish(ins, outs, sems):
        for cp in copies_of(ins, outs, *sems):
            cp.wait()

    return _Comm(arrays, out_shapes, [n_copies, n_copies], start, middle, finish)


def _sibling_comm(grads):
    def copies_of(ins, outs, send, recv):
        x, y, c = _position()
        return [
            pltpu.make_async_remote_copy(
                src_ref=ins[t].at[:, pl.ds(1 - c, 1)],
                dst_ref=outs[t],
                send_sem=send.at[t],
                recv_sem=recv.at[t],
                device_id=(x, y, 1 - c),
                device_id_type=MESH,
            )
            for t in range(len(ins))
        ]

    return _exchange_comm(grads, [SDS((4, 1) + g.shape[2:], g.dtype) for g in grads], len(grads), copies_of)


def _chips_comm(parts):
    def copies_of(ins, outs, send, recv):
        x, y, c = _position()
        return [
            pltpu.make_async_remote_copy(
                src_ref=ins[t].at[pl.ds(2 * px + py, 1)],
                dst_ref=outs[t].at[pl.ds(k, 1)],
                send_sem=send.at[3 * t + k],
                recv_sem=recv.at[3 * t + k],
                device_id=(px, py, c),
                device_id_type=MESH,
            )
            for t in range(len(ins))
            for k, (px, py) in enumerate(_other_chips(x, y))
        ]

    return _exchange_comm(parts, [SDS((3,) + p.shape[1:], p.dtype) for p in parts], 3 * len(parts), copies_of)


def _sum_with_sibling(grad, got, core, *, name):
    rows = grad.shape[2]

    def body(core_ref, a_ref, b_ref, o_ref):
        o_ref[...] = (a_ref[...].astype(F32) + b_ref[...].astype(F32)).astype(o_ref.dtype)

    return pl.pallas_call(
        body,
        name=name,
        grid_spec=pltpu.PrefetchScalarGridSpec(
            num_scalar_prefetch=1,
            grid=(4,),
            in_specs=[
                pl.BlockSpec((None, None, rows, D), lambda q, core_ref: (q, core_ref[0], 0, 0)),
                pl.BlockSpec((None, None, rows, D), lambda q, core_ref: (q, 0, 0, 0)),
            ],
            out_specs=pl.BlockSpec((None, rows, D), lambda q, core_ref: (q, 0, 0)),
        ),
        out_shape=SDS((4, rows, D), grad.dtype),
        compiler_params=_params("parallel"),
    )(core, grad, got)


def _sum_chips(part, got, chip, *, name):
    rows = part.shape[1]

    def body(chip_ref, a_ref, b_ref, o_ref):
        o_ref[...] = ((a_ref[...].astype(F32) + b_ref[0].astype(F32)) + b_ref[1].astype(F32)) + b_ref[2].astype(F32)

    return pl.pallas_call(
        body,
        name=name,
        grid_spec=pltpu.PrefetchScalarGridSpec(
            num_scalar_prefetch=1,
            grid=(1,),
            in_specs=[
                pl.BlockSpec((None, rows, D), lambda i, chip_ref: (chip_ref[0], 0, 0)),
                pl.BlockSpec((3, rows, D), lambda i, chip_ref: (0, 0, 0)),
            ],
            out_specs=pl.BlockSpec((rows, D), lambda i, chip_ref: (0, 0)),
        ),
        out_shape=SDS((rows, D), F32),
        compiler_params=_params("arbitrary"),
    )(chip, part, got)


def _all_reduce_small(pack, *, name):
    rows = pack.shape[1]

    def body(in_ref, out_ref, from_sibling, part, from_chips, send, recv):
        x, y, c = _position()
        me, sibling = (x, y, c), (x, y, 1 - c)
        chips = _other_chips(x, y)
        waiting = []

        def copy(k, src, dst, to):
            return pltpu.make_async_remote_copy(
                src_ref=src, dst_ref=dst, send_sem=send.at[k], recv_sem=recv.at[k], device_id=to, device_id_type=MESH
            )

        def exchange(copies):
            for cp in copies:
                cp.start()
            for cp in copies:
                cp.wait_recv()
            waiting.extend(copies)

        def block(px, py, pc):
            return out_ref.at[4 * px + 2 * py + pc]

        exchange([copy(q, in_ref.at[2 * q + 1 - c], from_sibling.at[q], sibling) for q in range(4)])
        for q in range(4):
            part[q] = in_ref[2 * q + c] + from_sibling[q]
        exchange([copy(4 + k, part.at[2 * px + py], from_chips.at[k], (px, py, c)) for k, (px, py) in enumerate(chips)])
        out_ref[4 * x + 2 * y + c] = ((part[2 * x + y] + from_chips[0]) + from_chips[1]) + from_chips[2]
        exchange(
            [copy(7, block(*me), block(*me), sibling)]
            + [copy(8 + k, block(*me), block(*me), (px, py, c)) for k, (px, py) in enumerate(chips)]
        )
        exchange([copy(11 + k, block(px, py, c), block(px, py, c), sibling) for k, (px, py) in enumerate(chips)])
        for cp in waiting:
            cp.wait_send()

    vmem = pl.BlockSpec(memory_space=pltpu.VMEM)
    return pl.pallas_call(
        body,
        name=name,
        in_specs=[vmem],
        out_specs=vmem,
        out_shape=SDS(pack.shape, F32),
        scratch_shapes=[
            pltpu.VMEM((4, rows, D), F32),
            pltpu.VMEM((4, rows, D), F32),
            pltpu.VMEM((3, rows, D), F32),
            pltpu.SemaphoreType.DMA((14,)),
            pltpu.SemaphoreType.DMA((14,)),
        ],
        compiler_params=pltpu.CompilerParams(vmem_limit_bytes=VMEM_LIMIT_BYTES),
    )(pack)


def _pack(arrays, rows):
    flat = jnp.concatenate([a.reshape(-1).astype(F32) for a in arrays])
    return jnp.pad(flat, (0, rows * D - flat.shape[0])).reshape(rows, D)


def _unpack(pack, shapes):
    flat = pack.reshape(-1)
    out, off = [], 0
    for sh in shapes:
        size = 1
        for dim in sh:
            size *= dim
        out.append(flat[off : off + size].reshape(sh))
        off += size
    return out


def _block_diag_pairs(w):
    w = w.reshape(N_RNN_TILES, 2, HEAD_DIM, HEAD_DIM)
    z = jnp.zeros_like(w[:, 0])
    top = jnp.concatenate([w[:, 0], z], axis=2)
    bot = jnp.concatenate([z, w[:, 1]], axis=2)
    return jnp.concatenate([top, bot], axis=1)


def _diag_blocks(w2):
    a = w2[:, :HEAD_DIM, :HEAD_DIM]
    b = w2[:, HEAD_DIM:, HEAD_DIM:]
    return jnp.stack([a, b], axis=1).reshape(RNN_HEADS, HEAD_DIM, HEAD_DIM)


BIG = ("w_in", "w_branch_a", "w_branch_b", "w_out", "w_up", "w_down")
TRANSPOSED = ("w_in", "w_up")
SMALL = (
    "norm_mix_g", "conv_w", "conv_b", "lru_w_a", "lru_b_a", "lru_w_x", "lru_b_x", "lru_lambda",
    "sgu_ln_g", "sgu_ln_b", "sgu_w_s", "sgu_b_s", "norm_ffn_g", "final_norm_g",
)
WEIGHTS = (
    "norm_mix_g", "w_in", "conv_w", "conv_b", "lru_w_a", "lru_b_a", "lru_w_x", "lru_b_x", "lru_lambda", "sgu_ln_g",
    "sgu_ln_b", "sgu_w_s", "sgu_b_s", "w_branch_a", "w_branch_b", "w_out", "norm_ffn_g", "w_up", "w_down", "final_norm_g",
)

TM = 512
TM_NT = 1024
TN_IN = 1664
TN_UP = 2048
TKA = 512
TKA_PIECES = 256
TC = 512
TC_BWD = 1024
TB = 256
TB_BWD = 512
TR = 256


_BRANCH_WEIGHTS = ("w_branch_a", "w_branch_b", "w_out")
GATHERS_RIDING = (
    {
        "in_proj": ([(0, name) for name in _BRANCH_WEIGHTS] + [(0, "w_up")], []),
        "branch_a_fwd": ([(1, "w_in")], [(0, name) for name in _BRANCH_WEIGHTS] + [(0, "w_up")]),
        "sgu_fwd": ([], [(1, "w_in")]),
        "merge_fwd": ([(0, "w_down")], []),
        "ffn_up": ([(1, name) for name in _BRANCH_WEIGHTS], [(0, "w_down")]),
        "ffn_down": ([], [(1, name) for name in _BRANCH_WEIGHTS]),
    },
    {"in_proj": ([(1, "w_down")], []), "branch_a_fwd": ([(1, "w_up")], [(1, "w_down")]), "sgu_fwd": ([], [(1, "w_up")])},
)


def _layer_forward(l, x, p, w, shards, arriving, loss_head=None):
    def run(key, fn, *args, **kw):
        first, second = GATHERS_RIDING[l].get(key, ((), ()))
        comms = []
        if first:
            comms.append(_gather_halves(shards=[shards[l2][n2] for l2, n2 in first]))
        if second:
            comms.append(_gather_halves(arrived=[arriving.pop(k) for k in second]))
        if not comms:
            return fn(*args, **kw)
        out, got = fn(*args, comm=_merge_comms(comms), **kw)
        arriving.update(zip(first, got[: len(first)]))
        for (l2, n2), full in zip(second, got[len(first) :]):
            w[l2][n2] = full.reshape(-1, D)
        return out

    proj, h = run("in_proj", _norm_matmul_nt, x, p["norm_mix_g"], w[l]["w_in"], tm=TM_NT, tn=TN_IN, name=f"in_proj_{l}")
    hseq, ya_pre = run(
        "branch_a_fwd", _branch_a_fwd, proj, p["conv_w"], p["conv_b"], p["wa2"], p["lru_b_a"], p["wx2"], p["lru_b_x"],
        p["lru_lambda"], tc=TC, name=f"branch_a_fwd_{l}",
    )
    yb_pre = run("sgu_fwd", _sgu_fwd, proj, p["sgu_ln_g"], p["sgu_ln_b"], p["wm"], p["sgu_bias"], tb=TB, name=f"sgu_fwd_{l}")
    x1, ya, yb = run(
        "merge_fwd", _merge_fwd, ya_pre, yb_pre, proj, x, w[l]["w_branch_a"], w[l]["w_branch_b"], w[l]["w_out"], tm=TM,
        name=f"merge_fwd_{l}",
    )
    f_pre, h2 = run("ffn_up", _norm_matmul_nt, x1, p["norm_ffn_g"], w[l]["w_up"], tm=TM_NT, tn=TN_UP, name=f"ffn_up_{l}")
    saved = dict(x=x, h=h, proj=proj, hseq=hseq, ya_pre=ya_pre, yb_pre=yb_pre, ya=ya, yb=yb, x1=x1, h2=h2, f_pre=f_pre)
    if loss_head is None:
        return run("ffn_down", _matmul_nn_res, f_pre, w[l]["w_down"], x1, relu2=True, tm=TM, name=f"ffn_down_{l}"), saved
    return _ffn_down_loss(f_pre, w[l]["w_down"], x1, *loss_head, tm=TM, name=f"ffn_down_loss_{l}"), saved


def _layer_backward(l, dx2, dx2b, sv, p, w, core, waiting, last):
    parts, from_chips = {}, {}

    def by_device(g):
        return g.reshape(4, 2, -1, D)

    def with_sibling(name, g, got):
        parts[name] = _sum_with_sibling(by_device(g), got, core, name=f"sum_sibling_{name}_{l}")

    df_pre = _matmul_nt_drelu2(dx2b, w["w_down"], sv["f_pre"], tm=TM_NT, tn=TN_UP, name=f"ffn_down_bwd_{l}")
    g_down = _matmul_tn([sv["f_pre"]], dx2b, relu2=True, tka=TKA, name=f"grad_w_down_{l}")
    g_up, (got,) = _matmul_tn(
        [df_pre], sv["h2"], relu2=False, tka=TKA, name=f"grad_w_up_{l}", comm=_sibling_comm([by_device(g_down)])
    )
    with_sibling("w_down", g_down, got)
    (dx1, dx1b, g_norm_ffn), (got,) = _matmul_nn_rmsnorm_bwd(
        [df_pre], w["w_up"], sv["x1"], p["norm_ffn_g"], dx2, tm=TM, name=f"ffn_up_bwd_{l}",
        comm=_sibling_comm([by_device(g_up)]),
    )
    with_sibling("w_up", g_up, got)
    (merged, dya, dyb, dga, dgb, dya_pre, dyb_pre), (from_chips[l, "w_up"],) = _merge_bwd(
        dx1b, sv["ya"], sv["yb"], sv["proj"], w["w_branch_a"], w["w_branch_b"], w["w_out"], tm=TM, name=f"merge_bwd_{l}",
        comm=_chips_comm([parts["w_up"]]),
    )
    g_out = _matmul_tn([merged], dx1b, relu2=False, tka=TKA, name=f"grad_w_out_{l}")
    g_ba = _matmul_tn([sv["ya_pre"]], dya, relu2=False, tka=TKA_PIECES, name=f"grad_w_branch_a_{l}")
    g_bb = _matmul_tn([sv["yb_pre"]], dyb, relu2=False, tka=TKA, name=f"grad_w_branch_b_{l}")
    branch = (("w_out", g_out), ("w_branch_a", g_ba), ("w_branch_b", g_bb))
    (du, dv, g_ws, g_bs, g_lng, g_lnb), got = _sgu_bwd(
        dyb_pre, sv["proj"], p["sgu_ln_g"], p["sgu_ln_b"], p["wm"], p["wmt"], p["sgu_bias"], p["mask"], tb=TB_BWD,
        name=f"sgu_bwd_{l}",
        comm=_merge_comms([_sibling_comm([by_device(g) for _, g in branch]), _chips_comm([parts["w_down"]])]),
    )
    from_chips[l, "w_down"] = got[-1]
    for (name, g), landed in zip(branch, got):
        with_sibling(name, g, landed)
    riding = [((l, name), parts[name]) for name, _ in branch] + list(waiting)
    (dxr, dgr, g_cw, g_cb, g_ba_, g_bx, g_lam, g_wa2, g_wx2), got = _branch_a_bwd(
        dya_pre, sv["proj"], sv["hseq"], p["conv_w"], p["conv_b"], p["wa2"], p["lru_b_a"], p["wx2"], p["lru_b_x"],
        p["lru_lambda"], p["wa2t"], p["wx2t"], tc=TC_BWD, name=f"branch_a_bwd_{l}", comm=_chips_comm([part for _, part in riding]),
    )
    for (key, _), landed in zip(riding, got):
        from_chips[key] = landed
    dproj = [dxr, dgr, du, dv, dga, dgb]
    g_in = _matmul_tn(dproj, sv["h"], relu2=False, tka=TKA_PIECES, name=f"grad_w_in_{l}")
    if last:
        (got,) = _comm_only(_sibling_comm([by_device(g_in)]), name=f"grad_w_in_to_sibling_{l}")
        with_sibling("w_in", g_in, got)
        riding = _chips_comm([parts["w_in"]])
    else:
        riding = _sibling_comm([by_device(g_in)])
    (dx, dxb, g_norm_mix), (got,) = _matmul_nn_rmsnorm_bwd(
        dproj, w["w_in"], sv["x"], p["norm_mix_g"], dx1, tm=TM, name=f"in_proj_bwd_{l}", comm=riding
    )
    if last:
        from_chips[l, "w_in"] = got
    else:
        with_sibling("w_in", g_in, got)
    small = dict(
        norm_mix_g=g_norm_mix[0], conv_w=g_cw, conv_b=g_cb[0], lru_w_a=_diag_blocks(g_wa2), lru_b_a=g_ba_.reshape(RNN_HEADS, HEAD_DIM),
        lru_w_x=_diag_blocks(g_wx2), lru_b_x=g_bx.reshape(RNN_HEADS, HEAD_DIM), lru_lambda=g_lam[0], sgu_ln_g=g_lng[0],
        sgu_ln_b=g_lnb[0], sgu_w_s=g_ws, sgu_b_s=g_bs[:, :, 0], norm_ffn_g=g_norm_ffn[0],
    )
    return dx, dxb, small, parts, from_chips


def _prepare_small(l, given):
    chunk_id = jnp.arange(SGU_BLOCK) // CHUNK
    mask = (chunk_id[:, None] >= chunk_id[None, :]).astype(F32)
    wm = given["sgu_w_s"][l] * mask
    wa2 = _block_diag_pairs(given["lru_w_a"][l])
    wx2 = _block_diag_pairs(given["lru_w_x"][l])
    row = lambda a: a.reshape(1, -1)
    return dict(
        norm_mix_g=row(given["norm_mix_g"][l]),
        norm_ffn_g=row(given["norm_ffn_g"][l]),
        conv_w=given["conv_w_full"][l],
        conv_b=row(given["conv_b"][l]),
        wa2=wa2.astype(BF16),
        wx2=wx2.astype(BF16),
        wa2t=jnp.swapaxes(wa2, 1, 2).astype(BF16),
        wx2t=jnp.swapaxes(wx2, 1, 2).astype(BF16),
        lru_b_a=row(given["lru_b_a"][l]),
        lru_b_x=row(given["lru_b_x"][l]),
        lru_lambda=row(given["lru_lambda"][l]),
        sgu_ln_g=row(given["sgu_ln_g"][l]),
        sgu_ln_b=row(given["sgu_ln_b"][l]),
        wm=wm.astype(BF16),
        wmt=jnp.swapaxes(wm, 1, 2).astype(BF16),
        sgu_bias=jnp.broadcast_to(given["sgu_b_s"][l][:, :, None], (SGU_GROUPS, SGU_BLOCK, LANES)),
        mask=mask,
    )


def _step(given):
    x_idx, y_idx, c_idx = _position()
    dev = 4 * x_idx + 2 * y_idx + c_idx
    core = c_idx.astype(jnp.int32).reshape(1)
    chip = (2 * x_idx + y_idx).astype(jnp.int32).reshape(1)

    def rows_first(name, a):
        return jnp.swapaxes(a, 1, 2) if name in TRANSPOSED else a

    shards = []
    for l in range(DEPTH):
        shards.append({name: rows_first(name, given[name])[l].astype(BF16)[None] for name in BIG})
    conv_mine = given["conv_w"].reshape(1, DEPTH * CONV_WIDTH, D_RNN // N_DEV)
    w_in_first, conv_all = _comm_only(_gather_comm([shards[0]["w_in"], conv_mine]), name="gather_first")
    weights = [{"w_in": w_in_first.reshape(-1, D)}, {}]
    conv_all = conv_all.reshape(N_DEV, DEPTH, CONV_WIDTH, D_RNN // N_DEV)
    given = dict(given, conv_w_full=jnp.moveaxis(conv_all, 0, 2).reshape(DEPTH, CONV_WIDTH, D_RNN))

    small_params = [_prepare_small(l, given) for l in range(DEPTH)]
    x = given["x"][0]
    saved, arriving = [], {}
    loss_head = (given["final_norm_g"].reshape(1, D), given["loss_target"][0])
    for l in range(DEPTH):
        x, sv = _layer_forward(
            l, x, small_params[l], weights, shards, arriving, loss_head=loss_head if l == DEPTH - 1 else None
        )
        saved.append(sv)
    dx, dxb, g_final, loss = x
    small_grads, parts, from_chips, waiting = [None] * DEPTH, [None] * DEPTH, {}, []
    for l in reversed(range(DEPTH)):
        dx, dxb, small_grads[l], parts[l], got = _layer_backward(
            l, dx, dxb, saved[l], small_params[l], weights[l], core, waiting, last=l == 0
        )
        from_chips.update(got)
        waiting = [((l, "w_in"), parts[l]["w_in"])]

    small_list = []
    for name in SMALL[:-1]:
        small_list.append(jnp.stack([small_grads[l][name] for l in range(DEPTH)]))
    small_list += [g_final[0], loss[0, :1]]
    small_shapes = [a.shape for a in small_list]
    pack = _pack(small_list, SMALL_ROWS).reshape(N_DEV, SMALL_ROWS_PER_DEV, D)
    summed = _unpack(_all_reduce_small(pack, name="all_reduce_small"), small_shapes)
    loss_total = summed[-1][0]
    grads = dict(zip(SMALL, summed[:-1]))
    cw = grads["conv_w"].reshape(DEPTH, CONV_WIDTH, N_DEV, D_RNN // N_DEV)
    grads["conv_w"] = lax.dynamic_index_in_dim(cw, dev, axis=2, keepdims=False)

    delta, new_m, new_v = {}, {}, {}
    for name in BIG:
        w, m, v = given[name], given["m_" + name], given["v_" + name]
        mine = [parts[l][name] for l in range(DEPTH)]
        theirs = [from_chips[l, name] for l in range(DEPTH)]
        if name == "w_up":
            sums = [_sum_chips(mine[l], theirs[l], chip, name=f"sum_chips_{name}_{l}").T for l in range(DEPTH)]
            out = _adamw_layers(w, sums, m, v, tr=TR, name=f"adamw_{name}")
        else:
            out = _adamw_reduced(
                rows_first(name, w), mine, theirs, rows_first(name, m), rows_first(name, v), chip, tr=TR, name=f"adamw_{name}"
            )
            out = [rows_first(name, a) for a in out]
        grads[name], delta[name], new_m[name], new_v[name] = out
    two_d = lambda a: a.reshape(1, -1) if a.ndim == 1 else a
    groups = [tuple(two_d(a) for a in (given[n], grads[n], given["m_" + n], given["v_" + n])) for n in SMALL]
    for n, (d, m2, v2) in zip(SMALL, _adamw_small(groups, name="adamw_small")):
        shape = given[n].shape
        delta[n], new_m[n], new_v[n] = d.reshape(shape), m2.reshape(shape), v2.reshape(shape)

    return (
        loss_total, dx[None],
        *[grads[n] for n in WEIGHTS], *[delta[n] for n in WEIGHTS], *[new_m[n] for n in WEIGHTS], *[new_v[n] for n in WEIGHTS],
    )


def kernel(x, norm_mix_g, w_in, conv_w, conv_b, lru_w_a, lru_b_a, lru_w_x, lru_b_x, lru_lambda, sgu_ln_g, sgu_ln_b, sgu_w_s, sgu_b_s, w_branch_a, w_branch_b, w_out, norm_ffn_g, w_up, w_down, final_norm_g, loss_target, m_norm_mix_g, m_w_in, m_conv_w, m_conv_b, m_lru_w_a, m_lru_b_a, m_lru_w_x, m_lru_b_x, m_lru_lambda, m_sgu_ln_g, m_sgu_ln_b, m_sgu_w_s, m_sgu_b_s, m_w_branch_a, m_w_branch_b, m_w_out, m_norm_ffn_g, m_w_up, m_w_down, m_final_norm_g, v_norm_mix_g, v_w_in, v_conv_w, v_conv_b, v_lru_w_a, v_lru_b_a, v_lru_w_x, v_lru_b_x, v_lru_lambda, v_sgu_ln_g, v_sgu_ln_b, v_sgu_w_s, v_sgu_b_s, v_w_branch_a, v_w_branch_b, v_w_out, v_norm_ffn_g, v_w_up, v_w_down, v_final_norm_g):
    return _step(dict(locals()))
```

```python
import jax
import jax.numpy as jnp
from jax import lax
from jax.experimental import pallas as pl
from jax.experimental.pallas import tpu as pltpu

F32 = jnp.float32
BF16 = jnp.bfloat16
SDS = jax.ShapeDtypeStruct
MESH = pl.DeviceIdType.MESH

D = 1024
D_RNN = 1280
D_SGU = 1024
D_FF = 4096
D_IN = 2 * D_RNN + 2 * D_SGU + 2 * D
DEPTH = 2
RNN_HEADS = 20
HEAD_DIM = 64
CONV_WIDTH = 4
LRU_C = 8.0
SGU_GROUPS = 8
SGU_BLOCK = 128
CHUNK = 64
EPS = 1e-6
N_DEV = 8

ADAM_LR = 0.001
ADAM_B1 = 0.9
ADAM_B2 = 0.999
ADAM_EPS = 1e-08
ADAM_WD = 0.01
ADAM_STEP = 10

LANES = 128
SUBLANES = 8
VMEM_LIMIT_BYTES = 56 * 1024 * 1024

N_RNN_TILES = D_RNN // LANES
RNN_TILES_PER_STEP = 5
U_BLK512 = (2 * D_RNN) // 512
V_BLK512 = (2 * D_RNN + D_SGU) // 512
GA_BLK512 = (2 * D_RNN + 2 * D_SGU) // 512
GB_BLK512 = (2 * D_RNN + 2 * D_SGU + D) // 512

SMALL_ROWS_PER_DEV = 80
SMALL_ROWS = N_DEV * SMALL_ROWS_PER_DEV


def _params(*sem):
    return pltpu.CompilerParams(dimension_semantics=sem, vmem_limit_bytes=VMEM_LIMIT_BYTES)


def _sigmoid(x):
    return 0.5 + 0.5 * jnp.tanh(0.5 * x)


_GELU_C = 0.7978845608028654
_GELU_K = 0.044715


def _gelu(x):
    t = jnp.tanh(_GELU_C * (x + _GELU_K * x * x * x))
    return 0.5 * x * (1.0 + t)


def _gelu_and_grad(x):
    t = jnp.tanh(_GELU_C * (x + _GELU_K * x * x * x))
    val = 0.5 * x * (1.0 + t)
    grad = 0.5 * (1.0 + t) + 0.5 * x * (1.0 - t * t) * _GELU_C * (1.0 + 3.0 * _GELU_K * x * x)
    return val, grad


def _one_minus_square(log_a, a):
    return -jnp.tanh(log_a) * (1.0 + a * a)


def _dot(a, b):
    return jnp.dot(a, b, preferred_element_type=F32)


def _dot_nt(a, b):
    return lax.dot_general(a, b, (((1,), (1,)), ((), ())), preferred_element_type=F32)


def _dot_tn(a, b):
    return lax.dot_general(a, b, (((0,), (0,)), ((), ())), preferred_element_type=F32)


def _norm_matmul_nt(x, g, w, *, tm, tn, name, comm=None):
    s, n = x.shape[0], w.shape[0]
    tm, tn = min(tm, s), min(tn, n)

    def body(x_ref, g_ref, w_ref, o_ref, h_ref):
        @pl.when(pl.program_id(1) == 0)
        def _():
            xv = x_ref[...]
            r = lax.rsqrt(jnp.mean(xv * xv, axis=-1, keepdims=True) + EPS)
            h_ref[...] = (xv * r * g_ref[...]).astype(BF16)

        o_ref[...] = _dot_nt(h_ref[...], w_ref[...]).astype(o_ref.dtype)

    return _call(
        body,
        (x, g, w),
        name=name,
        grid=(s // tm, n // tn),
        in_specs=[
            pl.BlockSpec((tm, D), lambda i, j: (i, 0)),
            pl.BlockSpec((1, D), lambda i, j: (0, 0)),
            pl.BlockSpec((tn, D), lambda i, j: (j, 0)),
        ],
        out_specs=[pl.BlockSpec((tm, tn), lambda i, j: (i, j)), pl.BlockSpec((tm, D), lambda i, j: (i, 0))],
        out_shape=[SDS((s, n), BF16), SDS((s, D), BF16)],
        semantics=("parallel", "arbitrary"),
        comm=comm,
    )


def _matmul_nn_res(a, w, res, *, relu2, tm, name, comm=None):
    s, k = a.shape
    tm = min(tm, s)

    def body(a_ref, w_ref, r_ref, o_ref):
        av = a_ref[...]
        if relu2:
            t = jnp.maximum(av.astype(F32), 0.0)
            av = (t * t).astype(BF16)
        o_ref[...] = r_ref[...] + _dot(av, w_ref[...])

    return _call(
        body,
        (a, w, res),
        name=name,
        grid=(s // tm,),
        in_specs=[
            pl.BlockSpec((tm, k), lambda i: (i, 0)),
            pl.BlockSpec((k, D), lambda i: (0, 0)),
            pl.BlockSpec((tm, D), lambda i: (i, 0)),
        ],
        out_specs=pl.BlockSpec((tm, D), lambda i: (i, 0)),
        out_shape=SDS((s, D), F32),
        semantics=("parallel",),
        comm=comm,
    )


def _matmul_nt_drelu2(a, w, pre, *, tm, tn, name):
    s, n = a.shape[0], w.shape[0]
    tm, tn = min(tm, s), min(tn, n)

    def body(a_ref, w_ref, p_ref, o_ref):
        d = _dot_nt(a_ref[...], w_ref[...])
        o_ref[...] = (d * (2.0 * jnp.maximum(p_ref[...].astype(F32), 0.0))).astype(o_ref.dtype)

    return pl.pallas_call(
        body,
        name=name,
        grid=(s // tm, n // tn),
        in_specs=[
            pl.BlockSpec((tm, D), lambda i, j: (i, 0)),
            pl.BlockSpec((tn, D), lambda i, j: (j, 0)),
            pl.BlockSpec((tm, tn), lambda i, j: (i, j)),
        ],
        out_specs=pl.BlockSpec((tm, tn), lambda i, j: (i, j)),
        out_shape=SDS((s, n), BF16),
        compiler_params=_params("parallel", "arbitrary"),
    )(a, w, pre)


def _matmul_tn(a_list, b, *, relu2, tka, name, comm=None):
    s = b.shape[0]
    n = len(a_list)
    nblk = [a.shape[1] // tka for a in a_list]
    starts = [sum(nblk[:p]) for p in range(n)]

    def body(*refs):
        a_refs, b_ref, o_ref = refs[:n], refs[n], refs[n + 1]
        i = pl.program_id(0)
        for p in range(n):

            @pl.when((i >= starts[p]) & (i < starts[p] + nblk[p]))
            def _(p=p):
                av = a_refs[p][...]
                if relu2:
                    t = jnp.maximum(av.astype(F32), 0.0)
                    av = (t * t).astype(BF16)
                o_ref[...] = _dot_tn(av, b_ref[...]).astype(o_ref.dtype)

    def piece_spec(p):
        return pl.BlockSpec((s, tka), lambda i: (0, jnp.clip(i - starts[p], 0, nblk[p] - 1)))

    return _call(
        body,
        (*a_list, b),
        name=name,
        grid=(sum(nblk),),
        in_specs=[piece_spec(p) for p in range(n)] + [pl.BlockSpec((s, D), lambda i: (0, 0))],
        out_specs=pl.BlockSpec((tka, D), lambda i: (i, 0)),
        out_shape=SDS((sum(nblk) * tka, D), BF16),
        semantics=("parallel",),
        comm=comm,
    )


def _matmul_nn_rmsnorm_bwd(a_list, w, x, g, res, *, tm, name, comm=None):
    s = x.shape[0]
    tm = min(tm, s)
    n = len(a_list)
    widths = [a.shape[1] for a in a_list]
    offs = [sum(widths[:p]) for p in range(n)]
    k = sum(widths)

    def body(*refs):
        a_refs = refs[:n]
        w_ref, x_ref, g_ref, r_ref, dx_ref, dxb_ref, dg_ref = refs[n:]

        @pl.when(pl.program_id(0) == 0)
        def _():
            dg_ref[...] = jnp.zeros_like(dg_ref)

        dh = _dot(a_refs[0][...], w_ref[0 : widths[0], :])
        for p in range(1, n):
            dh += _dot(a_refs[p][...], w_ref[offs[p] : offs[p] + widths[p], :])
        xv = x_ref[...]
        r = lax.rsqrt(jnp.mean(xv * xv, axis=-1, keepdims=True) + EPS)
        xhat = xv * r
        dxh = dh * g_ref[...]
        dx = r_ref[...] + r * (dxh - xhat * jnp.mean(dxh * xhat, axis=-1, keepdims=True))
        dx_ref[...] = dx
        dxb_ref[...] = dx.astype(BF16)
        dg_ref[...] += jnp.sum(dh * xhat, axis=0, keepdims=True)

    act = pl.BlockSpec((tm, D), lambda i: (i, 0))
    vec = pl.BlockSpec((1, D), lambda i: (0, 0))
    return _call(
        body,
        (*a_list, w, x, g, res),
        name=name,
        grid=(s // tm,),
        in_specs=[pl.BlockSpec((tm, wd), lambda i: (i, 0)) for wd in widths]
        + [pl.BlockSpec((k, D), lambda i: (0, 0), pipeline_mode=pl.Buffered(1)), act, vec, act],
        out_specs=[act, act, vec],
        out_shape=[SDS((s, D), F32), SDS((s, D), BF16), SDS((1, D), F32)],
        semantics=("arbitrary",),
        comm=comm,
    )


def _rows_after(ext, k, n):
    return pltpu.roll(ext, n + SUBLANES - k, 0)[:n, :]


def _scan_forward(a, b, n):
    row = lax.broadcasted_iota(jnp.int32, a.shape, 0)
    d = 1
    while d < n:
        if d < SUBLANES:
            m = row >= d
            a_s = jnp.where(m, pltpu.roll(a, d, 0), 1.0)
            b_s = jnp.where(m, pltpu.roll(b, d, 0), 0.0)
            b = a * b_s + b
            a = a * a_s
        else:
            b = jnp.concatenate([b[:d], a[d:] * b[: n - d] + b[d:]], axis=0)
            a = jnp.concatenate([a[:d], a[d:] * a[: n - d]], axis=0)
        d *= 2
    return a, b


def _scan_backward(a, b, n):
    row = lax.broadcasted_iota(jnp.int32, a.shape, 0)
    d = 1
    while d < n:
        if d < SUBLANES:
            m = row < n - d
            a_s = jnp.where(m, pltpu.roll(a, n - d, 0), 1.0)
            b_s = jnp.where(m, pltpu.roll(b, n - d, 0), 0.0)
            b = a * b_s + b
            a = a * a_s
        else:
            b = jnp.concatenate([a[: n - d] * b[d:] + b[: n - d], b[n - d :]], axis=0)
            a = jnp.concatenate([a[: n - d] * a[d:], a[n - d :]], axis=0)
        d *= 2
    return b


def _repeat_matrix(n):
    groups = n // SUBLANES
    return (jnp.arange(n)[:, None] // SUBLANES == jnp.arange(3 * groups)[None, :] % groups).astype(BF16)


def _scan_rows(a, b, n, repeat_ref, a_scr, b_scr, reverse):
    groups = n // SUBLANES
    a3 = a.reshape(groups, SUBLANES, LANES)
    b3 = b.reshape(groups, SUBLANES, LANES)
    sub = lax.broadcasted_iota(jnp.int32, a3.shape, 1)
    for d in (1, 2, 4):
        m = (sub < SUBLANES - d) if reverse else (sub >= d)
        shift = SUBLANES - d if reverse else d
        a_s = jnp.where(m, pltpu.roll(a3, shift, 1), 1.0)
        b_s = jnp.where(m, pltpu.roll(b3, shift, 1), 0.0)
        b3 = a3 * b_s + b3
        a3 = a3 * a_s
    a_scr[...] = a3.reshape(n, LANES)
    b_scr[...] = b3.reshape(n, LANES)
    edge = 0 if reverse else SUBLANES - 1
    a_tot = a_scr[pl.ds(edge, groups, stride=SUBLANES), :]
    b_tot = b_scr[pl.ds(edge, groups, stride=SUBLANES), :]
    row = lax.broadcasted_iota(jnp.int32, a_tot.shape, 0)
    if reverse:
        through = _scan_backward(a_tot, b_tot, groups)
        entering = jnp.where(row < groups - 1, pltpu.roll(through, groups - 1, 0), 0.0)
    else:
        _, through = _scan_forward(a_tot, b_tot, groups)
        entering = jnp.where(row >= 1, pltpu.roll(through, 1, 0), 0.0)
    hi = entering.astype(BF16)
    rest = entering - hi.astype(F32)
    mid = rest.astype(BF16)
    lo = (rest - mid.astype(F32)).astype(BF16)
    repeated = _dot(repeat_ref[...], jnp.concatenate([hi, mid, lo], axis=0))
    return b_scr[...] + a_scr[...] * repeated


def _softplus_neg(lam):
    z = -lam
    return jnp.maximum(z, 0.0) + jnp.log1p(jnp.exp(-jnp.abs(z)))


def _conv_and_gates(xc, xprev, cw_ref, cb_ref, wa_ref, ba_ref, wx_ref, bx_ref, lam_ref, ext_scr):
    n = xc.shape[0]
    ext_scr[:SUBLANES, :] = xprev
    ext_scr[SUBLANES:, :] = xc
    x1, x2, x3 = (ext_scr[pl.ds(SUBLANES - k, n), :] for k in (1, 2, 3))
    xr = cb_ref[...] + x3 * cw_ref[0:1, :] + x2 * cw_ref[1:2, :] + x1 * cw_ref[2:3, :] + xc * cw_ref[3:4, :]
    xrb = xr.astype(BF16)
    r = _sigmoid(_dot(xrb, wa_ref[...]) + ba_ref[...])
    i = _sigmoid(_dot(xrb, wx_ref[...]) + bx_ref[...])
    sp = _softplus_neg(lam_ref[...])
    log_a = (-LRU_C * r) * sp
    a = jnp.exp(log_a)
    return xr, (x1, x2, x3), r, i, a, _one_minus_square(log_a, a)


def _branch_a_fwd(proj, cw, cb, wa2, ba, wx2, bx, lam, *, tc, name, comm=None):
    s = proj.shape[0]
    tc = min(tc, s)

    def body(x_ref, g_ref, cw_ref, cb_ref, wa_ref, ba_ref, wx_ref, bx_ref, lam_ref, rep_ref, h_ref, y_ref,
             xprev, hlast, a_scr, b_scr, ext_scr):
        @pl.when(pl.program_id(1) == 0)
        def _():
            xprev[...] = jnp.zeros_like(xprev)
            hlast[...] = jnp.zeros_like(hlast)

        for t in range(RNN_TILES_PER_STEP):
            cols = lambda ref: ref.at[:, pl.ds(t * LANES, LANES)]
            one_tile(
                cols(x_ref), cols(g_ref), cols(cw_ref), cols(cb_ref), wa_ref.at[t], cols(ba_ref), wx_ref.at[t], cols(bx_ref),
                cols(lam_ref), rep_ref, cols(h_ref), cols(y_ref), cols(xprev), cols(hlast), a_scr.at[t], b_scr.at[t],
                ext_scr.at[t],
            )

    def one_tile(x_ref, g_ref, cw_ref, cb_ref, wa_ref, ba_ref, wx_ref, bx_ref, lam_ref, rep_ref, h_ref, y_ref,
                 xprev, hlast, a_scr, b_scr, ext_scr):
        xc = x_ref[...].astype(F32)
        xr, _, r, i, a, om = _conv_and_gates(
            xc, xprev[...], cw_ref, cb_ref, wa_ref, ba_ref, wx_ref, bx_ref, lam_ref, ext_scr
        )
        xprev[...] = xc[tc - SUBLANES :, :]
        u = jnp.sqrt(om) * (i * xr)
        row8 = lax.broadcasted_iota(jnp.int32, (SUBLANES, LANES), 0)
        first = u[:SUBLANES] + jnp.where(row8 == 0, a[:SUBLANES] * hlast[SUBLANES - 1 : SUBLANES, :], 0.0)
        h = _scan_rows(a, jnp.concatenate([first, u[SUBLANES:]], axis=0), tc, rep_ref, a_scr, b_scr, reverse=False)
        hlast[...] = h[tc - SUBLANES :, :]
        h_ref[...] = h
        y_ref[...] = (h * _gelu(g_ref[...].astype(F32))).astype(BF16)

    wide = RNN_TILES_PER_STEP * LANES
    tile = lambda j, c: (0, j)
    vec = pl.BlockSpec((1, wide), tile)
    mats = pl.BlockSpec((RNN_TILES_PER_STEP, LANES, LANES), lambda j, c: (j, 0, 0))
    repeat = _repeat_matrix(tc)
    return _call(
        body,
        (proj, proj, cw, cb, wa2, ba, wx2, bx, lam, repeat),
        name=name,
        grid=(N_RNN_TILES // RNN_TILES_PER_STEP, s // tc),
        in_specs=[
            pl.BlockSpec((tc, wide), lambda j, c: (c, j)),
            pl.BlockSpec((tc, wide), lambda j, c: (c, D_RNN // wide + j)),
            pl.BlockSpec((CONV_WIDTH, wide), tile),
            vec,
            mats,
            vec,
            mats,
            vec,
            vec,
            pl.BlockSpec(repeat.shape, lambda j, c: (0, 0)),
        ],
        out_specs=[pl.BlockSpec((tc, wide), lambda j, c: (c, j)), pl.BlockSpec((tc, wide), lambda j, c: (c, j))],
        out_shape=[SDS((s, D_RNN), F32), SDS((s, D_RNN), BF16)],
        scratch_shapes=[pltpu.VMEM((SUBLANES, wide), F32)] * 2
        + [pltpu.VMEM((RNN_TILES_PER_STEP, tc, LANES), F32)] * 2
        + [pltpu.VMEM((RNN_TILES_PER_STEP, tc + SUBLANES, LANES), F32)],
        semantics=("parallel", "arbitrary"),
        comm=comm,
    )


def _branch_a_bwd(dy, proj, h, cw, cb, wa2, ba, wx2, bx, lam, wa2t, wx2t, *, tc, name, comm=None):
    s = proj.shape[0]
    tc = min(tc, s)
    nc = s // tc
    halo16 = tc // 16
    halo8 = tc // SUBLANES

    def body(dy_ref, x_ref, xh_ref, g_ref, h_ref, hh_ref, cw_ref, cb_ref, wa_ref, ba_ref, wx_ref, bx_ref, lam_ref,
             wat_ref, wxt_ref, rep_ref, dx_ref, dg_ref, dcw_ref, dcb_ref, dba_ref, dbx_ref, dlam_ref, dwa_ref, dwx_ref,
             carry, dxr_next, a_scr, b_scr, ext_scr):
        cc = pl.program_id(1)
        ct = nc - 1 - cc

        @pl.when(cc == 0)
        def _():
            carry[...] = jnp.zeros_like(carry)
            dxr_next[...] = jnp.zeros_like(dxr_next)
            for ref in (dcw_ref, dcb_ref, dba_ref, dbx_ref, dlam_ref, dwa_ref, dwx_ref):
                ref[...] = jnp.zeros_like(ref)

        for t in range(RNN_TILES_PER_STEP):
            cols = lambda ref: ref.at[:, pl.ds(t * LANES, LANES)]
            one_tile(
                ct, cols(dy_ref), cols(x_ref), cols(xh_ref), cols(g_ref), cols(h_ref), cols(hh_ref), cols(cw_ref), cols(cb_ref),
                wa_ref.at[t], cols(ba_ref), wx_ref.at[t], cols(bx_ref), cols(lam_ref), wat_ref.at[t], wxt_ref.at[t], rep_ref,
                cols(dx_ref), cols(dg_ref), cols(dcw_ref), cols(dcb_ref), cols(dba_ref), cols(dbx_ref), cols(dlam_ref),
                dwa_ref.at[t], dwx_ref.at[t], cols(carry), cols(dxr_next), a_scr.at[t], b_scr.at[t], ext_scr.at[t],
            )

    def one_tile(ct, dy_ref, x_ref, xh_ref, g_ref, h_ref, hh_ref, cw_ref, cb_ref, wa_ref, ba_ref, wx_ref, bx_ref, lam_ref,
                 wat_ref, wxt_ref, rep_ref, dx_ref, dg_ref, dcw_ref, dcb_ref, dba_ref, dbx_ref, dlam_ref, dwa_ref, dwx_ref,
                 carry, dxr_next, a_scr, b_scr, ext_scr):
        xc = x_ref[...].astype(F32)
        xprev = jnp.where(ct > 0, xh_ref[SUBLANES:, :].astype(F32), 0.0)
        xr, (x1, x2, x3), r, i, a, om = _conv_and_gates(
            xc, xprev, cw_ref, cb_ref, wa_ref, ba_ref, wx_ref, bx_ref, lam_ref, ext_scr
        )
        inv_norm = lax.rsqrt(om)
        norm = om * inv_norm
        row = lax.broadcasted_iota(jnp.int32, xc.shape, 0)

        hv = h_ref[...]
        ge, ge_grad = _gelu_and_grad(g_ref[...].astype(F32))
        dyv = dy_ref[...].astype(F32)
        dg_ref[...] = (dyv * hv * ge_grad).astype(dg_ref.dtype)
        dh = dyv * ge

        b = dh + jnp.where(row == tc - 1, carry[0:1, :], 0.0)
        a_next = jnp.where(row < tc - 1, pltpu.roll(a, tc - 1, 0), 0.0)
        gadj = _scan_rows(a_next, b, tc, rep_ref, a_scr, b_scr, reverse=True)
        carry[...] = (a * gadj)[:SUBLANES, :]

        hprev_first = jnp.where(ct > 0, hh_ref[SUBLANES - 1 : SUBLANES, :], 0.0)
        hprev = jnp.where(row >= 1, pltpu.roll(hv, 1, 0), hprev_first)
        da = gadj * hprev
        ix = i * xr
        dnorm = gadj * ix
        di = gadj * norm * xr
        dlog_a = da * a - dnorm * (1.0 - om) * inv_norm
        sp = _softplus_neg(lam_ref[...])
        dr = dlog_a * (-LRU_C * sp)
        dsp = jnp.sum(dlog_a * (-LRU_C * r), axis=0, keepdims=True)
        dlam_ref[...] += dsp * (-_sigmoid(-lam_ref[...]))
        dza = dr * r * (1.0 - r)
        dzx = di * i * (1.0 - i)
        dzab, dzxb = dza.astype(BF16), dzx.astype(BF16)
        dxr = gadj * norm * i + _dot(dzab, wat_ref[...]) + _dot(dzxb, wxt_ref[...])
        xrb = xr.astype(BF16)
        dwa_ref[...] += _dot_tn(xrb, dzab)
        dwx_ref[...] += _dot_tn(xrb, dzxb)
        dba_ref[...] += jnp.sum(dza, axis=0, keepdims=True)
        dbx_ref[...] += jnp.sum(dzx, axis=0, keepdims=True)

        ext = jnp.concatenate([dxr, dxr_next[...]], axis=0)
        dx = (
            dxr * cw_ref[3:4, :]
            + _rows_after(ext, 1, tc) * cw_ref[2:3, :]
            + _rows_after(ext, 2, tc) * cw_ref[1:2, :]
            + _rows_after(ext, 3, tc) * cw_ref[0:1, :]
        )
        dxr_next[...] = dxr[:SUBLANES, :]
        dx_ref[...] = dx.astype(dx_ref.dtype)
        dcb_ref[...] += jnp.sum(dxr, axis=0, keepdims=True)
        dcw_ref[3:4, :] += jnp.sum(dxr * xc, axis=0, keepdims=True)
        dcw_ref[2:3, :] += jnp.sum(dxr * x1, axis=0, keepdims=True)
        dcw_ref[1:2, :] += jnp.sum(dxr * x2, axis=0, keepdims=True)
        dcw_ref[0:1, :] += jnp.sum(dxr * x3, axis=0, keepdims=True)

    wide = RNN_TILES_PER_STEP * LANES
    tile = lambda j, c: (0, j)
    mat = lambda j, c: (j, 0, 0)
    cur = lambda j, c: (nc - 1 - c, j)
    vec = pl.BlockSpec((1, wide), tile)
    matspec = pl.BlockSpec((RNN_TILES_PER_STEP, LANES, LANES), mat)
    repeat = _repeat_matrix(tc)
    return _call(
        body,
        (dy, proj, proj, proj, h, h, cw, cb, wa2, ba, wx2, bx, lam, wa2t, wx2t, repeat),
        name=name,
        grid=(N_RNN_TILES // RNN_TILES_PER_STEP, nc),
        in_specs=[
            pl.BlockSpec((tc, wide), cur),
            pl.BlockSpec((tc, wide), cur),
            pl.BlockSpec((16, wide), lambda j, c: (jnp.maximum((nc - 1 - c) * halo16 - 1, 0), j)),
            pl.BlockSpec((tc, wide), lambda j, c: (nc - 1 - c, D_RNN // wide + j)),
            pl.BlockSpec((tc, wide), cur),
            pl.BlockSpec((SUBLANES, wide), lambda j, c: (jnp.maximum((nc - 1 - c) * halo8 - 1, 0), j)),
            pl.BlockSpec((CONV_WIDTH, wide), tile),
            vec,
            matspec,
            vec,
            matspec,
            vec,
            vec,
            matspec,
            matspec,
            pl.BlockSpec(repeat.shape, lambda j, c: (0, 0)),
        ],
        out_specs=[
            pl.BlockSpec((tc, wide), cur),
            pl.BlockSpec((tc, wide), cur),
            pl.BlockSpec((CONV_WIDTH, wide), tile),
            vec,
            vec,
            vec,
            vec,
            matspec,
            matspec,
        ],
        out_shape=[
            SDS((s, D_RNN), BF16),
            SDS((s, D_RNN), BF16),
            SDS((CONV_WIDTH, D_RNN), F32),
            SDS((1, D_RNN), F32),
            SDS((1, D_RNN), F32),
            SDS((1, D_RNN), F32),
            SDS((1, D_RNN), F32),
            SDS((N_RNN_TILES, LANES, LANES), F32),
            SDS((N_RNN_TILES, LANES, LANES), F32),
        ],
        scratch_shapes=[pltpu.VMEM((SUBLANES, wide), F32)] * 2
        + [pltpu.VMEM((RNN_TILES_PER_STEP, tc, LANES), F32)] * 2
        + [pltpu.VMEM((RNN_TILES_PER_STEP, tc + SUBLANES, LANES), F32)],
        semantics=("parallel", "arbitrary"),
        comm=comm,
    )


def _sgu_specs(tb):
    half = lambda blk: pl.BlockSpec((tb, 512), lambda n: (n, blk))
    return [half(U_BLK512), half(U_BLK512 + 1), half(V_BLK512), half(V_BLK512 + 1)]


def _sgu_normed(v, lng_ref, lnb_ref):
    gv, gv_grad = _gelu_and_grad(v)
    mu = jnp.mean(gv, axis=-1, keepdims=True)
    xc = gv - mu
    rs = lax.rsqrt(jnp.mean(xc * xc, axis=-1, keepdims=True) + EPS)
    xhat = xc * rs
    return xhat * lng_ref[...] + lnb_ref[...], xhat, rs, gv_grad


def _sgu_fwd(proj, lng, lnb, wm, bias, *, tb, name, comm=None):
    s = proj.shape[0]
    tb = min(tb, s)

    def body(u0_ref, u1_ref, v0_ref, v1_ref, lng_ref, lnb_ref, wm_ref, bias_ref, y_ref):
        u = jnp.concatenate([u0_ref[...], u1_ref[...]], axis=1).astype(F32)
        v = jnp.concatenate([v0_ref[...], v1_ref[...]], axis=1).astype(F32)
        gu = _gelu(u)
        vn, _, _, _ = _sgu_normed(v, lng_ref, lnb_ref)
        vnb = vn.astype(BF16)
        for blk in range(tb // SGU_BLOCK):
            rows = slice(blk * SGU_BLOCK, (blk + 1) * SGU_BLOCK)
            for g in range(SGU_GROUPS):
                cols = slice(g * LANES, (g + 1) * LANES)
                mixed = _dot(wm_ref[g], vnb[rows, cols]) + bias_ref[g]
                y_ref[rows, cols] = (gu[rows, cols] * mixed).astype(BF16)

    const2 = lambda n: (0, 0)
    const3 = lambda n: (0, 0, 0)
    return _call(
        body,
        (proj, proj, proj, proj, lng, lnb, wm, bias),
        name=name,
        grid=(s // tb,),
        in_specs=_sgu_specs(tb)
        + [
            pl.BlockSpec((1, D_SGU), const2),
            pl.BlockSpec((1, D_SGU), const2),
            pl.BlockSpec((SGU_GROUPS, SGU_BLOCK, SGU_BLOCK), const3),
            pl.BlockSpec((SGU_GROUPS, SGU_BLOCK, LANES), const3),
        ],
        out_specs=pl.BlockSpec((tb, D_SGU), lambda n: (n, 0)),
        out_shape=SDS((s, D_SGU), BF16),
        semantics=("parallel",),
        comm=comm,
    )


def _sgu_bwd(dy, proj, lng, lnb, wm, wmt, bias, mask, *, tb, name, comm=None):
    s = proj.shape[0]
    tb = min(tb, s)
    nb = s // tb

    def body(dy_ref, u0_ref, u1_ref, v0_ref, v1_ref, lng_ref, lnb_ref, wm_ref, wmt_ref, bias_ref, mask_ref,
             du_ref, dv_ref, dws_ref, dbs_ref, dlng_ref, dlnb_ref, dvn_scr, dbs_acc):
        n = pl.program_id(0)

        @pl.when(n == 0)
        def _():
            dbs_acc[...] = jnp.zeros_like(dbs_acc)
            for ref in (dws_ref, dlng_ref, dlnb_ref):
                ref[...] = jnp.zeros_like(ref)

        u = jnp.concatenate([u0_ref[...], u1_ref[...]], axis=1).astype(F32)
        v = jnp.concatenate([v0_ref[...], v1_ref[...]], axis=1).astype(F32)
        gu, gu_grad = _gelu_and_grad(u)
        vn, xhat, rs, gv_grad = _sgu_normed(v, lng_ref, lnb_ref)
        vnb = vn.astype(BF16)
        dyv = dy_ref[...].astype(F32)
        for blk in range(tb // SGU_BLOCK):
            rows = slice(blk * SGU_BLOCK, (blk + 1) * SGU_BLOCK)
            for g in range(SGU_GROUPS):
                cols = slice(g * LANES, (g + 1) * LANES)
                vt = vnb[rows, cols]
                mixed = _dot(wm_ref[g], vt) + bias_ref[g]
                dyt = dyv[rows, cols]
                du_ref[rows, cols] = (dyt * mixed * gu_grad[rows, cols]).astype(BF16)
                dmix = dyt * gu[rows, cols]
                dmixb = dmix.astype(BF16)
                dvn_scr[rows, cols] = _dot(wmt_ref[g], dmixb)
                dws_ref[g] += _dot_nt(dmixb, vt) * mask_ref[...]
                dbs_acc[g] += dmix
        dvn = dvn_scr[...]
        dlng_ref[...] += jnp.sum(dvn * xhat, axis=0, keepdims=True)
        dlnb_ref[...] += jnp.sum(dvn, axis=0, keepdims=True)
        dxh = dvn * lng_ref[...]
        dgv = rs * (
            dxh - jnp.mean(dxh, axis=-1, keepdims=True) - xhat * jnp.mean(dxh * xhat, axis=-1, keepdims=True)
        )
        dv_ref[...] = (dgv * gv_grad).astype(BF16)

        @pl.when(n == nb - 1)
        def _():
            for g in range(SGU_GROUPS):
                dbs_ref[g] = jnp.broadcast_to(jnp.sum(dbs_acc[g], axis=-1, keepdims=True), (SGU_BLOCK, LANES))

    const2 = lambda n: (0, 0)
    const3 = lambda n: (0, 0, 0)
    gmat = pl.BlockSpec((SGU_GROUPS, SGU_BLOCK, SGU_BLOCK), const3)
    vec = pl.BlockSpec((1, D_SGU), const2)
    act = pl.BlockSpec((tb, D_SGU), lambda n: (n, 0))
    return _call(
        body,
        (dy, proj, proj, proj, proj, lng, lnb, wm, wmt, bias, mask),
        name=name,
        grid=(nb,),
        in_specs=[act] + _sgu_specs(tb) + [vec, vec, gmat, gmat, gmat, pl.BlockSpec((SGU_BLOCK, SGU_BLOCK), const2)],
        out_specs=[act, act, gmat, gmat, vec, vec],
        out_shape=[
            SDS((s, D_SGU), BF16),
            SDS((s, D_SGU), BF16),
            SDS((SGU_GROUPS, SGU_BLOCK, SGU_BLOCK), F32),
            SDS((SGU_GROUPS, SGU_BLOCK, LANES), F32),
            SDS((1, D_SGU), F32),
            SDS((1, D_SGU), F32),
        ],
        scratch_shapes=[pltpu.VMEM((tb, D_SGU), F32), pltpu.VMEM((SGU_GROUPS, SGU_BLOCK, LANES), F32)],
        semantics=("arbitrary",),
        comm=comm,
    )


def _gate_specs(tm):
    half = lambda blk: pl.BlockSpec((tm, 512), lambda i: (i, blk))
    return [half(GA_BLK512), half(GA_BLK512 + 1), half(GB_BLK512), half(GB_BLK512 + 1)]


def _merge_fwd(ya_pre, yb_pre, proj, x, w_ba, w_bb, w_out, *, tm, name, comm=None):
    s = x.shape[0]
    tm = min(tm, s)

    def body(ya_ref, yb_ref, a0, a1, b0, b1, x_ref, wa_ref, wb_ref, wo_ref, x1_ref, yao_ref, ybo_ref):
        ya = _dot(ya_ref[...], wa_ref[...])
        yb = _dot(yb_ref[...], wb_ref[...])
        sa = _sigmoid(jnp.concatenate([a0[...], a1[...]], axis=1).astype(F32))
        sb = _sigmoid(jnp.concatenate([b0[...], b1[...]], axis=1).astype(F32))
        merged = sa * ya + sb * yb
        x1_ref[...] = x_ref[...] + _dot(merged.astype(BF16), wo_ref[...])
        yao_ref[...] = ya.astype(BF16)
        ybo_ref[...] = yb.astype(BF16)

    whole = lambda r: pl.BlockSpec((r, D), lambda i: (0, 0))
    act = pl.BlockSpec((tm, D), lambda i: (i, 0))
    return _call(
        body,
        (ya_pre, yb_pre, proj, proj, proj, proj, x, w_ba, w_bb, w_out),
        name=name,
        grid=(s // tm,),
        in_specs=[pl.BlockSpec((tm, D_RNN), lambda i: (i, 0)), act] + _gate_specs(tm) + [act, whole(D_RNN), whole(D_SGU), whole(D)],
        out_specs=[act, act, act],
        out_shape=[SDS((s, D), F32), SDS((s, D), BF16), SDS((s, D), BF16)],
        semantics=("parallel",),
        comm=comm,
    )


def _merge_bwd(dx1, ya, yb, proj, w_ba, w_bb, w_out, *, tm, name, comm=None):
    s = dx1.shape[0]
    tm = min(tm, s)

    def body(dx_ref, ya_ref, yb_ref, a0, a1, b0, b1, wa_ref, wb_ref, wo_ref,
             mg_ref, dya_ref, dyb_ref, dga_ref, dgb_ref, dyap_ref, dybp_ref):
        dm = _dot_nt(dx_ref[...], wo_ref[...])
        ya = ya_ref[...].astype(F32)
        yb = yb_ref[...].astype(F32)
        sa = _sigmoid(jnp.concatenate([a0[...], a1[...]], axis=1).astype(F32))
        sb = _sigmoid(jnp.concatenate([b0[...], b1[...]], axis=1).astype(F32))
        mg_ref[...] = (sa * ya + sb * yb).astype(BF16)
        dya = (dm * sa).astype(BF16)
        dyb = (dm * sb).astype(BF16)
        dya_ref[...] = dya
        dyb_ref[...] = dyb
        dga_ref[...] = (dm * ya * sa * (1.0 - sa)).astype(BF16)
        dgb_ref[...] = (dm * yb * sb * (1.0 - sb)).astype(BF16)
        dyap_ref[...] = _dot_nt(dya, wa_ref[...]).astype(BF16)
        dybp_ref[...] = _dot_nt(dyb, wb_ref[...]).astype(BF16)

    whole = lambda r: pl.BlockSpec((r, D), lambda i: (0, 0))
    act = pl.BlockSpec((tm, D), lambda i: (i, 0))
    act_rnn = pl.BlockSpec((tm, D_RNN), lambda i: (i, 0))
    return _call(
        body,
        (dx1, ya, yb, proj, proj, proj, proj, w_ba, w_bb, w_out),
        name=name,
        grid=(s // tm,),
        in_specs=[act, act, act] + _gate_specs(tm) + [whole(D_RNN), whole(D_SGU), whole(D)],
        out_specs=[act, act, act, act, act, act_rnn, act],
        out_shape=[SDS((s, D), BF16)] * 5 + [SDS((s, D_RNN), BF16), SDS((s, D_SGU), BF16)],
        semantics=("parallel",),
        comm=comm,
    )


def _ffn_down_loss(a, w, res, g, target, *, tm, name):
    s, k = a.shape
    tm = min(tm, s)

    def body(a_ref, w_ref, r_ref, g_ref, t_ref, dx_ref, dxb_ref, dg_ref, loss_ref):
        @pl.when(pl.program_id(0) == 0)
        def _():
            dg_ref[...] = jnp.zeros_like(dg_ref)
            loss_ref[...] = jnp.zeros_like(loss_ref)

        t = jnp.maximum(a_ref[...].astype(F32), 0.0)
        xv = r_ref[...] + _dot((t * t).astype(BF16), w_ref[...])
        r = lax.rsqrt(jnp.mean(xv * xv, axis=-1, keepdims=True) + EPS)
        xhat = xv * r
        e = xhat * g_ref[...] - t_ref[...]
        loss_ref[...] += 0.5 * jnp.sum(jnp.mean(e * e, axis=-1, keepdims=True), axis=0, keepdims=True)
        dy = e * (1.0 / D)
        dxh = dy * g_ref[...]
        dx = r * (dxh - xhat * jnp.mean(dxh * xhat, axis=-1, keepdims=True))
        dx_ref[...] = dx
        dxb_ref[...] = dx.astype(BF16)
        dg_ref[...] += jnp.sum(dy * xhat, axis=0, keepdims=True)

    act = pl.BlockSpec((tm, D), lambda i: (i, 0))
    vec = pl.BlockSpec((1, D), lambda i: (0, 0))
    return pl.pallas_call(
        body,
        name=name,
        grid=(s // tm,),
        in_specs=[pl.BlockSpec((tm, k), lambda i: (i, 0)), pl.BlockSpec((k, D), lambda i: (0, 0)), act, vec, act],
        out_specs=[act, act, vec, pl.BlockSpec((SUBLANES, LANES), lambda i: (0, 0))],
        out_shape=[SDS((s, D), F32), SDS((s, D), BF16), SDS((1, D), F32), SDS((SUBLANES, LANES), F32)],
        compiler_params=_params("arbitrary"),
    )(a, w, res, g, target)


def _adamw_math(w, g, m, v):
    m2 = ADAM_B1 * m + (1.0 - ADAM_B1) * g
    v2 = ADAM_B2 * v + (1.0 - ADAM_B2) * (g * g)
    m_hat = m2 / (1.0 - ADAM_B1**ADAM_STEP)
    v_hat = v2 / (1.0 - ADAM_B2**ADAM_STEP)
    delta = -ADAM_LR * (m_hat / (jnp.sqrt(v_hat) + ADAM_EPS) + ADAM_WD * w)
    return delta, m2, v2


def _row_tile(rows, cap):
    return max(t for t in range(SUBLANES, min(cap, rows) + 1, SUBLANES) if rows % t == 0)


def _adamw_layers(w, grads, m, v, *, tr, name):
    depth, r, c = w.shape
    tr = _row_tile(r, tr)

    def body(*refs):
        g_refs = refs[:depth]
        w_ref, m_ref, v_ref, g_out, d_ref, mo_ref, vo_ref = refs[depth:]
        for l in range(depth):

            @pl.when(pl.program_id(0) == l)
            def _(l=l):
                g = g_refs[l][...]
                g_out[...] = g
                d_ref[...], mo_ref[...], vo_ref[...] = _adamw_math(w_ref[...], g, m_ref[...], v_ref[...])

    def of_layer(ll):
        return pl.BlockSpec((tr, c), lambda l, i: (jnp.where(l == ll, i, 0), 0))

    stacked = pl.BlockSpec((None, tr, c), lambda l, i: (l, i, 0))
    return pl.pallas_call(
        body,
        name=name,
        grid=(depth, r // tr),
        in_specs=[of_layer(ll) for ll in range(depth)] + [stacked] * 3,
        out_specs=[stacked] * 4,
        out_shape=[SDS((depth, r, c), F32)] * 4,
        compiler_params=_params("parallel", "parallel"),
    )(*grads, w, m, v)


def _adamw_reduced(w, parts, from_chips, m, v, chip, *, tr, name):
    depth, r, _ = w.shape
    tr = _row_tile(r, tr)

    def body(chip_ref, *refs):
        p_refs, c_refs = refs[:depth], refs[depth : 2 * depth]
        w_ref, m_ref, v_ref, g_out, d_ref, mo_ref, vo_ref = refs[2 * depth :]
        for l in range(depth):

            @pl.when(pl.program_id(0) == l)
            def _(l=l):
                got = c_refs[l]
                g = ((p_refs[l][...].astype(F32) + got[0].astype(F32)) + got[1].astype(F32)) + got[2].astype(F32)
                g_out[...] = g
                d_ref[...], mo_ref[...], vo_ref[...] = _adamw_math(w_ref[...], g, m_ref[...], v_ref[...])

    def mine_of_layer(ll):
        return pl.BlockSpec((None, tr, D), lambda l, i, chip_ref: (chip_ref[0], jnp.where(l == ll, i, 0), 0))

    def theirs_of_layer(ll):
        return pl.BlockSpec((3, tr, D), lambda l, i, chip_ref: (0, jnp.where(l == ll, i, 0), 0))

    stacked = pl.BlockSpec((None, tr, D), lambda l, i, chip_ref: (l, i, 0))
    return pl.pallas_call(
        body,
        name=name,
        grid_spec=pltpu.PrefetchScalarGridSpec(
            num_scalar_prefetch=1,
            grid=(depth, r // tr),
            in_specs=[mine_of_layer(ll) for ll in range(depth)]
            + [theirs_of_layer(ll) for ll in range(depth)]
            + [stacked] * 3,
            out_specs=[stacked] * 4,
        ),
        out_shape=[SDS((depth, r, D), F32)] * 4,
        compiler_params=_params("parallel", "parallel"),
    )(chip, *parts, *from_chips, w, m, v)


def _adamw_small(groups, *, name):
    n = len(groups)

    def body(*refs):
        ins, outs = refs[: 4 * n], refs[4 * n :]
        for i in range(n):
            w, g, m, v = (ref[...] for ref in ins[4 * i : 4 * i + 4])
            outs[3 * i][...], outs[3 * i + 1][...], outs[3 * i + 2][...] = _adamw_math(w, g, m, v)

    vmem = pl.BlockSpec(memory_space=pltpu.VMEM)
    outs = pl.pallas_call(
        body,
        name=name,
        in_specs=[vmem] * (4 * n),
        out_specs=[vmem] * (3 * n),
        out_shape=[SDS(grp[0].shape, F32) for grp in groups for _ in range(3)],
        compiler_params=pltpu.CompilerParams(vmem_limit_bytes=VMEM_LIMIT_BYTES),
    )(*[a for grp in groups for a in grp])
    return [tuple(outs[3 * i : 3 * i + 3]) for i in range(n)]


ANY = pl.BlockSpec(memory_space=pl.ANY)


def _position():
    return lax.axis_index("x"), lax.axis_index("y"), lax.axis_index("c")


def _other_chips(x, y):
    return [(1 - x, y), (x, 1 - y), (1 - x, 1 - y)]


class _Comm:
    def __init__(self, inputs, out_shapes, sem_counts, start, middle, finish, middle_at=1.0, aliases=()):
        self.inputs, self.out_shapes, self.sem_counts = list(inputs), list(out_shapes), list(sem_counts)
        self.start, self.middle, self.finish = start, middle, finish
        self.middle_at = middle_at
        self.aliases = list(aliases)

    def sem_shapes(self):
        return [pltpu.SemaphoreType.DMA((n,)) for n in self.sem_counts]


def _merge_comms(comms):
    bounds, i, o, s = [], 0, 0, 0
    for cm in comms:
        bounds.append((i, i + len(cm.inputs), o, o + len(cm.out_shapes), s, s + len(cm.sem_counts)))
        i, o, s = bounds[-1][1], bounds[-1][3], bounds[-1][5]

    def phase(which):
        def run(ins, outs, sems):
            for cm, (i0, i1, o0, o1, s0, s1) in zip(comms, bounds):
                getattr(cm, which)(ins[i0:i1], outs[o0:o1], sems[s0:s1])

        return run

    return _Comm(
        [a for cm in comms for a in cm.inputs],
        [a for cm in comms for a in cm.out_shapes],
        [a for cm in comms for a in cm.sem_counts],
        phase("start"),
        phase("middle"),
        phase("finish"),
        middle_at=max(cm.middle_at for cm in comms),
        aliases=[(i0 + i, o0 + o) for cm, (i0, _, o0, _, _, _) in zip(comms, bounds) for i, o in cm.aliases],
    )


def _call(body, args, *, semantics, comm=None, **kw):
    if comm is None:
        return pl.pallas_call(body, compiler_params=_params(*semantics), **kw)(*args)
    grid, in_specs, out_specs, out_shape = kw["grid"], kw["in_specs"], kw["out_specs"], kw["out_shape"]
    scratch = list(kw.get("scratch_shapes", ()))
    single = not isinstance(out_shape, (list, tuple))
    core_specs = [out_specs] if single else list(out_specs)
    core_shapes = [out_shape] if single else list(out_shape)
    n_in, n_out, n_scr = len(in_specs), len(core_shapes), len(scratch)
    n_cin, n_cout = len(comm.inputs), len(comm.out_shapes)
    steps = 1
    for g in grid:
        steps *= g
    middle = min(int(comm.middle_at * steps), steps - 1)

    def hosted(*refs):
        core_in, c_in = refs[:n_in], refs[n_in : n_in + n_cin]
        o0 = n_in + n_cin
        core_out, c_out = refs[o0 : o0 + n_out], refs[o0 + n_out : o0 + n_out + n_cout]
        s0 = o0 + n_out + n_cout
        core_scr, sems = refs[s0 : s0 + n_scr], refs[s0 + n_scr :]
        step = pl.program_id(0)
        for d in range(1, len(grid)):
            step = step * grid[d] + pl.program_id(d)

        @pl.when(step == 0)
        def _():
            comm.start(c_in, c_out, sems)

        body(*core_in, *core_out, *core_scr)

        @pl.when(step == middle)
        def _():
            comm.middle(c_in, c_out, sems)

        @pl.when(step == steps - 1)
        def _():
            comm.finish(c_in, c_out, sems)

    outs = pl.pallas_call(
        hosted,
        name=kw["name"],
        grid=grid,
        in_specs=list(in_specs) + [ANY] * n_cin,
        out_specs=core_specs + [ANY] * n_cout,
        out_shape=core_shapes + comm.out_shapes,
        scratch_shapes=scratch + comm.sem_shapes(),
        input_output_aliases={n_in + i: n_out + o for i, o in comm.aliases},
        compiler_params=_params(*(["arbitrary"] * len(grid))),
    )(*args, *comm.inputs)
    return (outs[0] if single else outs[:n_out]), outs[n_out:]


def _comm_only(comm, *, name):
    n_cin, n_cout = len(comm.inputs), len(comm.out_shapes)

    def body(*refs):
        ins, outs, sems = refs[:n_cin], refs[n_cin : n_cin + n_cout], refs[n_cin + n_cout :]
        comm.start(ins, outs, sems)
        comm.middle(ins, outs, sems)
        comm.finish(ins, outs, sems)

    return pl.pallas_call(
        body,
        name=name,
        in_specs=[ANY] * n_cin,
        out_specs=[ANY] * n_cout,
        out_shape=comm.out_shapes,
        scratch_shapes=comm.sem_shapes(),
    )(*comm.inputs)


def _gather_comm(shards, pass_on_at=1.0):
    n = len(shards)
    per = 7

    def plan(ins, outs, sems):
        send, recv, local = sems
        x, y, c = _position()
        me, sibling = (x, y, c), (x, y, 1 - c)
        chips = _other_chips(x, y)

        def block(t, px, py, pc):
            return outs[t].at[pl.ds(4 * px + 2 * py + pc, 1)]

        def copy(t, k, blk, to, src=None):
            return pltpu.make_async_remote_copy(
                src_ref=block(t, *blk) if src is None else src,
                dst_ref=block(t, *blk),
                send_sem=send.at[t * per + k],
                recv_sem=recv.at[t * per + k],
                device_id=to,
                device_id_type=MESH,
            )

        mine = [pltpu.make_async_copy(ins[t], block(t, *me), local.at[t]) for t in range(n)]
        to_chips = [copy(t, 1 + j, me, (*chip, c), src=ins[t]) for t in range(n) for j, chip in enumerate(chips)]
        to_sibling = [copy(t, 0, me, sibling, src=ins[t]) for t in range(n)]
        from_chips = [copy(t, 1 + j, (*chip, c), me) for t in range(n) for j, chip in enumerate(chips)]
        passed_on = [copy(t, 4 + j, (*chip, c), sibling) for t in range(n) for j, chip in enumerate(chips)]
        from_sibling = [copy(t, 0, sibling, me) for t in range(n)]
        from_sibling += [copy(t, 4 + j, (*chip, 1 - c), me) for t in range(n) for j, chip in enumerate(chips)]
        return mine, to_chips, to_sibling, from_chips, passed_on, from_sibling

    def start(ins, outs, sems):
        mine, to_chips, to_sibling, _, _, _ = plan(ins, outs, sems)
        for cp in mine + to_chips + to_sibling:
            cp.start()

    def middle(ins, outs, sems):
        _, _, _, from_chips, passed_on, _ = plan(ins, outs, sems)
        for arrived, onward in zip(from_chips, passed_on):
            arrived.wait_recv()
            onward.start()

    def finish(ins, outs, sems):
        mine, to_chips, to_sibling, _, passed_on, from_sibling = plan(ins, outs, sems)
        for cp in from_sibling:
            cp.wait_recv()
        for cp in to_chips + to_sibling + passed_on:
            cp.wait_send()
        for cp in mine:
            cp.wait()

    out_shapes = [SDS((N_DEV,) + sh.shape[1:], sh.dtype) for sh in shards]
    return _Comm(shards, out_shapes, [n * per, n * per, n], start, middle, finish, middle_at=pass_on_at)


def _gather_halves(shards=None, arrived=None):
    first_half = arrived is None
    arrays = shards if first_half else arrived
    n = len(arrays)
    per = 4 if first_half else 3

    def plan(ins, outs, sems):
        x, y, c = _position()
        me, sibling = (x, y, c), (x, y, 1 - c)
        chips = _other_chips(x, y)

        def block(t, px, py, pc):
            return outs[t].at[pl.ds(4 * px + 2 * py + pc, 1)]

        def copy(t, k, blk, to, src=None):
            return pltpu.make_async_remote_copy(
                src_ref=block(t, *blk) if src is None else src,
                dst_ref=block(t, *blk),
                send_sem=sems[0].at[t * per + k],
                recv_sem=sems[1].at[t * per + k],
                device_id=to,
                device_id_type=MESH,
            )

        if first_half:
            local = [pltpu.make_async_copy(ins[t], block(t, *me), sems[2].at[t]) for t in range(n)]
            sent = [copy(t, 1 + j, me, (*chip, c), src=ins[t]) for t in range(n) for j, chip in enumerate(chips)]
            sent += [copy(t, 0, me, sibling, src=ins[t]) for t in range(n)]
            landing = [copy(t, 1 + j, (*chip, c), me) for t in range(n) for j, chip in enumerate(chips)]
            landing += [copy(t, 0, sibling, me) for t in range(n)]
        else:
            local = []
            sent = [copy(t, j, (*chip, c), sibling) for t in range(n) for j, chip in enumerate(chips)]
            landing = [copy(t, j, (*chip, 1 - c), me) for t in range(n) for j, chip in enumerate(chips)]
        return local, sent, landing

    def start(ins, outs, sems):
        local, sent, _ = plan(ins, outs, sems)
        for cp in local + sent:
            cp.start()

    def middle(ins, outs, sems):
        pass

    def finish(ins, outs, sems):
        local, sent, landing = plan(ins, outs, sems)
        for cp in landing:
            cp.wait_recv()
        for cp in sent:
            cp.wait_send()
        for cp in local:
            cp.wait()

    if first_half:
        out_shapes = [SDS((N_DEV,) + sh.shape[1:], sh.dtype) for sh in shards]
        return _Comm(shards, out_shapes, [n * per, n * per, n], start, middle, finish)
    out_shapes = [SDS(a.shape, a.dtype) for a in arrived]
    return _Comm(arrived, out_shapes, [n * per, n * per], start, middle, finish, aliases=[(t, t) for t in range(n)])


def _exchange_comm(arrays, out_shapes, n_copies, copies_of):
    def start(ins, outs, sems):
        for cp in copies_of(ins, outs, *sems):
            cp.start()

    def middle(ins, outs, sems):
        pass

    def finish(ins, outs, sems):
        for cp in copies_of(ins, outs, *sems):
            cp.wait()

    return _Comm(arrays, out_shapes, [n_copies, n_copies], start, middle, finish)


def _sibling_comm(grads):
    def copies_of(ins, outs, send, recv):
        x, y, c = _position()
        return [
            pltpu.make_async_remote_copy(
                src_ref=ins[t].at[:, pl.ds(1 - c, 1)],
                dst_ref=outs[t],
                send_sem=send.at[t],
                recv_sem=recv.at[t],
                device_id=(x, y, 1 - c),
                device_id_type=MESH,
            )
            for t in range(len(ins))
        ]

    return _exchange_comm(grads, [SDS((4, 1) + g.shape[2:], g.dtype) for g in grads], len(grads), copies_of)


def _chips_comm(parts):
    def copies_of(ins, outs, send, recv):
        x, y, c = _position()
        return [
            pltpu.make_async_remote_copy(
                src_ref=ins[t].at[pl.ds(2 * px + py, 1)],
                dst_ref=outs[t].at[pl.ds(k, 1)],
                send_sem=send.at[3 * t + k],
                recv_sem=recv.at[3 * t + k],
                device_id=(px, py, c),
                device_id_type=MESH,
            )
            for t in range(len(ins))
            for k, (px, py) in enumerate(_other_chips(x, y))
        ]

    return _exchange_comm(parts, [SDS((3,) + p.shape[1:], p.dtype) for p in parts], 3 * len(parts), copies_of)


def _sum_with_sibling(grad, got, core, *, name):
    rows = grad.shape[2]

    def body(core_ref, a_ref, b_ref, o_ref):
        o_ref[...] = (a_ref[...].astype(F32) + b_ref[...].astype(F32)).astype(o_ref.dtype)

    return pl.pallas_call(
        body,
        name=name,
        grid_spec=pltpu.PrefetchScalarGridSpec(
            num_scalar_prefetch=1,
            grid=(4,),
            in_specs=[
                pl.BlockSpec((None, None, rows, D), lambda q, core_ref: (q, core_ref[0], 0, 0)),
                pl.BlockSpec((None, None, rows, D), lambda q, core_ref: (q, 0, 0, 0)),
            ],
            out_specs=pl.BlockSpec((None, rows, D), lambda q, core_ref: (q, 0, 0)),
        ),
        out_shape=SDS((4, rows, D), grad.dtype),
        compiler_params=_params("parallel"),
    )(core, grad, got)


def _sum_chips(part, got, chip, *, name):
    rows = part.shape[1]

    def body(chip_ref, a_ref, b_ref, o_ref):
        o_ref[...] = ((a_ref[...].astype(F32) + b_ref[0].astype(F32)) + b_ref[1].astype(F32)) + b_ref[2].astype(F32)

    return pl.pallas_call(
        body,
        name=name,
        grid_spec=pltpu.PrefetchScalarGridSpec(
            num_scalar_prefetch=1,
            grid=(1,),
            in_specs=[
                pl.BlockSpec((None, rows, D), lambda i, chip_ref: (chip_ref[0], 0, 0)),
                pl.BlockSpec((3, rows, D), lambda i, chip_ref: (0, 0, 0)),
            ],
            out_specs=pl.BlockSpec((rows, D), lambda i, chip_ref: (0, 0)),
        ),
        out_shape=SDS((rows, D), F32),
        compiler_params=_params("arbitrary"),
    )(chip, part, got)


def _all_reduce_small(pack, *, name):
    rows = pack.shape[1]

    def body(in_ref, out_ref, from_sibling, part, from_chips, send, recv):
        x, y, c = _position()
        me, sibling = (x, y, c), (x, y, 1 - c)
        chips = _other_chips(x, y)
        waiting = []

        def copy(k, src, dst, to):
            return pltpu.make_async_remote_copy(
                src_ref=src, dst_ref=dst, send_sem=send.at[k], recv_sem=recv.at[k], device_id=to, device_id_type=MESH
            )

        def exchange(copies):
            for cp in copies:
                cp.start()
            for cp in copies:
                cp.wait_recv()
            waiting.extend(copies)

        def block(px, py, pc):
            return out_ref.at[4 * px + 2 * py + pc]

        exchange([copy(q, in_ref.at[2 * q + 1 - c], from_sibling.at[q], sibling) for q in range(4)])
        for q in range(4):
            part[q] = in_ref[2 * q + c] + from_sibling[q]
        exchange([copy(4 + k, part.at[2 * px + py], from_chips.at[k], (px, py, c)) for k, (px, py) in enumerate(chips)])
        out_ref[4 * x + 2 * y + c] = ((part[2 * x + y] + from_chips[0]) + from_chips[1]) + from_chips[2]
        exchange(
            [copy(7, block(*me), block(*me), sibling)]
            + [copy(8 + k, block(*me), block(*me), (px, py, c)) for k, (px, py) in enumerate(chips)]
        )
        exchange([copy(11 + k, block(px, py, c), block(px, py, c), sibling) for k, (px, py) in enumerate(chips)])
        for cp in waiting:
            cp.wait_send()

    vmem = pl.BlockSpec(memory_space=pltpu.VMEM)
    return pl.pallas_call(
        body,
        name=name,
        in_specs=[vmem],
        out_specs=vmem,
        out_shape=SDS(pack.shape, F32),
        scratch_shapes=[
            pltpu.VMEM((4, rows, D), F32),
            pltpu.VMEM((4, rows, D), F32),
            pltpu.VMEM((3, rows, D), F32),
            pltpu.SemaphoreType.DMA((14,)),
            pltpu.SemaphoreType.DMA((14,)),
        ],
        compiler_params=pltpu.CompilerParams(vmem_limit_bytes=VMEM_LIMIT_BYTES),
    )(pack)


def _pack(arrays, rows):
    flat = jnp.concatenate([a.reshape(-1).astype(F32) for a in arrays])
    return jnp.pad(flat, (0, rows * D - flat.shape[0])).reshape(rows, D)


def _unpack(pack, shapes):
    flat = pack.reshape(-1)
    out, off = [], 0
    for sh in shapes:
        size = 1
        for dim in sh:
            size *= dim
        out.append(flat[off : off + size].reshape(sh))
        off += size
    return out


def _block_diag_pairs(w):
    w = w.reshape(N_RNN_TILES, 2, HEAD_DIM, HEAD_DIM)
    z = jnp.zeros_like(w[:, 0])
    top = jnp.concatenate([w[:, 0], z], axis=2)
    bot = jnp.concatenate([z, w[:, 1]], axis=2)
    return jnp.concatenate([top, bot], axis=1)


def _diag_blocks(w2):
    a = w2[:, :HEAD_DIM, :HEAD_DIM]
    b = w2[:, HEAD_DIM:, HEAD_DIM:]
    return jnp.stack([a, b], axis=1).reshape(RNN_HEADS, HEAD_DIM, HEAD_DIM)


BIG = ("w_in", "w_branch_a", "w_branch_b", "w_out", "w_up", "w_down")
TRANSPOSED = ("w_in", "w_up")
SMALL = (
    "norm_mix_g", "conv_w", "conv_b", "lru_w_a", "lru_b_a", "lru_w_x", "lru_b_x", "lru_lambda",
    "sgu_ln_g", "sgu_ln_b", "sgu_w_s", "sgu_b_s", "norm_ffn_g", "final_norm_g",
)
WEIGHTS = (
    "norm_mix_g", "w_in", "conv_w", "conv_b", "lru_w_a", "lru_b_a", "lru_w_x", "lru_b_x", "lru_lambda", "sgu_ln_g",
    "sgu_ln_b", "sgu_w_s", "sgu_b_s", "w_branch_a", "w_branch_b", "w_out", "norm_ffn_g", "w_up", "w_down", "final_norm_g",
)

TM = 512
TM_NT = 1024
TN_IN = 1664
TN_UP = 2048
TKA = 512
TKA_PIECES = 256
TC = 512
TC_BWD = 1024
TB = 256
TB_BWD = 512
TR = 256


_BRANCH_WEIGHTS = ("w_branch_a", "w_branch_b", "w_out")
GATHERS_RIDING = (
    {
        "in_proj": ([(0, name) for name in _BRANCH_WEIGHTS] + [(0, "w_up")], []),
        "branch_a_fwd": ([(1, "w_in")], [(0, name) for name in _BRANCH_WEIGHTS] + [(0, "w_up")]),
        "sgu_fwd": ([], [(1, "w_in")]),
        "merge_fwd": ([(0, "w_down")], []),
        "ffn_up": ([(1, name) for name in _BRANCH_WEIGHTS], [(0, "w_down")]),
        "ffn_down": ([], [(1, name) for name in _BRANCH_WEIGHTS]),
    },
    {"in_proj": ([(1, "w_down")], []), "branch_a_fwd": ([(1, "w_up")], [(1, "w_down")]), "sgu_fwd": ([], [(1, "w_up")])},
)


def _layer_forward(l, x, p, w, shards, arriving, loss_head=None):
    def run(key, fn, *args, **kw):
        first, second = GATHERS_RIDING[l].get(key, ((), ()))
        comms = []
        if first:
            comms.append(_gather_halves(shards=[shards[l2][n2] for l2, n2 in first]))
        if second:
            comms.append(_gather_halves(arrived=[arriving.pop(k) for k in second]))
        if not comms:
            return fn(*args, **kw)
        out, got = fn(*args, comm=_merge_comms(comms), **kw)
        arriving.update(zip(first, got[: len(first)]))
        for (l2, n2), full in zip(second, got[len(first) :]):
            w[l2][n2] = full.reshape(-1, D)
        return out

    proj, h = run("in_proj", _norm_matmul_nt, x, p["norm_mix_g"], w[l]["w_in"], tm=TM_NT, tn=TN_IN, name=f"in_proj_{l}")
    hseq, ya_pre = run(
        "branch_a_fwd", _branch_a_fwd, proj, p["conv_w"], p["conv_b"], p["wa2"], p["lru_b_a"], p["wx2"], p["lru_b_x"],
        p["lru_lambda"], tc=TC, name=f"branch_a_fwd_{l}",
    )
    yb_pre = run("sgu_fwd", _sgu_fwd, proj, p["sgu_ln_g"], p["sgu_ln_b"], p["wm"], p["sgu_bias"], tb=TB, name=f"sgu_fwd_{l}")
    x1, ya, yb = run(
        "merge_fwd", _merge_fwd, ya_pre, yb_pre, proj, x, w[l]["w_branch_a"], w[l]["w_branch_b"], w[l]["w_out"], tm=TM,
        name=f"merge_fwd_{l}",
    )
    f_pre, h2 = run("ffn_up", _norm_matmul_nt, x1, p["norm_ffn_g"], w[l]["w_up"], tm=TM_NT, tn=TN_UP, name=f"ffn_up_{l}")
    saved = dict(x=x, h=h, proj=proj, hseq=hseq, ya_pre=ya_pre, yb_pre=yb_pre, ya=ya, yb=yb, x1=x1, h2=h2, f_pre=f_pre)
    if loss_head is None:
        return run("ffn_down", _matmul_nn_res, f_pre, w[l]["w_down"], x1, relu2=True, tm=TM, name=f"ffn_down_{l}"), saved
    return _ffn_down_loss(f_pre, w[l]["w_down"], x1, *loss_head, tm=TM, name=f"ffn_down_loss_{l}"), saved


def _layer_backward(l, dx2, dx2b, sv, p, w, core, waiting, last):
    parts, from_chips = {}, {}

    def by_device(g):
        return g.reshape(4, 2, -1, D)

    def with_sibling(name, g, got):
        parts[name] = _sum_with_sibling(by_device(g), got, core, name=f"sum_sibling_{name}_{l}")

    df_pre = _matmul_nt_drelu2(dx2b, w["w_down"], sv["f_pre"], tm=TM_NT, tn=TN_UP, name=f"ffn_down_bwd_{l}")
    g_down = _matmul_tn([sv["f_pre"]], dx2b, relu2=True, tka=TKA, name=f"grad_w_down_{l}")
    g_up, (got,) = _matmul_tn(
        [df_pre], sv["h2"], relu2=False, tka=TKA, name=f"grad_w_up_{l}", comm=_sibling_comm([by_device(g_down)])
    )
    with_sibling("w_down", g_down, got)
    (dx1, dx1b, g_norm_ffn), (got,) = _matmul_nn_rmsnorm_bwd(
        [df_pre], w["w_up"], sv["x1"], p["norm_ffn_g"], dx2, tm=TM, name=f"ffn_up_bwd_{l}",
        comm=_sibling_comm([by_device(g_up)]),
    )
    with_sibling("w_up", g_up, got)
    (merged, dya, dyb, dga, dgb, dya_pre, dyb_pre), (from_chips[l, "w_up"],) = _merge_bwd(
        dx1b, sv["ya"], sv["yb"], sv["proj"], w["w_branch_a"], w["w_branch_b"], w["w_out"], tm=TM, name=f"merge_bwd_{l}",
        comm=_chips_comm([parts["w_up"]]),
    )
    g_out = _matmul_tn([merged], dx1b, relu2=False, tka=TKA, name=f"grad_w_out_{l}")
    g_ba = _matmul_tn([sv["ya_pre"]], dya, relu2=False, tka=TKA_PIECES, name=f"grad_w_branch_a_{l}")
    g_bb = _matmul_tn([sv["yb_pre"]], dyb, relu2=False, tka=TKA, name=f"grad_w_branch_b_{l}")
    branch = (("w_out", g_out), ("w_branch_a", g_ba), ("w_branch_b", g_bb))
    (du, dv, g_ws, g_bs, g_lng, g_lnb), got = _sgu_bwd(
        dyb_pre, sv["proj"], p["sgu_ln_g"], p["sgu_ln_b"], p["wm"], p["wmt"], p["sgu_bias"], p["mask"], tb=TB_BWD,
        name=f"sgu_bwd_{l}",
        comm=_merge_comms([_sibling_comm([by_device(g) for _, g in branch]), _chips_comm([parts["w_down"]])]),
    )
    from_chips[l, "w_down"] = got[-1]
    for (name, g), landed in zip(branch, got):
        with_sibling(name, g, landed)
    riding = [((l, name), parts[name]) for name, _ in branch] + list(waiting)
    (dxr, dgr, g_cw, g_cb, g_ba_, g_bx, g_lam, g_wa2, g_wx2), got = _branch_a_bwd(
        dya_pre, sv["proj"], sv["hseq"], p["conv_w"], p["conv_b"], p["wa2"], p["lru_b_a"], p["wx2"], p["lru_b_x"],
        p["lru_lambda"], p["wa2t"], p["wx2t"], tc=TC_BWD, name=f"branch_a_bwd_{l}", comm=_chips_comm([part for _, part in riding]),
    )
    for (key, _), landed in zip(riding, got):
        from_chips[key] = landed
    dproj = [dxr, dgr, du, dv, dga, dgb]
    g_in = _matmul_tn(dproj, sv["h"], relu2=False, tka=TKA_PIECES, name=f"grad_w_in_{l}")
    if last:
        (got,) = _comm_only(_sibling_comm([by_device(g_in)]), name=f"grad_w_in_to_sibling_{l}")
        with_sibling("w_in", g_in, got)
        riding = _chips_comm([parts["w_in"]])
    else:
        riding = _sibling_comm([by_device(g_in)])
    (dx, dxb, g_norm_mix), (got,) = _matmul_nn_rmsnorm_bwd(
        dproj, w["w_in"], sv["x"], p["norm_mix_g"], dx1, tm=TM, name=f"in_proj_bwd_{l}", comm=riding
    )
    if last:
        from_chips[l, "w_in"] = got
    else:
        with_sibling("w_in", g_in, got)
    small = dict(
        norm_mix_g=g_norm_mix[0], conv_w=g_cw, conv_b=g_cb[0], lru_w_a=_diag_blocks(g_wa2), lru_b_a=g_ba_.reshape(RNN_HEADS, HEAD_DIM),
        lru_w_x=_diag_blocks(g_wx2), lru_b_x=g_bx.reshape(RNN_HEADS, HEAD_DIM), lru_lambda=g_lam[0], sgu_ln_g=g_lng[0],
        sgu_ln_b=g_lnb[0], sgu_w_s=g_ws, sgu_b_s=g_bs[:, :, 0], norm_ffn_g=g_norm_ffn[0],
    )
    return dx, dxb, small, parts, from_chips


def _prepare_small(l, given):
    chunk_id = jnp.arange(SGU_BLOCK) // CHUNK
    mask = (chunk_id[:, None] >= chunk_id[None, :]).astype(F32)
    wm = given["sgu_w_s"][l] * mask
    wa2 = _block_diag_pairs(given["lru_w_a"][l])
    wx2 = _block_diag_pairs(given["lru_w_x"][l])
    row = lambda a: a.reshape(1, -1)
    return dict(
        norm_mix_g=row(given["norm_mix_g"][l]),
        norm_ffn_g=row(given["norm_ffn_g"][l]),
        conv_w=given["conv_w_full"][l],
        conv_b=row(given["conv_b"][l]),
        wa2=wa2.astype(BF16),
        wx2=wx2.astype(BF16),
        wa2t=jnp.swapaxes(wa2, 1, 2).astype(BF16),
        wx2t=jnp.swapaxes(wx2, 1, 2).astype(BF16),
        lru_b_a=row(given["lru_b_a"][l]),
        lru_b_x=row(given["lru_b_x"][l]),
        lru_lambda=row(given["lru_lambda"][l]),
        sgu_ln_g=row(given["sgu_ln_g"][l]),
        sgu_ln_b=row(given["sgu_ln_b"][l]),
        wm=wm.astype(BF16),
        wmt=jnp.swapaxes(wm, 1, 2).astype(BF16),
        sgu_bias=jnp.broadcast_to(given["sgu_b_s"][l][:, :, None], (SGU_GROUPS, SGU_BLOCK, LANES)),
        mask=mask,
    )


def _step(given):
    x_idx, y_idx, c_idx = _position()
    dev = 4 * x_idx + 2 * y_idx + c_idx
    core = c_idx.astype(jnp.int32).reshape(1)
    chip = (2 * x_idx + y_idx).astype(jnp.int32).reshape(1)

    def rows_first(name, a):
        return jnp.swapaxes(a, 1, 2) if name in TRANSPOSED else a

    shards = []
    for l in range(DEPTH):
        shards.append({name: rows_first(name, given[name])[l].astype(BF16)[None] for name in BIG})
    conv_mine = given["conv_w"].reshape(1, DEPTH * CONV_WIDTH, D_RNN // N_DEV)
    w_in_first, conv_all = _comm_only(_gather_comm([shards[0]["w_in"], conv_mine]), name="gather_first")
    weights = [{"w_in": w_in_first.reshape(-1, D)}, {}]
    conv_all = conv_all.reshape(N_DEV, DEPTH, CONV_WIDTH, D_RNN // N_DEV)
    given = dict(given, conv_w_full=jnp.moveaxis(conv_all, 0, 2).reshape(DEPTH, CONV_WIDTH, D_RNN))

    small_params = [_prepare_small(l, given) for l in range(DEPTH)]
    x = given["x"][0]
    saved, arriving = [], {}
    loss_head = (given["final_norm_g"].reshape(1, D), given["loss_target"][0])
    for l in range(DEPTH):
        x, sv = _layer_forward(
            l, x, small_params[l], weights, shards, arriving, loss_head=loss_head if l == DEPTH - 1 else None
        )
        saved.append(sv)
    dx, dxb, g_final, loss = x
    small_grads, parts, from_chips, waiting = [None] * DEPTH, [None] * DEPTH, {}, []
    for l in reversed(range(DEPTH)):
        dx, dxb, small_grads[l], parts[l], got = _layer_backward(
            l, dx, dxb, saved[l], small_params[l], weights[l], core, waiting, last=l == 0
        )
        from_chips.update(got)
        waiting = [((l, "w_in"), parts[l]["w_in"])]

    small_list = []
    for name in SMALL[:-1]:
        small_list.append(jnp.stack([small_grads[l][name] for l in range(DEPTH)]))
    small_list += [g_final[0], loss[0, :1]]
    small_shapes = [a.shape for a in small_list]
    pack = _pack(small_list, SMALL_ROWS).reshape(N_DEV, SMALL_ROWS_PER_DEV, D)
    summed = _unpack(_all_reduce_small(pack, name="all_reduce_small"), small_shapes)
    loss_total = summed[-1][0]
    grads = dict(zip(SMALL, summed[:-1]))
    cw = grads["conv_w"].reshape(DEPTH, CONV_WIDTH, N_DEV, D_RNN // N_DEV)
    grads["conv_w"] = lax.dynamic_index_in_dim(cw, dev, axis=2, keepdims=False)

    delta, new_m, new_v = {}, {}, {}
    for name in BIG:
        w, m, v = given[name], given["m_" + name], given["v_" + name]
        mine = [parts[l][name] for l in range(DEPTH)]
        theirs = [from_chips[l, name] for l in range(DEPTH)]
        if name == "w_up":
            sums = [_sum_chips(mine[l], theirs[l], chip, name=f"sum_chips_{name}_{l}").T for l in range(DEPTH)]
            out = _adamw_layers(w, sums, m, v, tr=TR, name=f"adamw_{name}")
        else:
            out = _adamw_reduced(
                rows_first(name, w), mine, theirs, rows_first(name, m), rows_first(name, v), chip, tr=TR, name=f"adamw_{name}"
            )
            out = [rows_first(name, a) for a in out]
        grads[name], delta[name], new_m[name], new_v[name] = out
    two_d = lambda a: a.reshape(1, -1) if a.ndim == 1 else a
    groups = [tuple(two_d(a) for a in (given[n], grads[n], given["m_" + n], given["v_" + n])) for n in SMALL]
    for n, (d, m2, v2) in zip(SMALL, _adamw_small(groups, name="adamw_small")):
        shape = given[n].shape
        delta[n], new_m[n], new_v[n] = d.reshape(shape), m2.reshape(shape), v2.reshape(shape)

    return (
        loss_total, dx[None],
        *[grads[n] for n in WEIGHTS], *[delta[n] for n in WEIGHTS], *[new_m[n] for n in WEIGHTS], *[new_v[n] for n in WEIGHTS],
    )


def kernel(x, norm_mix_g, w_in, conv_w, conv_b, lru_w_a, lru_b_a, lru_w_x, lru_b_x, lru_lambda, sgu_ln_g, sgu_ln_b, sgu_w_s, sgu_b_s, w_branch_a, w_branch_b, w_out, norm_ffn_g, w_up, w_down, final_norm_g, loss_target, m_norm_mix_g, m_w_in, m_conv_w, m_conv_b, m_lru_w_a, m_lru_b_a, m_lru_w_x, m_lru_b_x, m_lru_lambda, m_sgu_ln_g, m_sgu_ln_b, m_sgu_w_s, m_sgu_b_s, m_w_branch_a, m_w_branch_b, m_w_out, m_norm_ffn_g, m_w_up, m_w_down, m_final_norm_g, v_norm_mix_g, v_w_in, v_conv_w, v_conv_b, v_lru_w_a, v_lru_b_a, v_lru_w_x, v_lru_b_x, v_lru_lambda, v_sgu_ln_g, v_sgu_ln_b, v_sgu_w_s, v_sgu_b_s, v_w_branch_a, v_w_branch_b, v_w_out, v_norm_ffn_g, v_w_up, v_w_down, v_final_norm_g):
    return _step(dict(locals()))
```

```python
import jax
import jax.numpy as jnp
from jax import lax
from jax.experimental import pallas as pl
from jax.experimental.pallas import tpu as pltpu

F32 = jnp.float32
BF16 = jnp.bfloat16
SDS = jax.ShapeDtypeStruct
MESH = pl.DeviceIdType.MESH

D = 1024
D_RNN = 1280
D_SGU = 1024
D_FF = 4096
D_IN = 2 * D_RNN + 2 * D_SGU + 2 * D
DEPTH = 2
RNN_HEADS = 20
HEAD_DIM = 64
CONV_WIDTH = 4
LRU_C = 8.0
SGU_GROUPS = 8
SGU_BLOCK = 128
CHUNK = 64
EPS = 1e-6
N_DEV = 8

ADAM_LR = 0.001
ADAM_B1 = 0.9
ADAM_B2 = 0.999
ADAM_EPS = 1e-08
ADAM_WD = 0.01
ADAM_STEP = 10

LANES = 128
SUBLANES = 8
VMEM_LIMIT_BYTES = 56 * 1024 * 1024

N_RNN_TILES = D_RNN // LANES
RNN_TILES_PER_STEP = 5
U_BLK512 = (2 * D_RNN) // 512
V_BLK512 = (2 * D_RNN + D_SGU) // 512
GA_BLK512 = (2 * D_RNN + 2 * D_SGU) // 512
GB_BLK512 = (2 * D_RNN + 2 * D_SGU + D) // 512

SMALL_ROWS_PER_DEV = 80
SMALL_ROWS = N_DEV * SMALL_ROWS_PER_DEV


def _params(*sem):
    return pltpu.CompilerParams(dimension_semantics=sem, vmem_limit_bytes=VMEM_LIMIT_BYTES)


def _sigmoid(x):
    return 0.5 + 0.5 * jnp.tanh(0.5 * x)


_GELU_C = 0.7978845608028654
_GELU_K = 0.044715


def _gelu(x):
    t = jnp.tanh(_GELU_C * (x + _GELU_K * x * x * x))
    return 0.5 * x * (1.0 + t)


def _gelu_and_grad(x):
    t = jnp.tanh(_GELU_C * (x + _GELU_K * x * x * x))
    val = 0.5 * x * (1.0 + t)
    grad = 0.5 * (1.0 + t) + 0.5 * x * (1.0 - t * t) * _GELU_C * (1.0 + 3.0 * _GELU_K * x * x)
    return val, grad


def _one_minus_square(log_a, a):
    return -jnp.tanh(log_a) * (1.0 + a * a)


def _dot(a, b):
    return jnp.dot(a, b, preferred_element_type=F32)


def _dot_nt(a, b):
    return lax.dot_general(a, b, (((1,), (1,)), ((), ())), preferred_element_type=F32)


def _dot_tn(a, b):
    return lax.dot_general(a, b, (((0,), (0,)), ((), ())), preferred_element_type=F32)


def _norm_matmul_nt(x, g, w, *, tm, tn, name, comm=None):
    s, n = x.shape[0], w.shape[0]
    tm, tn = min(tm, s), min(tn, n)

    def body(x_ref, g_ref, w_ref, o_ref, h_ref):
        @pl.when(pl.program_id(1) == 0)
        def _():
            xv = x_ref[...]
            r = lax.rsqrt(jnp.mean(xv * xv, axis=-1, keepdims=True) + EPS)
            h_ref[...] = (xv * r * g_ref[...]).astype(BF16)

        o_ref[...] = _dot_nt(h_ref[...], w_ref[...]).astype(o_ref.dtype)

    return _call(
        body,
        (x, g, w),
        name=name,
        grid=(s // tm, n // tn),
        in_specs=[
            pl.BlockSpec((tm, D), lambda i, j: (i, 0)),
            pl.BlockSpec((1, D), lambda i, j: (0, 0)),
            pl.BlockSpec((tn, D), lambda i, j: (j, 0)),
        ],
        out_specs=[pl.BlockSpec((tm, tn), lambda i, j: (i, j)), pl.BlockSpec((tm, D), lambda i, j: (i, 0))],
        out_shape=[SDS((s, n), BF16), SDS((s, D), BF16)],
        semantics=("parallel", "arbitrary"),
        comm=comm,
    )


def _matmul_nn_res(a, w, res, *, relu2, tm, name, comm=None):
    s, k = a.shape
    tm = min(tm, s)

    def body(a_ref, w_ref, r_ref, o_ref):
        av = a_ref[...]
        if relu2:
            t = jnp.maximum(av.astype(F32), 0.0)
            av = (t * t).astype(BF16)
        o_ref[...] = r_ref[...] + _dot(av, w_ref[...])

    return _call(
        body,
        (a, w, res),
        name=name,
        grid=(s // tm,),
        in_specs=[
            pl.BlockSpec((tm, k), lambda i: (i, 0)),
            pl.BlockSpec((k, D), lambda i: (0, 0)),
            pl.BlockSpec((tm, D), lambda i: (i, 0)),
        ],
        out_specs=pl.BlockSpec((tm, D), lambda i: (i, 0)),
        out_shape=SDS((s, D), F32),
        semantics=("parallel",),
        comm=comm,
    )


def _matmul_nt_drelu2(a, w, pre, *, tm, tn, name):
    s, n = a.shape[0], w.shape[0]
    tm, tn = min(tm, s), min(tn, n)

    def body(a_ref, w_ref, p_ref, o_ref):
        d = _dot_nt(a_ref[...], w_ref[...])
        o_ref[...] = (d * (2.0 * jnp.maximum(p_ref[...].astype(F32), 0.0))).astype(o_ref.dtype)

    return pl.pallas_call(
        body,
        name=name,
        grid=(s // tm, n // tn),
        in_specs=[
            pl.BlockSpec((tm, D), lambda i, j: (i, 0)),
            pl.BlockSpec((tn, D), lambda i, j: (j, 0)),
            pl.BlockSpec((tm, tn), lambda i, j: (i, j)),
        ],
        out_specs=pl.BlockSpec((tm, tn), lambda i, j: (i, j)),
        out_shape=SDS((s, n), BF16),
        compiler_params=_params("parallel", "arbitrary"),
    )(a, w, pre)


def _matmul_tn(a_list, b, *, relu2, tka, name, comm=None):
    s = b.shape[0]
    n = len(a_list)
    nblk = [a.shape[1] // tka for a in a_list]
    starts = [sum(nblk[:p]) for p in range(n)]

    def body(*refs):
        a_refs, b_ref, o_ref = refs[:n], refs[n], refs[n + 1]
        i = pl.program_id(0)
        for p in range(n):

            @pl.when((i >= starts[p]) & (i < starts[p] + nblk[p]))
            def _(p=p):
                av = a_refs[p][...]
                if relu2:
                    t = jnp.maximum(av.astype(F32), 0.0)
                    av = (t * t).astype(BF16)
                o_ref[...] = _dot_tn(av, b_ref[...]).astype(o_ref.dtype)

    def piece_spec(p):
        return pl.BlockSpec((s, tka), lambda i: (0, jnp.clip(i - starts[p], 0, nblk[p] - 1)))

    return _call(
        body,
        (*a_list, b),
        name=name,
        grid=(sum(nblk),),
        in_specs=[piece_spec(p) for p in range(n)] + [pl.BlockSpec((s, D), lambda i: (0, 0))],
        out_specs=pl.BlockSpec((tka, D), lambda i: (i, 0)),
        out_shape=SDS((sum(nblk) * tka, D), BF16),
        semantics=("parallel",),
        comm=comm,
    )


def _matmuls_tn(pairs, *, ts, name):
    s = pairs[0][0].shape[0]
    ts = min(ts, s)
    n = len(pairs)
    steps = s // ts

    def body(*refs):
        ins, outs, accs = refs[: 2 * n], refs[2 * n : 3 * n], refs[3 * n :]
        for p in range(n):
            part = _dot_tn(ins[2 * p][...], ins[2 * p + 1][...])

            @pl.when(pl.program_id(0) == 0)
            def _(p=p, part=part):
                accs[p][...] = part

            @pl.when(pl.program_id(0) > 0)
            def _(p=p, part=part):
                accs[p][...] += part

        @pl.when(pl.program_id(0) == steps - 1)
        def _():
            for p in range(n):
                outs[p][...] = accs[p][...].astype(BF16)

    widths = [a.shape[1] for a, _ in pairs]
    in_specs = []
    for wd in widths:
        in_specs += [pl.BlockSpec((ts, wd), lambda i: (i, 0)), pl.BlockSpec((ts, D), lambda i: (i, 0))]
    return pl.pallas_call(
        body,
        name=name,
        grid=(steps,),
        in_specs=in_specs,
        out_specs=[pl.BlockSpec((wd, D), lambda i: (0, 0)) for wd in widths],
        out_shape=[SDS((wd, D), BF16) for wd in widths],
        scratch_shapes=[pltpu.VMEM((wd, D), F32) for wd in widths],
        compiler_params=_params("arbitrary"),
    )(*[x for pair in pairs for x in pair])


def _matmul_nn_rmsnorm_bwd(a_list, w, x, g, res, *, tm, name, comm=None):
    s = x.shape[0]
    tm = min(tm, s)
    n = len(a_list)
    widths = [a.shape[1] for a in a_list]
    offs = [sum(widths[:p]) for p in range(n)]
    k = sum(widths)

    def body(*refs):
        a_refs = refs[:n]
        w_ref, x_ref, g_ref, r_ref, dx_ref, dxb_ref, dg_ref = refs[n:]

        @pl.when(pl.program_id(0) == 0)
        def _():
            dg_ref[...] = jnp.zeros_like(dg_ref)

        dh = _dot(a_refs[0][...], w_ref[0 : widths[0], :])
        for p in range(1, n):
            dh += _dot(a_refs[p][...], w_ref[offs[p] : offs[p] + widths[p], :])
        xv = x_ref[...]
        r = lax.rsqrt(jnp.mean(xv * xv, axis=-1, keepdims=True) + EPS)
        xhat = xv * r
        dxh = dh * g_ref[...]
        dx = r_ref[...] + r * (dxh - xhat * jnp.mean(dxh * xhat, axis=-1, keepdims=True))
        dx_ref[...] = dx
        dxb_ref[...] = dx.astype(BF16)
        dg_ref[...] += jnp.sum(dh * xhat, axis=0, keepdims=True)

    act = pl.BlockSpec((tm, D), lambda i: (i, 0))
    vec = pl.BlockSpec((1, D), lambda i: (0, 0))
    return _call(
        body,
        (*a_list, w, x, g, res),
        name=name,
        grid=(s // tm,),
        in_specs=[pl.BlockSpec((tm, wd), lambda i: (i, 0)) for wd in widths]
        + [pl.BlockSpec((k, D), lambda i: (0, 0), pipeline_mode=pl.Buffered(1)), act, vec, act],
        out_specs=[act, act, vec],
        out_shape=[SDS((s, D), F32), SDS((s, D), BF16), SDS((1, D), F32)],
        semantics=("arbitrary",),
        comm=comm,
    )


def _rows_after(ext, k, n):
    return pltpu.roll(ext, n + SUBLANES - k, 0)[:n, :]


def _scan_forward(a, b, n):
    row = lax.broadcasted_iota(jnp.int32, a.shape, 0)
    d = 1
    while d < n:
        if d < SUBLANES:
            m = row >= d
            a_s = jnp.where(m, pltpu.roll(a, d, 0), 1.0)
            b_s = jnp.where(m, pltpu.roll(b, d, 0), 0.0)
            b = a * b_s + b
            a = a * a_s
        else:
            b = jnp.concatenate([b[:d], a[d:] * b[: n - d] + b[d:]], axis=0)
            a = jnp.concatenate([a[:d], a[d:] * a[: n - d]], axis=0)
        d *= 2
    return a, b


def _scan_backward(a, b, n):
    row = lax.broadcasted_iota(jnp.int32, a.shape, 0)
    d = 1
    while d < n:
        if d < SUBLANES:
            m = row < n - d
            a_s = jnp.where(m, pltpu.roll(a, n - d, 0), 1.0)
            b_s = jnp.where(m, pltpu.roll(b, n - d, 0), 0.0)
            b = a * b_s + b
            a = a * a_s
        else:
            b = jnp.concatenate([a[: n - d] * b[d:] + b[: n - d], b[n - d :]], axis=0)
            a = jnp.concatenate([a[: n - d] * a[d:], a[n - d :]], axis=0)
        d *= 2
    return b


def _repeat_matrix(n):
    groups = n // SUBLANES
    return (jnp.arange(n)[:, None] // SUBLANES == jnp.arange(3 * groups)[None, :] % groups).astype(BF16)


def _scan_rows(a, b, n, repeat_ref, a_scr, b_scr, reverse):
    groups = n // SUBLANES
    a3 = a.reshape(groups, SUBLANES, LANES)
    b3 = b.reshape(groups, SUBLANES, LANES)
    sub = lax.broadcasted_iota(jnp.int32, a3.shape, 1)
    for d in (1, 2, 4):
        m = (sub < SUBLANES - d) if reverse else (sub >= d)
        shift = SUBLANES - d if reverse else d
        a_s = jnp.where(m, pltpu.roll(a3, shift, 1), 1.0)
        b_s = jnp.where(m, pltpu.roll(b3, shift, 1), 0.0)
        b3 = a3 * b_s + b3
        a3 = a3 * a_s
    a_scr[...] = a3.reshape(n, LANES)
    b_scr[...] = b3.reshape(n, LANES)
    edge = 0 if reverse else SUBLANES - 1
    a_tot = a_scr[pl.ds(edge, groups, stride=SUBLANES), :]
    b_tot = b_scr[pl.ds(edge, groups, stride=SUBLANES), :]
    row = lax.broadcasted_iota(jnp.int32, a_tot.shape, 0)
    if reverse:
        through = _scan_backward(a_tot, b_tot, groups)
        entering = jnp.where(row < groups - 1, pltpu.roll(through, groups - 1, 0), 0.0)
    else:
        _, through = _scan_forward(a_tot, b_tot, groups)
        entering = jnp.where(row >= 1, pltpu.roll(through, 1, 0), 0.0)
    hi = entering.astype(BF16)
    rest = entering - hi.astype(F32)
    mid = rest.astype(BF16)
    lo = (rest - mid.astype(F32)).astype(BF16)
    repeated = _dot(repeat_ref[...], jnp.concatenate([hi, mid, lo], axis=0))
    return b_scr[...] + a_scr[...] * repeated


def _softplus_neg(lam):
    z = -lam
    return jnp.maximum(z, 0.0) + jnp.log1p(jnp.exp(-jnp.abs(z)))


def _conv_and_gates(xc, xprev, cw_ref, cb_ref, wa_ref, ba_ref, wx_ref, bx_ref, lam_ref, ext_scr):
    n = xc.shape[0]
    ext_scr[:SUBLANES, :] = xprev
    ext_scr[SUBLANES:, :] = xc
    x1, x2, x3 = (ext_scr[pl.ds(SUBLANES - k, n), :] for k in (1, 2, 3))
    xr = cb_ref[...] + x3 * cw_ref[0:1, :] + x2 * cw_ref[1:2, :] + x1 * cw_ref[2:3, :] + xc * cw_ref[3:4, :]
    xrb = xr.astype(BF16)
    r = _sigmoid(_dot(xrb, wa_ref[...]) + ba_ref[...])
    i = _sigmoid(_dot(xrb, wx_ref[...]) + bx_ref[...])
    sp = _softplus_neg(lam_ref[...])
    log_a = (-LRU_C * r) * sp
    a = jnp.exp(log_a)
    return xr, (x1, x2, x3), r, i, a, _one_minus_square(log_a, a)


def _branch_a_fwd(proj, cw, cb, wa2, ba, wx2, bx, lam, *, tc, name, comm=None):
    s = proj.shape[0]
    tc = min(tc, s)

    def body(x_ref, g_ref, cw_ref, cb_ref, wa_ref, ba_ref, wx_ref, bx_ref, lam_ref, rep_ref, h_ref, y_ref,
             xprev, hlast, a_scr, b_scr, ext_scr):
        @pl.when(pl.program_id(1) == 0)
        def _():
            xprev[...] = jnp.zeros_like(xprev)
            hlast[...] = jnp.zeros_like(hlast)

        for t in range(RNN_TILES_PER_STEP):
            cols = lambda ref: ref.at[:, pl.ds(t * LANES, LANES)]
            one_tile(
                cols(x_ref), cols(g_ref), cols(cw_ref), cols(cb_ref), wa_ref.at[t], cols(ba_ref), wx_ref.at[t], cols(bx_ref),
                cols(lam_ref), rep_ref, cols(h_ref), cols(y_ref), cols(xprev), cols(hlast), a_scr.at[t], b_scr.at[t],
                ext_scr.at[t],
            )

    def one_tile(x_ref, g_ref, cw_ref, cb_ref, wa_ref, ba_ref, wx_ref, bx_ref, lam_ref, rep_ref, h_ref, y_ref,
                 xprev, hlast, a_scr, b_scr, ext_scr):
        xc = x_ref[...].astype(F32)
        xr, _, r, i, a, om = _conv_and_gates(
            xc, xprev[...], cw_ref, cb_ref, wa_ref, ba_ref, wx_ref, bx_ref, lam_ref, ext_scr
        )
        xprev[...] = xc[tc - SUBLANES :, :]
        u = jnp.sqrt(om) * (i * xr)
        row8 = lax.broadcasted_iota(jnp.int32, (SUBLANES, LANES), 0)
        first = u[:SUBLANES] + jnp.where(row8 == 0, a[:SUBLANES] * hlast[SUBLANES - 1 : SUBLANES, :], 0.0)
        h = _scan_rows(a, jnp.concatenate([first, u[SUBLANES:]], axis=0), tc, rep_ref, a_scr, b_scr, reverse=False)
        hlast[...] = h[tc - SUBLANES :, :]
        h_ref[...] = h
        y_ref[...] = (h * _gelu(g_ref[...].astype(F32))).astype(BF16)

    wide = RNN_TILES_PER_STEP * LANES
    tile = lambda j, c: (0, j)
    vec = pl.BlockSpec((1, wide), tile)
    mats = pl.BlockSpec((RNN_TILES_PER_STEP, LANES, LANES), lambda j, c: (j, 0, 0))
    repeat = _repeat_matrix(tc)
    return _call(
        body,
        (proj, proj, cw, cb, wa2, ba, wx2, bx, lam, repeat),
        name=name,
        grid=(N_RNN_TILES // RNN_TILES_PER_STEP, s // tc),
        in_specs=[
            pl.BlockSpec((tc, wide), lambda j, c: (c, j)),
            pl.BlockSpec((tc, wide), lambda j, c: (c, D_RNN // wide + j)),
            pl.BlockSpec((CONV_WIDTH, wide), tile),
            vec,
            mats,
            vec,
            mats,
            vec,
            vec,
            pl.BlockSpec(repeat.shape, lambda j, c: (0, 0)),
        ],
        out_specs=[pl.BlockSpec((tc, wide), lambda j, c: (c, j)), pl.BlockSpec((tc, wide), lambda j, c: (c, j))],
        out_shape=[SDS((s, D_RNN), F32), SDS((s, D_RNN), BF16)],
        scratch_shapes=[pltpu.VMEM((SUBLANES, wide), F32)] * 2
        + [pltpu.VMEM((RNN_TILES_PER_STEP, tc, LANES), F32)] * 2
        + [pltpu.VMEM((RNN_TILES_PER_STEP, tc + SUBLANES, LANES), F32)],
        semantics=("parallel", "arbitrary"),
        comm=comm,
    )


def _branch_a_bwd(dy, proj, h, cw, cb, wa2, ba, wx2, bx, lam, wa2t, wx2t, *, tc, name, comm=None):
    s = proj.shape[0]
    tc = min(tc, s)
    nc = s // tc
    halo16 = tc // 16
    halo8 = tc // SUBLANES

    def body(dy_ref, x_ref, xh_ref, g_ref, h_ref, hh_ref, cw_ref, cb_ref, wa_ref, ba_ref, wx_ref, bx_ref, lam_ref,
             wat_ref, wxt_ref, rep_ref, dx_ref, dg_ref, dcw_ref, dcb_ref, dba_ref, dbx_ref, dlam_ref, dwa_ref, dwx_ref,
             carry, dxr_next, a_scr, b_scr, ext_scr):
        cc = pl.program_id(1)
        ct = nc - 1 - cc

        @pl.when(cc == 0)
        def _():
            carry[...] = jnp.zeros_like(carry)
            dxr_next[...] = jnp.zeros_like(dxr_next)
            for ref in (dcw_ref, dcb_ref, dba_ref, dbx_ref, dlam_ref, dwa_ref, dwx_ref):
                ref[...] = jnp.zeros_like(ref)

        for t in range(RNN_TILES_PER_STEP):
            cols = lambda ref: ref.at[:, pl.ds(t * LANES, LANES)]
            one_tile(
                ct, cols(dy_ref), cols(x_ref), cols(xh_ref), cols(g_ref), cols(h_ref), cols(hh_ref), cols(cw_ref), cols(cb_ref),
                wa_ref.at[t], cols(ba_ref), wx_ref.at[t], cols(bx_ref), cols(lam_ref), wat_ref.at[t], wxt_ref.at[t], rep_ref,
                cols(dx_ref), cols(dg_ref), cols(dcw_ref), cols(dcb_ref), cols(dba_ref), cols(dbx_ref), cols(dlam_ref),
                dwa_ref.at[t], dwx_ref.at[t], cols(carry), cols(dxr_next), a_scr.at[t], b_scr.at[t], ext_scr.at[t],
            )

    def one_tile(ct, dy_ref, x_ref, xh_ref, g_ref, h_ref, hh_ref, cw_ref, cb_ref, wa_ref, ba_ref, wx_ref, bx_ref, lam_ref,
                 wat_ref, wxt_ref, rep_ref, dx_ref, dg_ref, dcw_ref, dcb_ref, dba_ref, dbx_ref, dlam_ref, dwa_ref, dwx_ref,
                 carry, dxr_next, a_scr, b_scr, ext_scr):
        xc = x_ref[...].astype(F32)
        xprev = jnp.where(ct > 0, xh_ref[SUBLANES:, :].astype(F32), 0.0)
        xr, (x1, x2, x3), r, i, a, om = _conv_and_gates(
            xc, xprev, cw_ref, cb_ref, wa_ref, ba_ref, wx_ref, bx_ref, lam_ref, ext_scr
        )
        inv_norm = lax.rsqrt(om)
        norm = om * inv_norm
        row = lax.broadcasted_iota(jnp.int32, xc.shape, 0)

        hv = h_ref[...]
        ge, ge_grad = _gelu_and_grad(g_ref[...].astype(F32))
        dyv = dy_ref[...].astype(F32)
        dg_ref[...] = (dyv * hv * ge_grad).astype(dg_ref.dtype)
        dh = dyv * ge

        b = dh + jnp.where(row == tc - 1, carry[0:1, :], 0.0)
        a_next = jnp.where(row < tc - 1, pltpu.roll(a, tc - 1, 0), 0.0)
        gadj = _scan_rows(a_next, b, tc, rep_ref, a_scr, b_scr, reverse=True)
        carry[...] = (a * gadj)[:SUBLANES, :]

        hprev_first = jnp.where(ct > 0, hh_ref[SUBLANES - 1 : SUBLANES, :], 0.0)
        hprev = jnp.where(row >= 1, pltpu.roll(hv, 1, 0), hprev_first)
        da = gadj * hprev
        ix = i * xr
        dnorm = gadj * ix
        di = gadj * norm * xr
        dlog_a = da * a - dnorm * (1.0 - om) * inv_norm
        sp = _softplus_neg(lam_ref[...])
        dr = dlog_a * (-LRU_C * sp)
        dsp = jnp.sum(dlog_a * (-LRU_C * r), axis=0, keepdims=True)
        dlam_ref[...] += dsp * (-_sigmoid(-lam_ref[...]))
        dza = dr * r * (1.0 - r)
        dzx = di * i * (1.0 - i)
        dzab, dzxb = dza.astype(BF16), dzx.astype(BF16)
        dxr = gadj * norm * i + _dot(dzab, wat_ref[...]) + _dot(dzxb, wxt_ref[...])
        xrb = xr.astype(BF16)
        dwa_ref[...] += _dot_tn(xrb, dzab)
        dwx_ref[...] += _dot_tn(xrb, dzxb)
        dba_ref[...] += jnp.sum(dza, axis=0, keepdims=True)
        dbx_ref[...] += jnp.sum(dzx, axis=0, keepdims=True)

        ext = jnp.concatenate([dxr, dxr_next[...]], axis=0)
        dx = (
            dxr * cw_ref[3:4, :]
            + _rows_after(ext, 1, tc) * cw_ref[2:3, :]
            + _rows_after(ext, 2, tc) * cw_ref[1:2, :]
            + _rows_after(ext, 3, tc) * cw_ref[0:1, :]
        )
        dxr_next[...] = dxr[:SUBLANES, :]
        dx_ref[...] = dx.astype(dx_ref.dtype)
        dcb_ref[...] += jnp.sum(dxr, axis=0, keepdims=True)
        dcw_ref[3:4, :] += jnp.sum(dxr * xc, axis=0, keepdims=True)
        dcw_ref[2:3, :] += jnp.sum(dxr * x1, axis=0, keepdims=True)
        dcw_ref[1:2, :] += jnp.sum(dxr * x2, axis=0, keepdims=True)
        dcw_ref[0:1, :] += jnp.sum(dxr * x3, axis=0, keepdims=True)

    wide = RNN_TILES_PER_STEP * LANES
    tile = lambda j, c: (0, j)
    mat = lambda j, c: (j, 0, 0)
    cur = lambda j, c: (nc - 1 - c, j)
    vec = pl.BlockSpec((1, wide), tile)
    matspec = pl.BlockSpec((RNN_TILES_PER_STEP, LANES, LANES), mat)
    repeat = _repeat_matrix(tc)
    return _call(
        body,
        (dy, proj, proj, proj, h, h, cw, cb, wa2, ba, wx2, bx, lam, wa2t, wx2t, repeat),
        name=name,
        grid=(N_RNN_TILES // RNN_TILES_PER_STEP, nc),
        in_specs=[
            pl.BlockSpec((tc, wide), cur),
            pl.BlockSpec((tc, wide), cur),
            pl.BlockSpec((16, wide), lambda j, c: (jnp.maximum((nc - 1 - c) * halo16 - 1, 0), j)),
            pl.BlockSpec((tc, wide), lambda j, c: (nc - 1 - c, D_RNN // wide + j)),
            pl.BlockSpec((tc, wide), cur),
            pl.BlockSpec((SUBLANES, wide), lambda j, c: (jnp.maximum((nc - 1 - c) * halo8 - 1, 0), j)),
            pl.BlockSpec((CONV_WIDTH, wide), tile),
            vec,
            matspec,
            vec,
            matspec,
            vec,
            vec,
            matspec,
            matspec,
            pl.BlockSpec(repeat.shape, lambda j, c: (0, 0)),
        ],
        out_specs=[
            pl.BlockSpec((tc, wide), cur),
            pl.BlockSpec((tc, wide), cur),
            pl.BlockSpec((CONV_WIDTH, wide), tile),
            vec,
            vec,
            vec,
            vec,
            matspec,
            matspec,
        ],
        out_shape=[
            SDS((s, D_RNN), BF16),
            SDS((s, D_RNN), BF16),
            SDS((CONV_WIDTH, D_RNN), F32),
            SDS((1, D_RNN), F32),
            SDS((1, D_RNN), F32),
            SDS((1, D_RNN), F32),
            SDS((1, D_RNN), F32),
            SDS((N_RNN_TILES, LANES, LANES), F32),
            SDS((N_RNN_TILES, LANES, LANES), F32),
        ],
        scratch_shapes=[pltpu.VMEM((SUBLANES, wide), F32)] * 2
        + [pltpu.VMEM((RNN_TILES_PER_STEP, tc, LANES), F32)] * 2
        + [pltpu.VMEM((RNN_TILES_PER_STEP, tc + SUBLANES, LANES), F32)],
        semantics=("parallel", "arbitrary"),
        comm=comm,
    )


def _sgu_specs(tb):
    half = lambda blk: pl.BlockSpec((tb, 512), lambda n: (n, blk))
    return [half(U_BLK512), half(U_BLK512 + 1), half(V_BLK512), half(V_BLK512 + 1)]


def _sgu_normed(v, lng_ref, lnb_ref):
    gv, gv_grad = _gelu_and_grad(v)
    mu = jnp.mean(gv, axis=-1, keepdims=True)
    xc = gv - mu
    rs = lax.rsqrt(jnp.mean(xc * xc, axis=-1, keepdims=True) + EPS)
    xhat = xc * rs
    return xhat * lng_ref[...] + lnb_ref[...], xhat, rs, gv_grad


def _sgu_fwd(proj, lng, lnb, wm, bias, *, tb, name, comm=None):
    s = proj.shape[0]
    tb = min(tb, s)

    def body(u0_ref, u1_ref, v0_ref, v1_ref, lng_ref, lnb_ref, wm_ref, bias_ref, y_ref):
        u = jnp.concatenate([u0_ref[...], u1_ref[...]], axis=1).astype(F32)
        v = jnp.concatenate([v0_ref[...], v1_ref[...]], axis=1).astype(F32)
        gu = _gelu(u)
        vn, _, _, _ = _sgu_normed(v, lng_ref, lnb_ref)
        vnb = vn.astype(BF16)
        for blk in range(tb // SGU_BLOCK):
            rows = slice(blk * SGU_BLOCK, (blk + 1) * SGU_BLOCK)
            for g in range(SGU_GROUPS):
                cols = slice(g * LANES, (g + 1) * LANES)
                mixed = _dot(wm_ref[g], vnb[rows, cols]) + bias_ref[g]
                y_ref[rows, cols] = (gu[rows, cols] * mixed).astype(BF16)

    const2 = lambda n: (0, 0)
    const3 = lambda n: (0, 0, 0)
    return _call(
        body,
        (proj, proj, proj, proj, lng, lnb, wm, bias),
        name=name,
        grid=(s // tb,),
        in_specs=_sgu_specs(tb)
        + [
            pl.BlockSpec((1, D_SGU), const2),
            pl.BlockSpec((1, D_SGU), const2),
            pl.BlockSpec((SGU_GROUPS, SGU_BLOCK, SGU_BLOCK), const3),
            pl.BlockSpec((SGU_GROUPS, SGU_BLOCK, LANES), const3),
        ],
        out_specs=pl.BlockSpec((tb, D_SGU), lambda n: (n, 0)),
        out_shape=SDS((s, D_SGU), BF16),
        semantics=("parallel",),
        comm=comm,
    )


def _sgu_bwd(dy, proj, lng, lnb, wm, wmt, bias, mask, *, tb, name, comm=None):
    s = proj.shape[0]
    tb = min(tb, s)
    nb = s // tb

    def body(dy_ref, u0_ref, u1_ref, v0_ref, v1_ref, lng_ref, lnb_ref, wm_ref, wmt_ref, bias_ref, mask_ref,
             du_ref, dv_ref, dws_ref, dbs_ref, dlng_ref, dlnb_ref, dvn_scr, dbs_acc):
        n = pl.program_id(0)

        @pl.when(n == 0)
        def _():
            dbs_acc[...] = jnp.zeros_like(dbs_acc)
            for ref in (dws_ref, dlng_ref, dlnb_ref):
                ref[...] = jnp.zeros_like(ref)

        u = jnp.concatenate([u0_ref[...], u1_ref[...]], axis=1).astype(F32)
        v = jnp.concatenate([v0_ref[...], v1_ref[...]], axis=1).astype(F32)
        gu, gu_grad = _gelu_and_grad(u)
        vn, xhat, rs, gv_grad = _sgu_normed(v, lng_ref, lnb_ref)
        vnb = vn.astype(BF16)
        dyv = dy_ref[...].astype(F32)
        for blk in range(tb // SGU_BLOCK):
            rows = slice(blk * SGU_BLOCK, (blk + 1) * SGU_BLOCK)
            for g in range(SGU_GROUPS):
                cols = slice(g * LANES, (g + 1) * LANES)
                vt = vnb[rows, cols]
                mixed = _dot(wm_ref[g], vt) + bias_ref[g]
                dyt = dyv[rows, cols]
                du_ref[rows, cols] = (dyt * mixed * gu_grad[rows, cols]).astype(BF16)
                dmix = dyt * gu[rows, cols]
                dmixb = dmix.astype(BF16)
                dvn_scr[rows, cols] = _dot(wmt_ref[g], dmixb)
                dws_ref[g] += _dot_nt(dmixb, vt) * mask_ref[...]
                dbs_acc[g] += dmix
        dvn = dvn_scr[...]
        dlng_ref[...] += jnp.sum(dvn * xhat, axis=0, keepdims=True)
        dlnb_ref[...] += jnp.sum(dvn, axis=0, keepdims=True)
        dxh = dvn * lng_ref[...]
        dgv = rs * (
            dxh - jnp.mean(dxh, axis=-1, keepdims=True) - xhat * jnp.mean(dxh * xhat, axis=-1, keepdims=True)
        )
        dv_ref[...] = (dgv * gv_grad).astype(BF16)

        @pl.when(n == nb - 1)
        def _():
            for g in range(SGU_GROUPS):
                dbs_ref[g] = jnp.broadcast_to(jnp.sum(dbs_acc[g], axis=-1, keepdims=True), (SGU_BLOCK, LANES))

    const2 = lambda n: (0, 0)
    const3 = lambda n: (0, 0, 0)
    gmat = pl.BlockSpec((SGU_GROUPS, SGU_BLOCK, SGU_BLOCK), const3)
    vec = pl.BlockSpec((1, D_SGU), const2)
    act = pl.BlockSpec((tb, D_SGU), lambda n: (n, 0))
    return _call(
        body,
        (dy, proj, proj, proj, proj, lng, lnb, wm, wmt, bias, mask),
        name=name,
        grid=(nb,),
        in_specs=[act] + _sgu_specs(tb) + [vec, vec, gmat, gmat, gmat, pl.BlockSpec((SGU_BLOCK, SGU_BLOCK), const2)],
        out_specs=[act, act, gmat, gmat, vec, vec],
        out_shape=[
            SDS((s, D_SGU), BF16),
            SDS((s, D_SGU), BF16),
            SDS((SGU_GROUPS, SGU_BLOCK, SGU_BLOCK), F32),
            SDS((SGU_GROUPS, SGU_BLOCK, LANES), F32),
            SDS((1, D_SGU), F32),
            SDS((1, D_SGU), F32),
        ],
        scratch_shapes=[pltpu.VMEM((tb, D_SGU), F32), pltpu.VMEM((SGU_GROUPS, SGU_BLOCK, LANES), F32)],
        semantics=("arbitrary",),
        comm=comm,
    )


def _gate_specs(tm):
    half = lambda blk: pl.BlockSpec((tm, 512), lambda i: (i, blk))
    return [half(GA_BLK512), half(GA_BLK512 + 1), half(GB_BLK512), half(GB_BLK512 + 1)]


def _merge_fwd(ya_pre, yb_pre, proj, x, w_ba, w_bb, w_out, *, tm, name, comm=None):
    s = x.shape[0]
    tm = min(tm, s)

    def body(ya_ref, yb_ref, a0, a1, b0, b1, x_ref, wa_ref, wb_ref, wo_ref, x1_ref, yao_ref, ybo_ref):
        ya = _dot(ya_ref[...], wa_ref[...])
        yb = _dot(yb_ref[...], wb_ref[...])
        sa = _sigmoid(jnp.concatenate([a0[...], a1[...]], axis=1).astype(F32))
        sb = _sigmoid(jnp.concatenate([b0[...], b1[...]], axis=1).astype(F32))
        merged = sa * ya + sb * yb
        x1_ref[...] = x_ref[...] + _dot(merged.astype(BF16), wo_ref[...])
        yao_ref[...] = ya.astype(BF16)
        ybo_ref[...] = yb.astype(BF16)

    whole = lambda r: pl.BlockSpec((r, D), lambda i: (0, 0))
    act = pl.BlockSpec((tm, D), lambda i: (i, 0))
    return _call(
        body,
        (ya_pre, yb_pre, proj, proj, proj, proj, x, w_ba, w_bb, w_out),
        name=name,
        grid=(s // tm,),
        in_specs=[pl.BlockSpec((tm, D_RNN), lambda i: (i, 0)), act] + _gate_specs(tm) + [act, whole(D_RNN), whole(D_SGU), whole(D)],
        out_specs=[act, act, act],
        out_shape=[SDS((s, D), F32), SDS((s, D), BF16), SDS((s, D), BF16)],
        semantics=("parallel",),
        comm=comm,
    )


def _merge_bwd(dx1, ya, yb, proj, w_ba, w_bb, w_out, *, tm, name, comm=None):
    s = dx1.shape[0]
    tm = min(tm, s)

    def body(dx_ref, ya_ref, yb_ref, a0, a1, b0, b1, wa_ref, wb_ref, wo_ref,
             mg_ref, dya_ref, dyb_ref, dga_ref, dgb_ref, dyap_ref, dybp_ref):
        dm = _dot_nt(dx_ref[...], wo_ref[...])
        ya = ya_ref[...].astype(F32)
        yb = yb_ref[...].astype(F32)
        sa = _sigmoid(jnp.concatenate([a0[...], a1[...]], axis=1).astype(F32))
        sb = _sigmoid(jnp.concatenate([b0[...], b1[...]], axis=1).astype(F32))
        mg_ref[...] = (sa * ya + sb * yb).astype(BF16)
        dya = (dm * sa).astype(BF16)
        dyb = (dm * sb).astype(BF16)
        dya_ref[...] = dya
        dyb_ref[...] = dyb
        dga_ref[...] = (dm * ya * sa * (1.0 - sa)).astype(BF16)
        dgb_ref[...] = (dm * yb * sb * (1.0 - sb)).astype(BF16)
        dyap_ref[...] = _dot_nt(dya, wa_ref[...]).astype(BF16)
        dybp_ref[...] = _dot_nt(dyb, wb_ref[...]).astype(BF16)

    whole = lambda r: pl.BlockSpec((r, D), lambda i: (0, 0))
    act = pl.BlockSpec((tm, D), lambda i: (i, 0))
    act_rnn = pl.BlockSpec((tm, D_RNN), lambda i: (i, 0))
    return _call(
        body,
        (dx1, ya, yb, proj, proj, proj, proj, w_ba, w_bb, w_out),
        name=name,
        grid=(s // tm,),
        in_specs=[act, act, act] + _gate_specs(tm) + [whole(D_RNN), whole(D_SGU), whole(D)],
        out_specs=[act, act, act, act, act, act_rnn, act],
        out_shape=[SDS((s, D), BF16)] * 5 + [SDS((s, D_RNN), BF16), SDS((s, D_SGU), BF16)],
        semantics=("parallel",),
        comm=comm,
    )


def _ffn_down_loss(a, w, res, g, target, *, tm, name):
    s, k = a.shape
    tm = min(tm, s)

    def body(a_ref, w_ref, r_ref, g_ref, t_ref, dx_ref, dxb_ref, dg_ref, loss_ref):
        @pl.when(pl.program_id(0) == 0)
        def _():
            dg_ref[...] = jnp.zeros_like(dg_ref)
            loss_ref[...] = jnp.zeros_like(loss_ref)

        t = jnp.maximum(a_ref[...].astype(F32), 0.0)
        xv = r_ref[...] + _dot((t * t).astype(BF16), w_ref[...])
        r = lax.rsqrt(jnp.mean(xv * xv, axis=-1, keepdims=True) + EPS)
        xhat = xv * r
        e = xhat * g_ref[...] - t_ref[...]
        loss_ref[...] += 0.5 * jnp.sum(jnp.mean(e * e, axis=-1, keepdims=True), axis=0, keepdims=True)
        dy = e * (1.0 / D)
        dxh = dy * g_ref[...]
        dx = r * (dxh - xhat * jnp.mean(dxh * xhat, axis=-1, keepdims=True))
        dx_ref[...] = dx
        dxb_ref[...] = dx.astype(BF16)
        dg_ref[...] += jnp.sum(dy * xhat, axis=0, keepdims=True)

    act = pl.BlockSpec((tm, D), lambda i: (i, 0))
    vec = pl.BlockSpec((1, D), lambda i: (0, 0))
    return pl.pallas_call(
        body,
        name=name,
        grid=(s // tm,),
        in_specs=[pl.BlockSpec((tm, k), lambda i: (i, 0)), pl.BlockSpec((k, D), lambda i: (0, 0)), act, vec, act],
        out_specs=[act, act, vec, pl.BlockSpec((SUBLANES, LANES), lambda i: (0, 0))],
        out_shape=[SDS((s, D), F32), SDS((s, D), BF16), SDS((1, D), F32), SDS((SUBLANES, LANES), F32)],
        compiler_params=_params("arbitrary"),
    )(a, w, res, g, target)


def _adamw_math(w, g, m, v):
    m2 = ADAM_B1 * m + (1.0 - ADAM_B1) * g
    v2 = ADAM_B2 * v + (1.0 - ADAM_B2) * (g * g)
    m_hat = m2 / (1.0 - ADAM_B1**ADAM_STEP)
    v_hat = v2 / (1.0 - ADAM_B2**ADAM_STEP)
    delta = -ADAM_LR * (m_hat / (jnp.sqrt(v_hat) + ADAM_EPS) + ADAM_WD * w)
    return delta, m2, v2


def _row_tile(rows, cap):
    return max(t for t in range(SUBLANES, min(cap, rows) + 1, SUBLANES) if rows % t == 0)


def _adamw_layers(w, grads, m, v, *, tr, name):
    depth, r, c = w.shape
    tr = _row_tile(r, tr)

    def body(*refs):
        g_refs = refs[:depth]
        w_ref, m_ref, v_ref, g_out, d_ref, mo_ref, vo_ref = refs[depth:]
        for l in range(depth):

            @pl.when(pl.program_id(0) == l)
            def _(l=l):
                g = g_refs[l][...]
                g_out[...] = g
                d_ref[...], mo_ref[...], vo_ref[...] = _adamw_math(w_ref[...], g, m_ref[...], v_ref[...])

    def of_layer(ll):
        return pl.BlockSpec((tr, c), lambda l, i: (jnp.where(l == ll, i, 0), 0))

    stacked = pl.BlockSpec((None, tr, c), lambda l, i: (l, i, 0))
    return pl.pallas_call(
        body,
        name=name,
        grid=(depth, r // tr),
        in_specs=[of_layer(ll) for ll in range(depth)] + [stacked] * 3,
        out_specs=[stacked] * 4,
        out_shape=[SDS((depth, r, c), F32)] * 4,
        compiler_params=_params("parallel", "parallel"),
    )(*grads, w, m, v)


def _adamw_reduced(w, parts, from_chips, m, v, chip, *, tr, name):
    depth, r, _ = w.shape
    tr = _row_tile(r, tr)

    def body(chip_ref, *refs):
        p_refs, c_refs = refs[:depth], refs[depth : 2 * depth]
        w_ref, m_ref, v_ref, g_out, d_ref, mo_ref, vo_ref = refs[2 * depth :]
        for l in range(depth):

            @pl.when(pl.program_id(0) == l)
            def _(l=l):
                got = c_refs[l]
                g = ((p_refs[l][...].astype(F32) + got[0].astype(F32)) + got[1].astype(F32)) + got[2].astype(F32)
                g_out[...] = g
                d_ref[...], mo_ref[...], vo_ref[...] = _adamw_math(w_ref[...], g, m_ref[...], v_ref[...])

    def mine_of_layer(ll):
        return pl.BlockSpec((None, tr, D), lambda l, i, chip_ref: (chip_ref[0], jnp.where(l == ll, i, 0), 0))

    def theirs_of_layer(ll):
        return pl.BlockSpec((3, tr, D), lambda l, i, chip_ref: (0, jnp.where(l == ll, i, 0), 0))

    stacked = pl.BlockSpec((None, tr, D), lambda l, i, chip_ref: (l, i, 0))
    return pl.pallas_call(
        body,
        name=name,
        grid_spec=pltpu.PrefetchScalarGridSpec(
            num_scalar_prefetch=1,
            grid=(depth, r // tr),
            in_specs=[mine_of_layer(ll) for ll in range(depth)]
            + [theirs_of_layer(ll) for ll in range(depth)]
            + [stacked] * 3,
            out_specs=[stacked] * 4,
        ),
        out_shape=[SDS((depth, r, D), F32)] * 4,
        compiler_params=_params("parallel", "parallel"),
    )(chip, *parts, *from_chips, w, m, v)


def _adamw_small(groups, *, name):
    n = len(groups)

    def body(*refs):
        ins, outs = refs[: 4 * n], refs[4 * n :]
        for i in range(n):
            w, g, m, v = (ref[...] for ref in ins[4 * i : 4 * i + 4])
            outs[3 * i][...], outs[3 * i + 1][...], outs[3 * i + 2][...] = _adamw_math(w, g, m, v)

    vmem = pl.BlockSpec(memory_space=pltpu.VMEM)
    outs = pl.pallas_call(
        body,
        name=name,
        in_specs=[vmem] * (4 * n),
        out_specs=[vmem] * (3 * n),
        out_shape=[SDS(grp[0].shape, F32) for grp in groups for _ in range(3)],
        compiler_params=pltpu.CompilerParams(vmem_limit_bytes=VMEM_LIMIT_BYTES),
    )(*[a for grp in groups for a in grp])
    return [tuple(outs[3 * i : 3 * i + 3]) for i in range(n)]


ANY = pl.BlockSpec(memory_space=pl.ANY)


def _position():
    return lax.axis_index("x"), lax.axis_index("y"), lax.axis_index("c")


def _other_chips(x, y):
    return [(1 - x, y), (x, 1 - y), (1 - x, 1 - y)]


class _Comm:
    def __init__(self, inputs, out_shapes, sem_counts, start, middle, finish, middle_at=1.0, aliases=()):
        self.inputs, self.out_shapes, self.sem_counts = list(inputs), list(out_shapes), list(sem_counts)
        self.start, self.middle, self.finish = start, middle, finish
        self.middle_at = middle_at
        self.aliases = list(aliases)

    def sem_shapes(self):
        return [pltpu.SemaphoreType.DMA((n,)) for n in self.sem_counts]


def _merge_comms(comms):
    bounds, i, o, s = [], 0, 0, 0
    for cm in comms:
        bounds.append((i, i + len(cm.inputs), o, o + len(cm.out_shapes), s, s + len(cm.sem_counts)))
        i, o, s = bounds[-1][1], bounds[-1][3], bounds[-1][5]

    def phase(which):
        def run(ins, outs, sems):
            for cm, (i0, i1, o0, o1, s0, s1) in zip(comms, bounds):
                getattr(cm, which)(ins[i0:i1], outs[o0:o1], sems[s0:s1])

        return run

    return _Comm(
        [a for cm in comms for a in cm.inputs],
        [a for cm in comms for a in cm.out_shapes],
        [a for cm in comms for a in cm.sem_counts],
        phase("start"),
        phase("middle"),
        phase("finish"),
        middle_at=max(cm.middle_at for cm in comms),
        aliases=[(i0 + i, o0 + o) for cm, (i0, _, o0, _, _, _) in zip(comms, bounds) for i, o in cm.aliases],
    )


def _call(body, args, *, semantics, comm=None, **kw):
    if comm is None:
        return pl.pallas_call(body, compiler_params=_params(*semantics), **kw)(*args)
    grid, in_specs, out_specs, out_shape = kw["grid"], kw["in_specs"], kw["out_specs"], kw["out_shape"]
    scratch = list(kw.get("scratch_shapes", ()))
    single = not isinstance(out_shape, (list, tuple))
    core_specs = [out_specs] if single else list(out_specs)
    core_shapes = [out_shape] if single else list(out_shape)
    n_in, n_out, n_scr = len(in_specs), len(core_shapes), len(scratch)
    n_cin, n_cout = len(comm.inputs), len(comm.out_shapes)
    steps = 1
    for g in grid:
        steps *= g
    middle = min(int(comm.middle_at * steps), steps - 1)

    def hosted(*refs):
        core_in, c_in = refs[:n_in], refs[n_in : n_in + n_cin]
        o0 = n_in + n_cin
        core_out, c_out = refs[o0 : o0 + n_out], refs[o0 + n_out : o0 + n_out + n_cout]
        s0 = o0 + n_out + n_cout
        core_scr, sems = refs[s0 : s0 + n_scr], refs[s0 + n_scr :]
        step = pl.program_id(0)
        for d in range(1, len(grid)):
            step = step * grid[d] + pl.program_id(d)

        @pl.when(step == 0)
        def _():
            comm.start(c_in, c_out, sems)

        body(*core_in, *core_out, *core_scr)

        @pl.when(step == middle)
        def _():
            comm.middle(c_in, c_out, sems)

        @pl.when(step == steps - 1)
        def _():
            comm.finish(c_in, c_out, sems)

    outs = pl.pallas_call(
        hosted,
        name=kw["name"],
        grid=grid,
        in_specs=list(in_specs) + [ANY] * n_cin,
        out_specs=core_specs + [ANY] * n_cout,
        out_shape=core_shapes + comm.out_shapes,
        scratch_shapes=scratch + comm.sem_shapes(),
        input_output_aliases={n_in + i: n_out + o for i, o in comm.aliases},
        compiler_params=_params(*(["arbitrary"] * len(grid))),
    )(*args, *comm.inputs)
    return (outs[0] if single else outs[:n_out]), outs[n_out:]


def _comm_only(comm, *, name):
    n_cin, n_cout = len(comm.inputs), len(comm.out_shapes)

    def body(*refs):
        ins, outs, sems = refs[:n_cin], refs[n_cin : n_cin + n_cout], refs[n_cin + n_cout :]
        comm.start(ins, outs, sems)
        comm.middle(ins, outs, sems)
        comm.finish(ins, outs, sems)

    return pl.pallas_call(
        body,
        name=name,
        in_specs=[ANY] * n_cin,
        out_specs=[ANY] * n_cout,
        out_shape=comm.out_shapes,
        scratch_shapes=comm.sem_shapes(),
    )(*comm.inputs)


def _gather_comm(shards, pass_on_at=1.0):
    n = len(shards)
    per = 7

    def plan(ins, outs, sems):
        send, recv, local = sems
        x, y, c = _position()
        me, sibling = (x, y, c), (x, y, 1 - c)
        chips = _other_chips(x, y)

        def block(t, px, py, pc):
            return outs[t].at[pl.ds(4 * px + 2 * py + pc, 1)]

        def copy(t, k, blk, to, src=None):
            return pltpu.make_async_remote_copy(
                src_ref=block(t, *blk) if src is None else src,
                dst_ref=block(t, *blk),
                send_sem=send.at[t * per + k],
                recv_sem=recv.at[t * per + k],
                device_id=to,
                device_id_type=MESH,
            )

        mine = [pltpu.make_async_copy(ins[t], block(t, *me), local.at[t]) for t in range(n)]
        to_chips = [copy(t, 1 + j, me, (*chip, c), src=ins[t]) for t in range(n) for j, chip in enumerate(chips)]
        to_sibling = [copy(t, 0, me, sibling, src=ins[t]) for t in range(n)]
        from_chips = [copy(t, 1 + j, (*chip, c), me) for t in range(n) for j, chip in enumerate(chips)]
        passed_on = [copy(t, 4 + j, (*chip, c), sibling) for t in range(n) for j, chip in enumerate(chips)]
        from_sibling = [copy(t, 0, sibling, me) for t in range(n)]
        from_sibling += [copy(t, 4 + j, (*chip, 1 - c), me) for t in range(n) for j, chip in enumerate(chips)]
        return mine, to_chips, to_sibling, from_chips, passed_on, from_sibling

    def start(ins, outs, sems):
        mine, to_chips, to_sibling, _, _, _ = plan(ins, outs, sems)
        for cp in mine + to_chips + to_sibling:
            cp.start()

    def middle(ins, outs, sems):
        _, _, _, from_chips, passed_on, _ = plan(ins, outs, sems)
        for arrived, onward in zip(from_chips, passed_on):
            arrived.wait_recv()
            onward.start()

    def finish(ins, outs, sems):
        mine, to_chips, to_sibling, _, passed_on, from_sibling = plan(ins, outs, sems)
        for cp in from_sibling:
            cp.wait_recv()
        for cp in to_chips + to_sibling + passed_on:
            cp.wait_send()
        for cp in mine:
            cp.wait()

    out_shapes = [SDS((N_DEV,) + sh.shape[1:], sh.dtype) for sh in shards]
    return _Comm(shards, out_shapes, [n * per, n * per, n], start, middle, finish, middle_at=pass_on_at)


def _gather_halves(shards=None, arrived=None):
    first_half = arrived is None
    arrays = shards if first_half else arrived
    n = len(arrays)
    per = 4 if first_half else 3

    def plan(ins, outs, sems):
        x, y, c = _position()
        me, sibling = (x, y, c), (x, y, 1 - c)
        chips = _other_chips(x, y)

        def block(t, px, py, pc):
            return outs[t].at[pl.ds(4 * px + 2 * py + pc, 1)]

        def copy(t, k, blk, to, src=None):
            return pltpu.make_async_remote_copy(
                src_ref=block(t, *blk) if src is None else src,
                dst_ref=block(t, *blk),
                send_sem=sems[0].at[t * per + k],
                recv_sem=sems[1].at[t * per + k],
                device_id=to,
                device_id_type=MESH,
            )

        if first_half:
            local = [pltpu.make_async_copy(ins[t], block(t, *me), sems[2].at[t]) for t in range(n)]
            sent = [copy(t, 1 + j, me, (*chip, c), src=ins[t]) for t in range(n) for j, chip in enumerate(chips)]
            sent += [copy(t, 0, me, sibling, src=ins[t]) for t in range(n)]
            landing = [copy(t, 1 + j, (*chip, c), me) for t in range(n) for j, chip in enumerate(chips)]
            landing += [copy(t, 0, sibling, me) for t in range(n)]
        else:
            local = []
            sent = [copy(t, j, (*chip, c), sibling) for t in range(n) for j, chip in enumerate(chips)]
            landing = [copy(t, j, (*chip, 1 - c), me) for t in range(n) for j, chip in enumerate(chips)]
        return local, sent, landing

    def start(ins, outs, sems):
        local, sent, _ = plan(ins, outs, sems)
        for cp in local + sent:
            cp.start()

    def middle(ins, outs, sems):
        pass

    def finish(ins, outs, sems):
        local, sent, landing = plan(ins, outs, sems)
        for cp in landing:
            cp.wait_recv()
        for cp in sent:
            cp.wait_send()
        for cp in local:
            cp.wait()

    if first_half:
        out_shapes = [SDS((N_DEV,) + sh.shape[1:], sh.dtype) for sh in shards]
        return _Comm(shards, out_shapes, [n * per, n * per, n], start, middle, finish)
    out_shapes = [SDS(a.shape, a.dtype) for a in arrived]
    return _Comm(arrived, out_shapes, [n * per, n * per], start, middle, finish, aliases=[(t, t) for t in range(n)])


def _exchange_comm(arrays, out_shapes, n_copies, copies_of):
    def start(ins, outs, sems):
        for cp in copies_of(ins, outs, *sems):
            cp.start()

    def middle(ins, outs, sems):
        pass

    def finish(ins, outs, sems):
        for cp in copies_of(ins, outs, *sems):
            cp.wait()

    return _Comm(arrays, out_shapes, [n_copies, n_copies], start, middle, finish)


def _sibling_comm(grads):
    def copies_of(ins, outs, send, recv):
        x, y, c = _position()
        return [
            pltpu.make_async_remote_copy(
                src_ref=ins[t].at[:, pl.ds(1 - c, 1)],
                dst_ref=outs[t],
                send_sem=send.at[t],
                recv_sem=recv.at[t],
                device_id=(x, y, 1 - c),
                device_id_type=MESH,
            )
            for t in range(len(ins))
        ]

    return _exchange_comm(grads, [SDS((4, 1) + g.shape[2:], g.dtype) for g in grads], len(grads), copies_of)


def _chips_comm(parts):
    def copies_of(ins, outs, send, recv):
        x, y, c = _position()
        return [
            pltpu.make_async_remote_copy(
                src_ref=ins[t].at[pl.ds(2 * px + py, 1)],
                dst_ref=outs[t].at[pl.ds(k, 1)],
                send_sem=send.at[3 * t + k],
                recv_sem=recv.at[3 * t + k],
                device_id=(px, py, c),
                device_id_type=MESH,
            )
            for t in range(len(ins))
            for k, (px, py) in enumerate(_other_chips(x, y))
        ]

    return _exchange_comm(parts, [SDS((3,) + p.shape[1:], p.dtype) for p in parts], 3 * len(parts), copies_of)


def _sum_with_sibling(grad, got, core, *, name):
    rows = grad.shape[2]

    def body(core_ref, a_ref, b_ref, o_ref):
        o_ref[...] = (a_ref[...].astype(F32) + b_ref[...].astype(F32)).astype(o_ref.dtype)

    return pl.pallas_call(
        body,
        name=name,
        grid_spec=pltpu.PrefetchScalarGridSpec(
            num_scalar_prefetch=1,
            grid=(4,),
            in_specs=[
                pl.BlockSpec((None, None, rows, D), lambda q, core_ref: (q, core_ref[0], 0, 0)),
                pl.BlockSpec((None, None, rows, D), lambda q, core_ref: (q, 0, 0, 0)),
            ],
            out_specs=pl.BlockSpec((None, rows, D), lambda q, core_ref: (q, 0, 0)),
        ),
        out_shape=SDS((4, rows, D), grad.dtype),
        compiler_params=_params("parallel"),
    )(core, grad, got)


def _sum_chips(part, got, chip, *, name):
    rows = part.shape[1]

    def body(chip_ref, a_ref, b_ref, o_ref):
        o_ref[...] = ((a_ref[...].astype(F32) + b_ref[0].astype(F32)) + b_ref[1].astype(F32)) + b_ref[2].astype(F32)

    return pl.pallas_call(
        body,
        name=name,
        grid_spec=pltpu.PrefetchScalarGridSpec(
            num_scalar_prefetch=1,
            grid=(1,),
            in_specs=[
                pl.BlockSpec((None, rows, D), lambda i, chip_ref: (chip_ref[0], 0, 0)),
                pl.BlockSpec((3, rows, D), lambda i, chip_ref: (0, 0, 0)),
            ],
            out_specs=pl.BlockSpec((rows, D), lambda i, chip_ref: (0, 0)),
        ),
        out_shape=SDS((rows, D), F32),
        compiler_params=_params("arbitrary"),
    )(chip, part, got)


def _all_reduce_small(pack, *, name):
    rows = pack.shape[1]

    def body(in_ref, out_ref, from_sibling, part, from_chips, send, recv):
        x, y, c = _position()
        me, sibling = (x, y, c), (x, y, 1 - c)
        chips = _other_chips(x, y)
        waiting = []

        def copy(k, src, dst, to):
            return pltpu.make_async_remote_copy(
                src_ref=src, dst_ref=dst, send_sem=send.at[k], recv_sem=recv.at[k], device_id=to, device_id_type=MESH
            )

        def exchange(copies):
            for cp in copies:
                cp.start()
            for cp in copies:
                cp.wait_recv()
            waiting.extend(copies)

        def block(px, py, pc):
            return out_ref.at[4 * px + 2 * py + pc]

        exchange([copy(q, in_ref.at[2 * q + 1 - c], from_sibling.at[q], sibling) for q in range(4)])
        for q in range(4):
            part[q] = in_ref[2 * q + c] + from_sibling[q]
        exchange([copy(4 + k, part.at[2 * px + py], from_chips.at[k], (px, py, c)) for k, (px, py) in enumerate(chips)])
        out_ref[4 * x + 2 * y + c] = ((part[2 * x + y] + from_chips[0]) + from_chips[1]) + from_chips[2]
        exchange(
            [copy(7, block(*me), block(*me), sibling)]
            + [copy(8 + k, block(*me), block(*me), (px, py, c)) for k, (px, py) in enumerate(chips)]
        )
        exchange([copy(11 + k, block(px, py, c), block(px, py, c), sibling) for k, (px, py) in enumerate(chips)])
        for cp in waiting:
            cp.wait_send()

    vmem = pl.BlockSpec(memory_space=pltpu.VMEM)
    return pl.pallas_call(
        body,
        name=name,
        in_specs=[vmem],
        out_specs=vmem,
        out_shape=SDS(pack.shape, F32),
        scratch_shapes=[
            pltpu.VMEM((4, rows, D), F32),
            pltpu.VMEM((4, rows, D), F32),
            pltpu.VMEM((3, rows, D), F32),
            pltpu.SemaphoreType.DMA((14,)),
            pltpu.SemaphoreType.DMA((14,)),
        ],
        compiler_params=pltpu.CompilerParams(vmem_limit_bytes=VMEM_LIMIT_BYTES),
    )(pack)


def _pack(arrays, rows):
    flat = jnp.concatenate([a.reshape(-1).astype(F32) for a in arrays])
    return jnp.pad(flat, (0, rows * D - flat.shape[0])).reshape(rows, D)


def _unpack(pack, shapes):
    flat = pack.reshape(-1)
    out, off = [], 0
    for sh in shapes:
        size = 1
        for dim in sh:
            size *= dim
        out.append(flat[off : off + size].reshape(sh))
        off += size
    return out


def _block_diag_pairs(w):
    w = w.reshape(N_RNN_TILES, 2, HEAD_DIM, HEAD_DIM)
    z = jnp.zeros_like(w[:, 0])
    top = jnp.concatenate([w[:, 0], z], axis=2)
    bot = jnp.concatenate([z, w[:, 1]], axis=2)
    return jnp.concatenate([top, bot], axis=1)


def _diag_blocks(w2):
    a = w2[:, :HEAD_DIM, :HEAD_DIM]
    b = w2[:, HEAD_DIM:, HEAD_DIM:]
    return jnp.stack([a, b], axis=1).reshape(RNN_HEADS, HEAD_DIM, HEAD_DIM)


BIG = ("w_in", "w_branch_a", "w_branch_b", "w_out", "w_up", "w_down")
TRANSPOSED = ("w_in", "w_up")
SMALL = (
    "norm_mix_g", "conv_w", "conv_b", "lru_w_a", "lru_b_a", "lru_w_x", "lru_b_x", "lru_lambda",
    "sgu_ln_g", "sgu_ln_b", "sgu_w_s", "sgu_b_s", "norm_ffn_g", "final_norm_g",
)
WEIGHTS = (
    "norm_mix_g", "w_in", "conv_w", "conv_b", "lru_w_a", "lru_b_a", "lru_w_x", "lru_b_x", "lru_lambda", "sgu_ln_g",
    "sgu_ln_b", "sgu_w_s", "sgu_b_s", "w_branch_a", "w_branch_b", "w_out", "norm_ffn_g", "w_up", "w_down", "final_norm_g",
)

TM = 512
TM_NT = 1024
TN_IN = 1664
TN_UP = 2048
TKA = 512
TKA_PIECES = 256
TC = 512
TC_BWD = 1024
TB = 256
TB_BWD = 512
TR = 256


_BRANCH_WEIGHTS = ("w_branch_a", "w_branch_b", "w_out")
GATHERS_RIDING = (
    {
        "in_proj": ([(0, name) for name in _BRANCH_WEIGHTS] + [(0, "w_up")], []),
        "branch_a_fwd": ([(1, "w_in")], [(0, name) for name in _BRANCH_WEIGHTS] + [(0, "w_up")]),
        "sgu_fwd": ([], [(1, "w_in")]),
        "merge_fwd": ([(0, "w_down")], []),
        "ffn_up": ([(1, name) for name in _BRANCH_WEIGHTS], [(0, "w_down")]),
        "ffn_down": ([], [(1, name) for name in _BRANCH_WEIGHTS]),
    },
    {"in_proj": ([(1, "w_down")], []), "branch_a_fwd": ([(1, "w_up")], [(1, "w_down")]), "sgu_fwd": ([], [(1, "w_up")])},
)


def _layer_forward(l, x, p, w, shards, arriving, loss_head=None):
    def run(key, fn, *args, **kw):
        first, second = GATHERS_RIDING[l].get(key, ((), ()))
        comms = []
        if first:
            comms.append(_gather_halves(shards=[shards[l2][n2] for l2, n2 in first]))
        if second:
            comms.append(_gather_halves(arrived=[arriving.pop(k) for k in second]))
        if not comms:
            return fn(*args, **kw)
        out, got = fn(*args, comm=_merge_comms(comms), **kw)
        arriving.update(zip(first, got[: len(first)]))
        for (l2, n2), full in zip(second, got[len(first) :]):
            w[l2][n2] = full.reshape(-1, D)
        return out

    proj, h = run("in_proj", _norm_matmul_nt, x, p["norm_mix_g"], w[l]["w_in"], tm=TM_NT, tn=TN_IN, name=f"in_proj_{l}")
    hseq, ya_pre = run(
        "branch_a_fwd", _branch_a_fwd, proj, p["conv_w"], p["conv_b"], p["wa2"], p["lru_b_a"], p["wx2"], p["lru_b_x"],
        p["lru_lambda"], tc=TC, name=f"branch_a_fwd_{l}",
    )
    yb_pre = run("sgu_fwd", _sgu_fwd, proj, p["sgu_ln_g"], p["sgu_ln_b"], p["wm"], p["sgu_bias"], tb=TB, name=f"sgu_fwd_{l}")
    x1, ya, yb = run(
        "merge_fwd", _merge_fwd, ya_pre, yb_pre, proj, x, w[l]["w_branch_a"], w[l]["w_branch_b"], w[l]["w_out"], tm=TM,
        name=f"merge_fwd_{l}",
    )
    f_pre, h2 = run("ffn_up", _norm_matmul_nt, x1, p["norm_ffn_g"], w[l]["w_up"], tm=TM_NT, tn=TN_UP, name=f"ffn_up_{l}")
    saved = dict(x=x, h=h, proj=proj, hseq=hseq, ya_pre=ya_pre, yb_pre=yb_pre, ya=ya, yb=yb, x1=x1, h2=h2, f_pre=f_pre)
    if loss_head is None:
        return run("ffn_down", _matmul_nn_res, f_pre, w[l]["w_down"], x1, relu2=True, tm=TM, name=f"ffn_down_{l}"), saved
    return _ffn_down_loss(f_pre, w[l]["w_down"], x1, *loss_head, tm=TM, name=f"ffn_down_loss_{l}"), saved


def _layer_backward(l, dx2, dx2b, sv, p, w, core, waiting, last):
    parts, from_chips = {}, {}

    def by_device(g):
        return g.reshape(4, 2, -1, D)

    def with_sibling(name, g, got):
        parts[name] = _sum_with_sibling(by_device(g), got, core, name=f"sum_sibling_{name}_{l}")

    df_pre = _matmul_nt_drelu2(dx2b, w["w_down"], sv["f_pre"], tm=TM_NT, tn=TN_UP, name=f"ffn_down_bwd_{l}")
    g_down = _matmul_tn([sv["f_pre"]], dx2b, relu2=True, tka=TKA, name=f"grad_w_down_{l}")
    g_up, (got,) = _matmul_tn(
        [df_pre], sv["h2"], relu2=False, tka=TKA, name=f"grad_w_up_{l}", comm=_sibling_comm([by_device(g_down)])
    )
    with_sibling("w_down", g_down, got)
    (dx1, dx1b, g_norm_ffn), (got,) = _matmul_nn_rmsnorm_bwd(
        [df_pre], w["w_up"], sv["x1"], p["norm_ffn_g"], dx2, tm=TM, name=f"ffn_up_bwd_{l}",
        comm=_sibling_comm([by_device(g_up)]),
    )
    with_sibling("w_up", g_up, got)
    (merged, dya, dyb, dga, dgb, dya_pre, dyb_pre), (from_chips[l, "w_up"],) = _merge_bwd(
        dx1b, sv["ya"], sv["yb"], sv["proj"], w["w_branch_a"], w["w_branch_b"], w["w_out"], tm=TM, name=f"merge_bwd_{l}",
        comm=_chips_comm([parts["w_up"]]),
    )
    g_out, g_ba, g_bb = _matmuls_tn(
        [(merged, dx1b), (sv["ya_pre"], dya), (sv["yb_pre"], dyb)], ts=2 * TM, name=f"grad_w_branches_{l}"
    )
    branch = (("w_out", g_out), ("w_branch_a", g_ba), ("w_branch_b", g_bb))
    (du, dv, g_ws, g_bs, g_lng, g_lnb), got = _sgu_bwd(
        dyb_pre, sv["proj"], p["sgu_ln_g"], p["sgu_ln_b"], p["wm"], p["wmt"], p["sgu_bias"], p["mask"], tb=TB_BWD,
        name=f"sgu_bwd_{l}",
        comm=_merge_comms([_sibling_comm([by_device(g) for _, g in branch]), _chips_comm([parts["w_down"]])]),
    )
    from_chips[l, "w_down"] = got[-1]
    for (name, g), landed in zip(branch, got):
        with_sibling(name, g, landed)
    riding = [((l, name), parts[name]) for name, _ in branch] + list(waiting)
    (dxr, dgr, g_cw, g_cb, g_ba_, g_bx, g_lam, g_wa2, g_wx2), got = _branch_a_bwd(
        dya_pre, sv["proj"], sv["hseq"], p["conv_w"], p["conv_b"], p["wa2"], p["lru_b_a"], p["wx2"], p["lru_b_x"],
        p["lru_lambda"], p["wa2t"], p["wx2t"], tc=TC_BWD, name=f"branch_a_bwd_{l}", comm=_chips_comm([part for _, part in riding]),
    )
    for (key, _), landed in zip(riding, got):
        from_chips[key] = landed
    dproj = [dxr, dgr, du, dv, dga, dgb]
    g_in = _matmul_tn(dproj, sv["h"], relu2=False, tka=TKA_PIECES, name=f"grad_w_in_{l}")
    if last:
        (got,) = _comm_only(_sibling_comm([by_device(g_in)]), name=f"grad_w_in_to_sibling_{l}")
        with_sibling("w_in", g_in, got)
        riding = _chips_comm([parts["w_in"]])
    else:
        riding = _sibling_comm([by_device(g_in)])
    (dx, dxb, g_norm_mix), (got,) = _matmul_nn_rmsnorm_bwd(
        dproj, w["w_in"], sv["x"], p["norm_mix_g"], dx1, tm=TM, name=f"in_proj_bwd_{l}", comm=riding
    )
    if last:
        from_chips[l, "w_in"] = got
    else:
        with_sibling("w_in", g_in, got)
    small = dict(
        norm_mix_g=g_norm_mix[0], conv_w=g_cw, conv_b=g_cb[0], lru_w_a=_diag_blocks(g_wa2), lru_b_a=g_ba_.reshape(RNN_HEADS, HEAD_DIM),
        lru_w_x=_diag_blocks(g_wx2), lru_b_x=g_bx.reshape(RNN_HEADS, HEAD_DIM), lru_lambda=g_lam[0], sgu_ln_g=g_lng[0],
        sgu_ln_b=g_lnb[0], sgu_w_s=g_ws, sgu_b_s=g_bs[:, :, 0], norm_ffn_g=g_norm_ffn[0],
    )
    return dx, dxb, small, parts, from_chips


def _prepare_small(l, given):
    chunk_id = jnp.arange(SGU_BLOCK) // CHUNK
    mask = (chunk_id[:, None] >= chunk_id[None, :]).astype(F32)
    wm = given["sgu_w_s"][l] * mask
    wa2 = _block_diag_pairs(given["lru_w_a"][l])
    wx2 = _block_diag_pairs(given["lru_w_x"][l])
    row = lambda a: a.reshape(1, -1)
    return dict(
        norm_mix_g=row(given["norm_mix_g"][l]),
        norm_ffn_g=row(given["norm_ffn_g"][l]),
        conv_w=given["conv_w_full"][l],
        conv_b=row(given["conv_b"][l]),
        wa2=wa2.astype(BF16),
        wx2=wx2.astype(BF16),
        wa2t=jnp.swapaxes(wa2, 1, 2).astype(BF16),
        wx2t=jnp.swapaxes(wx2, 1, 2).astype(BF16),
        lru_b_a=row(given["lru_b_a"][l]),
        lru_b_x=row(given["lru_b_x"][l]),
        lru_lambda=row(given["lru_lambda"][l]),
        sgu_ln_g=row(given["sgu_ln_g"][l]),
        sgu_ln_b=row(given["sgu_ln_b"][l]),
        wm=wm.astype(BF16),
        wmt=jnp.swapaxes(wm, 1, 2).astype(BF16),
        sgu_bias=jnp.broadcast_to(given["sgu_b_s"][l][:, :, None], (SGU_GROUPS, SGU_BLOCK, LANES)),
        mask=mask,
    )


def _step(given):
    x_idx, y_idx, c_idx = _position()
    dev = 4 * x_idx + 2 * y_idx + c_idx
    core = c_idx.astype(jnp.int32).reshape(1)
    chip = (2 * x_idx + y_idx).astype(jnp.int32).reshape(1)

    def rows_first(name, a):
        return jnp.swapaxes(a, 1, 2) if name in TRANSPOSED else a

    shards = []
    for l in range(DEPTH):
        shards.append({name: rows_first(name, given[name])[l].astype(BF16)[None] for name in BIG})
    conv_mine = given["conv_w"].reshape(1, DEPTH * CONV_WIDTH, D_RNN // N_DEV)
    w_in_first, conv_all = _comm_only(_gather_comm([shards[0]["w_in"], conv_mine]), name="gather_first")
    weights = [{"w_in": w_in_first.reshape(-1, D)}, {}]
    conv_all = conv_all.reshape(N_DEV, DEPTH, CONV_WIDTH, D_RNN // N_DEV)
    given = dict(given, conv_w_full=jnp.moveaxis(conv_all, 0, 2).reshape(DEPTH, CONV_WIDTH, D_RNN))

    small_params = [_prepare_small(l, given) for l in range(DEPTH)]
    x = given["x"][0]
    saved, arriving = [], {}
    loss_head = (given["final_norm_g"].reshape(1, D), given["loss_target"][0])
    for l in range(DEPTH):
        x, sv = _layer_forward(
            l, x, small_params[l], weights, shards, arriving, loss_head=loss_head if l == DEPTH - 1 else None
        )
        saved.append(sv)
    dx, dxb, g_final, loss = x
    small_grads, parts, from_chips, waiting = [None] * DEPTH, [None] * DEPTH, {}, []
    for l in reversed(range(DEPTH)):
        dx, dxb, small_grads[l], parts[l], got = _layer_backward(
            l, dx, dxb, saved[l], small_params[l], weights[l], core, waiting, last=l == 0
        )
        from_chips.update(got)
        waiting = [((l, "w_in"), parts[l]["w_in"])]

    small_list = []
    for name in SMALL[:-1]:
        small_list.append(jnp.stack([small_grads[l][name] for l in range(DEPTH)]))
    small_list += [g_final[0], loss[0, :1]]
    small_shapes = [a.shape for a in small_list]
    pack = _pack(small_list, SMALL_ROWS).reshape(N_DEV, SMALL_ROWS_PER_DEV, D)
    summed = _unpack(_all_reduce_small(pack, name="all_reduce_small"), small_shapes)
    loss_total = summed[-1][0]
    grads = dict(zip(SMALL, summed[:-1]))
    cw = grads["conv_w"].reshape(DEPTH, CONV_WIDTH, N_DEV, D_RNN // N_DEV)
    grads["conv_w"] = lax.dynamic_index_in_dim(cw, dev, axis=2, keepdims=False)

    delta, new_m, new_v = {}, {}, {}
    for name in BIG:
        w, m, v = given[name], given["m_" + name], given["v_" + name]
        mine = [parts[l][name] for l in range(DEPTH)]
        theirs = [from_chips[l, name] for l in range(DEPTH)]
        if name == "w_up":
            sums = [_sum_chips(mine[l], theirs[l], chip, name=f"sum_chips_{name}_{l}").T for l in range(DEPTH)]
            out = _adamw_layers(w, sums, m, v, tr=TR, name=f"adamw_{name}")
        else:
            out = _adamw_reduced(
                rows_first(name, w), mine, theirs, rows_first(name, m), rows_first(name, v), chip, tr=TR, name=f"adamw_{name}"
            )
            out = [rows_first(name, a) for a in out]
        grads[name], delta[name], new_m[name], new_v[name] = out
    two_d = lambda a: a.reshape(1, -1) if a.ndim == 1 else a
    groups = [tuple(two_d(a) for a in (given[n], grads[n], given["m_" + n], given["v_" + n])) for n in SMALL]
    for n, (d, m2, v2) in zip(SMALL, _adamw_small(groups, name="adamw_small")):
        shape = given[n].shape
        delta[n], new_m[n], new_v[n] = d.reshape(shape), m2.reshape(shape), v2.reshape(shape)

    return (
        loss_total, dx[None],
        *[grads[n] for n in WEIGHTS], *[delta[n] for n in WEIGHTS], *[new_m[n] for n in WEIGHTS], *[new_v[n] for n in WEIGHTS],
    )


def kernel(x, norm_mix_g, w_in, conv_w, conv_b, lru_w_a, lru_b_a, lru_w_x, lru_b_x, lru_lambda, sgu_ln_g, sgu_ln_b, sgu_w_s, sgu_b_s, w_branch_a, w_branch_b, w_out, norm_ffn_g, w_up, w_down, final_norm_g, loss_target, m_norm_mix_g, m_w_in, m_conv_w, m_conv_b, m_lru_w_a, m_lru_b_a, m_lru_w_x, m_lru_b_x, m_lru_lambda, m_sgu_ln_g, m_sgu_ln_b, m_sgu_w_s, m_sgu_b_s, m_w_branch_a, m_w_branch_b, m_w_out, m_norm_ffn_g, m_w_up, m_w_down, m_final_norm_g, v_norm_mix_g, v_w_in, v_conv_w, v_conv_b, v_lru_w_a, v_lru_b_a, v_lru_w_x, v_lru_b_x, v_lru_lambda, v_sgu_ln_g, v_sgu_ln_b, v_sgu_w_s, v_sgu_b_s, v_w_branch_a, v_w_branch_b, v_w_out, v_norm_ffn_g, v_w_up, v_w_down, v_final_norm_g):
    return _step(dict(locals()))
```

```python
import jax
import jax.numpy as jnp
from jax import lax
from jax.experimental import pallas as pl
from jax.experimental.pallas import tpu as pltpu

F32 = jnp.float32
BF16 = jnp.bfloat16
SDS = jax.ShapeDtypeStruct
MESH = pl.DeviceIdType.MESH

D = 1024
D_RNN = 1280
D_SGU = 1024
D_IN = 2 * D_RNN + 2 * D_SGU + 2 * D
DEPTH = 2
RNN_HEADS = 20
HEAD_DIM = 64
CONV_WIDTH = 4
LRU_C = 8.0
SGU_GROUPS = 8
SGU_BLOCK = 128
CHUNK = 64
EPS = 1e-6
N_DEV = 8

ADAM_LR = 0.001
ADAM_B1 = 0.9
ADAM_B2 = 0.999
ADAM_EPS = 1e-08
ADAM_WD = 0.01
ADAM_STEP = 10

LANES = 128
SUBLANES = 8
VMEM_LIMIT_BYTES = 56 * 1024 * 1024

N_RNN_TILES = D_RNN // LANES
RNN_TILES_PER_STEP = 5
U_BLK512 = (2 * D_RNN) // 512
V_BLK512 = (2 * D_RNN + D_SGU) // 512
GA_BLK512 = (2 * D_RNN + 2 * D_SGU) // 512
GB_BLK512 = (2 * D_RNN + 2 * D_SGU + D) // 512

SMALL_ROWS_PER_DEV = 80
SMALL_ROWS = N_DEV * SMALL_ROWS_PER_DEV


def _params(*sem):
    return pltpu.CompilerParams(dimension_semantics=sem, vmem_limit_bytes=VMEM_LIMIT_BYTES)


def _sigmoid(x):
    return 0.5 + 0.5 * jnp.tanh(0.5 * x)


_GELU_C = 0.7978845608028654
_GELU_K = 0.044715


def _gelu(x):
    t = jnp.tanh(_GELU_C * (x + _GELU_K * x * x * x))
    return 0.5 * x * (1.0 + t)


def _gelu_and_grad(x):
    t = jnp.tanh(_GELU_C * (x + _GELU_K * x * x * x))
    val = 0.5 * x * (1.0 + t)
    grad = 0.5 * (1.0 + t) + 0.5 * x * (1.0 - t * t) * _GELU_C * (1.0 + 3.0 * _GELU_K * x * x)
    return val, grad


def _one_minus_square(log_a, a):
    return -jnp.tanh(log_a) * (1.0 + a * a)


def _dot(a, b):
    return jnp.dot(a, b, preferred_element_type=F32)


def _dot_nt(a, b):
    return lax.dot_general(a, b, (((1,), (1,)), ((), ())), preferred_element_type=F32)


def _dot_tn(a, b):
    return lax.dot_general(a, b, (((0,), (0,)), ((), ())), preferred_element_type=F32)


def _norm_matmul_nt(x, g, w, *, tm, tn, name, comm=None):
    s, n = x.shape[0], w.shape[0]
    tm, tn = min(tm, s), min(tn, n)

    def body(x_ref, g_ref, w_ref, o_ref, h_ref):
        @pl.when(pl.program_id(1) == 0)
        def _():
            xv = x_ref[...]
            r = lax.rsqrt(jnp.mean(xv * xv, axis=-1, keepdims=True) + EPS)
            h_ref[...] = (xv * r * g_ref[...]).astype(BF16)

        o_ref[...] = _dot_nt(h_ref[...], w_ref[...]).astype(o_ref.dtype)

    return _call(
        body,
        (x, g, w),
        name=name,
        grid=(s // tm, n // tn),
        in_specs=[
            pl.BlockSpec((tm, D), lambda i, j: (i, 0)),
            pl.BlockSpec((1, D), lambda i, j: (0, 0)),
            pl.BlockSpec((tn, D), lambda i, j: (j, 0)),
        ],
        out_specs=[pl.BlockSpec((tm, tn), lambda i, j: (i, j)), pl.BlockSpec((tm, D), lambda i, j: (i, 0))],
        out_shape=[SDS((s, n), BF16), SDS((s, D), BF16)],
        semantics=("parallel", "arbitrary"),
        comm=comm,
    )


def _matmul_nn_res(a, w, res, *, relu2, tm, name, comm=None):
    s, k = a.shape
    tm = min(tm, s)

    def body(a_ref, w_ref, r_ref, o_ref):
        av = a_ref[...]
        if relu2:
            t = jnp.maximum(av.astype(F32), 0.0)
            av = (t * t).astype(BF16)
        o_ref[...] = r_ref[...] + _dot(av, w_ref[...])

    return _call(
        body,
        (a, w, res),
        name=name,
        grid=(s // tm,),
        in_specs=[
            pl.BlockSpec((tm, k), lambda i: (i, 0)),
            pl.BlockSpec((k, D), lambda i: (0, 0)),
            pl.BlockSpec((tm, D), lambda i: (i, 0)),
        ],
        out_specs=pl.BlockSpec((tm, D), lambda i: (i, 0)),
        out_shape=SDS((s, D), F32),
        semantics=("parallel",),
        comm=comm,
    )


def _matmul_nt_drelu2(a, w, pre, *, tm, tn, name):
    s, n = a.shape[0], w.shape[0]
    tm, tn = min(tm, s), min(tn, n)

    def body(a_ref, w_ref, p_ref, o_ref):
        d = _dot_nt(a_ref[...], w_ref[...])
        o_ref[...] = (d * (2.0 * jnp.maximum(p_ref[...].astype(F32), 0.0))).astype(o_ref.dtype)

    return pl.pallas_call(
        body,
        name=name,
        grid=(s // tm, n // tn),
        in_specs=[
            pl.BlockSpec((tm, D), lambda i, j: (i, 0)),
            pl.BlockSpec((tn, D), lambda i, j: (j, 0)),
            pl.BlockSpec((tm, tn), lambda i, j: (i, j)),
        ],
        out_specs=pl.BlockSpec((tm, tn), lambda i, j: (i, j)),
        out_shape=SDS((s, n), BF16),
        compiler_params=_params("parallel", "arbitrary"),
    )(a, w, pre)


def _matmul_tn(a_list, b, *, relu2, tka, name, comm=None):
    s = b.shape[0]
    n = len(a_list)
    nblk = [a.shape[1] // tka for a in a_list]
    starts = [sum(nblk[:p]) for p in range(n)]

    def body(*refs):
        a_refs, b_ref, o_ref = refs[:n], refs[n], refs[n + 1]
        i = pl.program_id(0)
        for p in range(n):

            @pl.when((i >= starts[p]) & (i < starts[p] + nblk[p]))
            def _(p=p):
                av = a_refs[p][...]
                if relu2:
                    t = jnp.maximum(av.astype(F32), 0.0)
                    av = (t * t).astype(BF16)
                o_ref[...] = _dot_tn(av, b_ref[...]).astype(o_ref.dtype)

    def piece_spec(p):
        return pl.BlockSpec((s, tka), lambda i: (0, jnp.clip(i - starts[p], 0, nblk[p] - 1)))

    return _call(
        body,
        (*a_list, b),
        name=name,
        grid=(sum(nblk),),
        in_specs=[piece_spec(p) for p in range(n)] + [pl.BlockSpec((s, D), lambda i: (0, 0))],
        out_specs=pl.BlockSpec((tka, D), lambda i: (i, 0)),
        out_shape=SDS((sum(nblk) * tka, D), BF16),
        semantics=("parallel",),
        comm=comm,
    )


def _matmuls_tn(pairs, *, ts, name):
    s = pairs[0][0].shape[0]
    ts = min(ts, s)
    n = len(pairs)
    steps = s // ts

    def body(*refs):
        ins, outs, accs = refs[: 2 * n], refs[2 * n : 3 * n], refs[3 * n :]
        for p in range(n):
            part = _dot_tn(ins[2 * p][...], ins[2 * p + 1][...])

            @pl.when(pl.program_id(0) == 0)
            def _(p=p, part=part):
                accs[p][...] = part

            @pl.when(pl.program_id(0) > 0)
            def _(p=p, part=part):
                accs[p][...] += part

        @pl.when(pl.program_id(0) == steps - 1)
        def _():
            for p in range(n):
                outs[p][...] = accs[p][...].astype(BF16)

    widths = [a.shape[1] for a, _ in pairs]
    in_specs = []
    for wd in widths:
        in_specs += [pl.BlockSpec((ts, wd), lambda i: (i, 0)), pl.BlockSpec((ts, D), lambda i: (i, 0))]
    return pl.pallas_call(
        body,
        name=name,
        grid=(steps,),
        in_specs=in_specs,
        out_specs=[pl.BlockSpec((wd, D), lambda i: (0, 0)) for wd in widths],
        out_shape=[SDS((wd, D), BF16) for wd in widths],
        scratch_shapes=[pltpu.VMEM((wd, D), F32) for wd in widths],
        compiler_params=_params("arbitrary"),
    )(*[x for pair in pairs for x in pair])


def _matmul_nn_rmsnorm_bwd(a_list, w, x, g, res, *, tm, name, comm=None):
    s = x.shape[0]
    tm = min(tm, s)
    n = len(a_list)
    widths = [a.shape[1] for a in a_list]
    offs = [sum(widths[:p]) for p in range(n)]
    k = sum(widths)

    def body(*refs):
        a_refs = refs[:n]
        w_ref, x_ref, g_ref, r_ref, dx_ref, dxb_ref, dg_ref = refs[n:]

        @pl.when(pl.program_id(0) == 0)
        def _():
            dg_ref[...] = jnp.zeros_like(dg_ref)

        dh = _dot(a_refs[0][...], w_ref[0 : widths[0], :])
        for p in range(1, n):
            dh += _dot(a_refs[p][...], w_ref[offs[p] : offs[p] + widths[p], :])
        xv = x_ref[...]
        r = lax.rsqrt(jnp.mean(xv * xv, axis=-1, keepdims=True) + EPS)
        xhat = xv * r
        dxh = dh * g_ref[...]
        dx = r_ref[...] + r * (dxh - xhat * jnp.mean(dxh * xhat, axis=-1, keepdims=True))
        dx_ref[...] = dx
        dxb_ref[...] = dx.astype(BF16)
        dg_ref[...] += jnp.sum(dh * xhat, axis=0, keepdims=True)

    act = pl.BlockSpec((tm, D), lambda i: (i, 0))
    vec = pl.BlockSpec((1, D), lambda i: (0, 0))
    return _call(
        body,
        (*a_list, w, x, g, res),
        name=name,
        grid=(s // tm,),
        in_specs=[pl.BlockSpec((tm, wd), lambda i: (i, 0)) for wd in widths]
        + [pl.BlockSpec((k, D), lambda i: (0, 0), pipeline_mode=pl.Buffered(1)), act, vec, act],
        out_specs=[act, act, vec],
        out_shape=[SDS((s, D), F32), SDS((s, D), BF16), SDS((1, D), F32)],
        semantics=("arbitrary",),
        comm=comm,
    )


def _rows_after(ext, k, n):
    return pltpu.roll(ext, n + SUBLANES - k, 0)[:n, :]


def _scan_forward(a, b, n):
    row = lax.broadcasted_iota(jnp.int32, a.shape, 0)
    d = 1
    while d < n:
        if d < SUBLANES:
            m = row >= d
            a_s = jnp.where(m, pltpu.roll(a, d, 0), 1.0)
            b_s = jnp.where(m, pltpu.roll(b, d, 0), 0.0)
            b = a * b_s + b
            a = a * a_s
        else:
            b = jnp.concatenate([b[:d], a[d:] * b[: n - d] + b[d:]], axis=0)
            a = jnp.concatenate([a[:d], a[d:] * a[: n - d]], axis=0)
        d *= 2
    return a, b


def _scan_backward(a, b, n):
    row = lax.broadcasted_iota(jnp.int32, a.shape, 0)
    d = 1
    while d < n:
        if d < SUBLANES:
            m = row < n - d
            a_s = jnp.where(m, pltpu.roll(a, n - d, 0), 1.0)
            b_s = jnp.where(m, pltpu.roll(b, n - d, 0), 0.0)
            b = a * b_s + b
            a = a * a_s
        else:
            b = jnp.concatenate([a[: n - d] * b[d:] + b[: n - d], b[n - d :]], axis=0)
            a = jnp.concatenate([a[: n - d] * a[d:], a[n - d :]], axis=0)
        d *= 2
    return b


def _repeat_matrix(n):
    groups = n // SUBLANES
    return (jnp.arange(n)[:, None] // SUBLANES == jnp.arange(3 * groups)[None, :] % groups).astype(BF16)


def _scan_rows(a, b, n, repeat_ref, a_scr, b_scr, reverse):
    groups = n // SUBLANES
    a3 = a.reshape(groups, SUBLANES, LANES)
    b3 = b.reshape(groups, SUBLANES, LANES)
    sub = lax.broadcasted_iota(jnp.int32, a3.shape, 1)
    for d in (1, 2, 4):
        m = (sub < SUBLANES - d) if reverse else (sub >= d)
        shift = SUBLANES - d if reverse else d
        a_s = jnp.where(m, pltpu.roll(a3, shift, 1), 1.0)
        b_s = jnp.where(m, pltpu.roll(b3, shift, 1), 0.0)
        b3 = a3 * b_s + b3
        a3 = a3 * a_s
    a_scr[...] = a3.reshape(n, LANES)
    b_scr[...] = b3.reshape(n, LANES)
    edge = 0 if reverse else SUBLANES - 1
    a_tot = a_scr[pl.ds(edge, groups, stride=SUBLANES), :]
    b_tot = b_scr[pl.ds(edge, groups, stride=SUBLANES), :]
    row = lax.broadcasted_iota(jnp.int32, a_tot.shape, 0)
    if reverse:
        through = _scan_backward(a_tot, b_tot, groups)
        entering = jnp.where(row < groups - 1, pltpu.roll(through, groups - 1, 0), 0.0)
    else:
        _, through = _scan_forward(a_tot, b_tot, groups)
        entering = jnp.where(row >= 1, pltpu.roll(through, 1, 0), 0.0)
    hi = entering.astype(BF16)
    rest = entering - hi.astype(F32)
    mid = rest.astype(BF16)
    lo = (rest - mid.astype(F32)).astype(BF16)
    repeated = _dot(repeat_ref[...], jnp.concatenate([hi, mid, lo], axis=0))
    return b_scr[...] + a_scr[...] * repeated


def _softplus_neg(lam):
    z = -lam
    return jnp.maximum(z, 0.0) + jnp.log1p(jnp.exp(-jnp.abs(z)))


def _conv_and_gates(xc, xprev, cw_ref, cb_ref, wa_ref, ba_ref, wx_ref, bx_ref, lam_ref, ext_scr):
    n = xc.shape[0]
    ext_scr[:SUBLANES, :] = xprev
    ext_scr[SUBLANES:, :] = xc
    x1, x2, x3 = (ext_scr[pl.ds(SUBLANES - k, n), :] for k in (1, 2, 3))
    xr = cb_ref[...] + x3 * cw_ref[0:1, :] + x2 * cw_ref[1:2, :] + x1 * cw_ref[2:3, :] + xc * cw_ref[3:4, :]
    xrb = xr.astype(BF16)
    r = _sigmoid(_dot(xrb, wa_ref[...]) + ba_ref[...])
    i = _sigmoid(_dot(xrb, wx_ref[...]) + bx_ref[...])
    sp = _softplus_neg(lam_ref[...])
    log_a = (-LRU_C * r) * sp
    a = jnp.exp(log_a)
    return xr, (x1, x2, x3), r, i, a, _one_minus_square(log_a, a)


def _branch_a_fwd(proj, cw, cb, wa2, ba, wx2, bx, lam, *, tc, name, comm=None):
    s = proj.shape[0]
    tc = min(tc, s)

    def body(x_ref, g_ref, cw_ref, cb_ref, wa_ref, ba_ref, wx_ref, bx_ref, lam_ref, rep_ref, h_ref, y_ref,
             xprev, hlast, a_scr, b_scr, ext_scr):
        @pl.when(pl.program_id(1) == 0)
        def _():
            xprev[...] = jnp.zeros_like(xprev)
            hlast[...] = jnp.zeros_like(hlast)

        for t in range(RNN_TILES_PER_STEP):
            cols = lambda ref: ref.at[:, pl.ds(t * LANES, LANES)]
            one_tile(
                cols(x_ref), cols(g_ref), cols(cw_ref), cols(cb_ref), wa_ref.at[t], cols(ba_ref), wx_ref.at[t], cols(bx_ref),
                cols(lam_ref), rep_ref, cols(h_ref), cols(y_ref), cols(xprev), cols(hlast), a_scr.at[t], b_scr.at[t],
                ext_scr.at[t],
            )

    def one_tile(x_ref, g_ref, cw_ref, cb_ref, wa_ref, ba_ref, wx_ref, bx_ref, lam_ref, rep_ref, h_ref, y_ref,
                 xprev, hlast, a_scr, b_scr, ext_scr):
        xc = x_ref[...].astype(F32)
        xr, _, r, i, a, om = _conv_and_gates(
            xc, xprev[...], cw_ref, cb_ref, wa_ref, ba_ref, wx_ref, bx_ref, lam_ref, ext_scr
        )
        xprev[...] = xc[tc - SUBLANES :, :]
        u = jnp.sqrt(om) * (i * xr)
        row8 = lax.broadcasted_iota(jnp.int32, (SUBLANES, LANES), 0)
        first = u[:SUBLANES] + jnp.where(row8 == 0, a[:SUBLANES] * hlast[SUBLANES - 1 : SUBLANES, :], 0.0)
        h = _scan_rows(a, jnp.concatenate([first, u[SUBLANES:]], axis=0), tc, rep_ref, a_scr, b_scr, reverse=False)
        hlast[...] = h[tc - SUBLANES :, :]
        h_ref[...] = h
        y_ref[...] = (h * _gelu(g_ref[...].astype(F32))).astype(BF16)

    wide = RNN_TILES_PER_STEP * LANES
    tile = lambda j, c: (0, j)
    vec = pl.BlockSpec((1, wide), tile)
    mats = pl.BlockSpec((RNN_TILES_PER_STEP, LANES, LANES), lambda j, c: (j, 0, 0))
    repeat = _repeat_matrix(tc)
    return _call(
        body,
        (proj, proj, cw, cb, wa2, ba, wx2, bx, lam, repeat),
        name=name,
        grid=(N_RNN_TILES // RNN_TILES_PER_STEP, s // tc),
        in_specs=[
            pl.BlockSpec((tc, wide), lambda j, c: (c, j)),
            pl.BlockSpec((tc, wide), lambda j, c: (c, D_RNN // wide + j)),
            pl.BlockSpec((CONV_WIDTH, wide), tile),
            vec,
            mats,
            vec,
            mats,
            vec,
            vec,
            pl.BlockSpec(repeat.shape, lambda j, c: (0, 0)),
        ],
        out_specs=[pl.BlockSpec((tc, wide), lambda j, c: (c, j)), pl.BlockSpec((tc, wide), lambda j, c: (c, j))],
        out_shape=[SDS((s, D_RNN), F32), SDS((s, D_RNN), BF16)],
        scratch_shapes=[pltpu.VMEM((SUBLANES, wide), F32)] * 2
        + [pltpu.VMEM((RNN_TILES_PER_STEP, tc, LANES), F32)] * 2
        + [pltpu.VMEM((RNN_TILES_PER_STEP, tc + SUBLANES, LANES), F32)],
        semantics=("parallel", "arbitrary"),
        comm=comm,
    )


def _branch_a_bwd(dy, proj, h, cw, cb, wa2, ba, wx2, bx, lam, wa2t, wx2t, *, tc, name, comm=None):
    s = proj.shape[0]
    tc = min(tc, s)
    nc = s // tc
    halo16 = tc // 16
    halo8 = tc // SUBLANES

    def body(dy_ref, x_ref, xh_ref, g_ref, h_ref, hh_ref, cw_ref, cb_ref, wa_ref, ba_ref, wx_ref, bx_ref, lam_ref,
             wat_ref, wxt_ref, rep_ref, dx_ref, dg_ref, dcw_ref, dcb_ref, dba_ref, dbx_ref, dlam_ref, dwa_ref, dwx_ref,
             carry, dxr_next, a_scr, b_scr, ext_scr):
        cc = pl.program_id(1)
        ct = nc - 1 - cc

        @pl.when(cc == 0)
        def _():
            carry[...] = jnp.zeros_like(carry)
            dxr_next[...] = jnp.zeros_like(dxr_next)
            for ref in (dcw_ref, dcb_ref, dba_ref, dbx_ref, dlam_ref, dwa_ref, dwx_ref):
                ref[...] = jnp.zeros_like(ref)

        for t in range(RNN_TILES_PER_STEP):
            cols = lambda ref: ref.at[:, pl.ds(t * LANES, LANES)]
            one_tile(
                ct, cols(dy_ref), cols(x_ref), cols(xh_ref), cols(g_ref), cols(h_ref), cols(hh_ref), cols(cw_ref), cols(cb_ref),
                wa_ref.at[t], cols(ba_ref), wx_ref.at[t], cols(bx_ref), cols(lam_ref), wat_ref.at[t], wxt_ref.at[t], rep_ref,
                cols(dx_ref), cols(dg_ref), cols(dcw_ref), cols(dcb_ref), cols(dba_ref), cols(dbx_ref), cols(dlam_ref),
                dwa_ref.at[t], dwx_ref.at[t], cols(carry), cols(dxr_next), a_scr.at[t], b_scr.at[t], ext_scr.at[t],
            )

    def one_tile(ct, dy_ref, x_ref, xh_ref, g_ref, h_ref, hh_ref, cw_ref, cb_ref, wa_ref, ba_ref, wx_ref, bx_ref, lam_ref,
                 wat_ref, wxt_ref, rep_ref, dx_ref, dg_ref, dcw_ref, dcb_ref, dba_ref, dbx_ref, dlam_ref, dwa_ref, dwx_ref,
                 carry, dxr_next, a_scr, b_scr, ext_scr):
        xc = x_ref[...].astype(F32)
        xprev = jnp.where(ct > 0, xh_ref[SUBLANES:, :].astype(F32), 0.0)
        xr, (x1, x2, x3), r, i, a, om = _conv_and_gates(
            xc, xprev, cw_ref, cb_ref, wa_ref, ba_ref, wx_ref, bx_ref, lam_ref, ext_scr
        )
        inv_norm = lax.rsqrt(om)
        norm = om * inv_norm
        row = lax.broadcasted_iota(jnp.int32, xc.shape, 0)

        hv = h_ref[...]
        ge, ge_grad = _gelu_and_grad(g_ref[...].astype(F32))
        dyv = dy_ref[...].astype(F32)
        dg_ref[...] = (dyv * hv * ge_grad).astype(dg_ref.dtype)
        dh = dyv * ge

        b = dh + jnp.where(row == tc - 1, carry[0:1, :], 0.0)
        a_next = jnp.where(row < tc - 1, pltpu.roll(a, tc - 1, 0), 0.0)
        gadj = _scan_rows(a_next, b, tc, rep_ref, a_scr, b_scr, reverse=True)
        carry[...] = (a * gadj)[:SUBLANES, :]

        hprev_first = jnp.where(ct > 0, hh_ref[SUBLANES - 1 : SUBLANES, :], 0.0)
        hprev = jnp.where(row >= 1, pltpu.roll(hv, 1, 0), hprev_first)
        da = gadj * hprev
        ix = i * xr
        dnorm = gadj * ix
        di = gadj * norm * xr
        dlog_a = da * a - dnorm * (1.0 - om) * inv_norm
        sp = _softplus_neg(lam_ref[...])
        dr = dlog_a * (-LRU_C * sp)
        dsp = jnp.sum(dlog_a * (-LRU_C * r), axis=0, keepdims=True)
        dlam_ref[...] += dsp * (-_sigmoid(-lam_ref[...]))
        dza = dr * r * (1.0 - r)
        dzx = di * i * (1.0 - i)
        dzab, dzxb = dza.astype(BF16), dzx.astype(BF16)
        dxr = gadj * norm * i + _dot(dzab, wat_ref[...]) + _dot(dzxb, wxt_ref[...])
        xrb = xr.astype(BF16)
        dwa_ref[...] += _dot_tn(xrb, dzab)
        dwx_ref[...] += _dot_tn(xrb, dzxb)
        dba_ref[...] += jnp.sum(dza, axis=0, keepdims=True)
        dbx_ref[...] += jnp.sum(dzx, axis=0, keepdims=True)

        ext = jnp.concatenate([dxr, dxr_next[...]], axis=0)
        dx = (
            dxr * cw_ref[3:4, :]
            + _rows_after(ext, 1, tc) * cw_ref[2:3, :]
            + _rows_after(ext, 2, tc) * cw_ref[1:2, :]
            + _rows_after(ext, 3, tc) * cw_ref[0:1, :]
        )
        dxr_next[...] = dxr[:SUBLANES, :]
        dx_ref[...] = dx.astype(dx_ref.dtype)
        dcb_ref[...] += jnp.sum(dxr, axis=0, keepdims=True)
        dcw_ref[3:4, :] += jnp.sum(dxr * xc, axis=0, keepdims=True)
        dcw_ref[2:3, :] += jnp.sum(dxr * x1, axis=0, keepdims=True)
        dcw_ref[1:2, :] += jnp.sum(dxr * x2, axis=0, keepdims=True)
        dcw_ref[0:1, :] += jnp.sum(dxr * x3, axis=0, keepdims=True)

    wide = RNN_TILES_PER_STEP * LANES
    tile = lambda j, c: (0, j)
    mat = lambda j, c: (j, 0, 0)
    cur = lambda j, c: (nc - 1 - c, j)
    vec = pl.BlockSpec((1, wide), tile)
    matspec = pl.BlockSpec((RNN_TILES_PER_STEP, LANES, LANES), mat)
    repeat = _repeat_matrix(tc)
    return _call(
        body,
        (dy, proj, proj, proj, h, h, cw, cb, wa2, ba, wx2, bx, lam, wa2t, wx2t, repeat),
        name=name,
        grid=(N_RNN_TILES // RNN_TILES_PER_STEP, nc),
        in_specs=[
            pl.BlockSpec((tc, wide), cur),
            pl.BlockSpec((tc, wide), cur),
            pl.BlockSpec((16, wide), lambda j, c: (jnp.maximum((nc - 1 - c) * halo16 - 1, 0), j)),
            pl.BlockSpec((tc, wide), lambda j, c: (nc - 1 - c, D_RNN // wide + j)),
            pl.BlockSpec((tc, wide), cur),
            pl.BlockSpec((SUBLANES, wide), lambda j, c: (jnp.maximum((nc - 1 - c) * halo8 - 1, 0), j)),
            pl.BlockSpec((CONV_WIDTH, wide), tile),
            vec,
            matspec,
            vec,
            matspec,
            vec,
            vec,
            matspec,
            matspec,
            pl.BlockSpec(repeat.shape, lambda j, c: (0, 0)),
        ],
        out_specs=[
            pl.BlockSpec((tc, wide), cur),
            pl.BlockSpec((tc, wide), cur),
            pl.BlockSpec((CONV_WIDTH, wide), tile),
            vec,
            vec,
            vec,
            vec,
            matspec,
            matspec,
        ],
        out_shape=[
            SDS((s, D_RNN), BF16),
            SDS((s, D_RNN), BF16),
            SDS((CONV_WIDTH, D_RNN), F32),
            SDS((1, D_RNN), F32),
            SDS((1, D_RNN), F32),
            SDS((1, D_RNN), F32),
            SDS((1, D_RNN), F32),
            SDS((N_RNN_TILES, LANES, LANES), F32),
            SDS((N_RNN_TILES, LANES, LANES), F32),
        ],
        scratch_shapes=[pltpu.VMEM((SUBLANES, wide), F32)] * 2
        + [pltpu.VMEM((RNN_TILES_PER_STEP, tc, LANES), F32)] * 2
        + [pltpu.VMEM((RNN_TILES_PER_STEP, tc + SUBLANES, LANES), F32)],
        semantics=("parallel", "arbitrary"),
        comm=comm,
    )


def _sgu_specs(tb):
    half = lambda blk: pl.BlockSpec((tb, 512), lambda n: (n, blk))
    return [half(U_BLK512), half(U_BLK512 + 1), half(V_BLK512), half(V_BLK512 + 1)]


def _sgu_normed(v, lng_ref, lnb_ref):
    gv, gv_grad = _gelu_and_grad(v)
    mu = jnp.mean(gv, axis=-1, keepdims=True)
    xc = gv - mu
    rs = lax.rsqrt(jnp.mean(xc * xc, axis=-1, keepdims=True) + EPS)
    xhat = xc * rs
    return xhat * lng_ref[...] + lnb_ref[...], xhat, rs, gv_grad


def _sgu_fwd(proj, lng, lnb, wm, bias, *, tb, name, comm=None):
    s = proj.shape[0]
    tb = min(tb, s)

    def body(u0_ref, u1_ref, v0_ref, v1_ref, lng_ref, lnb_ref, wm_ref, bias_ref, y_ref):
        u = jnp.concatenate([u0_ref[...], u1_ref[...]], axis=1).astype(F32)
        v = jnp.concatenate([v0_ref[...], v1_ref[...]], axis=1).astype(F32)
        gu = _gelu(u)
        vn, _, _, _ = _sgu_normed(v, lng_ref, lnb_ref)
        vnb = vn.astype(BF16)
        for blk in range(tb // SGU_BLOCK):
            rows = slice(blk * SGU_BLOCK, (blk + 1) * SGU_BLOCK)
            for g in range(SGU_GROUPS):
                cols = slice(g * LANES, (g + 1) * LANES)
                mixed = _dot(wm_ref[g], vnb[rows, cols]) + bias_ref[g]
                y_ref[rows, cols] = (gu[rows, cols] * mixed).astype(BF16)

    const2 = lambda n: (0, 0)
    const3 = lambda n: (0, 0, 0)
    return _call(
        body,
        (proj, proj, proj, proj, lng, lnb, wm, bias),
        name=name,
        grid=(s // tb,),
        in_specs=_sgu_specs(tb)
        + [
            pl.BlockSpec((1, D_SGU), const2),
            pl.BlockSpec((1, D_SGU), const2),
            pl.BlockSpec((SGU_GROUPS, SGU_BLOCK, SGU_BLOCK), const3),
            pl.BlockSpec((SGU_GROUPS, SGU_BLOCK, LANES), const3),
        ],
        out_specs=pl.BlockSpec((tb, D_SGU), lambda n: (n, 0)),
        out_shape=SDS((s, D_SGU), BF16),
        semantics=("parallel",),
        comm=comm,
    )


def _sgu_bwd(dy, proj, lng, lnb, wm, wmt, bias, mask, *, tb, name, comm=None):
    s = proj.shape[0]
    tb = min(tb, s)
    nb = s // tb

    def body(dy_ref, u0_ref, u1_ref, v0_ref, v1_ref, lng_ref, lnb_ref, wm_ref, wmt_ref, bias_ref, mask_ref,
             du_ref, dv_ref, dws_ref, dbs_ref, dlng_ref, dlnb_ref, dvn_scr, dbs_acc):
        n = pl.program_id(0)

        @pl.when(n == 0)
        def _():
            dbs_acc[...] = jnp.zeros_like(dbs_acc)
            for ref in (dws_ref, dlng_ref, dlnb_ref):
                ref[...] = jnp.zeros_like(ref)

        u = jnp.concatenate([u0_ref[...], u1_ref[...]], axis=1).astype(F32)
        v = jnp.concatenate([v0_ref[...], v1_ref[...]], axis=1).astype(F32)
        gu, gu_grad = _gelu_and_grad(u)
        vn, xhat, rs, gv_grad = _sgu_normed(v, lng_ref, lnb_ref)
        vnb = vn.astype(BF16)
        dyv = dy_ref[...].astype(F32)
        for blk in range(tb // SGU_BLOCK):
            rows = slice(blk * SGU_BLOCK, (blk + 1) * SGU_BLOCK)
            for g in range(SGU_GROUPS):
                cols = slice(g * LANES, (g + 1) * LANES)
                vt = vnb[rows, cols]
                mixed = _dot(wm_ref[g], vt) + bias_ref[g]
                dyt = dyv[rows, cols]
                du_ref[rows, cols] = (dyt * mixed * gu_grad[rows, cols]).astype(BF16)
                dmix = dyt * gu[rows, cols]
                dmixb = dmix.astype(BF16)
                dvn_scr[rows, cols] = _dot(wmt_ref[g], dmixb)
                dws_ref[g] += _dot_nt(dmixb, vt) * mask_ref[...]
                dbs_acc[g] += dmix
        dvn = dvn_scr[...]
        dlng_ref[...] += jnp.sum(dvn * xhat, axis=0, keepdims=True)
        dlnb_ref[...] += jnp.sum(dvn, axis=0, keepdims=True)
        dxh = dvn * lng_ref[...]
        dgv = rs * (
            dxh - jnp.mean(dxh, axis=-1, keepdims=True) - xhat * jnp.mean(dxh * xhat, axis=-1, keepdims=True)
        )
        dv_ref[...] = (dgv * gv_grad).astype(BF16)

        @pl.when(n == nb - 1)
        def _():
            for g in range(SGU_GROUPS):
                dbs_ref[g] = jnp.broadcast_to(jnp.sum(dbs_acc[g], axis=-1, keepdims=True), (SGU_BLOCK, LANES))

    const2 = lambda n: (0, 0)
    const3 = lambda n: (0, 0, 0)
    gmat = pl.BlockSpec((SGU_GROUPS, SGU_BLOCK, SGU_BLOCK), const3)
    vec = pl.BlockSpec((1, D_SGU), const2)
    act = pl.BlockSpec((tb, D_SGU), lambda n: (n, 0))
    return _call(
        body,
        (dy, proj, proj, proj, proj, lng, lnb, wm, wmt, bias, mask),
        name=name,
        grid=(nb,),
        in_specs=[act] + _sgu_specs(tb) + [vec, vec, gmat, gmat, gmat, pl.BlockSpec((SGU_BLOCK, SGU_BLOCK), const2)],
        out_specs=[act, act, gmat, gmat, vec, vec],
        out_shape=[
            SDS((s, D_SGU), BF16),
            SDS((s, D_SGU), BF16),
            SDS((SGU_GROUPS, SGU_BLOCK, SGU_BLOCK), F32),
            SDS((SGU_GROUPS, SGU_BLOCK, LANES), F32),
            SDS((1, D_SGU), F32),
            SDS((1, D_SGU), F32),
        ],
        scratch_shapes=[pltpu.VMEM((tb, D_SGU), F32), pltpu.VMEM((SGU_GROUPS, SGU_BLOCK, LANES), F32)],
        semantics=("arbitrary",),
        comm=comm,
    )


def _gate_specs(tm):
    half = lambda blk: pl.BlockSpec((tm, 512), lambda i: (i, blk))
    return [half(GA_BLK512), half(GA_BLK512 + 1), half(GB_BLK512), half(GB_BLK512 + 1)]


def _merge_fwd(ya_pre, yb_pre, proj, x, w_ba, w_bb, w_out, *, tm, name, comm=None):
    s = x.shape[0]
    tm = min(tm, s)

    def body(ya_ref, yb_ref, a0, a1, b0, b1, x_ref, wa_ref, wb_ref, wo_ref, x1_ref, yao_ref, ybo_ref):
        ya = _dot(ya_ref[...], wa_ref[...])
        yb = _dot(yb_ref[...], wb_ref[...])
        sa = _sigmoid(jnp.concatenate([a0[...], a1[...]], axis=1).astype(F32))
        sb = _sigmoid(jnp.concatenate([b0[...], b1[...]], axis=1).astype(F32))
        merged = sa * ya + sb * yb
        x1_ref[...] = x_ref[...] + _dot(merged.astype(BF16), wo_ref[...])
        yao_ref[...] = ya.astype(BF16)
        ybo_ref[...] = yb.astype(BF16)

    whole = lambda r: pl.BlockSpec((r, D), lambda i: (0, 0))
    act = pl.BlockSpec((tm, D), lambda i: (i, 0))
    return _call(
        body,
        (ya_pre, yb_pre, proj, proj, proj, proj, x, w_ba, w_bb, w_out),
        name=name,
        grid=(s // tm,),
        in_specs=[pl.BlockSpec((tm, D_RNN), lambda i: (i, 0)), act] + _gate_specs(tm) + [act, whole(D_RNN), whole(D_SGU), whole(D)],
        out_specs=[act, act, act],
        out_shape=[SDS((s, D), F32), SDS((s, D), BF16), SDS((s, D), BF16)],
        semantics=("parallel",),
        comm=comm,
    )


def _merge_bwd(dx1, ya, yb, proj, w_ba, w_bb, w_out, *, tm, name, comm=None):
    s = dx1.shape[0]
    tm = min(tm, s)

    def body(dx_ref, ya_ref, yb_ref, a0, a1, b0, b1, wa_ref, wb_ref, wo_ref,
             mg_ref, dya_ref, dyb_ref, dga_ref, dgb_ref, dyap_ref, dybp_ref):
        dm = _dot_nt(dx_ref[...], wo_ref[...])
        ya = ya_ref[...].astype(F32)
        yb = yb_ref[...].astype(F32)
        sa = _sigmoid(jnp.concatenate([a0[...], a1[...]], axis=1).astype(F32))
        sb = _sigmoid(jnp.concatenate([b0[...], b1[...]], axis=1).astype(F32))
        mg_ref[...] = (sa * ya + sb * yb).astype(BF16)
        dya = (dm * sa).astype(BF16)
        dyb = (dm * sb).astype(BF16)
        dya_ref[...] = dya
        dyb_ref[...] = dyb
        dga_ref[...] = (dm * ya * sa * (1.0 - sa)).astype(BF16)
        dgb_ref[...] = (dm * yb * sb * (1.0 - sb)).astype(BF16)
        dyap_ref[...] = _dot_nt(dya, wa_ref[...]).astype(BF16)
        dybp_ref[...] = _dot_nt(dyb, wb_ref[...]).astype(BF16)

    whole = lambda r: pl.BlockSpec((r, D), lambda i: (0, 0))
    act = pl.BlockSpec((tm, D), lambda i: (i, 0))
    act_rnn = pl.BlockSpec((tm, D_RNN), lambda i: (i, 0))
    return _call(
        body,
        (dx1, ya, yb, proj, proj, proj, proj, w_ba, w_bb, w_out),
        name=name,
        grid=(s // tm,),
        in_specs=[act, act, act] + _gate_specs(tm) + [whole(D_RNN), whole(D_SGU), whole(D)],
        out_specs=[act, act, act, act, act, act_rnn, act],
        out_shape=[SDS((s, D), BF16)] * 5 + [SDS((s, D_RNN), BF16), SDS((s, D_SGU), BF16)],
        semantics=("parallel",),
        comm=comm,
    )


def _ffn_down_loss(a, w, res, g, target, *, tm, name):
    s, k = a.shape
    tm = min(tm, s)

    def body(a_ref, w_ref, r_ref, g_ref, t_ref, dx_ref, dxb_ref, dg_ref, loss_ref):
        @pl.when(pl.program_id(0) == 0)
        def _():
            dg_ref[...] = jnp.zeros_like(dg_ref)
            loss_ref[...] = jnp.zeros_like(loss_ref)

        t = jnp.maximum(a_ref[...].astype(F32), 0.0)
        xv = r_ref[...] + _dot((t * t).astype(BF16), w_ref[...])
        r = lax.rsqrt(jnp.mean(xv * xv, axis=-1, keepdims=True) + EPS)
        xhat = xv * r
        e = xhat * g_ref[...] - t_ref[...]
        loss_ref[...] += 0.5 * jnp.sum(jnp.mean(e * e, axis=-1, keepdims=True), axis=0, keepdims=True)
        dy = e * (1.0 / D)
        dxh = dy * g_ref[...]
        dx = r * (dxh - xhat * jnp.mean(dxh * xhat, axis=-1, keepdims=True))
        dx_ref[...] = dx
        dxb_ref[...] = dx.astype(BF16)
        dg_ref[...] += jnp.sum(dy * xhat, axis=0, keepdims=True)

    act = pl.BlockSpec((tm, D), lambda i: (i, 0))
    vec = pl.BlockSpec((1, D), lambda i: (0, 0))
    return pl.pallas_call(
        body,
        name=name,
        grid=(s // tm,),
        in_specs=[pl.BlockSpec((tm, k), lambda i: (i, 0)), pl.BlockSpec((k, D), lambda i: (0, 0)), act, vec, act],
        out_specs=[act, act, vec, pl.BlockSpec((SUBLANES, LANES), lambda i: (0, 0))],
        out_shape=[SDS((s, D), F32), SDS((s, D), BF16), SDS((1, D), F32), SDS((SUBLANES, LANES), F32)],
        compiler_params=_params("arbitrary"),
    )(a, w, res, g, target)


def _adamw_math(w, g, m, v):
    m2 = ADAM_B1 * m + (1.0 - ADAM_B1) * g
    v2 = ADAM_B2 * v + (1.0 - ADAM_B2) * (g * g)
    m_hat = m2 / (1.0 - ADAM_B1**ADAM_STEP)
    v_hat = v2 / (1.0 - ADAM_B2**ADAM_STEP)
    delta = -ADAM_LR * (m_hat / (jnp.sqrt(v_hat) + ADAM_EPS) + ADAM_WD * w)
    return delta, m2, v2


def _row_tile(rows, cap):
    return max(t for t in range(SUBLANES, min(cap, rows) + 1, SUBLANES) if rows % t == 0)


def _adamw_layers(w, grads, m, v, *, tr, name):
    depth, r, c = w.shape
    tr = _row_tile(r, tr)

    def body(*refs):
        g_refs = refs[:depth]
        w_ref, m_ref, v_ref, g_out, d_ref, mo_ref, vo_ref = refs[depth:]
        for l in range(depth):

            @pl.when(pl.program_id(0) == l)
            def _(l=l):
                g = g_refs[l][...]
                g_out[...] = g
                d_ref[...], mo_ref[...], vo_ref[...] = _adamw_math(w_ref[...], g, m_ref[...], v_ref[...])

    def of_layer(ll):
        return pl.BlockSpec((tr, c), lambda l, i: (jnp.where(l == ll, i, 0), 0))

    stacked = pl.BlockSpec((None, tr, c), lambda l, i: (l, i, 0))
    return pl.pallas_call(
        body,
        name=name,
        grid=(depth, r // tr),
        in_specs=[of_layer(ll) for ll in range(depth)] + [stacked] * 3,
        out_specs=[stacked] * 4,
        out_shape=[SDS((depth, r, c), F32)] * 4,
        compiler_params=_params("parallel", "parallel"),
    )(*grads, w, m, v)


def _adamw_reduced(w, parts, from_chips, m, v, chip, *, tr, name):
    depth, r, _ = w.shape
    tr = _row_tile(r, tr)

    def body(chip_ref, *refs):
        p_refs, c_refs = refs[:depth], refs[depth : 2 * depth]
        w_ref, m_ref, v_ref, g_out, d_ref, mo_ref, vo_ref = refs[2 * depth :]
        for l in range(depth):

            @pl.when(pl.program_id(0) == l)
            def _(l=l):
                got = c_refs[l]
                g = ((p_refs[l][...].astype(F32) + got[0].astype(F32)) + got[1].astype(F32)) + got[2].astype(F32)
                g_out[...] = g
                d_ref[...], mo_ref[...], vo_ref[...] = _adamw_math(w_ref[...], g, m_ref[...], v_ref[...])

    def mine_of_layer(ll):
        return pl.BlockSpec((None, tr, D), lambda l, i, chip_ref: (chip_ref[0], jnp.where(l == ll, i, 0), 0))

    def theirs_of_layer(ll):
        return pl.BlockSpec((3, tr, D), lambda l, i, chip_ref: (0, jnp.where(l == ll, i, 0), 0))

    stacked = pl.BlockSpec((None, tr, D), lambda l, i, chip_ref: (l, i, 0))
    return pl.pallas_call(
        body,
        name=name,
        grid_spec=pltpu.PrefetchScalarGridSpec(
            num_scalar_prefetch=1,
            grid=(depth, r // tr),
            in_specs=[mine_of_layer(ll) for ll in range(depth)]
            + [theirs_of_layer(ll) for ll in range(depth)]
            + [stacked] * 3,
            out_specs=[stacked] * 4,
        ),
        out_shape=[SDS((depth, r, D), F32)] * 4,
        compiler_params=_params("parallel", "parallel"),
    )(chip, *parts, *from_chips, w, m, v)


def _adamw_small(groups, *, name):
    n = len(groups)

    def body(*refs):
        ins, outs = refs[: 4 * n], refs[4 * n :]
        for i in range(n):
            w, g, m, v = (ref[...] for ref in ins[4 * i : 4 * i + 4])
            outs[3 * i][...], outs[3 * i + 1][...], outs[3 * i + 2][...] = _adamw_math(w, g, m, v)

    vmem = pl.BlockSpec(memory_space=pltpu.VMEM)
    outs = pl.pallas_call(
        body,
        name=name,
        in_specs=[vmem] * (4 * n),
        out_specs=[vmem] * (3 * n),
        out_shape=[SDS(grp[0].shape, F32) for grp in groups for _ in range(3)],
        compiler_params=pltpu.CompilerParams(vmem_limit_bytes=VMEM_LIMIT_BYTES),
    )(*[a for grp in groups for a in grp])
    return [tuple(outs[3 * i : 3 * i + 3]) for i in range(n)]


ANY = pl.BlockSpec(memory_space=pl.ANY)


def _position():
    return lax.axis_index("x"), lax.axis_index("y"), lax.axis_index("c")


def _other_chips(x, y):
    return [(1 - x, y), (x, 1 - y), (1 - x, 1 - y)]


class _Comm:
    def __init__(self, inputs, out_shapes, sem_counts, start, finish, aliases=()):
        self.inputs, self.out_shapes, self.sem_counts = list(inputs), list(out_shapes), list(sem_counts)
        self.start, self.finish = start, finish
        self.aliases = list(aliases)

    def sem_shapes(self):
        return [pltpu.SemaphoreType.DMA((n,)) for n in self.sem_counts]


def _merge_comms(comms):
    bounds, i, o, s = [], 0, 0, 0
    for cm in comms:
        bounds.append((i, i + len(cm.inputs), o, o + len(cm.out_shapes), s, s + len(cm.sem_counts)))
        i, o, s = bounds[-1][1], bounds[-1][3], bounds[-1][5]

    def phase(which):
        def run(ins, outs, sems):
            for cm, (i0, i1, o0, o1, s0, s1) in zip(comms, bounds):
                getattr(cm, which)(ins[i0:i1], outs[o0:o1], sems[s0:s1])

        return run

    return _Comm(
        [a for cm in comms for a in cm.inputs],
        [a for cm in comms for a in cm.out_shapes],
        [a for cm in comms for a in cm.sem_counts],
        phase("start"),
        phase("finish"),
        aliases=[(i0 + i, o0 + o) for cm, (i0, _, o0, _, _, _) in zip(comms, bounds) for i, o in cm.aliases],
    )


def _call(body, args, *, semantics, comm=None, **kw):
    if comm is None:
        return pl.pallas_call(body, compiler_params=_params(*semantics), **kw)(*args)
    grid, in_specs, out_specs, out_shape = kw["grid"], kw["in_specs"], kw["out_specs"], kw["out_shape"]
    scratch = list(kw.get("scratch_shapes", ()))
    single = not isinstance(out_shape, (list, tuple))
    core_specs = [out_specs] if single else list(out_specs)
    core_shapes = [out_shape] if single else list(out_shape)
    n_in, n_out, n_scr = len(in_specs), len(core_shapes), len(scratch)
    n_cin, n_cout = len(comm.inputs), len(comm.out_shapes)
    steps = 1
    for g in grid:
        steps *= g

    def hosted(*refs):
        core_in, c_in = refs[:n_in], refs[n_in : n_in + n_cin]
        o0 = n_in + n_cin
        core_out, c_out = refs[o0 : o0 + n_out], refs[o0 + n_out : o0 + n_out + n_cout]
        s0 = o0 + n_out + n_cout
        core_scr, sems = refs[s0 : s0 + n_scr], refs[s0 + n_scr :]
        step = pl.program_id(0)
        for d in range(1, len(grid)):
            step = step * grid[d] + pl.program_id(d)

        @pl.when(step == 0)
        def _():
            comm.start(c_in, c_out, sems)

        body(*core_in, *core_out, *core_scr)

        @pl.when(step == steps - 1)
        def _():
            comm.finish(c_in, c_out, sems)

    outs = pl.pallas_call(
        hosted,
        name=kw["name"],
        grid=grid,
        in_specs=list(in_specs) + [ANY] * n_cin,
        out_specs=core_specs + [ANY] * n_cout,
        out_shape=core_shapes + comm.out_shapes,
        scratch_shapes=scratch + comm.sem_shapes(),
        input_output_aliases={n_in + i: n_out + o for i, o in comm.aliases},
        compiler_params=_params(*(["arbitrary"] * len(grid))),
    )(*args, *comm.inputs)
    return (outs[0] if single else outs[:n_out]), outs[n_out:]


def _comm_only(comm, *, name):
    n_cin, n_cout = len(comm.inputs), len(comm.out_shapes)

    def body(*refs):
        ins, outs, sems = refs[:n_cin], refs[n_cin : n_cin + n_cout], refs[n_cin + n_cout :]
        comm.start(ins, outs, sems)
        comm.finish(ins, outs, sems)

    return pl.pallas_call(
        body,
        name=name,
        in_specs=[ANY] * n_cin,
        out_specs=[ANY] * n_cout,
        out_shape=comm.out_shapes,
        scratch_shapes=comm.sem_shapes(),
    )(*comm.inputs)


def _gather_comm(shards):
    n = len(shards)
    per = 7

    def plan(ins, outs, sems):
        send, recv, local = sems
        x, y, c = _position()
        me, sibling = (x, y, c), (x, y, 1 - c)
        chips = _other_chips(x, y)

        def block(t, px, py, pc):
            return outs[t].at[pl.ds(4 * px + 2 * py + pc, 1)]

        def copy(t, k, blk, to, src=None):
            return pltpu.make_async_remote_copy(
                src_ref=block(t, *blk) if src is None else src,
                dst_ref=block(t, *blk),
                send_sem=send.at[t * per + k],
                recv_sem=recv.at[t * per + k],
                device_id=to,
                device_id_type=MESH,
            )

        mine = [pltpu.make_async_copy(ins[t], block(t, *me), local.at[t]) for t in range(n)]
        to_chips = [copy(t, 1 + j, me, (*chip, c), src=ins[t]) for t in range(n) for j, chip in enumerate(chips)]
        to_sibling = [copy(t, 0, me, sibling, src=ins[t]) for t in range(n)]
        from_chips = [copy(t, 1 + j, (*chip, c), me) for t in range(n) for j, chip in enumerate(chips)]
        passed_on = [copy(t, 4 + j, (*chip, c), sibling) for t in range(n) for j, chip in enumerate(chips)]
        from_sibling = [copy(t, 0, sibling, me) for t in range(n)]
        from_sibling += [copy(t, 4 + j, (*chip, 1 - c), me) for t in range(n) for j, chip in enumerate(chips)]
        return mine, to_chips, to_sibling, from_chips, passed_on, from_sibling

    def start(ins, outs, sems):
        mine, to_chips, to_sibling, _, _, _ = plan(ins, outs, sems)
        for cp in mine + to_chips + to_sibling:
            cp.start()

    def finish(ins, outs, sems):
        mine, to_chips, to_sibling, from_chips, passed_on, from_sibling = plan(ins, outs, sems)
        for arrived, onward in zip(from_chips, passed_on):
            arrived.wait_recv()
            onward.start()
        for cp in from_sibling:
            cp.wait_recv()
        for cp in to_chips + to_sibling + passed_on:
            cp.wait_send()
        for cp in mine:
            cp.wait()

    out_shapes = [SDS((N_DEV,) + sh.shape[1:], sh.dtype) for sh in shards]
    return _Comm(shards, out_shapes, [n * per, n * per, n], start, finish)


def _gather_halves(shards=None, arrived=None):
    first_half = arrived is None
    arrays = shards if first_half else arrived
    n = len(arrays)
    per = 4 if first_half else 3

    def plan(ins, outs, sems):
        x, y, c = _position()
        me, sibling = (x, y, c), (x, y, 1 - c)
        chips = _other_chips(x, y)

        def block(t, px, py, pc):
            return outs[t].at[pl.ds(4 * px + 2 * py + pc, 1)]

        def copy(t, k, blk, to, src=None):
            return pltpu.make_async_remote_copy(
                src_ref=block(t, *blk) if src is None else src,
                dst_ref=block(t, *blk),
                send_sem=sems[0].at[t * per + k],
                recv_sem=sems[1].at[t * per + k],
                device_id=to,
                device_id_type=MESH,
            )

        if first_half:
            local = [pltpu.make_async_copy(ins[t], block(t, *me), sems[2].at[t]) for t in range(n)]
            sent = [copy(t, 1 + j, me, (*chip, c), src=ins[t]) for t in range(n) for j, chip in enumerate(chips)]
            sent += [copy(t, 0, me, sibling, src=ins[t]) for t in range(n)]
            landing = [copy(t, 1 + j, (*chip, c), me) for t in range(n) for j, chip in enumerate(chips)]
            landing += [copy(t, 0, sibling, me) for t in range(n)]
        else:
            local = []
            sent = [copy(t, j, (*chip, c), sibling) for t in range(n) for j, chip in enumerate(chips)]
            landing = [copy(t, j, (*chip, 1 - c), me) for t in range(n) for j, chip in enumerate(chips)]
        return local, sent, landing

    def start(ins, outs, sems):
        local, sent, _ = plan(ins, outs, sems)
        for cp in local + sent:
            cp.start()

    def finish(ins, outs, sems):
        local, sent, landing = plan(ins, outs, sems)
        for cp in landing:
            cp.wait_recv()
        for cp in sent:
            cp.wait_send()
        for cp in local:
            cp.wait()

    if first_half:
        out_shapes = [SDS((N_DEV,) + sh.shape[1:], sh.dtype) for sh in shards]
        return _Comm(shards, out_shapes, [n * per, n * per, n], start, finish)
    out_shapes = [SDS(a.shape, a.dtype) for a in arrived]
    return _Comm(arrived, out_shapes, [n * per, n * per], start, finish, aliases=[(t, t) for t in range(n)])


def _exchange_comm(arrays, out_shapes, n_copies, copies_of):
    def start(ins, outs, sems):
        for cp in copies_of(ins, outs, *sems):
            cp.start()

    def finish(ins, outs, sems):
        for cp in copies_of(ins, outs, *sems):
            cp.wait()

    return _Comm(arrays, out_shapes, [n_copies, n_copies], start, finish)


def _sibling_comm(grads):
    def copies_of(ins, outs, send, recv):
        x, y, c = _position()
        return [
            pltpu.make_async_remote_copy(
                src_ref=ins[t].at[:, pl.ds(1 - c, 1)],
                dst_ref=outs[t],
                send_sem=send.at[t],
                recv_sem=recv.at[t],
                device_id=(x, y, 1 - c),
                device_id_type=MESH,
            )
            for t in range(len(ins))
        ]

    return _exchange_comm(grads, [SDS((4, 1) + g.shape[2:], g.dtype) for g in grads], len(grads), copies_of)


def _chips_comm(parts):
    def copies_of(ins, outs, send, recv):
        x, y, c = _position()
        return [
            pltpu.make_async_remote_copy(
                src_ref=ins[t].at[pl.ds(2 * px + py, 1)],
                dst_ref=outs[t].at[pl.ds(k, 1)],
                send_sem=send.at[3 * t + k],
                recv_sem=recv.at[3 * t + k],
                device_id=(px, py, c),
                device_id_type=MESH,
            )
            for t in range(len(ins))
            for k, (px, py) in enumerate(_other_chips(x, y))
        ]

    return _exchange_comm(parts, [SDS((3,) + p.shape[1:], p.dtype) for p in parts], 3 * len(parts), copies_of)


def _sum_with_sibling(grad, got, core, *, name):
    rows = grad.shape[2]

    def body(core_ref, a_ref, b_ref, o_ref):
        o_ref[...] = (a_ref[...].astype(F32) + b_ref[...].astype(F32)).astype(o_ref.dtype)

    return pl.pallas_call(
        body,
        name=name,
        grid_spec=pltpu.PrefetchScalarGridSpec(
            num_scalar_prefetch=1,
            grid=(4,),
            in_specs=[
                pl.BlockSpec((None, None, rows, D), lambda q, core_ref: (q, core_ref[0], 0, 0)),
                pl.BlockSpec((None, None, rows, D), lambda q, core_ref: (q, 0, 0, 0)),
            ],
            out_specs=pl.BlockSpec((None, rows, D), lambda q, core_ref: (q, 0, 0)),
        ),
        out_shape=SDS((4, rows, D), grad.dtype),
        compiler_params=_params("parallel"),
    )(core, grad, got)


def _sum_chips(part, got, chip, *, name):
    rows = part.shape[1]

    def body(chip_ref, a_ref, b_ref, o_ref):
        o_ref[...] = ((a_ref[...].astype(F32) + b_ref[0].astype(F32)) + b_ref[1].astype(F32)) + b_ref[2].astype(F32)

    return pl.pallas_call(
        body,
        name=name,
        grid_spec=pltpu.PrefetchScalarGridSpec(
            num_scalar_prefetch=1,
            grid=(1,),
            in_specs=[
                pl.BlockSpec((None, rows, D), lambda i, chip_ref: (chip_ref[0], 0, 0)),
                pl.BlockSpec((3, rows, D), lambda i, chip_ref: (0, 0, 0)),
            ],
            out_specs=pl.BlockSpec((rows, D), lambda i, chip_ref: (0, 0)),
        ),
        out_shape=SDS((rows, D), F32),
        compiler_params=_params("arbitrary"),
    )(chip, part, got)


def _all_reduce_small(pack, *, name):
    rows = pack.shape[1]

    def body(in_ref, out_ref, from_sibling, part, from_chips, send, recv):
        x, y, c = _position()
        me, sibling = (x, y, c), (x, y, 1 - c)
        chips = _other_chips(x, y)
        waiting = []

        def copy(k, src, dst, to):
            return pltpu.make_async_remote_copy(
                src_ref=src, dst_ref=dst, send_sem=send.at[k], recv_sem=recv.at[k], device_id=to, device_id_type=MESH
            )

        def exchange(copies):
            for cp in copies:
                cp.start()
            for cp in copies:
                cp.wait_recv()
            waiting.extend(copies)

        def block(px, py, pc):
            return out_ref.at[4 * px + 2 * py + pc]

        exchange([copy(q, in_ref.at[2 * q + 1 - c], from_sibling.at[q], sibling) for q in range(4)])
        for q in range(4):
            part[q] = in_ref[2 * q + c] + from_sibling[q]
        exchange([copy(4 + k, part.at[2 * px + py], from_chips.at[k], (px, py, c)) for k, (px, py) in enumerate(chips)])
        out_ref[4 * x + 2 * y + c] = ((part[2 * x + y] + from_chips[0]) + from_chips[1]) + from_chips[2]
        exchange(
            [copy(7, block(*me), block(*me), sibling)]
            + [copy(8 + k, block(*me), block(*me), (px, py, c)) for k, (px, py) in enumerate(chips)]
        )
        exchange([copy(11 + k, block(px, py, c), block(px, py, c), sibling) for k, (px, py) in enumerate(chips)])
        for cp in waiting:
            cp.wait_send()

    vmem = pl.BlockSpec(memory_space=pltpu.VMEM)
    return pl.pallas_call(
        body,
        name=name,
        in_specs=[vmem],
        out_specs=vmem,
        out_shape=SDS(pack.shape, F32),
        scratch_shapes=[
            pltpu.VMEM((4, rows, D), F32),
            pltpu.VMEM((4, rows, D), F32),
            pltpu.VMEM((3, rows, D), F32),
            pltpu.SemaphoreType.DMA((14,)),
            pltpu.SemaphoreType.DMA((14,)),
        ],
        compiler_params=pltpu.CompilerParams(vmem_limit_bytes=VMEM_LIMIT_BYTES),
    )(pack)


def _pack(arrays, rows):
    flat = jnp.concatenate([a.reshape(-1).astype(F32) for a in arrays])
    return jnp.pad(flat, (0, rows * D - flat.shape[0])).reshape(rows, D)


def _unpack(pack, shapes):
    flat = pack.reshape(-1)
    out, off = [], 0
    for sh in shapes:
        size = 1
        for dim in sh:
            size *= dim
        out.append(flat[off : off + size].reshape(sh))
        off += size
    return out


def _block_diag_pairs(w):
    w = w.reshape(N_RNN_TILES, 2, HEAD_DIM, HEAD_DIM)
    z = jnp.zeros_like(w[:, 0])
    top = jnp.concatenate([w[:, 0], z], axis=2)
    bot = jnp.concatenate([z, w[:, 1]], axis=2)
    return jnp.concatenate([top, bot], axis=1)


def _diag_blocks(w2):
    a = w2[:, :HEAD_DIM, :HEAD_DIM]
    b = w2[:, HEAD_DIM:, HEAD_DIM:]
    return jnp.stack([a, b], axis=1).reshape(RNN_HEADS, HEAD_DIM, HEAD_DIM)


BIG = ("w_in", "w_branch_a", "w_branch_b", "w_out", "w_up", "w_down")
TRANSPOSED = ("w_in", "w_up")
SMALL = (
    "norm_mix_g", "conv_w", "conv_b", "lru_w_a", "lru_b_a", "lru_w_x", "lru_b_x", "lru_lambda",
    "sgu_ln_g", "sgu_ln_b", "sgu_w_s", "sgu_b_s", "norm_ffn_g", "final_norm_g",
)
WEIGHTS = (
    "norm_mix_g", "w_in", "conv_w", "conv_b", "lru_w_a", "lru_b_a", "lru_w_x", "lru_b_x", "lru_lambda", "sgu_ln_g",
    "sgu_ln_b", "sgu_w_s", "sgu_b_s", "w_branch_a", "w_branch_b", "w_out", "norm_ffn_g", "w_up", "w_down", "final_norm_g",
)

TM = 512
TM_NT = 1024
TN_IN = 1664
TN_UP = 2048
TKA = 512
TKA_PIECES = 256
TC = 512
TC_BWD = 1024
TB = 256
TB_BWD = 512
TR = 256


_BRANCH_WEIGHTS = ("w_branch_a", "w_branch_b", "w_out")
GATHERS_RIDING = (
    {
        "in_proj": ([(0, name) for name in _BRANCH_WEIGHTS] + [(0, "w_up")], []),
        "branch_a_fwd": ([(1, "w_in")], [(0, name) for name in _BRANCH_WEIGHTS] + [(0, "w_up")]),
        "sgu_fwd": ([], [(1, "w_in")]),
        "merge_fwd": ([(0, "w_down")], []),
        "ffn_up": ([(1, name) for name in _BRANCH_WEIGHTS], [(0, "w_down")]),
        "ffn_down": ([], [(1, name) for name in _BRANCH_WEIGHTS]),
    },
    {"in_proj": ([(1, "w_down")], []), "branch_a_fwd": ([(1, "w_up")], [(1, "w_down")]), "sgu_fwd": ([], [(1, "w_up")])},
)


def _layer_forward(l, x, p, w, shards, arriving, loss_head=None):
    def run(key, fn, *args, **kw):
        first, second = GATHERS_RIDING[l].get(key, ((), ()))
        comms = []
        if first:
            comms.append(_gather_halves(shards=[shards[l2][n2] for l2, n2 in first]))
        if second:
            comms.append(_gather_halves(arrived=[arriving.pop(k) for k in second]))
        if not comms:
            return fn(*args, **kw)
        out, got = fn(*args, comm=_merge_comms(comms), **kw)
        arriving.update(zip(first, got[: len(first)]))
        for (l2, n2), full in zip(second, got[len(first) :]):
            w[l2][n2] = full.reshape(-1, D)
        return out

    proj, h = run("in_proj", _norm_matmul_nt, x, p["norm_mix_g"], w[l]["w_in"], tm=TM_NT, tn=TN_IN, name=f"in_proj_{l}")
    hseq, ya_pre = run(
        "branch_a_fwd", _branch_a_fwd, proj, p["conv_w"], p["conv_b"], p["wa2"], p["lru_b_a"], p["wx2"], p["lru_b_x"],
        p["lru_lambda"], tc=TC, name=f"branch_a_fwd_{l}",
    )
    yb_pre = run("sgu_fwd", _sgu_fwd, proj, p["sgu_ln_g"], p["sgu_ln_b"], p["wm"], p["sgu_bias"], tb=TB, name=f"sgu_fwd_{l}")
    x1, ya, yb = run(
        "merge_fwd", _merge_fwd, ya_pre, yb_pre, proj, x, w[l]["w_branch_a"], w[l]["w_branch_b"], w[l]["w_out"], tm=TM,
        name=f"merge_fwd_{l}",
    )
    f_pre, h2 = run("ffn_up", _norm_matmul_nt, x1, p["norm_ffn_g"], w[l]["w_up"], tm=TM_NT, tn=TN_UP, name=f"ffn_up_{l}")
    saved = dict(x=x, h=h, proj=proj, hseq=hseq, ya_pre=ya_pre, yb_pre=yb_pre, ya=ya, yb=yb, x1=x1, h2=h2, f_pre=f_pre)
    if loss_head is None:
        return run("ffn_down", _matmul_nn_res, f_pre, w[l]["w_down"], x1, relu2=True, tm=TM, name=f"ffn_down_{l}"), saved
    return _ffn_down_loss(f_pre, w[l]["w_down"], x1, *loss_head, tm=TM, name=f"ffn_down_loss_{l}"), saved


def _layer_backward(l, dx2, dx2b, sv, p, w, core, waiting, last):
    parts, from_chips = {}, {}

    def by_device(g):
        return g.reshape(4, 2, -1, D)

    def with_sibling(name, g, got):
        parts[name] = _sum_with_sibling(by_device(g), got, core, name=f"sum_sibling_{name}_{l}")

    df_pre = _matmul_nt_drelu2(dx2b, w["w_down"], sv["f_pre"], tm=TM_NT, tn=TN_UP, name=f"ffn_down_bwd_{l}")
    g_down = _matmul_tn([sv["f_pre"]], dx2b, relu2=True, tka=TKA, name=f"grad_w_down_{l}")
    g_up, (got,) = _matmul_tn(
        [df_pre], sv["h2"], relu2=False, tka=TKA, name=f"grad_w_up_{l}", comm=_sibling_comm([by_device(g_down)])
    )
    with_sibling("w_down", g_down, got)
    (dx1, dx1b, g_norm_ffn), (got,) = _matmul_nn_rmsnorm_bwd(
        [df_pre], w["w_up"], sv["x1"], p["norm_ffn_g"], dx2, tm=TM, name=f"ffn_up_bwd_{l}",
        comm=_sibling_comm([by_device(g_up)]),
    )
    with_sibling("w_up", g_up, got)
    (merged, dya, dyb, dga, dgb, dya_pre, dyb_pre), (from_chips[l, "w_up"],) = _merge_bwd(
        dx1b, sv["ya"], sv["yb"], sv["proj"], w["w_branch_a"], w["w_branch_b"], w["w_out"], tm=TM, name=f"merge_bwd_{l}",
        comm=_chips_comm([parts["w_up"]]),
    )
    g_out, g_ba, g_bb = _matmuls_tn(
        [(merged, dx1b), (sv["ya_pre"], dya), (sv["yb_pre"], dyb)], ts=2 * TM, name=f"grad_w_branches_{l}"
    )
    branch = (("w_out", g_out), ("w_branch_a", g_ba), ("w_branch_b", g_bb))
    (du, dv, g_ws, g_bs, g_lng, g_lnb), got = _sgu_bwd(
        dyb_pre, sv["proj"], p["sgu_ln_g"], p["sgu_ln_b"], p["wm"], p["wmt"], p["sgu_bias"], p["mask"], tb=TB_BWD,
        name=f"sgu_bwd_{l}",
        comm=_merge_comms([_sibling_comm([by_device(g) for _, g in branch]), _chips_comm([parts["w_down"]])]),
    )
    from_chips[l, "w_down"] = got[-1]
    for (name, g), landed in zip(branch, got):
        with_sibling(name, g, landed)
    riding = [((l, name), parts[name]) for name, _ in branch] + list(waiting)
    (dxr, dgr, g_cw, g_cb, g_ba_, g_bx, g_lam, g_wa2, g_wx2), got = _branch_a_bwd(
        dya_pre, sv["proj"], sv["hseq"], p["conv_w"], p["conv_b"], p["wa2"], p["lru_b_a"], p["wx2"], p["lru_b_x"],
        p["lru_lambda"], p["wa2t"], p["wx2t"], tc=TC_BWD, name=f"branch_a_bwd_{l}", comm=_chips_comm([part for _, part in riding]),
    )
    for (key, _), landed in zip(riding, got):
        from_chips[key] = landed
    dproj = [dxr, dgr, du, dv, dga, dgb]
    g_in = _matmul_tn(dproj, sv["h"], relu2=False, tka=TKA_PIECES, name=f"grad_w_in_{l}")
    if last:
        (got,) = _comm_only(_sibling_comm([by_device(g_in)]), name=f"grad_w_in_to_sibling_{l}")
        with_sibling("w_in", g_in, got)
        riding = _chips_comm([parts["w_in"]])
    else:
        riding = _sibling_comm([by_device(g_in)])
    (dx, dxb, g_norm_mix), (got,) = _matmul_nn_rmsnorm_bwd(
        dproj, w["w_in"], sv["x"], p["norm_mix_g"], dx1, tm=TM, name=f"in_proj_bwd_{l}", comm=riding
    )
    if last:
        from_chips[l, "w_in"] = got
    else:
        with_sibling("w_in", g_in, got)
    small = dict(
        norm_mix_g=g_norm_mix[0], conv_w=g_cw, conv_b=g_cb[0], lru_w_a=_diag_blocks(g_wa2), lru_b_a=g_ba_.reshape(RNN_HEADS, HEAD_DIM),
        lru_w_x=_diag_blocks(g_wx2), lru_b_x=g_bx.reshape(RNN_HEADS, HEAD_DIM), lru_lambda=g_lam[0], sgu_ln_g=g_lng[0],
        sgu_ln_b=g_lnb[0], sgu_w_s=g_ws, sgu_b_s=g_bs[:, :, 0], norm_ffn_g=g_norm_ffn[0],
    )
    return dx, dxb, small, parts, from_chips


def _prepare_small(l, given):
    chunk_id = jnp.arange(SGU_BLOCK) // CHUNK
    mask = (chunk_id[:, None] >= chunk_id[None, :]).astype(F32)
    wm = given["sgu_w_s"][l] * mask
    wa2 = _block_diag_pairs(given["lru_w_a"][l])
    wx2 = _block_diag_pairs(given["lru_w_x"][l])
    row = lambda a: a.reshape(1, -1)
    return dict(
        norm_mix_g=row(given["norm_mix_g"][l]),
        norm_ffn_g=row(given["norm_ffn_g"][l]),
        conv_w=given["conv_w_full"][l],
        conv_b=row(given["conv_b"][l]),
        wa2=wa2.astype(BF16),
        wx2=wx2.astype(BF16),
        wa2t=jnp.swapaxes(wa2, 1, 2).astype(BF16),
        wx2t=jnp.swapaxes(wx2, 1, 2).astype(BF16),
        lru_b_a=row(given["lru_b_a"][l]),
        lru_b_x=row(given["lru_b_x"][l]),
        lru_lambda=row(given["lru_lambda"][l]),
        sgu_ln_g=row(given["sgu_ln_g"][l]),
        sgu_ln_b=row(given["sgu_ln_b"][l]),
        wm=wm.astype(BF16),
        wmt=jnp.swapaxes(wm, 1, 2).astype(BF16),
        sgu_bias=jnp.broadcast_to(given["sgu_b_s"][l][:, :, None], (SGU_GROUPS, SGU_BLOCK, LANES)),
        mask=mask,
    )


def _step(given):
    x_idx, y_idx, c_idx = _position()
    dev = 4 * x_idx + 2 * y_idx + c_idx
    core = c_idx.astype(jnp.int32).reshape(1)
    chip = (2 * x_idx + y_idx).astype(jnp.int32).reshape(1)

    def rows_first(name, a):
        return jnp.swapaxes(a, 1, 2) if name in TRANSPOSED else a

    shards = []
    for l in range(DEPTH):
        shards.append({name: rows_first(name, given[name])[l].astype(BF16)[None] for name in BIG})
    conv_mine = given["conv_w"].reshape(1, DEPTH * CONV_WIDTH, D_RNN // N_DEV)
    w_in_first, conv_all = _comm_only(_gather_comm([shards[0]["w_in"], conv_mine]), name="gather_first")
    weights = [{"w_in": w_in_first.reshape(-1, D)}, {}]
    conv_all = conv_all.reshape(N_DEV, DEPTH, CONV_WIDTH, D_RNN // N_DEV)
    given = dict(given, conv_w_full=jnp.moveaxis(conv_all, 0, 2).reshape(DEPTH, CONV_WIDTH, D_RNN))

    small_params = [_prepare_small(l, given) for l in range(DEPTH)]
    x = given["x"][0]
    saved, arriving = [], {}
    loss_head = (given["final_norm_g"].reshape(1, D), given["loss_target"][0])
    for l in range(DEPTH):
        x, sv = _layer_forward(
            l, x, small_params[l], weights, shards, arriving, loss_head=loss_head if l == DEPTH - 1 else None
        )
        saved.append(sv)
    dx, dxb, g_final, loss = x
    small_grads, parts, from_chips, waiting = [None] * DEPTH, [None] * DEPTH, {}, []
    for l in reversed(range(DEPTH)):
        dx, dxb, small_grads[l], parts[l], got = _layer_backward(
            l, dx, dxb, saved[l], small_params[l], weights[l], core, waiting, last=l == 0
        )
        from_chips.update(got)
        waiting = [((l, "w_in"), parts[l]["w_in"])]

    small_list = []
    for name in SMALL[:-1]:
        small_list.append(jnp.stack([small_grads[l][name] for l in range(DEPTH)]))
    small_list += [g_final[0], loss[0, :1]]
    small_shapes = [a.shape for a in small_list]
    pack = _pack(small_list, SMALL_ROWS).reshape(N_DEV, SMALL_ROWS_PER_DEV, D)
    summed = _unpack(_all_reduce_small(pack, name="all_reduce_small"), small_shapes)
    loss_total = summed[-1][0]
    grads = dict(zip(SMALL, summed[:-1]))
    cw = grads["conv_w"].reshape(DEPTH, CONV_WIDTH, N_DEV, D_RNN // N_DEV)
    grads["conv_w"] = lax.dynamic_index_in_dim(cw, dev, axis=2, keepdims=False)

    delta, new_m, new_v = {}, {}, {}
    for name in BIG:
        w, m, v = given[name], given["m_" + name], given["v_" + name]
        mine = [parts[l][name] for l in range(DEPTH)]
        theirs = [from_chips[l, name] for l in range(DEPTH)]
        if name == "w_up":
            sums = [_sum_chips(mine[l], theirs[l], chip, name=f"sum_chips_{name}_{l}").T for l in range(DEPTH)]
            out = _adamw_layers(w, sums, m, v, tr=TR, name=f"adamw_{name}")
        else:
            out = _adamw_reduced(
                rows_first(name, w), mine, theirs, rows_first(name, m), rows_first(name, v), chip, tr=TR, name=f"adamw_{name}"
            )
            out = [rows_first(name, a) for a in out]
        grads[name], delta[name], new_m[name], new_v[name] = out
    two_d = lambda a: a.reshape(1, -1) if a.ndim == 1 else a
    groups = [tuple(two_d(a) for a in (given[n], grads[n], given["m_" + n], given["v_" + n])) for n in SMALL]
    for n, (d, m2, v2) in zip(SMALL, _adamw_small(groups, name="adamw_small")):
        shape = given[n].shape
        delta[n], new_m[n], new_v[n] = d.reshape(shape), m2.reshape(shape), v2.reshape(shape)

    return (
        loss_total, dx[None],
        *[grads[n] for n in WEIGHTS], *[delta[n] for n in WEIGHTS], *[new_m[n] for n in WEIGHTS], *[new_v[n] for n in WEIGHTS],
    )


def kernel(x, norm_mix_g, w_in, conv_w, conv_b, lru_w_a, lru_b_a, lru_w_x, lru_b_x, lru_lambda, sgu_ln_g, sgu_ln_b, sgu_w_s, sgu_b_s, w_branch_a, w_branch_b, w_out, norm_ffn_g, w_up, w_down, final_norm_g, loss_target, m_norm_mix_g, m_w_in, m_conv_w, m_conv_b, m_lru_w_a, m_lru_b_a, m_lru_w_x, m_lru_b_x, m_lru_lambda, m_sgu_ln_g, m_sgu_ln_b, m_sgu_w_s, m_sgu_b_s, m_w_branch_a, m_w_branch_b, m_w_out, m_norm_ffn_g, m_w_up, m_w_down, m_final_norm_g, v_norm_mix_g, v_w_in, v_conv_w, v_conv_b, v_lru_w_a, v_lru_b_a, v_lru_w_x, v_lru_b_x, v_lru_lambda, v_sgu_ln_g, v_sgu_ln_b, v_sgu_w_s, v_sgu_b_s, v_w_branch_a, v_w_branch_b, v_w_out, v_norm_ffn_g, v_w_up, v_w_down, v_final_norm_g):
    return _step(dict(locals()))
```

```python
import jax
import jax.numpy as jnp
from jax import lax
from jax.experimental import pallas as pl
from jax.experimental.pallas import tpu as pltpu

F32 = jnp.float32
BF16 = jnp.bfloat16
SDS = jax.ShapeDtypeStruct
MESH = pl.DeviceIdType.MESH

D = 1024
D_RNN = 1280
D_SGU = 1024
D_IN = 2 * D_RNN + 2 * D_SGU + 2 * D
DEPTH = 2
RNN_HEADS = 20
HEAD_DIM = 64
CONV_WIDTH = 4
LRU_C = 8.0
SGU_GROUPS = 8
SGU_BLOCK = 128
CHUNK = 64
EPS = 1e-6
N_DEV = 8

ADAM_LR = 0.001
ADAM_B1 = 0.9
ADAM_B2 = 0.999
ADAM_EPS = 1e-08
ADAM_WD = 0.01
ADAM_STEP = 10

LANES = 128
SUBLANES = 8
VMEM_LIMIT_BYTES = 56 * 1024 * 1024

N_RNN_TILES = D_RNN // LANES
RNN_TILES_PER_STEP = 5
U_BLK512 = (2 * D_RNN) // 512
V_BLK512 = (2 * D_RNN + D_SGU) // 512
GA_BLK512 = (2 * D_RNN + 2 * D_SGU) // 512
GB_BLK512 = (2 * D_RNN + 2 * D_SGU + D) // 512

SMALL_ROWS_PER_DEV = 80
SMALL_ROWS = N_DEV * SMALL_ROWS_PER_DEV


def _params(*sem):
    return pltpu.CompilerParams(dimension_semantics=sem, vmem_limit_bytes=VMEM_LIMIT_BYTES)


def _sigmoid(x):
    return 0.5 + 0.5 * jnp.tanh(0.5 * x)


_GELU_C = 0.7978845608028654
_GELU_K = 0.044715


def _gelu(x):
    t = jnp.tanh(_GELU_C * (x + _GELU_K * x * x * x))
    return 0.5 * x * (1.0 + t)


def _gelu_and_grad(x):
    t = jnp.tanh(_GELU_C * (x + _GELU_K * x * x * x))
    val = 0.5 * x * (1.0 + t)
    grad = 0.5 * (1.0 + t) + 0.5 * x * (1.0 - t * t) * _GELU_C * (1.0 + 3.0 * _GELU_K * x * x)
    return val, grad


def _one_minus_square(log_a, a):
    return -jnp.tanh(log_a) * (1.0 + a * a)


def _dot(a, b):
    return jnp.dot(a, b, preferred_element_type=F32)


def _dot_nt(a, b):
    return lax.dot_general(a, b, (((1,), (1,)), ((), ())), preferred_element_type=F32)


def _dot_tn(a, b):
    return lax.dot_general(a, b, (((0,), (0,)), ((), ())), preferred_element_type=F32)


def _norm_matmul_nt(x, g, w, *, tm, tn, name, comm=None):
    s, n = x.shape[0], w.shape[0]
    tm, tn = min(tm, s), min(tn, n)

    def body(x_ref, g_ref, w_ref, o_ref, h_ref):
        @pl.when(pl.program_id(1) == 0)
        def _():
            xv = x_ref[...]
            r = lax.rsqrt(jnp.mean(xv * xv, axis=-1, keepdims=True) + EPS)
            h_ref[...] = (xv * r * g_ref[...]).astype(BF16)

        o_ref[...] = _dot_nt(h_ref[...], w_ref[...]).astype(o_ref.dtype)

    return _call(
        body,
        (x, g, w),
        name=name,
        grid=(s // tm, n // tn),
        in_specs=[
            pl.BlockSpec((tm, D), lambda i, j: (i, 0)),
            pl.BlockSpec((1, D), lambda i, j: (0, 0)),
            pl.BlockSpec((tn, D), lambda i, j: (j, 0)),
        ],
        out_specs=[pl.BlockSpec((tm, tn), lambda i, j: (i, j)), pl.BlockSpec((tm, D), lambda i, j: (i, 0))],
        out_shape=[SDS((s, n), BF16), SDS((s, D), BF16)],
        semantics=("parallel", "arbitrary"),
        comm=comm,
    )


def _matmul_nn_res(a, w, res, *, relu2, tm, name, comm=None):
    s, k = a.shape
    tm = min(tm, s)

    def body(a_ref, w_ref, r_ref, o_ref):
        av = a_ref[...]
        if relu2:
            t = jnp.maximum(av.astype(F32), 0.0)
            av = (t * t).astype(BF16)
        o_ref[...] = r_ref[...] + _dot(av, w_ref[...])

    return _call(
        body,
        (a, w, res),
        name=name,
        grid=(s // tm,),
        in_specs=[
            pl.BlockSpec((tm, k), lambda i: (i, 0)),
            pl.BlockSpec((k, D), lambda i: (0, 0)),
            pl.BlockSpec((tm, D), lambda i: (i, 0)),
        ],
        out_specs=pl.BlockSpec((tm, D), lambda i: (i, 0)),
        out_shape=SDS((s, D), F32),
        semantics=("parallel",),
        comm=comm,
    )


def _matmul_nt_drelu2(a, w, pre, *, tm, tn, name):
    s, n = a.shape[0], w.shape[0]
    tm, tn = min(tm, s), min(tn, n)

    def body(a_ref, w_ref, p_ref, o_ref):
        d = _dot_nt(a_ref[...], w_ref[...])
        o_ref[...] = (d * (2.0 * jnp.maximum(p_ref[...].astype(F32), 0.0))).astype(o_ref.dtype)

    return pl.pallas_call(
        body,
        name=name,
        grid=(s // tm, n // tn),
        in_specs=[
            pl.BlockSpec((tm, D), lambda i, j: (i, 0)),
            pl.BlockSpec((tn, D), lambda i, j: (j, 0)),
            pl.BlockSpec((tm, tn), lambda i, j: (i, j)),
        ],
        out_specs=pl.BlockSpec((tm, tn), lambda i, j: (i, j)),
        out_shape=SDS((s, n), BF16),
        compiler_params=_params("parallel", "arbitrary"),
    )(a, w, pre)


def _matmul_tn(a_list, b, *, relu2, tka, name, comm=None):
    s = b.shape[0]
    n = len(a_list)
    nblk = [a.shape[1] // tka for a in a_list]
    starts = [sum(nblk[:p]) for p in range(n)]

    def body(*refs):
        a_refs, b_ref, o_ref = refs[:n], refs[n], refs[n + 1]
        i = pl.program_id(0)
        for p in range(n):

            @pl.when((i >= starts[p]) & (i < starts[p] + nblk[p]))
            def _(p=p):
                av = a_refs[p][...]
                if relu2:
                    t = jnp.maximum(av.astype(F32), 0.0)
                    av = (t * t).astype(BF16)
                o_ref[...] = _dot_tn(av, b_ref[...]).astype(o_ref.dtype)

    def piece_spec(p):
        return pl.BlockSpec((s, tka), lambda i: (0, jnp.clip(i - starts[p], 0, nblk[p] - 1)))

    return _call(
        body,
        (*a_list, b),
        name=name,
        grid=(sum(nblk),),
        in_specs=[piece_spec(p) for p in range(n)] + [pl.BlockSpec((s, D), lambda i: (0, 0))],
        out_specs=pl.BlockSpec((tka, D), lambda i: (i, 0)),
        out_shape=SDS((sum(nblk) * tka, D), BF16),
        semantics=("parallel",),
        comm=comm,
    )


def _matmuls_tn(pairs, *, ts, name):
    s = pairs[0][0].shape[0]
    ts = min(ts, s)
    n = len(pairs)
    steps = s // ts

    def body(*refs):
        ins, outs, accs = refs[: 2 * n], refs[2 * n : 3 * n], refs[3 * n :]
        for p in range(n):
            part = _dot_tn(ins[2 * p][...], ins[2 * p + 1][...])

            @pl.when(pl.program_id(0) == 0)
            def _(p=p, part=part):
                accs[p][...] = part

            @pl.when(pl.program_id(0) > 0)
            def _(p=p, part=part):
                accs[p][...] += part

        @pl.when(pl.program_id(0) == steps - 1)
        def _():
            for p in range(n):
                outs[p][...] = accs[p][...].astype(BF16)

    widths = [a.shape[1] for a, _ in pairs]
    in_specs = []
    for wd in widths:
        in_specs += [pl.BlockSpec((ts, wd), lambda i: (i, 0)), pl.BlockSpec((ts, D), lambda i: (i, 0))]
    return pl.pallas_call(
        body,
        name=name,
        grid=(steps,),
        in_specs=in_specs,
        out_specs=[pl.BlockSpec((wd, D), lambda i: (0, 0)) for wd in widths],
        out_shape=[SDS((wd, D), BF16) for wd in widths],
        scratch_shapes=[pltpu.VMEM((wd, D), F32) for wd in widths],
        compiler_params=_params("arbitrary"),
    )(*[x for pair in pairs for x in pair])


def _matmul_nn_rmsnorm_bwd(a_list, w, x, g, res, *, tm, name, comm=None):
    s = x.shape[0]
    tm = min(tm, s)
    n = len(a_list)
    widths = [a.shape[1] for a in a_list]
    offs = [sum(widths[:p]) for p in range(n)]
    k = sum(widths)

    def body(*refs):
        a_refs = refs[:n]
        w_ref, x_ref, g_ref, r_ref, dx_ref, dxb_ref, dg_ref = refs[n:]

        @pl.when(pl.program_id(0) == 0)
        def _():
            dg_ref[...] = jnp.zeros_like(dg_ref)

        dh = _dot(a_refs[0][...], w_ref[0 : widths[0], :])
        for p in range(1, n):
            dh += _dot(a_refs[p][...], w_ref[offs[p] : offs[p] + widths[p], :])
        xv = x_ref[...]
        r = lax.rsqrt(jnp.mean(xv * xv, axis=-1, keepdims=True) + EPS)
        xhat = xv * r
        dxh = dh * g_ref[...]
        dx = r_ref[...] + r * (dxh - xhat * jnp.mean(dxh * xhat, axis=-1, keepdims=True))
        dx_ref[...] = dx
        dxb_ref[...] = dx.astype(BF16)
        dg_ref[...] += jnp.sum(dh * xhat, axis=0, keepdims=True)

    act = pl.BlockSpec((tm, D), lambda i: (i, 0))
    vec = pl.BlockSpec((1, D), lambda i: (0, 0))
    return _call(
        body,
        (*a_list, w, x, g, res),
        name=name,
        grid=(s // tm,),
        in_specs=[pl.BlockSpec((tm, wd), lambda i: (i, 0)) for wd in widths]
        + [pl.BlockSpec((k, D), lambda i: (0, 0), pipeline_mode=pl.Buffered(1)), act, vec, act],
        out_specs=[act, act, vec],
        out_shape=[SDS((s, D), F32), SDS((s, D), BF16), SDS((1, D), F32)],
        semantics=("arbitrary",),
        comm=comm,
    )


def _rows_after(ext, k, n):
    return pltpu.roll(ext, n + SUBLANES - k, 0)[:n, :]


def _scan_forward(a, b, n):
    row = lax.broadcasted_iota(jnp.int32, a.shape, 0)
    d = 1
    while d < n:
        if d < SUBLANES:
            m = row >= d
            a_s = jnp.where(m, pltpu.roll(a, d, 0), 1.0)
            b_s = jnp.where(m, pltpu.roll(b, d, 0), 0.0)
            b = a * b_s + b
            a = a * a_s
        else:
            b = jnp.concatenate([b[:d], a[d:] * b[: n - d] + b[d:]], axis=0)
            a = jnp.concatenate([a[:d], a[d:] * a[: n - d]], axis=0)
        d *= 2
    return a, b


def _scan_backward(a, b, n):
    row = lax.broadcasted_iota(jnp.int32, a.shape, 0)
    d = 1
    while d < n:
        if d < SUBLANES:
            m = row < n - d
            a_s = jnp.where(m, pltpu.roll(a, n - d, 0), 1.0)
            b_s = jnp.where(m, pltpu.roll(b, n - d, 0), 0.0)
            b = a * b_s + b
            a = a * a_s
        else:
            b = jnp.concatenate([a[: n - d] * b[d:] + b[: n - d], b[n - d :]], axis=0)
            a = jnp.concatenate([a[: n - d] * a[d:], a[n - d :]], axis=0)
        d *= 2
    return b


def _repeat_matrix(n):
    groups = n // SUBLANES
    return (jnp.arange(n)[:, None] // SUBLANES == jnp.arange(3 * groups)[None, :] % groups).astype(BF16)


def _scan_rows(a, b, n, repeat_ref, a_scr, b_scr, reverse):
    groups = n // SUBLANES
    a3 = a.reshape(groups, SUBLANES, LANES)
    b3 = b.reshape(groups, SUBLANES, LANES)
    sub = lax.broadcasted_iota(jnp.int32, a3.shape, 1)
    for d in (1, 2, 4):
        m = (sub < SUBLANES - d) if reverse else (sub >= d)
        shift = SUBLANES - d if reverse else d
        a_s = jnp.where(m, pltpu.roll(a3, shift, 1), 1.0)
        b_s = jnp.where(m, pltpu.roll(b3, shift, 1), 0.0)
        b3 = a3 * b_s + b3
        a3 = a3 * a_s
    a_scr[...] = a3.reshape(n, LANES)
    b_scr[...] = b3.reshape(n, LANES)
    edge = 0 if reverse else SUBLANES - 1
    a_tot = a_scr[pl.ds(edge, groups, stride=SUBLANES), :]
    b_tot = b_scr[pl.ds(edge, groups, stride=SUBLANES), :]
    row = lax.broadcasted_iota(jnp.int32, a_tot.shape, 0)
    if reverse:
        through = _scan_backward(a_tot, b_tot, groups)
        entering = jnp.where(row < groups - 1, pltpu.roll(through, groups - 1, 0), 0.0)
    else:
        _, through = _scan_forward(a_tot, b_tot, groups)
        entering = jnp.where(row >= 1, pltpu.roll(through, 1, 0), 0.0)
    hi = entering.astype(BF16)
    rest = entering - hi.astype(F32)
    mid = rest.astype(BF16)
    lo = (rest - mid.astype(F32)).astype(BF16)
    repeated = _dot(repeat_ref[...], jnp.concatenate([hi, mid, lo], axis=0))
    return b_scr[...] + a_scr[...] * repeated


def _softplus_neg(lam):
    z = -lam
    return jnp.maximum(z, 0.0) + jnp.log1p(jnp.exp(-jnp.abs(z)))


def _conv_and_gates(xc, xprev, cw_ref, cb_ref, wa_ref, ba_ref, wx_ref, bx_ref, lam_ref, ext_scr):
    n = xc.shape[0]
    ext_scr[:SUBLANES, :] = xprev
    ext_scr[SUBLANES:, :] = xc
    x1, x2, x3 = (ext_scr[pl.ds(SUBLANES - k, n), :] for k in (1, 2, 3))
    xr = cb_ref[...] + x3 * cw_ref[0:1, :] + x2 * cw_ref[1:2, :] + x1 * cw_ref[2:3, :] + xc * cw_ref[3:4, :]
    xrb = xr.astype(BF16)
    r = _sigmoid(_dot(xrb, wa_ref[...]) + ba_ref[...])
    i = _sigmoid(_dot(xrb, wx_ref[...]) + bx_ref[...])
    sp = _softplus_neg(lam_ref[...])
    log_a = (-LRU_C * r) * sp
    a = jnp.exp(log_a)
    return xr, (x1, x2, x3), r, i, a, _one_minus_square(log_a, a)


def _branch_a_fwd(proj, cw, cb, wa2, ba, wx2, bx, lam, *, tc, name, comm=None):
    s = proj.shape[0]
    tc = min(tc, s)

    def body(x_ref, g_ref, cw_ref, cb_ref, wa_ref, ba_ref, wx_ref, bx_ref, lam_ref, rep_ref, h_ref, y_ref,
             xprev, hlast, a_scr, b_scr, ext_scr):
        @pl.when(pl.program_id(1) == 0)
        def _():
            xprev[...] = jnp.zeros_like(xprev)
            hlast[...] = jnp.zeros_like(hlast)

        for t in range(RNN_TILES_PER_STEP):
            cols = lambda ref: ref.at[:, pl.ds(t * LANES, LANES)]
            one_tile(
                cols(x_ref), cols(g_ref), cols(cw_ref), cols(cb_ref), wa_ref.at[t], cols(ba_ref), wx_ref.at[t], cols(bx_ref),
                cols(lam_ref), rep_ref, cols(h_ref), cols(y_ref), cols(xprev), cols(hlast), a_scr.at[t], b_scr.at[t],
                ext_scr.at[t],
            )

    def one_tile(x_ref, g_ref, cw_ref, cb_ref, wa_ref, ba_ref, wx_ref, bx_ref, lam_ref, rep_ref, h_ref, y_ref,
                 xprev, hlast, a_scr, b_scr, ext_scr):
        xc = x_ref[...].astype(F32)
        xr, _, r, i, a, om = _conv_and_gates(
            xc, xprev[...], cw_ref, cb_ref, wa_ref, ba_ref, wx_ref, bx_ref, lam_ref, ext_scr
        )
        xprev[...] = xc[tc - SUBLANES :, :]
        u = jnp.sqrt(om) * (i * xr)
        row8 = lax.broadcasted_iota(jnp.int32, (SUBLANES, LANES), 0)
        first = u[:SUBLANES] + jnp.where(row8 == 0, a[:SUBLANES] * hlast[SUBLANES - 1 : SUBLANES, :], 0.0)
        h = _scan_rows(a, jnp.concatenate([first, u[SUBLANES:]], axis=0), tc, rep_ref, a_scr, b_scr, reverse=False)
        hlast[...] = h[tc - SUBLANES :, :]
        h_ref[...] = h
        y_ref[...] = (h * _gelu(g_ref[...].astype(F32))).astype(BF16)

    wide = RNN_TILES_PER_STEP * LANES
    tile = lambda j, c: (0, j)
    vec = pl.BlockSpec((1, wide), tile)
    mats = pl.BlockSpec((RNN_TILES_PER_STEP, LANES, LANES), lambda j, c: (j, 0, 0))
    repeat = _repeat_matrix(tc)
    return _call(
        body,
        (proj, proj, cw, cb, wa2, ba, wx2, bx, lam, repeat),
        name=name,
        grid=(N_RNN_TILES // RNN_TILES_PER_STEP, s // tc),
        in_specs=[
            pl.BlockSpec((tc, wide), lambda j, c: (c, j)),
            pl.BlockSpec((tc, wide), lambda j, c: (c, D_RNN // wide + j)),
            pl.BlockSpec((CONV_WIDTH, wide), tile),
            vec,
            mats,
            vec,
            mats,
            vec,
            vec,
            pl.BlockSpec(repeat.shape, lambda j, c: (0, 0)),
        ],
        out_specs=[pl.BlockSpec((tc, wide), lambda j, c: (c, j)), pl.BlockSpec((tc, wide), lambda j, c: (c, j))],
        out_shape=[SDS((s, D_RNN), F32), SDS((s, D_RNN), BF16)],
        scratch_shapes=[pltpu.VMEM((SUBLANES, wide), F32)] * 2
        + [pltpu.VMEM((RNN_TILES_PER_STEP, tc, LANES), F32)] * 2
        + [pltpu.VMEM((RNN_TILES_PER_STEP, tc + SUBLANES, LANES), F32)],
        semantics=("parallel", "arbitrary"),
        comm=comm,
    )


def _branch_a_bwd(dy, proj, h, cw, cb, wa2, ba, wx2, bx, lam, wa2t, wx2t, *, tc, name, comm=None):
    s = proj.shape[0]
    tc = min(tc, s)
    nc = s // tc
    halo16 = tc // 16
    halo8 = tc // SUBLANES

    def body(dy_ref, x_ref, xh_ref, g_ref, h_ref, hh_ref, cw_ref, cb_ref, wa_ref, ba_ref, wx_ref, bx_ref, lam_ref,
             wat_ref, wxt_ref, rep_ref, dx_ref, dg_ref, dcw_ref, dcb_ref, dba_ref, dbx_ref, dlam_ref, dwa_ref, dwx_ref,
             carry, dxr_next, a_scr, b_scr, ext_scr):
        cc = pl.program_id(1)
        ct = nc - 1 - cc

        @pl.when(cc == 0)
        def _():
            carry[...] = jnp.zeros_like(carry)
            dxr_next[...] = jnp.zeros_like(dxr_next)
            for ref in (dcw_ref, dcb_ref, dba_ref, dbx_ref, dlam_ref, dwa_ref, dwx_ref):
                ref[...] = jnp.zeros_like(ref)

        for t in range(RNN_TILES_PER_STEP):
            cols = lambda ref: ref.at[:, pl.ds(t * LANES, LANES)]
            one_tile(
                ct, cols(dy_ref), cols(x_ref), cols(xh_ref), cols(g_ref), cols(h_ref), cols(hh_ref), cols(cw_ref), cols(cb_ref),
                wa_ref.at[t], cols(ba_ref), wx_ref.at[t], cols(bx_ref), cols(lam_ref), wat_ref.at[t], wxt_ref.at[t], rep_ref,
                cols(dx_ref), cols(dg_ref), cols(dcw_ref), cols(dcb_ref), cols(dba_ref), cols(dbx_ref), cols(dlam_ref),
                dwa_ref.at[t], dwx_ref.at[t], cols(carry), cols(dxr_next), a_scr.at[t], b_scr.at[t], ext_scr.at[t],
            )

    def one_tile(ct, dy_ref, x_ref, xh_ref, g_ref, h_ref, hh_ref, cw_ref, cb_ref, wa_ref, ba_ref, wx_ref, bx_ref, lam_ref,
                 wat_ref, wxt_ref, rep_ref, dx_ref, dg_ref, dcw_ref, dcb_ref, dba_ref, dbx_ref, dlam_ref, dwa_ref, dwx_ref,
                 carry, dxr_next, a_scr, b_scr, ext_scr):
        xc = x_ref[...].astype(F32)
        xprev = jnp.where(ct > 0, xh_ref[SUBLANES:, :].astype(F32), 0.0)
        xr, (x1, x2, x3), r, i, a, om = _conv_and_gates(
            xc, xprev, cw_ref, cb_ref, wa_ref, ba_ref, wx_ref, bx_ref, lam_ref, ext_scr
        )
        inv_norm = lax.rsqrt(om)
        norm = om * inv_norm
        row = lax.broadcasted_iota(jnp.int32, xc.shape, 0)

        hv = h_ref[...]
        ge, ge_grad = _gelu_and_grad(g_ref[...].astype(F32))
        dyv = dy_ref[...].astype(F32)
        dg_ref[...] = (dyv * hv * ge_grad).astype(dg_ref.dtype)
        dh = dyv * ge

        b = dh + jnp.where(row == tc - 1, carry[0:1, :], 0.0)
        a_next = jnp.where(row < tc - 1, pltpu.roll(a, tc - 1, 0), 0.0)
        gadj = _scan_rows(a_next, b, tc, rep_ref, a_scr, b_scr, reverse=True)
        carry[...] = (a * gadj)[:SUBLANES, :]

        hprev_first = jnp.where(ct > 0, hh_ref[SUBLANES - 1 : SUBLANES, :], 0.0)
        hprev = jnp.where(row >= 1, pltpu.roll(hv, 1, 0), hprev_first)
        da = gadj * hprev
        ix = i * xr
        dnorm = gadj * ix
        di = gadj * norm * xr
        dlog_a = da * a - dnorm * (1.0 - om) * inv_norm
        sp = _softplus_neg(lam_ref[...])
        dr = dlog_a * (-LRU_C * sp)
        dsp = jnp.sum(dlog_a * (-LRU_C * r), axis=0, keepdims=True)
        dlam_ref[...] += dsp * (-_sigmoid(-lam_ref[...]))
        dza = dr * r * (1.0 - r)
        dzx = di * i * (1.0 - i)
        dzab, dzxb = dza.astype(BF16), dzx.astype(BF16)
        dxr = gadj * norm * i + _dot(dzab, wat_ref[...]) + _dot(dzxb, wxt_ref[...])
        xrb = xr.astype(BF16)
        dwa_ref[...] += _dot_tn(xrb, dzab)
        dwx_ref[...] += _dot_tn(xrb, dzxb)
        dba_ref[...] += jnp.sum(dza, axis=0, keepdims=True)
        dbx_ref[...] += jnp.sum(dzx, axis=0, keepdims=True)

        ext = jnp.concatenate([dxr, dxr_next[...]], axis=0)
        dx = (
            dxr * cw_ref[3:4, :]
            + _rows_after(ext, 1, tc) * cw_ref[2:3, :]
            + _rows_after(ext, 2, tc) * cw_ref[1:2, :]
            + _rows_after(ext, 3, tc) * cw_ref[0:1, :]
        )
        dxr_next[...] = dxr[:SUBLANES, :]
        dx_ref[...] = dx.astype(dx_ref.dtype)
        dcb_ref[...] += jnp.sum(dxr, axis=0, keepdims=True)
        dcw_ref[3:4, :] += jnp.sum(dxr * xc, axis=0, keepdims=True)
        dcw_ref[2:3, :] += jnp.sum(dxr * x1, axis=0, keepdims=True)
        dcw_ref[1:2, :] += jnp.sum(dxr * x2, axis=0, keepdims=True)
        dcw_ref[0:1, :] += jnp.sum(dxr * x3, axis=0, keepdims=True)

    wide = RNN_TILES_PER_STEP * LANES
    tile = lambda j, c: (0, j)
    mat = lambda j, c: (j, 0, 0)
    cur = lambda j, c: (nc - 1 - c, j)
    vec = pl.BlockSpec((1, wide), tile)
    matspec = pl.BlockSpec((RNN_TILES_PER_STEP, LANES, LANES), mat)
    repeat = _repeat_matrix(tc)
    return _call(
        body,
        (dy, proj, proj, proj, h, h, cw, cb, wa2, ba, wx2, bx, lam, wa2t, wx2t, repeat),
        name=name,
        grid=(N_RNN_TILES // RNN_TILES_PER_STEP, nc),
        in_specs=[
            pl.BlockSpec((tc, wide), cur),
            pl.BlockSpec((tc, wide), cur),
            pl.BlockSpec((16, wide), lambda j, c: (jnp.maximum((nc - 1 - c) * halo16 - 1, 0), j)),
            pl.BlockSpec((tc, wide), lambda j, c: (nc - 1 - c, D_RNN // wide + j)),
            pl.BlockSpec((tc, wide), cur),
            pl.BlockSpec((SUBLANES, wide), lambda j, c: (jnp.maximum((nc - 1 - c) * halo8 - 1, 0), j)),
            pl.BlockSpec((CONV_WIDTH, wide), tile),
            vec,
            matspec,
            vec,
            matspec,
            vec,
            vec,
            matspec,
            matspec,
            pl.BlockSpec(repeat.shape, lambda j, c: (0, 0)),
        ],
        out_specs=[
            pl.BlockSpec((tc, wide), cur),
            pl.BlockSpec((tc, wide), cur),
            pl.BlockSpec((CONV_WIDTH, wide), tile),
            vec,
            vec,
            vec,
            vec,
            matspec,
            matspec,
        ],
        out_shape=[
            SDS((s, D_RNN), BF16),
            SDS((s, D_RNN), BF16),
            SDS((CONV_WIDTH, D_RNN), F32),
            SDS((1, D_RNN), F32),
            SDS((1, D_RNN), F32),
            SDS((1, D_RNN), F32),
            SDS((1, D_RNN), F32),
            SDS((N_RNN_TILES, LANES, LANES), F32),
            SDS((N_RNN_TILES, LANES, LANES), F32),
        ],
        scratch_shapes=[pltpu.VMEM((SUBLANES, wide), F32)] * 2
        + [pltpu.VMEM((RNN_TILES_PER_STEP, tc, LANES), F32)] * 2
        + [pltpu.VMEM((RNN_TILES_PER_STEP, tc + SUBLANES, LANES), F32)],
        semantics=("parallel", "arbitrary"),
        comm=comm,
    )


def _sgu_specs(tb):
    half = lambda blk: pl.BlockSpec((tb, 512), lambda n: (n, blk))
    return [half(U_BLK512), half(U_BLK512 + 1), half(V_BLK512), half(V_BLK512 + 1)]


def _sgu_normed(v, lng_ref, lnb_ref):
    gv, gv_grad = _gelu_and_grad(v)
    mu = jnp.mean(gv, axis=-1, keepdims=True)
    xc = gv - mu
    rs = lax.rsqrt(jnp.mean(xc * xc, axis=-1, keepdims=True) + EPS)
    xhat = xc * rs
    return xhat * lng_ref[...] + lnb_ref[...], xhat, rs, gv_grad


def _sgu_fwd(proj, lng, lnb, wm, bias, *, tb, name, comm=None):
    s = proj.shape[0]
    tb = min(tb, s)

    def body(u0_ref, u1_ref, v0_ref, v1_ref, lng_ref, lnb_ref, wm_ref, bias_ref, y_ref):
        u = jnp.concatenate([u0_ref[...], u1_ref[...]], axis=1).astype(F32)
        v = jnp.concatenate([v0_ref[...], v1_ref[...]], axis=1).astype(F32)
        gu = _gelu(u)
        vn, _, _, _ = _sgu_normed(v, lng_ref, lnb_ref)
        vnb = vn.astype(BF16)
        for blk in range(tb // SGU_BLOCK):
            rows = slice(blk * SGU_BLOCK, (blk + 1) * SGU_BLOCK)
            for g in range(SGU_GROUPS):
                cols = slice(g * LANES, (g + 1) * LANES)
                mixed = _dot(wm_ref[g], vnb[rows, cols]) + bias_ref[g]
                y_ref[rows, cols] = (gu[rows, cols] * mixed).astype(BF16)

    const2 = lambda n: (0, 0)
    const3 = lambda n: (0, 0, 0)
    return _call(
        body,
        (proj, proj, proj, proj, lng, lnb, wm, bias),
        name=name,
        grid=(s // tb,),
        in_specs=_sgu_specs(tb)
        + [
            pl.BlockSpec((1, D_SGU), const2),
            pl.BlockSpec((1, D_SGU), const2),
            pl.BlockSpec((SGU_GROUPS, SGU_BLOCK, SGU_BLOCK), const3),
            pl.BlockSpec((SGU_GROUPS, SGU_BLOCK, LANES), const3),
        ],
        out_specs=pl.BlockSpec((tb, D_SGU), lambda n: (n, 0)),
        out_shape=SDS((s, D_SGU), BF16),
        semantics=("parallel",),
        comm=comm,
    )


def _sgu_bwd(dy, proj, lng, lnb, wm, wmt, bias, mask, *, tb, name, comm=None):
    s = proj.shape[0]
    tb = min(tb, s)
    nb = s // tb

    def body(dy_ref, u0_ref, u1_ref, v0_ref, v1_ref, lng_ref, lnb_ref, wm_ref, wmt_ref, bias_ref, mask_ref,
             du_ref, dv_ref, dws_ref, dbs_ref, dlng_ref, dlnb_ref, dvn_scr, dbs_acc):
        n = pl.program_id(0)

        @pl.when(n == 0)
        def _():
            dbs_acc[...] = jnp.zeros_like(dbs_acc)
            for ref in (dws_ref, dlng_ref, dlnb_ref):
                ref[...] = jnp.zeros_like(ref)

        u = jnp.concatenate([u0_ref[...], u1_ref[...]], axis=1).astype(F32)
        v = jnp.concatenate([v0_ref[...], v1_ref[...]], axis=1).astype(F32)
        gu, gu_grad = _gelu_and_grad(u)
        vn, xhat, rs, gv_grad = _sgu_normed(v, lng_ref, lnb_ref)
        vnb = vn.astype(BF16)
        dyv = dy_ref[...].astype(F32)
        for blk in range(tb // SGU_BLOCK):
            rows = slice(blk * SGU_BLOCK, (blk + 1) * SGU_BLOCK)
            for g in range(SGU_GROUPS):
                cols = slice(g * LANES, (g + 1) * LANES)
                vt = vnb[rows, cols]
                mixed = _dot(wm_ref[g], vt) + bias_ref[g]
                dyt = dyv[rows, cols]
                du_ref[rows, cols] = (dyt * mixed * gu_grad[rows, cols]).astype(BF16)
                dmix = dyt * gu[rows, cols]
                dmixb = dmix.astype(BF16)
                dvn_scr[rows, cols] = _dot(wmt_ref[g], dmixb)
                dws_ref[g] += _dot_nt(dmixb, vt) * mask_ref[...]
                dbs_acc[g] += dmix
        dvn = dvn_scr[...]
        dlng_ref[...] += jnp.sum(dvn * xhat, axis=0, keepdims=True)
        dlnb_ref[...] += jnp.sum(dvn, axis=0, keepdims=True)
        dxh = dvn * lng_ref[...]
        dgv = rs * (
            dxh - jnp.mean(dxh, axis=-1, keepdims=True) - xhat * jnp.mean(dxh * xhat, axis=-1, keepdims=True)
        )
        dv_ref[...] = (dgv * gv_grad).astype(BF16)

        @pl.when(n == nb - 1)
        def _():
            for g in range(SGU_GROUPS):
                dbs_ref[g] = jnp.broadcast_to(jnp.sum(dbs_acc[g], axis=-1, keepdims=True), (SGU_BLOCK, LANES))

    const2 = lambda n: (0, 0)
    const3 = lambda n: (0, 0, 0)
    gmat = pl.BlockSpec((SGU_GROUPS, SGU_BLOCK, SGU_BLOCK), const3)
    vec = pl.BlockSpec((1, D_SGU), const2)
    act = pl.BlockSpec((tb, D_SGU), lambda n: (n, 0))
    return _call(
        body,
        (dy, proj, proj, proj, proj, lng, lnb, wm, wmt, bias, mask),
        name=name,
        grid=(nb,),
        in_specs=[act] + _sgu_specs(tb) + [vec, vec, gmat, gmat, gmat, pl.BlockSpec((SGU_BLOCK, SGU_BLOCK), const2)],
        out_specs=[act, act, gmat, gmat, vec, vec],
        out_shape=[
            SDS((s, D_SGU), BF16),
            SDS((s, D_SGU), BF16),
            SDS((SGU_GROUPS, SGU_BLOCK, SGU_BLOCK), F32),
            SDS((SGU_GROUPS, SGU_BLOCK, LANES), F32),
            SDS((1, D_SGU), F32),
            SDS((1, D_SGU), F32),
        ],
        scratch_shapes=[pltpu.VMEM((tb, D_SGU), F32), pltpu.VMEM((SGU_GROUPS, SGU_BLOCK, LANES), F32)],
        semantics=("arbitrary",),
        comm=comm,
    )


def _gate_specs(tm):
    half = lambda blk: pl.BlockSpec((tm, 512), lambda i: (i, blk))
    return [half(GA_BLK512), half(GA_BLK512 + 1), half(GB_BLK512), half(GB_BLK512 + 1)]


def _merge_fwd(ya_pre, yb_pre, proj, x, w_ba, w_bb, w_out, *, tm, name, comm=None):
    s = x.shape[0]
    tm = min(tm, s)

    def body(ya_ref, yb_ref, a0, a1, b0, b1, x_ref, wa_ref, wb_ref, wo_ref, x1_ref, yao_ref, ybo_ref):
        ya = _dot(ya_ref[...], wa_ref[...])
        yb = _dot(yb_ref[...], wb_ref[...])
        sa = _sigmoid(jnp.concatenate([a0[...], a1[...]], axis=1).astype(F32))
        sb = _sigmoid(jnp.concatenate([b0[...], b1[...]], axis=1).astype(F32))
        merged = sa * ya + sb * yb
        x1_ref[...] = x_ref[...] + _dot(merged.astype(BF16), wo_ref[...])
        yao_ref[...] = ya.astype(BF16)
        ybo_ref[...] = yb.astype(BF16)

    whole = lambda r: pl.BlockSpec((r, D), lambda i: (0, 0))
    act = pl.BlockSpec((tm, D), lambda i: (i, 0))
    return _call(
        body,
        (ya_pre, yb_pre, proj, proj, proj, proj, x, w_ba, w_bb, w_out),
        name=name,
        grid=(s // tm,),
        in_specs=[pl.BlockSpec((tm, D_RNN), lambda i: (i, 0)), act] + _gate_specs(tm) + [act, whole(D_RNN), whole(D_SGU), whole(D)],
        out_specs=[act, act, act],
        out_shape=[SDS((s, D), F32), SDS((s, D), BF16), SDS((s, D), BF16)],
        semantics=("parallel",),
        comm=comm,
    )


def _merge_bwd(dx1, ya, yb, proj, w_ba, w_bb, w_out, *, tm, name, comm=None):
    s = dx1.shape[0]
    tm = min(tm, s)

    def body(dx_ref, ya_ref, yb_ref, a0, a1, b0, b1, wa_ref, wb_ref, wo_ref,
             mg_ref, dya_ref, dyb_ref, dga_ref, dgb_ref, dyap_ref, dybp_ref):
        dm = _dot_nt(dx_ref[...], wo_ref[...])
        ya = ya_ref[...].astype(F32)
        yb = yb_ref[...].astype(F32)
        sa = _sigmoid(jnp.concatenate([a0[...], a1[...]], axis=1).astype(F32))
        sb = _sigmoid(jnp.concatenate([b0[...], b1[...]], axis=1).astype(F32))
        mg_ref[...] = (sa * ya + sb * yb).astype(BF16)
        dya = (dm * sa).astype(BF16)
        dyb = (dm * sb).astype(BF16)
        dya_ref[...] = dya
        dyb_ref[...] = dyb
        dga_ref[...] = (dm * ya * sa * (1.0 - sa)).astype(BF16)
        dgb_ref[...] = (dm * yb * sb * (1.0 - sb)).astype(BF16)
        dyap_ref[...] = _dot_nt(dya, wa_ref[...]).astype(BF16)
        dybp_ref[...] = _dot_nt(dyb, wb_ref[...]).astype(BF16)

    whole = lambda r: pl.BlockSpec((r, D), lambda i: (0, 0))
    act = pl.BlockSpec((tm, D), lambda i: (i, 0))
    act_rnn = pl.BlockSpec((tm, D_RNN), lambda i: (i, 0))
    return _call(
        body,
        (dx1, ya, yb, proj, proj, proj, proj, w_ba, w_bb, w_out),
        name=name,
        grid=(s // tm,),
        in_specs=[act, act, act] + _gate_specs(tm) + [whole(D_RNN), whole(D_SGU), whole(D)],
        out_specs=[act, act, act, act, act, act_rnn, act],
        out_shape=[SDS((s, D), BF16)] * 5 + [SDS((s, D_RNN), BF16), SDS((s, D_SGU), BF16)],
        semantics=("parallel",),
        comm=comm,
    )


def _ffn_down_loss(a, w, res, g, target, *, tm, name):
    s, k = a.shape
    tm = min(tm, s)

    def body(a_ref, w_ref, r_ref, g_ref, t_ref, dx_ref, dxb_ref, dg_ref, loss_ref):
        @pl.when(pl.program_id(0) == 0)
        def _():
            dg_ref[...] = jnp.zeros_like(dg_ref)
            loss_ref[...] = jnp.zeros_like(loss_ref)

        t = jnp.maximum(a_ref[...].astype(F32), 0.0)
        xv = r_ref[...] + _dot((t * t).astype(BF16), w_ref[...])
        r = lax.rsqrt(jnp.mean(xv * xv, axis=-1, keepdims=True) + EPS)
        xhat = xv * r
        e = xhat * g_ref[...] - t_ref[...]
        loss_ref[...] += 0.5 * jnp.sum(jnp.mean(e * e, axis=-1, keepdims=True), axis=0, keepdims=True)
        dy = e * (1.0 / D)
        dxh = dy * g_ref[...]
        dx = r * (dxh - xhat * jnp.mean(dxh * xhat, axis=-1, keepdims=True))
        dx_ref[...] = dx
        dxb_ref[...] = dx.astype(BF16)
        dg_ref[...] += jnp.sum(dy * xhat, axis=0, keepdims=True)

    act = pl.BlockSpec((tm, D), lambda i: (i, 0))
    vec = pl.BlockSpec((1, D), lambda i: (0, 0))
    return pl.pallas_call(
        body,
        name=name,
        grid=(s // tm,),
        in_specs=[pl.BlockSpec((tm, k), lambda i: (i, 0)), pl.BlockSpec((k, D), lambda i: (0, 0)), act, vec, act],
        out_specs=[act, act, vec, pl.BlockSpec((SUBLANES, LANES), lambda i: (0, 0))],
        out_shape=[SDS((s, D), F32), SDS((s, D), BF16), SDS((1, D), F32), SDS((SUBLANES, LANES), F32)],
        compiler_params=_params("arbitrary"),
    )(a, w, res, g, target)


def _adamw_math(w, g, m, v):
    m2 = ADAM_B1 * m + (1.0 - ADAM_B1) * g
    v2 = ADAM_B2 * v + (1.0 - ADAM_B2) * (g * g)
    m_hat = m2 / (1.0 - ADAM_B1**ADAM_STEP)
    v_hat = v2 / (1.0 - ADAM_B2**ADAM_STEP)
    delta = -ADAM_LR * (m_hat / (jnp.sqrt(v_hat) + ADAM_EPS) + ADAM_WD * w)
    return delta, m2, v2


def _row_tile(rows, cap):
    return max(t for t in range(SUBLANES, min(cap, rows) + 1, SUBLANES) if rows % t == 0)


def _adamw_layers(w, grads, m, v, *, tr, name):
    depth, r, c = w.shape
    tr = _row_tile(r, tr)

    def body(*refs):
        g_refs = refs[:depth]
        w_ref, m_ref, v_ref, g_out, d_ref, mo_ref, vo_ref = refs[depth:]
        for l in range(depth):

            @pl.when(pl.program_id(0) == l)
            def _(l=l):
                g = g_refs[l][...]
                g_out[...] = g
                d_ref[...], mo_ref[...], vo_ref[...] = _adamw_math(w_ref[...], g, m_ref[...], v_ref[...])

    def of_layer(ll):
        return pl.BlockSpec((tr, c), lambda l, i: (jnp.where(l == ll, i, 0), 0))

    stacked = pl.BlockSpec((None, tr, c), lambda l, i: (l, i, 0))
    return pl.pallas_call(
        body,
        name=name,
        grid=(depth, r // tr),
        in_specs=[of_layer(ll) for ll in range(depth)] + [stacked] * 3,
        out_specs=[stacked] * 4,
        out_shape=[SDS((depth, r, c), F32)] * 4,
        compiler_params=_params("parallel", "parallel"),
    )(*grads, w, m, v)


def _adamw_reduced(w, parts, from_chips, m, v, chip, *, tr, name):
    depth, r, _ = w.shape
    tr = _row_tile(r, tr)

    def body(chip_ref, *refs):
        p_refs, c_refs = refs[:depth], refs[depth : 2 * depth]
        w_ref, m_ref, v_ref, g_out, d_ref, mo_ref, vo_ref = refs[2 * depth :]
        for l in range(depth):

            @pl.when(pl.program_id(0) == l)
            def _(l=l):
                got = c_refs[l]
                g = ((p_refs[l][...].astype(F32) + got[0].astype(F32)) + got[1].astype(F32)) + got[2].astype(F32)
                g_out[...] = g
                d_ref[...], mo_ref[...], vo_ref[...] = _adamw_math(w_ref[...], g, m_ref[...], v_ref[...])

    def mine_of_layer(ll):
        return pl.BlockSpec((None, tr, D), lambda l, i, chip_ref: (chip_ref[0], jnp.where(l == ll, i, 0), 0))

    def theirs_of_layer(ll):
        return pl.BlockSpec((3, tr, D), lambda l, i, chip_ref: (0, jnp.where(l == ll, i, 0), 0))

    stacked = pl.BlockSpec((None, tr, D), lambda l, i, chip_ref: (l, i, 0))
    return pl.pallas_call(
        body,
        name=name,
        grid_spec=pltpu.PrefetchScalarGridSpec(
            num_scalar_prefetch=1,
            grid=(depth, r // tr),
            in_specs=[mine_of_layer(ll) for ll in range(depth)]
            + [theirs_of_layer(ll) for ll in range(depth)]
            + [stacked] * 3,
            out_specs=[stacked] * 4,
        ),
        out_shape=[SDS((depth, r, D), F32)] * 4,
        compiler_params=_params("parallel", "parallel"),
    )(chip, *parts, *from_chips, w, m, v)


def _adamw_small(groups, *, name):
    n = len(groups)

    def body(*refs):
        ins, outs = refs[: 4 * n], refs[4 * n :]
        for i in range(n):
            w, g, m, v = (ref[...] for ref in ins[4 * i : 4 * i + 4])
            outs[3 * i][...], outs[3 * i + 1][...], outs[3 * i + 2][...] = _adamw_math(w, g, m, v)

    vmem = pl.BlockSpec(memory_space=pltpu.VMEM)
    outs = pl.pallas_call(
        body,
        name=name,
        in_specs=[vmem] * (4 * n),
        out_specs=[vmem] * (3 * n),
        out_shape=[SDS(grp[0].shape, F32) for grp in groups for _ in range(3)],
        compiler_params=pltpu.CompilerParams(vmem_limit_bytes=VMEM_LIMIT_BYTES),
    )(*[a for grp in groups for a in grp])
    return [tuple(outs[3 * i : 3 * i + 3]) for i in range(n)]


ANY = pl.BlockSpec(memory_space=pl.ANY)


def _position():
    return lax.axis_index("x"), lax.axis_index("y"), lax.axis_index("c")


def _other_chips(x, y):
    return [(1 - x, y), (x, 1 - y), (1 - x, 1 - y)]


class _Comm:
    def __init__(self, inputs, out_shapes, sem_counts, start, finish, aliases=()):
        self.inputs, self.out_shapes, self.sem_counts = list(inputs), list(out_shapes), list(sem_counts)
        self.start, self.finish = start, finish
        self.aliases = list(aliases)

    def sem_shapes(self):
        return [pltpu.SemaphoreType.DMA((n,)) for n in self.sem_counts]


def _merge_comms(comms):
    bounds, i, o, s = [], 0, 0, 0
    for cm in comms:
        bounds.append((i, i + len(cm.inputs), o, o + len(cm.out_shapes), s, s + len(cm.sem_counts)))
        i, o, s = bounds[-1][1], bounds[-1][3], bounds[-1][5]

    def phase(which):
        def run(ins, outs, sems):
            for cm, (i0, i1, o0, o1, s0, s1) in zip(comms, bounds):
                getattr(cm, which)(ins[i0:i1], outs[o0:o1], sems[s0:s1])

        return run

    return _Comm(
        [a for cm in comms for a in cm.inputs],
        [a for cm in comms for a in cm.out_shapes],
        [a for cm in comms for a in cm.sem_counts],
        phase("start"),
        phase("finish"),
        aliases=[(i0 + i, o0 + o) for cm, (i0, _, o0, _, _, _) in zip(comms, bounds) for i, o in cm.aliases],
    )


def _call(body, args, *, semantics, comm=None, **kw):
    if comm is None:
        return pl.pallas_call(body, compiler_params=_params(*semantics), **kw)(*args)
    grid, in_specs, out_specs, out_shape = kw["grid"], kw["in_specs"], kw["out_specs"], kw["out_shape"]
    scratch = list(kw.get("scratch_shapes", ()))
    single = not isinstance(out_shape, (list, tuple))
    core_specs = [out_specs] if single else list(out_specs)
    core_shapes = [out_shape] if single else list(out_shape)
    n_in, n_out, n_scr = len(in_specs), len(core_shapes), len(scratch)
    n_cin, n_cout = len(comm.inputs), len(comm.out_shapes)
    steps = 1
    for g in grid:
        steps *= g

    def hosted(*refs):
        core_in, c_in = refs[:n_in], refs[n_in : n_in + n_cin]
        o0 = n_in + n_cin
        core_out, c_out = refs[o0 : o0 + n_out], refs[o0 + n_out : o0 + n_out + n_cout]
        s0 = o0 + n_out + n_cout
        core_scr, sems = refs[s0 : s0 + n_scr], refs[s0 + n_scr :]
        step = pl.program_id(0)
        for d in range(1, len(grid)):
            step = step * grid[d] + pl.program_id(d)

        @pl.when(step == 0)
        def _():
            comm.start(c_in, c_out, sems)

        body(*core_in, *core_out, *core_scr)

        @pl.when(step == steps - 1)
        def _():
            comm.finish(c_in, c_out, sems)

    outs = pl.pallas_call(
        hosted,
        name=kw["name"],
        grid=grid,
        in_specs=list(in_specs) + [ANY] * n_cin,
        out_specs=core_specs + [ANY] * n_cout,
        out_shape=core_shapes + comm.out_shapes,
        scratch_shapes=scratch + comm.sem_shapes(),
        input_output_aliases={n_in + i: n_out + o for i, o in comm.aliases},
        compiler_params=_params(*(["arbitrary"] * len(grid))),
    )(*args, *comm.inputs)
    return (outs[0] if single else outs[:n_out]), outs[n_out:]


def _comm_only(comm, *, name):
    n_cin, n_cout = len(comm.inputs), len(comm.out_shapes)

    def body(*refs):
        ins, outs, sems = refs[:n_cin], refs[n_cin : n_cin + n_cout], refs[n_cin + n_cout :]
        comm.start(ins, outs, sems)
        comm.finish(ins, outs, sems)

    return pl.pallas_call(
        body,
        name=name,
        in_specs=[ANY] * n_cin,
        out_specs=[ANY] * n_cout,
        out_shape=comm.out_shapes,
        scratch_shapes=comm.sem_shapes(),
    )(*comm.inputs)


def _gather_comm(shards):
    n = len(shards)
    per = 7

    def plan(ins, outs, sems):
        send, recv, local = sems
        x, y, c = _position()
        me, sibling = (x, y, c), (x, y, 1 - c)
        chips = _other_chips(x, y)

        def block(t, px, py, pc):
            return outs[t].at[pl.ds(4 * px + 2 * py + pc, 1)]

        def copy(t, k, blk, to, src=None):
            return pltpu.make_async_remote_copy(
                src_ref=block(t, *blk) if src is None else src,
                dst_ref=block(t, *blk),
                send_sem=send.at[t * per + k],
                recv_sem=recv.at[t * per + k],
                device_id=to,
                device_id_type=MESH,
            )

        mine = [pltpu.make_async_copy(ins[t], block(t, *me), local.at[t]) for t in range(n)]
        to_chips = [copy(t, 1 + j, me, (*chip, c), src=ins[t]) for t in range(n) for j, chip in enumerate(chips)]
        to_sibling = [copy(t, 0, me, sibling, src=ins[t]) for t in range(n)]
        from_chips = [copy(t, 1 + j, (*chip, c), me) for t in range(n) for j, chip in enumerate(chips)]
        passed_on = [copy(t, 4 + j, (*chip, c), sibling) for t in range(n) for j, chip in enumerate(chips)]
        from_sibling = [copy(t, 0, sibling, me) for t in range(n)]
        from_sibling += [copy(t, 4 + j, (*chip, 1 - c), me) for t in range(n) for j, chip in enumerate(chips)]
        return mine, to_chips, to_sibling, from_chips, passed_on, from_sibling

    def start(ins, outs, sems):
        mine, to_chips, to_sibling, _, _, _ = plan(ins, outs, sems)
        for cp in mine + to_chips + to_sibling:
            cp.start()

    def finish(ins, outs, sems):
        mine, to_chips, to_sibling, from_chips, passed_on, from_sibling = plan(ins, outs, sems)
        for arrived, onward in zip(from_chips, passed_on):
            arrived.wait_recv()
            onward.start()
        for cp in from_sibling:
            cp.wait_recv()
        for cp in to_chips + to_sibling + passed_on:
            cp.wait_send()
        for cp in mine:
            cp.wait()

    out_shapes = [SDS((N_DEV,) + sh.shape[1:], sh.dtype) for sh in shards]
    return _Comm(shards, out_shapes, [n * per, n * per, n], start, finish)


def _gather_stage(stage, shards=None, arrived=None):
    n = len(arrived if shards is None else shards)
    targets = {"near": (0, 1), "far": (2,), "first": (0, 1, 2), "pass": (0, 1, 2)}[stage]
    to_sibling = stage in ("near", "first")
    per = len(targets) + to_sibling

    def plan(ins, outs, sems):
        x, y, c = _position()
        me, sibling = (x, y, c), (x, y, 1 - c)
        chips = [_other_chips(x, y)[j] for j in targets]

        def block(t, px, py, pc):
            return outs[t].at[pl.ds(4 * px + 2 * py + pc, 1)]

        def copy(t, k, blk, to, src=None):
            return pltpu.make_async_remote_copy(
                src_ref=block(t, *blk) if src is None else src,
                dst_ref=block(t, *blk),
                send_sem=sems[0].at[t * per + k],
                recv_sem=sems[1].at[t * per + k],
                device_id=to,
                device_id_type=MESH,
            )

        local = []
        if stage == "pass":
            sent = [copy(t, j, (*chip, c), sibling) for t in range(n) for j, chip in enumerate(chips)]
            landing = [copy(t, j, (*chip, 1 - c), me) for t in range(n) for j, chip in enumerate(chips)]
        else:
            sent = [copy(t, j, me, (*chip, c), src=ins[t]) for t in range(n) for j, chip in enumerate(chips)]
            landing = [copy(t, j, (*chip, c), me) for t in range(n) for j, chip in enumerate(chips)]
            if to_sibling:
                local = [pltpu.make_async_copy(ins[t], block(t, *me), sems[2].at[t]) for t in range(n)]
                sent += [copy(t, per - 1, me, sibling, src=ins[t]) for t in range(n)]
                landing += [copy(t, per - 1, sibling, me) for t in range(n)]
        return local, sent, landing

    def start(ins, outs, sems):
        local, sent, _ = plan(ins, outs, sems)
        for cp in local + sent:
            cp.start()

    def finish(ins, outs, sems):
        local, sent, landing = plan(ins, outs, sems)
        for cp in landing:
            cp.wait_recv()
        for cp in sent:
            cp.wait_send()
        for cp in local:
            cp.wait()

    if to_sibling:
        out_shapes = [SDS((N_DEV,) + sh.shape[1:], sh.dtype) for sh in shards]
        return _Comm(shards, out_shapes, [n * per, n * per, n], start, finish)
    out_shapes = [SDS(a.shape, a.dtype) for a in arrived]
    if stage == "pass":
        return _Comm(arrived, out_shapes, [n * per, n * per], start, finish, aliases=[(t, t) for t in range(n)])
    return _Comm(list(shards) + list(arrived), out_shapes, [n * per, n * per], start, finish, aliases=[(n + t, t) for t in range(n)])


def _exchange_comm(arrays, out_shapes, n_copies, copies_of):
    def start(ins, outs, sems):
        for cp in copies_of(ins, outs, *sems):
            cp.start()

    def finish(ins, outs, sems):
        for cp in copies_of(ins, outs, *sems):
            cp.wait()

    return _Comm(arrays, out_shapes, [n_copies, n_copies], start, finish)


def _sibling_comm(grads):
    def copies_of(ins, outs, send, recv):
        x, y, c = _position()
        return [
            pltpu.make_async_remote_copy(
                src_ref=ins[t].at[:, pl.ds(1 - c, 1)],
                dst_ref=outs[t],
                send_sem=send.at[t],
                recv_sem=recv.at[t],
                device_id=(x, y, 1 - c),
                device_id_type=MESH,
            )
            for t in range(len(ins))
        ]

    return _exchange_comm(grads, [SDS((4, 1) + g.shape[2:], g.dtype) for g in grads], len(grads), copies_of)


def _chips_comm(parts):
    def copies_of(ins, outs, send, recv):
        x, y, c = _position()
        return [
            pltpu.make_async_remote_copy(
                src_ref=ins[t].at[pl.ds(2 * px + py, 1)],
                dst_ref=outs[t].at[pl.ds(k, 1)],
                send_sem=send.at[3 * t + k],
                recv_sem=recv.at[3 * t + k],
                device_id=(px, py, c),
                device_id_type=MESH,
            )
            for t in range(len(ins))
            for k, (px, py) in enumerate(_other_chips(x, y))
        ]

    return _exchange_comm(parts, [SDS((3,) + p.shape[1:], p.dtype) for p in parts], 3 * len(parts), copies_of)


def _sum_with_sibling(grad, got, core, *, name):
    rows = grad.shape[2]

    def body(core_ref, a_ref, b_ref, o_ref):
        o_ref[...] = (a_ref[...].astype(F32) + b_ref[...].astype(F32)).astype(o_ref.dtype)

    return pl.pallas_call(
        body,
        name=name,
        grid_spec=pltpu.PrefetchScalarGridSpec(
            num_scalar_prefetch=1,
            grid=(4,),
            in_specs=[
                pl.BlockSpec((None, None, rows, D), lambda q, core_ref: (q, core_ref[0], 0, 0)),
                pl.BlockSpec((None, None, rows, D), lambda q, core_ref: (q, 0, 0, 0)),
            ],
            out_specs=pl.BlockSpec((None, rows, D), lambda q, core_ref: (q, 0, 0)),
        ),
        out_shape=SDS((4, rows, D), grad.dtype),
        compiler_params=_params("parallel"),
    )(core, grad, got)


def _sum_chips(part, got, chip, *, name):
    rows = part.shape[1]

    def body(chip_ref, a_ref, b_ref, o_ref):
        o_ref[...] = ((a_ref[...].astype(F32) + b_ref[0].astype(F32)) + b_ref[1].astype(F32)) + b_ref[2].astype(F32)

    return pl.pallas_call(
        body,
        name=name,
        grid_spec=pltpu.PrefetchScalarGridSpec(
            num_scalar_prefetch=1,
            grid=(1,),
            in_specs=[
                pl.BlockSpec((None, rows, D), lambda i, chip_ref: (chip_ref[0], 0, 0)),
                pl.BlockSpec((3, rows, D), lambda i, chip_ref: (0, 0, 0)),
            ],
            out_specs=pl.BlockSpec((rows, D), lambda i, chip_ref: (0, 0)),
        ),
        out_shape=SDS((rows, D), F32),
        compiler_params=_params("arbitrary"),
    )(chip, part, got)


def _all_reduce_small(pack, *, name):
    rows = pack.shape[1]

    def body(in_ref, out_ref, from_sibling, part, from_chips, send, recv):
        x, y, c = _position()
        me, sibling = (x, y, c), (x, y, 1 - c)
        chips = _other_chips(x, y)
        waiting = []

        def copy(k, src, dst, to):
            return pltpu.make_async_remote_copy(
                src_ref=src, dst_ref=dst, send_sem=send.at[k], recv_sem=recv.at[k], device_id=to, device_id_type=MESH
            )

        def exchange(copies):
            for cp in copies:
                cp.start()
            for cp in copies:
                cp.wait_recv()
            waiting.extend(copies)

        def block(px, py, pc):
            return out_ref.at[4 * px + 2 * py + pc]

        exchange([copy(q, in_ref.at[2 * q + 1 - c], from_sibling.at[q], sibling) for q in range(4)])
        for q in range(4):
            part[q] = in_ref[2 * q + c] + from_sibling[q]
        exchange([copy(4 + k, part.at[2 * px + py], from_chips.at[k], (px, py, c)) for k, (px, py) in enumerate(chips)])
        out_ref[4 * x + 2 * y + c] = ((part[2 * x + y] + from_chips[0]) + from_chips[1]) + from_chips[2]
        exchange(
            [copy(7, block(*me), block(*me), sibling)]
            + [copy(8 + k, block(*me), block(*me), (px, py, c)) for k, (px, py) in enumerate(chips)]
        )
        exchange([copy(11 + k, block(px, py, c), block(px, py, c), sibling) for k, (px, py) in enumerate(chips)])
        for cp in waiting:
            cp.wait_send()

    vmem = pl.BlockSpec(memory_space=pltpu.VMEM)
    return pl.pallas_call(
        body,
        name=name,
        in_specs=[vmem],
        out_specs=vmem,
        out_shape=SDS(pack.shape, F32),
        scratch_shapes=[
            pltpu.VMEM((4, rows, D), F32),
            pltpu.VMEM((4, rows, D), F32),
            pltpu.VMEM((3, rows, D), F32),
            pltpu.SemaphoreType.DMA((14,)),
            pltpu.SemaphoreType.DMA((14,)),
        ],
        compiler_params=pltpu.CompilerParams(vmem_limit_bytes=VMEM_LIMIT_BYTES),
    )(pack)


def _pack(arrays, rows):
    flat = jnp.concatenate([a.reshape(-1).astype(F32) for a in arrays])
    return jnp.pad(flat, (0, rows * D - flat.shape[0])).reshape(rows, D)


def _unpack(pack, shapes):
    flat = pack.reshape(-1)
    out, off = [], 0
    for sh in shapes:
        size = 1
        for dim in sh:
            size *= dim
        out.append(flat[off : off + size].reshape(sh))
        off += size
    return out


def _block_diag_pairs(w):
    w = w.reshape(N_RNN_TILES, 2, HEAD_DIM, HEAD_DIM)
    z = jnp.zeros_like(w[:, 0])
    top = jnp.concatenate([w[:, 0], z], axis=2)
    bot = jnp.concatenate([z, w[:, 1]], axis=2)
    return jnp.concatenate([top, bot], axis=1)


def _diag_blocks(w2):
    a = w2[:, :HEAD_DIM, :HEAD_DIM]
    b = w2[:, HEAD_DIM:, HEAD_DIM:]
    return jnp.stack([a, b], axis=1).reshape(RNN_HEADS, HEAD_DIM, HEAD_DIM)


BIG = ("w_in", "w_branch_a", "w_branch_b", "w_out", "w_up", "w_down")
TRANSPOSED = ("w_in", "w_up")
SMALL = (
    "norm_mix_g", "conv_w", "conv_b", "lru_w_a", "lru_b_a", "lru_w_x", "lru_b_x", "lru_lambda",
    "sgu_ln_g", "sgu_ln_b", "sgu_w_s", "sgu_b_s", "norm_ffn_g", "final_norm_g",
)
WEIGHTS = (
    "norm_mix_g", "w_in", "conv_w", "conv_b", "lru_w_a", "lru_b_a", "lru_w_x", "lru_b_x", "lru_lambda", "sgu_ln_g",
    "sgu_ln_b", "sgu_w_s", "sgu_b_s", "w_branch_a", "w_branch_b", "w_out", "norm_ffn_g", "w_up", "w_down", "final_norm_g",
)

TM = 512
TM_NT = 1024
TN_IN = 1664
TN_UP = 2048
TKA = 512
TKA_PIECES = 256
TC = 512
TC_BWD = 1024
TB = 256
TB_BWD = 512
TR = 256


_BRANCHES_0 = [(0, "w_branch_a"), (0, "w_branch_b"), (0, "w_out")]
_BRANCHES_1 = [(1, "w_branch_a"), (1, "w_branch_b"), (1, "w_out")]
GATHERS_RIDING = (
    {
        "in_proj": [("first", _BRANCHES_0), ("near", [(0, "w_up")])],
        "branch_a_fwd": [("pass", _BRANCHES_0), ("far", [(0, "w_up")]), ("near", [(1, "w_in")])],
        "sgu_fwd": [("pass", [(0, "w_up")]), ("near", [(0, "w_down")])],
        "merge_fwd": [("far", [(0, "w_down")])],
        "ffn_up": [("pass", [(0, "w_down")]), ("far", [(1, "w_in")])],
        "ffn_down": [("pass", [(1, "w_in")]), ("near", _BRANCHES_1)],
    },
    {
        "in_proj": [("far", _BRANCHES_1), ("near", [(1, "w_down")])],
        "branch_a_fwd": [("pass", _BRANCHES_1), ("far", [(1, "w_down")]), ("first", [(1, "w_up")])],
        "sgu_fwd": [("pass", [(1, "w_down"), (1, "w_up")])],
    },
)


def _layer_forward(l, x, p, w, shards, arriving, loss_head=None):
    def run(key, fn, *args, **kw):
        riding = GATHERS_RIDING[l].get(key, ())
        if not riding:
            return fn(*args, **kw)
        comms = []
        for stage, units in riding:
            mine = [shards[l2][n2] for l2, n2 in units] if stage != "pass" else None
            left = [arriving.pop(unit) for unit in units] if stage in ("far", "pass") else None
            comms.append(_gather_stage(stage, shards=mine, arrived=left))
        out, got = fn(*args, comm=_merge_comms(comms), **kw)
        got = list(got)
        for stage, units in riding:
            for l2, n2 in units:
                if stage == "pass":
                    w[l2][n2] = got.pop(0).reshape(-1, D)
                else:
                    arriving[l2, n2] = got.pop(0)
        return out

    proj, h = run("in_proj", _norm_matmul_nt, x, p["norm_mix_g"], w[l]["w_in"], tm=TM_NT, tn=TN_IN, name=f"in_proj_{l}")
    hseq, ya_pre = run(
        "branch_a_fwd", _branch_a_fwd, proj, p["conv_w"], p["conv_b"], p["wa2"], p["lru_b_a"], p["wx2"], p["lru_b_x"],
        p["lru_lambda"], tc=TC, name=f"branch_a_fwd_{l}",
    )
    yb_pre = run("sgu_fwd", _sgu_fwd, proj, p["sgu_ln_g"], p["sgu_ln_b"], p["wm"], p["sgu_bias"], tb=TB, name=f"sgu_fwd_{l}")
    x1, ya, yb = run(
        "merge_fwd", _merge_fwd, ya_pre, yb_pre, proj, x, w[l]["w_branch_a"], w[l]["w_branch_b"], w[l]["w_out"], tm=TM,
        name=f"merge_fwd_{l}",
    )
    f_pre, h2 = run("ffn_up", _norm_matmul_nt, x1, p["norm_ffn_g"], w[l]["w_up"], tm=TM_NT, tn=TN_UP, name=f"ffn_up_{l}")
    saved = dict(x=x, h=h, proj=proj, hseq=hseq, ya_pre=ya_pre, yb_pre=yb_pre, ya=ya, yb=yb, x1=x1, h2=h2, f_pre=f_pre)
    if loss_head is None:
        return run("ffn_down", _matmul_nn_res, f_pre, w[l]["w_down"], x1, relu2=True, tm=TM, name=f"ffn_down_{l}"), saved
    return _ffn_down_loss(f_pre, w[l]["w_down"], x1, *loss_head, tm=TM, name=f"ffn_down_loss_{l}"), saved


def _layer_backward(l, dx2, dx2b, sv, p, w, core, waiting, last):
    parts, from_chips = {}, {}

    def by_device(g):
        return g.reshape(4, 2, -1, D)

    def with_sibling(name, g, got):
        parts[name] = _sum_with_sibling(by_device(g), got, core, name=f"sum_sibling_{name}_{l}")

    df_pre = _matmul_nt_drelu2(dx2b, w["w_down"], sv["f_pre"], tm=TM_NT, tn=TN_UP, name=f"ffn_down_bwd_{l}")
    g_down = _matmul_tn([sv["f_pre"]], dx2b, relu2=True, tka=TKA, name=f"grad_w_down_{l}")
    g_up, (got,) = _matmul_tn(
        [df_pre], sv["h2"], relu2=False, tka=TKA, name=f"grad_w_up_{l}", comm=_sibling_comm([by_device(g_down)])
    )
    with_sibling("w_down", g_down, got)
    (dx1, dx1b, g_norm_ffn), (got,) = _matmul_nn_rmsnorm_bwd(
        [df_pre], w["w_up"], sv["x1"], p["norm_ffn_g"], dx2, tm=TM, name=f"ffn_up_bwd_{l}",
        comm=_sibling_comm([by_device(g_up)]),
    )
    with_sibling("w_up", g_up, got)
    (merged, dya, dyb, dga, dgb, dya_pre, dyb_pre), (from_chips[l, "w_up"],) = _merge_bwd(
        dx1b, sv["ya"], sv["yb"], sv["proj"], w["w_branch_a"], w["w_branch_b"], w["w_out"], tm=TM, name=f"merge_bwd_{l}",
        comm=_chips_comm([parts["w_up"]]),
    )
    g_out, g_ba, g_bb = _matmuls_tn(
        [(merged, dx1b), (sv["ya_pre"], dya), (sv["yb_pre"], dyb)], ts=2 * TM, name=f"grad_w_branches_{l}"
    )
    branch = (("w_out", g_out), ("w_branch_a", g_ba), ("w_branch_b", g_bb))
    (du, dv, g_ws, g_bs, g_lng, g_lnb), got = _sgu_bwd(
        dyb_pre, sv["proj"], p["sgu_ln_g"], p["sgu_ln_b"], p["wm"], p["wmt"], p["sgu_bias"], p["mask"], tb=TB_BWD,
        name=f"sgu_bwd_{l}",
        comm=_merge_comms([_sibling_comm([by_device(g) for _, g in branch]), _chips_comm([parts["w_down"]])]),
    )
    from_chips[l, "w_down"] = got[-1]
    for (name, g), landed in zip(branch, got):
        with_sibling(name, g, landed)
    riding = [((l, name), parts[name]) for name, _ in branch] + list(waiting)
    (dxr, dgr, g_cw, g_cb, g_ba_, g_bx, g_lam, g_wa2, g_wx2), got = _branch_a_bwd(
        dya_pre, sv["proj"], sv["hseq"], p["conv_w"], p["conv_b"], p["wa2"], p["lru_b_a"], p["wx2"], p["lru_b_x"],
        p["lru_lambda"], p["wa2t"], p["wx2t"], tc=TC_BWD, name=f"branch_a_bwd_{l}", comm=_chips_comm([part for _, part in riding]),
    )
    for (key, _), landed in zip(riding, got):
        from_chips[key] = landed
    dproj = [dxr, dgr, du, dv, dga, dgb]
    g_in = _matmul_tn(dproj, sv["h"], relu2=False, tka=TKA_PIECES, name=f"grad_w_in_{l}")
    if last:
        (got,) = _comm_only(_sibling_comm([by_device(g_in)]), name=f"grad_w_in_to_sibling_{l}")
        with_sibling("w_in", g_in, got)
        riding = _chips_comm([parts["w_in"]])
    else:
        riding = _sibling_comm([by_device(g_in)])
    (dx, dxb, g_norm_mix), (got,) = _matmul_nn_rmsnorm_bwd(
        dproj, w["w_in"], sv["x"], p["norm_mix_g"], dx1, tm=TM, name=f"in_proj_bwd_{l}", comm=riding
    )
    if last:
        from_chips[l, "w_in"] = got
    else:
        with_sibling("w_in", g_in, got)
    small = dict(
        norm_mix_g=g_norm_mix[0], conv_w=g_cw, conv_b=g_cb[0], lru_w_a=_diag_blocks(g_wa2), lru_b_a=g_ba_.reshape(RNN_HEADS, HEAD_DIM),
        lru_w_x=_diag_blocks(g_wx2), lru_b_x=g_bx.reshape(RNN_HEADS, HEAD_DIM), lru_lambda=g_lam[0], sgu_ln_g=g_lng[0],
        sgu_ln_b=g_lnb[0], sgu_w_s=g_ws, sgu_b_s=g_bs[:, :, 0], norm_ffn_g=g_norm_ffn[0],
    )
    return dx, dxb, small, parts, from_chips


def _prepare_small(l, given):
    chunk_id = jnp.arange(SGU_BLOCK) // CHUNK
    mask = (chunk_id[:, None] >= chunk_id[None, :]).astype(F32)
    wm = given["sgu_w_s"][l] * mask
    wa2 = _block_diag_pairs(given["lru_w_a"][l])
    wx2 = _block_diag_pairs(given["lru_w_x"][l])
    row = lambda a: a.reshape(1, -1)
    return dict(
        norm_mix_g=row(given["norm_mix_g"][l]),
        norm_ffn_g=row(given["norm_ffn_g"][l]),
        conv_w=given["conv_w_full"][l],
        conv_b=row(given["conv_b"][l]),
        wa2=wa2.astype(BF16),
        wx2=wx2.astype(BF16),
        wa2t=jnp.swapaxes(wa2, 1, 2).astype(BF16),
        wx2t=jnp.swapaxes(wx2, 1, 2).astype(BF16),
        lru_b_a=row(given["lru_b_a"][l]),
        lru_b_x=row(given["lru_b_x"][l]),
        lru_lambda=row(given["lru_lambda"][l]),
        sgu_ln_g=row(given["sgu_ln_g"][l]),
        sgu_ln_b=row(given["sgu_ln_b"][l]),
        wm=wm.astype(BF16),
        wmt=jnp.swapaxes(wm, 1, 2).astype(BF16),
        sgu_bias=jnp.broadcast_to(given["sgu_b_s"][l][:, :, None], (SGU_GROUPS, SGU_BLOCK, LANES)),
        mask=mask,
    )


def _step(given):
    x_idx, y_idx, c_idx = _position()
    dev = 4 * x_idx + 2 * y_idx + c_idx
    core = c_idx.astype(jnp.int32).reshape(1)
    chip = (2 * x_idx + y_idx).astype(jnp.int32).reshape(1)

    def rows_first(name, a):
        return jnp.swapaxes(a, 1, 2) if name in TRANSPOSED else a

    shards = []
    for l in range(DEPTH):
        shards.append({name: rows_first(name, given[name])[l].astype(BF16)[None] for name in BIG})
    conv_mine = given["conv_w"].reshape(1, DEPTH * CONV_WIDTH, D_RNN // N_DEV)
    w_in_first, conv_all = _comm_only(_gather_comm([shards[0]["w_in"], conv_mine]), name="gather_first")
    weights = [{"w_in": w_in_first.reshape(-1, D)}, {}]
    conv_all = conv_all.reshape(N_DEV, DEPTH, CONV_WIDTH, D_RNN // N_DEV)
    given = dict(given, conv_w_full=jnp.moveaxis(conv_all, 0, 2).reshape(DEPTH, CONV_WIDTH, D_RNN))

    small_params = [_prepare_small(l, given) for l in range(DEPTH)]
    x = given["x"][0]
    saved, arriving = [], {}
    loss_head = (given["final_norm_g"].reshape(1, D), given["loss_target"][0])
    for l in range(DEPTH):
        x, sv = _layer_forward(
            l, x, small_params[l], weights, shards, arriving, loss_head=loss_head if l == DEPTH - 1 else None
        )
        saved.append(sv)
    dx, dxb, g_final, loss = x
    small_grads, parts, from_chips, waiting = [None] * DEPTH, [None] * DEPTH, {}, []
    for l in reversed(range(DEPTH)):
        dx, dxb, small_grads[l], parts[l], got = _layer_backward(
            l, dx, dxb, saved[l], small_params[l], weights[l], core, waiting, last=l == 0
        )
        from_chips.update(got)
        waiting = [((l, "w_in"), parts[l]["w_in"])]

    small_list = []
    for name in SMALL[:-1]:
        small_list.append(jnp.stack([small_grads[l][name] for l in range(DEPTH)]))
    small_list += [g_final[0], loss[0, :1]]
    small_shapes = [a.shape for a in small_list]
    pack = _pack(small_list, SMALL_ROWS).reshape(N_DEV, SMALL_ROWS_PER_DEV, D)
    summed = _unpack(_all_reduce_small(pack, name="all_reduce_small"), small_shapes)
    loss_total = summed[-1][0]
    grads = dict(zip(SMALL, summed[:-1]))
    cw = grads["conv_w"].reshape(DEPTH, CONV_WIDTH, N_DEV, D_RNN // N_DEV)
    grads["conv_w"] = lax.dynamic_index_in_dim(cw, dev, axis=2, keepdims=False)

    delta, new_m, new_v = {}, {}, {}
    for name in BIG:
        w, m, v = given[name], given["m_" + name], given["v_" + name]
        mine = [parts[l][name] for l in range(DEPTH)]
        theirs = [from_chips[l, name] for l in range(DEPTH)]
        if name == "w_up":
            sums = [_sum_chips(mine[l], theirs[l], chip, name=f"sum_chips_{name}_{l}").T for l in range(DEPTH)]
            out = _adamw_layers(w, sums, m, v, tr=TR, name=f"adamw_{name}")
        else:
            out = _adamw_reduced(
                rows_first(name, w), mine, theirs, rows_first(name, m), rows_first(name, v), chip, tr=TR, name=f"adamw_{name}"
            )
            out = [rows_first(name, a) for a in out]
        grads[name], delta[name], new_m[name], new_v[name] = out
    two_d = lambda a: a.reshape(1, -1) if a.ndim == 1 else a
    groups = [tuple(two_d(a) for a in (given[n], grads[n], given["m_" + n], given["v_" + n])) for n in SMALL]
    for n, (d, m2, v2) in zip(SMALL, _adamw_small(groups, name="adamw_small")):
        shape = given[n].shape
        delta[n], new_m[n], new_v[n] = d.reshape(shape), m2.reshape(shape), v2.reshape(shape)

    return (
        loss_total, dx[None],
        *[grads[n] for n in WEIGHTS], *[delta[n] for n in WEIGHTS], *[new_m[n] for n in WEIGHTS], *[new_v[n] for n in WEIGHTS],
    )


def kernel(x, norm_mix_g, w_in, conv_w, conv_b, lru_w_a, lru_b_a, lru_w_x, lru_b_x, lru_lambda, sgu_ln_g, sgu_ln_b, sgu_w_s, sgu_b_s, w_branch_a, w_branch_b, w_out, norm_ffn_g, w_up, w_down, final_norm_g, loss_target, m_norm_mix_g, m_w_in, m_conv_w, m_conv_b, m_lru_w_a, m_lru_b_a, m_lru_w_x, m_lru_b_x, m_lru_lambda, m_sgu_ln_g, m_sgu_ln_b, m_sgu_w_s, m_sgu_b_s, m_w_branch_a, m_w_branch_b, m_w_out, m_norm_ffn_g, m_w_up, m_w_down, m_final_norm_g, v_norm_mix_g, v_w_in, v_conv_w, v_conv_b, v_lru_w_a, v_lru_b_a, v_lru_w_x, v_lru_b_x, v_lru_lambda, v_sgu_ln_g, v_sgu_ln_b, v_sgu_w_s, v_sgu_b_s, v_w_branch_a, v_w_branch_b, v_w_out, v_norm_ffn_g, v_w_up, v_w_down, v_final_norm_g):
    return _step(dict(locals()))
```

```python
import jax
import jax.numpy as jnp
from jax import lax
from jax.experimental import pallas as pl
from jax.experimental.pallas import tpu as pltpu

F32 = jnp.float32
BF16 = jnp.bfloat16
SDS = jax.ShapeDtypeStruct
MESH = pl.DeviceIdType.MESH

D = 1024
D_RNN = 1280
D_SGU = 1024
D_IN = 2 * D_RNN + 2 * D_SGU + 2 * D
DEPTH = 2
RNN_HEADS = 20
HEAD_DIM = 64
CONV_WIDTH = 4
LRU_C = 8.0
SGU_GROUPS = 8
SGU_BLOCK = 128
CHUNK = 64
EPS = 1e-6
N_DEV = 8

ADAM_LR = 0.001
ADAM_B1 = 0.9
ADAM_B2 = 0.999
ADAM_EPS = 1e-08
ADAM_WD = 0.01
ADAM_STEP = 10

LANES = 128
SUBLANES = 8
VMEM_LIMIT_BYTES = 56 * 1024 * 1024

N_RNN_TILES = D_RNN // LANES
RNN_TILES_PER_STEP = 5
U_BLK512 = (2 * D_RNN) // 512
V_BLK512 = (2 * D_RNN + D_SGU) // 512
GA_BLK512 = (2 * D_RNN + 2 * D_SGU) // 512
GB_BLK512 = (2 * D_RNN + 2 * D_SGU + D) // 512

SMALL_ROWS_PER_DEV = 80
SMALL_ROWS = N_DEV * SMALL_ROWS_PER_DEV


def _params(*sem):
    return pltpu.CompilerParams(dimension_semantics=sem, vmem_limit_bytes=VMEM_LIMIT_BYTES)


def _sigmoid(x):
    return 0.5 + 0.5 * jnp.tanh(0.5 * x)


_GELU_C = 0.7978845608028654
_GELU_K = 0.044715


def _gelu(x):
    t = jnp.tanh(_GELU_C * (x + _GELU_K * x * x * x))
    return 0.5 * x * (1.0 + t)


def _gelu_and_grad(x):
    t = jnp.tanh(_GELU_C * (x + _GELU_K * x * x * x))
    val = 0.5 * x * (1.0 + t)
    grad = 0.5 * (1.0 + t) + 0.5 * x * (1.0 - t * t) * _GELU_C * (1.0 + 3.0 * _GELU_K * x * x)
    return val, grad


def _one_minus_square(log_a, a):
    return -jnp.tanh(log_a) * (1.0 + a * a)


def _dot(a, b):
    return jnp.dot(a, b, preferred_element_type=F32)


def _dot_nt(a, b):
    return lax.dot_general(a, b, (((1,), (1,)), ((), ())), preferred_element_type=F32)


def _dot_tn(a, b):
    return lax.dot_general(a, b, (((0,), (0,)), ((), ())), preferred_element_type=F32)


def _norm_matmul_nt(x, g, w, *, tm, tn, name, comm=None):
    s, n = x.shape[0], w.shape[0]
    tm, tn = min(tm, s), min(tn, n)

    def body(x_ref, g_ref, w_ref, o_ref, h_ref):
        @pl.when(pl.program_id(1) == 0)
        def _():
            xv = x_ref[...]
            r = lax.rsqrt(jnp.mean(xv * xv, axis=-1, keepdims=True) + EPS)
            h_ref[...] = (xv * r * g_ref[...]).astype(BF16)

        o_ref[...] = _dot_nt(h_ref[...], w_ref[...]).astype(o_ref.dtype)

    return _call(
        body,
        (x, g, w),
        name=name,
        grid=(s // tm, n // tn),
        in_specs=[
            pl.BlockSpec((tm, D), lambda i, j: (i, 0)),
            pl.BlockSpec((1, D), lambda i, j: (0, 0)),
            pl.BlockSpec((tn, D), lambda i, j: (j, 0)),
        ],
        out_specs=[pl.BlockSpec((tm, tn), lambda i, j: (i, j)), pl.BlockSpec((tm, D), lambda i, j: (i, 0))],
        out_shape=[SDS((s, n), BF16), SDS((s, D), BF16)],
        semantics=("parallel", "arbitrary"),
        comm=comm,
    )


def _matmul_nn_res(a, w, res, *, relu2, tm, name, comm=None):
    s, k = a.shape
    tm = min(tm, s)

    def body(a_ref, w_ref, r_ref, o_ref):
        av = a_ref[...]
        if relu2:
            t = jnp.maximum(av.astype(F32), 0.0)
            av = (t * t).astype(BF16)
        o_ref[...] = r_ref[...] + _dot(av, w_ref[...])

    return _call(
        body,
        (a, w, res),
        name=name,
        grid=(s // tm,),
        in_specs=[
            pl.BlockSpec((tm, k), lambda i: (i, 0)),
            pl.BlockSpec((k, D), lambda i: (0, 0)),
            pl.BlockSpec((tm, D), lambda i: (i, 0)),
        ],
        out_specs=pl.BlockSpec((tm, D), lambda i: (i, 0)),
        out_shape=SDS((s, D), F32),
        semantics=("parallel",),
        comm=comm,
    )


def _matmul_nt_drelu2(a, w, pre, *, tm, tn, name):
    s, n = a.shape[0], w.shape[0]
    tm, tn = min(tm, s), min(tn, n)

    def body(a_ref, w_ref, p_ref, o_ref):
        d = _dot_nt(a_ref[...], w_ref[...])
        o_ref[...] = (d * (2.0 * jnp.maximum(p_ref[...].astype(F32), 0.0))).astype(o_ref.dtype)

    return pl.pallas_call(
        body,
        name=name,
        grid=(s // tm, n // tn),
        in_specs=[
            pl.BlockSpec((tm, D), lambda i, j: (i, 0)),
            pl.BlockSpec((tn, D), lambda i, j: (j, 0)),
            pl.BlockSpec((tm, tn), lambda i, j: (i, j)),
        ],
        out_specs=pl.BlockSpec((tm, tn), lambda i, j: (i, j)),
        out_shape=SDS((s, n), BF16),
        compiler_params=_params("parallel", "arbitrary"),
    )(a, w, pre)


def _matmul_tn(a_list, b, *, relu2, tka, name, comm=None):
    s = b.shape[0]
    n = len(a_list)
    nblk = [a.shape[1] // tka for a in a_list]
    starts = [sum(nblk[:p]) for p in range(n)]

    def body(*refs):
        a_refs, b_ref, o_ref = refs[:n], refs[n], refs[n + 1]
        i = pl.program_id(0)
        for p in range(n):

            @pl.when((i >= starts[p]) & (i < starts[p] + nblk[p]))
            def _(p=p):
                av = a_refs[p][...]
                if relu2:
                    t = jnp.maximum(av.astype(F32), 0.0)
                    av = (t * t).astype(BF16)
                o_ref[...] = _dot_tn(av, b_ref[...]).astype(o_ref.dtype)

    def piece_spec(p):
        return pl.BlockSpec((s, tka), lambda i: (0, jnp.clip(i - starts[p], 0, nblk[p] - 1)))

    return _call(
        body,
        (*a_list, b),
        name=name,
        grid=(sum(nblk),),
        in_specs=[piece_spec(p) for p in range(n)] + [pl.BlockSpec((s, D), lambda i: (0, 0))],
        out_specs=pl.BlockSpec((tka, D), lambda i: (i, 0)),
        out_shape=SDS((sum(nblk) * tka, D), BF16),
        semantics=("parallel",),
        comm=comm,
    )


def _matmuls_tn(pairs, *, ts, name):
    s = pairs[0][0].shape[0]
    ts = min(ts, s)
    n = len(pairs)
    steps = s // ts

    def body(*refs):
        ins, outs, accs = refs[: 2 * n], refs[2 * n : 3 * n], refs[3 * n :]
        for p in range(n):
            part = _dot_tn(ins[2 * p][...], ins[2 * p + 1][...])

            @pl.when(pl.program_id(0) == 0)
            def _(p=p, part=part):
                accs[p][...] = part

            @pl.when(pl.program_id(0) > 0)
            def _(p=p, part=part):
                accs[p][...] += part

        @pl.when(pl.program_id(0) == steps - 1)
        def _():
            for p in range(n):
                outs[p][...] = accs[p][...].astype(BF16)

    widths = [a.shape[1] for a, _ in pairs]
    in_specs = []
    for wd in widths:
        in_specs += [pl.BlockSpec((ts, wd), lambda i: (i, 0)), pl.BlockSpec((ts, D), lambda i: (i, 0))]
    return pl.pallas_call(
        body,
        name=name,
        grid=(steps,),
        in_specs=in_specs,
        out_specs=[pl.BlockSpec((wd, D), lambda i: (0, 0)) for wd in widths],
        out_shape=[SDS((wd, D), BF16) for wd in widths],
        scratch_shapes=[pltpu.VMEM((wd, D), F32) for wd in widths],
        compiler_params=_params("arbitrary"),
    )(*[x for pair in pairs for x in pair])


def _matmul_nn_rmsnorm_bwd(a_list, w, x, g, res, *, tm, name, comm=None):
    s = x.shape[0]
    tm = min(tm, s)
    n = len(a_list)
    widths = [a.shape[1] for a in a_list]
    offs = [sum(widths[:p]) for p in range(n)]
    k = sum(widths)

    def body(*refs):
        a_refs = refs[:n]
        w_ref, x_ref, g_ref, r_ref, dx_ref, dxb_ref, dg_ref = refs[n:]

        @pl.when(pl.program_id(0) == 0)
        def _():
            dg_ref[...] = jnp.zeros_like(dg_ref)

        dh = _dot(a_refs[0][...], w_ref[0 : widths[0], :])
        for p in range(1, n):
            dh += _dot(a_refs[p][...], w_ref[offs[p] : offs[p] + widths[p], :])
        xv = x_ref[...]
        r = lax.rsqrt(jnp.mean(xv * xv, axis=-1, keepdims=True) + EPS)
        xhat = xv * r
        dxh = dh * g_ref[...]
        dx = r_ref[...] + r * (dxh - xhat * jnp.mean(dxh * xhat, axis=-1, keepdims=True))
        dx_ref[...] = dx
        dxb_ref[...] = dx.astype(BF16)
        dg_ref[...] += jnp.sum(dh * xhat, axis=0, keepdims=True)

    act = pl.BlockSpec((tm, D), lambda i: (i, 0))
    vec = pl.BlockSpec((1, D), lambda i: (0, 0))
    return _call(
        body,
        (*a_list, w, x, g, res),
        name=name,
        grid=(s // tm,),
        in_specs=[pl.BlockSpec((tm, wd), lambda i: (i, 0)) for wd in widths]
        + [pl.BlockSpec((k, D), lambda i: (0, 0), pipeline_mode=pl.Buffered(1)), act, vec, act],
        out_specs=[act, act, vec],
        out_shape=[SDS((s, D), F32), SDS((s, D), BF16), SDS((1, D), F32)],
        semantics=("arbitrary",),
        comm=comm,
    )


def _rows_after(ext, k, n):
    return pltpu.roll(ext, n + SUBLANES - k, 0)[:n, :]


def _scan_forward(a, b, n):
    row = lax.broadcasted_iota(jnp.int32, a.shape, 0)
    d = 1
    while d < n:
        if d < SUBLANES:
            m = row >= d
            a_s = jnp.where(m, pltpu.roll(a, d, 0), 1.0)
            b_s = jnp.where(m, pltpu.roll(b, d, 0), 0.0)
            b = a * b_s + b
            a = a * a_s
        else:
            b = jnp.concatenate([b[:d], a[d:] * b[: n - d] + b[d:]], axis=0)
            a = jnp.concatenate([a[:d], a[d:] * a[: n - d]], axis=0)
        d *= 2
    return a, b


def _scan_backward(a, b, n):
    row = lax.broadcasted_iota(jnp.int32, a.shape, 0)
    d = 1
    while d < n:
        if d < SUBLANES:
            m = row < n - d
            a_s = jnp.where(m, pltpu.roll(a, n - d, 0), 1.0)
            b_s = jnp.where(m, pltpu.roll(b, n - d, 0), 0.0)
            b = a * b_s + b
            a = a * a_s
        else:
            b = jnp.concatenate([a[: n - d] * b[d:] + b[: n - d], b[n - d :]], axis=0)
            a = jnp.concatenate([a[: n - d] * a[d:], a[n - d :]], axis=0)
        d *= 2
    return b


def _repeat_matrix(n):
    groups = n // SUBLANES
    return (jnp.arange(n)[:, None] // SUBLANES == jnp.arange(3 * groups)[None, :] % groups).astype(BF16)


def _scan_rows(a, b, n, repeat_ref, a_scr, b_scr, reverse):
    groups = n // SUBLANES
    a3 = a.reshape(groups, SUBLANES, LANES)
    b3 = b.reshape(groups, SUBLANES, LANES)
    sub = lax.broadcasted_iota(jnp.int32, a3.shape, 1)
    for d in (1, 2, 4):
        m = (sub < SUBLANES - d) if reverse else (sub >= d)
        shift = SUBLANES - d if reverse else d
        a_s = jnp.where(m, pltpu.roll(a3, shift, 1), 1.0)
        b_s = jnp.where(m, pltpu.roll(b3, shift, 1), 0.0)
        b3 = a3 * b_s + b3
        a3 = a3 * a_s
    a_scr[...] = a3.reshape(n, LANES)
    b_scr[...] = b3.reshape(n, LANES)
    edge = 0 if reverse else SUBLANES - 1
    a_tot = a_scr[pl.ds(edge, groups, stride=SUBLANES), :]
    b_tot = b_scr[pl.ds(edge, groups, stride=SUBLANES), :]
    row = lax.broadcasted_iota(jnp.int32, a_tot.shape, 0)
    if reverse:
        through = _scan_backward(a_tot, b_tot, groups)
        entering = jnp.where(row < groups - 1, pltpu.roll(through, groups - 1, 0), 0.0)
    else:
        _, through = _scan_forward(a_tot, b_tot, groups)
        entering = jnp.where(row >= 1, pltpu.roll(through, 1, 0), 0.0)
    hi = entering.astype(BF16)
    rest = entering - hi.astype(F32)
    mid = rest.astype(BF16)
    lo = (rest - mid.astype(F32)).astype(BF16)
    repeated = _dot(repeat_ref[...], jnp.concatenate([hi, mid, lo], axis=0))
    return b_scr[...] + a_scr[...] * repeated


def _softplus_neg(lam):
    z = -lam
    return jnp.maximum(z, 0.0) + jnp.log1p(jnp.exp(-jnp.abs(z)))


def _conv_and_gates(xc, xprev, cw_ref, cb_ref, wa_ref, ba_ref, wx_ref, bx_ref, lam_ref, ext_scr):
    n = xc.shape[0]
    ext_scr[:SUBLANES, :] = xprev
    ext_scr[SUBLANES:, :] = xc
    x1, x2, x3 = (ext_scr[pl.ds(SUBLANES - k, n), :] for k in (1, 2, 3))
    xr = cb_ref[...] + x3 * cw_ref[0:1, :] + x2 * cw_ref[1:2, :] + x1 * cw_ref[2:3, :] + xc * cw_ref[3:4, :]
    xrb = xr.astype(BF16)
    r = _sigmoid(_dot(xrb, wa_ref[...]) + ba_ref[...])
    i = _sigmoid(_dot(xrb, wx_ref[...]) + bx_ref[...])
    sp = _softplus_neg(lam_ref[...])
    log_a = (-LRU_C * r) * sp
    a = jnp.exp(log_a)
    return xr, (x1, x2, x3), r, i, a, _one_minus_square(log_a, a)


def _branch_a_fwd(proj, cw, cb, wa2, ba, wx2, bx, lam, *, tc, name, comm=None):
    s = proj.shape[0]
    tc = min(tc, s)

    def body(x_ref, g_ref, cw_ref, cb_ref, wa_ref, ba_ref, wx_ref, bx_ref, lam_ref, rep_ref, h_ref, y_ref,
             xprev, hlast, a_scr, b_scr, ext_scr):
        @pl.when(pl.program_id(1) == 0)
        def _():
            xprev[...] = jnp.zeros_like(xprev)
            hlast[...] = jnp.zeros_like(hlast)

        for t in range(RNN_TILES_PER_STEP):
            cols = lambda ref: ref.at[:, pl.ds(t * LANES, LANES)]
            one_tile(
                cols(x_ref), cols(g_ref), cols(cw_ref), cols(cb_ref), wa_ref.at[t], cols(ba_ref), wx_ref.at[t], cols(bx_ref),
                cols(lam_ref), rep_ref, cols(h_ref), cols(y_ref), cols(xprev), cols(hlast), a_scr.at[t], b_scr.at[t],
                ext_scr.at[t],
            )

    def one_tile(x_ref, g_ref, cw_ref, cb_ref, wa_ref, ba_ref, wx_ref, bx_ref, lam_ref, rep_ref, h_ref, y_ref,
                 xprev, hlast, a_scr, b_scr, ext_scr):
        xc = x_ref[...].astype(F32)
        xr, _, r, i, a, om = _conv_and_gates(
            xc, xprev[...], cw_ref, cb_ref, wa_ref, ba_ref, wx_ref, bx_ref, lam_ref, ext_scr
        )
        xprev[...] = xc[tc - SUBLANES :, :]
        u = jnp.sqrt(om) * (i * xr)
        row8 = lax.broadcasted_iota(jnp.int32, (SUBLANES, LANES), 0)
        first = u[:SUBLANES] + jnp.where(row8 == 0, a[:SUBLANES] * hlast[SUBLANES - 1 : SUBLANES, :], 0.0)
        h = _scan_rows(a, jnp.concatenate([first, u[SUBLANES:]], axis=0), tc, rep_ref, a_scr, b_scr, reverse=False)
        hlast[...] = h[tc - SUBLANES :, :]
        h_ref[...] = h
        y_ref[...] = (h * _gelu(g_ref[...].astype(F32))).astype(BF16)

    wide = RNN_TILES_PER_STEP * LANES
    tile = lambda j, c: (0, j)
    vec = pl.BlockSpec((1, wide), tile)
    mats = pl.BlockSpec((RNN_TILES_PER_STEP, LANES, LANES), lambda j, c: (j, 0, 0))
    repeat = _repeat_matrix(tc)
    return _call(
        body,
        (proj, proj, cw, cb, wa2, ba, wx2, bx, lam, repeat),
        name=name,
        grid=(N_RNN_TILES // RNN_TILES_PER_STEP, s // tc),
        in_specs=[
            pl.BlockSpec((tc, wide), lambda j, c: (c, j)),
            pl.BlockSpec((tc, wide), lambda j, c: (c, D_RNN // wide + j)),
            pl.BlockSpec((CONV_WIDTH, wide), tile),
            vec,
            mats,
            vec,
            mats,
            vec,
            vec,
            pl.BlockSpec(repeat.shape, lambda j, c: (0, 0)),
        ],
        out_specs=[pl.BlockSpec((tc, wide), lambda j, c: (c, j)), pl.BlockSpec((tc, wide), lambda j, c: (c, j))],
        out_shape=[SDS((s, D_RNN), F32), SDS((s, D_RNN), BF16)],
        scratch_shapes=[pltpu.VMEM((SUBLANES, wide), F32)] * 2
        + [pltpu.VMEM((RNN_TILES_PER_STEP, tc, LANES), F32)] * 2
        + [pltpu.VMEM((RNN_TILES_PER_STEP, tc + SUBLANES, LANES), F32)],
        semantics=("parallel", "arbitrary"),
        comm=comm,
    )


def _branch_a_bwd(dy, proj, h, cw, cb, wa2, ba, wx2, bx, lam, wa2t, wx2t, *, tc, name, comm=None):
    s = proj.shape[0]
    tc = min(tc, s)
    nc = s // tc
    halo16 = tc // 16
    halo8 = tc // SUBLANES

    def body(dy_ref, x_ref, xh_ref, g_ref, h_ref, hh_ref, cw_ref, cb_ref, wa_ref, ba_ref, wx_ref, bx_ref, lam_ref,
             wat_ref, wxt_ref, rep_ref, dx_ref, dg_ref, dcw_ref, dcb_ref, dba_ref, dbx_ref, dlam_ref, dwa_ref, dwx_ref,
             carry, dxr_next, a_scr, b_scr, ext_scr):
        cc = pl.program_id(1)
        ct = nc - 1 - cc

        @pl.when(cc == 0)
        def _():
            carry[...] = jnp.zeros_like(carry)
            dxr_next[...] = jnp.zeros_like(dxr_next)
            for ref in (dcw_ref, dcb_ref, dba_ref, dbx_ref, dlam_ref, dwa_ref, dwx_ref):
                ref[...] = jnp.zeros_like(ref)

        for t in range(RNN_TILES_PER_STEP):
            cols = lambda ref: ref.at[:, pl.ds(t * LANES, LANES)]
            one_tile(
                ct, cols(dy_ref), cols(x_ref), cols(xh_ref), cols(g_ref), cols(h_ref), cols(hh_ref), cols(cw_ref), cols(cb_ref),
                wa_ref.at[t], cols(ba_ref), wx_ref.at[t], cols(bx_ref), cols(lam_ref), wat_ref.at[t], wxt_ref.at[t], rep_ref,
                cols(dx_ref), cols(dg_ref), cols(dcw_ref), cols(dcb_ref), cols(dba_ref), cols(dbx_ref), cols(dlam_ref),
                dwa_ref.at[t], dwx_ref.at[t], cols(carry), cols(dxr_next), a_scr.at[t], b_scr.at[t], ext_scr.at[t],
            )

    def one_tile(ct, dy_ref, x_ref, xh_ref, g_ref, h_ref, hh_ref, cw_ref, cb_ref, wa_ref, ba_ref, wx_ref, bx_ref, lam_ref,
                 wat_ref, wxt_ref, rep_ref, dx_ref, dg_ref, dcw_ref, dcb_ref, dba_ref, dbx_ref, dlam_ref, dwa_ref, dwx_ref,
                 carry, dxr_next, a_scr, b_scr, ext_scr):
        xc = x_ref[...].astype(F32)
        xprev = jnp.where(ct > 0, xh_ref[SUBLANES:, :].astype(F32), 0.0)
        xr, (x1, x2, x3), r, i, a, om = _conv_and_gates(
            xc, xprev, cw_ref, cb_ref, wa_ref, ba_ref, wx_ref, bx_ref, lam_ref, ext_scr
        )
        inv_norm = lax.rsqrt(om)
        norm = om * inv_norm
        row = lax.broadcasted_iota(jnp.int32, xc.shape, 0)

        hv = h_ref[...]
        ge, ge_grad = _gelu_and_grad(g_ref[...].astype(F32))
        dyv = dy_ref[...].astype(F32)
        dg_ref[...] = (dyv * hv * ge_grad).astype(dg_ref.dtype)
        dh = dyv * ge

        b = dh + jnp.where(row == tc - 1, carry[0:1, :], 0.0)
        a_next = jnp.where(row < tc - 1, pltpu.roll(a, tc - 1, 0), 0.0)
        gadj = _scan_rows(a_next, b, tc, rep_ref, a_scr, b_scr, reverse=True)
        carry[...] = (a * gadj)[:SUBLANES, :]

        hprev_first = jnp.where(ct > 0, hh_ref[SUBLANES - 1 : SUBLANES, :], 0.0)
        hprev = jnp.where(row >= 1, pltpu.roll(hv, 1, 0), hprev_first)
        da = gadj * hprev
        ix = i * xr
        dnorm = gadj * ix
        di = gadj * norm * xr
        dlog_a = da * a - dnorm * (1.0 - om) * inv_norm
        sp = _softplus_neg(lam_ref[...])
        dr = dlog_a * (-LRU_C * sp)
        dsp = jnp.sum(dlog_a * (-LRU_C * r), axis=0, keepdims=True)
        dlam_ref[...] += dsp * (-_sigmoid(-lam_ref[...]))
        dza = dr * r * (1.0 - r)
        dzx = di * i * (1.0 - i)
        dzab, dzxb = dza.astype(BF16), dzx.astype(BF16)
        dxr = gadj * norm * i + _dot(dzab, wat_ref[...]) + _dot(dzxb, wxt_ref[...])
        xrb = xr.astype(BF16)
        dwa_ref[...] += _dot_tn(xrb, dzab)
        dwx_ref[...] += _dot_tn(xrb, dzxb)
        dba_ref[...] += jnp.sum(dza, axis=0, keepdims=True)
        dbx_ref[...] += jnp.sum(dzx, axis=0, keepdims=True)

        ext = jnp.concatenate([dxr, dxr_next[...]], axis=0)
        dx = (
            dxr * cw_ref[3:4, :]
            + _rows_after(ext, 1, tc) * cw_ref[2:3, :]
            + _rows_after(ext, 2, tc) * cw_ref[1:2, :]
            + _rows_after(ext, 3, tc) * cw_ref[0:1, :]
        )
        dxr_next[...] = dxr[:SUBLANES, :]
        dx_ref[...] = dx.astype(dx_ref.dtype)
        dcb_ref[...] += jnp.sum(dxr, axis=0, keepdims=True)
        dcw_ref[3:4, :] += jnp.sum(dxr * xc, axis=0, keepdims=True)
        dcw_ref[2:3, :] += jnp.sum(dxr * x1, axis=0, keepdims=True)
        dcw_ref[1:2, :] += jnp.sum(dxr * x2, axis=0, keepdims=True)
        dcw_ref[0:1, :] += jnp.sum(dxr * x3, axis=0, keepdims=True)

    wide = RNN_TILES_PER_STEP * LANES
    tile = lambda j, c: (0, j)
    mat = lambda j, c: (j, 0, 0)
    cur = lambda j, c: (nc - 1 - c, j)
    vec = pl.BlockSpec((1, wide), tile)
    matspec = pl.BlockSpec((RNN_TILES_PER_STEP, LANES, LANES), mat)
    repeat = _repeat_matrix(tc)
    return _call(
        body,
        (dy, proj, proj, proj, h, h, cw, cb, wa2, ba, wx2, bx, lam, wa2t, wx2t, repeat),
        name=name,
        grid=(N_RNN_TILES // RNN_TILES_PER_STEP, nc),
        in_specs=[
            pl.BlockSpec((tc, wide), cur),
            pl.BlockSpec((tc, wide), cur),
            pl.BlockSpec((16, wide), lambda j, c: (jnp.maximum((nc - 1 - c) * halo16 - 1, 0), j)),
            pl.BlockSpec((tc, wide), lambda j, c: (nc - 1 - c, D_RNN // wide + j)),
            pl.BlockSpec((tc, wide), cur),
            pl.BlockSpec((SUBLANES, wide), lambda j, c: (jnp.maximum((nc - 1 - c) * halo8 - 1, 0), j)),
            pl.BlockSpec((CONV_WIDTH, wide), tile),
            vec,
            matspec,
            vec,
            matspec,
            vec,
            vec,
            matspec,
            matspec,
            pl.BlockSpec(repeat.shape, lambda j, c: (0, 0)),
        ],
        out_specs=[
            pl.BlockSpec((tc, wide), cur),
            pl.BlockSpec((tc, wide), cur),
            pl.BlockSpec((CONV_WIDTH, wide), tile),
            vec,
            vec,
            vec,
            vec,
            matspec,
            matspec,
        ],
        out_shape=[
            SDS((s, D_RNN), BF16),
            SDS((s, D_RNN), BF16),
            SDS((CONV_WIDTH, D_RNN), F32),
            SDS((1, D_RNN), F32),
            SDS((1, D_RNN), F32),
            SDS((1, D_RNN), F32),
            SDS((1, D_RNN), F32),
            SDS((N_RNN_TILES, LANES, LANES), F32),
            SDS((N_RNN_TILES, LANES, LANES), F32),
        ],
        scratch_shapes=[pltpu.VMEM((SUBLANES, wide), F32)] * 2
        + [pltpu.VMEM((RNN_TILES_PER_STEP, tc, LANES), F32)] * 2
        + [pltpu.VMEM((RNN_TILES_PER_STEP, tc + SUBLANES, LANES), F32)],
        semantics=("parallel", "arbitrary"),
        comm=comm,
    )


def _sgu_specs(tb):
    half = lambda blk: pl.BlockSpec((tb, 512), lambda n: (n, blk))
    return [half(U_BLK512), half(U_BLK512 + 1), half(V_BLK512), half(V_BLK512 + 1)]


def _sgu_normed(v, lng_ref, lnb_ref):
    gv, gv_grad = _gelu_and_grad(v)
    mu = jnp.mean(gv, axis=-1, keepdims=True)
    xc = gv - mu
    rs = lax.rsqrt(jnp.mean(xc * xc, axis=-1, keepdims=True) + EPS)
    xhat = xc * rs
    return xhat * lng_ref[...] + lnb_ref[...], xhat, rs, gv_grad


def _sgu_fwd(proj, lng, lnb, wm, bias, *, tb, name, comm=None):
    s = proj.shape[0]
    tb = min(tb, s)

    def body(u0_ref, u1_ref, v0_ref, v1_ref, lng_ref, lnb_ref, wm_ref, bias_ref, y_ref):
        u = jnp.concatenate([u0_ref[...], u1_ref[...]], axis=1).astype(F32)
        v = jnp.concatenate([v0_ref[...], v1_ref[...]], axis=1).astype(F32)
        gu = _gelu(u)
        vn, _, _, _ = _sgu_normed(v, lng_ref, lnb_ref)
        vnb = vn.astype(BF16)
        for blk in range(tb // SGU_BLOCK):
            rows = slice(blk * SGU_BLOCK, (blk + 1) * SGU_BLOCK)
            for g in range(SGU_GROUPS):
                cols = slice(g * LANES, (g + 1) * LANES)
                mixed = _dot(wm_ref[g], vnb[rows, cols]) + bias_ref[g]
                y_ref[rows, cols] = (gu[rows, cols] * mixed).astype(BF16)

    const2 = lambda n: (0, 0)
    const3 = lambda n: (0, 0, 0)
    return _call(
        body,
        (proj, proj, proj, proj, lng, lnb, wm, bias),
        name=name,
        grid=(s // tb,),
        in_specs=_sgu_specs(tb)
        + [
            pl.BlockSpec((1, D_SGU), const2),
            pl.BlockSpec((1, D_SGU), const2),
            pl.BlockSpec((SGU_GROUPS, SGU_BLOCK, SGU_BLOCK), const3),
            pl.BlockSpec((SGU_GROUPS, SGU_BLOCK, LANES), const3),
        ],
        out_specs=pl.BlockSpec((tb, D_SGU), lambda n: (n, 0)),
        out_shape=SDS((s, D_SGU), BF16),
        semantics=("parallel",),
        comm=comm,
    )


def _sgu_bwd(dy, proj, lng, lnb, wm, wmt, bias, mask, *, tb, name, comm=None):
    s = proj.shape[0]
    tb = min(tb, s)
    nb = s // tb

    def body(dy_ref, u0_ref, u1_ref, v0_ref, v1_ref, lng_ref, lnb_ref, wm_ref, wmt_ref, bias_ref, mask_ref,
             du_ref, dv_ref, dws_ref, dbs_ref, dlng_ref, dlnb_ref, dvn_scr, dbs_acc):
        n = pl.program_id(0)

        @pl.when(n == 0)
        def _():
            dbs_acc[...] = jnp.zeros_like(dbs_acc)
            for ref in (dws_ref, dlng_ref, dlnb_ref):
                ref[...] = jnp.zeros_like(ref)

        u = jnp.concatenate([u0_ref[...], u1_ref[...]], axis=1).astype(F32)
        v = jnp.concatenate([v0_ref[...], v1_ref[...]], axis=1).astype(F32)
        gu, gu_grad = _gelu_and_grad(u)
        vn, xhat, rs, gv_grad = _sgu_normed(v, lng_ref, lnb_ref)
        vnb = vn.astype(BF16)
        dyv = dy_ref[...].astype(F32)
        for blk in range(tb // SGU_BLOCK):
            rows = slice(blk * SGU_BLOCK, (blk + 1) * SGU_BLOCK)
            for g in range(SGU_GROUPS):
                cols = slice(g * LANES, (g + 1) * LANES)
                vt = vnb[rows, cols]
                mixed = _dot(wm_ref[g], vt) + bias_ref[g]
                dyt = dyv[rows, cols]
                du_ref[rows, cols] = (dyt * mixed * gu_grad[rows, cols]).astype(BF16)
                dmix = dyt * gu[rows, cols]
                dmixb = dmix.astype(BF16)
                dvn_scr[rows, cols] = _dot(wmt_ref[g], dmixb)
                dws_ref[g] += _dot_nt(dmixb, vt) * mask_ref[...]
                dbs_acc[g] += dmix
        dvn = dvn_scr[...]
        dlng_ref[...] += jnp.sum(dvn * xhat, axis=0, keepdims=True)
        dlnb_ref[...] += jnp.sum(dvn, axis=0, keepdims=True)
        dxh = dvn * lng_ref[...]
        dgv = rs * (
            dxh - jnp.mean(dxh, axis=-1, keepdims=True) - xhat * jnp.mean(dxh * xhat, axis=-1, keepdims=True)
        )
        dv_ref[...] = (dgv * gv_grad).astype(BF16)

        @pl.when(n == nb - 1)
        def _():
            for g in range(SGU_GROUPS):
                dbs_ref[g] = jnp.broadcast_to(jnp.sum(dbs_acc[g], axis=-1, keepdims=True), (SGU_BLOCK, LANES))

    const2 = lambda n: (0, 0)
    const3 = lambda n: (0, 0, 0)
    gmat = pl.BlockSpec((SGU_GROUPS, SGU_BLOCK, SGU_BLOCK), const3)
    vec = pl.BlockSpec((1, D_SGU), const2)
    act = pl.BlockSpec((tb, D_SGU), lambda n: (n, 0))
    return _call(
        body,
        (dy, proj, proj, proj, proj, lng, lnb, wm, wmt, bias, mask),
        name=name,
        grid=(nb,),
        in_specs=[act] + _sgu_specs(tb) + [vec, vec, gmat, gmat, gmat, pl.BlockSpec((SGU_BLOCK, SGU_BLOCK), const2)],
        out_specs=[act, act, gmat, gmat, vec, vec],
        out_shape=[
            SDS((s, D_SGU), BF16),
            SDS((s, D_SGU), BF16),
            SDS((SGU_GROUPS, SGU_BLOCK, SGU_BLOCK), F32),
            SDS((SGU_GROUPS, SGU_BLOCK, LANES), F32),
            SDS((1, D_SGU), F32),
            SDS((1, D_SGU), F32),
        ],
        scratch_shapes=[pltpu.VMEM((tb, D_SGU), F32), pltpu.VMEM((SGU_GROUPS, SGU_BLOCK, LANES), F32)],
        semantics=("arbitrary",),
        comm=comm,
    )


def _gate_specs(tm):
    half = lambda blk: pl.BlockSpec((tm, 512), lambda i: (i, blk))
    return [half(GA_BLK512), half(GA_BLK512 + 1), half(GB_BLK512), half(GB_BLK512 + 1)]


def _merge_fwd(ya_pre, yb_pre, proj, x, w_ba, w_bb, w_out, *, tm, name, comm=None):
    s = x.shape[0]
    tm = min(tm, s)

    def body(ya_ref, yb_ref, a0, a1, b0, b1, x_ref, wa_ref, wb_ref, wo_ref, x1_ref, yao_ref, ybo_ref):
        ya = _dot(ya_ref[...], wa_ref[...])
        yb = _dot(yb_ref[...], wb_ref[...])
        sa = _sigmoid(jnp.concatenate([a0[...], a1[...]], axis=1).astype(F32))
        sb = _sigmoid(jnp.concatenate([b0[...], b1[...]], axis=1).astype(F32))
        merged = sa * ya + sb * yb
        x1_ref[...] = x_ref[...] + _dot(merged.astype(BF16), wo_ref[...])
        yao_ref[...] = ya.astype(BF16)
        ybo_ref[...] = yb.astype(BF16)

    whole = lambda r: pl.BlockSpec((r, D), lambda i: (0, 0))
    act = pl.BlockSpec((tm, D), lambda i: (i, 0))
    return _call(
        body,
        (ya_pre, yb_pre, proj, proj, proj, proj, x, w_ba, w_bb, w_out),
        name=name,
        grid=(s // tm,),
        in_specs=[pl.BlockSpec((tm, D_RNN), lambda i: (i, 0)), act] + _gate_specs(tm) + [act, whole(D_RNN), whole(D_SGU), whole(D)],
        out_specs=[act, act, act],
        out_shape=[SDS((s, D), F32), SDS((s, D), BF16), SDS((s, D), BF16)],
        semantics=("parallel",),
        comm=comm,
    )


def _merge_bwd(dx1, ya, yb, proj, w_ba, w_bb, w_out, *, tm, name, comm=None):
    s = dx1.shape[0]
    tm = min(tm, s)

    def body(dx_ref, ya_ref, yb_ref, a0, a1, b0, b1, wa_ref, wb_ref, wo_ref,
             mg_ref, dya_ref, dyb_ref, dga_ref, dgb_ref, dyap_ref, dybp_ref):
        dm = _dot_nt(dx_ref[...], wo_ref[...])
        ya = ya_ref[...].astype(F32)
        yb = yb_ref[...].astype(F32)
        sa = _sigmoid(jnp.concatenate([a0[...], a1[...]], axis=1).astype(F32))
        sb = _sigmoid(jnp.concatenate([b0[...], b1[...]], axis=1).astype(F32))
        mg_ref[...] = (sa * ya + sb * yb).astype(BF16)
        dya = (dm * sa).astype(BF16)
        dyb = (dm * sb).astype(BF16)
        dya_ref[...] = dya
        dyb_ref[...] = dyb
        dga_ref[...] = (dm * ya * sa * (1.0 - sa)).astype(BF16)
        dgb_ref[...] = (dm * yb * sb * (1.0 - sb)).astype(BF16)
        dyap_ref[...] = _dot_nt(dya, wa_ref[...]).astype(BF16)
        dybp_ref[...] = _dot_nt(dyb, wb_ref[...]).astype(BF16)

    whole = lambda r: pl.BlockSpec((r, D), lambda i: (0, 0))
    act = pl.BlockSpec((tm, D), lambda i: (i, 0))
    act_rnn = pl.BlockSpec((tm, D_RNN), lambda i: (i, 0))
    return _call(
        body,
        (dx1, ya, yb, proj, proj, proj, proj, w_ba, w_bb, w_out),
        name=name,
        grid=(s // tm,),
        in_specs=[act, act, act] + _gate_specs(tm) + [whole(D_RNN), whole(D_SGU), whole(D)],
        out_specs=[act, act, act, act, act, act_rnn, act],
        out_shape=[SDS((s, D), BF16)] * 5 + [SDS((s, D_RNN), BF16), SDS((s, D_SGU), BF16)],
        semantics=("parallel",),
        comm=comm,
    )


def _ffn_down_loss(a, w, res, g, target, *, tm, name):
    s, k = a.shape
    tm = min(tm, s)

    def body(a_ref, w_ref, r_ref, g_ref, t_ref, dx_ref, dxb_ref, dg_ref, loss_ref):
        @pl.when(pl.program_id(0) == 0)
        def _():
            dg_ref[...] = jnp.zeros_like(dg_ref)
            loss_ref[...] = jnp.zeros_like(loss_ref)

        t = jnp.maximum(a_ref[...].astype(F32), 0.0)
        xv = r_ref[...] + _dot((t * t).astype(BF16), w_ref[...])
        r = lax.rsqrt(jnp.mean(xv * xv, axis=-1, keepdims=True) + EPS)
        xhat = xv * r
        e = xhat * g_ref[...] - t_ref[...]
        loss_ref[...] += 0.5 * jnp.sum(jnp.mean(e * e, axis=-1, keepdims=True), axis=0, keepdims=True)
        dy = e * (1.0 / D)
        dxh = dy * g_ref[...]
        dx = r * (dxh - xhat * jnp.mean(dxh * xhat, axis=-1, keepdims=True))
        dx_ref[...] = dx
        dxb_ref[...] = dx.astype(BF16)
        dg_ref[...] += jnp.sum(dy * xhat, axis=0, keepdims=True)

    act = pl.BlockSpec((tm, D), lambda i: (i, 0))
    vec = pl.BlockSpec((1, D), lambda i: (0, 0))
    return pl.pallas_call(
        body,
        name=name,
        grid=(s // tm,),
        in_specs=[pl.BlockSpec((tm, k), lambda i: (i, 0)), pl.BlockSpec((k, D), lambda i: (0, 0)), act, vec, act],
        out_specs=[act, act, vec, pl.BlockSpec((SUBLANES, LANES), lambda i: (0, 0))],
        out_shape=[SDS((s, D), F32), SDS((s, D), BF16), SDS((1, D), F32), SDS((SUBLANES, LANES), F32)],
        compiler_params=_params("arbitrary"),
    )(a, w, res, g, target)


def _adamw_math(w, g, m, v):
    m2 = ADAM_B1 * m + (1.0 - ADAM_B1) * g
    v2 = ADAM_B2 * v + (1.0 - ADAM_B2) * (g * g)
    m_hat = m2 / (1.0 - ADAM_B1**ADAM_STEP)
    v_hat = v2 / (1.0 - ADAM_B2**ADAM_STEP)
    delta = -ADAM_LR * (m_hat / (jnp.sqrt(v_hat) + ADAM_EPS) + ADAM_WD * w)
    return delta, m2, v2


def _row_tile(rows, cap):
    return max(t for t in range(SUBLANES, min(cap, rows) + 1, SUBLANES) if rows % t == 0)


def _adamw_layers(w, grads, m, v, *, tr, name):
    depth, r, c = w.shape
    tr = _row_tile(r, tr)

    def body(*refs):
        g_refs = refs[:depth]
        w_ref, m_ref, v_ref, g_out, d_ref, mo_ref, vo_ref = refs[depth:]
        for l in range(depth):

            @pl.when(pl.program_id(0) == l)
            def _(l=l):
                g = g_refs[l][...]
                g_out[...] = g
                d_ref[...], mo_ref[...], vo_ref[...] = _adamw_math(w_ref[...], g, m_ref[...], v_ref[...])

    def of_layer(ll):
        return pl.BlockSpec((tr, c), lambda l, i: (jnp.where(l == ll, i, 0), 0))

    stacked = pl.BlockSpec((None, tr, c), lambda l, i: (l, i, 0))
    return pl.pallas_call(
        body,
        name=name,
        grid=(depth, r // tr),
        in_specs=[of_layer(ll) for ll in range(depth)] + [stacked] * 3,
        out_specs=[stacked] * 4,
        out_shape=[SDS((depth, r, c), F32)] * 4,
        compiler_params=_params("parallel", "parallel"),
    )(*grads, w, m, v)


def _adamw_reduced(w, parts, from_chips, m, v, chip, *, tr, name):
    depth, r, _ = w.shape
    tr = _row_tile(r, tr)

    def body(chip_ref, *refs):
        p_refs, c_refs = refs[:depth], refs[depth : 2 * depth]
        w_ref, m_ref, v_ref, g_out, d_ref, mo_ref, vo_ref = refs[2 * depth :]
        for l in range(depth):

            @pl.when(pl.program_id(0) == l)
            def _(l=l):
                got = c_refs[l]
                g = ((p_refs[l][...].astype(F32) + got[0].astype(F32)) + got[1].astype(F32)) + got[2].astype(F32)
                g_out[...] = g
                d_ref[...], mo_ref[...], vo_ref[...] = _adamw_math(w_ref[...], g, m_ref[...], v_ref[...])

    def mine_of_layer(ll):
        return pl.BlockSpec((None, tr, D), lambda l, i, chip_ref: (chip_ref[0], jnp.where(l == ll, i, 0), 0))

    def theirs_of_layer(ll):
        return pl.BlockSpec((3, tr, D), lambda l, i, chip_ref: (0, jnp.where(l == ll, i, 0), 0))

    stacked = pl.BlockSpec((None, tr, D), lambda l, i, chip_ref: (l, i, 0))
    return pl.pallas_call(
        body,
        name=name,
        grid_spec=pltpu.PrefetchScalarGridSpec(
            num_scalar_prefetch=1,
            grid=(depth, r // tr),
            in_specs=[mine_of_layer(ll) for ll in range(depth)]
            + [theirs_of_layer(ll) for ll in range(depth)]
            + [stacked] * 3,
            out_specs=[stacked] * 4,
        ),
        out_shape=[SDS((depth, r, D), F32)] * 4,
        compiler_params=_params("parallel", "parallel"),
    )(chip, *parts, *from_chips, w, m, v)


def _adamw_small(groups, *, name):
    n = len(groups)

    def body(*refs):
        ins, outs = refs[: 4 * n], refs[4 * n :]
        for i in range(n):
            w, g, m, v = (ref[...] for ref in ins[4 * i : 4 * i + 4])
            outs[3 * i][...], outs[3 * i + 1][...], outs[3 * i + 2][...] = _adamw_math(w, g, m, v)

    vmem = pl.BlockSpec(memory_space=pltpu.VMEM)
    outs = pl.pallas_call(
        body,
        name=name,
        in_specs=[vmem] * (4 * n),
        out_specs=[vmem] * (3 * n),
        out_shape=[SDS(grp[0].shape, F32) for grp in groups for _ in range(3)],
        compiler_params=pltpu.CompilerParams(vmem_limit_bytes=VMEM_LIMIT_BYTES),
    )(*[a for grp in groups for a in grp])
    return [tuple(outs[3 * i : 3 * i + 3]) for i in range(n)]


ANY = pl.BlockSpec(memory_space=pl.ANY)


def _position():
    return lax.axis_index("x"), lax.axis_index("y"), lax.axis_index("c")


def _other_chips(x, y):
    return [(1 - x, y), (x, 1 - y), (1 - x, 1 - y)]


class _Comm:
    def __init__(self, inputs, out_shapes, sem_counts, start, finish, aliases=()):
        self.inputs, self.out_shapes, self.sem_counts = list(inputs), list(out_shapes), list(sem_counts)
        self.start, self.finish = start, finish
        self.aliases = list(aliases)

    def sem_shapes(self):
        return [pltpu.SemaphoreType.DMA((n,)) for n in self.sem_counts]


def _merge_comms(comms):
    bounds, i, o, s = [], 0, 0, 0
    for cm in comms:
        bounds.append((i, i + len(cm.inputs), o, o + len(cm.out_shapes), s, s + len(cm.sem_counts)))
        i, o, s = bounds[-1][1], bounds[-1][3], bounds[-1][5]

    def phase(which):
        def run(ins, outs, sems):
            for cm, (i0, i1, o0, o1, s0, s1) in zip(comms, bounds):
                getattr(cm, which)(ins[i0:i1], outs[o0:o1], sems[s0:s1])

        return run

    return _Comm(
        [a for cm in comms for a in cm.inputs],
        [a for cm in comms for a in cm.out_shapes],
        [a for cm in comms for a in cm.sem_counts],
        phase("start"),
        phase("finish"),
        aliases=[(i0 + i, o0 + o) for cm, (i0, _, o0, _, _, _) in zip(comms, bounds) for i, o in cm.aliases],
    )


def _call(body, args, *, semantics, comm=None, **kw):
    if comm is None:
        return pl.pallas_call(body, compiler_params=_params(*semantics), **kw)(*args)
    grid, in_specs, out_specs, out_shape = kw["grid"], kw["in_specs"], kw["out_specs"], kw["out_shape"]
    scratch = list(kw.get("scratch_shapes", ()))
    single = not isinstance(out_shape, (list, tuple))
    core_specs = [out_specs] if single else list(out_specs)
    core_shapes = [out_shape] if single else list(out_shape)
    n_in, n_out, n_scr = len(in_specs), len(core_shapes), len(scratch)
    n_cin, n_cout = len(comm.inputs), len(comm.out_shapes)
    steps = 1
    for g in grid:
        steps *= g

    def hosted(*refs):
        core_in, c_in = refs[:n_in], refs[n_in : n_in + n_cin]
        o0 = n_in + n_cin
        core_out, c_out = refs[o0 : o0 + n_out], refs[o0 + n_out : o0 + n_out + n_cout]
        s0 = o0 + n_out + n_cout
        core_scr, sems = refs[s0 : s0 + n_scr], refs[s0 + n_scr :]
        step = pl.program_id(0)
        for d in range(1, len(grid)):
            step = step * grid[d] + pl.program_id(d)

        @pl.when(step == 0)
        def _():
            comm.start(c_in, c_out, sems)

        body(*core_in, *core_out, *core_scr)

        @pl.when(step == steps - 1)
        def _():
            comm.finish(c_in, c_out, sems)

    outs = pl.pallas_call(
        hosted,
        name=kw["name"],
        grid=grid,
        in_specs=list(in_specs) + [ANY] * n_cin,
        out_specs=core_specs + [ANY] * n_cout,
        out_shape=core_shapes + comm.out_shapes,
        scratch_shapes=scratch + comm.sem_shapes(),
        input_output_aliases={n_in + i: n_out + o for i, o in comm.aliases},
        compiler_params=_params(*(["arbitrary"] * len(grid))),
    )(*args, *comm.inputs)
    return (outs[0] if single else outs[:n_out]), outs[n_out:]


def _comm_only(comm, *, name):
    n_cin, n_cout = len(comm.inputs), len(comm.out_shapes)

    def body(*refs):
        ins, outs, sems = refs[:n_cin], refs[n_cin : n_cin + n_cout], refs[n_cin + n_cout :]
        comm.start(ins, outs, sems)
        comm.finish(ins, outs, sems)

    return pl.pallas_call(
        body,
        name=name,
        in_specs=[ANY] * n_cin,
        out_specs=[ANY] * n_cout,
        out_shape=comm.out_shapes,
        scratch_shapes=comm.sem_shapes(),
    )(*comm.inputs)


def _gather_comm(shards):
    n = len(shards)
    per = 7

    def plan(ins, outs, sems):
        send, recv, local = sems
        x, y, c = _position()
        me, sibling = (x, y, c), (x, y, 1 - c)
        chips = _other_chips(x, y)

        def block(t, px, py, pc):
            return outs[t].at[pl.ds(4 * px + 2 * py + pc, 1)]

        def copy(t, k, blk, to, src=None):
            return pltpu.make_async_remote_copy(
                src_ref=block(t, *blk) if src is None else src,
                dst_ref=block(t, *blk),
                send_sem=send.at[t * per + k],
                recv_sem=recv.at[t * per + k],
                device_id=to,
                device_id_type=MESH,
            )

        mine = [pltpu.make_async_copy(ins[t], block(t, *me), local.at[t]) for t in range(n)]
        to_chips = [copy(t, 1 + j, me, (*chip, c), src=ins[t]) for t in range(n) for j, chip in enumerate(chips)]
        to_sibling = [copy(t, 0, me, sibling, src=ins[t]) for t in range(n)]
        from_chips = [copy(t, 1 + j, (*chip, c), me) for t in range(n) for j, chip in enumerate(chips)]
        passed_on = [copy(t, 4 + j, (*chip, c), sibling) for t in range(n) for j, chip in enumerate(chips)]
        from_sibling = [copy(t, 0, sibling, me) for t in range(n)]
        from_sibling += [copy(t, 4 + j, (*chip, 1 - c), me) for t in range(n) for j, chip in enumerate(chips)]
        return mine, to_chips, to_sibling, from_chips, passed_on, from_sibling

    def start(ins, outs, sems):
        mine, to_chips, to_sibling, _, _, _ = plan(ins, outs, sems)
        for cp in mine + to_chips + to_sibling:
            cp.start()

    def finish(ins, outs, sems):
        mine, to_chips, to_sibling, from_chips, passed_on, from_sibling = plan(ins, outs, sems)
        for arrived, onward in zip(from_chips, passed_on):
            arrived.wait_recv()
            onward.start()
        for cp in from_sibling:
            cp.wait_recv()
        for cp in to_chips + to_sibling + passed_on:
            cp.wait_send()
        for cp in mine:
            cp.wait()

    out_shapes = [SDS((N_DEV,) + sh.shape[1:], sh.dtype) for sh in shards]
    return _Comm(shards, out_shapes, [n * per, n * per, n], start, finish)


def _gather_stage(stage, shards=None, arrived=None):
    n = len(arrived if shards is None else shards)
    targets = {"near": (0, 1), "far": (2,), "first": (0, 1, 2), "pass": (0, 1, 2)}[stage]
    to_sibling = stage in ("near", "first")
    per = len(targets) + to_sibling

    def plan(ins, outs, sems):
        x, y, c = _position()
        me, sibling = (x, y, c), (x, y, 1 - c)
        chips = [_other_chips(x, y)[j] for j in targets]

        def block(t, px, py, pc):
            return outs[t].at[pl.ds(4 * px + 2 * py + pc, 1)]

        def copy(t, k, blk, to, src=None):
            return pltpu.make_async_remote_copy(
                src_ref=block(t, *blk) if src is None else src,
                dst_ref=block(t, *blk),
                send_sem=sems[0].at[t * per + k],
                recv_sem=sems[1].at[t * per + k],
                device_id=to,
                device_id_type=MESH,
            )

        local = []
        if stage == "pass":
            sent = [copy(t, j, (*chip, c), sibling) for t in range(n) for j, chip in enumerate(chips)]
            landing = [copy(t, j, (*chip, 1 - c), me) for t in range(n) for j, chip in enumerate(chips)]
        else:
            sent = [copy(t, j, me, (*chip, c), src=ins[t]) for t in range(n) for j, chip in enumerate(chips)]
            landing = [copy(t, j, (*chip, c), me) for t in range(n) for j, chip in enumerate(chips)]
            if to_sibling:
                local = [pltpu.make_async_copy(ins[t], block(t, *me), sems[2].at[t]) for t in range(n)]
                sent += [copy(t, per - 1, me, sibling, src=ins[t]) for t in range(n)]
                landing += [copy(t, per - 1, sibling, me) for t in range(n)]
        return local, sent, landing

    def start(ins, outs, sems):
        local, sent, _ = plan(ins, outs, sems)
        for cp in local + sent:
            cp.start()

    def finish(ins, outs, sems):
        local, sent, landing = plan(ins, outs, sems)
        for cp in landing:
            cp.wait_recv()
        for cp in sent:
            cp.wait_send()
        for cp in local:
            cp.wait()

    if to_sibling:
        out_shapes = [SDS((N_DEV,) + sh.shape[1:], sh.dtype) for sh in shards]
        return _Comm(shards, out_shapes, [n * per, n * per, n], start, finish)
    out_shapes = [SDS(a.shape, a.dtype) for a in arrived]
    if stage == "pass":
        return _Comm(arrived, out_shapes, [n * per, n * per], start, finish, aliases=[(t, t) for t in range(n)])
    return _Comm(list(shards) + list(arrived), out_shapes, [n * per, n * per], start, finish, aliases=[(n + t, t) for t in range(n)])


def _exchange_comm(arrays, out_shapes, n_copies, copies_of):
    def start(ins, outs, sems):
        for cp in copies_of(ins, outs, *sems):
            cp.start()

    def finish(ins, outs, sems):
        for cp in copies_of(ins, outs, *sems):
            cp.wait()

    return _Comm(arrays, out_shapes, [n_copies, n_copies], start, finish)


def _sibling_comm(grads):
    def copies_of(ins, outs, send, recv):
        x, y, c = _position()
        return [
            pltpu.make_async_remote_copy(
                src_ref=ins[t].at[:, pl.ds(1 - c, 1)],
                dst_ref=outs[t],
                send_sem=send.at[t],
                recv_sem=recv.at[t],
                device_id=(x, y, 1 - c),
                device_id_type=MESH,
            )
            for t in range(len(ins))
        ]

    return _exchange_comm(grads, [SDS((4, 1) + g.shape[2:], g.dtype) for g in grads], len(grads), copies_of)


def _chips_comm(parts):
    def copies_of(ins, outs, send, recv):
        x, y, c = _position()
        return [
            pltpu.make_async_remote_copy(
                src_ref=ins[t].at[pl.ds(2 * px + py, 1)],
                dst_ref=outs[t].at[pl.ds(k, 1)],
                send_sem=send.at[3 * t + k],
                recv_sem=recv.at[3 * t + k],
                device_id=(px, py, c),
                device_id_type=MESH,
            )
            for t in range(len(ins))
            for k, (px, py) in enumerate(_other_chips(x, y))
        ]

    return _exchange_comm(parts, [SDS((3,) + p.shape[1:], p.dtype) for p in parts], 3 * len(parts), copies_of)


def _sum_with_sibling(grad, got, core, *, name):
    rows = grad.shape[2]

    def body(core_ref, a_ref, b_ref, o_ref):
        o_ref[...] = (a_ref[...].astype(F32) + b_ref[...].astype(F32)).astype(o_ref.dtype)

    return pl.pallas_call(
        body,
        name=name,
        grid_spec=pltpu.PrefetchScalarGridSpec(
            num_scalar_prefetch=1,
            grid=(4,),
            in_specs=[
                pl.BlockSpec((None, None, rows, D), lambda q, core_ref: (q, core_ref[0], 0, 0)),
                pl.BlockSpec((None, None, rows, D), lambda q, core_ref: (q, 0, 0, 0)),
            ],
            out_specs=pl.BlockSpec((None, rows, D), lambda q, core_ref: (q, 0, 0)),
        ),
        out_shape=SDS((4, rows, D), grad.dtype),
        compiler_params=_params("parallel"),
    )(core, grad, got)


def _sum_chips(part, got, chip, *, name):
    rows = part.shape[1]

    def body(chip_ref, a_ref, b_ref, o_ref):
        o_ref[...] = ((a_ref[...].astype(F32) + b_ref[0].astype(F32)) + b_ref[1].astype(F32)) + b_ref[2].astype(F32)

    return pl.pallas_call(
        body,
        name=name,
        grid_spec=pltpu.PrefetchScalarGridSpec(
            num_scalar_prefetch=1,
            grid=(1,),
            in_specs=[
                pl.BlockSpec((None, rows, D), lambda i, chip_ref: (chip_ref[0], 0, 0)),
                pl.BlockSpec((3, rows, D), lambda i, chip_ref: (0, 0, 0)),
            ],
            out_specs=pl.BlockSpec((rows, D), lambda i, chip_ref: (0, 0)),
        ),
        out_shape=SDS((rows, D), F32),
        compiler_params=_params("arbitrary"),
    )(chip, part, got)


def _all_reduce_small(pack, *, name):
    rows = pack.shape[1]

    def body(in_ref, out_ref, from_sibling, part, from_chips, send, recv):
        x, y, c = _position()
        me, sibling = (x, y, c), (x, y, 1 - c)
        chips = _other_chips(x, y)
        waiting = []

        def copy(k, src, dst, to):
            return pltpu.make_async_remote_copy(
                src_ref=src, dst_ref=dst, send_sem=send.at[k], recv_sem=recv.at[k], device_id=to, device_id_type=MESH
            )

        def exchange(copies):
            for cp in copies:
                cp.start()
            for cp in copies:
                cp.wait_recv()
            waiting.extend(copies)

        def block(px, py, pc):
            return out_ref.at[4 * px + 2 * py + pc]

        exchange([copy(q, in_ref.at[2 * q + 1 - c], from_sibling.at[q], sibling) for q in range(4)])
        for q in range(4):
            part[q] = in_ref[2 * q + c] + from_sibling[q]
        exchange([copy(4 + k, part.at[2 * px + py], from_chips.at[k], (px, py, c)) for k, (px, py) in enumerate(chips)])
        out_ref[4 * x + 2 * y + c] = ((part[2 * x + y] + from_chips[0]) + from_chips[1]) + from_chips[2]
        exchange(
            [copy(7, block(*me), block(*me), sibling)]
            + [copy(8 + k, block(*me), block(*me), (px, py, c)) for k, (px, py) in enumerate(chips)]
        )
        exchange([copy(11 + k, block(px, py, c), block(px, py, c), sibling) for k, (px, py) in enumerate(chips)])
        for cp in waiting:
            cp.wait_send()

    vmem = pl.BlockSpec(memory_space=pltpu.VMEM)
    return pl.pallas_call(
        body,
        name=name,
        in_specs=[vmem],
        out_specs=vmem,
        out_shape=SDS(pack.shape, F32),
        scratch_shapes=[
            pltpu.VMEM((4, rows, D), F32),
            pltpu.VMEM((4, rows, D), F32),
            pltpu.VMEM((3, rows, D), F32),
            pltpu.SemaphoreType.DMA((14,)),
            pltpu.SemaphoreType.DMA((14,)),
        ],
        compiler_params=pltpu.CompilerParams(vmem_limit_bytes=VMEM_LIMIT_BYTES),
    )(pack)


def _pack(arrays, rows):
    flat = jnp.concatenate([a.reshape(-1).astype(F32) for a in arrays])
    return jnp.pad(flat, (0, rows * D - flat.shape[0])).reshape(rows, D)


def _unpack(pack, shapes):
    flat = pack.reshape(-1)
    out, off = [], 0
    for sh in shapes:
        size = 1
        for dim in sh:
            size *= dim
        out.append(flat[off : off + size].reshape(sh))
        off += size
    return out


def _block_diag_pairs(w):
    w = w.reshape(N_RNN_TILES, 2, HEAD_DIM, HEAD_DIM)
    z = jnp.zeros_like(w[:, 0])
    top = jnp.concatenate([w[:, 0], z], axis=2)
    bot = jnp.concatenate([z, w[:, 1]], axis=2)
    return jnp.concatenate([top, bot], axis=1)


def _diag_blocks(w2):
    a = w2[:, :HEAD_DIM, :HEAD_DIM]
    b = w2[:, HEAD_DIM:, HEAD_DIM:]
    return jnp.stack([a, b], axis=1).reshape(RNN_HEADS, HEAD_DIM, HEAD_DIM)


BIG = ("w_in", "w_branch_a", "w_branch_b", "w_out", "w_up", "w_down")
TRANSPOSED = ("w_in", "w_up")
SMALL = (
    "norm_mix_g", "conv_w", "conv_b", "lru_w_a", "lru_b_a", "lru_w_x", "lru_b_x", "lru_lambda",
    "sgu_ln_g", "sgu_ln_b", "sgu_w_s", "sgu_b_s", "norm_ffn_g", "final_norm_g",
)
WEIGHTS = (
    "norm_mix_g", "w_in", "conv_w", "conv_b", "lru_w_a", "lru_b_a", "lru_w_x", "lru_b_x", "lru_lambda", "sgu_ln_g",
    "sgu_ln_b", "sgu_w_s", "sgu_b_s", "w_branch_a", "w_branch_b", "w_out", "norm_ffn_g", "w_up", "w_down", "final_norm_g",
)

TM = 512
TM_NT = 1024
TN_IN = 1664
TN_UP = 2048
TKA = 512
TKA_PIECES = 256
TC = 512
TC_BWD = 1024
TB = 256
TB_BWD = 512
TR = 256


_BRANCHES_0 = [(0, "w_branch_a"), (0, "w_branch_b"), (0, "w_out")]
_BRANCHES_1 = [(1, "w_branch_a"), (1, "w_branch_b"), (1, "w_out")]
GATHERS_RIDING = (
    {
        "in_proj": [("first", _BRANCHES_0), ("near", [(0, "w_up")])],
        "branch_a_fwd": [("pass", _BRANCHES_0), ("far", [(0, "w_up")]), ("near", [(1, "w_in")])],
        "sgu_fwd": [("pass", [(0, "w_up")]), ("near", [(0, "w_down")])],
        "merge_fwd": [("far", [(0, "w_down")])],
        "ffn_up": [("pass", [(0, "w_down")]), ("far", [(1, "w_in")])],
        "ffn_down": [("pass", [(1, "w_in")]), ("near", _BRANCHES_1)],
    },
    {
        "in_proj": [("far", _BRANCHES_1), ("near", [(1, "w_down")])],
        "branch_a_fwd": [("pass", _BRANCHES_1), ("first", [(1, "w_up")])],
        "sgu_fwd": [("pass", [(1, "w_up")])],
        "merge_fwd": [("far", [(1, "w_down")])],
        "ffn_up": [("pass", [(1, "w_down")])],
    },
)


def _layer_forward(l, x, p, w, shards, arriving, loss_head=None):
    def run(key, fn, *args, **kw):
        riding = GATHERS_RIDING[l].get(key, ())
        if not riding:
            return fn(*args, **kw)
        comms = []
        for stage, units in riding:
            mine = [shards[l2][n2] for l2, n2 in units] if stage != "pass" else None
            left = [arriving.pop(unit) for unit in units] if stage in ("far", "pass") else None
            comms.append(_gather_stage(stage, shards=mine, arrived=left))
        out, got = fn(*args, comm=_merge_comms(comms), **kw)
        got = list(got)
        for stage, units in riding:
            for l2, n2 in units:
                if stage == "pass":
                    w[l2][n2] = got.pop(0).reshape(-1, D)
                else:
                    arriving[l2, n2] = got.pop(0)
        return out

    proj, h = run("in_proj", _norm_matmul_nt, x, p["norm_mix_g"], w[l]["w_in"], tm=TM_NT, tn=TN_IN, name=f"in_proj_{l}")
    hseq, ya_pre = run(
        "branch_a_fwd", _branch_a_fwd, proj, p["conv_w"], p["conv_b"], p["wa2"], p["lru_b_a"], p["wx2"], p["lru_b_x"],
        p["lru_lambda"], tc=TC, name=f"branch_a_fwd_{l}",
    )
    yb_pre = run("sgu_fwd", _sgu_fwd, proj, p["sgu_ln_g"], p["sgu_ln_b"], p["wm"], p["sgu_bias"], tb=TB, name=f"sgu_fwd_{l}")
    x1, ya, yb = run(
        "merge_fwd", _merge_fwd, ya_pre, yb_pre, proj, x, w[l]["w_branch_a"], w[l]["w_branch_b"], w[l]["w_out"], tm=TM,
        name=f"merge_fwd_{l}",
    )
    f_pre, h2 = run("ffn_up", _norm_matmul_nt, x1, p["norm_ffn_g"], w[l]["w_up"], tm=TM_NT, tn=TN_UP, name=f"ffn_up_{l}")
    saved = dict(x=x, h=h, proj=proj, hseq=hseq, ya_pre=ya_pre, yb_pre=yb_pre, ya=ya, yb=yb, x1=x1, h2=h2, f_pre=f_pre)
    if loss_head is None:
        return run("ffn_down", _matmul_nn_res, f_pre, w[l]["w_down"], x1, relu2=True, tm=TM, name=f"ffn_down_{l}"), saved
    return _ffn_down_loss(f_pre, w[l]["w_down"], x1, *loss_head, tm=TM, name=f"ffn_down_loss_{l}"), saved


def _layer_backward(l, dx2, dx2b, sv, p, w, core, waiting, last):
    parts, from_chips = {}, {}

    def by_device(g):
        return g.reshape(4, 2, -1, D)

    def with_sibling(name, g, got):
        parts[name] = _sum_with_sibling(by_device(g), got, core, name=f"sum_sibling_{name}_{l}")

    df_pre = _matmul_nt_drelu2(dx2b, w["w_down"], sv["f_pre"], tm=TM_NT, tn=TN_UP, name=f"ffn_down_bwd_{l}")
    g_down = _matmul_tn([sv["f_pre"]], dx2b, relu2=True, tka=TKA, name=f"grad_w_down_{l}")
    g_up, (got,) = _matmul_tn(
        [df_pre], sv["h2"], relu2=False, tka=TKA, name=f"grad_w_up_{l}", comm=_sibling_comm([by_device(g_down)])
    )
    with_sibling("w_down", g_down, got)
    (dx1, dx1b, g_norm_ffn), (got,) = _matmul_nn_rmsnorm_bwd(
        [df_pre], w["w_up"], sv["x1"], p["norm_ffn_g"], dx2, tm=TM, name=f"ffn_up_bwd_{l}",
        comm=_sibling_comm([by_device(g_up)]),
    )
    with_sibling("w_up", g_up, got)
    (merged, dya, dyb, dga, dgb, dya_pre, dyb_pre), (from_chips[l, "w_up"],) = _merge_bwd(
        dx1b, sv["ya"], sv["yb"], sv["proj"], w["w_branch_a"], w["w_branch_b"], w["w_out"], tm=TM, name=f"merge_bwd_{l}",
        comm=_chips_comm([parts["w_up"]]),
    )
    g_out, g_ba, g_bb = _matmuls_tn(
        [(merged, dx1b), (sv["ya_pre"], dya), (sv["yb_pre"], dyb)], ts=2 * TM, name=f"grad_w_branches_{l}"
    )
    branch = (("w_out", g_out), ("w_branch_a", g_ba), ("w_branch_b", g_bb))
    (du, dv, g_ws, g_bs, g_lng, g_lnb), got = _sgu_bwd(
        dyb_pre, sv["proj"], p["sgu_ln_g"], p["sgu_ln_b"], p["wm"], p["wmt"], p["sgu_bias"], p["mask"], tb=TB_BWD,
        name=f"sgu_bwd_{l}",
        comm=_merge_comms([_sibling_comm([by_device(g) for _, g in branch]), _chips_comm([parts["w_down"]])]),
    )
    from_chips[l, "w_down"] = got[-1]
    for (name, g), landed in zip(branch, got):
        with_sibling(name, g, landed)
    riding = [((l, name), parts[name]) for name, _ in branch] + list(waiting)
    (dxr, dgr, g_cw, g_cb, g_ba_, g_bx, g_lam, g_wa2, g_wx2), got = _branch_a_bwd(
        dya_pre, sv["proj"], sv["hseq"], p["conv_w"], p["conv_b"], p["wa2"], p["lru_b_a"], p["wx2"], p["lru_b_x"],
        p["lru_lambda"], p["wa2t"], p["wx2t"], tc=TC_BWD, name=f"branch_a_bwd_{l}", comm=_chips_comm([part for _, part in riding]),
    )
    for (key, _), landed in zip(riding, got):
        from_chips[key] = landed
    dproj = [dxr, dgr, du, dv, dga, dgb]
    g_in = _matmul_tn(dproj, sv["h"], relu2=False, tka=TKA_PIECES, name=f"grad_w_in_{l}")
    if last:
        (got,) = _comm_only(_sibling_comm([by_device(g_in)]), name=f"grad_w_in_to_sibling_{l}")
        with_sibling("w_in", g_in, got)
        riding = _chips_comm([parts["w_in"]])
    else:
        riding = _sibling_comm([by_device(g_in)])
    (dx, dxb, g_norm_mix), (got,) = _matmul_nn_rmsnorm_bwd(
        dproj, w["w_in"], sv["x"], p["norm_mix_g"], dx1, tm=TM, name=f"in_proj_bwd_{l}", comm=riding
    )
    if last:
        from_chips[l, "w_in"] = got
    else:
        with_sibling("w_in", g_in, got)
    small = dict(
        norm_mix_g=g_norm_mix[0], conv_w=g_cw, conv_b=g_cb[0], lru_w_a=_diag_blocks(g_wa2), lru_b_a=g_ba_.reshape(RNN_HEADS, HEAD_DIM),
        lru_w_x=_diag_blocks(g_wx2), lru_b_x=g_bx.reshape(RNN_HEADS, HEAD_DIM), lru_lambda=g_lam[0], sgu_ln_g=g_lng[0],
        sgu_ln_b=g_lnb[0], sgu_w_s=g_ws, sgu_b_s=g_bs[:, :, 0], norm_ffn_g=g_norm_ffn[0],
    )
    return dx, dxb, small, parts, from_chips


def _prepare_small(l, given):
    chunk_id = jnp.arange(SGU_BLOCK) // CHUNK
    mask = (chunk_id[:, None] >= chunk_id[None, :]).astype(F32)
    wm = given["sgu_w_s"][l] * mask
    wa2 = _block_diag_pairs(given["lru_w_a"][l])
    wx2 = _block_diag_pairs(given["lru_w_x"][l])
    row = lambda a: a.reshape(1, -1)
    return dict(
        norm_mix_g=row(given["norm_mix_g"][l]),
        norm_ffn_g=row(given["norm_ffn_g"][l]),
        conv_w=given["conv_w_full"][l],
        conv_b=row(given["conv_b"][l]),
        wa2=wa2.astype(BF16),
        wx2=wx2.astype(BF16),
        wa2t=jnp.swapaxes(wa2, 1, 2).astype(BF16),
        wx2t=jnp.swapaxes(wx2, 1, 2).astype(BF16),
        lru_b_a=row(given["lru_b_a"][l]),
        lru_b_x=row(given["lru_b_x"][l]),
        lru_lambda=row(given["lru_lambda"][l]),
        sgu_ln_g=row(given["sgu_ln_g"][l]),
        sgu_ln_b=row(given["sgu_ln_b"][l]),
        wm=wm.astype(BF16),
        wmt=jnp.swapaxes(wm, 1, 2).astype(BF16),
        sgu_bias=jnp.broadcast_to(given["sgu_b_s"][l][:, :, None], (SGU_GROUPS, SGU_BLOCK, LANES)),
        mask=mask,
    )


def _step(given):
    x_idx, y_idx, c_idx = _position()
    dev = 4 * x_idx + 2 * y_idx + c_idx
    core = c_idx.astype(jnp.int32).reshape(1)
    chip = (2 * x_idx + y_idx).astype(jnp.int32).reshape(1)

    def rows_first(name, a):
        return jnp.swapaxes(a, 1, 2) if name in TRANSPOSED else a

    shards = []
    for l in range(DEPTH):
        shards.append({name: rows_first(name, given[name])[l].astype(BF16)[None] for name in BIG})
    conv_mine = given["conv_w"].reshape(1, DEPTH * CONV_WIDTH, D_RNN // N_DEV)
    w_in_first, conv_all = _comm_only(_gather_comm([shards[0]["w_in"], conv_mine]), name="gather_first")
    weights = [{"w_in": w_in_first.reshape(-1, D)}, {}]
    conv_all = conv_all.reshape(N_DEV, DEPTH, CONV_WIDTH, D_RNN // N_DEV)
    given = dict(given, conv_w_full=jnp.moveaxis(conv_all, 0, 2).reshape(DEPTH, CONV_WIDTH, D_RNN))

    small_params = [_prepare_small(l, given) for l in range(DEPTH)]
    x = given["x"][0]
    saved, arriving = [], {}
    loss_head = (given["final_norm_g"].reshape(1, D), given["loss_target"][0])
    for l in range(DEPTH):
        x, sv = _layer_forward(
            l, x, small_params[l], weights, shards, arriving, loss_head=loss_head if l == DEPTH - 1 else None
        )
        saved.append(sv)
    dx, dxb, g_final, loss = x
    small_grads, parts, from_chips, waiting = [None] * DEPTH, [None] * DEPTH, {}, []
    for l in reversed(range(DEPTH)):
        dx, dxb, small_grads[l], parts[l], got = _layer_backward(
            l, dx, dxb, saved[l], small_params[l], weights[l], core, waiting, last=l == 0
        )
        from_chips.update(got)
        waiting = [((l, "w_in"), parts[l]["w_in"])]

    small_list = []
    for name in SMALL[:-1]:
        small_list.append(jnp.stack([small_grads[l][name] for l in range(DEPTH)]))
    small_list += [g_final[0], loss[0, :1]]
    small_shapes = [a.shape for a in small_list]
    pack = _pack(small_list, SMALL_ROWS).reshape(N_DEV, SMALL_ROWS_PER_DEV, D)
    summed = _unpack(_all_reduce_small(pack, name="all_reduce_small"), small_shapes)
    loss_total = summed[-1][0]
    grads = dict(zip(SMALL, summed[:-1]))
    cw = grads["conv_w"].reshape(DEPTH, CONV_WIDTH, N_DEV, D_RNN // N_DEV)
    grads["conv_w"] = lax.dynamic_index_in_dim(cw, dev, axis=2, keepdims=False)

    delta, new_m, new_v = {}, {}, {}
    for name in BIG:
        w, m, v = given[name], given["m_" + name], given["v_" + name]
        mine = [parts[l][name] for l in range(DEPTH)]
        theirs = [from_chips[l, name] for l in range(DEPTH)]
        if name == "w_up":
            sums = [_sum_chips(mine[l], theirs[l], chip, name=f"sum_chips_{name}_{l}").T for l in range(DEPTH)]
            out = _adamw_layers(w, sums, m, v, tr=TR, name=f"adamw_{name}")
        else:
            out = _adamw_reduced(
                rows_first(name, w), mine, theirs, rows_first(name, m), rows_first(name, v), chip, tr=TR, name=f"adamw_{name}"
            )
            out = [rows_first(name, a) for a in out]
        grads[name], delta[name], new_m[name], new_v[name] = out
    two_d = lambda a: a.reshape(1, -1) if a.ndim == 1 else a
    groups = [tuple(two_d(a) for a in (given[n], grads[n], given["m_" + n], given["v_" + n])) for n in SMALL]
    for n, (d, m2, v2) in zip(SMALL, _adamw_small(groups, name="adamw_small")):
        shape = given[n].shape
        delta[n], new_m[n], new_v[n] = d.reshape(shape), m2.reshape(shape), v2.reshape(shape)

    return (
        loss_total, dx[None],
        *[grads[n] for n in WEIGHTS], *[delta[n] for n in WEIGHTS], *[new_m[n] for n in WEIGHTS], *[new_v[n] for n in WEIGHTS],
    )


def kernel(x, norm_mix_g, w_in, conv_w, conv_b, lru_w_a, lru_b_a, lru_w_x, lru_b_x, lru_lambda, sgu_ln_g, sgu_ln_b, sgu_w_s, sgu_b_s, w_branch_a, w_branch_b, w_out, norm_ffn_g, w_up, w_down, final_norm_g, loss_target, m_norm_mix_g, m_w_in, m_conv_w, m_conv_b, m_lru_w_a, m_lru_b_a, m_lru_w_x, m_lru_b_x, m_lru_lambda, m_sgu_ln_g, m_sgu_ln_b, m_sgu_w_s, m_sgu_b_s, m_w_branch_a, m_w_branch_b, m_w_out, m_norm_ffn_g, m_w_up, m_w_down, m_final_norm_g, v_norm_mix_g, v_w_in, v_conv_w, v_conv_b, v_lru_w_a, v_lru_b_a, v_lru_w_x, v_lru_b_x, v_lru_lambda, v_sgu_ln_g, v_sgu_ln_b, v_sgu_w_s, v_sgu_b_s, v_w_branch_a, v_w_branch_b, v_w_out, v_norm_ffn_g, v_w_up, v_w_down, v_final_norm_g):
    return _step(dict(locals()))
```

```python
import jax
import jax.numpy as jnp
from jax import lax
from jax.experimental import pallas as pl
from jax.experimental.pallas import tpu as pltpu

F32 = jnp.float32
BF16 = jnp.bfloat16
SDS = jax.ShapeDtypeStruct
MESH = pl.DeviceIdType.MESH

D = 1024
D_RNN = 1280
D_SGU = 1024
D_IN = 2 * D_RNN + 2 * D_SGU + 2 * D
DEPTH = 2
RNN_HEADS = 20
HEAD_DIM = 64
CONV_WIDTH = 4
LRU_C = 8.0
SGU_GROUPS = 8
SGU_BLOCK = 128
CHUNK = 64
EPS = 1e-6
N_DEV = 8

ADAM_LR = 0.001
ADAM_B1 = 0.9
ADAM_B2 = 0.999
ADAM_EPS = 1e-08
ADAM_WD = 0.01
ADAM_STEP = 10

LANES = 128
SUBLANES = 8
VMEM_LIMIT_BYTES = 56 * 1024 * 1024

N_RNN_TILES = D_RNN // LANES
RNN_TILES_PER_STEP = 5
U_BLK512 = (2 * D_RNN) // 512
V_BLK512 = (2 * D_RNN + D_SGU) // 512
GA_BLK512 = (2 * D_RNN + 2 * D_SGU) // 512
GB_BLK512 = (2 * D_RNN + 2 * D_SGU + D) // 512

SMALL_ROWS_PER_DEV = 80
SMALL_ROWS = N_DEV * SMALL_ROWS_PER_DEV


def _params(*sem):
    return pltpu.CompilerParams(dimension_semantics=sem, vmem_limit_bytes=VMEM_LIMIT_BYTES)


def _sigmoid(x):
    return 0.5 + 0.5 * jnp.tanh(0.5 * x)


_GELU_C = 0.7978845608028654
_GELU_K = 0.044715


def _gelu(x):
    t = jnp.tanh(_GELU_C * (x + _GELU_K * x * x * x))
    return 0.5 * x * (1.0 + t)


def _gelu_and_grad(x):
    t = jnp.tanh(_GELU_C * (x + _GELU_K * x * x * x))
    val = 0.5 * x * (1.0 + t)
    grad = 0.5 * (1.0 + t) + 0.5 * x * (1.0 - t * t) * _GELU_C * (1.0 + 3.0 * _GELU_K * x * x)
    return val, grad


def _one_minus_square(log_a, a):
    return -jnp.tanh(log_a) * (1.0 + a * a)


def _dot(a, b):
    return jnp.dot(a, b, preferred_element_type=F32)


def _dot_nt(a, b):
    return lax.dot_general(a, b, (((1,), (1,)), ((), ())), preferred_element_type=F32)


def _dot_tn(a, b):
    return lax.dot_general(a, b, (((0,), (0,)), ((), ())), preferred_element_type=F32)


def _norm_matmul_nt(x, g, w, *, tm, tn, name, comm=None):
    s, n = x.shape[0], w.shape[0]
    tm, tn = min(tm, s), min(tn, n)

    def body(x_ref, g_ref, w_ref, o_ref, h_ref):
        @pl.when(pl.program_id(1) == 0)
        def _():
            xv = x_ref[...]
            r = lax.rsqrt(jnp.mean(xv * xv, axis=-1, keepdims=True) + EPS)
            h_ref[...] = (xv * r * g_ref[...]).astype(BF16)

        o_ref[...] = _dot_nt(h_ref[...], w_ref[...]).astype(o_ref.dtype)

    return _call(
        body,
        (x, g, w),
        name=name,
        grid=(s // tm, n // tn),
        in_specs=[
            pl.BlockSpec((tm, D), lambda i, j: (i, 0)),
            pl.BlockSpec((1, D), lambda i, j: (0, 0)),
            pl.BlockSpec((tn, D), lambda i, j: (j, 0)),
        ],
        out_specs=[pl.BlockSpec((tm, tn), lambda i, j: (i, j)), pl.BlockSpec((tm, D), lambda i, j: (i, 0))],
        out_shape=[SDS((s, n), BF16), SDS((s, D), BF16)],
        semantics=("parallel", "arbitrary"),
        comm=comm,
    )


def _matmul_nn_res(a, w, res, *, relu2, tm, name, comm=None):
    s, k = a.shape
    tm = min(tm, s)

    def body(a_ref, w_ref, r_ref, o_ref):
        av = a_ref[...]
        if relu2:
            t = jnp.maximum(av.astype(F32), 0.0)
            av = (t * t).astype(BF16)
        o_ref[...] = r_ref[...] + _dot(av, w_ref[...])

    return _call(
        body,
        (a, w, res),
        name=name,
        grid=(s // tm,),
        in_specs=[
            pl.BlockSpec((tm, k), lambda i: (i, 0)),
            pl.BlockSpec((k, D), lambda i: (0, 0)),
            pl.BlockSpec((tm, D), lambda i: (i, 0)),
        ],
        out_specs=pl.BlockSpec((tm, D), lambda i: (i, 0)),
        out_shape=SDS((s, D), F32),
        semantics=("parallel",),
        comm=comm,
    )


def _matmul_nt_drelu2(a, w, pre, *, tm, tn, name):
    s, n = a.shape[0], w.shape[0]
    tm, tn = min(tm, s), min(tn, n)

    def body(a_ref, w_ref, p_ref, o_ref):
        d = _dot_nt(a_ref[...], w_ref[...])
        o_ref[...] = (d * (2.0 * jnp.maximum(p_ref[...].astype(F32), 0.0))).astype(o_ref.dtype)

    return pl.pallas_call(
        body,
        name=name,
        grid=(s // tm, n // tn),
        in_specs=[
            pl.BlockSpec((tm, D), lambda i, j: (i, 0)),
            pl.BlockSpec((tn, D), lambda i, j: (j, 0)),
            pl.BlockSpec((tm, tn), lambda i, j: (i, j)),
        ],
        out_specs=pl.BlockSpec((tm, tn), lambda i, j: (i, j)),
        out_shape=SDS((s, n), BF16),
        compiler_params=_params("parallel", "arbitrary"),
    )(a, w, pre)


def _matmul_tn(a_list, b, *, relu2, tka, name, comm=None):
    s = b.shape[0]
    n = len(a_list)
    nblk = [a.shape[1] // tka for a in a_list]
    starts = [sum(nblk[:p]) for p in range(n)]

    def body(*refs):
        a_refs, b_ref, o_ref = refs[:n], refs[n], refs[n + 1]
        i = pl.program_id(0)
        for p in range(n):

            @pl.when((i >= starts[p]) & (i < starts[p] + nblk[p]))
            def _(p=p):
                av = a_refs[p][...]
                if relu2:
                    t = jnp.maximum(av.astype(F32), 0.0)
                    av = (t * t).astype(BF16)
                o_ref[...] = _dot_tn(av, b_ref[...]).astype(o_ref.dtype)

    def piece_spec(p):
        return pl.BlockSpec((s, tka), lambda i: (0, jnp.clip(i - starts[p], 0, nblk[p] - 1)))

    return _call(
        body,
        (*a_list, b),
        name=name,
        grid=(sum(nblk),),
        in_specs=[piece_spec(p) for p in range(n)] + [pl.BlockSpec((s, D), lambda i: (0, 0))],
        out_specs=pl.BlockSpec((tka, D), lambda i: (i, 0)),
        out_shape=SDS((sum(nblk) * tka, D), BF16),
        semantics=("parallel",),
        comm=comm,
    )


def _matmuls_tn(pairs, *, ts, name):
    s = pairs[0][0].shape[0]
    ts = min(ts, s)
    n = len(pairs)
    steps = s // ts

    def body(*refs):
        ins, outs, accs = refs[: 2 * n], refs[2 * n : 3 * n], refs[3 * n :]
        for p in range(n):
            part = _dot_tn(ins[2 * p][...], ins[2 * p + 1][...])

            @pl.when(pl.program_id(0) == 0)
            def _(p=p, part=part):
                accs[p][...] = part

            @pl.when(pl.program_id(0) > 0)
            def _(p=p, part=part):
                accs[p][...] += part

        @pl.when(pl.program_id(0) == steps - 1)
        def _():
            for p in range(n):
                outs[p][...] = accs[p][...].astype(BF16)

    widths = [a.shape[1] for a, _ in pairs]
    in_specs = []
    for wd in widths:
        in_specs += [pl.BlockSpec((ts, wd), lambda i: (i, 0)), pl.BlockSpec((ts, D), lambda i: (i, 0))]
    return pl.pallas_call(
        body,
        name=name,
        grid=(steps,),
        in_specs=in_specs,
        out_specs=[pl.BlockSpec((wd, D), lambda i: (0, 0)) for wd in widths],
        out_shape=[SDS((wd, D), BF16) for wd in widths],
        scratch_shapes=[pltpu.VMEM((wd, D), F32) for wd in widths],
        compiler_params=_params("arbitrary"),
    )(*[x for pair in pairs for x in pair])


def _matmul_nn_rmsnorm_bwd(a_list, w, x, g, res, *, tm, name, comm=None):
    s = x.shape[0]
    tm = min(tm, s)
    n = len(a_list)
    widths = [a.shape[1] for a in a_list]
    offs = [sum(widths[:p]) for p in range(n)]
    k = sum(widths)

    def body(*refs):
        a_refs = refs[:n]
        w_ref, x_ref, g_ref, r_ref, dx_ref, dxb_ref, dg_ref = refs[n:]

        @pl.when(pl.program_id(0) == 0)
        def _():
            dg_ref[...] = jnp.zeros_like(dg_ref)

        dh = _dot(a_refs[0][...], w_ref[0 : widths[0], :])
        for p in range(1, n):
            dh += _dot(a_refs[p][...], w_ref[offs[p] : offs[p] + widths[p], :])
        xv = x_ref[...]
        r = lax.rsqrt(jnp.mean(xv * xv, axis=-1, keepdims=True) + EPS)
        xhat = xv * r
        dxh = dh * g_ref[...]
        dx = r_ref[...] + r * (dxh - xhat * jnp.mean(dxh * xhat, axis=-1, keepdims=True))
        dx_ref[...] = dx
        dxb_ref[...] = dx.astype(BF16)
        dg_ref[...] += jnp.sum(dh * xhat, axis=0, keepdims=True)

    act = pl.BlockSpec((tm, D), lambda i: (i, 0))
    vec = pl.BlockSpec((1, D), lambda i: (0, 0))
    return _call(
        body,
        (*a_list, w, x, g, res),
        name=name,
        grid=(s // tm,),
        in_specs=[pl.BlockSpec((tm, wd), lambda i: (i, 0)) for wd in widths]
        + [pl.BlockSpec((k, D), lambda i: (0, 0), pipeline_mode=pl.Buffered(1)), act, vec, act],
        out_specs=[act, act, vec],
        out_shape=[SDS((s, D), F32), SDS((s, D), BF16), SDS((1, D), F32)],
        semantics=("arbitrary",),
        comm=comm,
    )


def _rows_after(ext, k, n):
    return pltpu.roll(ext, n + SUBLANES - k, 0)[:n, :]


def _scan_forward(a, b, n):
    row = lax.broadcasted_iota(jnp.int32, a.shape, 0)
    d = 1
    while d < n:
        if d < SUBLANES:
            m = row >= d
            a_s = jnp.where(m, pltpu.roll(a, d, 0), 1.0)
            b_s = jnp.where(m, pltpu.roll(b, d, 0), 0.0)
            b = a * b_s + b
            a = a * a_s
        else:
            b = jnp.concatenate([b[:d], a[d:] * b[: n - d] + b[d:]], axis=0)
            a = jnp.concatenate([a[:d], a[d:] * a[: n - d]], axis=0)
        d *= 2
    return a, b


def _scan_backward(a, b, n):
    row = lax.broadcasted_iota(jnp.int32, a.shape, 0)
    d = 1
    while d < n:
        if d < SUBLANES:
            m = row < n - d
            a_s = jnp.where(m, pltpu.roll(a, n - d, 0), 1.0)
            b_s = jnp.where(m, pltpu.roll(b, n - d, 0), 0.0)
            b = a * b_s + b
            a = a * a_s
        else:
            b = jnp.concatenate([a[: n - d] * b[d:] + b[: n - d], b[n - d :]], axis=0)
            a = jnp.concatenate([a[: n - d] * a[d:], a[n - d :]], axis=0)
        d *= 2
    return b


def _repeat_matrix(n):
    groups = n // SUBLANES
    return (jnp.arange(n)[:, None] // SUBLANES == jnp.arange(3 * groups)[None, :] % groups).astype(BF16)


def _scan_rows(a, b, n, repeat_ref, a_scr, b_scr, reverse):
    groups = n // SUBLANES
    a3 = a.reshape(groups, SUBLANES, LANES)
    b3 = b.reshape(groups, SUBLANES, LANES)
    sub = lax.broadcasted_iota(jnp.int32, a3.shape, 1)
    for d in (1, 2, 4):
        m = (sub < SUBLANES - d) if reverse else (sub >= d)
        shift = SUBLANES - d if reverse else d
        a_s = jnp.where(m, pltpu.roll(a3, shift, 1), 1.0)
        b_s = jnp.where(m, pltpu.roll(b3, shift, 1), 0.0)
        b3 = a3 * b_s + b3
        a3 = a3 * a_s
    a_scr[...] = a3.reshape(n, LANES)
    b_scr[...] = b3.reshape(n, LANES)
    edge = 0 if reverse else SUBLANES - 1
    a_tot = a_scr[pl.ds(edge, groups, stride=SUBLANES), :]
    b_tot = b_scr[pl.ds(edge, groups, stride=SUBLANES), :]
    row = lax.broadcasted_iota(jnp.int32, a_tot.shape, 0)
    if reverse:
        through = _scan_backward(a_tot, b_tot, groups)
        entering = jnp.where(row < groups - 1, pltpu.roll(through, groups - 1, 0), 0.0)
    else:
        _, through = _scan_forward(a_tot, b_tot, groups)
        entering = jnp.where(row >= 1, pltpu.roll(through, 1, 0), 0.0)
    hi = entering.astype(BF16)
    rest = entering - hi.astype(F32)
    mid = rest.astype(BF16)
    lo = (rest - mid.astype(F32)).astype(BF16)
    repeated = _dot(repeat_ref[...], jnp.concatenate([hi, mid, lo], axis=0))
    return b_scr[...] + a_scr[...] * repeated


def _softplus_neg(lam):
    z = -lam
    return jnp.maximum(z, 0.0) + jnp.log1p(jnp.exp(-jnp.abs(z)))


def _conv_and_gates(xc, xprev, cw_ref, cb_ref, wa_ref, ba_ref, wx_ref, bx_ref, lam_ref, ext_scr):
    n = xc.shape[0]
    ext_scr[:SUBLANES, :] = xprev
    ext_scr[SUBLANES:, :] = xc
    x1, x2, x3 = (ext_scr[pl.ds(SUBLANES - k, n), :] for k in (1, 2, 3))
    xr = cb_ref[...] + x3 * cw_ref[0:1, :] + x2 * cw_ref[1:2, :] + x1 * cw_ref[2:3, :] + xc * cw_ref[3:4, :]
    xrb = xr.astype(BF16)
    r = _sigmoid(_dot(xrb, wa_ref[...]) + ba_ref[...])
    i = _sigmoid(_dot(xrb, wx_ref[...]) + bx_ref[...])
    sp = _softplus_neg(lam_ref[...])
    log_a = (-LRU_C * r) * sp
    a = jnp.exp(log_a)
    return xr, (x1, x2, x3), r, i, a, _one_minus_square(log_a, a)


def _branch_a_fwd(proj, cw, cb, wa2, ba, wx2, bx, lam, *, tc, name, comm=None):
    s = proj.shape[0]
    tc = min(tc, s)

    def body(x_ref, g_ref, cw_ref, cb_ref, wa_ref, ba_ref, wx_ref, bx_ref, lam_ref, rep_ref, h_ref, y_ref,
             xprev, hlast, a_scr, b_scr, ext_scr):
        @pl.when(pl.program_id(1) == 0)
        def _():
            xprev[...] = jnp.zeros_like(xprev)
            hlast[...] = jnp.zeros_like(hlast)

        for t in range(RNN_TILES_PER_STEP):
            cols = lambda ref: ref.at[:, pl.ds(t * LANES, LANES)]
            one_tile(
                cols(x_ref), cols(g_ref), cols(cw_ref), cols(cb_ref), wa_ref.at[t], cols(ba_ref), wx_ref.at[t], cols(bx_ref),
                cols(lam_ref), rep_ref, cols(h_ref), cols(y_ref), cols(xprev), cols(hlast), a_scr.at[t], b_scr.at[t],
                ext_scr.at[t],
            )

    def one_tile(x_ref, g_ref, cw_ref, cb_ref, wa_ref, ba_ref, wx_ref, bx_ref, lam_ref, rep_ref, h_ref, y_ref,
                 xprev, hlast, a_scr, b_scr, ext_scr):
        xc = x_ref[...].astype(F32)
        xr, _, r, i, a, om = _conv_and_gates(
            xc, xprev[...], cw_ref, cb_ref, wa_ref, ba_ref, wx_ref, bx_ref, lam_ref, ext_scr
        )
        xprev[...] = xc[tc - SUBLANES :, :]
        u = jnp.sqrt(om) * (i * xr)
        row8 = lax.broadcasted_iota(jnp.int32, (SUBLANES, LANES), 0)
        first = u[:SUBLANES] + jnp.where(row8 == 0, a[:SUBLANES] * hlast[SUBLANES - 1 : SUBLANES, :], 0.0)
        h = _scan_rows(a, jnp.concatenate([first, u[SUBLANES:]], axis=0), tc, rep_ref, a_scr, b_scr, reverse=False)
        hlast[...] = h[tc - SUBLANES :, :]
        h_ref[...] = h
        y_ref[...] = (h * _gelu(g_ref[...].astype(F32))).astype(BF16)

    wide = RNN_TILES_PER_STEP * LANES
    tile = lambda j, c: (0, j)
    vec = pl.BlockSpec((1, wide), tile)
    mats = pl.BlockSpec((RNN_TILES_PER_STEP, LANES, LANES), lambda j, c: (j, 0, 0))
    repeat = _repeat_matrix(tc)
    return _call(
        body,
        (proj, proj, cw, cb, wa2, ba, wx2, bx, lam, repeat),
        name=name,
        grid=(N_RNN_TILES // RNN_TILES_PER_STEP, s // tc),
        in_specs=[
            pl.BlockSpec((tc, wide), lambda j, c: (c, j)),
            pl.BlockSpec((tc, wide), lambda j, c: (c, D_RNN // wide + j)),
            pl.BlockSpec((CONV_WIDTH, wide), tile),
            vec,
            mats,
            vec,
            mats,
            vec,
            vec,
            pl.BlockSpec(repeat.shape, lambda j, c: (0, 0)),
        ],
        out_specs=[pl.BlockSpec((tc, wide), lambda j, c: (c, j)), pl.BlockSpec((tc, wide), lambda j, c: (c, j))],
        out_shape=[SDS((s, D_RNN), F32), SDS((s, D_RNN), BF16)],
        scratch_shapes=[pltpu.VMEM((SUBLANES, wide), F32)] * 2
        + [pltpu.VMEM((RNN_TILES_PER_STEP, tc, LANES), F32)] * 2
        + [pltpu.VMEM((RNN_TILES_PER_STEP, tc + SUBLANES, LANES), F32)],
        semantics=("parallel", "arbitrary"),
        comm=comm,
    )


def _branch_a_bwd(dy, proj, h, cw, cb, wa2, ba, wx2, bx, lam, wa2t, wx2t, *, tc, name, comm=None):
    s = proj.shape[0]
    tc = min(tc, s)
    nc = s // tc
    halo16 = tc // 16
    halo8 = tc // SUBLANES

    def body(dy_ref, x_ref, xh_ref, g_ref, h_ref, hh_ref, cw_ref, cb_ref, wa_ref, ba_ref, wx_ref, bx_ref, lam_ref,
             wat_ref, wxt_ref, rep_ref, dx_ref, dg_ref, dcw_ref, dcb_ref, dba_ref, dbx_ref, dlam_ref, dwa_ref, dwx_ref,
             carry, dxr_next, a_scr, b_scr, ext_scr):
        cc = pl.program_id(1)
        ct = nc - 1 - cc

        @pl.when(cc == 0)
        def _():
            carry[...] = jnp.zeros_like(carry)
            dxr_next[...] = jnp.zeros_like(dxr_next)
            for ref in (dcw_ref, dcb_ref, dba_ref, dbx_ref, dlam_ref, dwa_ref, dwx_ref):
                ref[...] = jnp.zeros_like(ref)

        for t in range(RNN_TILES_PER_STEP):
            cols = lambda ref: ref.at[:, pl.ds(t * LANES, LANES)]
            one_tile(
                ct, cols(dy_ref), cols(x_ref), cols(xh_ref), cols(g_ref), cols(h_ref), cols(hh_ref), cols(cw_ref), cols(cb_ref),
                wa_ref.at[t], cols(ba_ref), wx_ref.at[t], cols(bx_ref), cols(lam_ref), wat_ref.at[t], wxt_ref.at[t], rep_ref,
                cols(dx_ref), cols(dg_ref), cols(dcw_ref), cols(dcb_ref), cols(dba_ref), cols(dbx_ref), cols(dlam_ref),
                dwa_ref.at[t], dwx_ref.at[t], cols(carry), cols(dxr_next), a_scr.at[t], b_scr.at[t], ext_scr.at[t],
            )

    def one_tile(ct, dy_ref, x_ref, xh_ref, g_ref, h_ref, hh_ref, cw_ref, cb_ref, wa_ref, ba_ref, wx_ref, bx_ref, lam_ref,
                 wat_ref, wxt_ref, rep_ref, dx_ref, dg_ref, dcw_ref, dcb_ref, dba_ref, dbx_ref, dlam_ref, dwa_ref, dwx_ref,
                 carry, dxr_next, a_scr, b_scr, ext_scr):
        xc = x_ref[...].astype(F32)
        xprev = jnp.where(ct > 0, xh_ref[SUBLANES:, :].astype(F32), 0.0)
        xr, (x1, x2, x3), r, i, a, om = _conv_and_gates(
            xc, xprev, cw_ref, cb_ref, wa_ref, ba_ref, wx_ref, bx_ref, lam_ref, ext_scr
        )
        inv_norm = lax.rsqrt(om)
        norm = om * inv_norm
        row = lax.broadcasted_iota(jnp.int32, xc.shape, 0)

        hv = h_ref[...]
        ge, ge_grad = _gelu_and_grad(g_ref[...].astype(F32))
        dyv = dy_ref[...].astype(F32)
        dg_ref[...] = (dyv * hv * ge_grad).astype(dg_ref.dtype)
        dh = dyv * ge

        b = dh + jnp.where(row == tc - 1, carry[0:1, :], 0.0)
        a_next = jnp.where(row < tc - 1, pltpu.roll(a, tc - 1, 0), 0.0)
        gadj = _scan_rows(a_next, b, tc, rep_ref, a_scr, b_scr, reverse=True)
        carry[...] = (a * gadj)[:SUBLANES, :]

        hprev_first = jnp.where(ct > 0, hh_ref[SUBLANES - 1 : SUBLANES, :], 0.0)
        hprev = jnp.where(row >= 1, pltpu.roll(hv, 1, 0), hprev_first)
        da = gadj * hprev
        ix = i * xr
        dnorm = gadj * ix
        di = gadj * norm * xr
        dlog_a = da * a - dnorm * (1.0 - om) * inv_norm
        sp = _softplus_neg(lam_ref[...])
        dr = dlog_a * (-LRU_C * sp)
        dsp = jnp.sum(dlog_a * (-LRU_C * r), axis=0, keepdims=True)
        dlam_ref[...] += dsp * (-_sigmoid(-lam_ref[...]))
        dza = dr * r * (1.0 - r)
        dzx = di * i * (1.0 - i)
        dzab, dzxb = dza.astype(BF16), dzx.astype(BF16)
        dxr = gadj * norm * i + _dot(dzab, wat_ref[...]) + _dot(dzxb, wxt_ref[...])
        xrb = xr.astype(BF16)
        dwa_ref[...] += _dot_tn(xrb, dzab)
        dwx_ref[...] += _dot_tn(xrb, dzxb)
        dba_ref[...] += jnp.sum(dza, axis=0, keepdims=True)
        dbx_ref[...] += jnp.sum(dzx, axis=0, keepdims=True)

        ext = jnp.concatenate([dxr, dxr_next[...]], axis=0)
        dx = (
            dxr * cw_ref[3:4, :]
            + _rows_after(ext, 1, tc) * cw_ref[2:3, :]
            + _rows_after(ext, 2, tc) * cw_ref[1:2, :]
            + _rows_after(ext, 3, tc) * cw_ref[0:1, :]
        )
        dxr_next[...] = dxr[:SUBLANES, :]
        dx_ref[...] = dx.astype(dx_ref.dtype)
        dcb_ref[...] += jnp.sum(dxr, axis=0, keepdims=True)
        dcw_ref[3:4, :] += jnp.sum(dxr * xc, axis=0, keepdims=True)
        dcw_ref[2:3, :] += jnp.sum(dxr * x1, axis=0, keepdims=True)
        dcw_ref[1:2, :] += jnp.sum(dxr * x2, axis=0, keepdims=True)
        dcw_ref[0:1, :] += jnp.sum(dxr * x3, axis=0, keepdims=True)

    wide = RNN_TILES_PER_STEP * LANES
    tile = lambda j, c: (0, j)
    mat = lambda j, c: (j, 0, 0)
    cur = lambda j, c: (nc - 1 - c, j)
    vec = pl.BlockSpec((1, wide), tile)
    matspec = pl.BlockSpec((RNN_TILES_PER_STEP, LANES, LANES), mat)
    repeat = _repeat_matrix(tc)
    return _call(
        body,
        (dy, proj, proj, proj, h, h, cw, cb, wa2, ba, wx2, bx, lam, wa2t, wx2t, repeat),
        name=name,
        grid=(N_RNN_TILES // RNN_TILES_PER_STEP, nc),
        in_specs=[
            pl.BlockSpec((tc, wide), cur),
            pl.BlockSpec((tc, wide), cur),
            pl.BlockSpec((16, wide), lambda j, c: (jnp.maximum((nc - 1 - c) * halo16 - 1, 0), j)),
            pl.BlockSpec((tc, wide), lambda j, c: (nc - 1 - c, D_RNN // wide + j)),
            pl.BlockSpec((tc, wide), cur),
            pl.BlockSpec((SUBLANES, wide), lambda j, c: (jnp.maximum((nc - 1 - c) * halo8 - 1, 0), j)),
            pl.BlockSpec((CONV_WIDTH, wide), tile),
            vec,
            matspec,
            vec,
            matspec,
            vec,
            vec,
            matspec,
            matspec,
            pl.BlockSpec(repeat.shape, lambda j, c: (0, 0)),
        ],
        out_specs=[
            pl.BlockSpec((tc, wide), cur),
            pl.BlockSpec((tc, wide), cur),
            pl.BlockSpec((CONV_WIDTH, wide), tile),
            vec,
            vec,
            vec,
            vec,
            matspec,
            matspec,
        ],
        out_shape=[
            SDS((s, D_RNN), BF16),
            SDS((s, D_RNN), BF16),
            SDS((CONV_WIDTH, D_RNN), F32),
            SDS((1, D_RNN), F32),
            SDS((1, D_RNN), F32),
            SDS((1, D_RNN), F32),
            SDS((1, D_RNN), F32),
            SDS((N_RNN_TILES, LANES, LANES), F32),
            SDS((N_RNN_TILES, LANES, LANES), F32),
        ],
        scratch_shapes=[pltpu.VMEM((SUBLANES, wide), F32)] * 2
        + [pltpu.VMEM((RNN_TILES_PER_STEP, tc, LANES), F32)] * 2
        + [pltpu.VMEM((RNN_TILES_PER_STEP, tc + SUBLANES, LANES), F32)],
        semantics=("parallel", "arbitrary"),
        comm=comm,
    )


def _sgu_specs(tb):
    half = lambda blk: pl.BlockSpec((tb, 512), lambda n: (n, blk))
    return [half(U_BLK512), half(U_BLK512 + 1), half(V_BLK512), half(V_BLK512 + 1)]


def _sgu_normed(v, lng_ref, lnb_ref):
    gv, gv_grad = _gelu_and_grad(v)
    mu = jnp.mean(gv, axis=-1, keepdims=True)
    xc = gv - mu
    rs = lax.rsqrt(jnp.mean(xc * xc, axis=-1, keepdims=True) + EPS)
    xhat = xc * rs
    return xhat * lng_ref[...] + lnb_ref[...], xhat, rs, gv_grad


def _sgu_fwd(proj, lng, lnb, wm, bias, *, tb, name, comm=None):
    s = proj.shape[0]
    tb = min(tb, s)

    def body(u0_ref, u1_ref, v0_ref, v1_ref, lng_ref, lnb_ref, wm_ref, bias_ref, y_ref):
        u = jnp.concatenate([u0_ref[...], u1_ref[...]], axis=1).astype(F32)
        v = jnp.concatenate([v0_ref[...], v1_ref[...]], axis=1).astype(F32)
        gu = _gelu(u)
        vn, _, _, _ = _sgu_normed(v, lng_ref, lnb_ref)
        vnb = vn.astype(BF16)
        for blk in range(tb // SGU_BLOCK):
            rows = slice(blk * SGU_BLOCK, (blk + 1) * SGU_BLOCK)
            for g in range(SGU_GROUPS):
                cols = slice(g * LANES, (g + 1) * LANES)
                mixed = _dot(wm_ref[g], vnb[rows, cols]) + bias_ref[g]
                y_ref[rows, cols] = (gu[rows, cols] * mixed).astype(BF16)

    const2 = lambda n: (0, 0)
    const3 = lambda n: (0, 0, 0)
    return _call(
        body,
        (proj, proj, proj, proj, lng, lnb, wm, bias),
        name=name,
        grid=(s // tb,),
        in_specs=_sgu_specs(tb)
        + [
            pl.BlockSpec((1, D_SGU), const2),
            pl.BlockSpec((1, D_SGU), const2),
            pl.BlockSpec((SGU_GROUPS, SGU_BLOCK, SGU_BLOCK), const3),
            pl.BlockSpec((SGU_GROUPS, SGU_BLOCK, LANES), const3),
        ],
        out_specs=pl.BlockSpec((tb, D_SGU), lambda n: (n, 0)),
        out_shape=SDS((s, D_SGU), BF16),
        semantics=("parallel",),
        comm=comm,
    )


def _sgu_bwd(dy, proj, lng, lnb, wm, wmt, bias, mask, *, tb, name, comm=None):
    s = proj.shape[0]
    tb = min(tb, s)
    nb = s // tb

    def body(dy_ref, u0_ref, u1_ref, v0_ref, v1_ref, lng_ref, lnb_ref, wm_ref, wmt_ref, bias_ref, mask_ref,
             du_ref, dv_ref, dws_ref, dbs_ref, dlng_ref, dlnb_ref, dvn_scr, dbs_acc):
        n = pl.program_id(0)

        @pl.when(n == 0)
        def _():
            dbs_acc[...] = jnp.zeros_like(dbs_acc)
            for ref in (dws_ref, dlng_ref, dlnb_ref):
                ref[...] = jnp.zeros_like(ref)

        u = jnp.concatenate([u0_ref[...], u1_ref[...]], axis=1).astype(F32)
        v = jnp.concatenate([v0_ref[...], v1_ref[...]], axis=1).astype(F32)
        gu, gu_grad = _gelu_and_grad(u)
        vn, xhat, rs, gv_grad = _sgu_normed(v, lng_ref, lnb_ref)
        vnb = vn.astype(BF16)
        dyv = dy_ref[...].astype(F32)
        for blk in range(tb // SGU_BLOCK):
            rows = slice(blk * SGU_BLOCK, (blk + 1) * SGU_BLOCK)
            for g in range(SGU_GROUPS):
                cols = slice(g * LANES, (g + 1) * LANES)
                vt = vnb[rows, cols]
                mixed = _dot(wm_ref[g], vt) + bias_ref[g]
                dyt = dyv[rows, cols]
                du_ref[rows, cols] = (dyt * mixed * gu_grad[rows, cols]).astype(BF16)
                dmix = dyt * gu[rows, cols]
                dmixb = dmix.astype(BF16)
                dvn_scr[rows, cols] = _dot(wmt_ref[g], dmixb)
                dws_ref[g] += _dot_nt(dmixb, vt) * mask_ref[...]
                dbs_acc[g] += dmix
        dvn = dvn_scr[...]
        dlng_ref[...] += jnp.sum(dvn * xhat, axis=0, keepdims=True)
        dlnb_ref[...] += jnp.sum(dvn, axis=0, keepdims=True)
        dxh = dvn * lng_ref[...]
        dgv = rs * (
            dxh - jnp.mean(dxh, axis=-1, keepdims=True) - xhat * jnp.mean(dxh * xhat, axis=-1, keepdims=True)
        )
        dv_ref[...] = (dgv * gv_grad).astype(BF16)

        @pl.when(n == nb - 1)
        def _():
            for g in range(SGU_GROUPS):
                dbs_ref[g] = jnp.broadcast_to(jnp.sum(dbs_acc[g], axis=-1, keepdims=True), (SGU_BLOCK, LANES))

    const2 = lambda n: (0, 0)
    const3 = lambda n: (0, 0, 0)
    gmat = pl.BlockSpec((SGU_GROUPS, SGU_BLOCK, SGU_BLOCK), const3)
    vec = pl.BlockSpec((1, D_SGU), const2)
    act = pl.BlockSpec((tb, D_SGU), lambda n: (n, 0))
    return _call(
        body,
        (dy, proj, proj, proj, proj, lng, lnb, wm, wmt, bias, mask),
        name=name,
        grid=(nb,),
        in_specs=[act] + _sgu_specs(tb) + [vec, vec, gmat, gmat, gmat, pl.BlockSpec((SGU_BLOCK, SGU_BLOCK), const2)],
        out_specs=[act, act, gmat, gmat, vec, vec],
        out_shape=[
            SDS((s, D_SGU), BF16),
            SDS((s, D_SGU), BF16),
            SDS((SGU_GROUPS, SGU_BLOCK, SGU_BLOCK), F32),
            SDS((SGU_GROUPS, SGU_BLOCK, LANES), F32),
            SDS((1, D_SGU), F32),
            SDS((1, D_SGU), F32),
        ],
        scratch_shapes=[pltpu.VMEM((tb, D_SGU), F32), pltpu.VMEM((SGU_GROUPS, SGU_BLOCK, LANES), F32)],
        semantics=("arbitrary",),
        comm=comm,
    )


def _gate_specs(tm):
    half = lambda blk: pl.BlockSpec((tm, 512), lambda i: (i, blk))
    return [half(GA_BLK512), half(GA_BLK512 + 1), half(GB_BLK512), half(GB_BLK512 + 1)]


def _merge_fwd(ya_pre, yb_pre, proj, x, w_ba, w_bb, w_out, *, tm, name, comm=None):
    s = x.shape[0]
    tm = min(tm, s)

    def body(ya_ref, yb_ref, a0, a1, b0, b1, x_ref, wa_ref, wb_ref, wo_ref, x1_ref, yao_ref, ybo_ref):
        ya = _dot(ya_ref[...], wa_ref[...])
        yb = _dot(yb_ref[...], wb_ref[...])
        sa = _sigmoid(jnp.concatenate([a0[...], a1[...]], axis=1).astype(F32))
        sb = _sigmoid(jnp.concatenate([b0[...], b1[...]], axis=1).astype(F32))
        merged = sa * ya + sb * yb
        x1_ref[...] = x_ref[...] + _dot(merged.astype(BF16), wo_ref[...])
        yao_ref[...] = ya.astype(BF16)
        ybo_ref[...] = yb.astype(BF16)

    whole = lambda r: pl.BlockSpec((r, D), lambda i: (0, 0))
    act = pl.BlockSpec((tm, D), lambda i: (i, 0))
    return _call(
        body,
        (ya_pre, yb_pre, proj, proj, proj, proj, x, w_ba, w_bb, w_out),
        name=name,
        grid=(s // tm,),
        in_specs=[pl.BlockSpec((tm, D_RNN), lambda i: (i, 0)), act] + _gate_specs(tm) + [act, whole(D_RNN), whole(D_SGU), whole(D)],
        out_specs=[act, act, act],
        out_shape=[SDS((s, D), F32), SDS((s, D), BF16), SDS((s, D), BF16)],
        semantics=("parallel",),
        comm=comm,
    )


def _merge_bwd(dx1, ya, yb, proj, w_ba, w_bb, w_out, *, tm, name, comm=None):
    s = dx1.shape[0]
    tm = min(tm, s)

    def body(dx_ref, ya_ref, yb_ref, a0, a1, b0, b1, wa_ref, wb_ref, wo_ref,
             mg_ref, dya_ref, dyb_ref, dga_ref, dgb_ref, dyap_ref, dybp_ref):
        dm = _dot_nt(dx_ref[...], wo_ref[...])
        ya = ya_ref[...].astype(F32)
        yb = yb_ref[...].astype(F32)
        sa = _sigmoid(jnp.concatenate([a0[...], a1[...]], axis=1).astype(F32))
        sb = _sigmoid(jnp.concatenate([b0[...], b1[...]], axis=1).astype(F32))
        mg_ref[...] = (sa * ya + sb * yb).astype(BF16)
        dya = (dm * sa).astype(BF16)
        dyb = (dm * sb).astype(BF16)
        dya_ref[...] = dya
        dyb_ref[...] = dyb
        dga_ref[...] = (dm * ya * sa * (1.0 - sa)).astype(BF16)
        dgb_ref[...] = (dm * yb * sb * (1.0 - sb)).astype(BF16)
        dyap_ref[...] = _dot_nt(dya, wa_ref[...]).astype(BF16)
        dybp_ref[...] = _dot_nt(dyb, wb_ref[...]).astype(BF16)

    whole = lambda r: pl.BlockSpec((r, D), lambda i: (0, 0))
    act = pl.BlockSpec((tm, D), lambda i: (i, 0))
    act_rnn = pl.BlockSpec((tm, D_RNN), lambda i: (i, 0))
    return _call(
        body,
        (dx1, ya, yb, proj, proj, proj, proj, w_ba, w_bb, w_out),
        name=name,
        grid=(s // tm,),
        in_specs=[act, act, act] + _gate_specs(tm) + [whole(D_RNN), whole(D_SGU), whole(D)],
        out_specs=[act, act, act, act, act, act_rnn, act],
        out_shape=[SDS((s, D), BF16)] * 5 + [SDS((s, D_RNN), BF16), SDS((s, D_SGU), BF16)],
        semantics=("parallel",),
        comm=comm,
    )


def _ffn_down_loss(a, w, res, g, target, *, tm, name):
    s, k = a.shape
    tm = min(tm, s)

    def body(a_ref, w_ref, r_ref, g_ref, t_ref, dx_ref, dxb_ref, dg_ref, loss_ref):
        @pl.when(pl.program_id(0) == 0)
        def _():
            dg_ref[...] = jnp.zeros_like(dg_ref)
            loss_ref[...] = jnp.zeros_like(loss_ref)

        t = jnp.maximum(a_ref[...].astype(F32), 0.0)
        xv = r_ref[...] + _dot((t * t).astype(BF16), w_ref[...])
        r = lax.rsqrt(jnp.mean(xv * xv, axis=-1, keepdims=True) + EPS)
        xhat = xv * r
        e = xhat * g_ref[...] - t_ref[...]
        loss_ref[...] += 0.5 * jnp.sum(jnp.mean(e * e, axis=-1, keepdims=True), axis=0, keepdims=True)
        dy = e * (1.0 / D)
        dxh = dy * g_ref[...]
        dx = r * (dxh - xhat * jnp.mean(dxh * xhat, axis=-1, keepdims=True))
        dx_ref[...] = dx
        dxb_ref[...] = dx.astype(BF16)
        dg_ref[...] += jnp.sum(dy * xhat, axis=0, keepdims=True)

    act = pl.BlockSpec((tm, D), lambda i: (i, 0))
    vec = pl.BlockSpec((1, D), lambda i: (0, 0))
    return pl.pallas_call(
        body,
        name=name,
        grid=(s // tm,),
        in_specs=[pl.BlockSpec((tm, k), lambda i: (i, 0)), pl.BlockSpec((k, D), lambda i: (0, 0)), act, vec, act],
        out_specs=[act, act, vec, pl.BlockSpec((SUBLANES, LANES), lambda i: (0, 0))],
        out_shape=[SDS((s, D), F32), SDS((s, D), BF16), SDS((1, D), F32), SDS((SUBLANES, LANES), F32)],
        compiler_params=_params("arbitrary"),
    )(a, w, res, g, target)


def _adamw_math(w, g, m, v):
    m2 = ADAM_B1 * m + (1.0 - ADAM_B1) * g
    v2 = ADAM_B2 * v + (1.0 - ADAM_B2) * (g * g)
    m_hat = m2 / (1.0 - ADAM_B1**ADAM_STEP)
    v_hat = v2 / (1.0 - ADAM_B2**ADAM_STEP)
    delta = -ADAM_LR * (m_hat / (jnp.sqrt(v_hat) + ADAM_EPS) + ADAM_WD * w)
    return delta, m2, v2


def _row_tile(rows, cap):
    return max(t for t in range(SUBLANES, min(cap, rows) + 1, SUBLANES) if rows % t == 0)


def _adamw_layers(w, grads, m, v, *, tr, name):
    depth, r, c = w.shape
    tr = _row_tile(r, tr)

    def body(*refs):
        g_refs = refs[:depth]
        w_ref, m_ref, v_ref, g_out, d_ref, mo_ref, vo_ref = refs[depth:]
        for l in range(depth):

            @pl.when(pl.program_id(0) == l)
            def _(l=l):
                g = g_refs[l][...]
                g_out[...] = g
                d_ref[...], mo_ref[...], vo_ref[...] = _adamw_math(w_ref[...], g, m_ref[...], v_ref[...])

    def of_layer(ll):
        return pl.BlockSpec((tr, c), lambda l, i: (jnp.where(l == ll, i, 0), 0))

    stacked = pl.BlockSpec((None, tr, c), lambda l, i: (l, i, 0))
    return pl.pallas_call(
        body,
        name=name,
        grid=(depth, r // tr),
        in_specs=[of_layer(ll) for ll in range(depth)] + [stacked] * 3,
        out_specs=[stacked] * 4,
        out_shape=[SDS((depth, r, c), F32)] * 4,
        compiler_params=_params("parallel", "parallel"),
    )(*grads, w, m, v)


def _adamw_reduced(w, parts, from_chips, m, v, chip, *, tr, name):
    depth, r, _ = w.shape
    tr = _row_tile(r, tr)

    def body(chip_ref, *refs):
        p_refs, c_refs = refs[:depth], refs[depth : 2 * depth]
        w_ref, m_ref, v_ref, g_out, d_ref, mo_ref, vo_ref = refs[2 * depth :]
        for l in range(depth):

            @pl.when(pl.program_id(0) == l)
            def _(l=l):
                got = c_refs[l]
                g = ((p_refs[l][...].astype(F32) + got[0].astype(F32)) + got[1].astype(F32)) + got[2].astype(F32)
                g_out[...] = g
                d_ref[...], mo_ref[...], vo_ref[...] = _adamw_math(w_ref[...], g, m_ref[...], v_ref[...])

    def mine_of_layer(ll):
        return pl.BlockSpec((None, tr, D), lambda l, i, chip_ref: (chip_ref[0], jnp.where(l == ll, i, 0), 0))

    def theirs_of_layer(ll):
        return pl.BlockSpec((3, tr, D), lambda l, i, chip_ref: (0, jnp.where(l == ll, i, 0), 0))

    stacked = pl.BlockSpec((None, tr, D), lambda l, i, chip_ref: (l, i, 0))
    return pl.pallas_call(
        body,
        name=name,
        grid_spec=pltpu.PrefetchScalarGridSpec(
            num_scalar_prefetch=1,
            grid=(depth, r // tr),
            in_specs=[mine_of_layer(ll) for ll in range(depth)]
            + [theirs_of_layer(ll) for ll in range(depth)]
            + [stacked] * 3,
            out_specs=[stacked] * 4,
        ),
        out_shape=[SDS((depth, r, D), F32)] * 4,
        compiler_params=_params("parallel", "parallel"),
    )(chip, *parts, *from_chips, w, m, v)


def _adamw_small(groups, *, name):
    n = len(groups)

    def body(*refs):
        ins, outs = refs[: 4 * n], refs[4 * n :]
        for i in range(n):
            w, g, m, v = (ref[...] for ref in ins[4 * i : 4 * i + 4])
            outs[3 * i][...], outs[3 * i + 1][...], outs[3 * i + 2][...] = _adamw_math(w, g, m, v)

    vmem = pl.BlockSpec(memory_space=pltpu.VMEM)
    outs = pl.pallas_call(
        body,
        name=name,
        in_specs=[vmem] * (4 * n),
        out_specs=[vmem] * (3 * n),
        out_shape=[SDS(grp[0].shape, F32) for grp in groups for _ in range(3)],
        compiler_params=pltpu.CompilerParams(vmem_limit_bytes=VMEM_LIMIT_BYTES),
    )(*[a for grp in groups for a in grp])
    return [tuple(outs[3 * i : 3 * i + 3]) for i in range(n)]


ANY = pl.BlockSpec(memory_space=pl.ANY)


def _position():
    return lax.axis_index("x"), lax.axis_index("y"), lax.axis_index("c")


def _other_chips(x, y):
    return [(1 - x, y), (x, 1 - y), (1 - x, 1 - y)]


class _Comm:
    def __init__(self, inputs, out_shapes, sem_counts, start, finish, aliases=()):
        self.inputs, self.out_shapes, self.sem_counts = list(inputs), list(out_shapes), list(sem_counts)
        self.start, self.finish = start, finish
        self.aliases = list(aliases)

    def sem_shapes(self):
        return [pltpu.SemaphoreType.DMA((n,)) for n in self.sem_counts]


def _merge_comms(comms):
    bounds, i, o, s = [], 0, 0, 0
    for cm in comms:
        bounds.append((i, i + len(cm.inputs), o, o + len(cm.out_shapes), s, s + len(cm.sem_counts)))
        i, o, s = bounds[-1][1], bounds[-1][3], bounds[-1][5]

    def phase(which):
        def run(ins, outs, sems):
            for cm, (i0, i1, o0, o1, s0, s1) in zip(comms, bounds):
                getattr(cm, which)(ins[i0:i1], outs[o0:o1], sems[s0:s1])

        return run

    return _Comm(
        [a for cm in comms for a in cm.inputs],
        [a for cm in comms for a in cm.out_shapes],
        [a for cm in comms for a in cm.sem_counts],
        phase("start"),
        phase("finish"),
        aliases=[(i0 + i, o0 + o) for cm, (i0, _, o0, _, _, _) in zip(comms, bounds) for i, o in cm.aliases],
    )


def _call(body, args, *, semantics, comm=None, **kw):
    if comm is None:
        return pl.pallas_call(body, compiler_params=_params(*semantics), **kw)(*args)
    grid, in_specs, out_specs, out_shape = kw["grid"], kw["in_specs"], kw["out_specs"], kw["out_shape"]
    scratch = list(kw.get("scratch_shapes", ()))
    single = not isinstance(out_shape, (list, tuple))
    core_specs = [out_specs] if single else list(out_specs)
    core_shapes = [out_shape] if single else list(out_shape)
    n_in, n_out, n_scr = len(in_specs), len(core_shapes), len(scratch)
    n_cin, n_cout = len(comm.inputs), len(comm.out_shapes)
    steps = 1
    for g in grid:
        steps *= g

    def hosted(*refs):
        core_in, c_in = refs[:n_in], refs[n_in : n_in + n_cin]
        o0 = n_in + n_cin
        core_out, c_out = refs[o0 : o0 + n_out], refs[o0 + n_out : o0 + n_out + n_cout]
        s0 = o0 + n_out + n_cout
        core_scr, sems = refs[s0 : s0 + n_scr], refs[s0 + n_scr :]
        step = pl.program_id(0)
        for d in range(1, len(grid)):
            step = step * grid[d] + pl.program_id(d)

        @pl.when(step == 0)
        def _():
            comm.start(c_in, c_out, sems)

        body(*core_in, *core_out, *core_scr)

        @pl.when(step == steps - 1)
        def _():
            comm.finish(c_in, c_out, sems)

    outs = pl.pallas_call(
        hosted,
        name=kw["name"],
        grid=grid,
        in_specs=list(in_specs) + [ANY] * n_cin,
        out_specs=core_specs + [ANY] * n_cout,
        out_shape=core_shapes + comm.out_shapes,
        scratch_shapes=scratch + comm.sem_shapes(),
        input_output_aliases={n_in + i: n_out + o for i, o in comm.aliases},
        compiler_params=_params(*(["arbitrary"] * len(grid))),
    )(*args, *comm.inputs)
    return (outs[0] if single else outs[:n_out]), outs[n_out:]


def _comm_only(comm, *, name):
    n_cin, n_cout = len(comm.inputs), len(comm.out_shapes)

    def body(*refs):
        ins, outs, sems = refs[:n_cin], refs[n_cin : n_cin + n_cout], refs[n_cin + n_cout :]
        comm.start(ins, outs, sems)
        comm.finish(ins, outs, sems)

    return pl.pallas_call(
        body,
        name=name,
        in_specs=[ANY] * n_cin,
        out_specs=[ANY] * n_cout,
        out_shape=comm.out_shapes,
        scratch_shapes=comm.sem_shapes(),
    )(*comm.inputs)


def _gather_comm(shards):
    n = len(shards)
    per = 7

    def plan(ins, outs, sems):
        send, recv, local = sems
        x, y, c = _position()
        me, sibling = (x, y, c), (x, y, 1 - c)
        chips = _other_chips(x, y)

        def block(t, px, py, pc):
            return outs[t].at[pl.ds(4 * px + 2 * py + pc, 1)]

        def copy(t, k, blk, to, src=None):
            return pltpu.make_async_remote_copy(
                src_ref=block(t, *blk) if src is None else src,
                dst_ref=block(t, *blk),
                send_sem=send.at[t * per + k],
                recv_sem=recv.at[t * per + k],
                device_id=to,
                device_id_type=MESH,
            )

        mine = [pltpu.make_async_copy(ins[t], block(t, *me), local.at[t]) for t in range(n)]
        to_chips = [copy(t, 1 + j, me, (*chip, c), src=ins[t]) for t in range(n) for j, chip in enumerate(chips)]
        to_sibling = [copy(t, 0, me, sibling, src=ins[t]) for t in range(n)]
        from_chips = [copy(t, 1 + j, (*chip, c), me) for t in range(n) for j, chip in enumerate(chips)]
        passed_on = [copy(t, 4 + j, (*chip, c), sibling) for t in range(n) for j, chip in enumerate(chips)]
        from_sibling = [copy(t, 0, sibling, me) for t in range(n)]
        from_sibling += [copy(t, 4 + j, (*chip, 1 - c), me) for t in range(n) for j, chip in enumerate(chips)]
        return mine, to_chips, to_sibling, from_chips, passed_on, from_sibling

    def start(ins, outs, sems):
        mine, to_chips, to_sibling, _, _, _ = plan(ins, outs, sems)
        for cp in mine + to_chips + to_sibling:
            cp.start()

    def finish(ins, outs, sems):
        mine, to_chips, to_sibling, from_chips, passed_on, from_sibling = plan(ins, outs, sems)
        for arrived, onward in zip(from_chips, passed_on):
            arrived.wait_recv()
            onward.start()
        for cp in from_sibling:
            cp.wait_recv()
        for cp in to_chips + to_sibling + passed_on:
            cp.wait_send()
        for cp in mine:
            cp.wait()

    out_shapes = [SDS((N_DEV,) + sh.shape[1:], sh.dtype) for sh in shards]
    return _Comm(shards, out_shapes, [n * per, n * per, n], start, finish)


def _gather_stage(stage, shards=None, arrived=None):
    n = len(arrived if shards is None else shards)
    targets = {"near": (0, 1), "far": (2,), "first": (0, 1, 2), "pass": (0, 1, 2)}[stage]
    to_sibling = stage in ("near", "first")
    per = len(targets) + to_sibling

    def plan(ins, outs, sems):
        x, y, c = _position()
        me, sibling = (x, y, c), (x, y, 1 - c)
        chips = [_other_chips(x, y)[j] for j in targets]

        def block(t, px, py, pc):
            return outs[t].at[pl.ds(4 * px + 2 * py + pc, 1)]

        def copy(t, k, blk, to, src=None):
            return pltpu.make_async_remote_copy(
                src_ref=block(t, *blk) if src is None else src,
                dst_ref=block(t, *blk),
                send_sem=sems[0].at[t * per + k],
                recv_sem=sems[1].at[t * per + k],
                device_id=to,
                device_id_type=MESH,
            )

        local = []
        if stage == "pass":
            sent = [copy(t, j, (*chip, c), sibling) for t in range(n) for j, chip in enumerate(chips)]
            landing = [copy(t, j, (*chip, 1 - c), me) for t in range(n) for j, chip in enumerate(chips)]
        else:
            sent = [copy(t, j, me, (*chip, c), src=ins[t]) for t in range(n) for j, chip in enumerate(chips)]
            landing = [copy(t, j, (*chip, c), me) for t in range(n) for j, chip in enumerate(chips)]
            if to_sibling:
                local = [pltpu.make_async_copy(ins[t], block(t, *me), sems[2].at[t]) for t in range(n)]
                sent += [copy(t, per - 1, me, sibling, src=ins[t]) for t in range(n)]
                landing += [copy(t, per - 1, sibling, me) for t in range(n)]
        return local, sent, landing

    def start(ins, outs, sems):
        local, sent, _ = plan(ins, outs, sems)
        for cp in local + sent:
            cp.start()

    def finish(ins, outs, sems):
        local, sent, landing = plan(ins, outs, sems)
        for cp in landing:
            cp.wait_recv()
        for cp in sent:
            cp.wait_send()
        for cp in local:
            cp.wait()

    if to_sibling:
        out_shapes = [SDS((N_DEV,) + sh.shape[1:], sh.dtype) for sh in shards]
        return _Comm(shards, out_shapes, [n * per, n * per, n], start, finish)
    out_shapes = [SDS(a.shape, a.dtype) for a in arrived]
    if stage == "pass":
        return _Comm(arrived, out_shapes, [n * per, n * per], start, finish, aliases=[(t, t) for t in range(n)])
    return _Comm(list(shards) + list(arrived), out_shapes, [n * per, n * per], start, finish, aliases=[(n + t, t) for t in range(n)])


def _exchange_comm(arrays, out_shapes, n_copies, copies_of):
    def start(ins, outs, sems):
        for cp in copies_of(ins, outs, *sems):
            cp.start()

    def finish(ins, outs, sems):
        for cp in copies_of(ins, outs, *sems):
            cp.wait()

    return _Comm(arrays, out_shapes, [n_copies, n_copies], start, finish)


def _sibling_comm(grads):
    def copies_of(ins, outs, send, recv):
        x, y, c = _position()
        return [
            pltpu.make_async_remote_copy(
                src_ref=ins[t].at[:, pl.ds(1 - c, 1)],
                dst_ref=outs[t],
                send_sem=send.at[t],
                recv_sem=recv.at[t],
                device_id=(x, y, 1 - c),
                device_id_type=MESH,
            )
            for t in range(len(ins))
        ]

    return _exchange_comm(grads, [SDS((4, 1) + g.shape[2:], g.dtype) for g in grads], len(grads), copies_of)


def _chips_comm(parts):
    def copies_of(ins, outs, send, recv):
        x, y, c = _position()
        return [
            pltpu.make_async_remote_copy(
                src_ref=ins[t].at[pl.ds(2 * px + py, 1)],
                dst_ref=outs[t].at[pl.ds(k, 1)],
                send_sem=send.at[3 * t + k],
                recv_sem=recv.at[3 * t + k],
                device_id=(px, py, c),
                device_id_type=MESH,
            )
            for t in range(len(ins))
            for k, (px, py) in enumerate(_other_chips(x, y))
        ]

    return _exchange_comm(parts, [SDS((3,) + p.shape[1:], p.dtype) for p in parts], 3 * len(parts), copies_of)


def _sum_with_sibling(grad, got, core, *, name):
    rows = grad.shape[2]

    def body(core_ref, a_ref, b_ref, o_ref):
        o_ref[...] = (a_ref[...].astype(F32) + b_ref[...].astype(F32)).astype(o_ref.dtype)

    return pl.pallas_call(
        body,
        name=name,
        grid_spec=pltpu.PrefetchScalarGridSpec(
            num_scalar_prefetch=1,
            grid=(4,),
            in_specs=[
                pl.BlockSpec((None, None, rows, D), lambda q, core_ref: (q, core_ref[0], 0, 0)),
                pl.BlockSpec((None, None, rows, D), lambda q, core_ref: (q, 0, 0, 0)),
            ],
            out_specs=pl.BlockSpec((None, rows, D), lambda q, core_ref: (q, 0, 0)),
        ),
        out_shape=SDS((4, rows, D), grad.dtype),
        compiler_params=_params("parallel"),
    )(core, grad, got)


def _sum_chips(part, got, chip, *, name):
    rows = part.shape[1]

    def body(chip_ref, a_ref, b_ref, o_ref):
        o_ref[...] = ((a_ref[...].astype(F32) + b_ref[0].astype(F32)) + b_ref[1].astype(F32)) + b_ref[2].astype(F32)

    return pl.pallas_call(
        body,
        name=name,
        grid_spec=pltpu.PrefetchScalarGridSpec(
            num_scalar_prefetch=1,
            grid=(1,),
            in_specs=[
                pl.BlockSpec((None, rows, D), lambda i, chip_ref: (chip_ref[0], 0, 0)),
                pl.BlockSpec((3, rows, D), lambda i, chip_ref: (0, 0, 0)),
            ],
            out_specs=pl.BlockSpec((rows, D), lambda i, chip_ref: (0, 0)),
        ),
        out_shape=SDS((rows, D), F32),
        compiler_params=_params("arbitrary"),
    )(chip, part, got)


def _all_reduce_small(pack, *, name):
    rows = pack.shape[1]

    def body(in_ref, out_ref, from_sibling, part, from_chips, send, recv):
        x, y, c = _position()
        me, sibling = (x, y, c), (x, y, 1 - c)
        chips = _other_chips(x, y)
        waiting = []

        def copy(k, src, dst, to):
            return pltpu.make_async_remote_copy(
                src_ref=src, dst_ref=dst, send_sem=send.at[k], recv_sem=recv.at[k], device_id=to, device_id_type=MESH
            )

        def exchange(copies):
            for cp in copies:
                cp.start()
            for cp in copies:
                cp.wait_recv()
            waiting.extend(copies)

        def block(px, py, pc):
            return out_ref.at[4 * px + 2 * py + pc]

        exchange([copy(q, in_ref.at[2 * q + 1 - c], from_sibling.at[q], sibling) for q in range(4)])
        for q in range(4):
            part[q] = in_ref[2 * q + c] + from_sibling[q]
        exchange([copy(4 + k, part.at[2 * px + py], from_chips.at[k], (px, py, c)) for k, (px, py) in enumerate(chips)])
        out_ref[4 * x + 2 * y + c] = ((part[2 * x + y] + from_chips[0]) + from_chips[1]) + from_chips[2]
        exchange(
            [copy(7, block(*me), block(*me), sibling)]
            + [copy(8 + k, block(*me), block(*me), (px, py, c)) for k, (px, py) in enumerate(chips)]
        )
        exchange([copy(11 + k, block(px, py, c), block(px, py, c), sibling) for k, (px, py) in enumerate(chips)])
        for cp in waiting:
            cp.wait_send()

    vmem = pl.BlockSpec(memory_space=pltpu.VMEM)
    return pl.pallas_call(
        body,
        name=name,
        in_specs=[vmem],
        out_specs=vmem,
        out_shape=SDS(pack.shape, F32),
        scratch_shapes=[
            pltpu.VMEM((4, rows, D), F32),
            pltpu.VMEM((4, rows, D), F32),
            pltpu.VMEM((3, rows, D), F32),
            pltpu.SemaphoreType.DMA((14,)),
            pltpu.SemaphoreType.DMA((14,)),
        ],
        compiler_params=pltpu.CompilerParams(vmem_limit_bytes=VMEM_LIMIT_BYTES),
    )(pack)


def _pack(arrays, rows):
    flat = jnp.concatenate([a.reshape(-1).astype(F32) for a in arrays])
    return jnp.pad(flat, (0, rows * D - flat.shape[0])).reshape(rows, D)


def _unpack(pack, shapes):
    flat = pack.reshape(-1)
    out, off = [], 0
    for sh in shapes:
        size = 1
        for dim in sh:
            size *= dim
        out.append(flat[off : off + size].reshape(sh))
        off += size
    return out


def _block_diag_pairs(w):
    w = w.reshape(N_RNN_TILES, 2, HEAD_DIM, HEAD_DIM)
    z = jnp.zeros_like(w[:, 0])
    top = jnp.concatenate([w[:, 0], z], axis=2)
    bot = jnp.concatenate([z, w[:, 1]], axis=2)
    return jnp.concatenate([top, bot], axis=1)


def _diag_blocks(w2):
    a = w2[:, :HEAD_DIM, :HEAD_DIM]
    b = w2[:, HEAD_DIM:, HEAD_DIM:]
    return jnp.stack([a, b], axis=1).reshape(RNN_HEADS, HEAD_DIM, HEAD_DIM)


BIG = ("w_in", "w_branch_a", "w_branch_b", "w_out", "w_up", "w_down")
TRANSPOSED = ("w_in", "w_up")
SMALL = (
    "norm_mix_g", "conv_w", "conv_b", "lru_w_a", "lru_b_a", "lru_w_x", "lru_b_x", "lru_lambda",
    "sgu_ln_g", "sgu_ln_b", "sgu_w_s", "sgu_b_s", "norm_ffn_g", "final_norm_g",
)
WEIGHTS = (
    "norm_mix_g", "w_in", "conv_w", "conv_b", "lru_w_a", "lru_b_a", "lru_w_x", "lru_b_x", "lru_lambda", "sgu_ln_g",
    "sgu_ln_b", "sgu_w_s", "sgu_b_s", "w_branch_a", "w_branch_b", "w_out", "norm_ffn_g", "w_up", "w_down", "final_norm_g",
)

TM = 512
TM_NT = 1024
TN_IN = 1664
TN_UP = 2048
TKA = 1024
TKA_PIECES = 256
TC = 512
TC_BWD = 1024
TB = 256
TB_BWD = 512
TR = 256


_BRANCHES_0 = [(0, "w_branch_a"), (0, "w_branch_b"), (0, "w_out")]
_BRANCHES_1 = [(1, "w_branch_a"), (1, "w_branch_b"), (1, "w_out")]
GATHERS_RIDING = (
    {
        "in_proj": [("first", _BRANCHES_0), ("near", [(0, "w_up")])],
        "branch_a_fwd": [("pass", _BRANCHES_0), ("far", [(0, "w_up")]), ("near", [(1, "w_in")])],
        "sgu_fwd": [("pass", [(0, "w_up")]), ("near", [(0, "w_down")])],
        "merge_fwd": [("far", [(0, "w_down")])],
        "ffn_up": [("pass", [(0, "w_down")]), ("far", [(1, "w_in")])],
        "ffn_down": [("pass", [(1, "w_in")]), ("near", _BRANCHES_1)],
    },
    {
        "in_proj": [("far", _BRANCHES_1), ("near", [(1, "w_down")])],
        "branch_a_fwd": [("pass", _BRANCHES_1), ("far", [(1, "w_down")]), ("first", [(1, "w_up")])],
        "sgu_fwd": [("pass", [(1, "w_down"), (1, "w_up")])],
    },
)


def _layer_forward(l, x, p, w, shards, arriving, loss_head=None):
    def run(key, fn, *args, **kw):
        riding = GATHERS_RIDING[l].get(key, ())
        if not riding:
            return fn(*args, **kw)
        comms = []
        for stage, units in riding:
            mine = [shards[l2][n2] for l2, n2 in units] if stage != "pass" else None
            left = [arriving.pop(unit) for unit in units] if stage in ("far", "pass") else None
            comms.append(_gather_stage(stage, shards=mine, arrived=left))
        out, got = fn(*args, comm=_merge_comms(comms), **kw)
        got = list(got)
        for stage, units in riding:
            for l2, n2 in units:
                if stage == "pass":
                    w[l2][n2] = got.pop(0).reshape(-1, D)
                else:
                    arriving[l2, n2] = got.pop(0)
        return out

    proj, h = run("in_proj", _norm_matmul_nt, x, p["norm_mix_g"], w[l]["w_in"], tm=TM_NT, tn=TN_IN, name=f"in_proj_{l}")
    hseq, ya_pre = run(
        "branch_a_fwd", _branch_a_fwd, proj, p["conv_w"], p["conv_b"], p["wa2"], p["lru_b_a"], p["wx2"], p["lru_b_x"],
        p["lru_lambda"], tc=TC, name=f"branch_a_fwd_{l}",
    )
    yb_pre = run("sgu_fwd", _sgu_fwd, proj, p["sgu_ln_g"], p["sgu_ln_b"], p["wm"], p["sgu_bias"], tb=TB, name=f"sgu_fwd_{l}")
    x1, ya, yb = run(
        "merge_fwd", _merge_fwd, ya_pre, yb_pre, proj, x, w[l]["w_branch_a"], w[l]["w_branch_b"], w[l]["w_out"], tm=TM,
        name=f"merge_fwd_{l}",
    )
    f_pre, h2 = run("ffn_up", _norm_matmul_nt, x1, p["norm_ffn_g"], w[l]["w_up"], tm=TM_NT, tn=TN_UP, name=f"ffn_up_{l}")
    saved = dict(x=x, h=h, proj=proj, hseq=hseq, ya_pre=ya_pre, yb_pre=yb_pre, ya=ya, yb=yb, x1=x1, h2=h2, f_pre=f_pre)
    if loss_head is None:
        return run("ffn_down", _matmul_nn_res, f_pre, w[l]["w_down"], x1, relu2=True, tm=TM, name=f"ffn_down_{l}"), saved
    return _ffn_down_loss(f_pre, w[l]["w_down"], x1, *loss_head, tm=TM, name=f"ffn_down_loss_{l}"), saved


def _layer_backward(l, dx2, dx2b, sv, p, w, core, waiting, last):
    parts, from_chips = {}, {}

    def by_device(g):
        return g.reshape(4, 2, -1, D)

    def with_sibling(name, g, got):
        parts[name] = _sum_with_sibling(by_device(g), got, core, name=f"sum_sibling_{name}_{l}")

    df_pre = _matmul_nt_drelu2(dx2b, w["w_down"], sv["f_pre"], tm=TM_NT, tn=TN_UP, name=f"ffn_down_bwd_{l}")
    g_down = _matmul_tn([sv["f_pre"]], dx2b, relu2=True, tka=TKA, name=f"grad_w_down_{l}")
    g_up, (got,) = _matmul_tn(
        [df_pre], sv["h2"], relu2=False, tka=TKA, name=f"grad_w_up_{l}", comm=_sibling_comm([by_device(g_down)])
    )
    with_sibling("w_down", g_down, got)
    (dx1, dx1b, g_norm_ffn), (got,) = _matmul_nn_rmsnorm_bwd(
        [df_pre], w["w_up"], sv["x1"], p["norm_ffn_g"], dx2, tm=TM, name=f"ffn_up_bwd_{l}",
        comm=_sibling_comm([by_device(g_up)]),
    )
    with_sibling("w_up", g_up, got)
    (merged, dya, dyb, dga, dgb, dya_pre, dyb_pre), (from_chips[l, "w_up"],) = _merge_bwd(
        dx1b, sv["ya"], sv["yb"], sv["proj"], w["w_branch_a"], w["w_branch_b"], w["w_out"], tm=TM, name=f"merge_bwd_{l}",
        comm=_chips_comm([parts["w_up"]]),
    )
    g_out, g_ba, g_bb = _matmuls_tn(
        [(merged, dx1b), (sv["ya_pre"], dya), (sv["yb_pre"], dyb)], ts=2 * TM, name=f"grad_w_branches_{l}"
    )
    branch = (("w_out", g_out), ("w_branch_a", g_ba), ("w_branch_b", g_bb))
    (du, dv, g_ws, g_bs, g_lng, g_lnb), got = _sgu_bwd(
        dyb_pre, sv["proj"], p["sgu_ln_g"], p["sgu_ln_b"], p["wm"], p["wmt"], p["sgu_bias"], p["mask"], tb=TB_BWD,
        name=f"sgu_bwd_{l}",
        comm=_merge_comms([_sibling_comm([by_device(g) for _, g in branch]), _chips_comm([parts["w_down"]])]),
    )
    from_chips[l, "w_down"] = got[-1]
    for (name, g), landed in zip(branch, got):
        with_sibling(name, g, landed)
    riding = [((l, name), parts[name]) for name, _ in branch] + list(waiting)
    (dxr, dgr, g_cw, g_cb, g_ba_, g_bx, g_lam, g_wa2, g_wx2), got = _branch_a_bwd(
        dya_pre, sv["proj"], sv["hseq"], p["conv_w"], p["conv_b"], p["wa2"], p["lru_b_a"], p["wx2"], p["lru_b_x"],
        p["lru_lambda"], p["wa2t"], p["wx2t"], tc=TC_BWD, name=f"branch_a_bwd_{l}", comm=_chips_comm([part for _, part in riding]),
    )
    for (key, _), landed in zip(riding, got):
        from_chips[key] = landed
    dproj = [dxr, dgr, du, dv, dga, dgb]
    g_in = _matmul_tn(dproj, sv["h"], relu2=False, tka=TKA_PIECES, name=f"grad_w_in_{l}")
    if last:
        (got,) = _comm_only(_sibling_comm([by_device(g_in)]), name=f"grad_w_in_to_sibling_{l}")
        with_sibling("w_in", g_in, got)
        riding = _chips_comm([parts["w_in"]])
    else:
        riding = _sibling_comm([by_device(g_in)])
    (dx, dxb, g_norm_mix), (got,) = _matmul_nn_rmsnorm_bwd(
        dproj, w["w_in"], sv["x"], p["norm_mix_g"], dx1, tm=TM, name=f"in_proj_bwd_{l}", comm=riding
    )
    if last:
        from_chips[l, "w_in"] = got
    else:
        with_sibling("w_in", g_in, got)
    small = dict(
        norm_mix_g=g_norm_mix[0], conv_w=g_cw, conv_b=g_cb[0], lru_w_a=_diag_blocks(g_wa2), lru_b_a=g_ba_.reshape(RNN_HEADS, HEAD_DIM),
        lru_w_x=_diag_blocks(g_wx2), lru_b_x=g_bx.reshape(RNN_HEADS, HEAD_DIM), lru_lambda=g_lam[0], sgu_ln_g=g_lng[0],
        sgu_ln_b=g_lnb[0], sgu_w_s=g_ws, sgu_b_s=g_bs[:, :, 0], norm_ffn_g=g_norm_ffn[0],
    )
    return dx, dxb, small, parts, from_chips


def _prepare_small(l, given):
    chunk_id = jnp.arange(SGU_BLOCK) // CHUNK
    mask = (chunk_id[:, None] >= chunk_id[None, :]).astype(F32)
    wm = given["sgu_w_s"][l] * mask
    wa2 = _block_diag_pairs(given["lru_w_a"][l])
    wx2 = _block_diag_pairs(given["lru_w_x"][l])
    row = lambda a: a.reshape(1, -1)
    return dict(
        norm_mix_g=row(given["norm_mix_g"][l]),
        norm_ffn_g=row(given["norm_ffn_g"][l]),
        conv_w=given["conv_w_full"][l],
        conv_b=row(given["conv_b"][l]),
        wa2=wa2.astype(BF16),
        wx2=wx2.astype(BF16),
        wa2t=jnp.swapaxes(wa2, 1, 2).astype(BF16),
        wx2t=jnp.swapaxes(wx2, 1, 2).astype(BF16),
        lru_b_a=row(given["lru_b_a"][l]),
        lru_b_x=row(given["lru_b_x"][l]),
        lru_lambda=row(given["lru_lambda"][l]),
        sgu_ln_g=row(given["sgu_ln_g"][l]),
        sgu_ln_b=row(given["sgu_ln_b"][l]),
        wm=wm.astype(BF16),
        wmt=jnp.swapaxes(wm, 1, 2).astype(BF16),
        sgu_bias=jnp.broadcast_to(given["sgu_b_s"][l][:, :, None], (SGU_GROUPS, SGU_BLOCK, LANES)),
        mask=mask,
    )


def _step(given):
    x_idx, y_idx, c_idx = _position()
    dev = 4 * x_idx + 2 * y_idx + c_idx
    core = c_idx.astype(jnp.int32).reshape(1)
    chip = (2 * x_idx + y_idx).astype(jnp.int32).reshape(1)

    def rows_first(name, a):
        return jnp.swapaxes(a, 1, 2) if name in TRANSPOSED else a

    shards = []
    for l in range(DEPTH):
        shards.append({name: rows_first(name, given[name])[l].astype(BF16)[None] for name in BIG})
    conv_mine = given["conv_w"].reshape(1, DEPTH * CONV_WIDTH, D_RNN // N_DEV)
    w_in_first, conv_all = _comm_only(_gather_comm([shards[0]["w_in"], conv_mine]), name="gather_first")
    weights = [{"w_in": w_in_first.reshape(-1, D)}, {}]
    conv_all = conv_all.reshape(N_DEV, DEPTH, CONV_WIDTH, D_RNN // N_DEV)
    given = dict(given, conv_w_full=jnp.moveaxis(conv_all, 0, 2).reshape(DEPTH, CONV_WIDTH, D_RNN))

    small_params = [_prepare_small(l, given) for l in range(DEPTH)]
    x = given["x"][0]
    saved, arriving = [], {}
    loss_head = (given["final_norm_g"].reshape(1, D), given["loss_target"][0])
    for l in range(DEPTH):
        x, sv = _layer_forward(
            l, x, small_params[l], weights, shards, arriving, loss_head=loss_head if l == DEPTH - 1 else None
        )
        saved.append(sv)
    dx, dxb, g_final, loss = x
    small_grads, parts, from_chips, waiting = [None] * DEPTH, [None] * DEPTH, {}, []
    for l in reversed(range(DEPTH)):
        dx, dxb, small_grads[l], parts[l], got = _layer_backward(
            l, dx, dxb, saved[l], small_params[l], weights[l], core, waiting, last=l == 0
        )
        from_chips.update(got)
        waiting = [((l, "w_in"), parts[l]["w_in"])]

    small_list = []
    for name in SMALL[:-1]:
        small_list.append(jnp.stack([small_grads[l][name] for l in range(DEPTH)]))
    small_list += [g_final[0], loss[0, :1]]
    small_shapes = [a.shape for a in small_list]
    pack = _pack(small_list, SMALL_ROWS).reshape(N_DEV, SMALL_ROWS_PER_DEV, D)
    summed = _unpack(_all_reduce_small(pack, name="all_reduce_small"), small_shapes)
    loss_total = summed[-1][0]
    grads = dict(zip(SMALL, summed[:-1]))
    cw = grads["conv_w"].reshape(DEPTH, CONV_WIDTH, N_DEV, D_RNN // N_DEV)
    grads["conv_w"] = lax.dynamic_index_in_dim(cw, dev, axis=2, keepdims=False)

    delta, new_m, new_v = {}, {}, {}
    for name in BIG:
        w, m, v = given[name], given["m_" + name], given["v_" + name]
        mine = [parts[l][name] for l in range(DEPTH)]
        theirs = [from_chips[l, name] for l in range(DEPTH)]
        if name == "w_up":
            sums = [_sum_chips(mine[l], theirs[l], chip, name=f"sum_chips_{name}_{l}").T for l in range(DEPTH)]
            out = _adamw_layers(w, sums, m, v, tr=TR, name=f"adamw_{name}")
        else:
            out = _adamw_reduced(
                rows_first(name, w), mine, theirs, rows_first(name, m), rows_first(name, v), chip, tr=TR, name=f"adamw_{name}"
            )
            out = [rows_first(name, a) for a in out]
        grads[name], delta[name], new_m[name], new_v[name] = out
    two_d = lambda a: a.reshape(1, -1) if a.ndim == 1 else a
    groups = [tuple(two_d(a) for a in (given[n], grads[n], given["m_" + n], given["v_" + n])) for n in SMALL]
    for n, (d, m2, v2) in zip(SMALL, _adamw_small(groups, name="adamw_small")):
        shape = given[n].shape
        delta[n], new_m[n], new_v[n] = d.reshape(shape), m2.reshape(shape), v2.reshape(shape)

    return (
        loss_total, dx[None],
        *[grads[n] for n in WEIGHTS], *[delta[n] for n in WEIGHTS], *[new_m[n] for n in WEIGHTS], *[new_v[n] for n in WEIGHTS],
    )


def kernel(x, norm_mix_g, w_in, conv_w, conv_b, lru_w_a, lru_b_a, lru_w_x, lru_b_x, lru_lambda, sgu_ln_g, sgu_ln_b, sgu_w_s, sgu_b_s, w_branch_a, w_branch_b, w_out, norm_ffn_g, w_up, w_down, final_norm_g, loss_target, m_norm_mix_g, m_w_in, m_conv_w, m_conv_b, m_lru_w_a, m_lru_b_a, m_lru_w_x, m_lru_b_x, m_lru_lambda, m_sgu_ln_g, m_sgu_ln_b, m_sgu_w_s, m_sgu_b_s, m_w_branch_a, m_w_branch_b, m_w_out, m_norm_ffn_g, m_w_up, m_w_down, m_final_norm_g, v_norm_mix_g, v_w_in, v_conv_w, v_conv_b, v_lru_w_a, v_lru_b_a, v_lru_w_x, v_lru_b_x, v_lru_lambda, v_sgu_ln_g, v_sgu_ln_b, v_sgu_w_s, v_sgu_b_s, v_w_branch_a, v_w_branch_b, v_w_out, v_norm_ffn_g, v_w_up, v_w_down, v_final_norm_g):
    return _step(dict(locals()))
```

```python
import jax
import jax.numpy as jnp
from jax import lax
from jax.experimental import pallas as pl
from jax.experimental.pallas import tpu as pltpu

F32 = jnp.float32
BF16 = jnp.bfloat16
SDS = jax.ShapeDtypeStruct
MESH = pl.DeviceIdType.MESH

D = 1024
D_RNN = 1280
D_SGU = 1024
D_IN = 2 * D_RNN + 2 * D_SGU + 2 * D
DEPTH = 2
RNN_HEADS = 20
HEAD_DIM = 64
CONV_WIDTH = 4
LRU_C = 8.0
SGU_GROUPS = 8
SGU_BLOCK = 128
CHUNK = 64
EPS = 1e-6
N_DEV = 8

ADAM_LR = 0.001
ADAM_B1 = 0.9
ADAM_B2 = 0.999
ADAM_EPS = 1e-08
ADAM_WD = 0.01
ADAM_STEP = 10

LANES = 128
SUBLANES = 8
VMEM_LIMIT_BYTES = 56 * 1024 * 1024

N_RNN_TILES = D_RNN // LANES
RNN_TILES_PER_STEP = 5
U_BLK512 = (2 * D_RNN) // 512
V_BLK512 = (2 * D_RNN + D_SGU) // 512
GA_BLK512 = (2 * D_RNN + 2 * D_SGU) // 512
GB_BLK512 = (2 * D_RNN + 2 * D_SGU + D) // 512

SMALL_ROWS_PER_DEV = 80
SMALL_ROWS = N_DEV * SMALL_ROWS_PER_DEV


def _params(*sem):
    return pltpu.CompilerParams(dimension_semantics=sem, vmem_limit_bytes=VMEM_LIMIT_BYTES)


def _sigmoid(x):
    return 0.5 + 0.5 * jnp.tanh(0.5 * x)


_GELU_C = 0.7978845608028654
_GELU_K = 0.044715


def _gelu(x):
    t = jnp.tanh(_GELU_C * (x + _GELU_K * x * x * x))
    return 0.5 * x * (1.0 + t)


def _gelu_and_grad(x):
    t = jnp.tanh(_GELU_C * (x + _GELU_K * x * x * x))
    val = 0.5 * x * (1.0 + t)
    grad = 0.5 * (1.0 + t) + 0.5 * x * (1.0 - t * t) * _GELU_C * (1.0 + 3.0 * _GELU_K * x * x)
    return val, grad


def _one_minus_square(log_a, a):
    return -jnp.tanh(log_a) * (1.0 + a * a)


def _dot(a, b):
    return jnp.dot(a, b, preferred_element_type=F32)


def _dot_nt(a, b):
    return lax.dot_general(a, b, (((1,), (1,)), ((), ())), preferred_element_type=F32)


def _dot_tn(a, b):
    return lax.dot_general(a, b, (((0,), (0,)), ((), ())), preferred_element_type=F32)


def _norm_matmul_nt(x, g, w, *, tm, tn, name, comm=None):
    s, n = x.shape[0], w.shape[0]
    tm, tn = min(tm, s), min(tn, n)

    def body(x_ref, g_ref, w_ref, o_ref, h_ref):
        @pl.when(pl.program_id(1) == 0)
        def _():
            xv = x_ref[...]
            r = lax.rsqrt(jnp.mean(xv * xv, axis=-1, keepdims=True) + EPS)
            h_ref[...] = (xv * r * g_ref[...]).astype(BF16)

        o_ref[...] = _dot_nt(h_ref[...], w_ref[...]).astype(o_ref.dtype)

    return _call(
        body,
        (x, g, w),
        name=name,
        grid=(s // tm, n // tn),
        in_specs=[
            pl.BlockSpec((tm, D), lambda i, j: (i, 0)),
            pl.BlockSpec((1, D), lambda i, j: (0, 0)),
            pl.BlockSpec((tn, D), lambda i, j: (j, 0)),
        ],
        out_specs=[pl.BlockSpec((tm, tn), lambda i, j: (i, j)), pl.BlockSpec((tm, D), lambda i, j: (i, 0))],
        out_shape=[SDS((s, n), BF16), SDS((s, D), BF16)],
        semantics=("parallel", "arbitrary"),
        comm=comm,
    )


def _matmul_nn_res(a, w, res, *, relu2, tm, name, comm=None):
    s, k = a.shape
    tm = min(tm, s)

    def body(a_ref, w_ref, r_ref, o_ref):
        av = a_ref[...]
        if relu2:
            t = jnp.maximum(av.astype(F32), 0.0)
            av = (t * t).astype(BF16)
        o_ref[...] = r_ref[...] + _dot(av, w_ref[...])

    return _call(
        body,
        (a, w, res),
        name=name,
        grid=(s // tm,),
        in_specs=[
            pl.BlockSpec((tm, k), lambda i: (i, 0)),
            pl.BlockSpec((k, D), lambda i: (0, 0)),
            pl.BlockSpec((tm, D), lambda i: (i, 0)),
        ],
        out_specs=pl.BlockSpec((tm, D), lambda i: (i, 0)),
        out_shape=SDS((s, D), F32),
        semantics=("parallel",),
        comm=comm,
    )


def _matmul_nt_drelu2(a, w, pre, *, tm, tn, name):
    s, n = a.shape[0], w.shape[0]
    tm, tn = min(tm, s), min(tn, n)

    def body(a_ref, w_ref, p_ref, o_ref):
        d = _dot_nt(a_ref[...], w_ref[...])
        o_ref[...] = (d * (2.0 * jnp.maximum(p_ref[...].astype(F32), 0.0))).astype(o_ref.dtype)

    return pl.pallas_call(
        body,
        name=name,
        grid=(s // tm, n // tn),
        in_specs=[
            pl.BlockSpec((tm, D), lambda i, j: (i, 0)),
            pl.BlockSpec((tn, D), lambda i, j: (j, 0)),
            pl.BlockSpec((tm, tn), lambda i, j: (i, j)),
        ],
        out_specs=pl.BlockSpec((tm, tn), lambda i, j: (i, j)),
        out_shape=SDS((s, n), BF16),
        compiler_params=_params("parallel", "arbitrary"),
    )(a, w, pre)


def _matmul_tn(a_list, b, *, relu2, tka, name, comm=None):
    s = b.shape[0]
    n = len(a_list)
    nblk = [a.shape[1] // tka for a in a_list]
    starts = [sum(nblk[:p]) for p in range(n)]

    def body(*refs):
        a_refs, b_ref, o_ref = refs[:n], refs[n], refs[n + 1]
        i = pl.program_id(0)
        for p in range(n):

            @pl.when((i >= starts[p]) & (i < starts[p] + nblk[p]))
            def _(p=p):
                av = a_refs[p][...]
                if relu2:
                    t = jnp.maximum(av.astype(F32), 0.0)
                    av = (t * t).astype(BF16)
                o_ref[...] = _dot_tn(av, b_ref[...]).astype(o_ref.dtype)

    def piece_spec(p):
        return pl.BlockSpec((s, tka), lambda i: (0, jnp.clip(i - starts[p], 0, nblk[p] - 1)))

    return _call(
        body,
        (*a_list, b),
        name=name,
        grid=(sum(nblk),),
        in_specs=[piece_spec(p) for p in range(n)] + [pl.BlockSpec((s, D), lambda i: (0, 0))],
        out_specs=pl.BlockSpec((tka, D), lambda i: (i, 0)),
        out_shape=SDS((sum(nblk) * tka, D), BF16),
        semantics=("parallel",),
        comm=comm,
    )


def _matmuls_tn(pairs, *, ts, name):
    s = pairs[0][0].shape[0]
    ts = min(ts, s)
    n = len(pairs)
    steps = s // ts

    def body(*refs):
        ins, outs, accs = refs[: 2 * n], refs[2 * n : 3 * n], refs[3 * n :]
        for p in range(n):
            part = _dot_tn(ins[2 * p][...], ins[2 * p + 1][...])

            @pl.when(pl.program_id(0) == 0)
            def _(p=p, part=part):
                accs[p][...] = part

            @pl.when(pl.program_id(0) > 0)
            def _(p=p, part=part):
                accs[p][...] += part

        @pl.when(pl.program_id(0) == steps - 1)
        def _():
            for p in range(n):
                outs[p][...] = accs[p][...].astype(BF16)

    widths = [a.shape[1] for a, _ in pairs]
    in_specs = []
    for wd in widths:
        in_specs += [pl.BlockSpec((ts, wd), lambda i: (i, 0)), pl.BlockSpec((ts, D), lambda i: (i, 0))]
    return pl.pallas_call(
        body,
        name=name,
        grid=(steps,),
        in_specs=in_specs,
        out_specs=[pl.BlockSpec((wd, D), lambda i: (0, 0)) for wd in widths],
        out_shape=[SDS((wd, D), BF16) for wd in widths],
        scratch_shapes=[pltpu.VMEM((wd, D), F32) for wd in widths],
        compiler_params=_params("arbitrary"),
    )(*[x for pair in pairs for x in pair])


def _matmul_nn_rmsnorm_bwd(a_list, w, x, g, res, *, tm, name, comm=None):
    s = x.shape[0]
    tm = min(tm, s)
    n = len(a_list)
    widths = [a.shape[1] for a in a_list]
    offs = [sum(widths[:p]) for p in range(n)]
    k = sum(widths)

    def body(*refs):
        a_refs = refs[:n]
        w_ref, x_ref, g_ref, r_ref, dx_ref, dxb_ref, dg_ref = refs[n:]

        @pl.when(pl.program_id(0) == 0)
        def _():
            dg_ref[...] = jnp.zeros_like(dg_ref)

        dh = _dot(a_refs[0][...], w_ref[0 : widths[0], :])
        for p in range(1, n):
            dh += _dot(a_refs[p][...], w_ref[offs[p] : offs[p] + widths[p], :])
        xv = x_ref[...]
        r = lax.rsqrt(jnp.mean(xv * xv, axis=-1, keepdims=True) + EPS)
        xhat = xv * r
        dxh = dh * g_ref[...]
        dx = r_ref[...] + r * (dxh - xhat * jnp.mean(dxh * xhat, axis=-1, keepdims=True))
        dx_ref[...] = dx
        dxb_ref[...] = dx.astype(BF16)
        dg_ref[...] += jnp.sum(dh * xhat, axis=0, keepdims=True)

    act = pl.BlockSpec((tm, D), lambda i: (i, 0))
    vec = pl.BlockSpec((1, D), lambda i: (0, 0))
    return _call(
        body,
        (*a_list, w, x, g, res),
        name=name,
        grid=(s // tm,),
        in_specs=[pl.BlockSpec((tm, wd), lambda i: (i, 0)) for wd in widths]
        + [pl.BlockSpec((k, D), lambda i: (0, 0), pipeline_mode=pl.Buffered(1)), act, vec, act],
        out_specs=[act, act, vec],
        out_shape=[SDS((s, D), F32), SDS((s, D), BF16), SDS((1, D), F32)],
        semantics=("arbitrary",),
        comm=comm,
    )


def _rows_after(ext, k, n):
    return pltpu.roll(ext, n + SUBLANES - k, 0)[:n, :]


def _scan_forward(a, b, n):
    row = lax.broadcasted_iota(jnp.int32, a.shape, 0)
    d = 1
    while d < n:
        if d < SUBLANES:
            m = row >= d
            a_s = jnp.where(m, pltpu.roll(a, d, 0), 1.0)
            b_s = jnp.where(m, pltpu.roll(b, d, 0), 0.0)
            b = a * b_s + b
            a = a * a_s
        else:
            b = jnp.concatenate([b[:d], a[d:] * b[: n - d] + b[d:]], axis=0)
            a = jnp.concatenate([a[:d], a[d:] * a[: n - d]], axis=0)
        d *= 2
    return a, b


def _scan_backward(a, b, n):
    row = lax.broadcasted_iota(jnp.int32, a.shape, 0)
    d = 1
    while d < n:
        if d < SUBLANES:
            m = row < n - d
            a_s = jnp.where(m, pltpu.roll(a, n - d, 0), 1.0)
            b_s = jnp.where(m, pltpu.roll(b, n - d, 0), 0.0)
            b = a * b_s + b
            a = a * a_s
        else:
            b = jnp.concatenate([a[: n - d] * b[d:] + b[: n - d], b[n - d :]], axis=0)
            a = jnp.concatenate([a[: n - d] * a[d:], a[n - d :]], axis=0)
        d *= 2
    return b


def _repeat_matrix(n):
    groups = n // SUBLANES
    return (jnp.arange(n)[:, None] // SUBLANES == jnp.arange(3 * groups)[None, :] % groups).astype(BF16)


def _scan_rows(a, b, n, repeat_ref, a_scr, b_scr, reverse):
    groups = n // SUBLANES
    a3 = a.reshape(groups, SUBLANES, LANES)
    b3 = b.reshape(groups, SUBLANES, LANES)
    sub = lax.broadcasted_iota(jnp.int32, a3.shape, 1)
    for d in (1, 2, 4):
        m = (sub < SUBLANES - d) if reverse else (sub >= d)
        shift = SUBLANES - d if reverse else d
        a_s = jnp.where(m, pltpu.roll(a3, shift, 1), 1.0)
        b_s = jnp.where(m, pltpu.roll(b3, shift, 1), 0.0)
        b3 = a3 * b_s + b3
        a3 = a3 * a_s
    a_scr[...] = a3.reshape(n, LANES)
    b_scr[...] = b3.reshape(n, LANES)
    edge = 0 if reverse else SUBLANES - 1
    a_tot = a_scr[pl.ds(edge, groups, stride=SUBLANES), :]
    b_tot = b_scr[pl.ds(edge, groups, stride=SUBLANES), :]
    row = lax.broadcasted_iota(jnp.int32, a_tot.shape, 0)
    if reverse:
        through = _scan_backward(a_tot, b_tot, groups)
        entering = jnp.where(row < groups - 1, pltpu.roll(through, groups - 1, 0), 0.0)
    else:
        _, through = _scan_forward(a_tot, b_tot, groups)
        entering = jnp.where(row >= 1, pltpu.roll(through, 1, 0), 0.0)
    hi = entering.astype(BF16)
    rest = entering - hi.astype(F32)
    mid = rest.astype(BF16)
    lo = (rest - mid.astype(F32)).astype(BF16)
    repeated = _dot(repeat_ref[...], jnp.concatenate([hi, mid, lo], axis=0))
    return b_scr[...] + a_scr[...] * repeated


def _softplus_neg(lam):
    z = -lam
    return jnp.maximum(z, 0.0) + jnp.log1p(jnp.exp(-jnp.abs(z)))


def _conv_and_gates(xc, xprev, cw_ref, cb_ref, wa_ref, ba_ref, wx_ref, bx_ref, lam_ref, ext_scr):
    n = xc.shape[0]
    ext_scr[:SUBLANES, :] = xprev
    ext_scr[SUBLANES:, :] = xc
    x1, x2, x3 = (ext_scr[pl.ds(SUBLANES - k, n), :] for k in (1, 2, 3))
    xr = cb_ref[...] + x3 * cw_ref[0:1, :] + x2 * cw_ref[1:2, :] + x1 * cw_ref[2:3, :] + xc * cw_ref[3:4, :]
    xrb = xr.astype(BF16)
    r = _sigmoid(_dot(xrb, wa_ref[...]) + ba_ref[...])
    i = _sigmoid(_dot(xrb, wx_ref[...]) + bx_ref[...])
    sp = _softplus_neg(lam_ref[...])
    log_a = (-LRU_C * r) * sp
    a = jnp.exp(log_a)
    return xr, (x1, x2, x3), r, i, a, _one_minus_square(log_a, a)


def _branch_a_fwd(proj, cw, cb, wa2, ba, wx2, bx, lam, *, tc, name, comm=None):
    s = proj.shape[0]
    tc = min(tc, s)

    def body(x_ref, g_ref, cw_ref, cb_ref, wa_ref, ba_ref, wx_ref, bx_ref, lam_ref, rep_ref, h_ref, y_ref,
             xprev, hlast, a_scr, b_scr, ext_scr):
        @pl.when(pl.program_id(1) == 0)
        def _():
            xprev[...] = jnp.zeros_like(xprev)
            hlast[...] = jnp.zeros_like(hlast)

        for t in range(RNN_TILES_PER_STEP):
            cols = lambda ref: ref.at[:, pl.ds(t * LANES, LANES)]
            one_tile(
                cols(x_ref), cols(g_ref), cols(cw_ref), cols(cb_ref), wa_ref.at[t], cols(ba_ref), wx_ref.at[t], cols(bx_ref),
                cols(lam_ref), rep_ref, cols(h_ref), cols(y_ref), cols(xprev), cols(hlast), a_scr.at[t], b_scr.at[t],
                ext_scr.at[t],
            )

    def one_tile(x_ref, g_ref, cw_ref, cb_ref, wa_ref, ba_ref, wx_ref, bx_ref, lam_ref, rep_ref, h_ref, y_ref,
                 xprev, hlast, a_scr, b_scr, ext_scr):
        xc = x_ref[...].astype(F32)
        xr, _, r, i, a, om = _conv_and_gates(
            xc, xprev[...], cw_ref, cb_ref, wa_ref, ba_ref, wx_ref, bx_ref, lam_ref, ext_scr
        )
        xprev[...] = xc[tc - SUBLANES :, :]
        u = jnp.sqrt(om) * (i * xr)
        row8 = lax.broadcasted_iota(jnp.int32, (SUBLANES, LANES), 0)
        first = u[:SUBLANES] + jnp.where(row8 == 0, a[:SUBLANES] * hlast[SUBLANES - 1 : SUBLANES, :], 0.0)
        h = _scan_rows(a, jnp.concatenate([first, u[SUBLANES:]], axis=0), tc, rep_ref, a_scr, b_scr, reverse=False)
        hlast[...] = h[tc - SUBLANES :, :]
        h_ref[...] = h
        y_ref[...] = (h * _gelu(g_ref[...].astype(F32))).astype(BF16)

    wide = RNN_TILES_PER_STEP * LANES
    tile = lambda j, c: (0, j)
    vec = pl.BlockSpec((1, wide), tile)
    mats = pl.BlockSpec((RNN_TILES_PER_STEP, LANES, LANES), lambda j, c: (j, 0, 0))
    repeat = _repeat_matrix(tc)
    return _call(
        body,
        (proj, proj, cw, cb, wa2, ba, wx2, bx, lam, repeat),
        name=name,
        grid=(N_RNN_TILES // RNN_TILES_PER_STEP, s // tc),
        in_specs=[
            pl.BlockSpec((tc, wide), lambda j, c: (c, j)),
            pl.BlockSpec((tc, wide), lambda j, c: (c, D_RNN // wide + j)),
            pl.BlockSpec((CONV_WIDTH, wide), tile),
            vec,
            mats,
            vec,
            mats,
            vec,
            vec,
            pl.BlockSpec(repeat.shape, lambda j, c: (0, 0)),
        ],
        out_specs=[pl.BlockSpec((tc, wide), lambda j, c: (c, j)), pl.BlockSpec((tc, wide), lambda j, c: (c, j))],
        out_shape=[SDS((s, D_RNN), F32), SDS((s, D_RNN), BF16)],
        scratch_shapes=[pltpu.VMEM((SUBLANES, wide), F32)] * 2
        + [pltpu.VMEM((RNN_TILES_PER_STEP, tc, LANES), F32)] * 2
        + [pltpu.VMEM((RNN_TILES_PER_STEP, tc + SUBLANES, LANES), F32)],
        semantics=("parallel", "arbitrary"),
        comm=comm,
    )


def _branch_a_bwd(dy, proj, h, cw, cb, wa2, ba, wx2, bx, lam, wa2t, wx2t, *, tc, name, comm=None):
    s = proj.shape[0]
    tc = min(tc, s)
    nc = s // tc
    halo16 = tc // 16
    halo8 = tc // SUBLANES

    def body(dy_ref, x_ref, xh_ref, g_ref, h_ref, hh_ref, cw_ref, cb_ref, wa_ref, ba_ref, wx_ref, bx_ref, lam_ref,
             wat_ref, wxt_ref, rep_ref, dx_ref, dg_ref, dcw_ref, dcb_ref, dba_ref, dbx_ref, dlam_ref, dwa_ref, dwx_ref,
             carry, dxr_next, a_scr, b_scr, ext_scr):
        cc = pl.program_id(1)
        ct = nc - 1 - cc

        @pl.when(cc == 0)
        def _():
            carry[...] = jnp.zeros_like(carry)
            dxr_next[...] = jnp.zeros_like(dxr_next)
            for ref in (dcw_ref, dcb_ref, dba_ref, dbx_ref, dlam_ref, dwa_ref, dwx_ref):
                ref[...] = jnp.zeros_like(ref)

        for t in range(RNN_TILES_PER_STEP):
            cols = lambda ref: ref.at[:, pl.ds(t * LANES, LANES)]
            one_tile(
                ct, cols(dy_ref), cols(x_ref), cols(xh_ref), cols(g_ref), cols(h_ref), cols(hh_ref), cols(cw_ref), cols(cb_ref),
                wa_ref.at[t], cols(ba_ref), wx_ref.at[t], cols(bx_ref), cols(lam_ref), wat_ref.at[t], wxt_ref.at[t], rep_ref,
                cols(dx_ref), cols(dg_ref), cols(dcw_ref), cols(dcb_ref), cols(dba_ref), cols(dbx_ref), cols(dlam_ref),
                dwa_ref.at[t], dwx_ref.at[t], cols(carry), cols(dxr_next), a_scr.at[t], b_scr.at[t], ext_scr.at[t],
            )

    def one_tile(ct, dy_ref, x_ref, xh_ref, g_ref, h_ref, hh_ref, cw_ref, cb_ref, wa_ref, ba_ref, wx_ref, bx_ref, lam_ref,
                 wat_ref, wxt_ref, rep_ref, dx_ref, dg_ref, dcw_ref, dcb_ref, dba_ref, dbx_ref, dlam_ref, dwa_ref, dwx_ref,
                 carry, dxr_next, a_scr, b_scr, ext_scr):
        xc = x_ref[...].astype(F32)
        xprev = jnp.where(ct > 0, xh_ref[SUBLANES:, :].astype(F32), 0.0)
        xr, (x1, x2, x3), r, i, a, om = _conv_and_gates(
            xc, xprev, cw_ref, cb_ref, wa_ref, ba_ref, wx_ref, bx_ref, lam_ref, ext_scr
        )
        inv_norm = lax.rsqrt(om)
        norm = om * inv_norm
        row = lax.broadcasted_iota(jnp.int32, xc.shape, 0)

        hv = h_ref[...]
        ge, ge_grad = _gelu_and_grad(g_ref[...].astype(F32))
        dyv = dy_ref[...].astype(F32)
        dg_ref[...] = (dyv * hv * ge_grad).astype(dg_ref.dtype)
        dh = dyv * ge

        b = dh + jnp.where(row == tc - 1, carry[0:1, :], 0.0)
        a_next = jnp.where(row < tc - 1, pltpu.roll(a, tc - 1, 0), 0.0)
        gadj = _scan_rows(a_next, b, tc, rep_ref, a_scr, b_scr, reverse=True)
        carry[...] = (a * gadj)[:SUBLANES, :]

        hprev_first = jnp.where(ct > 0, hh_ref[SUBLANES - 1 : SUBLANES, :], 0.0)
        hprev = jnp.where(row >= 1, pltpu.roll(hv, 1, 0), hprev_first)
        da = gadj * hprev
        ix = i * xr
        dnorm = gadj * ix
        di = gadj * norm * xr
        dlog_a = da * a - dnorm * (1.0 - om) * inv_norm
        sp = _softplus_neg(lam_ref[...])
        dr = dlog_a * (-LRU_C * sp)
        dsp = jnp.sum(dlog_a * (-LRU_C * r), axis=0, keepdims=True)
        dlam_ref[...] += dsp * (-_sigmoid(-lam_ref[...]))
        dza = dr * r * (1.0 - r)
        dzx = di * i * (1.0 - i)
        dzab, dzxb = dza.astype(BF16), dzx.astype(BF16)
        dxr = gadj * norm * i + _dot(dzab, wat_ref[...]) + _dot(dzxb, wxt_ref[...])
        xrb = xr.astype(BF16)
        dwa_ref[...] += _dot_tn(xrb, dzab)
        dwx_ref[...] += _dot_tn(xrb, dzxb)
        dba_ref[...] += jnp.sum(dza, axis=0, keepdims=True)
        dbx_ref[...] += jnp.sum(dzx, axis=0, keepdims=True)

        ext = jnp.concatenate([dxr, dxr_next[...]], axis=0)
        dx = (
            dxr * cw_ref[3:4, :]
            + _rows_after(ext, 1, tc) * cw_ref[2:3, :]
            + _rows_after(ext, 2, tc) * cw_ref[1:2, :]
            + _rows_after(ext, 3, tc) * cw_ref[0:1, :]
        )
        dxr_next[...] = dxr[:SUBLANES, :]
        dx_ref[...] = dx.astype(dx_ref.dtype)
        dcb_ref[...] += jnp.sum(dxr, axis=0, keepdims=True)
        dcw_ref[3:4, :] += jnp.sum(dxr * xc, axis=0, keepdims=True)
        dcw_ref[2:3, :] += jnp.sum(dxr * x1, axis=0, keepdims=True)
        dcw_ref[1:2, :] += jnp.sum(dxr * x2, axis=0, keepdims=True)
        dcw_ref[0:1, :] += jnp.sum(dxr * x3, axis=0, keepdims=True)

    wide = RNN_TILES_PER_STEP * LANES
    tile = lambda j, c: (0, j)
    mat = lambda j, c: (j, 0, 0)
    cur = lambda j, c: (nc - 1 - c, j)
    vec = pl.BlockSpec((1, wide), tile)
    matspec = pl.BlockSpec((RNN_TILES_PER_STEP, LANES, LANES), mat)
    repeat = _repeat_matrix(tc)
    return _call(
        body,
        (dy, proj, proj, proj, h, h, cw, cb, wa2, ba, wx2, bx, lam, wa2t, wx2t, repeat),
        name=name,
        grid=(N_RNN_TILES // RNN_TILES_PER_STEP, nc),
        in_specs=[
            pl.BlockSpec((tc, wide), cur),
            pl.BlockSpec((tc, wide), cur),
            pl.BlockSpec((16, wide), lambda j, c: (jnp.maximum((nc - 1 - c) * halo16 - 1, 0), j)),
            pl.BlockSpec((tc, wide), lambda j, c: (nc - 1 - c, D_RNN // wide + j)),
            pl.BlockSpec((tc, wide), cur),
            pl.BlockSpec((SUBLANES, wide), lambda j, c: (jnp.maximum((nc - 1 - c) * halo8 - 1, 0), j)),
            pl.BlockSpec((CONV_WIDTH, wide), tile),
            vec,
            matspec,
            vec,
            matspec,
            vec,
            vec,
            matspec,
            matspec,
            pl.BlockSpec(repeat.shape, lambda j, c: (0, 0)),
        ],
        out_specs=[
            pl.BlockSpec((tc, wide), cur),
            pl.BlockSpec((tc, wide), cur),
            pl.BlockSpec((CONV_WIDTH, wide), tile),
            vec,
            vec,
            vec,
            vec,
            matspec,
            matspec,
        ],
        out_shape=[
            SDS((s, D_RNN), BF16),
            SDS((s, D_RNN), BF16),
            SDS((CONV_WIDTH, D_RNN), F32),
            SDS((1, D_RNN), F32),
            SDS((1, D_RNN), F32),
            SDS((1, D_RNN), F32),
            SDS((1, D_RNN), F32),
            SDS((N_RNN_TILES, LANES, LANES), F32),
            SDS((N_RNN_TILES, LANES, LANES), F32),
        ],
        scratch_shapes=[pltpu.VMEM((SUBLANES, wide), F32)] * 2
        + [pltpu.VMEM((RNN_TILES_PER_STEP, tc, LANES), F32)] * 2
        + [pltpu.VMEM((RNN_TILES_PER_STEP, tc + SUBLANES, LANES), F32)],
        semantics=("parallel", "arbitrary"),
        comm=comm,
    )


def _sgu_specs(tb):
    half = lambda blk: pl.BlockSpec((tb, 512), lambda n: (n, blk))
    return [half(U_BLK512), half(U_BLK512 + 1), half(V_BLK512), half(V_BLK512 + 1)]


def _sgu_normed(v, lng_ref, lnb_ref):
    gv, gv_grad = _gelu_and_grad(v)
    mu = jnp.mean(gv, axis=-1, keepdims=True)
    xc = gv - mu
    rs = lax.rsqrt(jnp.mean(xc * xc, axis=-1, keepdims=True) + EPS)
    xhat = xc * rs
    return xhat * lng_ref[...] + lnb_ref[...], xhat, rs, gv_grad


def _sgu_fwd(proj, lng, lnb, wm, bias, *, tb, name, comm=None):
    s = proj.shape[0]
    tb = min(tb, s)

    def body(u0_ref, u1_ref, v0_ref, v1_ref, lng_ref, lnb_ref, wm_ref, bias_ref, y_ref):
        u = jnp.concatenate([u0_ref[...], u1_ref[...]], axis=1).astype(F32)
        v = jnp.concatenate([v0_ref[...], v1_ref[...]], axis=1).astype(F32)
        gu = _gelu(u)
        vn, _, _, _ = _sgu_normed(v, lng_ref, lnb_ref)
        vnb = vn.astype(BF16)
        for blk in range(tb // SGU_BLOCK):
            rows = slice(blk * SGU_BLOCK, (blk + 1) * SGU_BLOCK)
            for g in range(SGU_GROUPS):
                cols = slice(g * LANES, (g + 1) * LANES)
                mixed = _dot(wm_ref[g], vnb[rows, cols]) + bias_ref[g]
                y_ref[rows, cols] = (gu[rows, cols] * mixed).astype(BF16)

    const2 = lambda n: (0, 0)
    const3 = lambda n: (0, 0, 0)
    return _call(
        body,
        (proj, proj, proj, proj, lng, lnb, wm, bias),
        name=name,
        grid=(s // tb,),
        in_specs=_sgu_specs(tb)
        + [
            pl.BlockSpec((1, D_SGU), const2),
            pl.BlockSpec((1, D_SGU), const2),
            pl.BlockSpec((SGU_GROUPS, SGU_BLOCK, SGU_BLOCK), const3),
            pl.BlockSpec((SGU_GROUPS, SGU_BLOCK, LANES), const3),
        ],
        out_specs=pl.BlockSpec((tb, D_SGU), lambda n: (n, 0)),
        out_shape=SDS((s, D_SGU), BF16),
        semantics=("parallel",),
        comm=comm,
    )


def _sgu_bwd(dy, proj, lng, lnb, wm, wmt, bias, mask, *, tb, name, comm=None):
    s = proj.shape[0]
    tb = min(tb, s)
    nb = s // tb

    def body(dy_ref, u0_ref, u1_ref, v0_ref, v1_ref, lng_ref, lnb_ref, wm_ref, wmt_ref, bias_ref, mask_ref,
             du_ref, dv_ref, dws_ref, dbs_ref, dlng_ref, dlnb_ref, dvn_scr, dbs_acc):
        n = pl.program_id(0)

        @pl.when(n == 0)
        def _():
            dbs_acc[...] = jnp.zeros_like(dbs_acc)
            for ref in (dws_ref, dlng_ref, dlnb_ref):
                ref[...] = jnp.zeros_like(ref)

        u = jnp.concatenate([u0_ref[...], u1_ref[...]], axis=1).astype(F32)
        v = jnp.concatenate([v0_ref[...], v1_ref[...]], axis=1).astype(F32)
        gu, gu_grad = _gelu_and_grad(u)
        vn, xhat, rs, gv_grad = _sgu_normed(v, lng_ref, lnb_ref)
        vnb = vn.astype(BF16)
        dyv = dy_ref[...].astype(F32)
        for blk in range(tb // SGU_BLOCK):
            rows = slice(blk * SGU_BLOCK, (blk + 1) * SGU_BLOCK)
            for g in range(SGU_GROUPS):
                cols = slice(g * LANES, (g + 1) * LANES)
                vt = vnb[rows, cols]
                mixed = _dot(wm_ref[g], vt) + bias_ref[g]
                dyt = dyv[rows, cols]
                du_ref[rows, cols] = (dyt * mixed * gu_grad[rows, cols]).astype(BF16)
                dmix = dyt * gu[rows, cols]
                dmixb = dmix.astype(BF16)
                dvn_scr[rows, cols] = _dot(wmt_ref[g], dmixb)
                dws_ref[g] += _dot_nt(dmixb, vt) * mask_ref[...]
                dbs_acc[g] += dmix
        dvn = dvn_scr[...]
        dlng_ref[...] += jnp.sum(dvn * xhat, axis=0, keepdims=True)
        dlnb_ref[...] += jnp.sum(dvn, axis=0, keepdims=True)
        dxh = dvn * lng_ref[...]
        dgv = rs * (
            dxh - jnp.mean(dxh, axis=-1, keepdims=True) - xhat * jnp.mean(dxh * xhat, axis=-1, keepdims=True)
        )
        dv_ref[...] = (dgv * gv_grad).astype(BF16)

        @pl.when(n == nb - 1)
        def _():
            for g in range(SGU_GROUPS):
                dbs_ref[g] = jnp.broadcast_to(jnp.sum(dbs_acc[g], axis=-1, keepdims=True), (SGU_BLOCK, LANES))

    const2 = lambda n: (0, 0)
    const3 = lambda n: (0, 0, 0)
    gmat = pl.BlockSpec((SGU_GROUPS, SGU_BLOCK, SGU_BLOCK), const3)
    vec = pl.BlockSpec((1, D_SGU), const2)
    act = pl.BlockSpec((tb, D_SGU), lambda n: (n, 0))
    return _call(
        body,
        (dy, proj, proj, proj, proj, lng, lnb, wm, wmt, bias, mask),
        name=name,
        grid=(nb,),
        in_specs=[act] + _sgu_specs(tb) + [vec, vec, gmat, gmat, gmat, pl.BlockSpec((SGU_BLOCK, SGU_BLOCK), const2)],
        out_specs=[act, act, gmat, gmat, vec, vec],
        out_shape=[
            SDS((s, D_SGU), BF16),
            SDS((s, D_SGU), BF16),
            SDS((SGU_GROUPS, SGU_BLOCK, SGU_BLOCK), F32),
            SDS((SGU_GROUPS, SGU_BLOCK, LANES), F32),
            SDS((1, D_SGU), F32),
            SDS((1, D_SGU), F32),
        ],
        scratch_shapes=[pltpu.VMEM((tb, D_SGU), F32), pltpu.VMEM((SGU_GROUPS, SGU_BLOCK, LANES), F32)],
        semantics=("arbitrary",),
        comm=comm,
    )


def _gate_specs(tm):
    half = lambda blk: pl.BlockSpec((tm, 512), lambda i: (i, blk))
    return [half(GA_BLK512), half(GA_BLK512 + 1), half(GB_BLK512), half(GB_BLK512 + 1)]


def _merge_fwd(ya_pre, yb_pre, proj, x, w_ba, w_bb, w_out, *, tm, name, comm=None):
    s = x.shape[0]
    tm = min(tm, s)

    def body(ya_ref, yb_ref, a0, a1, b0, b1, x_ref, wa_ref, wb_ref, wo_ref, x1_ref, yao_ref, ybo_ref):
        ya = _dot(ya_ref[...], wa_ref[...])
        yb = _dot(yb_ref[...], wb_ref[...])
        sa = _sigmoid(jnp.concatenate([a0[...], a1[...]], axis=1).astype(F32))
        sb = _sigmoid(jnp.concatenate([b0[...], b1[...]], axis=1).astype(F32))
        merged = sa * ya + sb * yb
        x1_ref[...] = x_ref[...] + _dot(merged.astype(BF16), wo_ref[...])
        yao_ref[...] = ya.astype(BF16)
        ybo_ref[...] = yb.astype(BF16)

    whole = lambda r: pl.BlockSpec((r, D), lambda i: (0, 0))
    act = pl.BlockSpec((tm, D), lambda i: (i, 0))
    return _call(
        body,
        (ya_pre, yb_pre, proj, proj, proj, proj, x, w_ba, w_bb, w_out),
        name=name,
        grid=(s // tm,),
        in_specs=[pl.BlockSpec((tm, D_RNN), lambda i: (i, 0)), act] + _gate_specs(tm) + [act, whole(D_RNN), whole(D_SGU), whole(D)],
        out_specs=[act, act, act],
        out_shape=[SDS((s, D), F32), SDS((s, D), BF16), SDS((s, D), BF16)],
        semantics=("parallel",),
        comm=comm,
    )


def _merge_bwd(dx1, ya, yb, proj, w_ba, w_bb, w_out, *, tm, name, comm=None):
    s = dx1.shape[0]
    tm = min(tm, s)

    def body(dx_ref, ya_ref, yb_ref, a0, a1, b0, b1, wa_ref, wb_ref, wo_ref,
             mg_ref, dya_ref, dyb_ref, dga_ref, dgb_ref, dyap_ref, dybp_ref):
        dm = _dot_nt(dx_ref[...], wo_ref[...])
        ya = ya_ref[...].astype(F32)
        yb = yb_ref[...].astype(F32)
        sa = _sigmoid(jnp.concatenate([a0[...], a1[...]], axis=1).astype(F32))
        sb = _sigmoid(jnp.concatenate([b0[...], b1[...]], axis=1).astype(F32))
        mg_ref[...] = (sa * ya + sb * yb).astype(BF16)
        dya = (dm * sa).astype(BF16)
        dyb = (dm * sb).astype(BF16)
        dya_ref[...] = dya
        dyb_ref[...] = dyb
        dga_ref[...] = (dm * ya * sa * (1.0 - sa)).astype(BF16)
        dgb_ref[...] = (dm * yb * sb * (1.0 - sb)).astype(BF16)
        dyap_ref[...] = _dot_nt(dya, wa_ref[...]).astype(BF16)
        dybp_ref[...] = _dot_nt(dyb, wb_ref[...]).astype(BF16)

    whole = lambda r: pl.BlockSpec((r, D), lambda i: (0, 0))
    act = pl.BlockSpec((tm, D), lambda i: (i, 0))
    act_rnn = pl.BlockSpec((tm, D_RNN), lambda i: (i, 0))
    return _call(
        body,
        (dx1, ya, yb, proj, proj, proj, proj, w_ba, w_bb, w_out),
        name=name,
        grid=(s // tm,),
        in_specs=[act, act, act] + _gate_specs(tm) + [whole(D_RNN), whole(D_SGU), whole(D)],
        out_specs=[act, act, act, act, act, act_rnn, act],
        out_shape=[SDS((s, D), BF16)] * 5 + [SDS((s, D_RNN), BF16), SDS((s, D_SGU), BF16)],
        semantics=("parallel",),
        comm=comm,
    )


def _ffn_down_loss(a, w, res, g, target, *, tm, name):
    s, k = a.shape
    tm = min(tm, s)

    def body(a_ref, w_ref, r_ref, g_ref, t_ref, dx_ref, dxb_ref, dg_ref, loss_ref):
        @pl.when(pl.program_id(0) == 0)
        def _():
            dg_ref[...] = jnp.zeros_like(dg_ref)
            loss_ref[...] = jnp.zeros_like(loss_ref)

        t = jnp.maximum(a_ref[...].astype(F32), 0.0)
        xv = r_ref[...] + _dot((t * t).astype(BF16), w_ref[...])
        r = lax.rsqrt(jnp.mean(xv * xv, axis=-1, keepdims=True) + EPS)
        xhat = xv * r
        e = xhat * g_ref[...] - t_ref[...]
        loss_ref[...] += 0.5 * jnp.sum(jnp.mean(e * e, axis=-1, keepdims=True), axis=0, keepdims=True)
        dy = e * (1.0 / D)
        dxh = dy * g_ref[...]
        dx = r * (dxh - xhat * jnp.mean(dxh * xhat, axis=-1, keepdims=True))
        dx_ref[...] = dx
        dxb_ref[...] = dx.astype(BF16)
        dg_ref[...] += jnp.sum(dy * xhat, axis=0, keepdims=True)

    act = pl.BlockSpec((tm, D), lambda i: (i, 0))
    vec = pl.BlockSpec((1, D), lambda i: (0, 0))
    return pl.pallas_call(
        body,
        name=name,
        grid=(s // tm,),
        in_specs=[pl.BlockSpec((tm, k), lambda i: (i, 0)), pl.BlockSpec((k, D), lambda i: (0, 0)), act, vec, act],
        out_specs=[act, act, vec, pl.BlockSpec((SUBLANES, LANES), lambda i: (0, 0))],
        out_shape=[SDS((s, D), F32), SDS((s, D), BF16), SDS((1, D), F32), SDS((SUBLANES, LANES), F32)],
        compiler_params=_params("arbitrary"),
    )(a, w, res, g, target)


def _adamw_math(w, g, m, v):
    m2 = ADAM_B1 * m + (1.0 - ADAM_B1) * g
    v2 = ADAM_B2 * v + (1.0 - ADAM_B2) * (g * g)
    m_hat = m2 / (1.0 - ADAM_B1**ADAM_STEP)
    v_hat = v2 / (1.0 - ADAM_B2**ADAM_STEP)
    delta = -ADAM_LR * (m_hat / (jnp.sqrt(v_hat) + ADAM_EPS) + ADAM_WD * w)
    return delta, m2, v2


def _row_tile(rows, cap):
    return max(t for t in range(SUBLANES, min(cap, rows) + 1, SUBLANES) if rows % t == 0)


def _adamw_layers(w, grads, m, v, *, tr, name):
    depth, r, c = w.shape
    tr = _row_tile(r, tr)

    def body(*refs):
        g_refs = refs[:depth]
        w_ref, m_ref, v_ref, g_out, d_ref, mo_ref, vo_ref = refs[depth:]
        for l in range(depth):

            @pl.when(pl.program_id(0) == l)
            def _(l=l):
                g = g_refs[l][...]
                g_out[...] = g
                d_ref[...], mo_ref[...], vo_ref[...] = _adamw_math(w_ref[...], g, m_ref[...], v_ref[...])

    def of_layer(ll):
        return pl.BlockSpec((tr, c), lambda l, i: (jnp.where(l == ll, i, 0), 0))

    stacked = pl.BlockSpec((None, tr, c), lambda l, i: (l, i, 0))
    return pl.pallas_call(
        body,
        name=name,
        grid=(depth, r // tr),
        in_specs=[of_layer(ll) for ll in range(depth)] + [stacked] * 3,
        out_specs=[stacked] * 4,
        out_shape=[SDS((depth, r, c), F32)] * 4,
        compiler_params=_params("parallel", "parallel"),
    )(*grads, w, m, v)


def _adamw_reduced(w, parts, from_chips, m, v, chip, *, tr, name):
    depth, r, _ = w.shape
    tr = _row_tile(r, tr)

    def body(chip_ref, *refs):
        p_refs, c_refs = refs[:depth], refs[depth : 2 * depth]
        w_ref, m_ref, v_ref, g_out, d_ref, mo_ref, vo_ref = refs[2 * depth :]
        for l in range(depth):

            @pl.when(pl.program_id(0) == l)
            def _(l=l):
                got = c_refs[l]
                g = ((p_refs[l][...].astype(F32) + got[0].astype(F32)) + got[1].astype(F32)) + got[2].astype(F32)
                g_out[...] = g
                d_ref[...], mo_ref[...], vo_ref[...] = _adamw_math(w_ref[...], g, m_ref[...], v_ref[...])

    def mine_of_layer(ll):
        return pl.BlockSpec((None, tr, D), lambda l, i, chip_ref: (chip_ref[0], jnp.where(l == ll, i, 0), 0))

    def theirs_of_layer(ll):
        return pl.BlockSpec((3, tr, D), lambda l, i, chip_ref: (0, jnp.where(l == ll, i, 0), 0))

    stacked = pl.BlockSpec((None, tr, D), lambda l, i, chip_ref: (l, i, 0))
    return pl.pallas_call(
        body,
        name=name,
        grid_spec=pltpu.PrefetchScalarGridSpec(
            num_scalar_prefetch=1,
            grid=(depth, r // tr),
            in_specs=[mine_of_layer(ll) for ll in range(depth)]
            + [theirs_of_layer(ll) for ll in range(depth)]
            + [stacked] * 3,
            out_specs=[stacked] * 4,
        ),
        out_shape=[SDS((depth, r, D), F32)] * 4,
        compiler_params=_params("parallel", "parallel"),
    )(chip, *parts, *from_chips, w, m, v)


def _adamw_small(groups, *, name):
    n = len(groups)

    def body(*refs):
        ins, outs = refs[: 4 * n], refs[4 * n :]
        for i in range(n):
            w, g, m, v = (ref[...] for ref in ins[4 * i : 4 * i + 4])
            outs[3 * i][...], outs[3 * i + 1][...], outs[3 * i + 2][...] = _adamw_math(w, g, m, v)

    vmem = pl.BlockSpec(memory_space=pltpu.VMEM)
    outs = pl.pallas_call(
        body,
        name=name,
        in_specs=[vmem] * (4 * n),
        out_specs=[vmem] * (3 * n),
        out_shape=[SDS(grp[0].shape, F32) for grp in groups for _ in range(3)],
        compiler_params=pltpu.CompilerParams(vmem_limit_bytes=VMEM_LIMIT_BYTES),
    )(*[a for grp in groups for a in grp])
    return [tuple(outs[3 * i : 3 * i + 3]) for i in range(n)]


ANY = pl.BlockSpec(memory_space=pl.ANY)


def _position():
    return lax.axis_index("x"), lax.axis_index("y"), lax.axis_index("c")


def _other_chips(x, y):
    return [(1 - x, y), (x, 1 - y), (1 - x, 1 - y)]


class _Comm:
    def __init__(self, inputs, out_shapes, sem_counts, start, finish, aliases=()):
        self.inputs, self.out_shapes, self.sem_counts = list(inputs), list(out_shapes), list(sem_counts)
        self.start, self.finish = start, finish
        self.aliases = list(aliases)

    def sem_shapes(self):
        return [pltpu.SemaphoreType.DMA((n,)) for n in self.sem_counts]


def _merge_comms(comms):
    bounds, i, o, s = [], 0, 0, 0
    for cm in comms:
        bounds.append((i, i + len(cm.inputs), o, o + len(cm.out_shapes), s, s + len(cm.sem_counts)))
        i, o, s = bounds[-1][1], bounds[-1][3], bounds[-1][5]

    def phase(which):
        def run(ins, outs, sems):
            for cm, (i0, i1, o0, o1, s0, s1) in zip(comms, bounds):
                getattr(cm, which)(ins[i0:i1], outs[o0:o1], sems[s0:s1])

        return run

    return _Comm(
        [a for cm in comms for a in cm.inputs],
        [a for cm in comms for a in cm.out_shapes],
        [a for cm in comms for a in cm.sem_counts],
        phase("start"),
        phase("finish"),
        aliases=[(i0 + i, o0 + o) for cm, (i0, _, o0, _, _, _) in zip(comms, bounds) for i, o in cm.aliases],
    )


def _call(body, args, *, semantics, comm=None, **kw):
    if comm is None:
        return pl.pallas_call(body, compiler_params=_params(*semantics), **kw)(*args)
    grid, in_specs, out_specs, out_shape = kw["grid"], kw["in_specs"], kw["out_specs"], kw["out_shape"]
    scratch = list(kw.get("scratch_shapes", ()))
    single = not isinstance(out_shape, (list, tuple))
    core_specs = [out_specs] if single else list(out_specs)
    core_shapes = [out_shape] if single else list(out_shape)
    n_in, n_out, n_scr = len(in_specs), len(core_shapes), len(scratch)
    n_cin, n_cout = len(comm.inputs), len(comm.out_shapes)
    steps = 1
    for g in grid:
        steps *= g

    def hosted(*refs):
        core_in, c_in = refs[:n_in], refs[n_in : n_in + n_cin]
        o0 = n_in + n_cin
        core_out, c_out = refs[o0 : o0 + n_out], refs[o0 + n_out : o0 + n_out + n_cout]
        s0 = o0 + n_out + n_cout
        core_scr, sems = refs[s0 : s0 + n_scr], refs[s0 + n_scr :]
        step = pl.program_id(0)
        for d in range(1, len(grid)):
            step = step * grid[d] + pl.program_id(d)

        @pl.when(step == 0)
        def _():
            comm.start(c_in, c_out, sems)

        body(*core_in, *core_out, *core_scr)

        @pl.when(step == steps - 1)
        def _():
            comm.finish(c_in, c_out, sems)

    outs = pl.pallas_call(
        hosted,
        name=kw["name"],
        grid=grid,
        in_specs=list(in_specs) + [ANY] * n_cin,
        out_specs=core_specs + [ANY] * n_cout,
        out_shape=core_shapes + comm.out_shapes,
        scratch_shapes=scratch + comm.sem_shapes(),
        input_output_aliases={n_in + i: n_out + o for i, o in comm.aliases},
        compiler_params=_params(*(["arbitrary"] * len(grid))),
    )(*args, *comm.inputs)
    return (outs[0] if single else outs[:n_out]), outs[n_out:]


def _comm_only(comm, *, name):
    n_cin, n_cout = len(comm.inputs), len(comm.out_shapes)

    def body(*refs):
        ins, outs, sems = refs[:n_cin], refs[n_cin : n_cin + n_cout], refs[n_cin + n_cout :]
        comm.start(ins, outs, sems)
        comm.finish(ins, outs, sems)

    return pl.pallas_call(
        body,
        name=name,
        in_specs=[ANY] * n_cin,
        out_specs=[ANY] * n_cout,
        out_shape=comm.out_shapes,
        scratch_shapes=comm.sem_shapes(),
    )(*comm.inputs)


def _gather_comm(shards):
    n = len(shards)
    per = 7

    def plan(ins, outs, sems):
        send, recv, local = sems
        x, y, c = _position()
        me, sibling = (x, y, c), (x, y, 1 - c)
        chips = _other_chips(x, y)

        def block(t, px, py, pc):
            return outs[t].at[pl.ds(4 * px + 2 * py + pc, 1)]

        def copy(t, k, blk, to, src=None):
            return pltpu.make_async_remote_copy(
                src_ref=block(t, *blk) if src is None else src,
                dst_ref=block(t, *blk),
                send_sem=send.at[t * per + k],
                recv_sem=recv.at[t * per + k],
                device_id=to,
                device_id_type=MESH,
            )

        mine = [pltpu.make_async_copy(ins[t], block(t, *me), local.at[t]) for t in range(n)]
        to_chips = [copy(t, 1 + j, me, (*chip, c), src=ins[t]) for t in range(n) for j, chip in enumerate(chips)]
        to_sibling = [copy(t, 0, me, sibling, src=ins[t]) for t in range(n)]
        from_chips = [copy(t, 1 + j, (*chip, c), me) for t in range(n) for j, chip in enumerate(chips)]
        passed_on = [copy(t, 4 + j, (*chip, c), sibling) for t in range(n) for j, chip in enumerate(chips)]
        from_sibling = [copy(t, 0, sibling, me) for t in range(n)]
        from_sibling += [copy(t, 4 + j, (*chip, 1 - c), me) for t in range(n) for j, chip in enumerate(chips)]
        return mine, to_chips, to_sibling, from_chips, passed_on, from_sibling

    def start(ins, outs, sems):
        mine, to_chips, to_sibling, _, _, _ = plan(ins, outs, sems)
        for cp in mine + to_chips + to_sibling:
            cp.start()

    def finish(ins, outs, sems):
        mine, to_chips, to_sibling, from_chips, passed_on, from_sibling = plan(ins, outs, sems)
        for arrived, onward in zip(from_chips, passed_on):
            arrived.wait_recv()
            onward.start()
        for cp in from_sibling:
            cp.wait_recv()
        for cp in to_chips + to_sibling + passed_on:
            cp.wait_send()
        for cp in mine:
            cp.wait()

    out_shapes = [SDS((N_DEV,) + sh.shape[1:], sh.dtype) for sh in shards]
    return _Comm(shards, out_shapes, [n * per, n * per, n], start, finish)


def _gather_stage(stage, shards=None, arrived=None):
    n = len(arrived if shards is None else shards)
    targets = {"near": (0, 1), "far": (2,), "first": (0, 1, 2), "pass": (0, 1, 2)}[stage]
    to_sibling = stage in ("near", "first")
    per = len(targets) + to_sibling

    def plan(ins, outs, sems):
        x, y, c = _position()
        me, sibling = (x, y, c), (x, y, 1 - c)
        chips = [_other_chips(x, y)[j] for j in targets]

        def block(t, px, py, pc):
            return outs[t].at[pl.ds(4 * px + 2 * py + pc, 1)]

        def copy(t, k, blk, to, src=None):
            return pltpu.make_async_remote_copy(
                src_ref=block(t, *blk) if src is None else src,
                dst_ref=block(t, *blk),
                send_sem=sems[0].at[t * per + k],
                recv_sem=sems[1].at[t * per + k],
                device_id=to,
                device_id_type=MESH,
            )

        local = []
        if stage == "pass":
            sent = [copy(t, j, (*chip, c), sibling) for t in range(n) for j, chip in enumerate(chips)]
            landing = [copy(t, j, (*chip, 1 - c), me) for t in range(n) for j, chip in enumerate(chips)]
        else:
            sent = [copy(t, j, me, (*chip, c), src=ins[t]) for t in range(n) for j, chip in enumerate(chips)]
            landing = [copy(t, j, (*chip, c), me) for t in range(n) for j, chip in enumerate(chips)]
            if to_sibling:
                local = [pltpu.make_async_copy(ins[t], block(t, *me), sems[2].at[t]) for t in range(n)]
                sent += [copy(t, per - 1, me, sibling, src=ins[t]) for t in range(n)]
                landing += [copy(t, per - 1, sibling, me) for t in range(n)]
        return local, sent, landing

    def start(ins, outs, sems):
        local, sent, _ = plan(ins, outs, sems)
        for cp in local + sent:
            cp.start()

    def finish(ins, outs, sems):
        local, sent, landing = plan(ins, outs, sems)
        for cp in landing:
            cp.wait_recv()
        for cp in sent:
            cp.wait_send()
        for cp in local:
            cp.wait()

    if to_sibling:
        out_shapes = [SDS((N_DEV,) + sh.shape[1:], sh.dtype) for sh in shards]
        return _Comm(shards, out_shapes, [n * per, n * per, n], start, finish)
    out_shapes = [SDS(a.shape, a.dtype) for a in arrived]
    if stage == "pass":
        return _Comm(arrived, out_shapes, [n * per, n * per], start, finish, aliases=[(t, t) for t in range(n)])
    return _Comm(list(shards) + list(arrived), out_shapes, [n * per, n * per], start, finish, aliases=[(n + t, t) for t in range(n)])


def _exchange_comm(arrays, out_shapes, n_copies, copies_of):
    def start(ins, outs, sems):
        for cp in copies_of(ins, outs, *sems):
            cp.start()

    def finish(ins, outs, sems):
        for cp in copies_of(ins, outs, *sems):
            cp.wait()

    return _Comm(arrays, out_shapes, [n_copies, n_copies], start, finish)


def _sibling_comm(grads):
    def copies_of(ins, outs, send, recv):
        x, y, c = _position()
        return [
            pltpu.make_async_remote_copy(
                src_ref=ins[t].at[:, pl.ds(1 - c, 1)],
                dst_ref=outs[t],
                send_sem=send.at[t],
                recv_sem=recv.at[t],
                device_id=(x, y, 1 - c),
                device_id_type=MESH,
            )
            for t in range(len(ins))
        ]

    return _exchange_comm(grads, [SDS((4, 1) + g.shape[2:], g.dtype) for g in grads], len(grads), copies_of)


def _chips_comm(parts):
    def copies_of(ins, outs, send, recv):
        x, y, c = _position()
        return [
            pltpu.make_async_remote_copy(
                src_ref=ins[t].at[pl.ds(2 * px + py, 1)],
                dst_ref=outs[t].at[pl.ds(k, 1)],
                send_sem=send.at[3 * t + k],
                recv_sem=recv.at[3 * t + k],
                device_id=(px, py, c),
                device_id_type=MESH,
            )
            for t in range(len(ins))
            for k, (px, py) in enumerate(_other_chips(x, y))
        ]

    return _exchange_comm(parts, [SDS((3,) + p.shape[1:], p.dtype) for p in parts], 3 * len(parts), copies_of)


def _sum_with_sibling(grad, got, core, *, name):
    rows = grad.shape[2]

    def body(core_ref, a_ref, b_ref, o_ref):
        o_ref[...] = (a_ref[...].astype(F32) + b_ref[...].astype(F32)).astype(o_ref.dtype)

    return pl.pallas_call(
        body,
        name=name,
        grid_spec=pltpu.PrefetchScalarGridSpec(
            num_scalar_prefetch=1,
            grid=(4,),
            in_specs=[
                pl.BlockSpec((None, None, rows, D), lambda q, core_ref: (q, core_ref[0], 0, 0)),
                pl.BlockSpec((None, None, rows, D), lambda q, core_ref: (q, 0, 0, 0)),
            ],
            out_specs=pl.BlockSpec((None, rows, D), lambda q, core_ref: (q, 0, 0)),
        ),
        out_shape=SDS((4, rows, D), grad.dtype),
        compiler_params=_params("parallel"),
    )(core, grad, got)


def _sum_chips(part, got, chip, *, name):
    rows = part.shape[1]

    def body(chip_ref, a_ref, b_ref, o_ref):
        o_ref[...] = ((a_ref[...].astype(F32) + b_ref[0].astype(F32)) + b_ref[1].astype(F32)) + b_ref[2].astype(F32)

    return pl.pallas_call(
        body,
        name=name,
        grid_spec=pltpu.PrefetchScalarGridSpec(
            num_scalar_prefetch=1,
            grid=(1,),
            in_specs=[
                pl.BlockSpec((None, rows, D), lambda i, chip_ref: (chip_ref[0], 0, 0)),
                pl.BlockSpec((3, rows, D), lambda i, chip_ref: (0, 0, 0)),
            ],
            out_specs=pl.BlockSpec((rows, D), lambda i, chip_ref: (0, 0)),
        ),
        out_shape=SDS((rows, D), F32),
        compiler_params=_params("arbitrary"),
    )(chip, part, got)


def _all_reduce_small(pack, *, name):
    rows = pack.shape[1]

    def body(in_ref, out_ref, from_sibling, part, from_chips, send, recv):
        x, y, c = _position()
        me, sibling = (x, y, c), (x, y, 1 - c)
        chips = _other_chips(x, y)
        waiting = []

        def copy(k, src, dst, to):
            return pltpu.make_async_remote_copy(
                src_ref=src, dst_ref=dst, send_sem=send.at[k], recv_sem=recv.at[k], device_id=to, device_id_type=MESH
            )

        def exchange(copies):
            for cp in copies:
                cp.start()
            for cp in copies:
                cp.wait_recv()
            waiting.extend(copies)

        def block(px, py, pc):
            return out_ref.at[4 * px + 2 * py + pc]

        exchange([copy(q, in_ref.at[2 * q + 1 - c], from_sibling.at[q], sibling) for q in range(4)])
        for q in range(4):
            part[q] = in_ref[2 * q + c] + from_sibling[q]
        exchange([copy(4 + k, part.at[2 * px + py], from_chips.at[k], (px, py, c)) for k, (px, py) in enumerate(chips)])
        out_ref[4 * x + 2 * y + c] = ((part[2 * x + y] + from_chips[0]) + from_chips[1]) + from_chips[2]
        exchange(
            [copy(7, block(*me), block(*me), sibling)]
            + [copy(8 + k, block(*me), block(*me), (px, py, c)) for k, (px, py) in enumerate(chips)]
        )
        exchange([copy(11 + k, block(px, py, c), block(px, py, c), sibling) for k, (px, py) in enumerate(chips)])
        for cp in waiting:
            cp.wait_send()

    vmem = pl.BlockSpec(memory_space=pltpu.VMEM)
    return pl.pallas_call(
        body,
        name=name,
        in_specs=[vmem],
        out_specs=vmem,
        out_shape=SDS(pack.shape, F32),
        scratch_shapes=[
            pltpu.VMEM((4, rows, D), F32),
            pltpu.VMEM((4, rows, D), F32),
            pltpu.VMEM((3, rows, D), F32),
            pltpu.SemaphoreType.DMA((14,)),
            pltpu.SemaphoreType.DMA((14,)),
        ],
        compiler_params=pltpu.CompilerParams(vmem_limit_bytes=VMEM_LIMIT_BYTES),
    )(pack)


def _pack(arrays, rows):
    flat = jnp.concatenate([a.reshape(-1).astype(F32) for a in arrays])
    return jnp.pad(flat, (0, rows * D - flat.shape[0])).reshape(rows, D)


def _unpack(pack, shapes):
    flat = pack.reshape(-1)
    out, off = [], 0
    for sh in shapes:
        size = 1
        for dim in sh:
            size *= dim
        out.append(flat[off : off + size].reshape(sh))
        off += size
    return out


def _block_diag_pairs(w):
    w = w.reshape(N_RNN_TILES, 2, HEAD_DIM, HEAD_DIM)
    z = jnp.zeros_like(w[:, 0])
    top = jnp.concatenate([w[:, 0], z], axis=2)
    bot = jnp.concatenate([z, w[:, 1]], axis=2)
    return jnp.concatenate([top, bot], axis=1)


def _diag_blocks(w2):
    a = w2[:, :HEAD_DIM, :HEAD_DIM]
    b = w2[:, HEAD_DIM:, HEAD_DIM:]
    return jnp.stack([a, b], axis=1).reshape(RNN_HEADS, HEAD_DIM, HEAD_DIM)


BIG = ("w_in", "w_branch_a", "w_branch_b", "w_out", "w_up", "w_down")
TRANSPOSED = ("w_in", "w_up")
SMALL = (
    "norm_mix_g", "conv_w", "conv_b", "lru_w_a", "lru_b_a", "lru_w_x", "lru_b_x", "lru_lambda",
    "sgu_ln_g", "sgu_ln_b", "sgu_w_s", "sgu_b_s", "norm_ffn_g", "final_norm_g",
)
WEIGHTS = (
    "norm_mix_g", "w_in", "conv_w", "conv_b", "lru_w_a", "lru_b_a", "lru_w_x", "lru_b_x", "lru_lambda", "sgu_ln_g",
    "sgu_ln_b", "sgu_w_s", "sgu_b_s", "w_branch_a", "w_branch_b", "w_out", "norm_ffn_g", "w_up", "w_down", "final_norm_g",
)

TM = 512
TM_NT = 1024
TN_IN = 3328
TN_UP = 4096
TKA = 512
TKA_PIECES = 256
TC = 512
TC_BWD = 1024
TB = 256
TB_BWD = 512
TR = 256


_BRANCHES_0 = [(0, "w_branch_a"), (0, "w_branch_b"), (0, "w_out")]
_BRANCHES_1 = [(1, "w_branch_a"), (1, "w_branch_b"), (1, "w_out")]
GATHERS_RIDING = (
    {
        "in_proj": [("first", _BRANCHES_0), ("near", [(0, "w_up")])],
        "branch_a_fwd": [("pass", _BRANCHES_0), ("far", [(0, "w_up")]), ("near", [(1, "w_in")])],
        "sgu_fwd": [("pass", [(0, "w_up")]), ("near", [(0, "w_down")])],
        "merge_fwd": [("far", [(0, "w_down")])],
        "ffn_up": [("pass", [(0, "w_down")]), ("far", [(1, "w_in")])],
        "ffn_down": [("pass", [(1, "w_in")]), ("near", _BRANCHES_1)],
    },
    {
        "in_proj": [("far", _BRANCHES_1), ("near", [(1, "w_down")])],
        "branch_a_fwd": [("pass", _BRANCHES_1), ("far", [(1, "w_down")]), ("first", [(1, "w_up")])],
        "sgu_fwd": [("pass", [(1, "w_down"), (1, "w_up")])],
    },
)


def _layer_forward(l, x, p, w, shards, arriving, loss_head=None):
    def run(key, fn, *args, **kw):
        riding = GATHERS_RIDING[l].get(key, ())
        if not riding:
            return fn(*args, **kw)
        comms = []
        for stage, units in riding:
            mine = [shards[l2][n2] for l2, n2 in units] if stage != "pass" else None
            left = [arriving.pop(unit) for unit in units] if stage in ("far", "pass") else None
            comms.append(_gather_stage(stage, shards=mine, arrived=left))
        out, got = fn(*args, comm=_merge_comms(comms), **kw)
        got = list(got)
        for stage, units in riding:
            for l2, n2 in units:
                if stage == "pass":
                    w[l2][n2] = got.pop(0).reshape(-1, D)
                else:
                    arriving[l2, n2] = got.pop(0)
        return out

    proj, h = run("in_proj", _norm_matmul_nt, x, p["norm_mix_g"], w[l]["w_in"], tm=TM_NT, tn=TN_IN, name=f"in_proj_{l}")
    hseq, ya_pre = run(
        "branch_a_fwd", _branch_a_fwd, proj, p["conv_w"], p["conv_b"], p["wa2"], p["lru_b_a"], p["wx2"], p["lru_b_x"],
        p["lru_lambda"], tc=TC, name=f"branch_a_fwd_{l}",
    )
    yb_pre = run("sgu_fwd", _sgu_fwd, proj, p["sgu_ln_g"], p["sgu_ln_b"], p["wm"], p["sgu_bias"], tb=TB, name=f"sgu_fwd_{l}")
    x1, ya, yb = run(
        "merge_fwd", _merge_fwd, ya_pre, yb_pre, proj, x, w[l]["w_branch_a"], w[l]["w_branch_b"], w[l]["w_out"], tm=TM,
        name=f"merge_fwd_{l}",
    )
    f_pre, h2 = run("ffn_up", _norm_matmul_nt, x1, p["norm_ffn_g"], w[l]["w_up"], tm=TM_NT, tn=TN_UP, name=f"ffn_up_{l}")
    saved = dict(x=x, h=h, proj=proj, hseq=hseq, ya_pre=ya_pre, yb_pre=yb_pre, ya=ya, yb=yb, x1=x1, h2=h2, f_pre=f_pre)
    if loss_head is None:
        return run("ffn_down", _matmul_nn_res, f_pre, w[l]["w_down"], x1, relu2=True, tm=TM, name=f"ffn_down_{l}"), saved
    return _ffn_down_loss(f_pre, w[l]["w_down"], x1, *loss_head, tm=TM, name=f"ffn_down_loss_{l}"), saved


def _layer_backward(l, dx2, dx2b, sv, p, w, core, waiting, last):
    parts, from_chips = {}, {}

    def by_device(g):
        return g.reshape(4, 2, -1, D)

    def with_sibling(name, g, got):
        parts[name] = _sum_with_sibling(by_device(g), got, core, name=f"sum_sibling_{name}_{l}")

    df_pre = _matmul_nt_drelu2(dx2b, w["w_down"], sv["f_pre"], tm=TM_NT, tn=TN_UP, name=f"ffn_down_bwd_{l}")
    g_down = _matmul_tn([sv["f_pre"]], dx2b, relu2=True, tka=TKA, name=f"grad_w_down_{l}")
    g_up, (got,) = _matmul_tn(
        [df_pre], sv["h2"], relu2=False, tka=TKA, name=f"grad_w_up_{l}", comm=_sibling_comm([by_device(g_down)])
    )
    with_sibling("w_down", g_down, got)
    (dx1, dx1b, g_norm_ffn), (got,) = _matmul_nn_rmsnorm_bwd(
        [df_pre], w["w_up"], sv["x1"], p["norm_ffn_g"], dx2, tm=TM, name=f"ffn_up_bwd_{l}",
        comm=_sibling_comm([by_device(g_up)]),
    )
    with_sibling("w_up", g_up, got)
    (merged, dya, dyb, dga, dgb, dya_pre, dyb_pre), (from_chips[l, "w_up"],) = _merge_bwd(
        dx1b, sv["ya"], sv["yb"], sv["proj"], w["w_branch_a"], w["w_branch_b"], w["w_out"], tm=TM, name=f"merge_bwd_{l}",
        comm=_chips_comm([parts["w_up"]]),
    )
    g_out, g_ba, g_bb = _matmuls_tn(
        [(merged, dx1b), (sv["ya_pre"], dya), (sv["yb_pre"], dyb)], ts=2 * TM, name=f"grad_w_branches_{l}"
    )
    branch = (("w_out", g_out), ("w_branch_a", g_ba), ("w_branch_b", g_bb))
    (du, dv, g_ws, g_bs, g_lng, g_lnb), got = _sgu_bwd(
        dyb_pre, sv["proj"], p["sgu_ln_g"], p["sgu_ln_b"], p["wm"], p["wmt"], p["sgu_bias"], p["mask"], tb=TB_BWD,
        name=f"sgu_bwd_{l}",
        comm=_merge_comms([_sibling_comm([by_device(g) for _, g in branch]), _chips_comm([parts["w_down"]])]),
    )
    from_chips[l, "w_down"] = got[-1]
    for (name, g), landed in zip(branch, got):
        with_sibling(name, g, landed)
    riding = [((l, name), parts[name]) for name, _ in branch] + list(waiting)
    (dxr, dgr, g_cw, g_cb, g_ba_, g_bx, g_lam, g_wa2, g_wx2), got = _branch_a_bwd(
        dya_pre, sv["proj"], sv["hseq"], p["conv_w"], p["conv_b"], p["wa2"], p["lru_b_a"], p["wx2"], p["lru_b_x"],
        p["lru_lambda"], p["wa2t"], p["wx2t"], tc=TC_BWD, name=f"branch_a_bwd_{l}", comm=_chips_comm([part for _, part in riding]),
    )
    for (key, _), landed in zip(riding, got):
        from_chips[key] = landed
    dproj = [dxr, dgr, du, dv, dga, dgb]
    g_in = _matmul_tn(dproj, sv["h"], relu2=False, tka=TKA_PIECES, name=f"grad_w_in_{l}")
    if last:
        (got,) = _comm_only(_sibling_comm([by_device(g_in)]), name=f"grad_w_in_to_sibling_{l}")
        with_sibling("w_in", g_in, got)
        riding = _chips_comm([parts["w_in"]])
    else:
        riding = _sibling_comm([by_device(g_in)])
    (dx, dxb, g_norm_mix), (got,) = _matmul_nn_rmsnorm_bwd(
        dproj, w["w_in"], sv["x"], p["norm_mix_g"], dx1, tm=TM, name=f"in_proj_bwd_{l}", comm=riding
    )
    if last:
        from_chips[l, "w_in"] = got
    else:
        with_sibling("w_in", g_in, got)
    small = dict(
        norm_mix_g=g_norm_mix[0], conv_w=g_cw, conv_b=g_cb[0], lru_w_a=_diag_blocks(g_wa2), lru_b_a=g_ba_.reshape(RNN_HEADS, HEAD_DIM),
        lru_w_x=_diag_blocks(g_wx2), lru_b_x=g_bx.reshape(RNN_HEADS, HEAD_DIM), lru_lambda=g_lam[0], sgu_ln_g=g_lng[0],
        sgu_ln_b=g_lnb[0], sgu_w_s=g_ws, sgu_b_s=g_bs[:, :, 0], norm_ffn_g=g_norm_ffn[0],
    )
    return dx, dxb, small, parts, from_chips


def _prepare_small(l, given):
    chunk_id = jnp.arange(SGU_BLOCK) // CHUNK
    mask = (chunk_id[:, None] >= chunk_id[None, :]).astype(F32)
    wm = given["sgu_w_s"][l] * mask
    wa2 = _block_diag_pairs(given["lru_w_a"][l])
    wx2 = _block_diag_pairs(given["lru_w_x"][l])
    row = lambda a: a.reshape(1, -1)
    return dict(
        norm_mix_g=row(given["norm_mix_g"][l]),
        norm_ffn_g=row(given["norm_ffn_g"][l]),
        conv_w=given["conv_w_full"][l],
        conv_b=row(given["conv_b"][l]),
        wa2=wa2.astype(BF16),
        wx2=wx2.astype(BF16),
        wa2t=jnp.swapaxes(wa2, 1, 2).astype(BF16),
        wx2t=jnp.swapaxes(wx2, 1, 2).astype(BF16),
        lru_b_a=row(given["lru_b_a"][l]),
        lru_b_x=row(given["lru_b_x"][l]),
        lru_lambda=row(given["lru_lambda"][l]),
        sgu_ln_g=row(given["sgu_ln_g"][l]),
        sgu_ln_b=row(given["sgu_ln_b"][l]),
        wm=wm.astype(BF16),
        wmt=jnp.swapaxes(wm, 1, 2).astype(BF16),
        sgu_bias=jnp.broadcast_to(given["sgu_b_s"][l][:, :, None], (SGU_GROUPS, SGU_BLOCK, LANES)),
        mask=mask,
    )


def _step(given):
    x_idx, y_idx, c_idx = _position()
    dev = 4 * x_idx + 2 * y_idx + c_idx
    core = c_idx.astype(jnp.int32).reshape(1)
    chip = (2 * x_idx + y_idx).astype(jnp.int32).reshape(1)

    def rows_first(name, a):
        return jnp.swapaxes(a, 1, 2) if name in TRANSPOSED else a

    shards = []
    for l in range(DEPTH):
        shards.append({name: rows_first(name, given[name])[l].astype(BF16)[None] for name in BIG})
    conv_mine = given["conv_w"].reshape(1, DEPTH * CONV_WIDTH, D_RNN // N_DEV)
    w_in_first, conv_all = _comm_only(_gather_comm([shards[0]["w_in"], conv_mine]), name="gather_first")
    weights = [{"w_in": w_in_first.reshape(-1, D)}, {}]
    conv_all = conv_all.reshape(N_DEV, DEPTH, CONV_WIDTH, D_RNN // N_DEV)
    given = dict(given, conv_w_full=jnp.moveaxis(conv_all, 0, 2).reshape(DEPTH, CONV_WIDTH, D_RNN))

    small_params = [_prepare_small(l, given) for l in range(DEPTH)]
    x = given["x"][0]
    saved, arriving = [], {}
    loss_head = (given["final_norm_g"].reshape(1, D), given["loss_target"][0])
    for l in range(DEPTH):
        x, sv = _layer_forward(
            l, x, small_params[l], weights, shards, arriving, loss_head=loss_head if l == DEPTH - 1 else None
        )
        saved.append(sv)
    dx, dxb, g_final, loss = x
    small_grads, parts, from_chips, waiting = [None] * DEPTH, [None] * DEPTH, {}, []
    for l in reversed(range(DEPTH)):
        dx, dxb, small_grads[l], parts[l], got = _layer_backward(
            l, dx, dxb, saved[l], small_params[l], weights[l], core, waiting, last=l == 0
        )
        from_chips.update(got)
        waiting = [((l, "w_in"), parts[l]["w_in"])]

    small_list = []
    for name in SMALL[:-1]:
        small_list.append(jnp.stack([small_grads[l][name] for l in range(DEPTH)]))
    small_list += [g_final[0], loss[0, :1]]
    small_shapes = [a.shape for a in small_list]
    pack = _pack(small_list, SMALL_ROWS).reshape(N_DEV, SMALL_ROWS_PER_DEV, D)
    summed = _unpack(_all_reduce_small(pack, name="all_reduce_small"), small_shapes)
    loss_total = summed[-1][0]
    grads = dict(zip(SMALL, summed[:-1]))
    cw = grads["conv_w"].reshape(DEPTH, CONV_WIDTH, N_DEV, D_RNN // N_DEV)
    grads["conv_w"] = lax.dynamic_index_in_dim(cw, dev, axis=2, keepdims=False)

    delta, new_m, new_v = {}, {}, {}
    for name in BIG:
        w, m, v = given[name], given["m_" + name], given["v_" + name]
        mine = [parts[l][name] for l in range(DEPTH)]
        theirs = [from_chips[l, name] for l in range(DEPTH)]
        if name == "w_up":
            sums = [_sum_chips(mine[l], theirs[l], chip, name=f"sum_chips_{name}_{l}").T for l in range(DEPTH)]
            out = _adamw_layers(w, sums, m, v, tr=TR, name=f"adamw_{name}")
        else:
            out = _adamw_reduced(
                rows_first(name, w), mine, theirs, rows_first(name, m), rows_first(name, v), chip, tr=TR, name=f"adamw_{name}"
            )
            out = [rows_first(name, a) for a in out]
        grads[name], delta[name], new_m[name], new_v[name] = out
    two_d = lambda a: a.reshape(1, -1) if a.ndim == 1 else a
    groups = [tuple(two_d(a) for a in (given[n], grads[n], given["m_" + n], given["v_" + n])) for n in SMALL]
    for n, (d, m2, v2) in zip(SMALL, _adamw_small(groups, name="adamw_small")):
        shape = given[n].shape
        delta[n], new_m[n], new_v[n] = d.reshape(shape), m2.reshape(shape), v2.reshape(shape)

    return (
        loss_total, dx[None],
        *[grads[n] for n in WEIGHTS], *[delta[n] for n in WEIGHTS], *[new_m[n] for n in WEIGHTS], *[new_v[n] for n in WEIGHTS],
    )


def kernel(x, norm_mix_g, w_in, conv_w, conv_b, lru_w_a, lru_b_a, lru_w_x, lru_b_x, lru_lambda, sgu_ln_g, sgu_ln_b, sgu_w_s, sgu_b_s, w_branch_a, w_branch_b, w_out, norm_ffn_g, w_up, w_down, final_norm_g, loss_target, m_norm_mix_g, m_w_in, m_conv_w, m_conv_b, m_lru_w_a, m_lru_b_a, m_lru_w_x, m_lru_b_x, m_lru_lambda, m_sgu_ln_g, m_sgu_ln_b, m_sgu_w_s, m_sgu_b_s, m_w_branch_a, m_w_branch_b, m_w_out, m_norm_ffn_g, m_w_up, m_w_down, m_final_norm_g, v_norm_mix_g, v_w_in, v_conv_w, v_conv_b, v_lru_w_a, v_lru_b_a, v_lru_w_x, v_lru_b_x, v_lru_lambda, v_sgu_ln_g, v_sgu_ln_b, v_sgu_w_s, v_sgu_b_s, v_w_branch_a, v_w_branch_b, v_w_out, v_norm_ffn_g, v_w_up, v_w_down, v_final_norm_g):
    return _step(dict(locals()))
```

```python
import jax
import jax.numpy as jnp
from jax import lax
from jax.experimental import pallas as pl
from jax.experimental.pallas import tpu as pltpu

F32 = jnp.float32
BF16 = jnp.bfloat16
SDS = jax.ShapeDtypeStruct
MESH = pl.DeviceIdType.MESH

D = 1024
D_RNN = 1280
D_SGU = 1024
D_IN = 2 * D_RNN + 2 * D_SGU + 2 * D
DEPTH = 2
RNN_HEADS = 20
HEAD_DIM = 64
CONV_WIDTH = 4
LRU_C = 8.0
SGU_GROUPS = 8
SGU_BLOCK = 128
CHUNK = 64
EPS = 1e-6
N_DEV = 8

ADAM_LR = 0.001
ADAM_B1 = 0.9
ADAM_B2 = 0.999
ADAM_EPS = 1e-08
ADAM_WD = 0.01
ADAM_STEP = 10

LANES = 128
SUBLANES = 8
VMEM_LIMIT_BYTES = 56 * 1024 * 1024

N_RNN_TILES = D_RNN // LANES
RNN_TILES_PER_STEP = 5
U_BLK512 = (2 * D_RNN) // 512
V_BLK512 = (2 * D_RNN + D_SGU) // 512
GA_BLK512 = (2 * D_RNN + 2 * D_SGU) // 512
GB_BLK512 = (2 * D_RNN + 2 * D_SGU + D) // 512

SMALL_ROWS_PER_DEV = 80
SMALL_ROWS = N_DEV * SMALL_ROWS_PER_DEV


def _params(*sem):
    return pltpu.CompilerParams(dimension_semantics=sem, vmem_limit_bytes=VMEM_LIMIT_BYTES)


def _sigmoid(x):
    return 0.5 + 0.5 * jnp.tanh(0.5 * x)


_GELU_C = 0.7978845608028654
_GELU_K = 0.044715


def _gelu(x):
    t = jnp.tanh(_GELU_C * (x + _GELU_K * x * x * x))
    return 0.5 * x * (1.0 + t)


def _gelu_and_grad(x):
    t = jnp.tanh(_GELU_C * (x + _GELU_K * x * x * x))
    val = 0.5 * x * (1.0 + t)
    grad = 0.5 * (1.0 + t) + 0.5 * x * (1.0 - t * t) * _GELU_C * (1.0 + 3.0 * _GELU_K * x * x)
    return val, grad


def _one_minus_square(log_a, a):
    return -jnp.tanh(log_a) * (1.0 + a * a)


def _dot(a, b):
    return jnp.dot(a, b, preferred_element_type=F32)


def _dot_nt(a, b):
    return lax.dot_general(a, b, (((1,), (1,)), ((), ())), preferred_element_type=F32)


def _dot_tn(a, b):
    return lax.dot_general(a, b, (((0,), (0,)), ((), ())), preferred_element_type=F32)


def _norm_matmul_nt(x, g, w, *, tm, tn, name, comm=None):
    s, n = x.shape[0], w.shape[0]
    tm, tn = min(tm, s), min(tn, n)

    def body(x_ref, g_ref, w_ref, o_ref, h_ref):
        @pl.when(pl.program_id(1) == 0)
        def _():
            xv = x_ref[...]
            r = lax.rsqrt(jnp.mean(xv * xv, axis=-1, keepdims=True) + EPS)
            h_ref[...] = (xv * r * g_ref[...]).astype(BF16)

        o_ref[...] = _dot_nt(h_ref[...], w_ref[...]).astype(o_ref.dtype)

    return _call(
        body,
        (x, g, w),
        name=name,
        grid=(s // tm, n // tn),
        in_specs=[
            pl.BlockSpec((tm, D), lambda i, j: (i, 0)),
            pl.BlockSpec((1, D), lambda i, j: (0, 0)),
            pl.BlockSpec((tn, D), lambda i, j: (j, 0)),
        ],
        out_specs=[pl.BlockSpec((tm, tn), lambda i, j: (i, j)), pl.BlockSpec((tm, D), lambda i, j: (i, 0))],
        out_shape=[SDS((s, n), BF16), SDS((s, D), BF16)],
        semantics=("parallel", "arbitrary"),
        comm=comm,
    )


def _matmul_nn_res(a, w, res, *, relu2, tm, name, comm=None):
    s, k = a.shape
    tm = min(tm, s)

    def body(a_ref, w_ref, r_ref, o_ref):
        av = a_ref[...]
        if relu2:
            t = jnp.maximum(av.astype(F32), 0.0)
            av = (t * t).astype(BF16)
        o_ref[...] = r_ref[...] + _dot(av, w_ref[...])

    return _call(
        body,
        (a, w, res),
        name=name,
        grid=(s // tm,),
        in_specs=[
            pl.BlockSpec((tm, k), lambda i: (i, 0)),
            pl.BlockSpec((k, D), lambda i: (0, 0)),
            pl.BlockSpec((tm, D), lambda i: (i, 0)),
        ],
        out_specs=pl.BlockSpec((tm, D), lambda i: (i, 0)),
        out_shape=SDS((s, D), F32),
        semantics=("parallel",),
        comm=comm,
    )


def _matmul_nt_drelu2(a, w, pre, *, tm, tn, name):
    s, n = a.shape[0], w.shape[0]
    tm, tn = min(tm, s), min(tn, n)

    def body(a_ref, w_ref, p_ref, o_ref):
        d = _dot_nt(a_ref[...], w_ref[...])
        o_ref[...] = (d * (2.0 * jnp.maximum(p_ref[...].astype(F32), 0.0))).astype(o_ref.dtype)

    return pl.pallas_call(
        body,
        name=name,
        grid=(s // tm, n // tn),
        in_specs=[
            pl.BlockSpec((tm, D), lambda i, j: (i, 0)),
            pl.BlockSpec((tn, D), lambda i, j: (j, 0)),
            pl.BlockSpec((tm, tn), lambda i, j: (i, j)),
        ],
        out_specs=pl.BlockSpec((tm, tn), lambda i, j: (i, j)),
        out_shape=SDS((s, n), BF16),
        compiler_params=_params("parallel", "arbitrary"),
    )(a, w, pre)


def _matmul_tn(a_list, b, *, relu2, tka, name, comm=None):
    s = b.shape[0]
    n = len(a_list)
    nblk = [a.shape[1] // tka for a in a_list]
    starts = [sum(nblk[:p]) for p in range(n)]

    def body(*refs):
        a_refs, b_ref, o_ref = refs[:n], refs[n], refs[n + 1]
        i = pl.program_id(0)
        for p in range(n):

            @pl.when((i >= starts[p]) & (i < starts[p] + nblk[p]))
            def _(p=p):
                av = a_refs[p][...]
                if relu2:
                    t = jnp.maximum(av.astype(F32), 0.0)
                    av = (t * t).astype(BF16)
                o_ref[...] = _dot_tn(av, b_ref[...]).astype(o_ref.dtype)

    def piece_spec(p):
        return pl.BlockSpec((s, tka), lambda i: (0, jnp.clip(i - starts[p], 0, nblk[p] - 1)))

    return _call(
        body,
        (*a_list, b),
        name=name,
        grid=(sum(nblk),),
        in_specs=[piece_spec(p) for p in range(n)] + [pl.BlockSpec((s, D), lambda i: (0, 0))],
        out_specs=pl.BlockSpec((tka, D), lambda i: (i, 0)),
        out_shape=SDS((sum(nblk) * tka, D), BF16),
        semantics=("parallel",),
        comm=comm,
    )


def _matmuls_tn(pairs, *, ts, name):
    s = pairs[0][0].shape[0]
    ts = min(ts, s)
    n = len(pairs)
    steps = s // ts

    def body(*refs):
        ins, outs, accs = refs[: 2 * n], refs[2 * n : 3 * n], refs[3 * n :]
        for p in range(n):
            part = _dot_tn(ins[2 * p][...], ins[2 * p + 1][...])

            @pl.when(pl.program_id(0) == 0)
            def _(p=p, part=part):
                accs[p][...] = part

            @pl.when(pl.program_id(0) > 0)
            def _(p=p, part=part):
                accs[p][...] += part

        @pl.when(pl.program_id(0) == steps - 1)
        def _():
            for p in range(n):
                outs[p][...] = accs[p][...].astype(BF16)

    widths = [a.shape[1] for a, _ in pairs]
    in_specs = []
    for wd in widths:
        in_specs += [pl.BlockSpec((ts, wd), lambda i: (i, 0)), pl.BlockSpec((ts, D), lambda i: (i, 0))]
    return pl.pallas_call(
        body,
        name=name,
        grid=(steps,),
        in_specs=in_specs,
        out_specs=[pl.BlockSpec((wd, D), lambda i: (0, 0)) for wd in widths],
        out_shape=[SDS((wd, D), BF16) for wd in widths],
        scratch_shapes=[pltpu.VMEM((wd, D), F32) for wd in widths],
        compiler_params=_params("arbitrary"),
    )(*[x for pair in pairs for x in pair])


def _matmul_nn_rmsnorm_bwd(a_list, w, x, g, res, *, tm, name, comm=None):
    s = x.shape[0]
    tm = min(tm, s)
    n = len(a_list)
    widths = [a.shape[1] for a in a_list]
    offs = [sum(widths[:p]) for p in range(n)]
    k = sum(widths)

    def body(*refs):
        a_refs = refs[:n]
        w_ref, x_ref, g_ref, r_ref, dx_ref, dxb_ref, dg_ref = refs[n:]

        @pl.when(pl.program_id(0) == 0)
        def _():
            dg_ref[...] = jnp.zeros_like(dg_ref)

        dh = _dot(a_refs[0][...], w_ref[0 : widths[0], :])
        for p in range(1, n):
            dh += _dot(a_refs[p][...], w_ref[offs[p] : offs[p] + widths[p], :])
        xv = x_ref[...]
        r = lax.rsqrt(jnp.mean(xv * xv, axis=-1, keepdims=True) + EPS)
        xhat = xv * r
        dxh = dh * g_ref[...]
        dx = r_ref[...] + r * (dxh - xhat * jnp.mean(dxh * xhat, axis=-1, keepdims=True))
        dx_ref[...] = dx
        dxb_ref[...] = dx.astype(BF16)
        dg_ref[...] += jnp.sum(dh * xhat, axis=0, keepdims=True)

    act = pl.BlockSpec((tm, D), lambda i: (i, 0))
    vec = pl.BlockSpec((1, D), lambda i: (0, 0))
    return _call(
        body,
        (*a_list, w, x, g, res),
        name=name,
        grid=(s // tm,),
        in_specs=[pl.BlockSpec((tm, wd), lambda i: (i, 0)) for wd in widths]
        + [pl.BlockSpec((k, D), lambda i: (0, 0), pipeline_mode=pl.Buffered(1)), act, vec, act],
        out_specs=[act, act, vec],
        out_shape=[SDS((s, D), F32), SDS((s, D), BF16), SDS((1, D), F32)],
        semantics=("arbitrary",),
        comm=comm,
    )


def _rows_after(ext, k, n):
    return pltpu.roll(ext, n + SUBLANES - k, 0)[:n, :]


def _scan_forward(a, b, n):
    row = lax.broadcasted_iota(jnp.int32, a.shape, 0)
    d = 1
    while d < n:
        if d < SUBLANES:
            m = row >= d
            a_s = jnp.where(m, pltpu.roll(a, d, 0), 1.0)
            b_s = jnp.where(m, pltpu.roll(b, d, 0), 0.0)
            b = a * b_s + b
            a = a * a_s
        else:
            b = jnp.concatenate([b[:d], a[d:] * b[: n - d] + b[d:]], axis=0)
            a = jnp.concatenate([a[:d], a[d:] * a[: n - d]], axis=0)
        d *= 2
    return a, b


def _scan_backward(a, b, n):
    row = lax.broadcasted_iota(jnp.int32, a.shape, 0)
    d = 1
    while d < n:
        if d < SUBLANES:
            m = row < n - d
            a_s = jnp.where(m, pltpu.roll(a, n - d, 0), 1.0)
            b_s = jnp.where(m, pltpu.roll(b, n - d, 0), 0.0)
            b = a * b_s + b
            a = a * a_s
        else:
            b = jnp.concatenate([a[: n - d] * b[d:] + b[: n - d], b[n - d :]], axis=0)
            a = jnp.concatenate([a[: n - d] * a[d:], a[n - d :]], axis=0)
        d *= 2
    return b


def _repeat_matrix(n):
    groups = n // SUBLANES
    return (jnp.arange(n)[:, None] // SUBLANES == jnp.arange(3 * groups)[None, :] % groups).astype(BF16)


def _scan_rows(a, b, n, repeat_ref, a_scr, b_scr, reverse):
    groups = n // SUBLANES
    a3 = a.reshape(groups, SUBLANES, LANES)
    b3 = b.reshape(groups, SUBLANES, LANES)
    sub = lax.broadcasted_iota(jnp.int32, a3.shape, 1)
    for d in (1, 2, 4):
        m = (sub < SUBLANES - d) if reverse else (sub >= d)
        shift = SUBLANES - d if reverse else d
        a_s = jnp.where(m, pltpu.roll(a3, shift, 1), 1.0)
        b_s = jnp.where(m, pltpu.roll(b3, shift, 1), 0.0)
        b3 = a3 * b_s + b3
        a3 = a3 * a_s
    a_scr[...] = a3.reshape(n, LANES)
    b_scr[...] = b3.reshape(n, LANES)
    edge = 0 if reverse else SUBLANES - 1
    a_tot = a_scr[pl.ds(edge, groups, stride=SUBLANES), :]
    b_tot = b_scr[pl.ds(edge, groups, stride=SUBLANES), :]
    row = lax.broadcasted_iota(jnp.int32, a_tot.shape, 0)
    if reverse:
        through = _scan_backward(a_tot, b_tot, groups)
        entering = jnp.where(row < groups - 1, pltpu.roll(through, groups - 1, 0), 0.0)
    else:
        _, through = _scan_forward(a_tot, b_tot, groups)
        entering = jnp.where(row >= 1, pltpu.roll(through, 1, 0), 0.0)
    hi = entering.astype(BF16)
    rest = entering - hi.astype(F32)
    mid = rest.astype(BF16)
    lo = (rest - mid.astype(F32)).astype(BF16)
    repeated = _dot(repeat_ref[...], jnp.concatenate([hi, mid, lo], axis=0))
    return b_scr[...] + a_scr[...] * repeated


def _softplus_neg(lam):
    z = -lam
    return jnp.maximum(z, 0.0) + jnp.log1p(jnp.exp(-jnp.abs(z)))


def _conv_and_gates(xc, xprev, cw_ref, cb_ref, wa_ref, ba_ref, wx_ref, bx_ref, lam_ref, ext_scr):
    n = xc.shape[0]
    ext_scr[:SUBLANES, :] = xprev
    ext_scr[SUBLANES:, :] = xc
    x1, x2, x3 = (ext_scr[pl.ds(SUBLANES - k, n), :] for k in (1, 2, 3))
    xr = cb_ref[...] + x3 * cw_ref[0:1, :] + x2 * cw_ref[1:2, :] + x1 * cw_ref[2:3, :] + xc * cw_ref[3:4, :]
    xrb = xr.astype(BF16)
    r = _sigmoid(_dot(xrb, wa_ref[...]) + ba_ref[...])
    i = _sigmoid(_dot(xrb, wx_ref[...]) + bx_ref[...])
    sp = _softplus_neg(lam_ref[...])
    log_a = (-LRU_C * r) * sp
    a = jnp.exp(log_a)
    return xr, (x1, x2, x3), r, i, a, _one_minus_square(log_a, a)


def _branch_a_fwd(proj, cw, cb, wa2, ba, wx2, bx, lam, *, tc, name, comm=None):
    s = proj.shape[0]
    tc = min(tc, s)

    def body(x_ref, g_ref, cw_ref, cb_ref, wa_ref, ba_ref, wx_ref, bx_ref, lam_ref, rep_ref, h_ref, y_ref,
             xprev, hlast, a_scr, b_scr, ext_scr):
        @pl.when(pl.program_id(1) == 0)
        def _():
            xprev[...] = jnp.zeros_like(xprev)
            hlast[...] = jnp.zeros_like(hlast)

        for t in range(RNN_TILES_PER_STEP):
            cols = lambda ref: ref.at[:, pl.ds(t * LANES, LANES)]
            one_tile(
                cols(x_ref), cols(g_ref), cols(cw_ref), cols(cb_ref), wa_ref.at[t], cols(ba_ref), wx_ref.at[t], cols(bx_ref),
                cols(lam_ref), rep_ref, cols(h_ref), cols(y_ref), cols(xprev), cols(hlast), a_scr.at[t], b_scr.at[t],
                ext_scr.at[t],
            )

    def one_tile(x_ref, g_ref, cw_ref, cb_ref, wa_ref, ba_ref, wx_ref, bx_ref, lam_ref, rep_ref, h_ref, y_ref,
                 xprev, hlast, a_scr, b_scr, ext_scr):
        xc = x_ref[...].astype(F32)
        xr, _, r, i, a, om = _conv_and_gates(
            xc, xprev[...], cw_ref, cb_ref, wa_ref, ba_ref, wx_ref, bx_ref, lam_ref, ext_scr
        )
        xprev[...] = xc[tc - SUBLANES :, :]
        u = jnp.sqrt(om) * (i * xr)
        row8 = lax.broadcasted_iota(jnp.int32, (SUBLANES, LANES), 0)
        first = u[:SUBLANES] + jnp.where(row8 == 0, a[:SUBLANES] * hlast[SUBLANES - 1 : SUBLANES, :], 0.0)
        h = _scan_rows(a, jnp.concatenate([first, u[SUBLANES:]], axis=0), tc, rep_ref, a_scr, b_scr, reverse=False)
        hlast[...] = h[tc - SUBLANES :, :]
        h_ref[...] = h
        y_ref[...] = (h * _gelu(g_ref[...].astype(F32))).astype(BF16)

    wide = RNN_TILES_PER_STEP * LANES
    tile = lambda j, c: (0, j)
    vec = pl.BlockSpec((1, wide), tile)
    mats = pl.BlockSpec((RNN_TILES_PER_STEP, LANES, LANES), lambda j, c: (j, 0, 0))
    repeat = _repeat_matrix(tc)
    return _call(
        body,
        (proj, proj, cw, cb, wa2, ba, wx2, bx, lam, repeat),
        name=name,
        grid=(N_RNN_TILES // RNN_TILES_PER_STEP, s // tc),
        in_specs=[
            pl.BlockSpec((tc, wide), lambda j, c: (c, j)),
            pl.BlockSpec((tc, wide), lambda j, c: (c, D_RNN // wide + j)),
            pl.BlockSpec((CONV_WIDTH, wide), tile),
            vec,
            mats,
            vec,
            mats,
            vec,
            vec,
            pl.BlockSpec(repeat.shape, lambda j, c: (0, 0)),
        ],
        out_specs=[pl.BlockSpec((tc, wide), lambda j, c: (c, j)), pl.BlockSpec((tc, wide), lambda j, c: (c, j))],
        out_shape=[SDS((s, D_RNN), F32), SDS((s, D_RNN), BF16)],
        scratch_shapes=[pltpu.VMEM((SUBLANES, wide), F32)] * 2
        + [pltpu.VMEM((RNN_TILES_PER_STEP, tc, LANES), F32)] * 2
        + [pltpu.VMEM((RNN_TILES_PER_STEP, tc + SUBLANES, LANES), F32)],
        semantics=("parallel", "arbitrary"),
        comm=comm,
    )


def _branch_a_bwd(dy, proj, h, cw, cb, wa2, ba, wx2, bx, lam, wa2t, wx2t, *, tc, name, comm=None):
    s = proj.shape[0]
    tc = min(tc, s)
    nc = s // tc
    halo16 = tc // 16
    halo8 = tc // SUBLANES

    def body(dy_ref, x_ref, xh_ref, g_ref, h_ref, hh_ref, cw_ref, cb_ref, wa_ref, ba_ref, wx_ref, bx_ref, lam_ref,
             wat_ref, wxt_ref, rep_ref, dx_ref, dg_ref, dcw_ref, dcb_ref, dba_ref, dbx_ref, dlam_ref, dwa_ref, dwx_ref,
             carry, dxr_next, a_scr, b_scr, ext_scr):
        cc = pl.program_id(1)
        ct = nc - 1 - cc

        @pl.when(cc == 0)
        def _():
            carry[...] = jnp.zeros_like(carry)
            dxr_next[...] = jnp.zeros_like(dxr_next)
            for ref in (dcw_ref, dcb_ref, dba_ref, dbx_ref, dlam_ref, dwa_ref, dwx_ref):
                ref[...] = jnp.zeros_like(ref)

        for t in range(RNN_TILES_PER_STEP):
            cols = lambda ref: ref.at[:, pl.ds(t * LANES, LANES)]
            one_tile(
                ct, cols(dy_ref), cols(x_ref), cols(xh_ref), cols(g_ref), cols(h_ref), cols(hh_ref), cols(cw_ref), cols(cb_ref),
                wa_ref.at[t], cols(ba_ref), wx_ref.at[t], cols(bx_ref), cols(lam_ref), wat_ref.at[t], wxt_ref.at[t], rep_ref,
                cols(dx_ref), cols(dg_ref), cols(dcw_ref), cols(dcb_ref), cols(dba_ref), cols(dbx_ref), cols(dlam_ref),
                dwa_ref.at[t], dwx_ref.at[t], cols(carry), cols(dxr_next), a_scr.at[t], b_scr.at[t], ext_scr.at[t],
            )

    def one_tile(ct, dy_ref, x_ref, xh_ref, g_ref, h_ref, hh_ref, cw_ref, cb_ref, wa_ref, ba_ref, wx_ref, bx_ref, lam_ref,
                 wat_ref, wxt_ref, rep_ref, dx_ref, dg_ref, dcw_ref, dcb_ref, dba_ref, dbx_ref, dlam_ref, dwa_ref, dwx_ref,
                 carry, dxr_next, a_scr, b_scr, ext_scr):
        xc = x_ref[...].astype(F32)
        xprev = jnp.where(ct > 0, xh_ref[SUBLANES:, :].astype(F32), 0.0)
        xr, (x1, x2, x3), r, i, a, om = _conv_and_gates(
            xc, xprev, cw_ref, cb_ref, wa_ref, ba_ref, wx_ref, bx_ref, lam_ref, ext_scr
        )
        inv_norm = lax.rsqrt(om)
        norm = om * inv_norm
        row = lax.broadcasted_iota(jnp.int32, xc.shape, 0)

        hv = h_ref[...]
        ge, ge_grad = _gelu_and_grad(g_ref[...].astype(F32))
        dyv = dy_ref[...].astype(F32)
        dg_ref[...] = (dyv * hv * ge_grad).astype(dg_ref.dtype)
        dh = dyv * ge

        b = dh + jnp.where(row == tc - 1, carry[0:1, :], 0.0)
        a_next = jnp.where(row < tc - 1, pltpu.roll(a, tc - 1, 0), 0.0)
        gadj = _scan_rows(a_next, b, tc, rep_ref, a_scr, b_scr, reverse=True)
        carry[...] = (a * gadj)[:SUBLANES, :]

        hprev_first = jnp.where(ct > 0, hh_ref[SUBLANES - 1 : SUBLANES, :], 0.0)
        hprev = jnp.where(row >= 1, pltpu.roll(hv, 1, 0), hprev_first)
        da = gadj * hprev
        ix = i * xr
        dnorm = gadj * ix
        di = gadj * norm * xr
        dlog_a = da * a - dnorm * (1.0 - om) * inv_norm
        sp = _softplus_neg(lam_ref[...])
        dr = dlog_a * (-LRU_C * sp)
        dsp = jnp.sum(dlog_a * (-LRU_C * r), axis=0, keepdims=True)
        dlam_ref[...] += dsp * (-_sigmoid(-lam_ref[...]))
        dza = dr * r * (1.0 - r)
        dzx = di * i * (1.0 - i)
        dzab, dzxb = dza.astype(BF16), dzx.astype(BF16)
        dxr = gadj * norm * i + _dot(dzab, wat_ref[...]) + _dot(dzxb, wxt_ref[...])
        xrb = xr.astype(BF16)
        dwa_ref[...] += _dot_tn(xrb, dzab)
        dwx_ref[...] += _dot_tn(xrb, dzxb)
        dba_ref[...] += jnp.sum(dza, axis=0, keepdims=True)
        dbx_ref[...] += jnp.sum(dzx, axis=0, keepdims=True)

        ext = jnp.concatenate([dxr, dxr_next[...]], axis=0)
        dx = (
            dxr * cw_ref[3:4, :]
            + _rows_after(ext, 1, tc) * cw_ref[2:3, :]
            + _rows_after(ext, 2, tc) * cw_ref[1:2, :]
            + _rows_after(ext, 3, tc) * cw_ref[0:1, :]
        )
        dxr_next[...] = dxr[:SUBLANES, :]
        dx_ref[...] = dx.astype(dx_ref.dtype)
        dcb_ref[...] += jnp.sum(dxr, axis=0, keepdims=True)
        dcw_ref[3:4, :] += jnp.sum(dxr * xc, axis=0, keepdims=True)
        dcw_ref[2:3, :] += jnp.sum(dxr * x1, axis=0, keepdims=True)
        dcw_ref[1:2, :] += jnp.sum(dxr * x2, axis=0, keepdims=True)
        dcw_ref[0:1, :] += jnp.sum(dxr * x3, axis=0, keepdims=True)

    wide = RNN_TILES_PER_STEP * LANES
    tile = lambda j, c: (0, j)
    mat = lambda j, c: (j, 0, 0)
    cur = lambda j, c: (nc - 1 - c, j)
    vec = pl.BlockSpec((1, wide), tile)
    matspec = pl.BlockSpec((RNN_TILES_PER_STEP, LANES, LANES), mat)
    repeat = _repeat_matrix(tc)
    return _call(
        body,
        (dy, proj, proj, proj, h, h, cw, cb, wa2, ba, wx2, bx, lam, wa2t, wx2t, repeat),
        name=name,
        grid=(N_RNN_TILES // RNN_TILES_PER_STEP, nc),
        in_specs=[
            pl.BlockSpec((tc, wide), cur),
            pl.BlockSpec((tc, wide), cur),
            pl.BlockSpec((16, wide), lambda j, c: (jnp.maximum((nc - 1 - c) * halo16 - 1, 0), j)),
            pl.BlockSpec((tc, wide), lambda j, c: (nc - 1 - c, D_RNN // wide + j)),
            pl.BlockSpec((tc, wide), cur),
            pl.BlockSpec((SUBLANES, wide), lambda j, c: (jnp.maximum((nc - 1 - c) * halo8 - 1, 0), j)),
            pl.BlockSpec((CONV_WIDTH, wide), tile),
            vec,
            matspec,
            vec,
            matspec,
            vec,
            vec,
            matspec,
            matspec,
            pl.BlockSpec(repeat.shape, lambda j, c: (0, 0)),
        ],
        out_specs=[
            pl.BlockSpec((tc, wide), cur),
            pl.BlockSpec((tc, wide), cur),
            pl.BlockSpec((CONV_WIDTH, wide), tile),
            vec,
            vec,
            vec,
            vec,
            matspec,
            matspec,
        ],
        out_shape=[
            SDS((s, D_RNN), BF16),
            SDS((s, D_RNN), BF16),
            SDS((CONV_WIDTH, D_RNN), F32),
            SDS((1, D_RNN), F32),
            SDS((1, D_RNN), F32),
            SDS((1, D_RNN), F32),
            SDS((1, D_RNN), F32),
            SDS((N_RNN_TILES, LANES, LANES), F32),
            SDS((N_RNN_TILES, LANES, LANES), F32),
        ],
        scratch_shapes=[pltpu.VMEM((SUBLANES, wide), F32)] * 2
        + [pltpu.VMEM((RNN_TILES_PER_STEP, tc, LANES), F32)] * 2
        + [pltpu.VMEM((RNN_TILES_PER_STEP, tc + SUBLANES, LANES), F32)],
        semantics=("parallel", "arbitrary"),
        comm=comm,
    )


def _sgu_specs(tb):
    half = lambda blk: pl.BlockSpec((tb, 512), lambda n: (n, blk))
    return [half(U_BLK512), half(U_BLK512 + 1), half(V_BLK512), half(V_BLK512 + 1)]


def _sgu_normed(v, lng_ref, lnb_ref):
    gv, gv_grad = _gelu_and_grad(v)
    mu = jnp.mean(gv, axis=-1, keepdims=True)
    xc = gv - mu
    rs = lax.rsqrt(jnp.mean(xc * xc, axis=-1, keepdims=True) + EPS)
    xhat = xc * rs
    return xhat * lng_ref[...] + lnb_ref[...], xhat, rs, gv_grad


def _sgu_fwd(proj, lng, lnb, wm, bias, *, tb, name, comm=None):
    s = proj.shape[0]
    tb = min(tb, s)

    def body(u0_ref, u1_ref, v0_ref, v1_ref, lng_ref, lnb_ref, wm_ref, bias_ref, y_ref):
        u = jnp.concatenate([u0_ref[...], u1_ref[...]], axis=1).astype(F32)
        v = jnp.concatenate([v0_ref[...], v1_ref[...]], axis=1).astype(F32)
        gu = _gelu(u)
        vn, _, _, _ = _sgu_normed(v, lng_ref, lnb_ref)
        vnb = vn.astype(BF16)
        for blk in range(tb // SGU_BLOCK):
            rows = slice(blk * SGU_BLOCK, (blk + 1) * SGU_BLOCK)
            for g in range(SGU_GROUPS):
                cols = slice(g * LANES, (g + 1) * LANES)
                mixed = _dot(wm_ref[g], vnb[rows, cols]) + bias_ref[g]
                y_ref[rows, cols] = (gu[rows, cols] * mixed).astype(BF16)

    const2 = lambda n: (0, 0)
    const3 = lambda n: (0, 0, 0)
    return _call(
        body,
        (proj, proj, proj, proj, lng, lnb, wm, bias),
        name=name,
        grid=(s // tb,),
        in_specs=_sgu_specs(tb)
        + [
            pl.BlockSpec((1, D_SGU), const2),
            pl.BlockSpec((1, D_SGU), const2),
            pl.BlockSpec((SGU_GROUPS, SGU_BLOCK, SGU_BLOCK), const3),
            pl.BlockSpec((SGU_GROUPS, SGU_BLOCK, LANES), const3),
        ],
        out_specs=pl.BlockSpec((tb, D_SGU), lambda n: (n, 0)),
        out_shape=SDS((s, D_SGU), BF16),
        semantics=("parallel",),
        comm=comm,
    )


def _sgu_bwd(dy, proj, lng, lnb, wm, wmt, bias, mask, *, tb, name, comm=None):
    s = proj.shape[0]
    tb = min(tb, s)
    nb = s // tb

    def body(dy_ref, u0_ref, u1_ref, v0_ref, v1_ref, lng_ref, lnb_ref, wm_ref, wmt_ref, bias_ref, mask_ref,
             du_ref, dv_ref, dws_ref, dbs_ref, dlng_ref, dlnb_ref, dvn_scr, dbs_acc):
        n = pl.program_id(0)

        @pl.when(n == 0)
        def _():
            dbs_acc[...] = jnp.zeros_like(dbs_acc)
            for ref in (dws_ref, dlng_ref, dlnb_ref):
                ref[...] = jnp.zeros_like(ref)

        u = jnp.concatenate([u0_ref[...], u1_ref[...]], axis=1).astype(F32)
        v = jnp.concatenate([v0_ref[...], v1_ref[...]], axis=1).astype(F32)
        gu, gu_grad = _gelu_and_grad(u)
        vn, xhat, rs, gv_grad = _sgu_normed(v, lng_ref, lnb_ref)
        vnb = vn.astype(BF16)
        dyv = dy_ref[...].astype(F32)
        for blk in range(tb // SGU_BLOCK):
            rows = slice(blk * SGU_BLOCK, (blk + 1) * SGU_BLOCK)
            for g in range(SGU_GROUPS):
                cols = slice(g * LANES, (g + 1) * LANES)
                vt = vnb[rows, cols]
                mixed = _dot(wm_ref[g], vt) + bias_ref[g]
                dyt = dyv[rows, cols]
                du_ref[rows, cols] = (dyt * mixed * gu_grad[rows, cols]).astype(BF16)
                dmix = dyt * gu[rows, cols]
                dmixb = dmix.astype(BF16)
                dvn_scr[rows, cols] = _dot(wmt_ref[g], dmixb)
                dws_ref[g] += _dot_nt(dmixb, vt) * mask_ref[...]
                dbs_acc[g] += dmix
        dvn = dvn_scr[...]
        dlng_ref[...] += jnp.sum(dvn * xhat, axis=0, keepdims=True)
        dlnb_ref[...] += jnp.sum(dvn, axis=0, keepdims=True)
        dxh = dvn * lng_ref[...]
        dgv = rs * (
            dxh - jnp.mean(dxh, axis=-1, keepdims=True) - xhat * jnp.mean(dxh * xhat, axis=-1, keepdims=True)
        )
        dv_ref[...] = (dgv * gv_grad).astype(BF16)

        @pl.when(n == nb - 1)
        def _():
            for g in range(SGU_GROUPS):
                dbs_ref[g] = jnp.broadcast_to(jnp.sum(dbs_acc[g], axis=-1, keepdims=True), (SGU_BLOCK, LANES))

    const2 = lambda n: (0, 0)
    const3 = lambda n: (0, 0, 0)
    gmat = pl.BlockSpec((SGU_GROUPS, SGU_BLOCK, SGU_BLOCK), const3)
    vec = pl.BlockSpec((1, D_SGU), const2)
    act = pl.BlockSpec((tb, D_SGU), lambda n: (n, 0))
    return _call(
        body,
        (dy, proj, proj, proj, proj, lng, lnb, wm, wmt, bias, mask),
        name=name,
        grid=(nb,),
        in_specs=[act] + _sgu_specs(tb) + [vec, vec, gmat, gmat, gmat, pl.BlockSpec((SGU_BLOCK, SGU_BLOCK), const2)],
        out_specs=[act, act, gmat, gmat, vec, vec],
        out_shape=[
            SDS((s, D_SGU), BF16),
            SDS((s, D_SGU), BF16),
            SDS((SGU_GROUPS, SGU_BLOCK, SGU_BLOCK), F32),
            SDS((SGU_GROUPS, SGU_BLOCK, LANES), F32),
            SDS((1, D_SGU), F32),
            SDS((1, D_SGU), F32),
        ],
        scratch_shapes=[pltpu.VMEM((tb, D_SGU), F32), pltpu.VMEM((SGU_GROUPS, SGU_BLOCK, LANES), F32)],
        semantics=("arbitrary",),
        comm=comm,
    )


def _gate_specs(tm):
    half = lambda blk: pl.BlockSpec((tm, 512), lambda i: (i, blk))
    return [half(GA_BLK512), half(GA_BLK512 + 1), half(GB_BLK512), half(GB_BLK512 + 1)]


def _merge_fwd(ya_pre, yb_pre, proj, x, w_ba, w_bb, w_out, *, tm, name, comm=None):
    s = x.shape[0]
    tm = min(tm, s)

    def body(ya_ref, yb_ref, a0, a1, b0, b1, x_ref, wa_ref, wb_ref, wo_ref, x1_ref, yao_ref, ybo_ref):
        ya = _dot(ya_ref[...], wa_ref[...])
        yb = _dot(yb_ref[...], wb_ref[...])
        sa = _sigmoid(jnp.concatenate([a0[...], a1[...]], axis=1).astype(F32))
        sb = _sigmoid(jnp.concatenate([b0[...], b1[...]], axis=1).astype(F32))
        merged = sa * ya + sb * yb
        x1_ref[...] = x_ref[...] + _dot(merged.astype(BF16), wo_ref[...])
        yao_ref[...] = ya.astype(BF16)
        ybo_ref[...] = yb.astype(BF16)

    whole = lambda r: pl.BlockSpec((r, D), lambda i: (0, 0))
    act = pl.BlockSpec((tm, D), lambda i: (i, 0))
    return _call(
        body,
        (ya_pre, yb_pre, proj, proj, proj, proj, x, w_ba, w_bb, w_out),
        name=name,
        grid=(s // tm,),
        in_specs=[pl.BlockSpec((tm, D_RNN), lambda i: (i, 0)), act] + _gate_specs(tm) + [act, whole(D_RNN), whole(D_SGU), whole(D)],
        out_specs=[act, act, act],
        out_shape=[SDS((s, D), F32), SDS((s, D), BF16), SDS((s, D), BF16)],
        semantics=("parallel",),
        comm=comm,
    )


def _merge_bwd(dx1, ya, yb, proj, w_ba, w_bb, w_out, *, tm, name, comm=None):
    s = dx1.shape[0]
    tm = min(tm, s)

    def body(dx_ref, ya_ref, yb_ref, a0, a1, b0, b1, wa_ref, wb_ref, wo_ref,
             mg_ref, dya_ref, dyb_ref, dga_ref, dgb_ref, dyap_ref, dybp_ref):
        dm = _dot_nt(dx_ref[...], wo_ref[...])
        ya = ya_ref[...].astype(F32)
        yb = yb_ref[...].astype(F32)
        sa = _sigmoid(jnp.concatenate([a0[...], a1[...]], axis=1).astype(F32))
        sb = _sigmoid(jnp.concatenate([b0[...], b1[...]], axis=1).astype(F32))
        mg_ref[...] = (sa * ya + sb * yb).astype(BF16)
        dya = (dm * sa).astype(BF16)
        dyb = (dm * sb).astype(BF16)
        dya_ref[...] = dya
        dyb_ref[...] = dyb
        dga_ref[...] = (dm * ya * sa * (1.0 - sa)).astype(BF16)
        dgb_ref[...] = (dm * yb * sb * (1.0 - sb)).astype(BF16)
        dyap_ref[...] = _dot_nt(dya, wa_ref[...]).astype(BF16)
        dybp_ref[...] = _dot_nt(dyb, wb_ref[...]).astype(BF16)

    whole = lambda r: pl.BlockSpec((r, D), lambda i: (0, 0))
    act = pl.BlockSpec((tm, D), lambda i: (i, 0))
    act_rnn = pl.BlockSpec((tm, D_RNN), lambda i: (i, 0))
    return _call(
        body,
        (dx1, ya, yb, proj, proj, proj, proj, w_ba, w_bb, w_out),
        name=name,
        grid=(s // tm,),
        in_specs=[act, act, act] + _gate_specs(tm) + [whole(D_RNN), whole(D_SGU), whole(D)],
        out_specs=[act, act, act, act, act, act_rnn, act],
        out_shape=[SDS((s, D), BF16)] * 5 + [SDS((s, D_RNN), BF16), SDS((s, D_SGU), BF16)],
        semantics=("parallel",),
        comm=comm,
    )


def _ffn_down_loss(a, w, res, g, target, *, tm, name):
    s, k = a.shape
    tm = min(tm, s)

    def body(a_ref, w_ref, r_ref, g_ref, t_ref, dx_ref, dxb_ref, dg_ref, loss_ref):
        @pl.when(pl.program_id(0) == 0)
        def _():
            dg_ref[...] = jnp.zeros_like(dg_ref)
            loss_ref[...] = jnp.zeros_like(loss_ref)

        t = jnp.maximum(a_ref[...].astype(F32), 0.0)
        xv = r_ref[...] + _dot((t * t).astype(BF16), w_ref[...])
        r = lax.rsqrt(jnp.mean(xv * xv, axis=-1, keepdims=True) + EPS)
        xhat = xv * r
        e = xhat * g_ref[...] - t_ref[...]
        loss_ref[...] += 0.5 * jnp.sum(jnp.mean(e * e, axis=-1, keepdims=True), axis=0, keepdims=True)
        dy = e * (1.0 / D)
        dxh = dy * g_ref[...]
        dx = r * (dxh - xhat * jnp.mean(dxh * xhat, axis=-1, keepdims=True))
        dx_ref[...] = dx
        dxb_ref[...] = dx.astype(BF16)
        dg_ref[...] += jnp.sum(dy * xhat, axis=0, keepdims=True)

    act = pl.BlockSpec((tm, D), lambda i: (i, 0))
    vec = pl.BlockSpec((1, D), lambda i: (0, 0))
    return pl.pallas_call(
        body,
        name=name,
        grid=(s // tm,),
        in_specs=[pl.BlockSpec((tm, k), lambda i: (i, 0)), pl.BlockSpec((k, D), lambda i: (0, 0)), act, vec, act],
        out_specs=[act, act, vec, pl.BlockSpec((SUBLANES, LANES), lambda i: (0, 0))],
        out_shape=[SDS((s, D), F32), SDS((s, D), BF16), SDS((1, D), F32), SDS((SUBLANES, LANES), F32)],
        compiler_params=_params("arbitrary"),
    )(a, w, res, g, target)


def _adamw_math(w, g, m, v):
    m2 = ADAM_B1 * m + (1.0 - ADAM_B1) * g
    v2 = ADAM_B2 * v + (1.0 - ADAM_B2) * (g * g)
    m_hat = m2 / (1.0 - ADAM_B1**ADAM_STEP)
    v_hat = v2 / (1.0 - ADAM_B2**ADAM_STEP)
    delta = -ADAM_LR * (m_hat / (jnp.sqrt(v_hat) + ADAM_EPS) + ADAM_WD * w)
    return delta, m2, v2


def _row_tile(rows, cap):
    return max(t for t in range(SUBLANES, min(cap, rows) + 1, SUBLANES) if rows % t == 0)


def _adamw_layers(w, grads, m, v, *, tr, name):
    depth, r, c = w.shape
    tr = _row_tile(r, tr)

    def body(*refs):
        g_refs = refs[:depth]
        w_ref, m_ref, v_ref, g_out, d_ref, mo_ref, vo_ref = refs[depth:]
        for l in range(depth):

            @pl.when(pl.program_id(0) == l)
            def _(l=l):
                g = g_refs[l][...]
                g_out[...] = g
                d_ref[...], mo_ref[...], vo_ref[...] = _adamw_math(w_ref[...], g, m_ref[...], v_ref[...])

    def of_layer(ll):
        return pl.BlockSpec((tr, c), lambda l, i: (jnp.where(l == ll, i, 0), 0))

    stacked = pl.BlockSpec((None, tr, c), lambda l, i: (l, i, 0))
    return pl.pallas_call(
        body,
        name=name,
        grid=(depth, r // tr),
        in_specs=[of_layer(ll) for ll in range(depth)] + [stacked] * 3,
        out_specs=[stacked] * 4,
        out_shape=[SDS((depth, r, c), F32)] * 4,
        compiler_params=_params("parallel", "parallel"),
    )(*grads, w, m, v)


def _adamw_reduced(w, parts, from_chips, m, v, chip, *, tr, name):
    depth, r, _ = w.shape
    tr = _row_tile(r, tr)

    def body(chip_ref, *refs):
        p_refs, c_refs = refs[:depth], refs[depth : 2 * depth]
        w_ref, m_ref, v_ref, g_out, d_ref, mo_ref, vo_ref = refs[2 * depth :]
        for l in range(depth):

            @pl.when(pl.program_id(0) == l)
            def _(l=l):
                got = c_refs[l]
                g = ((p_refs[l][...].astype(F32) + got[0].astype(F32)) + got[1].astype(F32)) + got[2].astype(F32)
                g_out[...] = g
                d_ref[...], mo_ref[...], vo_ref[...] = _adamw_math(w_ref[...], g, m_ref[...], v_ref[...])

    def mine_of_layer(ll):
        return pl.BlockSpec((None, tr, D), lambda l, i, chip_ref: (chip_ref[0], jnp.where(l == ll, i, 0), 0))

    def theirs_of_layer(ll):
        return pl.BlockSpec((3, tr, D), lambda l, i, chip_ref: (0, jnp.where(l == ll, i, 0), 0))

    stacked = pl.BlockSpec((None, tr, D), lambda l, i, chip_ref: (l, i, 0))
    return pl.pallas_call(
        body,
        name=name,
        grid_spec=pltpu.PrefetchScalarGridSpec(
            num_scalar_prefetch=1,
            grid=(depth, r // tr),
            in_specs=[mine_of_layer(ll) for ll in range(depth)]
            + [theirs_of_layer(ll) for ll in range(depth)]
            + [stacked] * 3,
            out_specs=[stacked] * 4,
        ),
        out_shape=[SDS((depth, r, D), F32)] * 4,
        compiler_params=_params("parallel", "parallel"),
    )(chip, *parts, *from_chips, w, m, v)


def _adamw_small(groups, *, name):
    n = len(groups)

    def body(*refs):
        ins, outs = refs[: 4 * n], refs[4 * n :]
        for i in range(n):
            w, g, m, v = (ref[...] for ref in ins[4 * i : 4 * i + 4])
            outs[3 * i][...], outs[3 * i + 1][...], outs[3 * i + 2][...] = _adamw_math(w, g, m, v)

    vmem = pl.BlockSpec(memory_space=pltpu.VMEM)
    outs = pl.pallas_call(
        body,
        name=name,
        in_specs=[vmem] * (4 * n),
        out_specs=[vmem] * (3 * n),
        out_shape=[SDS(grp[0].shape, F32) for grp in groups for _ in range(3)],
        compiler_params=pltpu.CompilerParams(vmem_limit_bytes=VMEM_LIMIT_BYTES),
    )(*[a for grp in groups for a in grp])
    return [tuple(outs[3 * i : 3 * i + 3]) for i in range(n)]


ANY = pl.BlockSpec(memory_space=pl.ANY)


def _position():
    return lax.axis_index("x"), lax.axis_index("y"), lax.axis_index("c")


def _other_chips(x, y):
    return [(1 - x, y), (x, 1 - y), (1 - x, 1 - y)]


class _Comm:
    def __init__(self, inputs, out_shapes, sem_counts, start, finish, aliases=()):
        self.inputs, self.out_shapes, self.sem_counts = list(inputs), list(out_shapes), list(sem_counts)
        self.start, self.finish = start, finish
        self.aliases = list(aliases)

    def sem_shapes(self):
        return [pltpu.SemaphoreType.DMA((n,)) for n in self.sem_counts]


def _merge_comms(comms):
    bounds, i, o, s = [], 0, 0, 0
    for cm in comms:
        bounds.append((i, i + len(cm.inputs), o, o + len(cm.out_shapes), s, s + len(cm.sem_counts)))
        i, o, s = bounds[-1][1], bounds[-1][3], bounds[-1][5]

    def phase(which):
        def run(ins, outs, sems):
            for cm, (i0, i1, o0, o1, s0, s1) in zip(comms, bounds):
                getattr(cm, which)(ins[i0:i1], outs[o0:o1], sems[s0:s1])

        return run

    return _Comm(
        [a for cm in comms for a in cm.inputs],
        [a for cm in comms for a in cm.out_shapes],
        [a for cm in comms for a in cm.sem_counts],
        phase("start"),
        phase("finish"),
        aliases=[(i0 + i, o0 + o) for cm, (i0, _, o0, _, _, _) in zip(comms, bounds) for i, o in cm.aliases],
    )


def _call(body, args, *, semantics, comm=None, **kw):
    if comm is None:
        return pl.pallas_call(body, compiler_params=_params(*semantics), **kw)(*args)
    grid, in_specs, out_specs, out_shape = kw["grid"], kw["in_specs"], kw["out_specs"], kw["out_shape"]
    scratch = list(kw.get("scratch_shapes", ()))
    single = not isinstance(out_shape, (list, tuple))
    core_specs = [out_specs] if single else list(out_specs)
    core_shapes = [out_shape] if single else list(out_shape)
    n_in, n_out, n_scr = len(in_specs), len(core_shapes), len(scratch)
    n_cin, n_cout = len(comm.inputs), len(comm.out_shapes)
    steps = 1
    for g in grid:
        steps *= g

    def hosted(*refs):
        core_in, c_in = refs[:n_in], refs[n_in : n_in + n_cin]
        o0 = n_in + n_cin
        core_out, c_out = refs[o0 : o0 + n_out], refs[o0 + n_out : o0 + n_out + n_cout]
        s0 = o0 + n_out + n_cout
        core_scr, sems = refs[s0 : s0 + n_scr], refs[s0 + n_scr :]
        step = pl.program_id(0)
        for d in range(1, len(grid)):
            step = step * grid[d] + pl.program_id(d)

        @pl.when(step == 0)
        def _():
            comm.start(c_in, c_out, sems)

        body(*core_in, *core_out, *core_scr)

        @pl.when(step == steps - 1)
        def _():
            comm.finish(c_in, c_out, sems)

    outs = pl.pallas_call(
        hosted,
        name=kw["name"],
        grid=grid,
        in_specs=list(in_specs) + [ANY] * n_cin,
        out_specs=core_specs + [ANY] * n_cout,
        out_shape=core_shapes + comm.out_shapes,
        scratch_shapes=scratch + comm.sem_shapes(),
        input_output_aliases={n_in + i: n_out + o for i, o in comm.aliases},
        compiler_params=_params(*(["arbitrary"] * len(grid))),
    )(*args, *comm.inputs)
    return (outs[0] if single else outs[:n_out]), outs[n_out:]


def _comm_only(comm, *, name):
    n_cin, n_cout = len(comm.inputs), len(comm.out_shapes)

    def body(*refs):
        ins, outs, sems = refs[:n_cin], refs[n_cin : n_cin + n_cout], refs[n_cin + n_cout :]
        comm.start(ins, outs, sems)
        comm.finish(ins, outs, sems)

    return pl.pallas_call(
        body,
        name=name,
        in_specs=[ANY] * n_cin,
        out_specs=[ANY] * n_cout,
        out_shape=comm.out_shapes,
        scratch_shapes=comm.sem_shapes(),
    )(*comm.inputs)


def _gather_comm(shards):
    n = len(shards)
    per = 7

    def plan(ins, outs, sems):
        send, recv, local = sems
        x, y, c = _position()
        me, sibling = (x, y, c), (x, y, 1 - c)
        chips = _other_chips(x, y)

        def block(t, px, py, pc):
            return outs[t].at[pl.ds(4 * px + 2 * py + pc, 1)]

        def copy(t, k, blk, to, src=None):
            return pltpu.make_async_remote_copy(
                src_ref=block(t, *blk) if src is None else src,
                dst_ref=block(t, *blk),
                send_sem=send.at[t * per + k],
                recv_sem=recv.at[t * per + k],
                device_id=to,
                device_id_type=MESH,
            )

        mine = [pltpu.make_async_copy(ins[t], block(t, *me), local.at[t]) for t in range(n)]
        to_chips = [copy(t, 1 + j, me, (*chip, c), src=ins[t]) for t in range(n) for j, chip in enumerate(chips)]
        to_sibling = [copy(t, 0, me, sibling, src=ins[t]) for t in range(n)]
        from_chips = [copy(t, 1 + j, (*chip, c), me) for t in range(n) for j, chip in enumerate(chips)]
        passed_on = [copy(t, 4 + j, (*chip, c), sibling) for t in range(n) for j, chip in enumerate(chips)]
        from_sibling = [copy(t, 0, sibling, me) for t in range(n)]
        from_sibling += [copy(t, 4 + j, (*chip, 1 - c), me) for t in range(n) for j, chip in enumerate(chips)]
        return mine, to_chips, to_sibling, from_chips, passed_on, from_sibling

    def start(ins, outs, sems):
        mine, to_chips, to_sibling, _, _, _ = plan(ins, outs, sems)
        for cp in mine + to_chips + to_sibling:
            cp.start()

    def finish(ins, outs, sems):
        mine, to_chips, to_sibling, from_chips, passed_on, from_sibling = plan(ins, outs, sems)
        for arrived, onward in zip(from_chips, passed_on):
            arrived.wait_recv()
            onward.start()
        for cp in from_sibling:
            cp.wait_recv()
        for cp in to_chips + to_sibling + passed_on:
            cp.wait_send()
        for cp in mine:
            cp.wait()

    out_shapes = [SDS((N_DEV,) + sh.shape[1:], sh.dtype) for sh in shards]
    return _Comm(shards, out_shapes, [n * per, n * per, n], start, finish)


def _gather_stage(stage, shards=None, arrived=None):
    n = len(arrived if shards is None else shards)
    targets = {"near": (0, 1), "far": (2,), "first": (0, 1, 2), "pass": (0, 1, 2)}[stage]
    to_sibling = stage in ("near", "first")
    per = len(targets) + to_sibling

    def plan(ins, outs, sems):
        x, y, c = _position()
        me, sibling = (x, y, c), (x, y, 1 - c)
        chips = [_other_chips(x, y)[j] for j in targets]

        def block(t, px, py, pc):
            return outs[t].at[pl.ds(4 * px + 2 * py + pc, 1)]

        def copy(t, k, blk, to, src=None):
            return pltpu.make_async_remote_copy(
                src_ref=block(t, *blk) if src is None else src,
                dst_ref=block(t, *blk),
                send_sem=sems[0].at[t * per + k],
                recv_sem=sems[1].at[t * per + k],
                device_id=to,
                device_id_type=MESH,
            )

        local = []
        if stage == "pass":
            sent = [copy(t, j, (*chip, c), sibling) for t in range(n) for j, chip in enumerate(chips)]
            landing = [copy(t, j, (*chip, 1 - c), me) for t in range(n) for j, chip in enumerate(chips)]
        else:
            sent = [copy(t, j, me, (*chip, c), src=ins[t]) for t in range(n) for j, chip in enumerate(chips)]
            landing = [copy(t, j, (*chip, c), me) for t in range(n) for j, chip in enumerate(chips)]
            if to_sibling:
                local = [pltpu.make_async_copy(ins[t], block(t, *me), sems[2].at[t]) for t in range(n)]
                sent += [copy(t, per - 1, me, sibling, src=ins[t]) for t in range(n)]
                landing += [copy(t, per - 1, sibling, me) for t in range(n)]
        return local, sent, landing

    def start(ins, outs, sems):
        local, sent, _ = plan(ins, outs, sems)
        for cp in local + sent:
            cp.start()

    def finish(ins, outs, sems):
        local, sent, landing = plan(ins, outs, sems)
        for cp in landing:
            cp.wait_recv()
        for cp in sent:
            cp.wait_send()
        for cp in local:
            cp.wait()

    if to_sibling:
        out_shapes = [SDS((N_DEV,) + sh.shape[1:], sh.dtype) for sh in shards]
        return _Comm(shards, out_shapes, [n * per, n * per, n], start, finish)
    out_shapes = [SDS(a.shape, a.dtype) for a in arrived]
    if stage == "pass":
        return _Comm(arrived, out_shapes, [n * per, n * per], start, finish, aliases=[(t, t) for t in range(n)])
    return _Comm(list(shards) + list(arrived), out_shapes, [n * per, n * per], start, finish, aliases=[(n + t, t) for t in range(n)])


def _exchange_comm(arrays, out_shapes, n_copies, copies_of):
    def start(ins, outs, sems):
        for cp in copies_of(ins, outs, *sems):
            cp.start()

    def finish(ins, outs, sems):
        for cp in copies_of(ins, outs, *sems):
            cp.wait()

    return _Comm(arrays, out_shapes, [n_copies, n_copies], start, finish)


def _sibling_comm(grads):
    def copies_of(ins, outs, send, recv):
        x, y, c = _position()
        return [
            pltpu.make_async_remote_copy(
                src_ref=ins[t].at[:, pl.ds(1 - c, 1)],
                dst_ref=outs[t],
                send_sem=send.at[t],
                recv_sem=recv.at[t],
                device_id=(x, y, 1 - c),
                device_id_type=MESH,
            )
            for t in range(len(ins))
        ]

    return _exchange_comm(grads, [SDS((4, 1) + g.shape[2:], g.dtype) for g in grads], len(grads), copies_of)


def _chips_comm(parts):
    def copies_of(ins, outs, send, recv):
        x, y, c = _position()
        return [
            pltpu.make_async_remote_copy(
                src_ref=ins[t].at[pl.ds(2 * px + py, 1)],
                dst_ref=outs[t].at[pl.ds(k, 1)],
                send_sem=send.at[3 * t + k],
                recv_sem=recv.at[3 * t + k],
                device_id=(px, py, c),
                device_id_type=MESH,
            )
            for t in range(len(ins))
            for k, (px, py) in enumerate(_other_chips(x, y))
        ]

    return _exchange_comm(parts, [SDS((3,) + p.shape[1:], p.dtype) for p in parts], 3 * len(parts), copies_of)


def _sum_with_sibling(grad, got, core, *, name):
    rows = grad.shape[2]

    def body(core_ref, a_ref, b_ref, o_ref):
        o_ref[...] = (a_ref[...].astype(F32) + b_ref[...].astype(F32)).astype(o_ref.dtype)

    return pl.pallas_call(
        body,
        name=name,
        grid_spec=pltpu.PrefetchScalarGridSpec(
            num_scalar_prefetch=1,
            grid=(4,),
            in_specs=[
                pl.BlockSpec((None, None, rows, D), lambda q, core_ref: (q, core_ref[0], 0, 0)),
                pl.BlockSpec((None, None, rows, D), lambda q, core_ref: (q, 0, 0, 0)),
            ],
            out_specs=pl.BlockSpec((None, rows, D), lambda q, core_ref: (q, 0, 0)),
        ),
        out_shape=SDS((4, rows, D), grad.dtype),
        compiler_params=_params("parallel"),
    )(core, grad, got)


def _sum_chips(part, got, chip, *, name):
    rows = part.shape[1]

    def body(chip_ref, a_ref, b_ref, o_ref):
        o_ref[...] = ((a_ref[...].astype(F32) + b_ref[0].astype(F32)) + b_ref[1].astype(F32)) + b_ref[2].astype(F32)

    return pl.pallas_call(
        body,
        name=name,
        grid_spec=pltpu.PrefetchScalarGridSpec(
            num_scalar_prefetch=1,
            grid=(1,),
            in_specs=[
                pl.BlockSpec((None, rows, D), lambda i, chip_ref: (chip_ref[0], 0, 0)),
                pl.BlockSpec((3, rows, D), lambda i, chip_ref: (0, 0, 0)),
            ],
            out_specs=pl.BlockSpec((rows, D), lambda i, chip_ref: (0, 0)),
        ),
        out_shape=SDS((rows, D), F32),
        compiler_params=_params("arbitrary"),
    )(chip, part, got)


def _all_reduce_small(pack, *, name):
    rows = pack.shape[1]

    def body(in_ref, out_ref, from_sibling, part, from_chips, send, recv):
        x, y, c = _position()
        me, sibling = (x, y, c), (x, y, 1 - c)
        chips = _other_chips(x, y)
        waiting = []

        def copy(k, src, dst, to):
            return pltpu.make_async_remote_copy(
                src_ref=src, dst_ref=dst, send_sem=send.at[k], recv_sem=recv.at[k], device_id=to, device_id_type=MESH
            )

        def exchange(copies):
            for cp in copies:
                cp.start()
            for cp in copies:
                cp.wait_recv()
            waiting.extend(copies)

        def block(px, py, pc):
            return out_ref.at[4 * px + 2 * py + pc]

        exchange([copy(q, in_ref.at[2 * q + 1 - c], from_sibling.at[q], sibling) for q in range(4)])
        for q in range(4):
            part[q] = in_ref[2 * q + c] + from_sibling[q]
        exchange([copy(4 + k, part.at[2 * px + py], from_chips.at[k], (px, py, c)) for k, (px, py) in enumerate(chips)])
        out_ref[4 * x + 2 * y + c] = ((part[2 * x + y] + from_chips[0]) + from_chips[1]) + from_chips[2]
        exchange(
            [copy(7, block(*me), block(*me), sibling)]
            + [copy(8 + k, block(*me), block(*me), (px, py, c)) for k, (px, py) in enumerate(chips)]
        )
        exchange([copy(11 + k, block(px, py, c), block(px, py, c), sibling) for k, (px, py) in enumerate(chips)])
        for cp in waiting:
            cp.wait_send()

    vmem = pl.BlockSpec(memory_space=pltpu.VMEM)
    return pl.pallas_call(
        body,
        name=name,
        in_specs=[vmem],
        out_specs=vmem,
        out_shape=SDS(pack.shape, F32),
        scratch_shapes=[
            pltpu.VMEM((4, rows, D), F32),
            pltpu.VMEM((4, rows, D), F32),
            pltpu.VMEM((3, rows, D), F32),
            pltpu.SemaphoreType.DMA((14,)),
            pltpu.SemaphoreType.DMA((14,)),
        ],
        compiler_params=pltpu.CompilerParams(vmem_limit_bytes=VMEM_LIMIT_BYTES),
    )(pack)


def _pack(arrays, rows):
    flat = jnp.concatenate([a.reshape(-1).astype(F32) for a in arrays])
    return jnp.pad(flat, (0, rows * D - flat.shape[0])).reshape(rows, D)


def _unpack(pack, shapes):
    flat = pack.reshape(-1)
    out, off = [], 0
    for sh in shapes:
        size = 1
        for dim in sh:
            size *= dim
        out.append(flat[off : off + size].reshape(sh))
        off += size
    return out


def _block_diag_pairs(w):
    w = w.reshape(N_RNN_TILES, 2, HEAD_DIM, HEAD_DIM)
    z = jnp.zeros_like(w[:, 0])
    top = jnp.concatenate([w[:, 0], z], axis=2)
    bot = jnp.concatenate([z, w[:, 1]], axis=2)
    return jnp.concatenate([top, bot], axis=1)


def _diag_blocks(w2):
    a = w2[:, :HEAD_DIM, :HEAD_DIM]
    b = w2[:, HEAD_DIM:, HEAD_DIM:]
    return jnp.stack([a, b], axis=1).reshape(RNN_HEADS, HEAD_DIM, HEAD_DIM)


BIG = ("w_in", "w_branch_a", "w_branch_b", "w_out", "w_up", "w_down")
TRANSPOSED = ("w_in", "w_up")
SMALL = (
    "norm_mix_g", "conv_w", "conv_b", "lru_w_a", "lru_b_a", "lru_w_x", "lru_b_x", "lru_lambda",
    "sgu_ln_g", "sgu_ln_b", "sgu_w_s", "sgu_b_s", "norm_ffn_g", "final_norm_g",
)
WEIGHTS = (
    "norm_mix_g", "w_in", "conv_w", "conv_b", "lru_w_a", "lru_b_a", "lru_w_x", "lru_b_x", "lru_lambda", "sgu_ln_g",
    "sgu_ln_b", "sgu_w_s", "sgu_b_s", "w_branch_a", "w_branch_b", "w_out", "norm_ffn_g", "w_up", "w_down", "final_norm_g",
)

TM = 512
TM_NT = 1024
TN_IN = 3328
TN_UP = 4096
TN_DOWN_BWD = 2048
TKA = 512
TKA_PIECES = 256
TC = 512
TC_BWD = 1024
TB = 256
TB_BWD = 512
TR = 256


_BRANCHES_0 = [(0, "w_branch_a"), (0, "w_branch_b"), (0, "w_out")]
_BRANCHES_1 = [(1, "w_branch_a"), (1, "w_branch_b"), (1, "w_out")]
GATHERS_RIDING = (
    {
        "in_proj": [("first", _BRANCHES_0), ("near", [(0, "w_up")])],
        "branch_a_fwd": [("pass", _BRANCHES_0), ("far", [(0, "w_up")]), ("near", [(1, "w_in")])],
        "sgu_fwd": [("pass", [(0, "w_up")]), ("near", [(0, "w_down")])],
        "merge_fwd": [("far", [(0, "w_down")])],
        "ffn_up": [("pass", [(0, "w_down")]), ("far", [(1, "w_in")])],
        "ffn_down": [("pass", [(1, "w_in")]), ("near", _BRANCHES_1)],
    },
    {
        "in_proj": [("far", _BRANCHES_1), ("near", [(1, "w_down")])],
        "branch_a_fwd": [("pass", _BRANCHES_1), ("far", [(1, "w_down")]), ("first", [(1, "w_up")])],
        "sgu_fwd": [("pass", [(1, "w_down"), (1, "w_up")])],
    },
)


def _layer_forward(l, x, p, w, shards, arriving, loss_head=None):
    def run(key, fn, *args, **kw):
        riding = GATHERS_RIDING[l].get(key, ())
        if not riding:
            return fn(*args, **kw)
        comms = []
        for stage, units in riding:
            mine = [shards[l2][n2] for l2, n2 in units] if stage != "pass" else None
            left = [arriving.pop(unit) for unit in units] if stage in ("far", "pass") else None
            comms.append(_gather_stage(stage, shards=mine, arrived=left))
        out, got = fn(*args, comm=_merge_comms(comms), **kw)
        got = list(got)
        for stage, units in riding:
            for l2, n2 in units:
                if stage == "pass":
                    w[l2][n2] = got.pop(0).reshape(-1, D)
                else:
                    arriving[l2, n2] = got.pop(0)
        return out

    proj, h = run("in_proj", _norm_matmul_nt, x, p["norm_mix_g"], w[l]["w_in"], tm=TM_NT, tn=TN_IN, name=f"in_proj_{l}")
    hseq, ya_pre = run(
        "branch_a_fwd", _branch_a_fwd, proj, p["conv_w"], p["conv_b"], p["wa2"], p["lru_b_a"], p["wx2"], p["lru_b_x"],
        p["lru_lambda"], tc=TC, name=f"branch_a_fwd_{l}",
    )
    yb_pre = run("sgu_fwd", _sgu_fwd, proj, p["sgu_ln_g"], p["sgu_ln_b"], p["wm"], p["sgu_bias"], tb=TB, name=f"sgu_fwd_{l}")
    x1, ya, yb = run(
        "merge_fwd", _merge_fwd, ya_pre, yb_pre, proj, x, w[l]["w_branch_a"], w[l]["w_branch_b"], w[l]["w_out"], tm=TM,
        name=f"merge_fwd_{l}",
    )
    f_pre, h2 = run("ffn_up", _norm_matmul_nt, x1, p["norm_ffn_g"], w[l]["w_up"], tm=TM_NT, tn=TN_UP, name=f"ffn_up_{l}")
    saved = dict(x=x, h=h, proj=proj, hseq=hseq, ya_pre=ya_pre, yb_pre=yb_pre, ya=ya, yb=yb, x1=x1, h2=h2, f_pre=f_pre)
    if loss_head is None:
        return run("ffn_down", _matmul_nn_res, f_pre, w[l]["w_down"], x1, relu2=True, tm=TM, name=f"ffn_down_{l}"), saved
    return _ffn_down_loss(f_pre, w[l]["w_down"], x1, *loss_head, tm=TM, name=f"ffn_down_loss_{l}"), saved


def _layer_backward(l, dx2, dx2b, sv, p, w, core, waiting, last):
    parts, from_chips = {}, {}

    def by_device(g):
        return g.reshape(4, 2, -1, D)

    def with_sibling(name, g, got):
        parts[name] = _sum_with_sibling(by_device(g), got, core, name=f"sum_sibling_{name}_{l}")

    df_pre = _matmul_nt_drelu2(dx2b, w["w_down"], sv["f_pre"], tm=TM_NT, tn=TN_DOWN_BWD, name=f"ffn_down_bwd_{l}")
    g_down = _matmul_tn([sv["f_pre"]], dx2b, relu2=True, tka=TKA, name=f"grad_w_down_{l}")
    g_up, (got,) = _matmul_tn(
        [df_pre], sv["h2"], relu2=False, tka=TKA, name=f"grad_w_up_{l}", comm=_sibling_comm([by_device(g_down)])
    )
    with_sibling("w_down", g_down, got)
    (dx1, dx1b, g_norm_ffn), (got,) = _matmul_nn_rmsnorm_bwd(
        [df_pre], w["w_up"], sv["x1"], p["norm_ffn_g"], dx2, tm=TM, name=f"ffn_up_bwd_{l}",
        comm=_sibling_comm([by_device(g_up)]),
    )
    with_sibling("w_up", g_up, got)
    (merged, dya, dyb, dga, dgb, dya_pre, dyb_pre), (from_chips[l, "w_up"],) = _merge_bwd(
        dx1b, sv["ya"], sv["yb"], sv["proj"], w["w_branch_a"], w["w_branch_b"], w["w_out"], tm=TM, name=f"merge_bwd_{l}",
        comm=_chips_comm([parts["w_up"]]),
    )
    g_out, g_ba, g_bb = _matmuls_tn(
        [(merged, dx1b), (sv["ya_pre"], dya), (sv["yb_pre"], dyb)], ts=2 * TM, name=f"grad_w_branches_{l}"
    )
    branch = (("w_out", g_out), ("w_branch_a", g_ba), ("w_branch_b", g_bb))
    (du, dv, g_ws, g_bs, g_lng, g_lnb), got = _sgu_bwd(
        dyb_pre, sv["proj"], p["sgu_ln_g"], p["sgu_ln_b"], p["wm"], p["wmt"], p["sgu_bias"], p["mask"], tb=TB_BWD,
        name=f"sgu_bwd_{l}",
        comm=_merge_comms([_sibling_comm([by_device(g) for _, g in branch]), _chips_comm([parts["w_down"]])]),
    )
    from_chips[l, "w_down"] = got[-1]
    for (name, g), landed in zip(branch, got):
        with_sibling(name, g, landed)
    riding = [((l, name), parts[name]) for name, _ in branch] + list(waiting)
    (dxr, dgr, g_cw, g_cb, g_ba_, g_bx, g_lam, g_wa2, g_wx2), got = _branch_a_bwd(
        dya_pre, sv["proj"], sv["hseq"], p["conv_w"], p["conv_b"], p["wa2"], p["lru_b_a"], p["wx2"], p["lru_b_x"],
        p["lru_lambda"], p["wa2t"], p["wx2t"], tc=TC_BWD, name=f"branch_a_bwd_{l}", comm=_chips_comm([part for _, part in riding]),
    )
    for (key, _), landed in zip(riding, got):
        from_chips[key] = landed
    dproj = [dxr, dgr, du, dv, dga, dgb]
    g_in = _matmul_tn(dproj, sv["h"], relu2=False, tka=TKA_PIECES, name=f"grad_w_in_{l}")
    if last:
        (got,) = _comm_only(_sibling_comm([by_device(g_in)]), name=f"grad_w_in_to_sibling_{l}")
        with_sibling("w_in", g_in, got)
        riding = _chips_comm([parts["w_in"]])
    else:
        riding = _sibling_comm([by_device(g_in)])
    (dx, dxb, g_norm_mix), (got,) = _matmul_nn_rmsnorm_bwd(
        dproj, w["w_in"], sv["x"], p["norm_mix_g"], dx1, tm=TM, name=f"in_proj_bwd_{l}", comm=riding
    )
    if last:
        from_chips[l, "w_in"] = got
    else:
        with_sibling("w_in", g_in, got)
    small = dict(
        norm_mix_g=g_norm_mix[0], conv_w=g_cw, conv_b=g_cb[0], lru_w_a=_diag_blocks(g_wa2), lru_b_a=g_ba_.reshape(RNN_HEADS, HEAD_DIM),
        lru_w_x=_diag_blocks(g_wx2), lru_b_x=g_bx.reshape(RNN_HEADS, HEAD_DIM), lru_lambda=g_lam[0], sgu_ln_g=g_lng[0],
        sgu_ln_b=g_lnb[0], sgu_w_s=g_ws, sgu_b_s=g_bs[:, :, 0], norm_ffn_g=g_norm_ffn[0],
    )
    return dx, dxb, small, parts, from_chips


def _prepare_small(l, given):
    chunk_id = jnp.arange(SGU_BLOCK) // CHUNK
    mask = (chunk_id[:, None] >= chunk_id[None, :]).astype(F32)
    wm = given["sgu_w_s"][l] * mask
    wa2 = _block_diag_pairs(given["lru_w_a"][l])
    wx2 = _block_diag_pairs(given["lru_w_x"][l])
    row = lambda a: a.reshape(1, -1)
    return dict(
        norm_mix_g=row(given["norm_mix_g"][l]),
        norm_ffn_g=row(given["norm_ffn_g"][l]),
        conv_w=given["conv_w_full"][l],
        conv_b=row(given["conv_b"][l]),
        wa2=wa2.astype(BF16),
        wx2=wx2.astype(BF16),
        wa2t=jnp.swapaxes(wa2, 1, 2).astype(BF16),
        wx2t=jnp.swapaxes(wx2, 1, 2).astype(BF16),
        lru_b_a=row(given["lru_b_a"][l]),
        lru_b_x=row(given["lru_b_x"][l]),
        lru_lambda=row(given["lru_lambda"][l]),
        sgu_ln_g=row(given["sgu_ln_g"][l]),
        sgu_ln_b=row(given["sgu_ln_b"][l]),
        wm=wm.astype(BF16),
        wmt=jnp.swapaxes(wm, 1, 2).astype(BF16),
        sgu_bias=jnp.broadcast_to(given["sgu_b_s"][l][:, :, None], (SGU_GROUPS, SGU_BLOCK, LANES)),
        mask=mask,
    )


def _step(given):
    x_idx, y_idx, c_idx = _position()
    dev = 4 * x_idx + 2 * y_idx + c_idx
    core = c_idx.astype(jnp.int32).reshape(1)
    chip = (2 * x_idx + y_idx).astype(jnp.int32).reshape(1)

    def rows_first(name, a):
        return jnp.swapaxes(a, 1, 2) if name in TRANSPOSED else a

    shards = []
    for l in range(DEPTH):
        shards.append({name: rows_first(name, given[name])[l].astype(BF16)[None] for name in BIG})
    conv_mine = given["conv_w"].reshape(1, DEPTH * CONV_WIDTH, D_RNN // N_DEV)
    w_in_first, conv_all = _comm_only(_gather_comm([shards[0]["w_in"], conv_mine]), name="gather_first")
    weights = [{"w_in": w_in_first.reshape(-1, D)}, {}]
    conv_all = conv_all.reshape(N_DEV, DEPTH, CONV_WIDTH, D_RNN // N_DEV)
    given = dict(given, conv_w_full=jnp.moveaxis(conv_all, 0, 2).reshape(DEPTH, CONV_WIDTH, D_RNN))

    small_params = [_prepare_small(l, given) for l in range(DEPTH)]
    x = given["x"][0]
    saved, arriving = [], {}
    loss_head = (given["final_norm_g"].reshape(1, D), given["loss_target"][0])
    for l in range(DEPTH):
        x, sv = _layer_forward(
            l, x, small_params[l], weights, shards, arriving, loss_head=loss_head if l == DEPTH - 1 else None
        )
        saved.append(sv)
    dx, dxb, g_final, loss = x
    small_grads, parts, from_chips, waiting = [None] * DEPTH, [None] * DEPTH, {}, []
    for l in reversed(range(DEPTH)):
        dx, dxb, small_grads[l], parts[l], got = _layer_backward(
            l, dx, dxb, saved[l], small_params[l], weights[l], core, waiting, last=l == 0
        )
        from_chips.update(got)
        waiting = [((l, "w_in"), parts[l]["w_in"])]

    small_list = []
    for name in SMALL[:-1]:
        small_list.append(jnp.stack([small_grads[l][name] for l in range(DEPTH)]))
    small_list += [g_final[0], loss[0, :1]]
    small_shapes = [a.shape for a in small_list]
    pack = _pack(small_list, SMALL_ROWS).reshape(N_DEV, SMALL_ROWS_PER_DEV, D)
    summed = _unpack(_all_reduce_small(pack, name="all_reduce_small"), small_shapes)
    loss_total = summed[-1][0]
    grads = dict(zip(SMALL, summed[:-1]))
    cw = grads["conv_w"].reshape(DEPTH, CONV_WIDTH, N_DEV, D_RNN // N_DEV)
    grads["conv_w"] = lax.dynamic_index_in_dim(cw, dev, axis=2, keepdims=False)

    delta, new_m, new_v = {}, {}, {}
    for name in BIG:
        w, m, v = given[name], given["m_" + name], given["v_" + name]
        mine = [parts[l][name] for l in range(DEPTH)]
        theirs = [from_chips[l, name] for l in range(DEPTH)]
        if name == "w_up":
            sums = [_sum_chips(mine[l], theirs[l], chip, name=f"sum_chips_{name}_{l}").T for l in range(DEPTH)]
            out = _adamw_layers(w, sums, m, v, tr=TR, name=f"adamw_{name}")
        else:
            out = _adamw_reduced(
                rows_first(name, w), mine, theirs, rows_first(name, m), rows_first(name, v), chip, tr=TR, name=f"adamw_{name}"
            )
            out = [rows_first(name, a) for a in out]
        grads[name], delta[name], new_m[name], new_v[name] = out
    two_d = lambda a: a.reshape(1, -1) if a.ndim == 1 else a
    groups = [tuple(two_d(a) for a in (given[n], grads[n], given["m_" + n], given["v_" + n])) for n in SMALL]
    for n, (d, m2, v2) in zip(SMALL, _adamw_small(groups, name="adamw_small")):
        shape = given[n].shape
        delta[n], new_m[n], new_v[n] = d.reshape(shape), m2.reshape(shape), v2.reshape(shape)

    return (
        loss_total, dx[None],
        *[grads[n] for n in WEIGHTS], *[delta[n] for n in WEIGHTS], *[new_m[n] for n in WEIGHTS], *[new_v[n] for n in WEIGHTS],
    )


def kernel(x, norm_mix_g, w_in, conv_w, conv_b, lru_w_a, lru_b_a, lru_w_x, lru_b_x, lru_lambda, sgu_ln_g, sgu_ln_b, sgu_w_s, sgu_b_s, w_branch_a, w_branch_b, w_out, norm_ffn_g, w_up, w_down, final_norm_g, loss_target, m_norm_mix_g, m_w_in, m_conv_w, m_conv_b, m_lru_w_a, m_lru_b_a, m_lru_w_x, m_lru_b_x, m_lru_lambda, m_sgu_ln_g, m_sgu_ln_b, m_sgu_w_s, m_sgu_b_s, m_w_branch_a, m_w_branch_b, m_w_out, m_norm_ffn_g, m_w_up, m_w_down, m_final_norm_g, v_norm_mix_g, v_w_in, v_conv_w, v_conv_b, v_lru_w_a, v_lru_b_a, v_lru_w_x, v_lru_b_x, v_lru_lambda, v_sgu_ln_g, v_sgu_ln_b, v_sgu_w_s, v_sgu_b_s, v_w_branch_a, v_w_branch_b, v_w_out, v_norm_ffn_g, v_w_up, v_w_down, v_final_norm_g):
    return _step(dict(locals()))
```

```python
import jax
import jax.numpy as jnp
from jax import lax
from jax.experimental import pallas as pl
from jax.experimental.pallas import tpu as pltpu

F32 = jnp.float32
BF16 = jnp.bfloat16
SDS = jax.ShapeDtypeStruct
MESH = pl.DeviceIdType.MESH

D = 1024
D_RNN = 1280
D_SGU = 1024
D_IN = 2 * D_RNN + 2 * D_SGU + 2 * D
DEPTH = 2
RNN_HEADS = 20
HEAD_DIM = 64
CONV_WIDTH = 4
LRU_C = 8.0
SGU_GROUPS = 8
SGU_BLOCK = 128
CHUNK = 64
EPS = 1e-6
N_DEV = 8

ADAM_LR = 0.001
ADAM_B1 = 0.9
ADAM_B2 = 0.999
ADAM_EPS = 1e-08
ADAM_WD = 0.01
ADAM_STEP = 10

LANES = 128
SUBLANES = 8
VMEM_LIMIT_BYTES = 56 * 1024 * 1024

N_RNN_TILES = D_RNN // LANES
RNN_TILES_PER_STEP = 5
U_BLK512 = (2 * D_RNN) // 512
V_BLK512 = (2 * D_RNN + D_SGU) // 512
GA_BLK512 = (2 * D_RNN + 2 * D_SGU) // 512
GB_BLK512 = (2 * D_RNN + 2 * D_SGU + D) // 512

SMALL_ROWS_PER_DEV = 80
SMALL_ROWS = N_DEV * SMALL_ROWS_PER_DEV


def _params(*sem):
    return pltpu.CompilerParams(dimension_semantics=sem, vmem_limit_bytes=VMEM_LIMIT_BYTES)


def _sigmoid(x):
    return 0.5 + 0.5 * jnp.tanh(0.5 * x)


_GELU_C = 0.7978845608028654
_GELU_K = 0.044715


def _gelu(x):
    t = jnp.tanh(_GELU_C * (x + _GELU_K * x * x * x))
    return 0.5 * x * (1.0 + t)


def _gelu_and_grad(x):
    t = jnp.tanh(_GELU_C * (x + _GELU_K * x * x * x))
    val = 0.5 * x * (1.0 + t)
    grad = 0.5 * (1.0 + t) + 0.5 * x * (1.0 - t * t) * _GELU_C * (1.0 + 3.0 * _GELU_K * x * x)
    return val, grad


def _one_minus_square(log_a, a):
    return -jnp.tanh(log_a) * (1.0 + a * a)


def _dot(a, b):
    return jnp.dot(a, b, preferred_element_type=F32)


def _dot_nt(a, b):
    return lax.dot_general(a, b, (((1,), (1,)), ((), ())), preferred_element_type=F32)


def _dot_tn(a, b):
    return lax.dot_general(a, b, (((0,), (0,)), ((), ())), preferred_element_type=F32)


def _norm_matmul_nt(x, g, w, *, tm, tn, name, comm=None):
    s, n = x.shape[0], w.shape[0]
    tm, tn = min(tm, s), min(tn, n)

    def body(x_ref, g_ref, w_ref, o_ref, h_ref):
        @pl.when(pl.program_id(1) == 0)
        def _():
            xv = x_ref[...]
            r = lax.rsqrt(jnp.mean(xv * xv, axis=-1, keepdims=True) + EPS)
            h_ref[...] = (xv * r * g_ref[...]).astype(BF16)

        o_ref[...] = _dot_nt(h_ref[...], w_ref[...]).astype(o_ref.dtype)

    return _call(
        body,
        (x, g, w),
        name=name,
        grid=(s // tm, n // tn),
        in_specs=[
            pl.BlockSpec((tm, D), lambda i, j: (i, 0)),
            pl.BlockSpec((1, D), lambda i, j: (0, 0)),
            pl.BlockSpec((tn, D), lambda i, j: (j, 0)),
        ],
        out_specs=[pl.BlockSpec((tm, tn), lambda i, j: (i, j)), pl.BlockSpec((tm, D), lambda i, j: (i, 0))],
        out_shape=[SDS((s, n), BF16), SDS((s, D), BF16)],
        semantics=("parallel", "arbitrary"),
        comm=comm,
    )


def _matmul_nn_res(a, w, res, *, relu2, tm, name, comm=None):
    s, k = a.shape
    tm = min(tm, s)

    def body(a_ref, w_ref, r_ref, o_ref):
        av = a_ref[...]
        if relu2:
            t = jnp.maximum(av.astype(F32), 0.0)
            av = (t * t).astype(BF16)
        o_ref[...] = r_ref[...] + _dot(av, w_ref[...])

    return _call(
        body,
        (a, w, res),
        name=name,
        grid=(s // tm,),
        in_specs=[
            pl.BlockSpec((tm, k), lambda i: (i, 0)),
            pl.BlockSpec((k, D), lambda i: (0, 0)),
            pl.BlockSpec((tm, D), lambda i: (i, 0)),
        ],
        out_specs=pl.BlockSpec((tm, D), lambda i: (i, 0)),
        out_shape=SDS((s, D), F32),
        semantics=("parallel",),
        comm=comm,
    )


def _matmul_nt_drelu2(a, w, pre, *, tm, tn, name):
    s, n = a.shape[0], w.shape[0]
    tm, tn = min(tm, s), min(tn, n)

    def body(a_ref, w_ref, p_ref, o_ref):
        d = _dot_nt(a_ref[...], w_ref[...])
        o_ref[...] = (d * (2.0 * jnp.maximum(p_ref[...].astype(F32), 0.0))).astype(o_ref.dtype)

    return pl.pallas_call(
        body,
        name=name,
        grid=(s // tm, n // tn),
        in_specs=[
            pl.BlockSpec((tm, D), lambda i, j: (i, 0)),
            pl.BlockSpec((tn, D), lambda i, j: (j, 0)),
            pl.BlockSpec((tm, tn), lambda i, j: (i, j)),
        ],
        out_specs=pl.BlockSpec((tm, tn), lambda i, j: (i, j)),
        out_shape=SDS((s, n), BF16),
        compiler_params=_params("parallel", "arbitrary"),
    )(a, w, pre)


def _matmul_tn(a_list, b, *, relu2, tka, name, comm=None):
    s = b.shape[0]
    n = len(a_list)
    nblk = [a.shape[1] // tka for a in a_list]
    starts = [sum(nblk[:p]) for p in range(n)]

    def body(*refs):
        a_refs, b_ref, o_ref = refs[:n], refs[n], refs[n + 1]
        i = pl.program_id(0)
        for p in range(n):

            @pl.when((i >= starts[p]) & (i < starts[p] + nblk[p]))
            def _(p=p):
                av = a_refs[p][...]
                if relu2:
                    t = jnp.maximum(av.astype(F32), 0.0)
                    av = (t * t).astype(BF16)
                o_ref[...] = _dot_tn(av, b_ref[...]).astype(o_ref.dtype)

    def piece_spec(p):
        return pl.BlockSpec((s, tka), lambda i: (0, jnp.clip(i - starts[p], 0, nblk[p] - 1)))

    return _call(
        body,
        (*a_list, b),
        name=name,
        grid=(sum(nblk),),
        in_specs=[piece_spec(p) for p in range(n)] + [pl.BlockSpec((s, D), lambda i: (0, 0))],
        out_specs=pl.BlockSpec((tka, D), lambda i: (i, 0)),
        out_shape=SDS((sum(nblk) * tka, D), BF16),
        semantics=("parallel",),
        comm=comm,
    )


def _matmuls_tn(pairs, *, ts, name):
    s = pairs[0][0].shape[0]
    ts = min(ts, s)
    n = len(pairs)
    steps = s // ts

    def body(*refs):
        ins, outs, accs = refs[: 2 * n], refs[2 * n : 3 * n], refs[3 * n :]
        for p in range(n):
            part = _dot_tn(ins[2 * p][...], ins[2 * p + 1][...])

            @pl.when(pl.program_id(0) == 0)
            def _(p=p, part=part):
                accs[p][...] = part

            @pl.when(pl.program_id(0) > 0)
            def _(p=p, part=part):
                accs[p][...] += part

        @pl.when(pl.program_id(0) == steps - 1)
        def _():
            for p in range(n):
                outs[p][...] = accs[p][...].astype(BF16)

    widths = [a.shape[1] for a, _ in pairs]
    in_specs = []
    for wd in widths:
        in_specs += [pl.BlockSpec((ts, wd), lambda i: (i, 0)), pl.BlockSpec((ts, D), lambda i: (i, 0))]
    return pl.pallas_call(
        body,
        name=name,
        grid=(steps,),
        in_specs=in_specs,
        out_specs=[pl.BlockSpec((wd, D), lambda i: (0, 0)) for wd in widths],
        out_shape=[SDS((wd, D), BF16) for wd in widths],
        scratch_shapes=[pltpu.VMEM((wd, D), F32) for wd in widths],
        compiler_params=_params("arbitrary"),
    )(*[x for pair in pairs for x in pair])


def _matmul_nn_rmsnorm_bwd(a_list, w, x, g, res, *, tm, name, comm=None):
    s = x.shape[0]
    tm = min(tm, s)
    n = len(a_list)
    widths = [a.shape[1] for a in a_list]
    offs = [sum(widths[:p]) for p in range(n)]
    k = sum(widths)

    def body(*refs):
        a_refs = refs[:n]
        w_ref, x_ref, g_ref, r_ref, dx_ref, dxb_ref, dg_ref = refs[n:]

        @pl.when(pl.program_id(0) == 0)
        def _():
            dg_ref[...] = jnp.zeros_like(dg_ref)

        dh = _dot(a_refs[0][...], w_ref[0 : widths[0], :])
        for p in range(1, n):
            dh += _dot(a_refs[p][...], w_ref[offs[p] : offs[p] + widths[p], :])
        xv = x_ref[...]
        r = lax.rsqrt(jnp.mean(xv * xv, axis=-1, keepdims=True) + EPS)
        xhat = xv * r
        dxh = dh * g_ref[...]
        dx = r_ref[...] + r * (dxh - xhat * jnp.mean(dxh * xhat, axis=-1, keepdims=True))
        dx_ref[...] = dx
        dxb_ref[...] = dx.astype(BF16)
        dg_ref[...] += jnp.sum(dh * xhat, axis=0, keepdims=True)

    act = pl.BlockSpec((tm, D), lambda i: (i, 0))
    vec = pl.BlockSpec((1, D), lambda i: (0, 0))
    return _call(
        body,
        (*a_list, w, x, g, res),
        name=name,
        grid=(s // tm,),
        in_specs=[pl.BlockSpec((tm, wd), lambda i: (i, 0)) for wd in widths]
        + [pl.BlockSpec((k, D), lambda i: (0, 0), pipeline_mode=pl.Buffered(1)), act, vec, act],
        out_specs=[act, act, vec],
        out_shape=[SDS((s, D), F32), SDS((s, D), BF16), SDS((1, D), F32)],
        semantics=("arbitrary",),
        comm=comm,
    )


def _rows_after(ext, k, n):
    return pltpu.roll(ext, n + SUBLANES - k, 0)[:n, :]


def _scan_forward(a, b, n):
    row = lax.broadcasted_iota(jnp.int32, a.shape, 0)
    d = 1
    while d < n:
        if d < SUBLANES:
            m = row >= d
            a_s = jnp.where(m, pltpu.roll(a, d, 0), 1.0)
            b_s = jnp.where(m, pltpu.roll(b, d, 0), 0.0)
            b = a * b_s + b
            a = a * a_s
        else:
            b = jnp.concatenate([b[:d], a[d:] * b[: n - d] + b[d:]], axis=0)
            a = jnp.concatenate([a[:d], a[d:] * a[: n - d]], axis=0)
        d *= 2
    return a, b


def _scan_backward(a, b, n):
    row = lax.broadcasted_iota(jnp.int32, a.shape, 0)
    d = 1
    while d < n:
        if d < SUBLANES:
            m = row < n - d
            a_s = jnp.where(m, pltpu.roll(a, n - d, 0), 1.0)
            b_s = jnp.where(m, pltpu.roll(b, n - d, 0), 0.0)
            b = a * b_s + b
            a = a * a_s
        else:
            b = jnp.concatenate([a[: n - d] * b[d:] + b[: n - d], b[n - d :]], axis=0)
            a = jnp.concatenate([a[: n - d] * a[d:], a[n - d :]], axis=0)
        d *= 2
    return b


def _repeat_matrix(n):
    groups = n // SUBLANES
    return (jnp.arange(n)[:, None] // SUBLANES == jnp.arange(3 * groups)[None, :] % groups).astype(BF16)


def _scan_rows(a, b, n, repeat_ref, a_scr, b_scr, reverse):
    groups = n // SUBLANES
    a3 = a.reshape(groups, SUBLANES, LANES)
    b3 = b.reshape(groups, SUBLANES, LANES)
    sub = lax.broadcasted_iota(jnp.int32, a3.shape, 1)
    for d in (1, 2, 4):
        m = (sub < SUBLANES - d) if reverse else (sub >= d)
        shift = SUBLANES - d if reverse else d
        a_s = jnp.where(m, pltpu.roll(a3, shift, 1), 1.0)
        b_s = jnp.where(m, pltpu.roll(b3, shift, 1), 0.0)
        b3 = a3 * b_s + b3
        a3 = a3 * a_s
    a_scr[...] = a3.reshape(n, LANES)
    b_scr[...] = b3.reshape(n, LANES)
    edge = 0 if reverse else SUBLANES - 1
    a_tot = a_scr[pl.ds(edge, groups, stride=SUBLANES), :]
    b_tot = b_scr[pl.ds(edge, groups, stride=SUBLANES), :]
    row = lax.broadcasted_iota(jnp.int32, a_tot.shape, 0)
    if reverse:
        through = _scan_backward(a_tot, b_tot, groups)
        entering = jnp.where(row < groups - 1, pltpu.roll(through, groups - 1, 0), 0.0)
    else:
        _, through = _scan_forward(a_tot, b_tot, groups)
        entering = jnp.where(row >= 1, pltpu.roll(through, 1, 0), 0.0)
    hi = entering.astype(BF16)
    rest = entering - hi.astype(F32)
    mid = rest.astype(BF16)
    lo = (rest - mid.astype(F32)).astype(BF16)
    repeated = _dot(repeat_ref[...], jnp.concatenate([hi, mid, lo], axis=0))
    return b_scr[...] + a_scr[...] * repeated


def _softplus_neg(lam):
    z = -lam
    return jnp.maximum(z, 0.0) + jnp.log1p(jnp.exp(-jnp.abs(z)))


def _conv_and_gates(xc, xprev, cw_ref, cb_ref, wa_ref, ba_ref, wx_ref, bx_ref, lam_ref, ext_scr):
    n = xc.shape[0]
    ext_scr[:SUBLANES, :] = xprev
    ext_scr[SUBLANES:, :] = xc
    x1, x2, x3 = (ext_scr[pl.ds(SUBLANES - k, n), :] for k in (1, 2, 3))
    xr = cb_ref[...] + x3 * cw_ref[0:1, :] + x2 * cw_ref[1:2, :] + x1 * cw_ref[2:3, :] + xc * cw_ref[3:4, :]
    xrb = xr.astype(BF16)
    r = _sigmoid(_dot(xrb, wa_ref[...]) + ba_ref[...])
    i = _sigmoid(_dot(xrb, wx_ref[...]) + bx_ref[...])
    sp = _softplus_neg(lam_ref[...])
    log_a = (-LRU_C * r) * sp
    a = jnp.exp(log_a)
    return xr, (x1, x2, x3), r, i, a, _one_minus_square(log_a, a)


def _branch_a_fwd(proj, cw, cb, wa2, ba, wx2, bx, lam, *, tc, name, comm=None):
    s = proj.shape[0]
    tc = min(tc, s)

    def body(x_ref, g_ref, cw_ref, cb_ref, wa_ref, ba_ref, wx_ref, bx_ref, lam_ref, rep_ref, h_ref, y_ref,
             xprev, hlast, a_scr, b_scr, ext_scr):
        @pl.when(pl.program_id(1) == 0)
        def _():
            xprev[...] = jnp.zeros_like(xprev)
            hlast[...] = jnp.zeros_like(hlast)

        for t in range(RNN_TILES_PER_STEP):
            cols = lambda ref: ref.at[:, pl.ds(t * LANES, LANES)]
            one_tile(
                cols(x_ref), cols(g_ref), cols(cw_ref), cols(cb_ref), wa_ref.at[t], cols(ba_ref), wx_ref.at[t], cols(bx_ref),
                cols(lam_ref), rep_ref, cols(h_ref), cols(y_ref), cols(xprev), cols(hlast), a_scr.at[t], b_scr.at[t],
                ext_scr.at[t],
            )

    def one_tile(x_ref, g_ref, cw_ref, cb_ref, wa_ref, ba_ref, wx_ref, bx_ref, lam_ref, rep_ref, h_ref, y_ref,
                 xprev, hlast, a_scr, b_scr, ext_scr):
        xc = x_ref[...].astype(F32)
        xr, _, r, i, a, om = _conv_and_gates(
            xc, xprev[...], cw_ref, cb_ref, wa_ref, ba_ref, wx_ref, bx_ref, lam_ref, ext_scr
        )
        xprev[...] = xc[tc - SUBLANES :, :]
        u = jnp.sqrt(om) * (i * xr)
        row8 = lax.broadcasted_iota(jnp.int32, (SUBLANES, LANES), 0)
        first = u[:SUBLANES] + jnp.where(row8 == 0, a[:SUBLANES] * hlast[SUBLANES - 1 : SUBLANES, :], 0.0)
        h = _scan_rows(a, jnp.concatenate([first, u[SUBLANES:]], axis=0), tc, rep_ref, a_scr, b_scr, reverse=False)
        hlast[...] = h[tc - SUBLANES :, :]
        h_ref[...] = h
        y_ref[...] = (h * _gelu(g_ref[...].astype(F32))).astype(BF16)

    wide = RNN_TILES_PER_STEP * LANES
    tile = lambda j, c: (0, j)
    vec = pl.BlockSpec((1, wide), tile)
    mats = pl.BlockSpec((RNN_TILES_PER_STEP, LANES, LANES), lambda j, c: (j, 0, 0))
    repeat = _repeat_matrix(tc)
    return _call(
        body,
        (proj, proj, cw, cb, wa2, ba, wx2, bx, lam, repeat),
        name=name,
        grid=(N_RNN_TILES // RNN_TILES_PER_STEP, s // tc),
        in_specs=[
            pl.BlockSpec((tc, wide), lambda j, c: (c, j)),
            pl.BlockSpec((tc, wide), lambda j, c: (c, D_RNN // wide + j)),
            pl.BlockSpec((CONV_WIDTH, wide), tile),
            vec,
            mats,
            vec,
            mats,
            vec,
            vec,
            pl.BlockSpec(repeat.shape, lambda j, c: (0, 0)),
        ],
        out_specs=[pl.BlockSpec((tc, wide), lambda j, c: (c, j)), pl.BlockSpec((tc, wide), lambda j, c: (c, j))],
        out_shape=[SDS((s, D_RNN), F32), SDS((s, D_RNN), BF16)],
        scratch_shapes=[pltpu.VMEM((SUBLANES, wide), F32)] * 2
        + [pltpu.VMEM((RNN_TILES_PER_STEP, tc, LANES), F32)] * 2
        + [pltpu.VMEM((RNN_TILES_PER_STEP, tc + SUBLANES, LANES), F32)],
        semantics=("parallel", "arbitrary"),
        comm=comm,
    )


def _branch_a_bwd(dy, proj, h, cw, cb, wa2, ba, wx2, bx, lam, wa2t, wx2t, *, tc, name, comm=None):
    s = proj.shape[0]
    tc = min(tc, s)
    nc = s // tc
    halo16 = tc // 16
    halo8 = tc // SUBLANES

    def body(dy_ref, x_ref, xh_ref, g_ref, h_ref, hh_ref, cw_ref, cb_ref, wa_ref, ba_ref, wx_ref, bx_ref, lam_ref,
             wat_ref, wxt_ref, rep_ref, dx_ref, dg_ref, dcw_ref, dcb_ref, dba_ref, dbx_ref, dlam_ref, dwa_ref, dwx_ref,
             carry, dxr_next, a_scr, b_scr, ext_scr):
        cc = pl.program_id(1)
        ct = nc - 1 - cc

        @pl.when(cc == 0)
        def _():
            carry[...] = jnp.zeros_like(carry)
            dxr_next[...] = jnp.zeros_like(dxr_next)
            for ref in (dcw_ref, dcb_ref, dba_ref, dbx_ref, dlam_ref, dwa_ref, dwx_ref):
                ref[...] = jnp.zeros_like(ref)

        for t in range(RNN_TILES_PER_STEP):
            cols = lambda ref: ref.at[:, pl.ds(t * LANES, LANES)]
            one_tile(
                ct, cols(dy_ref), cols(x_ref), cols(xh_ref), cols(g_ref), cols(h_ref), cols(hh_ref), cols(cw_ref), cols(cb_ref),
                wa_ref.at[t], cols(ba_ref), wx_ref.at[t], cols(bx_ref), cols(lam_ref), wat_ref.at[t], wxt_ref.at[t], rep_ref,
                cols(dx_ref), cols(dg_ref), cols(dcw_ref), cols(dcb_ref), cols(dba_ref), cols(dbx_ref), cols(dlam_ref),
                dwa_ref.at[t], dwx_ref.at[t], cols(carry), cols(dxr_next), a_scr.at[t], b_scr.at[t], ext_scr.at[t],
            )

    def one_tile(ct, dy_ref, x_ref, xh_ref, g_ref, h_ref, hh_ref, cw_ref, cb_ref, wa_ref, ba_ref, wx_ref, bx_ref, lam_ref,
                 wat_ref, wxt_ref, rep_ref, dx_ref, dg_ref, dcw_ref, dcb_ref, dba_ref, dbx_ref, dlam_ref, dwa_ref, dwx_ref,
                 carry, dxr_next, a_scr, b_scr, ext_scr):
        xc = x_ref[...].astype(F32)
        xprev = jnp.where(ct > 0, xh_ref[SUBLANES:, :].astype(F32), 0.0)
        xr, (x1, x2, x3), r, i, a, om = _conv_and_gates(
            xc, xprev, cw_ref, cb_ref, wa_ref, ba_ref, wx_ref, bx_ref, lam_ref, ext_scr
        )
        inv_norm = lax.rsqrt(om)
        norm = om * inv_norm
        row = lax.broadcasted_iota(jnp.int32, xc.shape, 0)

        hv = h_ref[...]
        ge, ge_grad = _gelu_and_grad(g_ref[...].astype(F32))
        dyv = dy_ref[...].astype(F32)
        dg_ref[...] = (dyv * hv * ge_grad).astype(dg_ref.dtype)
        dh = dyv * ge

        b = dh + jnp.where(row == tc - 1, carry[0:1, :], 0.0)
        a_next = jnp.where(row < tc - 1, pltpu.roll(a, tc - 1, 0), 0.0)
        gadj = _scan_rows(a_next, b, tc, rep_ref, a_scr, b_scr, reverse=True)
        carry[...] = (a * gadj)[:SUBLANES, :]

        hprev_first = jnp.where(ct > 0, hh_ref[SUBLANES - 1 : SUBLANES, :], 0.0)
        hprev = jnp.where(row >= 1, pltpu.roll(hv, 1, 0), hprev_first)
        da = gadj * hprev
        ix = i * xr
        dnorm = gadj * ix
        di = gadj * norm * xr
        dlog_a = da * a - dnorm * (1.0 - om) * inv_norm
        sp = _softplus_neg(lam_ref[...])
        dr = dlog_a * (-LRU_C * sp)
        dsp = jnp.sum(dlog_a * (-LRU_C * r), axis=0, keepdims=True)
        dlam_ref[...] += dsp * (-_sigmoid(-lam_ref[...]))
        dza = dr * r * (1.0 - r)
        dzx = di * i * (1.0 - i)
        dzab, dzxb = dza.astype(BF16), dzx.astype(BF16)
        dxr = gadj * norm * i + _dot(dzab, wat_ref[...]) + _dot(dzxb, wxt_ref[...])
        xrb = xr.astype(BF16)
        dwa_ref[...] += _dot_tn(xrb, dzab)
        dwx_ref[...] += _dot_tn(xrb, dzxb)
        dba_ref[...] += jnp.sum(dza, axis=0, keepdims=True)
        dbx_ref[...] += jnp.sum(dzx, axis=0, keepdims=True)

        ext = jnp.concatenate([dxr, dxr_next[...]], axis=0)
        dx = (
            dxr * cw_ref[3:4, :]
            + _rows_after(ext, 1, tc) * cw_ref[2:3, :]
            + _rows_after(ext, 2, tc) * cw_ref[1:2, :]
            + _rows_after(ext, 3, tc) * cw_ref[0:1, :]
        )
        dxr_next[...] = dxr[:SUBLANES, :]
        dx_ref[...] = dx.astype(dx_ref.dtype)
        dcb_ref[...] += jnp.sum(dxr, axis=0, keepdims=True)
        dcw_ref[3:4, :] += jnp.sum(dxr * xc, axis=0, keepdims=True)
        dcw_ref[2:3, :] += jnp.sum(dxr * x1, axis=0, keepdims=True)
        dcw_ref[1:2, :] += jnp.sum(dxr * x2, axis=0, keepdims=True)
        dcw_ref[0:1, :] += jnp.sum(dxr * x3, axis=0, keepdims=True)

    wide = RNN_TILES_PER_STEP * LANES
    tile = lambda j, c: (0, j)
    mat = lambda j, c: (j, 0, 0)
    cur = lambda j, c: (nc - 1 - c, j)
    vec = pl.BlockSpec((1, wide), tile)
    matspec = pl.BlockSpec((RNN_TILES_PER_STEP, LANES, LANES), mat)
    repeat = _repeat_matrix(tc)
    return _call(
        body,
        (dy, proj, proj, proj, h, h, cw, cb, wa2, ba, wx2, bx, lam, wa2t, wx2t, repeat),
        name=name,
        grid=(N_RNN_TILES // RNN_TILES_PER_STEP, nc),
        in_specs=[
            pl.BlockSpec((tc, wide), cur),
            pl.BlockSpec((tc, wide), cur),
            pl.BlockSpec((16, wide), lambda j, c: (jnp.maximum((nc - 1 - c) * halo16 - 1, 0), j)),
            pl.BlockSpec((tc, wide), lambda j, c: (nc - 1 - c, D_RNN // wide + j)),
            pl.BlockSpec((tc, wide), cur),
            pl.BlockSpec((SUBLANES, wide), lambda j, c: (jnp.maximum((nc - 1 - c) * halo8 - 1, 0), j)),
            pl.BlockSpec((CONV_WIDTH, wide), tile),
            vec,
            matspec,
            vec,
            matspec,
            vec,
            vec,
            matspec,
            matspec,
            pl.BlockSpec(repeat.shape, lambda j, c: (0, 0)),
        ],
        out_specs=[
            pl.BlockSpec((tc, wide), cur),
            pl.BlockSpec((tc, wide), cur),
            pl.BlockSpec((CONV_WIDTH, wide), tile),
            vec,
            vec,
            vec,
            vec,
            matspec,
            matspec,
        ],
        out_shape=[
            SDS((s, D_RNN), BF16),
            SDS((s, D_RNN), BF16),
            SDS((CONV_WIDTH, D_RNN), F32),
            SDS((1, D_RNN), F32),
            SDS((1, D_RNN), F32),
            SDS((1, D_RNN), F32),
            SDS((1, D_RNN), F32),
            SDS((N_RNN_TILES, LANES, LANES), F32),
            SDS((N_RNN_TILES, LANES, LANES), F32),
        ],
        scratch_shapes=[pltpu.VMEM((SUBLANES, wide), F32)] * 2
        + [pltpu.VMEM((RNN_TILES_PER_STEP, tc, LANES), F32)] * 2
        + [pltpu.VMEM((RNN_TILES_PER_STEP, tc + SUBLANES, LANES), F32)],
        semantics=("parallel", "arbitrary"),
        comm=comm,
    )


def _sgu_specs(tb):
    half = lambda blk: pl.BlockSpec((tb, 512), lambda n: (n, blk))
    return [half(U_BLK512), half(U_BLK512 + 1), half(V_BLK512), half(V_BLK512 + 1)]


def _sgu_normed(v, lng_ref, lnb_ref):
    gv, gv_grad = _gelu_and_grad(v)
    mu = jnp.mean(gv, axis=-1, keepdims=True)
    xc = gv - mu
    rs = lax.rsqrt(jnp.mean(xc * xc, axis=-1, keepdims=True) + EPS)
    xhat = xc * rs
    return xhat * lng_ref[...] + lnb_ref[...], xhat, rs, gv_grad


def _sgu_fwd(proj, lng, lnb, wm, bias, *, tb, name, comm=None):
    s = proj.shape[0]
    tb = min(tb, s)

    def body(u0_ref, u1_ref, v0_ref, v1_ref, lng_ref, lnb_ref, wm_ref, bias_ref, y_ref):
        u = jnp.concatenate([u0_ref[...], u1_ref[...]], axis=1).astype(F32)
        v = jnp.concatenate([v0_ref[...], v1_ref[...]], axis=1).astype(F32)
        gu = _gelu(u)
        vn, _, _, _ = _sgu_normed(v, lng_ref, lnb_ref)
        vnb = vn.astype(BF16)
        for blk in range(tb // SGU_BLOCK):
            rows = slice(blk * SGU_BLOCK, (blk + 1) * SGU_BLOCK)
            for g in range(SGU_GROUPS):
                cols = slice(g * LANES, (g + 1) * LANES)
                mixed = _dot(wm_ref[g], vnb[rows, cols]) + bias_ref[g]
                y_ref[rows, cols] = (gu[rows, cols] * mixed).astype(BF16)

    const2 = lambda n: (0, 0)
    const3 = lambda n: (0, 0, 0)
    return _call(
        body,
        (proj, proj, proj, proj, lng, lnb, wm, bias),
        name=name,
        grid=(s // tb,),
        in_specs=_sgu_specs(tb)
        + [
            pl.BlockSpec((1, D_SGU), const2),
            pl.BlockSpec((1, D_SGU), const2),
            pl.BlockSpec((SGU_GROUPS, SGU_BLOCK, SGU_BLOCK), const3),
            pl.BlockSpec((SGU_GROUPS, SGU_BLOCK, LANES), const3),
        ],
        out_specs=pl.BlockSpec((tb, D_SGU), lambda n: (n, 0)),
        out_shape=SDS((s, D_SGU), BF16),
        semantics=("parallel",),
        comm=comm,
    )


def _sgu_bwd(dy, proj, lng, lnb, wm, wmt, bias, mask, *, tb, name, comm=None):
    s = proj.shape[0]
    tb = min(tb, s)
    nb = s // tb

    def body(dy_ref, u0_ref, u1_ref, v0_ref, v1_ref, lng_ref, lnb_ref, wm_ref, wmt_ref, bias_ref, mask_ref,
             du_ref, dv_ref, dws_ref, dbs_ref, dlng_ref, dlnb_ref, dvn_scr, dbs_acc):
        n = pl.program_id(0)

        @pl.when(n == 0)
        def _():
            dbs_acc[...] = jnp.zeros_like(dbs_acc)
            for ref in (dws_ref, dlng_ref, dlnb_ref):
                ref[...] = jnp.zeros_like(ref)

        u = jnp.concatenate([u0_ref[...], u1_ref[...]], axis=1).astype(F32)
        v = jnp.concatenate([v0_ref[...], v1_ref[...]], axis=1).astype(F32)
        gu, gu_grad = _gelu_and_grad(u)
        vn, xhat, rs, gv_grad = _sgu_normed(v, lng_ref, lnb_ref)
        vnb = vn.astype(BF16)
        dyv = dy_ref[...].astype(F32)
        for blk in range(tb // SGU_BLOCK):
            rows = slice(blk * SGU_BLOCK, (blk + 1) * SGU_BLOCK)
            for g in range(SGU_GROUPS):
                cols = slice(g * LANES, (g + 1) * LANES)
                vt = vnb[rows, cols]
                mixed = _dot(wm_ref[g], vt) + bias_ref[g]
                dyt = dyv[rows, cols]
                du_ref[rows, cols] = (dyt * mixed * gu_grad[rows, cols]).astype(BF16)
                dmix = dyt * gu[rows, cols]
                dmixb = dmix.astype(BF16)
                dvn_scr[rows, cols] = _dot(wmt_ref[g], dmixb)
                dws_ref[g] += _dot_nt(dmixb, vt) * mask_ref[...]
                dbs_acc[g] += dmix
        dvn = dvn_scr[...]
        dlng_ref[...] += jnp.sum(dvn * xhat, axis=0, keepdims=True)
        dlnb_ref[...] += jnp.sum(dvn, axis=0, keepdims=True)
        dxh = dvn * lng_ref[...]
        dgv = rs * (
            dxh - jnp.mean(dxh, axis=-1, keepdims=True) - xhat * jnp.mean(dxh * xhat, axis=-1, keepdims=True)
        )
        dv_ref[...] = (dgv * gv_grad).astype(BF16)

        @pl.when(n == nb - 1)
        def _():
            for g in range(SGU_GROUPS):
                dbs_ref[g] = jnp.broadcast_to(jnp.sum(dbs_acc[g], axis=-1, keepdims=True), (SGU_BLOCK, LANES))

    const2 = lambda n: (0, 0)
    const3 = lambda n: (0, 0, 0)
    gmat = pl.BlockSpec((SGU_GROUPS, SGU_BLOCK, SGU_BLOCK), const3)
    vec = pl.BlockSpec((1, D_SGU), const2)
    act = pl.BlockSpec((tb, D_SGU), lambda n: (n, 0))
    return _call(
        body,
        (dy, proj, proj, proj, proj, lng, lnb, wm, wmt, bias, mask),
        name=name,
        grid=(nb,),
        in_specs=[act] + _sgu_specs(tb) + [vec, vec, gmat, gmat, gmat, pl.BlockSpec((SGU_BLOCK, SGU_BLOCK), const2)],
        out_specs=[act, act, gmat, gmat, vec, vec],
        out_shape=[
            SDS((s, D_SGU), BF16),
            SDS((s, D_SGU), BF16),
            SDS((SGU_GROUPS, SGU_BLOCK, SGU_BLOCK), F32),
            SDS((SGU_GROUPS, SGU_BLOCK, LANES), F32),
            SDS((1, D_SGU), F32),
            SDS((1, D_SGU), F32),
        ],
        scratch_shapes=[pltpu.VMEM((tb, D_SGU), F32), pltpu.VMEM((SGU_GROUPS, SGU_BLOCK, LANES), F32)],
        semantics=("arbitrary",),
        comm=comm,
    )


def _gate_specs(tm):
    half = lambda blk: pl.BlockSpec((tm, 512), lambda i: (i, blk))
    return [half(GA_BLK512), half(GA_BLK512 + 1), half(GB_BLK512), half(GB_BLK512 + 1)]


def _merge_fwd(ya_pre, yb_pre, proj, x, w_ba, w_bb, w_out, *, tm, name, comm=None):
    s = x.shape[0]
    tm = min(tm, s)

    def body(ya_ref, yb_ref, a0, a1, b0, b1, x_ref, wa_ref, wb_ref, wo_ref, x1_ref, yao_ref, ybo_ref):
        ya = _dot(ya_ref[...], wa_ref[...])
        yb = _dot(yb_ref[...], wb_ref[...])
        sa = _sigmoid(jnp.concatenate([a0[...], a1[...]], axis=1).astype(F32))
        sb = _sigmoid(jnp.concatenate([b0[...], b1[...]], axis=1).astype(F32))
        merged = sa * ya + sb * yb
        x1_ref[...] = x_ref[...] + _dot(merged.astype(BF16), wo_ref[...])
        yao_ref[...] = ya.astype(BF16)
        ybo_ref[...] = yb.astype(BF16)

    whole = lambda r: pl.BlockSpec((r, D), lambda i: (0, 0))
    act = pl.BlockSpec((tm, D), lambda i: (i, 0))
    return _call(
        body,
        (ya_pre, yb_pre, proj, proj, proj, proj, x, w_ba, w_bb, w_out),
        name=name,
        grid=(s // tm,),
        in_specs=[pl.BlockSpec((tm, D_RNN), lambda i: (i, 0)), act] + _gate_specs(tm) + [act, whole(D_RNN), whole(D_SGU), whole(D)],
        out_specs=[act, act, act],
        out_shape=[SDS((s, D), F32), SDS((s, D), BF16), SDS((s, D), BF16)],
        semantics=("parallel",),
        comm=comm,
    )


def _merge_bwd(dx1, ya, yb, proj, w_ba, w_bb, w_out, *, tm, name, comm=None):
    s = dx1.shape[0]
    tm = min(tm, s)

    def body(dx_ref, ya_ref, yb_ref, a0, a1, b0, b1, wa_ref, wb_ref, wo_ref,
             mg_ref, dya_ref, dyb_ref, dga_ref, dgb_ref, dyap_ref, dybp_ref):
        dm = _dot_nt(dx_ref[...], wo_ref[...])
        ya = ya_ref[...].astype(F32)
        yb = yb_ref[...].astype(F32)
        sa = _sigmoid(jnp.concatenate([a0[...], a1[...]], axis=1).astype(F32))
        sb = _sigmoid(jnp.concatenate([b0[...], b1[...]], axis=1).astype(F32))
        mg_ref[...] = (sa * ya + sb * yb).astype(BF16)
        dya = (dm * sa).astype(BF16)
        dyb = (dm * sb).astype(BF16)
        dya_ref[...] = dya
        dyb_ref[...] = dyb
        dga_ref[...] = (dm * ya * sa * (1.0 - sa)).astype(BF16)
        dgb_ref[...] = (dm * yb * sb * (1.0 - sb)).astype(BF16)
        dyap_ref[...] = _dot_nt(dya, wa_ref[...]).astype(BF16)
        dybp_ref[...] = _dot_nt(dyb, wb_ref[...]).astype(BF16)

    whole = lambda r: pl.BlockSpec((r, D), lambda i: (0, 0))
    act = pl.BlockSpec((tm, D), lambda i: (i, 0))
    act_rnn = pl.BlockSpec((tm, D_RNN), lambda i: (i, 0))
    return _call(
        body,
        (dx1, ya, yb, proj, proj, proj, proj, w_ba, w_bb, w_out),
        name=name,
        grid=(s // tm,),
        in_specs=[act, act, act] + _gate_specs(tm) + [whole(D_RNN), whole(D_SGU), whole(D)],
        out_specs=[act, act, act, act, act, act_rnn, act],
        out_shape=[SDS((s, D), BF16)] * 5 + [SDS((s, D_RNN), BF16), SDS((s, D_SGU), BF16)],
        semantics=("parallel",),
        comm=comm,
    )


def _ffn_down_loss(a, w, res, g, target, *, tm, name):
    s, k = a.shape
    tm = min(tm, s)

    def body(a_ref, w_ref, r_ref, g_ref, t_ref, dx_ref, dxb_ref, dg_ref, loss_ref):
        @pl.when(pl.program_id(0) == 0)
        def _():
            dg_ref[...] = jnp.zeros_like(dg_ref)
            loss_ref[...] = jnp.zeros_like(loss_ref)

        t = jnp.maximum(a_ref[...].astype(F32), 0.0)
        xv = r_ref[...] + _dot((t * t).astype(BF16), w_ref[...])
        r = lax.rsqrt(jnp.mean(xv * xv, axis=-1, keepdims=True) + EPS)
        xhat = xv * r
        e = xhat * g_ref[...] - t_ref[...]
        loss_ref[...] += 0.5 * jnp.sum(jnp.mean(e * e, axis=-1, keepdims=True), axis=0, keepdims=True)
        dy = e * (1.0 / D)
        dxh = dy * g_ref[...]
        dx = r * (dxh - xhat * jnp.mean(dxh * xhat, axis=-1, keepdims=True))
        dx_ref[...] = dx
        dxb_ref[...] = dx.astype(BF16)
        dg_ref[...] += jnp.sum(dy * xhat, axis=0, keepdims=True)

    act = pl.BlockSpec((tm, D), lambda i: (i, 0))
    vec = pl.BlockSpec((1, D), lambda i: (0, 0))
    return pl.pallas_call(
        body,
        name=name,
        grid=(s // tm,),
        in_specs=[pl.BlockSpec((tm, k), lambda i: (i, 0)), pl.BlockSpec((k, D), lambda i: (0, 0)), act, vec, act],
        out_specs=[act, act, vec, pl.BlockSpec((SUBLANES, LANES), lambda i: (0, 0))],
        out_shape=[SDS((s, D), F32), SDS((s, D), BF16), SDS((1, D), F32), SDS((SUBLANES, LANES), F32)],
        compiler_params=_params("arbitrary"),
    )(a, w, res, g, target)


def _adamw_math(w, g, m, v):
    m2 = ADAM_B1 * m + (1.0 - ADAM_B1) * g
    v2 = ADAM_B2 * v + (1.0 - ADAM_B2) * (g * g)
    m_hat = m2 / (1.0 - ADAM_B1**ADAM_STEP)
    v_hat = v2 / (1.0 - ADAM_B2**ADAM_STEP)
    delta = -ADAM_LR * (m_hat / (jnp.sqrt(v_hat) + ADAM_EPS) + ADAM_WD * w)
    return delta, m2, v2


def _row_tile(rows, cap):
    return max(t for t in range(SUBLANES, min(cap, rows) + 1, SUBLANES) if rows % t == 0)


def _adamw_layers(w, grads, m, v, *, tr, name):
    depth, r, c = w.shape
    tr = _row_tile(r, tr)

    def body(*refs):
        g_refs = refs[:depth]
        w_ref, m_ref, v_ref, g_out, d_ref, mo_ref, vo_ref = refs[depth:]
        for l in range(depth):

            @pl.when(pl.program_id(0) == l)
            def _(l=l):
                g = g_refs[l][...]
                g_out[...] = g
                d_ref[...], mo_ref[...], vo_ref[...] = _adamw_math(w_ref[...], g, m_ref[...], v_ref[...])

    def of_layer(ll):
        return pl.BlockSpec((tr, c), lambda l, i: (jnp.where(l == ll, i, 0), 0))

    stacked = pl.BlockSpec((None, tr, c), lambda l, i: (l, i, 0))
    return pl.pallas_call(
        body,
        name=name,
        grid=(depth, r // tr),
        in_specs=[of_layer(ll) for ll in range(depth)] + [stacked] * 3,
        out_specs=[stacked] * 4,
        out_shape=[SDS((depth, r, c), F32)] * 4,
        compiler_params=_params("parallel", "parallel"),
    )(*grads, w, m, v)


def _adamw_reduced(w, parts, from_chips, m, v, chip, *, tr, name):
    depth, r, _ = w.shape
    tr = _row_tile(r, tr)

    def body(chip_ref, *refs):
        p_refs, c_refs = refs[:depth], refs[depth : 2 * depth]
        w_ref, m_ref, v_ref, g_out, d_ref, mo_ref, vo_ref = refs[2 * depth :]
        for l in range(depth):

            @pl.when(pl.program_id(0) == l)
            def _(l=l):
                got = c_refs[l]
                g = ((p_refs[l][...].astype(F32) + got[0].astype(F32)) + got[1].astype(F32)) + got[2].astype(F32)
                g_out[...] = g
                d_ref[...], mo_ref[...], vo_ref[...] = _adamw_math(w_ref[...], g, m_ref[...], v_ref[...])

    def mine_of_layer(ll):
        return pl.BlockSpec((None, tr, D), lambda l, i, chip_ref: (chip_ref[0], jnp.where(l == ll, i, 0), 0))

    def theirs_of_layer(ll):
        return pl.BlockSpec((3, tr, D), lambda l, i, chip_ref: (0, jnp.where(l == ll, i, 0), 0))

    stacked = pl.BlockSpec((None, tr, D), lambda l, i, chip_ref: (l, i, 0))
    return pl.pallas_call(
        body,
        name=name,
        grid_spec=pltpu.PrefetchScalarGridSpec(
            num_scalar_prefetch=1,
            grid=(depth, r // tr),
            in_specs=[mine_of_layer(ll) for ll in range(depth)]
            + [theirs_of_layer(ll) for ll in range(depth)]
            + [stacked] * 3,
            out_specs=[stacked] * 4,
        ),
        out_shape=[SDS((depth, r, D), F32)] * 4,
        compiler_params=_params("parallel", "parallel"),
    )(chip, *parts, *from_chips, w, m, v)


def _adamw_small(groups, *, name):
    n = len(groups)

    def body(*refs):
        ins, outs = refs[: 4 * n], refs[4 * n :]
        for i in range(n):
            w, g, m, v = (ref[...] for ref in ins[4 * i : 4 * i + 4])
            outs[3 * i][...], outs[3 * i + 1][...], outs[3 * i + 2][...] = _adamw_math(w, g, m, v)

    vmem = pl.BlockSpec(memory_space=pltpu.VMEM)
    outs = pl.pallas_call(
        body,
        name=name,
        in_specs=[vmem] * (4 * n),
        out_specs=[vmem] * (3 * n),
        out_shape=[SDS(grp[0].shape, F32) for grp in groups for _ in range(3)],
        compiler_params=pltpu.CompilerParams(vmem_limit_bytes=VMEM_LIMIT_BYTES),
    )(*[a for grp in groups for a in grp])
    return [tuple(outs[3 * i : 3 * i + 3]) for i in range(n)]


ANY = pl.BlockSpec(memory_space=pl.ANY)


def _position():
    return lax.axis_index("x"), lax.axis_index("y"), lax.axis_index("c")


def _other_chips(x, y):
    return [(1 - x, y), (x, 1 - y), (1 - x, 1 - y)]


class _Comm:
    def __init__(self, inputs, out_shapes, sem_counts, start, finish, aliases=()):
        self.inputs, self.out_shapes, self.sem_counts = list(inputs), list(out_shapes), list(sem_counts)
        self.start, self.finish = start, finish
        self.aliases = list(aliases)

    def sem_shapes(self):
        return [pltpu.SemaphoreType.DMA((n,)) for n in self.sem_counts]


def _merge_comms(comms):
    bounds, i, o, s = [], 0, 0, 0
    for cm in comms:
        bounds.append((i, i + len(cm.inputs), o, o + len(cm.out_shapes), s, s + len(cm.sem_counts)))
        i, o, s = bounds[-1][1], bounds[-1][3], bounds[-1][5]

    def phase(which):
        def run(ins, outs, sems):
            for cm, (i0, i1, o0, o1, s0, s1) in zip(comms, bounds):
                getattr(cm, which)(ins[i0:i1], outs[o0:o1], sems[s0:s1])

        return run

    return _Comm(
        [a for cm in comms for a in cm.inputs],
        [a for cm in comms for a in cm.out_shapes],
        [a for cm in comms for a in cm.sem_counts],
        phase("start"),
        phase("finish"),
        aliases=[(i0 + i, o0 + o) for cm, (i0, _, o0, _, _, _) in zip(comms, bounds) for i, o in cm.aliases],
    )


def _call(body, args, *, semantics, comm=None, **kw):
    if comm is None:
        return pl.pallas_call(body, compiler_params=_params(*semantics), **kw)(*args)
    grid, in_specs, out_specs, out_shape = kw["grid"], kw["in_specs"], kw["out_specs"], kw["out_shape"]
    scratch = list(kw.get("scratch_shapes", ()))
    single = not isinstance(out_shape, (list, tuple))
    core_specs = [out_specs] if single else list(out_specs)
    core_shapes = [out_shape] if single else list(out_shape)
    n_in, n_out, n_scr = len(in_specs), len(core_shapes), len(scratch)
    n_cin, n_cout = len(comm.inputs), len(comm.out_shapes)
    steps = 1
    for g in grid:
        steps *= g

    def hosted(*refs):
        core_in, c_in = refs[:n_in], refs[n_in : n_in + n_cin]
        o0 = n_in + n_cin
        core_out, c_out = refs[o0 : o0 + n_out], refs[o0 + n_out : o0 + n_out + n_cout]
        s0 = o0 + n_out + n_cout
        core_scr, sems = refs[s0 : s0 + n_scr], refs[s0 + n_scr :]
        step = pl.program_id(0)
        for d in range(1, len(grid)):
            step = step * grid[d] + pl.program_id(d)

        @pl.when(step == 0)
        def _():
            comm.start(c_in, c_out, sems)

        body(*core_in, *core_out, *core_scr)

        @pl.when(step == steps - 1)
        def _():
            comm.finish(c_in, c_out, sems)

    outs = pl.pallas_call(
        hosted,
        name=kw["name"],
        grid=grid,
        in_specs=list(in_specs) + [ANY] * n_cin,
        out_specs=core_specs + [ANY] * n_cout,
        out_shape=core_shapes + comm.out_shapes,
        scratch_shapes=scratch + comm.sem_shapes(),
        input_output_aliases={n_in + i: n_out + o for i, o in comm.aliases},
        compiler_params=_params(*(["arbitrary"] * len(grid))),
    )(*args, *comm.inputs)
    return (outs[0] if single else outs[:n_out]), outs[n_out:]


def _comm_only(comm, *, name):
    n_cin, n_cout = len(comm.inputs), len(comm.out_shapes)

    def body(*refs):
        ins, outs, sems = refs[:n_cin], refs[n_cin : n_cin + n_cout], refs[n_cin + n_cout :]
        comm.start(ins, outs, sems)
        comm.finish(ins, outs, sems)

    return pl.pallas_call(
        body,
        name=name,
        in_specs=[ANY] * n_cin,
        out_specs=[ANY] * n_cout,
        out_shape=comm.out_shapes,
        scratch_shapes=comm.sem_shapes(),
    )(*comm.inputs)


def _gather_comm(shards):
    n = len(shards)
    per = 7

    def plan(ins, outs, sems):
        send, recv, local = sems
        x, y, c = _position()
        me, sibling = (x, y, c), (x, y, 1 - c)
        chips = _other_chips(x, y)

        def block(t, px, py, pc):
            return outs[t].at[pl.ds(4 * px + 2 * py + pc, 1)]

        def copy(t, k, blk, to, src=None):
            return pltpu.make_async_remote_copy(
                src_ref=block(t, *blk) if src is None else src,
                dst_ref=block(t, *blk),
                send_sem=send.at[t * per + k],
                recv_sem=recv.at[t * per + k],
                device_id=to,
                device_id_type=MESH,
            )

        mine = [pltpu.make_async_copy(ins[t], block(t, *me), local.at[t]) for t in range(n)]
        to_chips = [copy(t, 1 + j, me, (*chip, c), src=ins[t]) for t in range(n) for j, chip in enumerate(chips)]
        to_sibling = [copy(t, 0, me, sibling, src=ins[t]) for t in range(n)]
        from_chips = [copy(t, 1 + j, (*chip, c), me) for t in range(n) for j, chip in enumerate(chips)]
        passed_on = [copy(t, 4 + j, (*chip, c), sibling) for t in range(n) for j, chip in enumerate(chips)]
        from_sibling = [copy(t, 0, sibling, me) for t in range(n)]
        from_sibling += [copy(t, 4 + j, (*chip, 1 - c), me) for t in range(n) for j, chip in enumerate(chips)]
        return mine, to_chips, to_sibling, from_chips, passed_on, from_sibling

    def start(ins, outs, sems):
        mine, to_chips, to_sibling, _, _, _ = plan(ins, outs, sems)
        for cp in mine + to_chips + to_sibling:
            cp.start()

    def finish(ins, outs, sems):
        mine, to_chips, to_sibling, from_chips, passed_on, from_sibling = plan(ins, outs, sems)
        for arrived, onward in zip(from_chips, passed_on):
            arrived.wait_recv()
            onward.start()
        for cp in from_sibling:
            cp.wait_recv()
        for cp in to_chips + to_sibling + passed_on:
            cp.wait_send()
        for cp in mine:
            cp.wait()

    out_shapes = [SDS((N_DEV,) + sh.shape[1:], sh.dtype) for sh in shards]
    return _Comm(shards, out_shapes, [n * per, n * per, n], start, finish)


def _gather_stage(stage, shards=None, arrived=None):
    n = len(arrived if shards is None else shards)
    targets = {"near": (0, 1), "far": (2,), "first": (0, 1, 2), "pass": (0, 1, 2)}[stage]
    to_sibling = stage in ("near", "first")
    per = len(targets) + to_sibling

    def plan(ins, outs, sems):
        x, y, c = _position()
        me, sibling = (x, y, c), (x, y, 1 - c)
        chips = [_other_chips(x, y)[j] for j in targets]

        def block(t, px, py, pc):
            return outs[t].at[pl.ds(4 * px + 2 * py + pc, 1)]

        def copy(t, k, blk, to, src=None):
            return pltpu.make_async_remote_copy(
                src_ref=block(t, *blk) if src is None else src,
                dst_ref=block(t, *blk),
                send_sem=sems[0].at[t * per + k],
                recv_sem=sems[1].at[t * per + k],
                device_id=to,
                device_id_type=MESH,
            )

        local = []
        if stage == "pass":
            sent = [copy(t, j, (*chip, c), sibling) for t in range(n) for j, chip in enumerate(chips)]
            landing = [copy(t, j, (*chip, 1 - c), me) for t in range(n) for j, chip in enumerate(chips)]
        else:
            sent = [copy(t, j, me, (*chip, c), src=ins[t]) for t in range(n) for j, chip in enumerate(chips)]
            landing = [copy(t, j, (*chip, c), me) for t in range(n) for j, chip in enumerate(chips)]
            if to_sibling:
                local = [pltpu.make_async_copy(ins[t], block(t, *me), sems[2].at[t]) for t in range(n)]
                sent += [copy(t, per - 1, me, sibling, src=ins[t]) for t in range(n)]
                landing += [copy(t, per - 1, sibling, me) for t in range(n)]
        return local, sent, landing

    def start(ins, outs, sems):
        local, sent, _ = plan(ins, outs, sems)
        for cp in local + sent:
            cp.start()

    def finish(ins, outs, sems):
        local, sent, landing = plan(ins, outs, sems)
        for cp in landing:
            cp.wait_recv()
        for cp in sent:
            cp.wait_send()
        for cp in local:
            cp.wait()

    if to_sibling:
        out_shapes = [SDS((N_DEV,) + sh.shape[1:], sh.dtype) for sh in shards]
        return _Comm(shards, out_shapes, [n * per, n * per, n], start, finish)
    out_shapes = [SDS(a.shape, a.dtype) for a in arrived]
    if stage == "pass":
        return _Comm(arrived, out_shapes, [n * per, n * per], start, finish, aliases=[(t, t) for t in range(n)])
    return _Comm(list(shards) + list(arrived), out_shapes, [n * per, n * per], start, finish, aliases=[(n + t, t) for t in range(n)])


def _exchange_comm(arrays, out_shapes, n_copies, copies_of):
    def start(ins, outs, sems):
        for cp in copies_of(ins, outs, *sems):
            cp.start()

    def finish(ins, outs, sems):
        for cp in copies_of(ins, outs, *sems):
            cp.wait()

    return _Comm(arrays, out_shapes, [n_copies, n_copies], start, finish)


def _sibling_comm(grads):
    def copies_of(ins, outs, send, recv):
        x, y, c = _position()
        return [
            pltpu.make_async_remote_copy(
                src_ref=ins[t].at[:, pl.ds(1 - c, 1)],
                dst_ref=outs[t],
                send_sem=send.at[t],
                recv_sem=recv.at[t],
                device_id=(x, y, 1 - c),
                device_id_type=MESH,
            )
            for t in range(len(ins))
        ]

    return _exchange_comm(grads, [SDS((4, 1) + g.shape[2:], g.dtype) for g in grads], len(grads), copies_of)


def _chips_comm(parts):
    def copies_of(ins, outs, send, recv):
        x, y, c = _position()
        return [
            pltpu.make_async_remote_copy(
                src_ref=ins[t].at[pl.ds(2 * px + py, 1)],
                dst_ref=outs[t].at[pl.ds(k, 1)],
                send_sem=send.at[3 * t + k],
                recv_sem=recv.at[3 * t + k],
                device_id=(px, py, c),
                device_id_type=MESH,
            )
            for t in range(len(ins))
            for k, (px, py) in enumerate(_other_chips(x, y))
        ]

    return _exchange_comm(parts, [SDS((3,) + p.shape[1:], p.dtype) for p in parts], 3 * len(parts), copies_of)


def _sum_with_sibling(grad, got, core, *, name):
    rows = grad.shape[2]

    def body(core_ref, a_ref, b_ref, o_ref):
        o_ref[...] = (a_ref[...].astype(F32) + b_ref[...].astype(F32)).astype(o_ref.dtype)

    return pl.pallas_call(
        body,
        name=name,
        grid_spec=pltpu.PrefetchScalarGridSpec(
            num_scalar_prefetch=1,
            grid=(4,),
            in_specs=[
                pl.BlockSpec((None, None, rows, D), lambda q, core_ref: (q, core_ref[0], 0, 0)),
                pl.BlockSpec((None, None, rows, D), lambda q, core_ref: (q, 0, 0, 0)),
            ],
            out_specs=pl.BlockSpec((None, rows, D), lambda q, core_ref: (q, 0, 0)),
        ),
        out_shape=SDS((4, rows, D), grad.dtype),
        compiler_params=_params("parallel"),
    )(core, grad, got)


def _sum_chips(part, got, chip, *, name):
    rows = part.shape[1]

    def body(chip_ref, a_ref, b_ref, o_ref):
        o_ref[...] = ((a_ref[...].astype(F32) + b_ref[0].astype(F32)) + b_ref[1].astype(F32)) + b_ref[2].astype(F32)

    return pl.pallas_call(
        body,
        name=name,
        grid_spec=pltpu.PrefetchScalarGridSpec(
            num_scalar_prefetch=1,
            grid=(1,),
            in_specs=[
                pl.BlockSpec((None, rows, D), lambda i, chip_ref: (chip_ref[0], 0, 0)),
                pl.BlockSpec((3, rows, D), lambda i, chip_ref: (0, 0, 0)),
            ],
            out_specs=pl.BlockSpec((rows, D), lambda i, chip_ref: (0, 0)),
        ),
        out_shape=SDS((rows, D), F32),
        compiler_params=_params("arbitrary"),
    )(chip, part, got)


def _all_reduce_small(pack, *, name):
    rows = pack.shape[1]

    def body(in_ref, out_ref, from_sibling, part, from_chips, send, recv):
        x, y, c = _position()
        me, sibling = (x, y, c), (x, y, 1 - c)
        chips = _other_chips(x, y)
        waiting = []

        def copy(k, src, dst, to):
            return pltpu.make_async_remote_copy(
                src_ref=src, dst_ref=dst, send_sem=send.at[k], recv_sem=recv.at[k], device_id=to, device_id_type=MESH
            )

        def exchange(copies):
            for cp in copies:
                cp.start()
            for cp in copies:
                cp.wait_recv()
            waiting.extend(copies)

        def block(px, py, pc):
            return out_ref.at[4 * px + 2 * py + pc]

        exchange([copy(q, in_ref.at[2 * q + 1 - c], from_sibling.at[q], sibling) for q in range(4)])
        for q in range(4):
            part[q] = in_ref[2 * q + c] + from_sibling[q]
        exchange([copy(4 + k, part.at[2 * px + py], from_chips.at[k], (px, py, c)) for k, (px, py) in enumerate(chips)])
        out_ref[4 * x + 2 * y + c] = ((part[2 * x + y] + from_chips[0]) + from_chips[1]) + from_chips[2]
        exchange(
            [copy(7, block(*me), block(*me), sibling)]
            + [copy(8 + k, block(*me), block(*me), (px, py, c)) for k, (px, py) in enumerate(chips)]
        )
        exchange([copy(11 + k, block(px, py, c), block(px, py, c), sibling) for k, (px, py) in enumerate(chips)])
        for cp in waiting:
            cp.wait_send()

    vmem = pl.BlockSpec(memory_space=pltpu.VMEM)
    return pl.pallas_call(
        body,
        name=name,
        in_specs=[vmem],
        out_specs=vmem,
        out_shape=SDS(pack.shape, F32),
        scratch_shapes=[
            pltpu.VMEM((4, rows, D), F32),
            pltpu.VMEM((4, rows, D), F32),
            pltpu.VMEM((3, rows, D), F32),
            pltpu.SemaphoreType.DMA((14,)),
            pltpu.SemaphoreType.DMA((14,)),
        ],
        compiler_params=pltpu.CompilerParams(vmem_limit_bytes=VMEM_LIMIT_BYTES),
    )(pack)


def _pack(arrays, rows):
    flat = jnp.concatenate([a.reshape(-1).astype(F32) for a in arrays])
    return jnp.pad(flat, (0, rows * D - flat.shape[0])).reshape(rows, D)


def _unpack(pack, shapes):
    flat = pack.reshape(-1)
    out, off = [], 0
    for sh in shapes:
        size = 1
        for dim in sh:
            size *= dim
        out.append(flat[off : off + size].reshape(sh))
        off += size
    return out


def _block_diag_pairs(w):
    w = w.reshape(N_RNN_TILES, 2, HEAD_DIM, HEAD_DIM)
    z = jnp.zeros_like(w[:, 0])
    top = jnp.concatenate([w[:, 0], z], axis=2)
    bot = jnp.concatenate([z, w[:, 1]], axis=2)
    return jnp.concatenate([top, bot], axis=1)


def _diag_blocks(w2):
    a = w2[:, :HEAD_DIM, :HEAD_DIM]
    b = w2[:, HEAD_DIM:, HEAD_DIM:]
    return jnp.stack([a, b], axis=1).reshape(RNN_HEADS, HEAD_DIM, HEAD_DIM)


BIG = ("w_in", "w_branch_a", "w_branch_b", "w_out", "w_up", "w_down")
TRANSPOSED = ("w_in", "w_up")
SMALL = (
    "norm_mix_g", "conv_w", "conv_b", "lru_w_a", "lru_b_a", "lru_w_x", "lru_b_x", "lru_lambda",
    "sgu_ln_g", "sgu_ln_b", "sgu_w_s", "sgu_b_s", "norm_ffn_g", "final_norm_g",
)
WEIGHTS = (
    "norm_mix_g", "w_in", "conv_w", "conv_b", "lru_w_a", "lru_b_a", "lru_w_x", "lru_b_x", "lru_lambda", "sgu_ln_g",
    "sgu_ln_b", "sgu_w_s", "sgu_b_s", "w_branch_a", "w_branch_b", "w_out", "norm_ffn_g", "w_up", "w_down", "final_norm_g",
)

TM = 512
TM_NT = 1024
TN_IN = 3328
TN_UP = 4096
TN_DOWN_BWD = 2048
TKA = 512
TKA_PIECES = 256
TC = 512
TC_BWD = 1024
TB = 256
TB_BWD = 512
TR = 256


_BRANCHES_0 = [(0, "w_branch_a"), (0, "w_branch_b"), (0, "w_out")]
_BRANCHES_1 = [(1, "w_branch_a"), (1, "w_branch_b"), (1, "w_out")]
GATHERS_RIDING = (
    {
        "in_proj": [("first", _BRANCHES_0), ("near", [(0, "w_up")])],
        "branch_a_fwd": [("pass", _BRANCHES_0), ("far", [(0, "w_up")]), ("near", [(1, "w_in")])],
        "sgu_fwd": [("pass", [(0, "w_up")]), ("near", [(0, "w_down")])],
        "merge_fwd": [("far", [(0, "w_down")])],
        "ffn_up": [("pass", [(0, "w_down")]), ("far", [(1, "w_in")])],
        "ffn_down": [("pass", [(1, "w_in")]), ("near", _BRANCHES_1)],
    },
    {
        "in_proj": [("far", _BRANCHES_1), ("near", [(1, "w_down")])],
        "branch_a_fwd": [("pass", _BRANCHES_1), ("far", [(1, "w_down")]), ("first", [(1, "w_up")])],
        "sgu_fwd": [("pass", [(1, "w_down"), (1, "w_up")])],
    },
)


def _layer_forward(l, x, p, w, shards, arriving, loss_head=None):
    def run(key, fn, *args, **kw):
        riding = GATHERS_RIDING[l].get(key, ())
        if not riding:
            return fn(*args, **kw)
        comms = []
        for stage, units in riding:
            mine = [shards[l2][n2] for l2, n2 in units] if stage != "pass" else None
            left = [arriving.pop(unit) for unit in units] if stage in ("far", "pass") else None
            comms.append(_gather_stage(stage, shards=mine, arrived=left))
        out, got = fn(*args, comm=_merge_comms(comms), **kw)
        got = list(got)
        for stage, units in riding:
            for l2, n2 in units:
                if stage == "pass":
                    w[l2][n2] = got.pop(0).reshape(-1, D)
                else:
                    arriving[l2, n2] = got.pop(0)
        return out

    proj, h = run("in_proj", _norm_matmul_nt, x, p["norm_mix_g"], w[l]["w_in"], tm=TM_NT, tn=TN_IN, name=f"in_proj_{l}")
    hseq, ya_pre = run(
        "branch_a_fwd", _branch_a_fwd, proj, p["conv_w"], p["conv_b"], p["wa2"], p["lru_b_a"], p["wx2"], p["lru_b_x"],
        p["lru_lambda"], tc=TC, name=f"branch_a_fwd_{l}",
    )
    yb_pre = run("sgu_fwd", _sgu_fwd, proj, p["sgu_ln_g"], p["sgu_ln_b"], p["wm"], p["sgu_bias"], tb=TB, name=f"sgu_fwd_{l}")
    x1, ya, yb = run(
        "merge_fwd", _merge_fwd, ya_pre, yb_pre, proj, x, w[l]["w_branch_a"], w[l]["w_branch_b"], w[l]["w_out"], tm=TM,
        name=f"merge_fwd_{l}",
    )
    f_pre, h2 = run("ffn_up", _norm_matmul_nt, x1, p["norm_ffn_g"], w[l]["w_up"], tm=TM_NT, tn=TN_UP, name=f"ffn_up_{l}")
    saved = dict(x=x, h=h, proj=proj, hseq=hseq, ya_pre=ya_pre, yb_pre=yb_pre, ya=ya, yb=yb, x1=x1, h2=h2, f_pre=f_pre)
    if loss_head is None:
        return run("ffn_down", _matmul_nn_res, f_pre, w[l]["w_down"], x1, relu2=True, tm=TM, name=f"ffn_down_{l}"), saved
    return _ffn_down_loss(f_pre, w[l]["w_down"], x1, *loss_head, tm=TM, name=f"ffn_down_loss_{l}"), saved


def _layer_backward(l, dx2, dx2b, sv, p, w, core, waiting, last):
    parts, from_chips = {}, {}

    def by_device(g):
        return g.reshape(4, 2, -1, D)

    def with_sibling(name, g, got):
        parts[name] = _sum_with_sibling(by_device(g), got, core, name=f"sum_sibling_{name}_{l}")

    df_pre = _matmul_nt_drelu2(dx2b, w["w_down"], sv["f_pre"], tm=TM_NT, tn=TN_DOWN_BWD, name=f"ffn_down_bwd_{l}")
    g_down = _matmul_tn([sv["f_pre"]], dx2b, relu2=True, tka=TKA, name=f"grad_w_down_{l}")
    g_up, (got,) = _matmul_tn(
        [df_pre], sv["h2"], relu2=False, tka=TKA, name=f"grad_w_up_{l}", comm=_sibling_comm([by_device(g_down)])
    )
    with_sibling("w_down", g_down, got)
    (dx1, dx1b, g_norm_ffn), (got,) = _matmul_nn_rmsnorm_bwd(
        [df_pre], w["w_up"], sv["x1"], p["norm_ffn_g"], dx2, tm=TM, name=f"ffn_up_bwd_{l}",
        comm=_sibling_comm([by_device(g_up)]),
    )
    with_sibling("w_up", g_up, got)
    late = [] if last else [((l, "w_up"), parts["w_up"])]
    out = _merge_bwd(
        dx1b, sv["ya"], sv["yb"], sv["proj"], w["w_branch_a"], w["w_branch_b"], w["w_out"], tm=TM, name=f"merge_bwd_{l}",
        comm=None if late else _chips_comm([parts["w_up"]]),
    )
    if not late:
        out, (from_chips[l, "w_up"],) = out
    merged, dya, dyb, dga, dgb, dya_pre, dyb_pre = out
    g_out, g_ba, g_bb = _matmuls_tn(
        [(merged, dx1b), (sv["ya_pre"], dya), (sv["yb_pre"], dyb)], ts=2 * TM, name=f"grad_w_branches_{l}"
    )
    branch = (("w_out", g_out), ("w_branch_a", g_ba), ("w_branch_b", g_bb))
    (du, dv, g_ws, g_bs, g_lng, g_lnb), got = _sgu_bwd(
        dyb_pre, sv["proj"], p["sgu_ln_g"], p["sgu_ln_b"], p["wm"], p["wmt"], p["sgu_bias"], p["mask"], tb=TB_BWD,
        name=f"sgu_bwd_{l}",
        comm=_merge_comms([_sibling_comm([by_device(g) for _, g in branch]), _chips_comm([parts["w_down"]])]),
    )
    from_chips[l, "w_down"] = got[-1]
    for (name, g), landed in zip(branch, got):
        with_sibling(name, g, landed)
    riding = [((l, name), parts[name]) for name, _ in branch] + late + list(waiting)
    (dxr, dgr, g_cw, g_cb, g_ba_, g_bx, g_lam, g_wa2, g_wx2), got = _branch_a_bwd(
        dya_pre, sv["proj"], sv["hseq"], p["conv_w"], p["conv_b"], p["wa2"], p["lru_b_a"], p["wx2"], p["lru_b_x"],
        p["lru_lambda"], p["wa2t"], p["wx2t"], tc=TC_BWD, name=f"branch_a_bwd_{l}", comm=_chips_comm([part for _, part in riding]),
    )
    for (key, _), landed in zip(riding, got):
        from_chips[key] = landed
    dproj = [dxr, dgr, du, dv, dga, dgb]
    g_in = _matmul_tn(dproj, sv["h"], relu2=False, tka=TKA_PIECES, name=f"grad_w_in_{l}")
    if last:
        (got,) = _comm_only(_sibling_comm([by_device(g_in)]), name=f"grad_w_in_to_sibling_{l}")
        with_sibling("w_in", g_in, got)
        riding = _chips_comm([parts["w_in"]])
    else:
        riding = _sibling_comm([by_device(g_in)])
    (dx, dxb, g_norm_mix), (got,) = _matmul_nn_rmsnorm_bwd(
        dproj, w["w_in"], sv["x"], p["norm_mix_g"], dx1, tm=TM, name=f"in_proj_bwd_{l}", comm=riding
    )
    if last:
        from_chips[l, "w_in"] = got
    else:
        with_sibling("w_in", g_in, got)
    small = dict(
        norm_mix_g=g_norm_mix[0], conv_w=g_cw, conv_b=g_cb[0], lru_w_a=_diag_blocks(g_wa2), lru_b_a=g_ba_.reshape(RNN_HEADS, HEAD_DIM),
        lru_w_x=_diag_blocks(g_wx2), lru_b_x=g_bx.reshape(RNN_HEADS, HEAD_DIM), lru_lambda=g_lam[0], sgu_ln_g=g_lng[0],
        sgu_ln_b=g_lnb[0], sgu_w_s=g_ws, sgu_b_s=g_bs[:, :, 0], norm_ffn_g=g_norm_ffn[0],
    )
    return dx, dxb, small, parts, from_chips


def _prepare_small(l, given):
    chunk_id = jnp.arange(SGU_BLOCK) // CHUNK
    mask = (chunk_id[:, None] >= chunk_id[None, :]).astype(F32)
    wm = given["sgu_w_s"][l] * mask
    wa2 = _block_diag_pairs(given["lru_w_a"][l])
    wx2 = _block_diag_pairs(given["lru_w_x"][l])
    row = lambda a: a.reshape(1, -1)
    return dict(
        norm_mix_g=row(given["norm_mix_g"][l]),
        norm_ffn_g=row(given["norm_ffn_g"][l]),
        conv_w=given["conv_w_full"][l],
        conv_b=row(given["conv_b"][l]),
        wa2=wa2.astype(BF16),
        wx2=wx2.astype(BF16),
        wa2t=jnp.swapaxes(wa2, 1, 2).astype(BF16),
        wx2t=jnp.swapaxes(wx2, 1, 2).astype(BF16),
        lru_b_a=row(given["lru_b_a"][l]),
        lru_b_x=row(given["lru_b_x"][l]),
        lru_lambda=row(given["lru_lambda"][l]),
        sgu_ln_g=row(given["sgu_ln_g"][l]),
        sgu_ln_b=row(given["sgu_ln_b"][l]),
        wm=wm.astype(BF16),
        wmt=jnp.swapaxes(wm, 1, 2).astype(BF16),
        sgu_bias=jnp.broadcast_to(given["sgu_b_s"][l][:, :, None], (SGU_GROUPS, SGU_BLOCK, LANES)),
        mask=mask,
    )


def _step(given):
    x_idx, y_idx, c_idx = _position()
    dev = 4 * x_idx + 2 * y_idx + c_idx
    core = c_idx.astype(jnp.int32).reshape(1)
    chip = (2 * x_idx + y_idx).astype(jnp.int32).reshape(1)

    def rows_first(name, a):
        return jnp.swapaxes(a, 1, 2) if name in TRANSPOSED else a

    shards = []
    for l in range(DEPTH):
        shards.append({name: rows_first(name, given[name])[l].astype(BF16)[None] for name in BIG})
    conv_mine = given["conv_w"].reshape(1, DEPTH * CONV_WIDTH, D_RNN // N_DEV)
    w_in_first, conv_all = _comm_only(_gather_comm([shards[0]["w_in"], conv_mine]), name="gather_first")
    weights = [{"w_in": w_in_first.reshape(-1, D)}, {}]
    conv_all = conv_all.reshape(N_DEV, DEPTH, CONV_WIDTH, D_RNN // N_DEV)
    given = dict(given, conv_w_full=jnp.moveaxis(conv_all, 0, 2).reshape(DEPTH, CONV_WIDTH, D_RNN))

    small_params = [_prepare_small(l, given) for l in range(DEPTH)]
    x = given["x"][0]
    saved, arriving = [], {}
    loss_head = (given["final_norm_g"].reshape(1, D), given["loss_target"][0])
    for l in range(DEPTH):
        x, sv = _layer_forward(
            l, x, small_params[l], weights, shards, arriving, loss_head=loss_head if l == DEPTH - 1 else None
        )
        saved.append(sv)
    dx, dxb, g_final, loss = x
    small_grads, parts, from_chips, waiting = [None] * DEPTH, [None] * DEPTH, {}, []
    for l in reversed(range(DEPTH)):
        dx, dxb, small_grads[l], parts[l], got = _layer_backward(
            l, dx, dxb, saved[l], small_params[l], weights[l], core, waiting, last=l == 0
        )
        from_chips.update(got)
        waiting = [((l, "w_in"), parts[l]["w_in"])]

    small_list = []
    for name in SMALL[:-1]:
        small_list.append(jnp.stack([small_grads[l][name] for l in range(DEPTH)]))
    small_list += [g_final[0], loss[0, :1]]
    small_shapes = [a.shape for a in small_list]
    pack = _pack(small_list, SMALL_ROWS).reshape(N_DEV, SMALL_ROWS_PER_DEV, D)
    summed = _unpack(_all_reduce_small(pack, name="all_reduce_small"), small_shapes)
    loss_total = summed[-1][0]
    grads = dict(zip(SMALL, summed[:-1]))
    cw = grads["conv_w"].reshape(DEPTH, CONV_WIDTH, N_DEV, D_RNN // N_DEV)
    grads["conv_w"] = lax.dynamic_index_in_dim(cw, dev, axis=2, keepdims=False)

    delta, new_m, new_v = {}, {}, {}
    for name in BIG:
        w, m, v = given[name], given["m_" + name], given["v_" + name]
        mine = [parts[l][name] for l in range(DEPTH)]
        theirs = [from_chips[l, name] for l in range(DEPTH)]
        if name == "w_up":
            sums = [_sum_chips(mine[l], theirs[l], chip, name=f"sum_chips_{name}_{l}").T for l in range(DEPTH)]
            out = _adamw_layers(w, sums, m, v, tr=TR, name=f"adamw_{name}")
        else:
            out = _adamw_reduced(
                rows_first(name, w), mine, theirs, rows_first(name, m), rows_first(name, v), chip, tr=TR, name=f"adamw_{name}"
            )
            out = [rows_first(name, a) for a in out]
        grads[name], delta[name], new_m[name], new_v[name] = out
    two_d = lambda a: a.reshape(1, -1) if a.ndim == 1 else a
    groups = [tuple(two_d(a) for a in (given[n], grads[n], given["m_" + n], given["v_" + n])) for n in SMALL]
    for n, (d, m2, v2) in zip(SMALL, _adamw_small(groups, name="adamw_small")):
        shape = given[n].shape
        delta[n], new_m[n], new_v[n] = d.reshape(shape), m2.reshape(shape), v2.reshape(shape)

    return (
        loss_total, dx[None],
        *[grads[n] for n in WEIGHTS], *[delta[n] for n in WEIGHTS], *[new_m[n] for n in WEIGHTS], *[new_v[n] for n in WEIGHTS],
    )


def kernel(x, norm_mix_g, w_in, conv_w, conv_b, lru_w_a, lru_b_a, lru_w_x, lru_b_x, lru_lambda, sgu_ln_g, sgu_ln_b, sgu_w_s, sgu_b_s, w_branch_a, w_branch_b, w_out, norm_ffn_g, w_up, w_down, final_norm_g, loss_target, m_norm_mix_g, m_w_in, m_conv_w, m_conv_b, m_lru_w_a, m_lru_b_a, m_lru_w_x, m_lru_b_x, m_lru_lambda, m_sgu_ln_g, m_sgu_ln_b, m_sgu_w_s, m_sgu_b_s, m_w_branch_a, m_w_branch_b, m_w_out, m_norm_ffn_g, m_w_up, m_w_down, m_final_norm_g, v_norm_mix_g, v_w_in, v_conv_w, v_conv_b, v_lru_w_a, v_lru_b_a, v_lru_w_x, v_lru_b_x, v_lru_lambda, v_sgu_ln_g, v_sgu_ln_b, v_sgu_w_s, v_sgu_b_s, v_w_branch_a, v_w_branch_b, v_w_out, v_norm_ffn_g, v_w_up, v_w_down, v_final_norm_g):
    return _step(dict(locals()))
```

```python
import jax
import jax.numpy as jnp
from jax import lax
from jax.experimental import pallas as pl
from jax.experimental.pallas import tpu as pltpu

F32 = jnp.float32
BF16 = jnp.bfloat16
SDS = jax.ShapeDtypeStruct
MESH = pl.DeviceIdType.MESH

D = 1024
D_RNN = 1280
D_SGU = 1024
D_IN = 2 * D_RNN + 2 * D_SGU + 2 * D
DEPTH = 2
RNN_HEADS = 20
HEAD_DIM = 64
CONV_WIDTH = 4
LRU_C = 8.0
SGU_GROUPS = 8
SGU_BLOCK = 128
CHUNK = 64
EPS = 1e-6
N_DEV = 8

ADAM_LR = 0.001
ADAM_B1 = 0.9
ADAM_B2 = 0.999
ADAM_EPS = 1e-08
ADAM_WD = 0.01
ADAM_STEP = 10

LANES = 128
SUBLANES = 8
VMEM_LIMIT_BYTES = 56 * 1024 * 1024

N_RNN_TILES = D_RNN // LANES
RNN_TILES_PER_STEP = 5
U_BLK512 = (2 * D_RNN) // 512
V_BLK512 = (2 * D_RNN + D_SGU) // 512
GA_BLK512 = (2 * D_RNN + 2 * D_SGU) // 512
GB_BLK512 = (2 * D_RNN + 2 * D_SGU + D) // 512

SMALL_ROWS_PER_DEV = 80
SMALL_ROWS = N_DEV * SMALL_ROWS_PER_DEV


def _params(*sem):
    return pltpu.CompilerParams(dimension_semantics=sem, vmem_limit_bytes=VMEM_LIMIT_BYTES)


def _sigmoid(x):
    return 0.5 + 0.5 * jnp.tanh(0.5 * x)


_GELU_C = 0.7978845608028654
_GELU_K = 0.044715


def _gelu(x):
    t = jnp.tanh(_GELU_C * (x + _GELU_K * x * x * x))
    return 0.5 * x * (1.0 + t)


def _gelu_and_grad(x):
    t = jnp.tanh(_GELU_C * (x + _GELU_K * x * x * x))
    val = 0.5 * x * (1.0 + t)
    grad = 0.5 * (1.0 + t) + 0.5 * x * (1.0 - t * t) * _GELU_C * (1.0 + 3.0 * _GELU_K * x * x)
    return val, grad


def _one_minus_square(log_a, a):
    return -jnp.tanh(log_a) * (1.0 + a * a)


def _dot(a, b):
    return jnp.dot(a, b, preferred_element_type=F32)


def _dot_nt(a, b):
    return lax.dot_general(a, b, (((1,), (1,)), ((), ())), preferred_element_type=F32)


def _dot_tn(a, b):
    return lax.dot_general(a, b, (((0,), (0,)), ((), ())), preferred_element_type=F32)


def _norm_matmul_nt(x, g, w, *, tm, tn, name, comm=None):
    s, n = x.shape[0], w.shape[0]
    tm, tn = min(tm, s), min(tn, n)

    def body(x_ref, g_ref, w_ref, o_ref, h_ref):
        @pl.when(pl.program_id(1) == 0)
        def _():
            xv = x_ref[...]
            r = lax.rsqrt(jnp.mean(xv * xv, axis=-1, keepdims=True) + EPS)
            h_ref[...] = (xv * r * g_ref[...]).astype(BF16)

        o_ref[...] = _dot_nt(h_ref[...], w_ref[...]).astype(o_ref.dtype)

    return _call(
        body,
        (x, g, w),
        name=name,
        grid=(s // tm, n // tn),
        in_specs=[
            pl.BlockSpec((tm, D), lambda i, j: (i, 0)),
            pl.BlockSpec((1, D), lambda i, j: (0, 0)),
            pl.BlockSpec((tn, D), lambda i, j: (j, 0)),
        ],
        out_specs=[pl.BlockSpec((tm, tn), lambda i, j: (i, j)), pl.BlockSpec((tm, D), lambda i, j: (i, 0))],
        out_shape=[SDS((s, n), BF16), SDS((s, D), BF16)],
        semantics=("parallel", "arbitrary"),
        comm=comm,
    )


def _matmul_nn_res(a, w, res, *, relu2, tm, name, comm=None):
    s, k = a.shape
    tm = min(tm, s)

    def body(a_ref, w_ref, r_ref, o_ref):
        av = a_ref[...]
        if relu2:
            t = jnp.maximum(av.astype(F32), 0.0)
            av = (t * t).astype(BF16)
        o_ref[...] = r_ref[...] + _dot(av, w_ref[...])

    return _call(
        body,
        (a, w, res),
        name=name,
        grid=(s // tm,),
        in_specs=[
            pl.BlockSpec((tm, k), lambda i: (i, 0)),
            pl.BlockSpec((k, D), lambda i: (0, 0)),
            pl.BlockSpec((tm, D), lambda i: (i, 0)),
        ],
        out_specs=pl.BlockSpec((tm, D), lambda i: (i, 0)),
        out_shape=SDS((s, D), F32),
        semantics=("parallel",),
        comm=comm,
    )


def _matmul_nt_drelu2(a, w, pre, *, tm, tn, name):
    s, n = a.shape[0], w.shape[0]
    tm, tn = min(tm, s), min(tn, n)

    def body(a_ref, w_ref, p_ref, o_ref):
        d = _dot_nt(a_ref[...], w_ref[...])
        o_ref[...] = (d * (2.0 * jnp.maximum(p_ref[...].astype(F32), 0.0))).astype(o_ref.dtype)

    return pl.pallas_call(
        body,
        name=name,
        grid=(s // tm, n // tn),
        in_specs=[
            pl.BlockSpec((tm, D), lambda i, j: (i, 0)),
            pl.BlockSpec((tn, D), lambda i, j: (j, 0)),
            pl.BlockSpec((tm, tn), lambda i, j: (i, j)),
        ],
        out_specs=pl.BlockSpec((tm, tn), lambda i, j: (i, j)),
        out_shape=SDS((s, n), BF16),
        compiler_params=_params("parallel", "arbitrary"),
    )(a, w, pre)


def _matmul_tn(a_list, b, *, relu2, tka, name, comm=None):
    s = b.shape[0]
    n = len(a_list)
    nblk = [a.shape[1] // tka for a in a_list]
    starts = [sum(nblk[:p]) for p in range(n)]

    def body(*refs):
        a_refs, b_ref, o_ref = refs[:n], refs[n], refs[n + 1]
        i = pl.program_id(0)
        for p in range(n):

            @pl.when((i >= starts[p]) & (i < starts[p] + nblk[p]))
            def _(p=p):
                av = a_refs[p][...]
                if relu2:
                    t = jnp.maximum(av.astype(F32), 0.0)
                    av = (t * t).astype(BF16)
                o_ref[...] = _dot_tn(av, b_ref[...]).astype(o_ref.dtype)

    def piece_spec(p):
        return pl.BlockSpec((s, tka), lambda i: (0, jnp.clip(i - starts[p], 0, nblk[p] - 1)))

    return _call(
        body,
        (*a_list, b),
        name=name,
        grid=(sum(nblk),),
        in_specs=[piece_spec(p) for p in range(n)] + [pl.BlockSpec((s, D), lambda i: (0, 0))],
        out_specs=pl.BlockSpec((tka, D), lambda i: (i, 0)),
        out_shape=SDS((sum(nblk) * tka, D), BF16),
        semantics=("parallel",),
        comm=comm,
    )


def _matmuls_tn(pairs, *, ts, name):
    s = pairs[0][0].shape[0]
    ts = min(ts, s)
    n = len(pairs)
    steps = s // ts

    def body(*refs):
        ins, outs, accs = refs[: 2 * n], refs[2 * n : 3 * n], refs[3 * n :]
        for p in range(n):
            part = _dot_tn(ins[2 * p][...], ins[2 * p + 1][...])

            @pl.when(pl.program_id(0) == 0)
            def _(p=p, part=part):
                accs[p][...] = part

            @pl.when(pl.program_id(0) > 0)
            def _(p=p, part=part):
                accs[p][...] += part

        @pl.when(pl.program_id(0) == steps - 1)
        def _():
            for p in range(n):
                outs[p][...] = accs[p][...].astype(BF16)

    widths = [a.shape[1] for a, _ in pairs]
    in_specs = []
    for wd in widths:
        in_specs += [pl.BlockSpec((ts, wd), lambda i: (i, 0)), pl.BlockSpec((ts, D), lambda i: (i, 0))]
    return pl.pallas_call(
        body,
        name=name,
        grid=(steps,),
        in_specs=in_specs,
        out_specs=[pl.BlockSpec((wd, D), lambda i: (0, 0)) for wd in widths],
        out_shape=[SDS((wd, D), BF16) for wd in widths],
        scratch_shapes=[pltpu.VMEM((wd, D), F32) for wd in widths],
        compiler_params=_params("arbitrary"),
    )(*[x for pair in pairs for x in pair])


def _matmul_nn_rmsnorm_bwd(a_list, w, x, g, res, *, tm, name, comm=None):
    s = x.shape[0]
    tm = min(tm, s)
    n = len(a_list)
    widths = [a.shape[1] for a in a_list]
    offs = [sum(widths[:p]) for p in range(n)]
    k = sum(widths)

    def body(*refs):
        a_refs = refs[:n]
        w_ref, x_ref, g_ref, r_ref, dx_ref, dxb_ref, dg_ref = refs[n:]

        @pl.when(pl.program_id(0) == 0)
        def _():
            dg_ref[...] = jnp.zeros_like(dg_ref)

        dh = _dot(a_refs[0][...], w_ref[0 : widths[0], :])
        for p in range(1, n):
            dh += _dot(a_refs[p][...], w_ref[offs[p] : offs[p] + widths[p], :])
        xv = x_ref[...]
        r = lax.rsqrt(jnp.mean(xv * xv, axis=-1, keepdims=True) + EPS)
        xhat = xv * r
        dxh = dh * g_ref[...]
        dx = r_ref[...] + r * (dxh - xhat * jnp.mean(dxh * xhat, axis=-1, keepdims=True))
        dx_ref[...] = dx
        dxb_ref[...] = dx.astype(BF16)
        dg_ref[...] += jnp.sum(dh * xhat, axis=0, keepdims=True)

    act = pl.BlockSpec((tm, D), lambda i: (i, 0))
    vec = pl.BlockSpec((1, D), lambda i: (0, 0))
    return _call(
        body,
        (*a_list, w, x, g, res),
        name=name,
        grid=(s // tm,),
        in_specs=[pl.BlockSpec((tm, wd), lambda i: (i, 0)) for wd in widths]
        + [pl.BlockSpec((k, D), lambda i: (0, 0), pipeline_mode=pl.Buffered(1)), act, vec, act],
        out_specs=[act, act, vec],
        out_shape=[SDS((s, D), F32), SDS((s, D), BF16), SDS((1, D), F32)],
        semantics=("arbitrary",),
        comm=comm,
    )


def _rows_after(ext, k, n):
    return pltpu.roll(ext, n + SUBLANES - k, 0)[:n, :]


def _scan_forward(a, b, n):
    row = lax.broadcasted_iota(jnp.int32, a.shape, 0)
    d = 1
    while d < n:
        if d < SUBLANES:
            m = row >= d
            a_s = jnp.where(m, pltpu.roll(a, d, 0), 1.0)
            b_s = jnp.where(m, pltpu.roll(b, d, 0), 0.0)
            b = a * b_s + b
            a = a * a_s
        else:
            b = jnp.concatenate([b[:d], a[d:] * b[: n - d] + b[d:]], axis=0)
            a = jnp.concatenate([a[:d], a[d:] * a[: n - d]], axis=0)
        d *= 2
    return a, b


def _scan_backward(a, b, n):
    row = lax.broadcasted_iota(jnp.int32, a.shape, 0)
    d = 1
    while d < n:
        if d < SUBLANES:
            m = row < n - d
            a_s = jnp.where(m, pltpu.roll(a, n - d, 0), 1.0)
            b_s = jnp.where(m, pltpu.roll(b, n - d, 0), 0.0)
            b = a * b_s + b
            a = a * a_s
        else:
            b = jnp.concatenate([a[: n - d] * b[d:] + b[: n - d], b[n - d :]], axis=0)
            a = jnp.concatenate([a[: n - d] * a[d:], a[n - d :]], axis=0)
        d *= 2
    return b


def _repeat_matrix(n):
    groups = n // SUBLANES
    return (jnp.arange(n)[:, None] // SUBLANES == jnp.arange(3 * groups)[None, :] % groups).astype(BF16)


def _scan_rows(a, b, n, repeat_ref, a_scr, b_scr, reverse):
    groups = n // SUBLANES
    a3 = a.reshape(groups, SUBLANES, LANES)
    b3 = b.reshape(groups, SUBLANES, LANES)
    sub = lax.broadcasted_iota(jnp.int32, a3.shape, 1)
    for d in (1, 2, 4):
        m = (sub < SUBLANES - d) if reverse else (sub >= d)
        shift = SUBLANES - d if reverse else d
        a_s = jnp.where(m, pltpu.roll(a3, shift, 1), 1.0)
        b_s = jnp.where(m, pltpu.roll(b3, shift, 1), 0.0)
        b3 = a3 * b_s + b3
        a3 = a3 * a_s
    a_scr[...] = a3.reshape(n, LANES)
    b_scr[...] = b3.reshape(n, LANES)
    edge = 0 if reverse else SUBLANES - 1
    a_tot = a_scr[pl.ds(edge, groups, stride=SUBLANES), :]
    b_tot = b_scr[pl.ds(edge, groups, stride=SUBLANES), :]
    row = lax.broadcasted_iota(jnp.int32, a_tot.shape, 0)
    if reverse:
        through = _scan_backward(a_tot, b_tot, groups)
        entering = jnp.where(row < groups - 1, pltpu.roll(through, groups - 1, 0), 0.0)
    else:
        _, through = _scan_forward(a_tot, b_tot, groups)
        entering = jnp.where(row >= 1, pltpu.roll(through, 1, 0), 0.0)
    hi = entering.astype(BF16)
    rest = entering - hi.astype(F32)
    mid = rest.astype(BF16)
    lo = (rest - mid.astype(F32)).astype(BF16)
    repeated = _dot(repeat_ref[...], jnp.concatenate([hi, mid, lo], axis=0))
    return b_scr[...] + a_scr[...] * repeated


def _softplus_neg(lam):
    z = -lam
    return jnp.maximum(z, 0.0) + jnp.log1p(jnp.exp(-jnp.abs(z)))


def _conv_and_gates(xc, xprev, cw_ref, cb_ref, wa_ref, ba_ref, wx_ref, bx_ref, lam_ref, ext_scr):
    n = xc.shape[0]
    ext_scr[:SUBLANES, :] = xprev
    ext_scr[SUBLANES:, :] = xc
    x1, x2, x3 = (ext_scr[pl.ds(SUBLANES - k, n), :] for k in (1, 2, 3))
    xr = cb_ref[...] + x3 * cw_ref[0:1, :] + x2 * cw_ref[1:2, :] + x1 * cw_ref[2:3, :] + xc * cw_ref[3:4, :]
    xrb = xr.astype(BF16)
    r = _sigmoid(_dot(xrb, wa_ref[...]) + ba_ref[...])
    i = _sigmoid(_dot(xrb, wx_ref[...]) + bx_ref[...])
    sp = _softplus_neg(lam_ref[...])
    log_a = (-LRU_C * r) * sp
    a = jnp.exp(log_a)
    return xr, (x1, x2, x3), r, i, a, _one_minus_square(log_a, a)


def _branch_a_fwd(proj, cw, cb, wa2, ba, wx2, bx, lam, *, tc, name, comm=None):
    s = proj.shape[0]
    tc = min(tc, s)

    def body(x_ref, g_ref, cw_ref, cb_ref, wa_ref, ba_ref, wx_ref, bx_ref, lam_ref, rep_ref, h_ref, y_ref,
             xprev, hlast, a_scr, b_scr, ext_scr):
        @pl.when(pl.program_id(1) == 0)
        def _():
            xprev[...] = jnp.zeros_like(xprev)
            hlast[...] = jnp.zeros_like(hlast)

        for t in range(RNN_TILES_PER_STEP):
            cols = lambda ref: ref.at[:, pl.ds(t * LANES, LANES)]
            one_tile(
                cols(x_ref), cols(g_ref), cols(cw_ref), cols(cb_ref), wa_ref.at[t], cols(ba_ref), wx_ref.at[t], cols(bx_ref),
                cols(lam_ref), rep_ref, cols(h_ref), cols(y_ref), cols(xprev), cols(hlast), a_scr.at[t], b_scr.at[t],
                ext_scr.at[t],
            )

    def one_tile(x_ref, g_ref, cw_ref, cb_ref, wa_ref, ba_ref, wx_ref, bx_ref, lam_ref, rep_ref, h_ref, y_ref,
                 xprev, hlast, a_scr, b_scr, ext_scr):
        xc = x_ref[...].astype(F32)
        xr, _, r, i, a, om = _conv_and_gates(
            xc, xprev[...], cw_ref, cb_ref, wa_ref, ba_ref, wx_ref, bx_ref, lam_ref, ext_scr
        )
        xprev[...] = xc[tc - SUBLANES :, :]
        u = jnp.sqrt(om) * (i * xr)
        row8 = lax.broadcasted_iota(jnp.int32, (SUBLANES, LANES), 0)
        first = u[:SUBLANES] + jnp.where(row8 == 0, a[:SUBLANES] * hlast[SUBLANES - 1 : SUBLANES, :], 0.0)
        h = _scan_rows(a, jnp.concatenate([first, u[SUBLANES:]], axis=0), tc, rep_ref, a_scr, b_scr, reverse=False)
        hlast[...] = h[tc - SUBLANES :, :]
        h_ref[...] = h
        y_ref[...] = (h * _gelu(g_ref[...].astype(F32))).astype(BF16)

    wide = RNN_TILES_PER_STEP * LANES
    tile = lambda j, c: (0, j)
    vec = pl.BlockSpec((1, wide), tile)
    mats = pl.BlockSpec((RNN_TILES_PER_STEP, LANES, LANES), lambda j, c: (j, 0, 0))
    repeat = _repeat_matrix(tc)
    return _call(
        body,
        (proj, proj, cw, cb, wa2, ba, wx2, bx, lam, repeat),
        name=name,
        grid=(N_RNN_TILES // RNN_TILES_PER_STEP, s // tc),
        in_specs=[
            pl.BlockSpec((tc, wide), lambda j, c: (c, j)),
            pl.BlockSpec((tc, wide), lambda j, c: (c, D_RNN // wide + j)),
            pl.BlockSpec((CONV_WIDTH, wide), tile),
            vec,
            mats,
            vec,
            mats,
            vec,
            vec,
            pl.BlockSpec(repeat.shape, lambda j, c: (0, 0)),
        ],
        out_specs=[pl.BlockSpec((tc, wide), lambda j, c: (c, j)), pl.BlockSpec((tc, wide), lambda j, c: (c, j))],
        out_shape=[SDS((s, D_RNN), F32), SDS((s, D_RNN), BF16)],
        scratch_shapes=[pltpu.VMEM((SUBLANES, wide), F32)] * 2
        + [pltpu.VMEM((RNN_TILES_PER_STEP, tc, LANES), F32)] * 2
        + [pltpu.VMEM((RNN_TILES_PER_STEP, tc + SUBLANES, LANES), F32)],
        semantics=("parallel", "arbitrary"),
        comm=comm,
    )


def _branch_a_bwd(dy, proj, h, cw, cb, wa2, ba, wx2, bx, lam, wa2t, wx2t, *, tc, name, comm=None):
    s = proj.shape[0]
    tc = min(tc, s)
    nc = s // tc
    halo16 = tc // 16
    halo8 = tc // SUBLANES

    def body(dy_ref, x_ref, xh_ref, g_ref, h_ref, hh_ref, cw_ref, cb_ref, wa_ref, ba_ref, wx_ref, bx_ref, lam_ref,
             wat_ref, wxt_ref, rep_ref, dx_ref, dg_ref, dcw_ref, dcb_ref, dba_ref, dbx_ref, dlam_ref, dwa_ref, dwx_ref,
             carry, dxr_next, a_scr, b_scr, ext_scr):
        cc = pl.program_id(1)
        ct = nc - 1 - cc

        @pl.when(cc == 0)
        def _():
            carry[...] = jnp.zeros_like(carry)
            dxr_next[...] = jnp.zeros_like(dxr_next)
            for ref in (dcw_ref, dcb_ref, dba_ref, dbx_ref, dlam_ref, dwa_ref, dwx_ref):
                ref[...] = jnp.zeros_like(ref)

        for t in range(RNN_TILES_PER_STEP):
            cols = lambda ref: ref.at[:, pl.ds(t * LANES, LANES)]
            one_tile(
                ct, cols(dy_ref), cols(x_ref), cols(xh_ref), cols(g_ref), cols(h_ref), cols(hh_ref), cols(cw_ref), cols(cb_ref),
                wa_ref.at[t], cols(ba_ref), wx_ref.at[t], cols(bx_ref), cols(lam_ref), wat_ref.at[t], wxt_ref.at[t], rep_ref,
                cols(dx_ref), cols(dg_ref), cols(dcw_ref), cols(dcb_ref), cols(dba_ref), cols(dbx_ref), cols(dlam_ref),
                dwa_ref.at[t], dwx_ref.at[t], cols(carry), cols(dxr_next), a_scr.at[t], b_scr.at[t], ext_scr.at[t],
            )

    def one_tile(ct, dy_ref, x_ref, xh_ref, g_ref, h_ref, hh_ref, cw_ref, cb_ref, wa_ref, ba_ref, wx_ref, bx_ref, lam_ref,
                 wat_ref, wxt_ref, rep_ref, dx_ref, dg_ref, dcw_ref, dcb_ref, dba_ref, dbx_ref, dlam_ref, dwa_ref, dwx_ref,
                 carry, dxr_next, a_scr, b_scr, ext_scr):
        xc = x_ref[...].astype(F32)
        xprev = jnp.where(ct > 0, xh_ref[SUBLANES:, :].astype(F32), 0.0)
        xr, (x1, x2, x3), r, i, a, om = _conv_and_gates(
            xc, xprev, cw_ref, cb_ref, wa_ref, ba_ref, wx_ref, bx_ref, lam_ref, ext_scr
        )
        inv_norm = lax.rsqrt(om)
        norm = om * inv_norm
        row = lax.broadcasted_iota(jnp.int32, xc.shape, 0)

        hv = h_ref[...]
        ge, ge_grad = _gelu_and_grad(g_ref[...].astype(F32))
        dyv = dy_ref[...].astype(F32)
        dg_ref[...] = (dyv * hv * ge_grad).astype(dg_ref.dtype)
        dh = dyv * ge

        b = dh + jnp.where(row == tc - 1, carry[0:1, :], 0.0)
        a_next = jnp.where(row < tc - 1, pltpu.roll(a, tc - 1, 0), 0.0)
        gadj = _scan_rows(a_next, b, tc, rep_ref, a_scr, b_scr, reverse=True)
        carry[...] = (a * gadj)[:SUBLANES, :]

        hprev_first = jnp.where(ct > 0, hh_ref[SUBLANES - 1 : SUBLANES, :], 0.0)
        hprev = jnp.where(row >= 1, pltpu.roll(hv, 1, 0), hprev_first)
        da = gadj * hprev
        ix = i * xr
        dnorm = gadj * ix
        di = gadj * norm * xr
        dlog_a = da * a - dnorm * (1.0 - om) * inv_norm
        sp = _softplus_neg(lam_ref[...])
        dr = dlog_a * (-LRU_C * sp)
        dsp = jnp.sum(dlog_a * (-LRU_C * r), axis=0, keepdims=True)
        dlam_ref[...] += dsp * (-_sigmoid(-lam_ref[...]))
        dza = dr * r * (1.0 - r)
        dzx = di * i * (1.0 - i)
        dzab, dzxb = dza.astype(BF16), dzx.astype(BF16)
        dxr = gadj * norm * i + _dot(dzab, wat_ref[...]) + _dot(dzxb, wxt_ref[...])
        xrb = xr.astype(BF16)
        dwa_ref[...] += _dot_tn(xrb, dzab)
        dwx_ref[...] += _dot_tn(xrb, dzxb)
        dba_ref[...] += jnp.sum(dza, axis=0, keepdims=True)
        dbx_ref[...] += jnp.sum(dzx, axis=0, keepdims=True)

        ext = jnp.concatenate([dxr, dxr_next[...]], axis=0)
        dx = (
            dxr * cw_ref[3:4, :]
            + _rows_after(ext, 1, tc) * cw_ref[2:3, :]
            + _rows_after(ext, 2, tc) * cw_ref[1:2, :]
            + _rows_after(ext, 3, tc) * cw_ref[0:1, :]
        )
        dxr_next[...] = dxr[:SUBLANES, :]
        dx_ref[...] = dx.astype(dx_ref.dtype)
        dcb_ref[...] += jnp.sum(dxr, axis=0, keepdims=True)
        dcw_ref[3:4, :] += jnp.sum(dxr * xc, axis=0, keepdims=True)
        dcw_ref[2:3, :] += jnp.sum(dxr * x1, axis=0, keepdims=True)
        dcw_ref[1:2, :] += jnp.sum(dxr * x2, axis=0, keepdims=True)
        dcw_ref[0:1, :] += jnp.sum(dxr * x3, axis=0, keepdims=True)

    wide = RNN_TILES_PER_STEP * LANES
    tile = lambda j, c: (0, j)
    mat = lambda j, c: (j, 0, 0)
    cur = lambda j, c: (nc - 1 - c, j)
    vec = pl.BlockSpec((1, wide), tile)
    matspec = pl.BlockSpec((RNN_TILES_PER_STEP, LANES, LANES), mat)
    repeat = _repeat_matrix(tc)
    return _call(
        body,
        (dy, proj, proj, proj, h, h, cw, cb, wa2, ba, wx2, bx, lam, wa2t, wx2t, repeat),
        name=name,
        grid=(N_RNN_TILES // RNN_TILES_PER_STEP, nc),
        in_specs=[
            pl.BlockSpec((tc, wide), cur),
            pl.BlockSpec((tc, wide), cur),
            pl.BlockSpec((16, wide), lambda j, c: (jnp.maximum((nc - 1 - c) * halo16 - 1, 0), j)),
            pl.BlockSpec((tc, wide), lambda j, c: (nc - 1 - c, D_RNN // wide + j)),
            pl.BlockSpec((tc, wide), cur),
            pl.BlockSpec((SUBLANES, wide), lambda j, c: (jnp.maximum((nc - 1 - c) * halo8 - 1, 0), j)),
            pl.BlockSpec((CONV_WIDTH, wide), tile),
            vec,
            matspec,
            vec,
            matspec,
            vec,
            vec,
            matspec,
            matspec,
            pl.BlockSpec(repeat.shape, lambda j, c: (0, 0)),
        ],
        out_specs=[
            pl.BlockSpec((tc, wide), cur),
            pl.BlockSpec((tc, wide), cur),
            pl.BlockSpec((CONV_WIDTH, wide), tile),
            vec,
            vec,
            vec,
            vec,
            matspec,
            matspec,
        ],
        out_shape=[
            SDS((s, D_RNN), BF16),
            SDS((s, D_RNN), BF16),
            SDS((CONV_WIDTH, D_RNN), F32),
            SDS((1, D_RNN), F32),
            SDS((1, D_RNN), F32),
            SDS((1, D_RNN), F32),
            SDS((1, D_RNN), F32),
            SDS((N_RNN_TILES, LANES, LANES), F32),
            SDS((N_RNN_TILES, LANES, LANES), F32),
        ],
        scratch_shapes=[pltpu.VMEM((SUBLANES, wide), F32)] * 2
        + [pltpu.VMEM((RNN_TILES_PER_STEP, tc, LANES), F32)] * 2
        + [pltpu.VMEM((RNN_TILES_PER_STEP, tc + SUBLANES, LANES), F32)],
        semantics=("parallel", "arbitrary"),
        comm=comm,
    )


def _sgu_specs(tb):
    half = lambda blk: pl.BlockSpec((tb, 512), lambda n: (n, blk))
    return [half(U_BLK512), half(U_BLK512 + 1), half(V_BLK512), half(V_BLK512 + 1)]


def _sgu_normed(v, lng_ref, lnb_ref):
    gv, gv_grad = _gelu_and_grad(v)
    mu = jnp.mean(gv, axis=-1, keepdims=True)
    xc = gv - mu
    rs = lax.rsqrt(jnp.mean(xc * xc, axis=-1, keepdims=True) + EPS)
    xhat = xc * rs
    return xhat * lng_ref[...] + lnb_ref[...], xhat, rs, gv_grad


def _sgu_fwd(proj, lng, lnb, wm, bias, *, tb, name, comm=None):
    s = proj.shape[0]
    tb = min(tb, s)

    def body(u0_ref, u1_ref, v0_ref, v1_ref, lng_ref, lnb_ref, wm_ref, bias_ref, y_ref):
        u = jnp.concatenate([u0_ref[...], u1_ref[...]], axis=1).astype(F32)
        v = jnp.concatenate([v0_ref[...], v1_ref[...]], axis=1).astype(F32)
        gu = _gelu(u)
        vn, _, _, _ = _sgu_normed(v, lng_ref, lnb_ref)
        vnb = vn.astype(BF16)
        for blk in range(tb // SGU_BLOCK):
            rows = slice(blk * SGU_BLOCK, (blk + 1) * SGU_BLOCK)
            for g in range(SGU_GROUPS):
                cols = slice(g * LANES, (g + 1) * LANES)
                mixed = _dot(wm_ref[g], vnb[rows, cols]) + bias_ref[g]
                y_ref[rows, cols] = (gu[rows, cols] * mixed).astype(BF16)

    const2 = lambda n: (0, 0)
    const3 = lambda n: (0, 0, 0)
    return _call(
        body,
        (proj, proj, proj, proj, lng, lnb, wm, bias),
        name=name,
        grid=(s // tb,),
        in_specs=_sgu_specs(tb)
        + [
            pl.BlockSpec((1, D_SGU), const2),
            pl.BlockSpec((1, D_SGU), const2),
            pl.BlockSpec((SGU_GROUPS, SGU_BLOCK, SGU_BLOCK), const3),
            pl.BlockSpec((SGU_GROUPS, SGU_BLOCK, LANES), const3),
        ],
        out_specs=pl.BlockSpec((tb, D_SGU), lambda n: (n, 0)),
        out_shape=SDS((s, D_SGU), BF16),
        semantics=("parallel",),
        comm=comm,
    )


def _sgu_bwd(dy, proj, lng, lnb, wm, wmt, bias, mask, *, tb, name, comm=None):
    s = proj.shape[0]
    tb = min(tb, s)
    nb = s // tb

    def body(dy_ref, u0_ref, u1_ref, v0_ref, v1_ref, lng_ref, lnb_ref, wm_ref, wmt_ref, bias_ref, mask_ref,
             du_ref, dv_ref, dws_ref, dbs_ref, dlng_ref, dlnb_ref, dvn_scr, dbs_acc):
        n = pl.program_id(0)

        @pl.when(n == 0)
        def _():
            dbs_acc[...] = jnp.zeros_like(dbs_acc)
            for ref in (dws_ref, dlng_ref, dlnb_ref):
                ref[...] = jnp.zeros_like(ref)

        u = jnp.concatenate([u0_ref[...], u1_ref[...]], axis=1).astype(F32)
        v = jnp.concatenate([v0_ref[...], v1_ref[...]], axis=1).astype(F32)
        gu, gu_grad = _gelu_and_grad(u)
        vn, xhat, rs, gv_grad = _sgu_normed(v, lng_ref, lnb_ref)
        vnb = vn.astype(BF16)
        dyv = dy_ref[...].astype(F32)
        for blk in range(tb // SGU_BLOCK):
            rows = slice(blk * SGU_BLOCK, (blk + 1) * SGU_BLOCK)
            for g in range(SGU_GROUPS):
                cols = slice(g * LANES, (g + 1) * LANES)
                vt = vnb[rows, cols]
                mixed = _dot(wm_ref[g], vt) + bias_ref[g]
                dyt = dyv[rows, cols]
                du_ref[rows, cols] = (dyt * mixed * gu_grad[rows, cols]).astype(BF16)
                dmix = dyt * gu[rows, cols]
                dmixb = dmix.astype(BF16)
                dvn_scr[rows, cols] = _dot(wmt_ref[g], dmixb)
                dws_ref[g] += _dot_nt(dmixb, vt) * mask_ref[...]
                dbs_acc[g] += dmix
        dvn = dvn_scr[...]
        dlng_ref[...] += jnp.sum(dvn * xhat, axis=0, keepdims=True)
        dlnb_ref[...] += jnp.sum(dvn, axis=0, keepdims=True)
        dxh = dvn * lng_ref[...]
        dgv = rs * (
            dxh - jnp.mean(dxh, axis=-1, keepdims=True) - xhat * jnp.mean(dxh * xhat, axis=-1, keepdims=True)
        )
        dv_ref[...] = (dgv * gv_grad).astype(BF16)

        @pl.when(n == nb - 1)
        def _():
            for g in range(SGU_GROUPS):
                dbs_ref[g] = jnp.broadcast_to(jnp.sum(dbs_acc[g], axis=-1, keepdims=True), (SGU_BLOCK, LANES))

    const2 = lambda n: (0, 0)
    const3 = lambda n: (0, 0, 0)
    gmat = pl.BlockSpec((SGU_GROUPS, SGU_BLOCK, SGU_BLOCK), const3)
    vec = pl.BlockSpec((1, D_SGU), const2)
    act = pl.BlockSpec((tb, D_SGU), lambda n: (n, 0))
    return _call(
        body,
        (dy, proj, proj, proj, proj, lng, lnb, wm, wmt, bias, mask),
        name=name,
        grid=(nb,),
        in_specs=[act] + _sgu_specs(tb) + [vec, vec, gmat, gmat, gmat, pl.BlockSpec((SGU_BLOCK, SGU_BLOCK), const2)],
        out_specs=[act, act, gmat, gmat, vec, vec],
        out_shape=[
            SDS((s, D_SGU), BF16),
            SDS((s, D_SGU), BF16),
            SDS((SGU_GROUPS, SGU_BLOCK, SGU_BLOCK), F32),
            SDS((SGU_GROUPS, SGU_BLOCK, LANES), F32),
            SDS((1, D_SGU), F32),
            SDS((1, D_SGU), F32),
        ],
        scratch_shapes=[pltpu.VMEM((tb, D_SGU), F32), pltpu.VMEM((SGU_GROUPS, SGU_BLOCK, LANES), F32)],
        semantics=("arbitrary",),
        comm=comm,
    )


def _gate_specs(tm):
    half = lambda blk: pl.BlockSpec((tm, 512), lambda i: (i, blk))
    return [half(GA_BLK512), half(GA_BLK512 + 1), half(GB_BLK512), half(GB_BLK512 + 1)]


def _merge_fwd(ya_pre, yb_pre, proj, x, w_ba, w_bb, w_out, *, tm, name, comm=None):
    s = x.shape[0]
    tm = min(tm, s)

    def body(ya_ref, yb_ref, a0, a1, b0, b1, x_ref, wa_ref, wb_ref, wo_ref, x1_ref, yao_ref, ybo_ref):
        ya = _dot(ya_ref[...], wa_ref[...])
        yb = _dot(yb_ref[...], wb_ref[...])
        sa = _sigmoid(jnp.concatenate([a0[...], a1[...]], axis=1).astype(F32))
        sb = _sigmoid(jnp.concatenate([b0[...], b1[...]], axis=1).astype(F32))
        merged = sa * ya + sb * yb
        x1_ref[...] = x_ref[...] + _dot(merged.astype(BF16), wo_ref[...])
        yao_ref[...] = ya.astype(BF16)
        ybo_ref[...] = yb.astype(BF16)

    whole = lambda r: pl.BlockSpec((r, D), lambda i: (0, 0))
    act = pl.BlockSpec((tm, D), lambda i: (i, 0))
    return _call(
        body,
        (ya_pre, yb_pre, proj, proj, proj, proj, x, w_ba, w_bb, w_out),
        name=name,
        grid=(s // tm,),
        in_specs=[pl.BlockSpec((tm, D_RNN), lambda i: (i, 0)), act] + _gate_specs(tm) + [act, whole(D_RNN), whole(D_SGU), whole(D)],
        out_specs=[act, act, act],
        out_shape=[SDS((s, D), F32), SDS((s, D), BF16), SDS((s, D), BF16)],
        semantics=("parallel",),
        comm=comm,
    )


def _merge_bwd(dx1, ya, yb, proj, w_ba, w_bb, w_out, *, tm, name, comm=None):
    s = dx1.shape[0]
    tm = min(tm, s)

    def body(dx_ref, ya_ref, yb_ref, a0, a1, b0, b1, wa_ref, wb_ref, wo_ref,
             mg_ref, dya_ref, dyb_ref, dga_ref, dgb_ref, dyap_ref, dybp_ref):
        dm = _dot_nt(dx_ref[...], wo_ref[...])
        ya = ya_ref[...].astype(F32)
        yb = yb_ref[...].astype(F32)
        sa = _sigmoid(jnp.concatenate([a0[...], a1[...]], axis=1).astype(F32))
        sb = _sigmoid(jnp.concatenate([b0[...], b1[...]], axis=1).astype(F32))
        mg_ref[...] = (sa * ya + sb * yb).astype(BF16)
        dya = (dm * sa).astype(BF16)
        dyb = (dm * sb).astype(BF16)
        dya_ref[...] = dya
        dyb_ref[...] = dyb
        dga_ref[...] = (dm * ya * sa * (1.0 - sa)).astype(BF16)
        dgb_ref[...] = (dm * yb * sb * (1.0 - sb)).astype(BF16)
        dyap_ref[...] = _dot_nt(dya, wa_ref[...]).astype(BF16)
        dybp_ref[...] = _dot_nt(dyb, wb_ref[...]).astype(BF16)

    whole = lambda r: pl.BlockSpec((r, D), lambda i: (0, 0))
    act = pl.BlockSpec((tm, D), lambda i: (i, 0))
    act_rnn = pl.BlockSpec((tm, D_RNN), lambda i: (i, 0))
    return _call(
        body,
        (dx1, ya, yb, proj, proj, proj, proj, w_ba, w_bb, w_out),
        name=name,
        grid=(s // tm,),
        in_specs=[act, act, act] + _gate_specs(tm) + [whole(D_RNN), whole(D_SGU), whole(D)],
        out_specs=[act, act, act, act, act, act_rnn, act],
        out_shape=[SDS((s, D), BF16)] * 5 + [SDS((s, D_RNN), BF16), SDS((s, D_SGU), BF16)],
        semantics=("parallel",),
        comm=comm,
    )


def _ffn_down_loss(a, w, res, g, target, *, tm, name):
    s, k = a.shape
    tm = min(tm, s)

    def body(a_ref, w_ref, r_ref, g_ref, t_ref, dx_ref, dxb_ref, dg_ref, loss_ref):
        @pl.when(pl.program_id(0) == 0)
        def _():
            dg_ref[...] = jnp.zeros_like(dg_ref)
            loss_ref[...] = jnp.zeros_like(loss_ref)

        t = jnp.maximum(a_ref[...].astype(F32), 0.0)
        xv = r_ref[...] + _dot((t * t).astype(BF16), w_ref[...])
        r = lax.rsqrt(jnp.mean(xv * xv, axis=-1, keepdims=True) + EPS)
        xhat = xv * r
        e = xhat * g_ref[...] - t_ref[...]
        loss_ref[...] += 0.5 * jnp.sum(jnp.mean(e * e, axis=-1, keepdims=True), axis=0, keepdims=True)
        dy = e * (1.0 / D)
        dxh = dy * g_ref[...]
        dx = r * (dxh - xhat * jnp.mean(dxh * xhat, axis=-1, keepdims=True))
        dx_ref[...] = dx
        dxb_ref[...] = dx.astype(BF16)
        dg_ref[...] += jnp.sum(dy * xhat, axis=0, keepdims=True)

    act = pl.BlockSpec((tm, D), lambda i: (i, 0))
    vec = pl.BlockSpec((1, D), lambda i: (0, 0))
    return pl.pallas_call(
        body,
        name=name,
        grid=(s // tm,),
        in_specs=[pl.BlockSpec((tm, k), lambda i: (i, 0)), pl.BlockSpec((k, D), lambda i: (0, 0)), act, vec, act],
        out_specs=[act, act, vec, pl.BlockSpec((SUBLANES, LANES), lambda i: (0, 0))],
        out_shape=[SDS((s, D), F32), SDS((s, D), BF16), SDS((1, D), F32), SDS((SUBLANES, LANES), F32)],
        compiler_params=_params("arbitrary"),
    )(a, w, res, g, target)


def _adamw_math(w, g, m, v):
    m2 = ADAM_B1 * m + (1.0 - ADAM_B1) * g
    v2 = ADAM_B2 * v + (1.0 - ADAM_B2) * (g * g)
    m_hat = m2 / (1.0 - ADAM_B1**ADAM_STEP)
    v_hat = v2 / (1.0 - ADAM_B2**ADAM_STEP)
    delta = -ADAM_LR * (m_hat / (jnp.sqrt(v_hat) + ADAM_EPS) + ADAM_WD * w)
    return delta, m2, v2


def _row_tile(rows, cap):
    return max(t for t in range(SUBLANES, min(cap, rows) + 1, SUBLANES) if rows % t == 0)


def _adamw_layers(w, grads, m, v, *, tr, name):
    depth, r, c = w.shape
    tr = _row_tile(r, tr)

    def body(*refs):
        g_refs = refs[:depth]
        w_ref, m_ref, v_ref, g_out, d_ref, mo_ref, vo_ref = refs[depth:]
        for l in range(depth):

            @pl.when(pl.program_id(0) == l)
            def _(l=l):
                g = g_refs[l][...]
                g_out[...] = g
                d_ref[...], mo_ref[...], vo_ref[...] = _adamw_math(w_ref[...], g, m_ref[...], v_ref[...])

    def of_layer(ll):
        return pl.BlockSpec((tr, c), lambda l, i: (jnp.where(l == ll, i, 0), 0))

    stacked = pl.BlockSpec((None, tr, c), lambda l, i: (l, i, 0))
    return pl.pallas_call(
        body,
        name=name,
        grid=(depth, r // tr),
        in_specs=[of_layer(ll) for ll in range(depth)] + [stacked] * 3,
        out_specs=[stacked] * 4,
        out_shape=[SDS((depth, r, c), F32)] * 4,
        compiler_params=_params("parallel", "parallel"),
    )(*grads, w, m, v)


def _adamw_reduced(w, parts, from_chips, m, v, chip, *, tr, name):
    depth, r, _ = w.shape
    tr = _row_tile(r, tr)

    def body(chip_ref, *refs):
        p_refs, c_refs = refs[:depth], refs[depth : 2 * depth]
        w_ref, m_ref, v_ref, g_out, d_ref, mo_ref, vo_ref = refs[2 * depth :]
        for l in range(depth):

            @pl.when(pl.program_id(0) == l)
            def _(l=l):
                got = c_refs[l]
                g = ((p_refs[l][...].astype(F32) + got[0].astype(F32)) + got[1].astype(F32)) + got[2].astype(F32)
                g_out[...] = g
                d_ref[...], mo_ref[...], vo_ref[...] = _adamw_math(w_ref[...], g, m_ref[...], v_ref[...])

    def mine_of_layer(ll):
        return pl.BlockSpec((None, tr, D), lambda l, i, chip_ref: (chip_ref[0], jnp.where(l == ll, i, 0), 0))

    def theirs_of_layer(ll):
        return pl.BlockSpec((3, tr, D), lambda l, i, chip_ref: (0, jnp.where(l == ll, i, 0), 0))

    stacked = pl.BlockSpec((None, tr, D), lambda l, i, chip_ref: (l, i, 0))
    return pl.pallas_call(
        body,
        name=name,
        grid_spec=pltpu.PrefetchScalarGridSpec(
            num_scalar_prefetch=1,
            grid=(depth, r // tr),
            in_specs=[mine_of_layer(ll) for ll in range(depth)]
            + [theirs_of_layer(ll) for ll in range(depth)]
            + [stacked] * 3,
            out_specs=[stacked] * 4,
        ),
        out_shape=[SDS((depth, r, D), F32)] * 4,
        compiler_params=_params("parallel", "parallel"),
    )(chip, *parts, *from_chips, w, m, v)


def _adamw_small(groups, *, name):
    n = len(groups)

    def body(*refs):
        ins, outs = refs[: 4 * n], refs[4 * n :]
        for i in range(n):
            w, g, m, v = (ref[...] for ref in ins[4 * i : 4 * i + 4])
            outs[3 * i][...], outs[3 * i + 1][...], outs[3 * i + 2][...] = _adamw_math(w, g, m, v)

    vmem = pl.BlockSpec(memory_space=pltpu.VMEM)
    outs = pl.pallas_call(
        body,
        name=name,
        in_specs=[vmem] * (4 * n),
        out_specs=[vmem] * (3 * n),
        out_shape=[SDS(grp[0].shape, F32) for grp in groups for _ in range(3)],
        compiler_params=pltpu.CompilerParams(vmem_limit_bytes=VMEM_LIMIT_BYTES),
    )(*[a for grp in groups for a in grp])
    return [tuple(outs[3 * i : 3 * i + 3]) for i in range(n)]


ANY = pl.BlockSpec(memory_space=pl.ANY)


def _position():
    return lax.axis_index("x"), lax.axis_index("y"), lax.axis_index("c")


def _other_chips(x, y):
    return [(1 - x, y), (x, 1 - y), (1 - x, 1 - y)]


class _Comm:
    def __init__(self, inputs, out_shapes, sem_counts, start, finish, aliases=()):
        self.inputs, self.out_shapes, self.sem_counts = list(inputs), list(out_shapes), list(sem_counts)
        self.start, self.finish = start, finish
        self.aliases = list(aliases)

    def sem_shapes(self):
        return [pltpu.SemaphoreType.DMA((n,)) for n in self.sem_counts]


def _merge_comms(comms):
    bounds, i, o, s = [], 0, 0, 0
    for cm in comms:
        bounds.append((i, i + len(cm.inputs), o, o + len(cm.out_shapes), s, s + len(cm.sem_counts)))
        i, o, s = bounds[-1][1], bounds[-1][3], bounds[-1][5]

    def phase(which):
        def run(ins, outs, sems):
            for cm, (i0, i1, o0, o1, s0, s1) in zip(comms, bounds):
                getattr(cm, which)(ins[i0:i1], outs[o0:o1], sems[s0:s1])

        return run

    return _Comm(
        [a for cm in comms for a in cm.inputs],
        [a for cm in comms for a in cm.out_shapes],
        [a for cm in comms for a in cm.sem_counts],
        phase("start"),
        phase("finish"),
        aliases=[(i0 + i, o0 + o) for cm, (i0, _, o0, _, _, _) in zip(comms, bounds) for i, o in cm.aliases],
    )


def _call(body, args, *, semantics, comm=None, **kw):
    if comm is None:
        return pl.pallas_call(body, compiler_params=_params(*semantics), **kw)(*args)
    grid, in_specs, out_specs, out_shape = kw["grid"], kw["in_specs"], kw["out_specs"], kw["out_shape"]
    scratch = list(kw.get("scratch_shapes", ()))
    single = not isinstance(out_shape, (list, tuple))
    core_specs = [out_specs] if single else list(out_specs)
    core_shapes = [out_shape] if single else list(out_shape)
    n_in, n_out, n_scr = len(in_specs), len(core_shapes), len(scratch)
    n_cin, n_cout = len(comm.inputs), len(comm.out_shapes)
    steps = 1
    for g in grid:
        steps *= g

    def hosted(*refs):
        core_in, c_in = refs[:n_in], refs[n_in : n_in + n_cin]
        o0 = n_in + n_cin
        core_out, c_out = refs[o0 : o0 + n_out], refs[o0 + n_out : o0 + n_out + n_cout]
        s0 = o0 + n_out + n_cout
        core_scr, sems = refs[s0 : s0 + n_scr], refs[s0 + n_scr :]
        step = pl.program_id(0)
        for d in range(1, len(grid)):
            step = step * grid[d] + pl.program_id(d)

        @pl.when(step == 0)
        def _():
            comm.start(c_in, c_out, sems)

        body(*core_in, *core_out, *core_scr)

        @pl.when(step == steps - 1)
        def _():
            comm.finish(c_in, c_out, sems)

    outs = pl.pallas_call(
        hosted,
        name=kw["name"],
        grid=grid,
        in_specs=list(in_specs) + [ANY] * n_cin,
        out_specs=core_specs + [ANY] * n_cout,
        out_shape=core_shapes + comm.out_shapes,
        scratch_shapes=scratch + comm.sem_shapes(),
        input_output_aliases={n_in + i: n_out + o for i, o in comm.aliases},
        compiler_params=_params(*(["arbitrary"] * len(grid))),
    )(*args, *comm.inputs)
    return (outs[0] if single else outs[:n_out]), outs[n_out:]


def _comm_only(comm, *, name):
    n_cin, n_cout = len(comm.inputs), len(comm.out_shapes)

    def body(*refs):
        ins, outs, sems = refs[:n_cin], refs[n_cin : n_cin + n_cout], refs[n_cin + n_cout :]
        comm.start(ins, outs, sems)
        comm.finish(ins, outs, sems)

    return pl.pallas_call(
        body,
        name=name,
        in_specs=[ANY] * n_cin,
        out_specs=[ANY] * n_cout,
        out_shape=comm.out_shapes,
        scratch_shapes=comm.sem_shapes(),
    )(*comm.inputs)


def _gather_comm(shards):
    n = len(shards)
    per = 7

    def plan(ins, outs, sems):
        send, recv, local = sems
        x, y, c = _position()
        me, sibling = (x, y, c), (x, y, 1 - c)
        chips = _other_chips(x, y)

        def block(t, px, py, pc):
            return outs[t].at[pl.ds(4 * px + 2 * py + pc, 1)]

        def copy(t, k, blk, to, src=None):
            return pltpu.make_async_remote_copy(
                src_ref=block(t, *blk) if src is None else src,
                dst_ref=block(t, *blk),
                send_sem=send.at[t * per + k],
                recv_sem=recv.at[t * per + k],
                device_id=to,
                device_id_type=MESH,
            )

        mine = [pltpu.make_async_copy(ins[t], block(t, *me), local.at[t]) for t in range(n)]
        to_chips = [copy(t, 1 + j, me, (*chip, c), src=ins[t]) for t in range(n) for j, chip in enumerate(chips)]
        to_sibling = [copy(t, 0, me, sibling, src=ins[t]) for t in range(n)]
        from_chips = [copy(t, 1 + j, (*chip, c), me) for t in range(n) for j, chip in enumerate(chips)]
        passed_on = [copy(t, 4 + j, (*chip, c), sibling) for t in range(n) for j, chip in enumerate(chips)]
        from_sibling = [copy(t, 0, sibling, me) for t in range(n)]
        from_sibling += [copy(t, 4 + j, (*chip, 1 - c), me) for t in range(n) for j, chip in enumerate(chips)]
        return mine, to_chips, to_sibling, from_chips, passed_on, from_sibling

    def start(ins, outs, sems):
        mine, to_chips, to_sibling, _, _, _ = plan(ins, outs, sems)
        for cp in mine + to_chips + to_sibling:
            cp.start()

    def finish(ins, outs, sems):
        mine, to_chips, to_sibling, from_chips, passed_on, from_sibling = plan(ins, outs, sems)
        for arrived, onward in zip(from_chips, passed_on):
            arrived.wait_recv()
            onward.start()
        for cp in from_sibling:
            cp.wait_recv()
        for cp in to_chips + to_sibling + passed_on:
            cp.wait_send()
        for cp in mine:
            cp.wait()

    out_shapes = [SDS((N_DEV,) + sh.shape[1:], sh.dtype) for sh in shards]
    return _Comm(shards, out_shapes, [n * per, n * per, n], start, finish)


def _gather_stage(stage, shards=None, arrived=None):
    n = len(arrived if shards is None else shards)
    targets = {"near": (0, 1), "far": (2,), "first": (0, 1, 2), "pass": (0, 1, 2)}[stage]
    to_sibling = stage in ("near", "first")
    per = len(targets) + to_sibling

    def plan(ins, outs, sems):
        x, y, c = _position()
        me, sibling = (x, y, c), (x, y, 1 - c)
        chips = [_other_chips(x, y)[j] for j in targets]

        def block(t, px, py, pc):
            return outs[t].at[pl.ds(4 * px + 2 * py + pc, 1)]

        def copy(t, k, blk, to, src=None):
            return pltpu.make_async_remote_copy(
                src_ref=block(t, *blk) if src is None else src,
                dst_ref=block(t, *blk),
                send_sem=sems[0].at[t * per + k],
                recv_sem=sems[1].at[t * per + k],
                device_id=to,
                device_id_type=MESH,
            )

        local = []
        if stage == "pass":
            sent = [copy(t, j, (*chip, c), sibling) for t in range(n) for j, chip in enumerate(chips)]
            landing = [copy(t, j, (*chip, 1 - c), me) for t in range(n) for j, chip in enumerate(chips)]
        else:
            sent = [copy(t, j, me, (*chip, c), src=ins[t]) for t in range(n) for j, chip in enumerate(chips)]
            landing = [copy(t, j, (*chip, c), me) for t in range(n) for j, chip in enumerate(chips)]
            if to_sibling:
                local = [pltpu.make_async_copy(ins[t], block(t, *me), sems[2].at[t]) for t in range(n)]
                sent += [copy(t, per - 1, me, sibling, src=ins[t]) for t in range(n)]
                landing += [copy(t, per - 1, sibling, me) for t in range(n)]
        return local, sent, landing

    def start(ins, outs, sems):
        local, sent, _ = plan(ins, outs, sems)
        for cp in local + sent:
            cp.start()

    def finish(ins, outs, sems):
        local, sent, landing = plan(ins, outs, sems)
        for cp in landing:
            cp.wait_recv()
        for cp in sent:
            cp.wait_send()
        for cp in local:
            cp.wait()

    if to_sibling:
        out_shapes = [SDS((N_DEV,) + sh.shape[1:], sh.dtype) for sh in shards]
        return _Comm(shards, out_shapes, [n * per, n * per, n], start, finish)
    out_shapes = [SDS(a.shape, a.dtype) for a in arrived]
    if stage == "pass":
        return _Comm(arrived, out_shapes, [n * per, n * per], start, finish, aliases=[(t, t) for t in range(n)])
    return _Comm(list(shards) + list(arrived), out_shapes, [n * per, n * per], start, finish, aliases=[(n + t, t) for t in range(n)])


def _exchange_comm(arrays, out_shapes, n_copies, copies_of):
    def start(ins, outs, sems):
        for cp in copies_of(ins, outs, *sems):
            cp.start()

    def finish(ins, outs, sems):
        for cp in copies_of(ins, outs, *sems):
            cp.wait()

    return _Comm(arrays, out_shapes, [n_copies, n_copies], start, finish)


def _sibling_comm(grads):
    def copies_of(ins, outs, send, recv):
        x, y, c = _position()
        return [
            pltpu.make_async_remote_copy(
                src_ref=ins[t].at[:, pl.ds(1 - c, 1)],
                dst_ref=outs[t],
                send_sem=send.at[t],
                recv_sem=recv.at[t],
                device_id=(x, y, 1 - c),
                device_id_type=MESH,
            )
            for t in range(len(ins))
        ]

    return _exchange_comm(grads, [SDS((4, 1) + g.shape[2:], g.dtype) for g in grads], len(grads), copies_of)


def _chips_comm(parts):
    def copies_of(ins, outs, send, recv):
        x, y, c = _position()
        return [
            pltpu.make_async_remote_copy(
                src_ref=ins[t].at[pl.ds(2 * px + py, 1)],
                dst_ref=outs[t].at[pl.ds(k, 1)],
                send_sem=send.at[3 * t + k],
                recv_sem=recv.at[3 * t + k],
                device_id=(px, py, c),
                device_id_type=MESH,
            )
            for t in range(len(ins))
            for k, (px, py) in enumerate(_other_chips(x, y))
        ]

    return _exchange_comm(parts, [SDS((3,) + p.shape[1:], p.dtype) for p in parts], 3 * len(parts), copies_of)


def _sum_with_sibling(grad, got, core, *, name):
    rows = grad.shape[2]

    def body(core_ref, a_ref, b_ref, o_ref):
        o_ref[...] = (a_ref[...].astype(F32) + b_ref[...].astype(F32)).astype(o_ref.dtype)

    return pl.pallas_call(
        body,
        name=name,
        grid_spec=pltpu.PrefetchScalarGridSpec(
            num_scalar_prefetch=1,
            grid=(4,),
            in_specs=[
                pl.BlockSpec((None, None, rows, D), lambda q, core_ref: (q, core_ref[0], 0, 0)),
                pl.BlockSpec((None, None, rows, D), lambda q, core_ref: (q, 0, 0, 0)),
            ],
            out_specs=pl.BlockSpec((None, rows, D), lambda q, core_ref: (q, 0, 0)),
        ),
        out_shape=SDS((4, rows, D), grad.dtype),
        compiler_params=_params("parallel"),
    )(core, grad, got)


def _sum_chips(part, got, chip, *, name):
    rows = part.shape[1]

    def body(chip_ref, a_ref, b_ref, o_ref):
        o_ref[...] = ((a_ref[...].astype(F32) + b_ref[0].astype(F32)) + b_ref[1].astype(F32)) + b_ref[2].astype(F32)

    return pl.pallas_call(
        body,
        name=name,
        grid_spec=pltpu.PrefetchScalarGridSpec(
            num_scalar_prefetch=1,
            grid=(1,),
            in_specs=[
                pl.BlockSpec((None, rows, D), lambda i, chip_ref: (chip_ref[0], 0, 0)),
                pl.BlockSpec((3, rows, D), lambda i, chip_ref: (0, 0, 0)),
            ],
            out_specs=pl.BlockSpec((rows, D), lambda i, chip_ref: (0, 0)),
        ),
        out_shape=SDS((rows, D), F32),
        compiler_params=_params("arbitrary"),
    )(chip, part, got)


def _all_reduce_small(pack, *, name):
    rows = pack.shape[1]

    def body(in_ref, out_ref, from_sibling, part, from_chips, send, recv):
        x, y, c = _position()
        me, sibling = (x, y, c), (x, y, 1 - c)
        chips = _other_chips(x, y)
        waiting = []

        def copy(k, src, dst, to):
            return pltpu.make_async_remote_copy(
                src_ref=src, dst_ref=dst, send_sem=send.at[k], recv_sem=recv.at[k], device_id=to, device_id_type=MESH
            )

        def exchange(copies):
            for cp in copies:
                cp.start()
            for cp in copies:
                cp.wait_recv()
            waiting.extend(copies)

        def block(px, py, pc):
            return out_ref.at[4 * px + 2 * py + pc]

        exchange([copy(q, in_ref.at[2 * q + 1 - c], from_sibling.at[q], sibling) for q in range(4)])
        for q in range(4):
            part[q] = in_ref[2 * q + c] + from_sibling[q]
        exchange([copy(4 + k, part.at[2 * px + py], from_chips.at[k], (px, py, c)) for k, (px, py) in enumerate(chips)])
        out_ref[4 * x + 2 * y + c] = ((part[2 * x + y] + from_chips[0]) + from_chips[1]) + from_chips[2]
        exchange(
            [copy(7, block(*me), block(*me), sibling)]
            + [copy(8 + k, block(*me), block(*me), (px, py, c)) for k, (px, py) in enumerate(chips)]
        )
        exchange([copy(11 + k, block(px, py, c), block(px, py, c), sibling) for k, (px, py) in enumerate(chips)])
        for cp in waiting:
            cp.wait_send()

    vmem = pl.BlockSpec(memory_space=pltpu.VMEM)
    return pl.pallas_call(
        body,
        name=name,
        in_specs=[vmem],
        out_specs=vmem,
        out_shape=SDS(pack.shape, F32),
        scratch_shapes=[
            pltpu.VMEM((4, rows, D), F32),
            pltpu.VMEM((4, rows, D), F32),
            pltpu.VMEM((3, rows, D), F32),
            pltpu.SemaphoreType.DMA((14,)),
            pltpu.SemaphoreType.DMA((14,)),
        ],
        compiler_params=pltpu.CompilerParams(vmem_limit_bytes=VMEM_LIMIT_BYTES),
    )(pack)


def _pack(arrays, rows):
    flat = jnp.concatenate([a.reshape(-1).astype(F32) for a in arrays])
    return jnp.pad(flat, (0, rows * D - flat.shape[0])).reshape(rows, D)


def _unpack(pack, shapes):
    flat = pack.reshape(-1)
    out, off = [], 0
    for sh in shapes:
        size = 1
        for dim in sh:
            size *= dim
        out.append(flat[off : off + size].reshape(sh))
        off += size
    return out


def _block_diag_pairs(w):
    w = w.reshape(N_RNN_TILES, 2, HEAD_DIM, HEAD_DIM)
    z = jnp.zeros_like(w[:, 0])
    top = jnp.concatenate([w[:, 0], z], axis=2)
    bot = jnp.concatenate([z, w[:, 1]], axis=2)
    return jnp.concatenate([top, bot], axis=1)


def _diag_blocks(w2):
    a = w2[:, :HEAD_DIM, :HEAD_DIM]
    b = w2[:, HEAD_DIM:, HEAD_DIM:]
    return jnp.stack([a, b], axis=1).reshape(RNN_HEADS, HEAD_DIM, HEAD_DIM)


BIG = ("w_in", "w_branch_a", "w_branch_b", "w_out", "w_up", "w_down")
TRANSPOSED = ("w_in", "w_up")
SMALL = (
    "norm_mix_g", "conv_w", "conv_b", "lru_w_a", "lru_b_a", "lru_w_x", "lru_b_x", "lru_lambda",
    "sgu_ln_g", "sgu_ln_b", "sgu_w_s", "sgu_b_s", "norm_ffn_g", "final_norm_g",
)
WEIGHTS = (
    "norm_mix_g", "w_in", "conv_w", "conv_b", "lru_w_a", "lru_b_a", "lru_w_x", "lru_b_x", "lru_lambda", "sgu_ln_g",
    "sgu_ln_b", "sgu_w_s", "sgu_b_s", "w_branch_a", "w_branch_b", "w_out", "norm_ffn_g", "w_up", "w_down", "final_norm_g",
)

TM = 512
TM_NT = 1024
TN_IN = 3328
TN_UP = 4096
TN_DOWN_BWD = 2048
TKA = 512
TKA_PIECES = 256
TC = 512
TC_BWD = 1024
TB = 256
TB_BWD = 512
TR = 256


_BRANCHES_0 = [(0, "w_branch_a"), (0, "w_branch_b"), (0, "w_out")]
_BRANCHES_1 = [(1, "w_branch_a"), (1, "w_branch_b"), (1, "w_out")]
GATHERS_RIDING = (
    {
        "in_proj": [("first", _BRANCHES_0), ("near", [(0, "w_up")])],
        "branch_a_fwd": [("pass", _BRANCHES_0), ("far", [(0, "w_up")]), ("near", [(1, "w_in")])],
        "sgu_fwd": [("pass", [(0, "w_up")]), ("near", [(0, "w_down")])],
        "merge_fwd": [("far", [(0, "w_down")])],
        "ffn_up": [("pass", [(0, "w_down")]), ("far", [(1, "w_in")])],
        "ffn_down": [("pass", [(1, "w_in")]), ("near", _BRANCHES_1)],
    },
    {
        "in_proj": [("far", _BRANCHES_1), ("near", [(1, "w_down")])],
        "branch_a_fwd": [("pass", _BRANCHES_1), ("far", [(1, "w_down")]), ("first", [(1, "w_up")])],
        "sgu_fwd": [("pass", [(1, "w_down"), (1, "w_up")])],
    },
)


def _layer_forward(l, x, p, w, shards, arriving, loss_head=None):
    def run(key, fn, *args, **kw):
        riding = GATHERS_RIDING[l].get(key, ())
        if not riding:
            return fn(*args, **kw)
        comms = []
        for stage, units in riding:
            mine = [shards[l2][n2] for l2, n2 in units] if stage != "pass" else None
            left = [arriving.pop(unit) for unit in units] if stage in ("far", "pass") else None
            comms.append(_gather_stage(stage, shards=mine, arrived=left))
        out, got = fn(*args, comm=_merge_comms(comms), **kw)
        got = list(got)
        for stage, units in riding:
            for l2, n2 in units:
                if stage == "pass":
                    w[l2][n2] = got.pop(0).reshape(-1, D)
                else:
                    arriving[l2, n2] = got.pop(0)
        return out

    proj, h = run("in_proj", _norm_matmul_nt, x, p["norm_mix_g"], w[l]["w_in"], tm=TM_NT, tn=TN_IN, name=f"in_proj_{l}")
    hseq, ya_pre = run(
        "branch_a_fwd", _branch_a_fwd, proj, p["conv_w"], p["conv_b"], p["wa2"], p["lru_b_a"], p["wx2"], p["lru_b_x"],
        p["lru_lambda"], tc=TC, name=f"branch_a_fwd_{l}",
    )
    yb_pre = run("sgu_fwd", _sgu_fwd, proj, p["sgu_ln_g"], p["sgu_ln_b"], p["wm"], p["sgu_bias"], tb=TB, name=f"sgu_fwd_{l}")
    x1, ya, yb = run(
        "merge_fwd", _merge_fwd, ya_pre, yb_pre, proj, x, w[l]["w_branch_a"], w[l]["w_branch_b"], w[l]["w_out"], tm=TM,
        name=f"merge_fwd_{l}",
    )
    f_pre, h2 = run("ffn_up", _norm_matmul_nt, x1, p["norm_ffn_g"], w[l]["w_up"], tm=TM_NT, tn=TN_UP, name=f"ffn_up_{l}")
    saved = dict(x=x, h=h, proj=proj, hseq=hseq, ya_pre=ya_pre, yb_pre=yb_pre, ya=ya, yb=yb, x1=x1, h2=h2, f_pre=f_pre)
    if loss_head is None:
        return run("ffn_down", _matmul_nn_res, f_pre, w[l]["w_down"], x1, relu2=True, tm=TM, name=f"ffn_down_{l}"), saved
    return _ffn_down_loss(f_pre, w[l]["w_down"], x1, *loss_head, tm=TM, name=f"ffn_down_loss_{l}"), saved


def _layer_backward(l, dx2, dx2b, sv, p, w, core, waiting, last):
    parts, from_chips = {}, {}

    def by_device(g):
        return g.reshape(4, 2, -1, D)

    def with_sibling(name, g, got):
        parts[name] = _sum_with_sibling(by_device(g), got, core, name=f"sum_sibling_{name}_{l}")

    df_pre = _matmul_nt_drelu2(dx2b, w["w_down"], sv["f_pre"], tm=TM_NT, tn=TN_DOWN_BWD, name=f"ffn_down_bwd_{l}")
    g_down = _matmul_tn([sv["f_pre"]], dx2b, relu2=True, tka=TKA, name=f"grad_w_down_{l}")
    g_up, (got,) = _matmul_tn(
        [df_pre], sv["h2"], relu2=False, tka=TKA, name=f"grad_w_up_{l}", comm=_sibling_comm([by_device(g_down)])
    )
    with_sibling("w_down", g_down, got)
    (dx1, dx1b, g_norm_ffn), (got,) = _matmul_nn_rmsnorm_bwd(
        [df_pre], w["w_up"], sv["x1"], p["norm_ffn_g"], dx2, tm=TM, name=f"ffn_up_bwd_{l}",
        comm=_sibling_comm([by_device(g_up)]),
    )
    with_sibling("w_up", g_up, got)
    late = [] if last else [((l, "w_up"), parts["w_up"])]
    merged, dya, dyb, dga, dgb, dya_pre, dyb_pre = _merge_bwd(
        dx1b, sv["ya"], sv["yb"], sv["proj"], w["w_branch_a"], w["w_branch_b"], w["w_out"], tm=TM, name=f"merge_bwd_{l}"
    )
    g_out, g_ba, g_bb = _matmuls_tn(
        [(merged, dx1b), (sv["ya_pre"], dya), (sv["yb_pre"], dyb)], ts=2 * TM, name=f"grad_w_branches_{l}"
    )
    branch = (("w_out", g_out), ("w_branch_a", g_ba), ("w_branch_b", g_bb))
    (du, dv, g_ws, g_bs, g_lng, g_lnb), got = _sgu_bwd(
        dyb_pre, sv["proj"], p["sgu_ln_g"], p["sgu_ln_b"], p["wm"], p["wmt"], p["sgu_bias"], p["mask"], tb=TB_BWD,
        name=f"sgu_bwd_{l}",
        comm=_merge_comms([_sibling_comm([by_device(g) for _, g in branch]), _chips_comm([parts["w_down"]])]),
    )
    from_chips[l, "w_down"] = got[-1]
    for (name, g), landed in zip(branch, got):
        with_sibling(name, g, landed)
    riding = [((l, name), parts[name]) for name, _ in branch] + late + list(waiting)
    (dxr, dgr, g_cw, g_cb, g_ba_, g_bx, g_lam, g_wa2, g_wx2), got = _branch_a_bwd(
        dya_pre, sv["proj"], sv["hseq"], p["conv_w"], p["conv_b"], p["wa2"], p["lru_b_a"], p["wx2"], p["lru_b_x"],
        p["lru_lambda"], p["wa2t"], p["wx2t"], tc=TC_BWD, name=f"branch_a_bwd_{l}", comm=_chips_comm([part for _, part in riding]),
    )
    for (key, _), landed in zip(riding, got):
        from_chips[key] = landed
    dproj = [dxr, dgr, du, dv, dga, dgb]
    g_in = _matmul_tn(
        dproj, sv["h"], relu2=False, tka=TKA_PIECES, name=f"grad_w_in_{l}", comm=_chips_comm([parts["w_up"]]) if last else None
    )
    if last:
        g_in, (from_chips[l, "w_up"],) = g_in
    if last:
        (got,) = _comm_only(_sibling_comm([by_device(g_in)]), name=f"grad_w_in_to_sibling_{l}")
        with_sibling("w_in", g_in, got)
        riding = _chips_comm([parts["w_in"]])
    else:
        riding = _sibling_comm([by_device(g_in)])
    (dx, dxb, g_norm_mix), (got,) = _matmul_nn_rmsnorm_bwd(
        dproj, w["w_in"], sv["x"], p["norm_mix_g"], dx1, tm=TM, name=f"in_proj_bwd_{l}", comm=riding
    )
    if last:
        from_chips[l, "w_in"] = got
    else:
        with_sibling("w_in", g_in, got)
    small = dict(
        norm_mix_g=g_norm_mix[0], conv_w=g_cw, conv_b=g_cb[0], lru_w_a=_diag_blocks(g_wa2), lru_b_a=g_ba_.reshape(RNN_HEADS, HEAD_DIM),
        lru_w_x=_diag_blocks(g_wx2), lru_b_x=g_bx.reshape(RNN_HEADS, HEAD_DIM), lru_lambda=g_lam[0], sgu_ln_g=g_lng[0],
        sgu_ln_b=g_lnb[0], sgu_w_s=g_ws, sgu_b_s=g_bs[:, :, 0], norm_ffn_g=g_norm_ffn[0],
    )
    return dx, dxb, small, parts, from_chips


def _prepare_small(l, given):
    chunk_id = jnp.arange(SGU_BLOCK) // CHUNK
    mask = (chunk_id[:, None] >= chunk_id[None, :]).astype(F32)
    wm = given["sgu_w_s"][l] * mask
    wa2 = _block_diag_pairs(given["lru_w_a"][l])
    wx2 = _block_diag_pairs(given["lru_w_x"][l])
    row = lambda a: a.reshape(1, -1)
    return dict(
        norm_mix_g=row(given["norm_mix_g"][l]),
        norm_ffn_g=row(given["norm_ffn_g"][l]),
        conv_w=given["conv_w_full"][l],
        conv_b=row(given["conv_b"][l]),
        wa2=wa2.astype(BF16),
        wx2=wx2.astype(BF16),
        wa2t=jnp.swapaxes(wa2, 1, 2).astype(BF16),
        wx2t=jnp.swapaxes(wx2, 1, 2).astype(BF16),
        lru_b_a=row(given["lru_b_a"][l]),
        lru_b_x=row(given["lru_b_x"][l]),
        lru_lambda=row(given["lru_lambda"][l]),
        sgu_ln_g=row(given["sgu_ln_g"][l]),
        sgu_ln_b=row(given["sgu_ln_b"][l]),
        wm=wm.astype(BF16),
        wmt=jnp.swapaxes(wm, 1, 2).astype(BF16),
        sgu_bias=jnp.broadcast_to(given["sgu_b_s"][l][:, :, None], (SGU_GROUPS, SGU_BLOCK, LANES)),
        mask=mask,
    )


def _step(given):
    x_idx, y_idx, c_idx = _position()
    dev = 4 * x_idx + 2 * y_idx + c_idx
    core = c_idx.astype(jnp.int32).reshape(1)
    chip = (2 * x_idx + y_idx).astype(jnp.int32).reshape(1)

    def rows_first(name, a):
        return jnp.swapaxes(a, 1, 2) if name in TRANSPOSED else a

    shards = []
    for l in range(DEPTH):
        shards.append({name: rows_first(name, given[name])[l].astype(BF16)[None] for name in BIG})
    conv_mine = given["conv_w"].reshape(1, DEPTH * CONV_WIDTH, D_RNN // N_DEV)
    w_in_first, conv_all = _comm_only(_gather_comm([shards[0]["w_in"], conv_mine]), name="gather_first")
    weights = [{"w_in": w_in_first.reshape(-1, D)}, {}]
    conv_all = conv_all.reshape(N_DEV, DEPTH, CONV_WIDTH, D_RNN // N_DEV)
    given = dict(given, conv_w_full=jnp.moveaxis(conv_all, 0, 2).reshape(DEPTH, CONV_WIDTH, D_RNN))

    small_params = [_prepare_small(l, given) for l in range(DEPTH)]
    x = given["x"][0]
    saved, arriving = [], {}
    loss_head = (given["final_norm_g"].reshape(1, D), given["loss_target"][0])
    for l in range(DEPTH):
        x, sv = _layer_forward(
            l, x, small_params[l], weights, shards, arriving, loss_head=loss_head if l == DEPTH - 1 else None
        )
        saved.append(sv)
    dx, dxb, g_final, loss = x
    small_grads, parts, from_chips, waiting = [None] * DEPTH, [None] * DEPTH, {}, []
    for l in reversed(range(DEPTH)):
        dx, dxb, small_grads[l], parts[l], got = _layer_backward(
            l, dx, dxb, saved[l], small_params[l], weights[l], core, waiting, last=l == 0
        )
        from_chips.update(got)
        waiting = [((l, "w_in"), parts[l]["w_in"])]

    small_list = []
    for name in SMALL[:-1]:
        small_list.append(jnp.stack([small_grads[l][name] for l in range(DEPTH)]))
    small_list += [g_final[0], loss[0, :1]]
    small_shapes = [a.shape for a in small_list]
    pack = _pack(small_list, SMALL_ROWS).reshape(N_DEV, SMALL_ROWS_PER_DEV, D)
    summed = _unpack(_all_reduce_small(pack, name="all_reduce_small"), small_shapes)
    loss_total = summed[-1][0]
    grads = dict(zip(SMALL, summed[:-1]))
    cw = grads["conv_w"].reshape(DEPTH, CONV_WIDTH, N_DEV, D_RNN // N_DEV)
    grads["conv_w"] = lax.dynamic_index_in_dim(cw, dev, axis=2, keepdims=False)

    delta, new_m, new_v = {}, {}, {}
    for name in BIG:
        w, m, v = given[name], given["m_" + name], given["v_" + name]
        mine = [parts[l][name] for l in range(DEPTH)]
        theirs = [from_chips[l, name] for l in range(DEPTH)]
        if name == "w_up":
            sums = [_sum_chips(mine[l], theirs[l], chip, name=f"sum_chips_{name}_{l}").T for l in range(DEPTH)]
            out = _adamw_layers(w, sums, m, v, tr=TR, name=f"adamw_{name}")
        else:
            out = _adamw_reduced(
                rows_first(name, w), mine, theirs, rows_first(name, m), rows_first(name, v), chip, tr=TR, name=f"adamw_{name}"
            )
            out = [rows_first(name, a) for a in out]
        grads[name], delta[name], new_m[name], new_v[name] = out
    two_d = lambda a: a.reshape(1, -1) if a.ndim == 1 else a
    groups = [tuple(two_d(a) for a in (given[n], grads[n], given["m_" + n], given["v_" + n])) for n in SMALL]
    for n, (d, m2, v2) in zip(SMALL, _adamw_small(groups, name="adamw_small")):
        shape = given[n].shape
        delta[n], new_m[n], new_v[n] = d.reshape(shape), m2.reshape(shape), v2.reshape(shape)

    return (
        loss_total, dx[None],
        *[grads[n] for n in WEIGHTS], *[delta[n] for n in WEIGHTS], *[new_m[n] for n in WEIGHTS], *[new_v[n] for n in WEIGHTS],
    )


def kernel(x, norm_mix_g, w_in, conv_w, conv_b, lru_w_a, lru_b_a, lru_w_x, lru_b_x, lru_lambda, sgu_ln_g, sgu_ln_b, sgu_w_s, sgu_b_s, w_branch_a, w_branch_b, w_out, norm_ffn_g, w_up, w_down, final_norm_g, loss_target, m_norm_mix_g, m_w_in, m_conv_w, m_conv_b, m_lru_w_a, m_lru_b_a, m_lru_w_x, m_lru_b_x, m_lru_lambda, m_sgu_ln_g, m_sgu_ln_b, m_sgu_w_s, m_sgu_b_s, m_w_branch_a, m_w_branch_b, m_w_out, m_norm_ffn_g, m_w_up, m_w_down, m_final_norm_g, v_norm_mix_g, v_w_in, v_conv_w, v_conv_b, v_lru_w_a, v_lru_b_a, v_lru_w_x, v_lru_b_x, v_lru_lambda, v_sgu_ln_g, v_sgu_ln_b, v_sgu_w_s, v_sgu_b_s, v_w_branch_a, v_w_branch_b, v_w_out, v_norm_ffn_g, v_w_up, v_w_down, v_final_norm_g):
    return _step(dict(locals()))
```
